```python
import jax, jax.numpy as jnp
from jax import lax
import numpy as np

D_MODEL = 1024
BATCH = 8
SEQ = 2048
DEPTH = 2
DEC_BATCH = 128
DEC_SEQ = 1
PAST_LEN = 16384
PAGE_SIZE = 128

HEAD_A = 64
H_A = 8
D_A = H_A * HEAD_A
R_W = 32
R_A = 32
CHUNK = 128
G_B = 8
C_B = 64
D_B = G_B * C_B
N_SHIFT = 3 * D_A + R_W + R_A
N_IN = N_SHIFT + 2 * D_B + 2 * D_MODEL
N_GROUPS = 4
EXP_PER_GROUP = 8
N_EXPERTS = N_GROUPS * EXP_PER_GROUP
TOP_K = 2
D_EXPERT = 256
ALPHA = (2 * DEPTH) ** 0.25
BETA = (8 * DEPTH) ** -0.25
LN_EPS = 1e-5
GN_EPS = 64e-5

kernel_name = 'hybrid_rwkv7_chunkgmlp_hmoe_step'


def layer_norm(x, g, b, eps=LN_EPS):
    xf = x.astype(jnp.float32)
    mu = jnp.mean(xf, axis=-1, keepdims=True)
    var = jnp.mean(jnp.square(xf - mu), axis=-1, keepdims=True)
    return ((xf - mu) * lax.rsqrt(var + eps) * g + b).astype(x.dtype)


def wkv7_scan(S0, r, w, k, v, kk, a):
    def step(S, inp):
        r_t, w_t, k_t, v_t, kk_t, a_t = inp
        sa = jnp.einsum('bhij,bhj->bhi', S, -kk_t)
        S = (S * w_t[:, :, None, :] + sa[..., None] * (kk_t * a_t)[:, :, None, :]
             + v_t[..., None] * k_t[:, :, None, :])
        return S, jnp.einsum('bhij,bhj->bhi', S, r_t)
    xs = tuple(jnp.moveaxis(t.astype(jnp.float32), 1, 0) for t in (r, w, k, v, kk, a))
    S, o = lax.scan(step, S0.astype(jnp.float32), xs)
    return S, jnp.moveaxis(o, 0, 1)


def token_mix(h, h_prev0, S0, w_in, mu_shift, w0, w_decay_up, a0, w_iclr_up, k_k, k_a,
              r_k, lnx_g, lnx_b, lnv_g, lnv_b, w_spatial, b_spatial, w_branch_a,
              w_branch_b, w_out):
    B, T, _ = h.shape
    f32 = jnp.float32
    z = jnp.einsum('btd,dn->btn', h, w_in)
    zA = z[..., :N_SHIFT]
    zA_prev0 = h_prev0 @ w_in[:, :N_SHIFT]
    zA_prev = jnp.concatenate([zA_prev0[:, None, :], zA[:, :-1]], axis=1)
    zA = zA + mu_shift * (zA_prev - zA)
    r, k, v, xw, xa = jnp.split(zA, [D_A, 2 * D_A, 3 * D_A, 3 * D_A + R_W], axis=-1)
    w_raw = -jax.nn.softplus(-(w0 + jnp.tanh(xw) @ w_decay_up).astype(f32)) - 0.5
    decay = jnp.exp(-jnp.exp(w_raw))
    a = jax.nn.sigmoid((a0 + xa @ w_iclr_up).astype(f32))
    hd = lambda t: t.reshape(B, T, H_A, HEAD_A)
    kk = hd(k * k_k).astype(f32)
    kk = kk / jnp.maximum(jnp.linalg.norm(kk, axis=-1, keepdims=True), 1e-12)
    k = k.astype(f32) * (1.0 + (a - 1.0) * k_a)
    r_h, k_h, v_h = hd(r), hd(k), hd(v)
    S, o = wkv7_scan(S0, r_h, hd(decay), k_h, v_h, kk, hd(a))
    mu = jnp.mean(o, axis=-1, keepdims=True)
    var = jnp.mean(jnp.square(o - mu), axis=-1, keepdims=True)
    o = (o - mu) * lax.rsqrt(var + GN_EPS) * lnx_g.reshape(H_A, HEAD_A) + lnx_b.reshape(H_A, HEAD_A)
    o = o + jnp.sum(r_h * k_h * r_k, axis=-1, keepdims=True) * v_h
    oA = o.reshape(B, T, D_A).astype(h.dtype)
    u = jax.nn.gelu(z[..., N_SHIFT:N_SHIFT + D_B], approximate=False)
    vg = jax.nn.gelu(z[..., N_SHIFT + D_B:N_SHIFT + 2 * D_B], approximate=False)
    vg = layer_norm(vg, lnv_g, lnv_b)
    L = min(T, CHUNK)
    vc = vg.reshape(B, T // L, L, G_B, C_B)
    ws = w_spatial[:, :L, :L] * jnp.tril(jnp.ones((L, L), w_spatial.dtype))
    s = jnp.einsum('gts,bnsgc->bntgc', ws, vc) + b_spatial[:, :L].T[None, None, :, :, None]
    oB = u * s.reshape(B, T, D_B)
    off = N_SHIFT + 2 * D_B
    gA = jax.nn.sigmoid(z[..., off:off + D_MODEL])
    gB = jax.nn.sigmoid(z[..., off + D_MODEL:])
    y = (gA * (oA @ w_branch_a) + gB * (oB @ w_branch_b)) @ w_out
    return y, S.astype(S0.dtype), h[:, -1], vg


def hier_moe(h, w_rg, b_rg, w_re, b_re, w_eg, w_eu, w_ed):
    B, T, D = h.shape
    f32 = jnp.float32
    xt = h.reshape(B * T, D)
    lg = (xt @ w_rg + b_rg).astype(f32)
    pg = jax.nn.softmax(lg, axis=-1)
    gsel = jnp.argmax(lg, axis=-1)
    pg_sel = jnp.take_along_axis(pg, gsel[:, None], axis=1)
    le = (xt @ w_re + b_re).astype(f32).reshape(-1, N_GROUPS, EXP_PER_GROUP)
    le_sel = jnp.take_along_axis(le, gsel[:, None, None], axis=1)[:, 0]
    top_v, top_i = lax.top_k(le_sel, TOP_K)
    wts = jax.nn.softmax(top_v, axis=-1) * pg_sel
    eid = gsel[:, None] * EXP_PER_GROUP + top_i
    gate = jnp.sum(jax.nn.one_hot(eid, N_EXPERTS, dtype=f32) * wts[..., None], axis=1)

    def expert_step(acc, inp):
        wg, wu, wd, ge = inp
        hid = jax.nn.silu(xt @ wg) * (xt @ wu)
        return acc + ge[:, None] * (hid @ wd).astype(f32), None

    y, _ = lax.scan(expert_step, jnp.zeros(xt.shape, f32), (w_eg, w_eu, w_ed, gate.T))
    return y.astype(h.dtype).reshape(B, T, D)


def run_trunk(x, c, wkv_in, shift_in, params):
    (w_ada, b_ada, w_in, mu_shift, w0, w_decay_up, a0, w_iclr_up, k_k, k_a, r_k,
     lnx_g, lnx_b, lnv_g, lnv_b, w_spatial, b_spatial, w_branch_a, w_branch_b, w_out,
     ln1_g, ln1_b, w_route_group, b_route_group, w_route_expert, b_route_expert,
     w_exp_gate, w_exp_up, w_exp_down, ln2_g, ln2_b) = params
    wkv_out, shift_out, v_out = [], [], []
    for l in range(DEPTH):
        mod = jax.nn.silu(c) @ w_ada[l] + b_ada[l]
        sh1, sc1, g1, sh2, sc2, g2 = jnp.split(mod[:, None, :], 6, axis=-1)
        h = x * (1.0 + sc1) + sh1
        y, S, h_last, v_rows = token_mix(
            h, shift_in[l], wkv_in[l], w_in[l], mu_shift[l], w0[l], w_decay_up[l], a0[l],
            w_iclr_up[l], k_k[l], k_a[l], r_k[l], lnx_g[l], lnx_b[l], lnv_g[l], lnv_b[l],
            w_spatial[l], b_spatial[l], w_branch_a[l], w_branch_b[l], w_out[l])
        x = layer_norm(ALPHA * x + g1 * y, ln1_g[l], ln1_b[l])
        h2 = x * (1.0 + sc2) + sh2
        m = hier_moe(h2, w_route_group[l], b_route_group[l], w_route_expert[l],
                     b_route_expert[l], w_exp_gate[l], w_exp_up[l], w_exp_down[l])
        x = layer_norm(ALPHA * x + g2 * m, ln2_g[l], ln2_b[l])
        wkv_out.append(S)
        shift_out.append(h_last)
        v_out.append(v_rows)
    return x, jnp.stack(wkv_out), jnp.stack(shift_out), jnp.stack(v_out)


def setup_inputs(seed: int = 0) -> dict:
    key = jax.random.key(seed)
    ks = iter(jax.random.split(key, 48))
    def nrm(shape, scale):
        return jax.random.normal(next(ks), shape, jnp.float32) * scale
    def unif(shape, lo, hi):
        return jax.random.uniform(next(ks), shape, jnp.float32, lo, hi)
    L, D = DEPTH, D_MODEL
    return {
        'x_prompt': nrm((BATCH, SEQ, D), 1.0),
        'x_sample': nrm((DEC_BATCH, DEC_SEQ, D), 1.0),
        'c_prompt': nrm((BATCH, D), 1.0),
        'c_sample': nrm((DEC_BATCH, D), 1.0),
        'state_wkv': nrm((L, DEC_BATCH, H_A, HEAD_A, HEAD_A), 0.3),
        'state_shift': nrm((L, DEC_BATCH, D), 1.0),
        'w_ada': nrm((L, D, 6 * D), 0.5 * D ** -0.5),
        'b_ada': nrm((L, 6 * D), 0.02),
        'w_in': nrm((L, D, N_IN), D ** -0.5),
        'mu_shift': unif((L, N_SHIFT), 0.1, 0.9),
        'w0': unif((L, D_A), -5.0, 1.0),
        'w_decay_up': nrm((L, R_W, D_A), 0.3 * R_W ** -0.5),
        'a0': nrm((L, D_A), 0.3),
        'w_iclr_up': nrm((L, R_A, D_A), 0.3 * R_A ** -0.5),
        'k_k': 0.85 + nrm((L, D_A), 0.02),
        'k_a': 1.0 + nrm((L, D_A), 0.02),
        'r_k': nrm((L, H_A, HEAD_A), 0.1),
        'lnx_g': 1.0 + nrm((L, D_A), 0.02),
        'lnx_b': nrm((L, D_A), 0.02),
        'lnv_g': 1.0 + nrm((L, D_B), 0.02),
        'lnv_b': nrm((L, D_B), 0.02),
        'w_spatial': nrm((L, G_B, CHUNK, CHUNK), 0.5 * CHUNK ** -0.5),
        'b_spatial': 1.0 + nrm((L, G_B, CHUNK), 0.01),
        'w_branch_a': nrm((L, D_A, D), D_A ** -0.5),
        'w_branch_b': nrm((L, D_B, D), D_B ** -0.5),
        'w_out': nrm((L, D, D), BETA * D ** -0.5),
        'ln1_g': 1.0 + nrm((L, D), 0.02),
        'ln1_b': nrm((L, D), 0.02),
        'w_route_group': nrm((L, D, N_GROUPS), D ** -0.5),
        'b_route_group': nrm((L, N_GROUPS), 0.01),
        'w_route_expert': nrm((L, D, N_EXPERTS), D ** -0.5),
        'b_route_expert': nrm((L, N_EXPERTS), 0.01),
        'w_exp_gate': nrm((L, N_EXPERTS, D, D_EXPERT), D ** -0.5),
        'w_exp_up': nrm((L, N_EXPERTS, D, D_EXPERT), D ** -0.5),
        'w_exp_down': nrm((L, N_EXPERTS, D_EXPERT, D), BETA * D_EXPERT ** -0.5),
        'ln2_g': 1.0 + nrm((L, D), 0.02),
        'ln2_b': nrm((L, D), 0.02),
    }


def reference(x_prompt, x_sample, c_prompt, c_sample, state_wkv, state_shift,
              w_ada, b_ada, w_in, mu_shift, w0, w_decay_up, a0, w_iclr_up, k_k, k_a, r_k,
              lnx_g, lnx_b, lnv_g, lnv_b, w_spatial, b_spatial, w_branch_a, w_branch_b,
              w_out, ln1_g, ln1_b, w_route_group, b_route_group, w_route_expert,
              b_route_expert, w_exp_gate, w_exp_up, w_exp_down, ln2_g, ln2_b):
    params = (w_ada, b_ada, w_in, mu_shift, w0, w_decay_up, a0, w_iclr_up, k_k, k_a, r_k,
              lnx_g, lnx_b, lnv_g, lnv_b, w_spatial, b_spatial, w_branch_a, w_branch_b,
              w_out, ln1_g, ln1_b, w_route_group, b_route_group, w_route_expert,
              b_route_expert, w_exp_gate, w_exp_up, w_exp_down, ln2_g, ln2_b)
    bp = x_prompt.shape[0]
    wkv0 = jnp.zeros((DEPTH, bp, H_A, HEAD_A, HEAD_A), state_wkv.dtype)
    shift0 = jnp.zeros((DEPTH, bp, D_MODEL), x_prompt.dtype)
    y_prompt, new_wkv_prompt, new_shift_prompt, _ = run_trunk(x_prompt, c_prompt, wkv0, shift0, params)
    y_sample, new_wkv_sample, new_shift_sample, new_chunk_v_sample = run_trunk(
        x_sample, c_sample, state_wkv, state_shift, params)
    return (y_prompt, y_sample, new_wkv_prompt, new_shift_prompt,
            new_wkv_sample, new_shift_sample, new_chunk_v_sample)
```

```python
import functools

import jax
import jax.numpy as jnp
from jax import lax
from jax.experimental import pallas as pl
from jax.experimental.pallas import tpu as pltpu

F32 = jnp.float32
BF16 = jnp.bfloat16

D_MODEL = 1024
DEPTH = 2
HEAD = 64
N_HEADS = 8
D_A = N_HEADS * HEAD
R_LORA = 32
CHUNK = 128
N_GROUPS_B = 8
D_B = 512
N_SHIFT = 3 * D_A + 2 * R_LORA
N_ROUTE_GROUPS = 4
EXP_PER_GROUP = 8
N_EXPERTS = N_ROUTE_GROUPS * EXP_PER_GROUP
D_EXPERT = 256
ALPHA = (2 * DEPTH) ** 0.25
LN_EPS = 1e-5
GN_EPS = 64e-5

LANES = 128
N_SHIFT_PAD = 13 * LANES
COL_U = N_SHIFT_PAD
COL_VG = COL_U + D_B
COL_GA = COL_VG + D_B
COL_GB = COL_GA + D_MODEL
N_IN_PAD = COL_GB + D_MODEL
WKV_CHUNK = 64
VMEM_CAP_BYTES = 60000 * 1024

_NN = (((1,), (0,)), ((), ()))
_B_NT = (((2,), (2,)), ((0,), (0,)))
_B_NN = (((2,), (1,)), ((0,), (0,)))
_B_TN = (((1,), (1,)), ((0,), (0,)))


def _dot(a, b, dims=_NN):
    return lax.dot_general(a, b, dims, preferred_element_type=F32)


def _split2(x):
    hi = x.astype(BF16)
    lo = (x - hi.astype(F32)).astype(BF16)
    return hi, lo


def _mm1(a, b, dims=_NN):
    return _dot(a.astype(BF16), b.astype(BF16), dims)


def _mm3(a, b, dims=_NN):
    ah, al = _split2(a)
    bh, bl = _split2(b)
    return _dot(ah, bh, dims) + (_dot(ah, bl, dims) + _dot(al, bh, dims))


def _segsum(x, bd):
    hi, lo = _split2(x)
    return _dot(hi, bd) + _dot(lo, bd)


def _layer_norm(x, g, b):
    mu = jnp.mean(x, axis=-1, keepdims=True)
    d = x - mu
    var = jnp.mean(d * d, axis=-1, keepdims=True)
    return d * lax.rsqrt(var + LN_EPS) * g + b


def _gelu(x):
    return 0.5 * x * (1.0 + lax.erf(x * 0.7071067811865476))


def _params(sem, est_bytes):
    limit = int(min(VMEM_CAP_BYTES, max(est_bytes, 16 * 1024 * 1024)))
    return pltpu.CompilerParams(dimension_semantics=sem, vmem_limit_bytes=limit)


def _const_spec(shape, single_buffer=False):
    nd = len(shape)
    if single_buffer:
        return pl.BlockSpec(shape, lambda *_: (0,) * nd, pipeline_mode=pl.Buffered(1))
    return pl.BlockSpec(shape, lambda *_: (0,) * nd)


def _ada_body(c_ref, w_ref, b_ref, o_ref):
    c = c_ref[...]
    s = c * jax.nn.sigmoid(c)
    o_ref[0] = _mm3(s, w_ref[0]) + b_ref[0]


def _ada(c_all, w_ada, b_ada):
    depth, d, n6 = w_ada.shape
    m = c_all.shape[0]
    tn = 512
    return pl.pallas_call(
        _ada_body,
        grid=(depth, n6 // tn),
        in_specs=[
            pl.BlockSpec((m, d), lambda l, j: (0, 0)),
            pl.BlockSpec((1, d, tn), lambda l, j: (l, 0, j)),
            pl.BlockSpec((1, 1, tn), lambda l, j: (l, 0, j)),
        ],
        out_specs=pl.BlockSpec((1, m, tn), lambda l, j: (l, 0, j)),
        out_shape=jax.ShapeDtypeStruct((depth, m, n6), F32),
        compiler_params=_params(("arbitrary", "arbitrary"), 24 * 2**20),
        name="ada_mod",
    )(c_all, w_ada, b_ada.reshape(depth, 1, n6))


def _mm_body(x_ref, w_ref, o_ref):
    o_ref[...] = _mm1(x_ref[...], w_ref[...])


def _matmul(x, w):
    m, k = x.shape
    n = w.shape[1]
    return pl.pallas_call(
        _mm_body,
        grid=(1,),
        in_specs=[_const_spec((m, k)), _const_spec((k, n))],
        out_specs=_const_spec((m, n)),
        out_shape=jax.ShapeDtypeStruct((m, n), F32),
        compiler_params=_params(("arbitrary",), 24 * 2**20),
        name="shift_proj",
    )(x, w)


def _in_body(seq_mode, tm, x_ref, sc_ref, sh_ref, w_ref, mu_ref, zp_ref, lora_ref,
             w0a0_ref, kk_ref, ka_ref, lng_ref, lnb_ref, bd_ref,
             r_ref, lw_ref, kp_ref, v_ref, kap_ref, bb_ref, u_ref, vg_ref,
             ga_ref, gb_ref, hl_ref, carry_ref):
    h = x_ref[0] * (1.0 + sc_ref[0]) + sh_ref[0]
    if seq_mode:
        hl_ref[0] = h[tm - 1:tm, :]
    else:
        hl_ref[0] = h
    z = _dot(h.astype(BF16), w_ref[...])
    za = z[:, :N_SHIFT_PAD]
    if seq_mode:
        @pl.when(pl.program_id(1) == 0)
        def _():
            carry_ref[...] = zp_ref[0]

        row = lax.broadcasted_iota(jnp.int32, za.shape, 0)
        prev = jnp.where(row == 0, carry_ref[...], pltpu.roll(za, 1, 0))
        carry_ref[...] = za[tm - 1:tm, :]
    else:
        prev = zp_ref[0]
    mix = za + mu_ref[...] * (prev - za)
    r = mix[:, 0:D_A]
    k = mix[:, D_A:2 * D_A]
    v = mix[:, 2 * D_A:3 * D_A]
    xwa = mix[:, 3 * D_A:N_SHIFT_PAD]
    lane = lax.broadcasted_iota(jnp.int32, xwa.shape, 1)
    lora_in = jnp.where(lane < R_LORA, jnp.tanh(xwa), xwa)
    pre = w0a0_ref[...] + _mm3(lora_in, lora_ref[...])
    yw = -pre[:, :D_A]
    softplus = jnp.maximum(yw, 0.0) + jnp.log1p(jnp.exp(-jnp.abs(yw)))
    lw = -jnp.exp(-softplus - 0.5)
    a = jax.nn.sigmoid(pre[:, D_A:])
    bd = bd_ref[...]
    kk = k * kk_ref[...]
    kap = kk / jnp.maximum(jnp.sqrt(_segsum(kk * kk, bd)), 1e-12)
    r_ref[0] = r
    lw_ref[0] = lw
    kp_ref[0] = k * (1.0 + (a - 1.0) * ka_ref[...])
    v_ref[0] = v
    kap_ref[0] = kap
    bb_ref[0] = kap * a
    u_ref[0] = _gelu(z[:, COL_U:COL_VG])
    vg_ref[0] = _layer_norm(_gelu(z[:, COL_VG:COL_GA]), lng_ref[...], lnb_ref[...])
    ga_ref[0] = jax.nn.sigmoid(z[:, COL_GA:COL_GB])
    gb_ref[0] = jax.nn.sigmoid(z[:, COL_GB:N_IN_PAD])


def _in_stage(seq_mode, x, sc, sh, zprev, p):
    b, t, d = x.shape
    tm = 256 if seq_mode else t
    tmod = 1 if seq_mode else tm
    grid = (b, t // tm)
    tok = lambda n: pl.BlockSpec((1, tm, n), lambda i, j: (i, j, 0))
    mod = pl.BlockSpec((1, tmod, d), lambda i, j: (i, j if not seq_mode else 0, 0))
    zp_spec = (pl.BlockSpec((1, 1, N_SHIFT_PAD), lambda i, j: (i, 0, 0)) if seq_mode
               else tok(N_SHIFT_PAD))
    hl_spec = (pl.BlockSpec((1, 1, d), lambda i, j: (i, 0, 0)) if seq_mode else tok(d))
    hl_shape = (b, 1, d) if seq_mode else (b, t, d)
    out_cols = [D_A] * 6 + [D_B] * 2 + [d] * 2
    est = (2 * tm * (d + N_SHIFT_PAD + sum(out_cols) + d) * 4 + 2 * d * N_IN_PAD * 2
           + 3 * tm * N_IN_PAD * 4 + 4 * 2**20)
    outs = pl.pallas_call(
        functools.partial(_in_body, seq_mode, tm),
        grid=grid,
        in_specs=[tok(d), mod, mod,
                  _const_spec((d, N_IN_PAD), True), _const_spec((1, N_SHIFT_PAD)), zp_spec,
                  _const_spec((LANES, 2 * D_A)), _const_spec((1, 2 * D_A)),
                  _const_spec((1, D_A)), _const_spec((1, D_A)),
                  _const_spec((1, D_B)), _const_spec((1, D_B)),
                  _const_spec((D_A, D_A))],
        out_specs=[tok(n) for n in out_cols] + [hl_spec],
        out_shape=[jax.ShapeDtypeStruct((b, t, n), F32) for n in out_cols]
        + [jax.ShapeDtypeStruct(hl_shape, F32)],
        scratch_shapes=[pltpu.VMEM((1, N_SHIFT_PAD), F32)],
        compiler_params=_params(("arbitrary", "arbitrary"), est),
        name="in_stage_seq" if seq_mode else "in_stage_row",
    )(x, sc, sh, p["w_in"], p["mu"], zprev, p["lora"], p["w0a0"], p["k_k"], p["k_a"],
      p["lnv_g"], p["lnv_b"], p["bd"])
    return outs


def _heads(x):
    return jnp.stack([x[:, HEAD * h:HEAD * (h + 1)] for h in range(N_HEADS)], axis=0)


def _wkv_seq_body(nsub, r_ref, lw_ref, kp_ref, v_ref, kap_ref, bb_ref, s0_ref,
                  o_ref, sout_ref, s_scr):
    c = WKV_CHUNK

    @pl.when(pl.program_id(1) == 0)
    def _():
        s_scr[...] = s0_ref[0]

    row = lax.broadcasted_iota(jnp.int32, (c, c), 0)
    col = lax.broadcasted_iota(jnp.int32, (c, c), 1)
    tri_incl = (row >= col)[None]
    tri_strict = (row > col)[None]
    tri_bf = (row >= col).astype(BF16)
    mm = _mm3

    def sub(i, carry):
        sl = pl.ds(pl.multiple_of(i * c, c), c)
        r = r_ref[0, sl, :]
        lw = lw_ref[0, sl, :]
        k = kp_ref[0, sl, :]
        v = v_ref[0, sl, :]
        kap = kap_ref[0, sl, :]
        b = bb_ref[0, sl, :]
        h1 = lw.astype(BF16)
        r1 = lw - h1.astype(F32)
        h2 = r1.astype(BF16)
        h3 = (r1 - h2.astype(F32)).astype(BF16)
        g = _dot(tri_bf, h1) + (_dot(tri_bf, h2) + _dot(tri_bf, h3))
        g_end = g[c - 1:c, :]
        e_neg = jnp.exp(-g)
        e_end = jnp.exp(g_end - g)
        kap_t = _heads(kap * jnp.exp(g - lw))
        b_t = _heads(b * e_neg)
        k_t = _heads(k * e_neg)
        r_t = _heads(r * jnp.exp(g))
        b_e = _heads(b * e_end)
        k_e = _heads(k * e_end)
        vv = _heads(v)
        s = s_scr[...]
        l_b = jnp.where(tri_strict, mm(kap_t, b_t, _B_NT), 0.0)
        l_k = jnp.where(tri_strict, mm(kap_t, k_t, _B_NT), 0.0)
        a_rb = jnp.where(tri_incl, mm(r_t, b_t, _B_NT), 0.0)
        a_rk = jnp.where(tri_incl, mm(r_t, k_t, _B_NT), 0.0)
        w = -(mm(kap_t, s, _B_NT) + mm(l_k, vv, _B_NN))
        m = -l_b
        n = 1
        while n < c:
            w = w + mm(m, w, _B_NN)
            n *= 2
            if n < c:
                m = mm(m, m, _B_NN)
        o = mm(r_t, s, _B_NT) + mm(a_rb, w, _B_NN) + mm(a_rk, vv, _B_NN)
        s_scr[...] = (s * _heads(jnp.exp(g_end)) + mm(w, b_e, _B_TN) + mm(vv, k_e, _B_TN))
        for h in range(N_HEADS):
            o_ref[0, sl, HEAD * h:HEAD * (h + 1)] = o[h]
        return carry

    lax.fori_loop(0, nsub, sub, 0)

    @pl.when(pl.program_id(1) == pl.num_programs(1) - 1)
    def _():
        sout_ref[0] = s_scr[...]


def _wkv_seq(r, lw, kp, v, kap, bb, s0):
    b, t, _ = r.shape
    tc = 256
    nsub = tc // WKV_CHUNK
    tok = pl.BlockSpec((1, tc, D_A), lambda i, j: (i, j, 0))
    st = pl.BlockSpec((1, N_HEADS, HEAD, HEAD), lambda i, j: (i, 0, 0, 0))
    return pl.pallas_call(
        functools.partial(_wkv_seq_body, nsub),
        grid=(b, t // tc),
        in_specs=[tok] * 6 + [st],
        out_specs=[tok, st],
        out_shape=[jax.ShapeDtypeStruct((b, t, D_A), F32),
                   jax.ShapeDtypeStruct((b, N_HEADS, HEAD, HEAD), F32)],
        scratch_shapes=[pltpu.VMEM((N_HEADS, HEAD, HEAD), F32)],
        compiler_params=_params(("arbitrary", "arbitrary"), 32 * 2**20),
        name="wkv_seq",
    )(r, lw, kp, v, kap, bb, s0)


def _wkv_step_body(r_ref, lw_ref, kp_ref, v_ref, kap_ref, bb_ref, s0_ref, o_ref, sout_ref):
    eye = (lax.broadcasted_iota(jnp.int32, (HEAD, HEAD), 0)
           == lax.broadcasted_iota(jnp.int32, (HEAD, HEAD), 1))[None]
    for h in range(N_HEADS):
        hs = slice(HEAD * h, HEAD * (h + 1))
        row = lambda ref: ref[:, hs][:, None, :]
        s = s0_ref[:, h]
        sa = -jnp.sum(s * row(kap_ref), axis=-1, keepdims=True)
        v_col = jnp.sum(jnp.where(eye, row(v_ref), 0.0), axis=-1, keepdims=True)
        s_new = s * jnp.exp(row(lw_ref)) + sa * row(bb_ref) + v_col * row(kp_ref)
        sout_ref[:, h] = s_new
        o_col = jnp.sum(s_new * row(r_ref), axis=-1, keepdims=True)
        o_ref[:, hs] = jnp.sum(jnp.where(eye, o_col, 0.0), axis=1)


def _wkv_step(r, lw, kp, v, kap, bb, s0):
    m = r.shape[0]
    nb = 8
    tok = pl.BlockSpec((nb, D_A), lambda i: (i, 0))
    st = pl.BlockSpec((nb, N_HEADS, HEAD, HEAD), lambda i: (i, 0, 0, 0))
    return pl.pallas_call(
        _wkv_step_body,
        grid=(m // nb,),
        in_specs=[tok] * 6 + [st],
        out_specs=[tok, st],
        out_shape=[jax.ShapeDtypeStruct((m, D_A), F32),
                   jax.ShapeDtypeStruct((m, N_HEADS, HEAD, HEAD), F32)],
        compiler_params=_params(("arbitrary",), 24 * 2**20),
        name="wkv_step",
    )(r, lw, kp, v, kap, bb, s0)


def _merge_body(seq_mode, tm, x_ref, o_ref, r_ref, kp_ref, v_ref, u_ref, vg_ref,
                ga_ref, gb_ref, g1_ref, sc2_ref, sh2_ref, lxg_ref, lxb_ref, rk_ref,
                bd_ref, ws_ref, bs_ref, pa_ref, pb_ref, wo_ref, l1g_ref, l1b_ref,
                x1_ref, h2_ref):
    bd = bd_ref[...]
    o = o_ref[0]
    inv_n = 1.0 / HEAD
    mu = _segsum(o, bd) * inv_n
    d = o - mu
    var = _segsum(d * d, bd) * inv_n
    on = d * lax.rsqrt(var + GN_EPS) * lxg_ref[...] + lxb_ref[...]
    v = v_ref[0]
    o_a = on + _segsum(r_ref[0] * kp_ref[0] * rk_ref[...], bd) * v
    vg = vg_ref[0]
    if seq_mode:
        row = lax.broadcasted_iota(jnp.int32, (CHUNK, CHUNK), 0)
        col = lax.broadcasted_iota(jnp.int32, (CHUNK, CHUNK), 1)
        lane = lax.broadcasted_iota(jnp.int32, (CHUNK, D_B), 1)
        gsz = D_B // N_GROUPS_B
        pieces = []
        for j in range(tm // CHUNK):
            vc = vg[j * CHUNK:(j + 1) * CHUNK, :]
            acc = bs_ref[...]
            for g in range(N_GROUPS_B):
                wg = jnp.where(row >= col, ws_ref[g], 0.0)
                vm = jnp.where((lane >= g * gsz) & (lane < (g + 1) * gsz), vc, 0.0)
                acc = acc + _mm3(wg, vm)
            pieces.append(acc)
        s = jnp.concatenate(pieces, axis=0) if len(pieces) > 1 else pieces[0]
    else:
        s = vg * ws_ref[...] + bs_ref[...]
    o_b = u_ref[0] * s
    y = _mm1(ga_ref[0] * _mm1(o_a, pa_ref[...]) + gb_ref[0] * _mm1(o_b, pb_ref[...]),
             wo_ref[...])
    x1 = _layer_norm(ALPHA * x_ref[0] + g1_ref[0] * y, l1g_ref[...], l1b_ref[...])
    x1_ref[0] = x1
    h2_ref[0] = x1 * (1.0 + sc2_ref[0]) + sh2_ref[0]


def _merge_stage(seq_mode, x, o, r, kp, v, u, vg, ga, gb, g1, sc2, sh2, p):
    b, t, d = x.shape
    tm = 256 if seq_mode else t
    tmod = 1 if seq_mode else tm
    tok = lambda n: pl.BlockSpec((1, tm, n), lambda i, j: (i, j, 0))
    mod = pl.BlockSpec((1, tmod, d), lambda i, j: (i, j if not seq_mode else 0, 0))
    ws, bs = (p["w_spatial"], p["b_spatial_full"]) if seq_mode else (p["ws_row"], p["bs_row"])
    est = 2 * tm * (2 * d + 7 * D_A + 2 * d + 2 * d) * 4 + 8 * tm * d * 4 + 16 * 2**20
    return pl.pallas_call(
        functools.partial(_merge_body, seq_mode, tm),
        grid=(b, t // tm),
        in_specs=[tok(d)] + [tok(D_A)] * 6 + [tok(d), tok(d), mod, mod, mod,
                  _const_spec((1, D_A)), _const_spec((1, D_A)), _const_spec((1, D_A)),
                  _const_spec((D_A, D_A)), _const_spec(ws.shape), _const_spec(bs.shape),
                  _const_spec((D_A, d)), _const_spec((D_B, d)), _const_spec((d, d)),
                  _const_spec((1, d)), _const_spec((1, d))],
        out_specs=[tok(d), tok(d)],
        out_shape=[jax.ShapeDtypeStruct((b, t, d), F32)] * 2,
        compiler_params=_params(("arbitrary", "arbitrary"), est),
        name="merge_seq" if seq_mode else "merge_row",
    )(x, o, r, kp, v, u, vg, ga, gb, g1, sc2, sh2, p["lnx_g"], p["lnx_b"], p["r_k"],
      p["bd"], ws, bs, p["w_branch_a"], p["w_branch_b"], p["w_out"], p["ln1_g"], p["ln1_b"])


def _route_body(h_ref, w_ref, b_ref, gate_ref):
    lg = _mm3(h_ref[...], w_ref[...]) + b_ref[...]
    lane = lax.broadcasted_iota(jnp.int32, lg.shape, 1)
    lanef = lane.astype(F32)
    neg = -jnp.inf
    is_g = (lane >= N_EXPERTS) & (lane < N_EXPERTS + N_ROUTE_GROUPS)
    mg = jnp.max(jnp.where(is_g, lg, neg), axis=-1, keepdims=True)
    gidx = jnp.min(jnp.where(is_g & (lg == mg), lanef - N_EXPERTS, 1e9), axis=-1, keepdims=True)
    pg_sel = 1.0 / jnp.sum(jnp.where(is_g, jnp.exp(lg - mg), 0.0), axis=-1, keepdims=True)
    lo = gidx * EXP_PER_GROUP
    in_grp = (lanef >= lo) & (lanef < lo + EXP_PER_GROUP)
    t1 = jnp.max(jnp.where(in_grp, lg, neg), axis=-1, keepdims=True)
    i1 = jnp.min(jnp.where(in_grp & (lg == t1), lanef, 1e9), axis=-1, keepdims=True)
    rest = in_grp & (lanef != i1)
    t2 = jnp.max(jnp.where(rest, lg, neg), axis=-1, keepdims=True)
    i2 = jnp.min(jnp.where(rest & (lg == t2), lanef, 1e9), axis=-1, keepdims=True)
    e2 = jnp.exp(t2 - t1)
    w1 = pg_sel / (1.0 + e2)
    w2 = pg_sel * e2 / (1.0 + e2)
    gate_ref[...] = jnp.where(lanef == i1, w1, 0.0) + jnp.where(lanef == i2, w2, 0.0)


def _route(h2, p):
    m, d = h2.shape
    tm = min(m, 512)
    return pl.pallas_call(
        _route_body,
        grid=(m // tm,),
        in_specs=[pl.BlockSpec((tm, d), lambda i: (i, 0)),
                  _const_spec((d, LANES)), _const_spec((1, LANES))],
        out_specs=pl.BlockSpec((tm, LANES), lambda i: (i, 0)),
        out_shape=jax.ShapeDtypeStruct((m, LANES), F32),
        compiler_params=_params(("arbitrary",), 24 * 2**20),
        name="route",
    )(h2, p["w_route"], p["b_route"])


def _moe_body(h_ref, gate_ref, wg_ref, wu_ref, wd_ref, x1_ref, g2_ref, l2g_ref, l2b_ref,
              out_ref, acc_ref, xb_ref):
    e = pl.program_id(1)

    @pl.when(e == 0)
    def _():
        acc_ref[...] = jnp.zeros_like(acc_ref)
        xb_ref[...] = h_ref[...].astype(BF16)

    xb = xb_ref[...]
    pre = _dot(xb, wg_ref[0])
    hid = pre * jax.nn.sigmoid(pre) * _dot(xb, wu_ref[0])
    ye = _mm1(hid, wd_ref[0])
    gate = gate_ref[...]
    lane = lax.broadcasted_iota(jnp.int32, gate.shape, 1)
    ge = jnp.sum(jnp.where(lane == e, gate, 0.0), axis=-1, keepdims=True)
    acc_ref[...] += ge * ye

    @pl.when(e == pl.num_programs(1) - 1)
    def _():
        out_ref[...] = _layer_norm(ALPHA * x1_ref[...] + g2_ref[0] * acc_ref[...],
                                   l2g_ref[...], l2b_ref[...])


def _moe(seq_len, h2, gate, x1, g2, p):
    m, d = h2.shape
    tm = min(seq_len if g2.shape[1] == 1 else m, 1024)
    tok = lambda n: pl.BlockSpec((tm, n), lambda i, e: (i, 0))
    if g2.shape[1] == 1:
        tiles_per_seq = seq_len // tm
        g2_spec = pl.BlockSpec((1, 1, d), lambda i, e: (i // tiles_per_seq, 0, 0))
    else:
        g2_spec = pl.BlockSpec((1, tm, d), lambda i, e: (0, i, 0))
    est = 2 * tm * (3 * d + LANES) * 4 + tm * d * 6 + 4 * tm * d * 4 + 8 * 2**20
    return pl.pallas_call(
        _moe_body,
        grid=(m // tm, N_EXPERTS),
        in_specs=[tok(d), tok(LANES),
                  pl.BlockSpec((1, d, D_EXPERT), lambda i, e: (e, 0, 0)),
                  pl.BlockSpec((1, d, D_EXPERT), lambda i, e: (e, 0, 0)),
                  pl.BlockSpec((1, D_EXPERT, d), lambda i, e: (e, 0, 0)),
                  tok(d), g2_spec, _const_spec((1, d)), _const_spec((1, d))],
        out_specs=tok(d),
        out_shape=jax.ShapeDtypeStruct((m, d), F32),
        scratch_shapes=[pltpu.VMEM((tm, d), F32), pltpu.VMEM((tm, d), BF16)],
        compiler_params=_params(("arbitrary", "arbitrary"), est),
        name="moe_dense",
    )(h2, gate, p["w_exp_gate"], p["w_exp_up"], p["w_exp_down"], x1, g2, p["ln2_g"], p["ln2_b"])


def _prep_layer(l, w_in, mu_shift, w0, w_decay_up, a0, w_iclr_up, k_k, k_a, r_k,
                lnx_g, lnx_b, lnv_g, lnv_b, w_spatial, b_spatial, w_branch_a, w_branch_b,
                w_out, ln1_g, ln1_b, w_route_group, b_route_group, w_route_expert,
                b_route_expert, w_exp_gate, w_exp_up, w_exp_down, ln2_g, ln2_b):
    d = D_MODEL
    pad_a = N_SHIFT_PAD - N_SHIFT
    wi = w_in[l]
    w_in_p = jnp.concatenate(
        [wi[:, :N_SHIFT], jnp.zeros((d, pad_a), F32), wi[:, N_SHIFT:]], axis=1).astype(BF16)
    mu = jnp.concatenate([mu_shift[l], jnp.zeros((pad_a,), F32)])[None]
    lora = jnp.zeros((LANES, 2 * D_A), F32)
    lora = lora.at[:R_LORA, :D_A].set(w_decay_up[l]).at[R_LORA:2 * R_LORA, D_A:].set(w_iclr_up[l])
    seg = jnp.arange(D_A) // HEAD
    row1 = lambda x: x.reshape(1, -1)
    gsz = D_B // N_GROUPS_B
    w_route = jnp.concatenate(
        [w_route_expert[l], w_route_group[l],
         jnp.zeros((d, LANES - N_EXPERTS - N_ROUTE_GROUPS), F32)], axis=1)
    b_route = jnp.concatenate(
        [b_route_expert[l], b_route_group[l],
         jnp.zeros((LANES - N_EXPERTS - N_ROUTE_GROUPS,), F32)])[None]
    return dict(
        w_in=w_in_p, mu=mu, lora=lora,
        w0a0=jnp.concatenate([w0[l], a0[l]])[None],
        k_k=row1(k_k[l]), k_a=row1(k_a[l]), r_k=row1(r_k[l]),
        lnx_g=row1(lnx_g[l]), lnx_b=row1(lnx_b[l]),
        lnv_g=row1(lnv_g[l]), lnv_b=row1(lnv_b[l]),
        bd=(seg[:, None] == seg[None, :]).astype(BF16),
        w_spatial=w_spatial[l],
        b_spatial_full=jnp.repeat(b_spatial[l].T, gsz, axis=1),
        ws_row=jnp.repeat(w_spatial[l][:, 0, 0], gsz)[None],
        bs_row=jnp.repeat(b_spatial[l][:, 0], gsz)[None],
        w_branch_a=w_branch_a[l].astype(BF16), w_branch_b=w_branch_b[l].astype(BF16),
        w_out=w_out[l].astype(BF16), ln1_g=row1(ln1_g[l]), ln1_b=row1(ln1_b[l]),
        w_route=w_route, b_route=b_route,
        w_exp_gate=w_exp_gate[l].astype(BF16), w_exp_up=w_exp_up[l].astype(BF16),
        w_exp_down=w_exp_down[l].astype(BF16), ln2_g=row1(ln2_g[l]), ln2_b=row1(ln2_b[l]),
    )


def _trunk(seq_mode, x, mods, wkv_in, shift_in, preps):
    b, t, d = x.shape
    wkv_out, shift_out, v_out = [], [], []
    for l in range(DEPTH):
        p = preps[l]
        sh1, sc1, g1, sh2, sc2, g2 = mods[l]
        if seq_mode:
            zprev = jnp.zeros((b, 1, N_SHIFT_PAD), F32) if shift_in is None else shift_in[l]
        else:
            zprev = _matmul(shift_in[l], p["w_in"][:, :N_SHIFT_PAD]).reshape(b, t, N_SHIFT_PAD)
        r, lw, kp, v, kap, bb, u, vg, ga, gb, hl = _in_stage(seq_mode, x, sc1, sh1, zprev, p)
        if seq_mode:
            o, s_new = _wkv_seq(r, lw, kp, v, kap, bb, wkv_in[l])
        else:
            flat = lambda a: a.reshape(t, D_A)
            o, s_new = _wkv_step(flat(r), flat(lw), flat(kp), flat(v), flat(kap), flat(bb),
                                 wkv_in[l])
            o = o.reshape(b, t, D_A)
        x1, h2 = _merge_stage(seq_mode, x, o, r, kp, v, u, vg, ga, gb, g1, sc2, sh2, p)
        m = b * t
        h2f = h2.reshape(m, d)
        gate = _route(h2f, p)
        x = _moe(t, h2f, gate, x1.reshape(m, d), g2, p).reshape(b, t, d)
        wkv_out.append(s_new)
        shift_out.append(hl)
        v_out.append(vg)
    return x, wkv_out, shift_out, v_out


def kernel(x_prompt, x_sample, c_prompt, c_sample, state_wkv, state_shift, w_ada, b_ada, w_in, mu_shift, w0, w_decay_up, a0, w_iclr_up, k_k, k_a, r_k, lnx_g, lnx_b, lnv_g, lnv_b, w_spatial, b_spatial, w_branch_a, w_branch_b, w_out, ln1_g, ln1_b, w_route_group, b_route_group, w_route_expert, b_route_expert, w_exp_gate, w_exp_up, w_exp_down, ln2_g, ln2_b):
    bp, tp, d = x_prompt.shape
    bs = x_sample.shape[0]
    layer_params = (w_in, mu_shift, w0, w_decay_up, a0, w_iclr_up, k_k, k_a, r_k, lnx_g,
                    lnx_b, lnv_g, lnv_b, w_spatial, b_spatial, w_branch_a, w_branch_b, w_out,
                    ln1_g, ln1_b, w_route_group, b_route_group, w_route_expert,
                    b_route_expert, w_exp_gate, w_exp_up, w_exp_down, ln2_g, ln2_b)
    preps = [_prep_layer(l, *layer_params) for l in range(DEPTH)]
    mod_all = _ada(jnp.concatenate([c_prompt, c_sample], axis=0), w_ada, b_ada)
    mods_p, mods_s = [], []
    for l in range(DEPTH):
        parts = jnp.split(mod_all[l], 6, axis=-1)
        mods_p.append([q[:bp].reshape(bp, 1, d) for q in parts])
        mods_s.append([q[bp:].reshape(1, bs, d) for q in parts])

    wkv0 = jnp.zeros((DEPTH, bp, N_HEADS, HEAD, HEAD), F32)
    y_p, wkv_p, shift_p, _ = _trunk(True, x_prompt, mods_p, wkv0, None, preps)
    y_s, wkv_s, shift_s, v_s = _trunk(False, x_sample.reshape(1, bs, d), mods_s, state_wkv,
                                      state_shift, preps)
    return (y_p,
            y_s.reshape(bs, 1, d),
            jnp.stack(wkv_p),
            jnp.stack([s.reshape(bp, d) for s in shift_p]),
            jnp.stack(wkv_s),
            jnp.stack([s.reshape(bs, d) for s in shift_s]),
            jnp.stack([q.reshape(bs, 1, D_B) for q in v_s]))
```

```python
import functools

import jax
import jax.numpy as jnp
from jax import lax
from jax.experimental import pallas as pl
from jax.experimental.pallas import tpu as pltpu

F32 = jnp.float32
BF16 = jnp.bfloat16

D_MODEL = 1024
DEPTH = 2
HEAD = 64
N_HEADS = 8
D_A = N_HEADS * HEAD
R_LORA = 32
CHUNK = 128
N_GROUPS_B = 8
D_B = 512
N_SHIFT = 3 * D_A + 2 * R_LORA
N_ROUTE_GROUPS = 4
EXP_PER_GROUP = 8
N_EXPERTS = N_ROUTE_GROUPS * EXP_PER_GROUP
D_EXPERT = 256
ALPHA = (2 * DEPTH) ** 0.25
LN_EPS = 1e-5
GN_EPS = 64e-5

LANES = 128
N_SHIFT_PAD = 13 * LANES
COL_U = N_SHIFT_PAD
COL_VG = COL_U + D_B
COL_GA = COL_VG + D_B
COL_GB = COL_GA + D_MODEL
N_IN_PAD = COL_GB + D_MODEL
WKV_CHUNK = 64
VMEM_CAP_BYTES = 60000 * 1024

_NN = (((1,), (0,)), ((), ()))
_B_NT = (((2,), (2,)), ((0,), (0,)))
_B_NN = (((2,), (1,)), ((0,), (0,)))
_B_TN = (((1,), (1,)), ((0,), (0,)))


def _dot(a, b, dims=_NN):
    return lax.dot_general(a, b, dims, preferred_element_type=F32)


def _split2(x):
    hi = x.astype(BF16)
    lo = (x - hi.astype(F32)).astype(BF16)
    return hi, lo


def _mm1(a, b, dims=_NN):
    return _dot(a.astype(BF16), b.astype(BF16), dims)


def _mm3(a, b, dims=_NN):
    ah, al = _split2(a)
    bh, bl = _split2(b)
    return _dot(ah, bh, dims) + (_dot(ah, bl, dims) + _dot(al, bh, dims))


def _segsum(x, bd):
    hi, lo = _split2(x)
    return _dot(hi, bd) + _dot(lo, bd)


def _layer_norm(x, g, b):
    mu = jnp.mean(x, axis=-1, keepdims=True)
    d = x - mu
    var = jnp.mean(d * d, axis=-1, keepdims=True)
    return d * lax.rsqrt(var + LN_EPS) * g + b


def _gelu(x):
    return 0.5 * x * (1.0 + lax.erf(x * 0.7071067811865476))


def _params(sem, est_bytes):
    limit = int(min(VMEM_CAP_BYTES, max(est_bytes, 16 * 1024 * 1024)))
    return pltpu.CompilerParams(dimension_semantics=sem, vmem_limit_bytes=limit)


def _const_spec(shape, single_buffer=False):
    nd = len(shape)
    if single_buffer:
        return pl.BlockSpec(shape, lambda *_: (0,) * nd, pipeline_mode=pl.Buffered(1))
    return pl.BlockSpec(shape, lambda *_: (0,) * nd)


def _ada_body(c_ref, w_ref, b_ref, o_ref):
    c = c_ref[...]
    s = c * jax.nn.sigmoid(c)
    o_ref[0] = _mm3(s, w_ref[0]) + b_ref[0]


def _ada(c_all, w_ada, b_ada):
    depth, d, n6 = w_ada.shape
    m = c_all.shape[0]
    tn = 512
    return pl.pallas_call(
        _ada_body,
        grid=(depth, n6 // tn),
        in_specs=[
            pl.BlockSpec((m, d), lambda l, j: (0, 0)),
            pl.BlockSpec((1, d, tn), lambda l, j: (l, 0, j)),
            pl.BlockSpec((1, 1, tn), lambda l, j: (l, 0, j)),
        ],
        out_specs=pl.BlockSpec((1, m, tn), lambda l, j: (l, 0, j)),
        out_shape=jax.ShapeDtypeStruct((depth, m, n6), F32),
        compiler_params=_params(("arbitrary", "arbitrary"), 24 * 2**20),
        name="ada_mod",
    )(c_all, w_ada, b_ada.reshape(depth, 1, n6))


def _mm_body(x_ref, w_ref, o_ref):
    o_ref[...] = _mm1(x_ref[...], w_ref[...])


def _matmul(x, w):
    m, k = x.shape
    n = w.shape[1]
    return pl.pallas_call(
        _mm_body,
        grid=(1,),
        in_specs=[_const_spec((m, k)), _const_spec((k, n))],
        out_specs=_const_spec((m, n)),
        out_shape=jax.ShapeDtypeStruct((m, n), F32),
        compiler_params=_params(("arbitrary",), 24 * 2**20),
        name="shift_proj",
    )(x, w)


def _in_body(seq_mode, tm, x_ref, sc_ref, sh_ref, w_ref, mu_ref, zp_ref, lora_ref,
             w0a0_ref, kk_ref, ka_ref, lng_ref, lnb_ref, bd_ref,
             r_ref, lw_ref, kp_ref, v_ref, kap_ref, bb_ref, u_ref, vg_ref,
             ga_ref, gb_ref, hl_ref, carry_ref):
    h = x_ref[0] * (1.0 + sc_ref[0]) + sh_ref[0]
    if seq_mode:
        hl_ref[0] = h[tm - 1:tm, :]
    else:
        hl_ref[0] = h
    z = _dot(h.astype(BF16), w_ref[...])
    za = z[:, :N_SHIFT_PAD]
    if seq_mode:
        @pl.when(pl.program_id(1) == 0)
        def _():
            carry_ref[...] = zp_ref[0]

        row = lax.broadcasted_iota(jnp.int32, za.shape, 0)
        prev = jnp.where(row == 0, carry_ref[...], pltpu.roll(za, 1, 0))
        carry_ref[...] = za[tm - 1:tm, :]
    else:
        prev = zp_ref[0]
    mix = za + mu_ref[...] * (prev - za)
    r = mix[:, 0:D_A]
    k = mix[:, D_A:2 * D_A]
    v = mix[:, 2 * D_A:3 * D_A]
    xwa = mix[:, 3 * D_A:N_SHIFT_PAD]
    lane = lax.broadcasted_iota(jnp.int32, xwa.shape, 1)
    lora_in = jnp.where(lane < R_LORA, jnp.tanh(xwa), xwa)
    pre = w0a0_ref[...] + _mm3(lora_in, lora_ref[...])
    yw = -pre[:, :D_A]
    softplus = jnp.maximum(yw, 0.0) + jnp.log1p(jnp.exp(-jnp.abs(yw)))
    lw = -jnp.exp(-softplus - 0.5)
    a = jax.nn.sigmoid(pre[:, D_A:])
    bd = bd_ref[...]
    kk = k * kk_ref[...]
    kap = kk / jnp.maximum(jnp.sqrt(_segsum(kk * kk, bd)), 1e-12)
    r_ref[0] = r
    lw_ref[0] = lw
    kp_ref[0] = k * (1.0 + (a - 1.0) * ka_ref[...])
    v_ref[0] = v
    kap_ref[0] = kap
    bb_ref[0] = kap * a
    u_ref[0] = _gelu(z[:, COL_U:COL_VG])
    vg_ref[0] = _layer_norm(_gelu(z[:, COL_VG:COL_GA]), lng_ref[...], lnb_ref[...])
    ga_ref[0] = jax.nn.sigmoid(z[:, COL_GA:COL_GB])
    gb_ref[0] = jax.nn.sigmoid(z[:, COL_GB:N_IN_PAD])


def _in_stage(seq_mode, x, sc, sh, zprev, p):
    b, t, d = x.shape
    tm = 256 if seq_mode else t
    tmod = 1 if seq_mode else tm
    grid = (b, t // tm)
    tok = lambda n: pl.BlockSpec((1, tm, n), lambda i, j: (i, j, 0))
    mod = pl.BlockSpec((1, tmod, d), lambda i, j: (i, j if not seq_mode else 0, 0))
    zp_spec = (pl.BlockSpec((1, 1, N_SHIFT_PAD), lambda i, j: (i, 0, 0)) if seq_mode
               else tok(N_SHIFT_PAD))
    hl_spec = (pl.BlockSpec((1, 1, d), lambda i, j: (i, 0, 0)) if seq_mode else tok(d))
    hl_shape = (b, 1, d) if seq_mode else (b, t, d)
    out_cols = [D_A] * 6 + [D_B] * 2 + [d] * 2
    est = (2 * tm * (d + N_SHIFT_PAD + sum(out_cols) + d) * 4 + 2 * d * N_IN_PAD * 2
           + 3 * tm * N_IN_PAD * 4 + 4 * 2**20)
    outs = pl.pallas_call(
        functools.partial(_in_body, seq_mode, tm),
        grid=grid,
        in_specs=[tok(d), mod, mod,
                  _const_spec((d, N_IN_PAD), True), _const_spec((1, N_SHIFT_PAD)), zp_spec,
                  _const_spec((LANES, 2 * D_A)), _const_spec((1, 2 * D_A)),
                  _const_spec((1, D_A)), _const_spec((1, D_A)),
                  _const_spec((1, D_B)), _const_spec((1, D_B)),
                  _const_spec((D_A, D_A))],
        out_specs=[tok(n) for n in out_cols] + [hl_spec],
        out_shape=[jax.ShapeDtypeStruct((b, t, n), F32) for n in out_cols]
        + [jax.ShapeDtypeStruct(hl_shape, F32)],
        scratch_shapes=[pltpu.VMEM((1, N_SHIFT_PAD), F32)],
        compiler_params=_params(("arbitrary", "arbitrary"), est),
        name="in_stage_seq" if seq_mode else "in_stage_row",
    )(x, sc, sh, p["w_in"], p["mu"], zprev, p["lora"], p["w0a0"], p["k_k"], p["k_a"],
      p["lnv_g"], p["lnv_b"], p["bd"])
    return outs


_WKV_3PASS = frozenset({"ark", "arkv", "h"})


def _wmm(name, a, b, dims):
    return (_mm3 if name in _WKV_3PASS else _mm1)(a, b, dims)


def _chunk_heads(x, nsub):
    c = WKV_CHUNK
    return jnp.stack([x[c * i:c * (i + 1), HEAD * h:HEAD * (h + 1)]
                      for i in range(nsub) for h in range(N_HEADS)], axis=0)


def _masked_rowsum(mask_bf, x):
    h1, h2 = _split2(x)
    return _dot(mask_bf, h1) + _dot(mask_bf, h2)


def _wkv_seq_body(nsub, r_ref, lw_ref, kp_ref, v_ref, kap_ref, bb_ref, s0_ref,
                  o_ref, sout_ref, s_scr):
    c = WKV_CHUNK
    tc = nsub * c

    @pl.when(pl.program_id(1) == 0)
    def _():
        s_scr[...] = s0_ref[0]

    row = lax.broadcasted_iota(jnp.int32, (tc, tc), 0)
    col = lax.broadcasted_iota(jnp.int32, (tc, tc), 1)
    shift = c.bit_length() - 1
    same_chunk = (row >> shift) == (col >> shift)
    lw = lw_ref[0]
    g = _masked_rowsum((same_chunk & (row >= col)).astype(BF16), lw)
    g_end = _masked_rowsum(same_chunk.astype(BF16), lw)
    e_neg = jnp.exp(-g)
    e_end = jnp.exp(g_end - g)
    ch = functools.partial(_chunk_heads, nsub=nsub)
    k = kp_ref[0]
    b = bb_ref[0]
    kap_t = ch(kap_ref[0] * jnp.exp(g - lw))
    b_t = ch(b * e_neg)
    k_t = ch(k * e_neg)
    r_t = ch(r_ref[0] * jnp.exp(g))
    b_e = ch(b * e_end)
    k_e = ch(k * e_end)
    vv = ch(v_ref[0])
    decay_end = ch(jnp.exp(g_end))

    r64 = lax.broadcasted_iota(jnp.int32, (c, c), 0)
    c64 = lax.broadcasted_iota(jnp.int32, (c, c), 1)
    tri_incl = (r64 >= c64)[None]
    tri_strict = (r64 > c64)[None]
    eye = (r64 == c64)[None]

    l_b = jnp.where(tri_strict, _wmm("lb", kap_t, b_t, _B_NT), 0.0)
    l_k = jnp.where(tri_strict, _wmm("lk", kap_t, k_t, _B_NT), 0.0)
    a_rb = jnp.where(tri_incl, _wmm("arb", r_t, b_t, _B_NT), 0.0)
    a_rk = jnp.where(tri_incl, _wmm("ark", r_t, k_t, _B_NT), 0.0)
    m = -l_b
    t_inv = jnp.where(eye, 1.0, 0.0) + m
    m = _wmm("inv", m, m, _B_NN)
    span = 2
    while 2 * span < c:
        both = _wmm("inv", jnp.concatenate([m, t_inv], axis=1), m, _B_NN)
        m = both[:, :c]
        t_inv = t_inv + both[:, c:]
        span *= 2
    t_inv = t_inv + _wmm("inv", t_inv, m, _B_NN)
    a1 = -_wmm("ta", t_inv, kap_t, _B_NN)
    u0 = -_wmm("tu", t_inv, _wmm("lkv", l_k, vv, _B_NN), _B_NN)
    a2 = r_t + _wmm("arba", a_rb, a1, _B_NN)
    o0 = _wmm("arbu", a_rb, u0, _B_NN) + _wmm("arkv", a_rk, vv, _B_NN)
    g_mat = jnp.where(eye, decay_end, 0.0) + _wmm("g", a1, b_e, _B_TN)
    h_mat = _wmm("h", jnp.concatenate([u0, vv], axis=1),
                 jnp.concatenate([b_e, k_e], axis=1), _B_TN)

    s = s_scr[...]
    for i in range(nsub):
        hs = slice(N_HEADS * i, N_HEADS * (i + 1))
        o = _wmm("o", a2[hs], s, _B_NT) + o0[hs]
        s = _wmm("s", s, g_mat[hs], _B_NN) + h_mat[hs]
        for h in range(N_HEADS):
            o_ref[0, c * i:c * (i + 1), HEAD * h:HEAD * (h + 1)] = o[h]
    s_scr[...] = s

    @pl.when(pl.program_id(1) == pl.num_programs(1) - 1)
    def _():
        sout_ref[0] = s


def _wkv_seq(r, lw, kp, v, kap, bb, s0):
    b, t, _ = r.shape
    tc = 256
    nsub = tc // WKV_CHUNK
    tok = pl.BlockSpec((1, tc, D_A), lambda i, j: (i, j, 0))
    st = pl.BlockSpec((1, N_HEADS, HEAD, HEAD), lambda i, j: (i, 0, 0, 0))
    return pl.pallas_call(
        functools.partial(_wkv_seq_body, nsub),
        grid=(b, t // tc),
        in_specs=[tok] * 6 + [st],
        out_specs=[tok, st],
        out_shape=[jax.ShapeDtypeStruct((b, t, D_A), F32),
                   jax.ShapeDtypeStruct((b, N_HEADS, HEAD, HEAD), F32)],
        scratch_shapes=[pltpu.VMEM((N_HEADS, HEAD, HEAD), F32)],
        compiler_params=_params(("arbitrary", "arbitrary"), 32 * 2**20),
        name="wkv_seq",
    )(r, lw, kp, v, kap, bb, s0)


def _wkv_step_body(r_ref, lw_ref, kp_ref, v_ref, kap_ref, bb_ref, s0_ref, o_ref, sout_ref):
    eye = (lax.broadcasted_iota(jnp.int32, (HEAD, HEAD), 0)
           == lax.broadcasted_iota(jnp.int32, (HEAD, HEAD), 1))[None]
    for h in range(N_HEADS):
        hs = slice(HEAD * h, HEAD * (h + 1))
        row = lambda ref: ref[:, hs][:, None, :]
        s = s0_ref[:, h]
        sa = -jnp.sum(s * row(kap_ref), axis=-1, keepdims=True)
        v_col = jnp.sum(jnp.where(eye, row(v_ref), 0.0), axis=-1, keepdims=True)
        s_new = s * jnp.exp(row(lw_ref)) + sa * row(bb_ref) + v_col * row(kp_ref)
        sout_ref[:, h] = s_new
        o_col = jnp.sum(s_new * row(r_ref), axis=-1, keepdims=True)
        o_ref[:, hs] = jnp.sum(jnp.where(eye, o_col, 0.0), axis=1)


def _wkv_step(r, lw, kp, v, kap, bb, s0):
    m = r.shape[0]
    nb = 8
    tok = pl.BlockSpec((nb, D_A), lambda i: (i, 0))
    st = pl.BlockSpec((nb, N_HEADS, HEAD, HEAD), lambda i: (i, 0, 0, 0))
    return pl.pallas_call(
        _wkv_step_body,
        grid=(m // nb,),
        in_specs=[tok] * 6 + [st],
        out_specs=[tok, st],
        out_shape=[jax.ShapeDtypeStruct((m, D_A), F32),
                   jax.ShapeDtypeStruct((m, N_HEADS, HEAD, HEAD), F32)],
        compiler_params=_params(("arbitrary",), 24 * 2**20),
        name="wkv_step",
    )(r, lw, kp, v, kap, bb, s0)


def _merge_body(seq_mode, tm, x_ref, o_ref, r_ref, kp_ref, v_ref, u_ref, vg_ref,
                ga_ref, gb_ref, g1_ref, sc2_ref, sh2_ref, lxg_ref, lxb_ref, rk_ref,
                bd_ref, ws_ref, bs_ref, pa_ref, pb_ref, wo_ref, l1g_ref, l1b_ref,
                x1_ref, h2_ref):
    bd = bd_ref[...]
    o = o_ref[0]
    inv_n = 1.0 / HEAD
    mu = _segsum(o, bd) * inv_n
    d = o - mu
    var = _segsum(d * d, bd) * inv_n
    on = d * lax.rsqrt(var + GN_EPS) * lxg_ref[...] + lxb_ref[...]
    v = v_ref[0]
    o_a = on + _segsum(r_ref[0] * kp_ref[0] * rk_ref[...], bd) * v
    vg = vg_ref[0]
    if seq_mode:
        row = lax.broadcasted_iota(jnp.int32, (CHUNK, CHUNK), 0)
        col = lax.broadcasted_iota(jnp.int32, (CHUNK, CHUNK), 1)
        lane = lax.broadcasted_iota(jnp.int32, (CHUNK, D_B), 1)
        gsz = D_B // N_GROUPS_B
        pieces = []
        for j in range(tm // CHUNK):
            vc = vg[j * CHUNK:(j + 1) * CHUNK, :]
            acc = bs_ref[...]
            for g in range(N_GROUPS_B):
                wg = jnp.where(row >= col, ws_ref[g], 0.0)
                vm = jnp.where((lane >= g * gsz) & (lane < (g + 1) * gsz), vc, 0.0)
                acc = acc + _mm3(wg, vm)
            pieces.append(acc)
        s = jnp.concatenate(pieces, axis=0) if len(pieces) > 1 else pieces[0]
    else:
        s = vg * ws_ref[...] + bs_ref[...]
    o_b = u_ref[0] * s
    y = _mm1(ga_ref[0] * _mm1(o_a, pa_ref[...]) + gb_ref[0] * _mm1(o_b, pb_ref[...]),
             wo_ref[...])
    x1 = _layer_norm(ALPHA * x_ref[0] + g1_ref[0] * y, l1g_ref[...], l1b_ref[...])
    x1_ref[0] = x1
    h2_ref[0] = x1 * (1.0 + sc2_ref[0]) + sh2_ref[0]


def _merge_stage(seq_mode, x, o, r, kp, v, u, vg, ga, gb, g1, sc2, sh2, p):
    b, t, d = x.shape
    tm = 256 if seq_mode else t
    tmod = 1 if seq_mode else tm
    tok = lambda n: pl.BlockSpec((1, tm, n), lambda i, j: (i, j, 0))
    mod = pl.BlockSpec((1, tmod, d), lambda i, j: (i, j if not seq_mode else 0, 0))
    ws, bs = (p["w_spatial"], p["b_spatial_full"]) if seq_mode else (p["ws_row"], p["bs_row"])
    est = 2 * tm * (2 * d + 7 * D_A + 2 * d + 2 * d) * 4 + 8 * tm * d * 4 + 16 * 2**20
    return pl.pallas_call(
        functools.partial(_merge_body, seq_mode, tm),
        grid=(b, t // tm),
        in_specs=[tok(d)] + [tok(D_A)] * 6 + [tok(d), tok(d), mod, mod, mod,
                  _const_spec((1, D_A)), _const_spec((1, D_A)), _const_spec((1, D_A)),
                  _const_spec((D_A, D_A)), _const_spec(ws.shape), _const_spec(bs.shape),
                  _const_spec((D_A, d)), _const_spec((D_B, d)), _const_spec((d, d)),
                  _const_spec((1, d)), _const_spec((1, d))],
        out_specs=[tok(d), tok(d)],
        out_shape=[jax.ShapeDtypeStruct((b, t, d), F32)] * 2,
        compiler_params=_params(("arbitrary", "arbitrary"), est),
        name="merge_seq" if seq_mode else "merge_row",
    )(x, o, r, kp, v, u, vg, ga, gb, g1, sc2, sh2, p["lnx_g"], p["lnx_b"], p["r_k"],
      p["bd"], ws, bs, p["w_branch_a"], p["w_branch_b"], p["w_out"], p["ln1_g"], p["ln1_b"])


def _route_body(h_ref, w_ref, b_ref, gate_ref, idx_ref, wts_ref, cnt_ref, base_ref):
    @pl.when(pl.program_id(0) == 0)
    def _():
        base_ref[...] = jnp.zeros_like(base_ref)

    lg = _mm3(h_ref[...], w_ref[...]) + b_ref[...]
    lane = lax.broadcasted_iota(jnp.int32, lg.shape, 1)
    lanef = lane.astype(F32)
    neg = -jnp.inf
    is_g = (lane >= N_EXPERTS) & (lane < N_EXPERTS + N_ROUTE_GROUPS)
    mg = jnp.max(jnp.where(is_g, lg, neg), axis=-1, keepdims=True)
    gidx = jnp.min(jnp.where(is_g & (lg == mg), lanef - N_EXPERTS, 1e9), axis=-1, keepdims=True)
    pg_sel = 1.0 / jnp.sum(jnp.where(is_g, jnp.exp(lg - mg), 0.0), axis=-1, keepdims=True)
    lo = gidx * EXP_PER_GROUP
    in_grp = (lanef >= lo) & (lanef < lo + EXP_PER_GROUP)
    t1 = jnp.max(jnp.where(in_grp, lg, neg), axis=-1, keepdims=True)
    i1 = jnp.min(jnp.where(in_grp & (lg == t1), lanef, 1e9), axis=-1, keepdims=True)
    rest = in_grp & (lanef != i1)
    t2 = jnp.max(jnp.where(rest, lg, neg), axis=-1, keepdims=True)
    i2 = jnp.min(jnp.where(rest & (lg == t2), lanef, 1e9), axis=-1, keepdims=True)
    e2 = jnp.exp(t2 - t1)
    w1 = pg_sel / (1.0 + e2)
    w2 = pg_sel * e2 / (1.0 + e2)
    gate_ref[...] = jnp.where(lanef == i1, w1, 0.0) + jnp.where(lanef == i2, w2, 0.0)
    tm = lg.shape[0]
    hit = (lanef == i1) | (lanef == i2)
    earlier = (lax.broadcasted_iota(jnp.int32, (tm, tm), 0)
               > lax.broadcasted_iota(jnp.int32, (tm, tm), 1))
    before = base_ref[...] + _dot(earlier.astype(BF16), hit.astype(BF16))
    rank1 = jnp.sum(jnp.where(lanef == i1, before, 0.0), axis=-1, keepdims=True)
    rank2 = jnp.sum(jnp.where(lanef == i2, before, 0.0), axis=-1, keepdims=True)
    base_ref[...] += jnp.sum(hit.astype(F32), axis=0, keepdims=True)
    cnt_ref[...] = base_ref[...].astype(jnp.int32)
    idx = jnp.where(lane == 0, i1, jnp.where(lane == 1, i2, jnp.where(lane == 2, rank1, rank2)))
    idx_ref[...] = idx.astype(jnp.int32)
    wts_ref[...] = jnp.where(lane == 0, w1, w2)


def _route(h2, p):
    m, d = h2.shape
    tm = min(m, 512)
    tok = pl.BlockSpec((tm, LANES), lambda i: (i, 0))
    return pl.pallas_call(
        _route_body,
        grid=(m // tm,),
        in_specs=[pl.BlockSpec((tm, d), lambda i: (i, 0)),
                  _const_spec((d, LANES)), _const_spec((1, LANES))],
        out_specs=[tok, tok, tok, _const_spec((1, LANES))],
        out_shape=[jax.ShapeDtypeStruct((m, LANES), F32),
                   jax.ShapeDtypeStruct((m, LANES), jnp.int32),
                   jax.ShapeDtypeStruct((m, LANES), F32),
                   jax.ShapeDtypeStruct((1, LANES), jnp.int32)],
        scratch_shapes=[pltpu.VMEM((1, LANES), F32)],
        compiler_params=_params(("arbitrary",), 24 * 2**20),
        name="route",
    )(h2, p["w_route"], p["b_route"])


def _moe_body(h_ref, gate_ref, wg_ref, wu_ref, wd_ref, x1_ref, g2_ref, l2g_ref, l2b_ref,
              out_ref, acc_ref, xb_ref):
    e = pl.program_id(1)

    @pl.when(e == 0)
    def _():
        acc_ref[...] = jnp.zeros_like(acc_ref)
        xb_ref[...] = h_ref[...].astype(BF16)

    xb = xb_ref[...]
    pre = _dot(xb, wg_ref[0])
    hid = pre * jax.nn.sigmoid(pre) * _dot(xb, wu_ref[0])
    ye = _mm1(hid, wd_ref[0])
    gate = gate_ref[...]
    lane = lax.broadcasted_iota(jnp.int32, gate.shape, 1)
    ge = jnp.sum(jnp.where(lane == e, gate, 0.0), axis=-1, keepdims=True)
    acc_ref[...] += ge * ye

    @pl.when(e == pl.num_programs(1) - 1)
    def _():
        out_ref[...] = _layer_norm(ALPHA * x1_ref[...] + g2_ref[0] * acc_ref[...],
                                   l2g_ref[...], l2b_ref[...])


def _moe(seq_len, h2, gate, x1, g2, p):
    m, d = h2.shape
    tm = min(seq_len if g2.shape[1] == 1 else m, 1024)
    tok = lambda n: pl.BlockSpec((tm, n), lambda i, e: (i, 0))
    if g2.shape[1] == 1:
        tiles_per_seq = seq_len // tm
        g2_spec = pl.BlockSpec((1, 1, d), lambda i, e: (i // tiles_per_seq, 0, 0))
    else:
        g2_spec = pl.BlockSpec((1, tm, d), lambda i, e: (0, i, 0))
    est = 2 * tm * (3 * d + LANES) * 4 + tm * d * 6 + 4 * tm * d * 4 + 8 * 2**20
    return pl.pallas_call(
        _moe_body,
        grid=(m // tm, N_EXPERTS),
        in_specs=[tok(d), tok(LANES),
                  pl.BlockSpec((1, d, D_EXPERT), lambda i, e: (e, 0, 0)),
                  pl.BlockSpec((1, d, D_EXPERT), lambda i, e: (e, 0, 0)),
                  pl.BlockSpec((1, D_EXPERT, d), lambda i, e: (e, 0, 0)),
                  tok(d), g2_spec, _const_spec((1, d)), _const_spec((1, d))],
        out_specs=tok(d),
        out_shape=jax.ShapeDtypeStruct((m, d), F32),
        scratch_shapes=[pltpu.VMEM((tm, d), F32), pltpu.VMEM((tm, d), BF16)],
        compiler_params=_params(("arbitrary", "arbitrary"), est),
        name="moe_dense",
    )(h2, gate, p["w_exp_gate"], p["w_exp_up"], p["w_exp_down"], x1, g2, p["ln2_g"], p["ln2_b"])


MOE_BLOCK = 256
MOE_TOK_TILE = 256


def _dispatch_body(tail_ref, pos_ref, h_hbm, xs_hbm, zero_buf, sem, zsem):
    step = pl.program_id(0)
    nt = MOE_TOK_TILE
    nb_max = xs_hbm.shape[0] // MOE_BLOCK

    def zero_copy(row0):
        return pltpu.make_async_copy(
            zero_buf, xs_hbm.at[pl.ds(pl.multiple_of(row0, MOE_BLOCK), MOE_BLOCK)], zsem)

    @pl.when(step == 0)
    def _():
        zero_buf[...] = jnp.zeros_like(zero_buf)
        n_used = tail_ref[N_EXPERTS]

        def each_zero_copy(fn):
            for e in range(N_EXPERTS):
                @pl.when(tail_ref[e] >= 0)
                def _():
                    fn(zero_copy(tail_ref[e]))

            def unused(j, c):
                fn(zero_copy(j * MOE_BLOCK))
                return c

            lax.fori_loop(n_used, nb_max, unused, 0)

        each_zero_copy(lambda cp: cp.start())
        each_zero_copy(lambda cp: cp.wait())

    def row_copy(t, k):
        src = h_hbm.at[pl.ds(step * nt + t, 1)]
        return pltpu.make_async_copy(src, xs_hbm.at[pl.ds(pos_ref[0, 0, 2 * t + k], 1)], sem)

    def issue(t, c):
        row_copy(t, 0).start()
        row_copy(t, 1).start()
        return c

    def drain(t, c):
        row_copy(t, 0).wait()
        row_copy(t, 1).wait()
        return c

    lax.fori_loop(0, nt, issue, 0)
    lax.fori_loop(0, nt, drain, 0)


def _dispatch(h2, pos, tail_start, n_rows):
    m, d = h2.shape
    nt = MOE_TOK_TILE
    grid_spec = pltpu.PrefetchScalarGridSpec(
        num_scalar_prefetch=1,
        grid=(m // nt,),
        in_specs=[pl.BlockSpec((1, 1, 2 * nt), lambda i, tail: (i, 0, 0),
                               memory_space=pltpu.SMEM),
                  pl.BlockSpec(memory_space=pl.ANY)],
        out_specs=pl.BlockSpec(memory_space=pl.ANY),
        scratch_shapes=[pltpu.VMEM((MOE_BLOCK, d), F32),
                        pltpu.SemaphoreType.DMA, pltpu.SemaphoreType.DMA],
    )
    return pl.pallas_call(
        _dispatch_body,
        grid_spec=grid_spec,
        out_shape=jax.ShapeDtypeStruct((n_rows, d), F32),
        compiler_params=_params(("arbitrary",), 16 * 2**20),
        name="moe_dispatch",
    )(tail_start, pos.reshape(m // nt, 1, 2 * nt), h2)


def _gmm_body(be_ref, nb_ref, x_ref, wg_ref, wu_ref, wd_ref, y_ref):
    j = pl.program_id(0)

    @pl.when(j < nb_ref[0])
    def _():
        xb = x_ref[...].astype(BF16)
        pre = _dot(xb, wg_ref[0])
        hid = pre * jax.nn.sigmoid(pre) * _dot(xb, wu_ref[0])
        y_ref[...] = _mm1(hid, wd_ref[0])

    @pl.when(j >= nb_ref[0])
    def _():
        y_ref[...] = jnp.zeros_like(y_ref)


def _gmm(xs, blk_expert, n_blocks, p):
    n_rows, d = xs.shape
    nb_max = n_rows // MOE_BLOCK
    live = lambda j, be, nb: jnp.minimum(j, nb[0] - 1)
    wspec = lambda shape: pl.BlockSpec(shape, lambda j, be, nb: (be[live(j, be, nb)], 0, 0))
    grid_spec = pltpu.PrefetchScalarGridSpec(
        num_scalar_prefetch=2,
        grid=(nb_max,),
        in_specs=[pl.BlockSpec((MOE_BLOCK, d), lambda j, be, nb: (live(j, be, nb), 0)),
                  wspec((1, d, D_EXPERT)), wspec((1, d, D_EXPERT)), wspec((1, D_EXPERT, d))],
        out_specs=pl.BlockSpec((MOE_BLOCK, d), lambda j, be, nb: (j, 0)),
    )
    return pl.pallas_call(
        _gmm_body,
        grid_spec=grid_spec,
        out_shape=jax.ShapeDtypeStruct((n_rows, d), F32),
        compiler_params=_params(("arbitrary",), 24 * 2**20),
        name="moe_gmm",
    )(blk_expert, n_blocks, xs, p["w_exp_gate"], p["w_exp_up"], p["w_exp_down"])


def _combine_body(pos_ref, ys_hbm, wts_ref, x1_ref, g2_ref, l2g_ref, l2b_ref, out_ref,
                  buf, sem):
    nt = MOE_TOK_TILE

    def row_copy(t, k):
        return pltpu.make_async_copy(ys_hbm.at[pl.ds(pos_ref[0, 0, 2 * t + k], 1)],
                                     buf.at[k, pl.ds(t, 1)], sem)

    def issue(t, c):
        row_copy(t, 0).start()
        row_copy(t, 1).start()
        return c

    def drain(t, c):
        row_copy(t, 0).wait()
        row_copy(t, 1).wait()
        return c

    lax.fori_loop(0, nt, issue, 0)
    lax.fori_loop(0, nt, drain, 0)
    w = wts_ref[...]
    moe = w[:, 0:1] * buf[0] + w[:, 1:2] * buf[1]
    out_ref[...] = _layer_norm(ALPHA * x1_ref[...] + g2_ref[0] * moe, l2g_ref[...], l2b_ref[...])


def _combine(seq_len, ys, pos, wts, x1, g2, p):
    m, d = x1.shape
    nt = MOE_TOK_TILE
    tiles_per_seq = seq_len // nt
    tok = lambda n: pl.BlockSpec((nt, n), lambda i: (i, 0))
    return pl.pallas_call(
        _combine_body,
        grid=(m // nt,),
        in_specs=[pl.BlockSpec((1, 1, 2 * nt), lambda i: (i, 0, 0), memory_space=pltpu.SMEM),
                  pl.BlockSpec(memory_space=pl.ANY), tok(LANES), tok(d),
                  pl.BlockSpec((1, 1, d), lambda i: (i // tiles_per_seq, 0, 0)),
                  _const_spec((1, d)), _const_spec((1, d))],
        out_specs=tok(d),
        out_shape=jax.ShapeDtypeStruct((m, d), F32),
        scratch_shapes=[pltpu.VMEM((2, nt, d), F32), pltpu.SemaphoreType.DMA],
        compiler_params=_params(("arbitrary",), 24 * 2**20),
        name="moe_combine",
    )(pos.reshape(m // nt, 1, 2 * nt), ys, wts, x1, g2, p["ln2_g"], p["ln2_b"])


def _moe_routed(seq_len, h2, idx, wts, counts, x1, g2, p):
    m, d = h2.shape
    blk = MOE_BLOCK
    nb_max = -(-(2 * m + N_EXPERTS * (blk - 1)) // blk)
    cnt = counts[0, :N_EXPERTS]
    padded = ((cnt + blk - 1) // blk) * blk
    ends = jnp.cumsum(padded)
    starts = ends - padded
    n_blocks = (ends[-1] // blk).astype(jnp.int32).reshape(1)
    first_row = jnp.arange(nb_max, dtype=jnp.int32) * blk
    blk_expert = jnp.minimum(
        jnp.sum((first_row[:, None] >= ends[None, :]).astype(jnp.int32), axis=1),
        N_EXPERTS - 1).astype(jnp.int32)
    tail_start = jnp.concatenate(
        [jnp.where(padded > 0, ends - blk, -1).astype(jnp.int32), n_blocks])
    pos = (starts[idx[:, 0:2]] + idx[:, 2:4]).astype(jnp.int32)
    xs = _dispatch(h2, pos, tail_start, nb_max * blk)
    ys = _gmm(xs, blk_expert, n_blocks, p)
    return _combine(seq_len, ys, pos, wts, x1, g2, p)


def _prep_layer(l, w_in, mu_shift, w0, w_decay_up, a0, w_iclr_up, k_k, k_a, r_k,
                lnx_g, lnx_b, lnv_g, lnv_b, w_spatial, b_spatial, w_branch_a, w_branch_b,
                w_out, ln1_g, ln1_b, w_route_group, b_route_group, w_route_expert,
                b_route_expert, w_exp_gate, w_exp_up, w_exp_down, ln2_g, ln2_b):
    d = D_MODEL
    pad_a = N_SHIFT_PAD - N_SHIFT
    wi = w_in[l]
    w_in_p = jnp.concatenate(
        [wi[:, :N_SHIFT], jnp.zeros((d, pad_a), F32), wi[:, N_SHIFT:]], axis=1).astype(BF16)
    mu = jnp.concatenate([mu_shift[l], jnp.zeros((pad_a,), F32)])[None]
    lora = jnp.zeros((LANES, 2 * D_A), F32)
    lora = lora.at[:R_LORA, :D_A].set(w_decay_up[l]).at[R_LORA:2 * R_LORA, D_A:].set(w_iclr_up[l])
    seg = jnp.arange(D_A) // HEAD
    row1 = lambda x: x.reshape(1, -1)
    gsz = D_B // N_GROUPS_B
    w_route = jnp.concatenate(
        [w_route_expert[l], w_route_group[l],
         jnp.zeros((d, LANES - N_EXPERTS - N_ROUTE_GROUPS), F32)], axis=1)
    b_route = jnp.concatenate(
        [b_route_expert[l], b_route_group[l],
         jnp.zeros((LANES - N_EXPERTS - N_ROUTE_GROUPS,), F32)])[None]
    return dict(
        w_in=w_in_p, mu=mu, lora=lora,
        w0a0=jnp.concatenate([w0[l], a0[l]])[None],
        k_k=row1(k_k[l]), k_a=row1(k_a[l]), r_k=row1(r_k[l]),
        lnx_g=row1(lnx_g[l]), lnx_b=row1(lnx_b[l]),
        lnv_g=row1(lnv_g[l]), lnv_b=row1(lnv_b[l]),
        bd=(seg[:, None] == seg[None, :]).astype(BF16),
        w_spatial=w_spatial[l],
        b_spatial_full=jnp.repeat(b_spatial[l].T, gsz, axis=1),
        ws_row=jnp.repeat(w_spatial[l][:, 0, 0], gsz)[None],
        bs_row=jnp.repeat(b_spatial[l][:, 0], gsz)[None],
        w_branch_a=w_branch_a[l].astype(BF16), w_branch_b=w_branch_b[l].astype(BF16),
        w_out=w_out[l].astype(BF16), ln1_g=row1(ln1_g[l]), ln1_b=row1(ln1_b[l]),
        w_route=w_route, b_route=b_route,
        w_exp_gate=w_exp_gate[l].astype(BF16), w_exp_up=w_exp_up[l].astype(BF16),
        w_exp_down=w_exp_down[l].astype(BF16), ln2_g=row1(ln2_g[l]), ln2_b=row1(ln2_b[l]),
    )


def _trunk(seq_mode, x, mods, wkv_in, shift_in, preps):
    b, t, d = x.shape
    wkv_out, shift_out, v_out = [], [], []
    for l in range(DEPTH):
        p = preps[l]
        sh1, sc1, g1, sh2, sc2, g2 = mods[l]
        if seq_mode:
            zprev = jnp.zeros((b, 1, N_SHIFT_PAD), F32) if shift_in is None else shift_in[l]
        else:
            zprev = _matmul(shift_in[l], p["w_in"][:, :N_SHIFT_PAD]).reshape(b, t, N_SHIFT_PAD)
        r, lw, kp, v, kap, bb, u, vg, ga, gb, hl = _in_stage(seq_mode, x, sc1, sh1, zprev, p)
        if seq_mode:
            o, s_new = _wkv_seq(r, lw, kp, v, kap, bb, wkv_in[l])
        else:
            flat = lambda a: a.reshape(t, D_A)
            o, s_new = _wkv_step(flat(r), flat(lw), flat(kp), flat(v), flat(kap), flat(bb),
                                 wkv_in[l])
            o = o.reshape(b, t, D_A)
        x1, h2 = _merge_stage(seq_mode, x, o, r, kp, v, u, vg, ga, gb, g1, sc2, sh2, p)
        m = b * t
        h2f = h2.reshape(m, d)
        gate, idx, wts, counts = _route(h2f, p)
        if seq_mode:
            x = _moe_routed(t, h2f, idx, wts, counts, x1.reshape(m, d), g2, p)
        else:
            x = _moe(t, h2f, gate, x1.reshape(m, d), g2, p)
        x = x.reshape(b, t, d)
        wkv_out.append(s_new)
        shift_out.append(hl)
        v_out.append(vg)
    return x, wkv_out, shift_out, v_out


def kernel(x_prompt, x_sample, c_prompt, c_sample, state_wkv, state_shift, w_ada, b_ada, w_in, mu_shift, w0, w_decay_up, a0, w_iclr_up, k_k, k_a, r_k, lnx_g, lnx_b, lnv_g, lnv_b, w_spatial, b_spatial, w_branch_a, w_branch_b, w_out, ln1_g, ln1_b, w_route_group, b_route_group, w_route_expert, b_route_expert, w_exp_gate, w_exp_up, w_exp_down, ln2_g, ln2_b):
    bp, tp, d = x_prompt.shape
    bs = x_sample.shape[0]
    layer_params = (w_in, mu_shift, w0, w_decay_up, a0, w_iclr_up, k_k, k_a, r_k, lnx_g,
                    lnx_b, lnv_g, lnv_b, w_spatial, b_spatial, w_branch_a, w_branch_b, w_out,
                    ln1_g, ln1_b, w_route_group, b_route_group, w_route_expert,
                    b_route_expert, w_exp_gate, w_exp_up, w_exp_down, ln2_g, ln2_b)
    preps = [_prep_layer(l, *layer_params) for l in range(DEPTH)]
    mod_all = _ada(jnp.concatenate([c_prompt, c_sample], axis=0), w_ada, b_ada)
    mods_p, mods_s = [], []
    for l in range(DEPTH):
        parts = jnp.split(mod_all[l], 6, axis=-1)
        mods_p.append([q[:bp].reshape(bp, 1, d) for q in parts])
        mods_s.append([q[bp:].reshape(1, bs, d) for q in parts])

    wkv0 = jnp.zeros((DEPTH, bp, N_HEADS, HEAD, HEAD), F32)
    y_p, wkv_p, shift_p, _ = _trunk(True, x_prompt, mods_p, wkv0, None, preps)
    y_s, wkv_s, shift_s, v_s = _trunk(False, x_sample.reshape(1, bs, d), mods_s, state_wkv,
                                      state_shift, preps)
    return (y_p,
            y_s.reshape(bs, 1, d),
            jnp.stack(wkv_p),
            jnp.stack([s.reshape(bp, d) for s in shift_p]),
            jnp.stack(wkv_s),
            jnp.stack([s.reshape(bs, d) for s in shift_s]),
            jnp.stack([q.reshape(bs, 1, D_B) for q in v_s]))
```

```python
import functools

import jax
import jax.numpy as jnp
from jax import lax
from jax.experimental import pallas as pl
from jax.experimental.pallas import tpu as pltpu

F32 = jnp.float32
BF16 = jnp.bfloat16

D_MODEL = 1024
DEPTH = 2
HEAD = 64
N_HEADS = 8
D_A = N_HEADS * HEAD
R_LORA = 32
CHUNK = 128
N_GROUPS_B = 8
D_B = 512
N_SHIFT = 3 * D_A + 2 * R_LORA
N_ROUTE_GROUPS = 4
EXP_PER_GROUP = 8
N_EXPERTS = N_ROUTE_GROUPS * EXP_PER_GROUP
D_EXPERT = 256
ALPHA = (2 * DEPTH) ** 0.25
LN_EPS = 1e-5
GN_EPS = 64e-5

LANES = 128
N_SHIFT_PAD = 13 * LANES
COL_U = N_SHIFT_PAD
COL_VG = COL_U + D_B
COL_GA = COL_VG + D_B
COL_GB = COL_GA + D_MODEL
N_IN_PAD = COL_GB + D_MODEL
WKV_CHUNK = 64
VMEM_CAP_BYTES = 60000 * 1024

_NN = (((1,), (0,)), ((), ()))
_B_NT = (((2,), (2,)), ((0,), (0,)))
_B_NN = (((2,), (1,)), ((0,), (0,)))
_B_TN = (((1,), (1,)), ((0,), (0,)))


def _dot(a, b, dims=_NN):
    return lax.dot_general(a, b, dims, preferred_element_type=F32)


def _split2(x):
    hi = x.astype(BF16)
    lo = (x - hi.astype(F32)).astype(BF16)
    return hi, lo


def _mm1(a, b, dims=_NN):
    return _dot(a.astype(BF16), b.astype(BF16), dims)


def _mm3(a, b, dims=_NN):
    ah, al = _split2(a)
    bh, bl = _split2(b)
    return _dot(ah, bh, dims) + (_dot(ah, bl, dims) + _dot(al, bh, dims))


def _segsum(x, bd):
    hi, lo = _split2(x)
    return _dot(hi, bd) + _dot(lo, bd)


def _layer_norm(x, g, b):
    mu = jnp.mean(x, axis=-1, keepdims=True)
    d = x - mu
    var = jnp.mean(d * d, axis=-1, keepdims=True)
    return d * lax.rsqrt(var + LN_EPS) * g + b


def _gelu(x):
    return 0.5 * x * (1.0 + lax.erf(x * 0.7071067811865476))


def _params(sem, est_bytes):
    limit = int(min(VMEM_CAP_BYTES, max(est_bytes, 16 * 1024 * 1024)))
    return pltpu.CompilerParams(dimension_semantics=sem, vmem_limit_bytes=limit)


def _const_spec(shape, single_buffer=False):
    nd = len(shape)
    if single_buffer:
        return pl.BlockSpec(shape, lambda *_: (0,) * nd, pipeline_mode=pl.Buffered(1))
    return pl.BlockSpec(shape, lambda *_: (0,) * nd)


def _ada_body(c_ref, w_ref, b_ref, o_ref):
    c = c_ref[...]
    s = c * jax.nn.sigmoid(c)
    o_ref[0] = _mm3(s, w_ref[0]) + b_ref[0]


def _ada(c_all, w_ada, b_ada):
    depth, d, n6 = w_ada.shape
    m = c_all.shape[0]
    tn = 512
    return pl.pallas_call(
        _ada_body,
        grid=(depth, n6 // tn),
        in_specs=[
            pl.BlockSpec((m, d), lambda l, j: (0, 0)),
            pl.BlockSpec((1, d, tn), lambda l, j: (l, 0, j)),
            pl.BlockSpec((1, 1, tn), lambda l, j: (l, 0, j)),
        ],
        out_specs=pl.BlockSpec((1, m, tn), lambda l, j: (l, 0, j)),
        out_shape=jax.ShapeDtypeStruct((depth, m, n6), F32),
        compiler_params=_params(("arbitrary", "arbitrary"), 24 * 2**20),
        name="ada_mod",
    )(c_all, w_ada, b_ada.reshape(depth, 1, n6))


def _mm_body(x_ref, w_ref, o_ref):
    o_ref[...] = _mm1(x_ref[...], w_ref[...])


def _matmul(x, w):
    m, k = x.shape
    n = w.shape[1]
    return pl.pallas_call(
        _mm_body,
        grid=(1,),
        in_specs=[_const_spec((m, k)), _const_spec((k, n))],
        out_specs=_const_spec((m, n)),
        out_shape=jax.ShapeDtypeStruct((m, n), F32),
        compiler_params=_params(("arbitrary",), 24 * 2**20),
        name="shift_proj",
    )(x, w)


def _in_body(seq_mode, tm, x_ref, sc_ref, sh_ref, w_ref, mu_ref, zp_ref, lora_ref,
             w0a0_ref, kk_ref, ka_ref, lng_ref, lnb_ref, bd_ref,
             r_ref, lw_ref, kp_ref, v_ref, kap_ref, bb_ref, u_ref, vg_ref,
             ga_ref, gb_ref, hl_ref, carry_ref):
    h = x_ref[0] * (1.0 + sc_ref[0]) + sh_ref[0]
    if seq_mode:
        hl_ref[0] = h[tm - 1:tm, :]
    else:
        hl_ref[0] = h
    z = _dot(h.astype(BF16), w_ref[...])
    za = z[:, :N_SHIFT_PAD]
    if seq_mode:
        @pl.when(pl.program_id(1) == 0)
        def _():
            carry_ref[...] = zp_ref[0]

        row = lax.broadcasted_iota(jnp.int32, za.shape, 0)
        prev = jnp.where(row == 0, carry_ref[...], pltpu.roll(za, 1, 0))
        carry_ref[...] = za[tm - 1:tm, :]
    else:
        prev = zp_ref[0]
    mix = za + mu_ref[...] * (prev - za)
    r = mix[:, 0:D_A]
    k = mix[:, D_A:2 * D_A]
    v = mix[:, 2 * D_A:3 * D_A]
    xwa = mix[:, 3 * D_A:N_SHIFT_PAD]
    lane = lax.broadcasted_iota(jnp.int32, xwa.shape, 1)
    lora_in = jnp.where(lane < R_LORA, jnp.tanh(xwa), xwa)
    pre = w0a0_ref[...] + _mm3(lora_in, lora_ref[...])
    yw = -pre[:, :D_A]
    softplus = jnp.maximum(yw, 0.0) + jnp.log1p(jnp.exp(-jnp.abs(yw)))
    lw = -jnp.exp(-softplus - 0.5)
    a = jax.nn.sigmoid(pre[:, D_A:])
    bd = bd_ref[...]
    kk = k * kk_ref[...]
    kap = kk / jnp.maximum(jnp.sqrt(_segsum(kk * kk, bd)), 1e-12)
    r_ref[0] = r
    lw_ref[0] = lw
    kp_ref[0] = k * (1.0 + (a - 1.0) * ka_ref[...])
    v_ref[0] = v
    kap_ref[0] = kap
    bb_ref[0] = kap * a
    u_ref[0] = _gelu(z[:, COL_U:COL_VG])
    vg_ref[0] = _layer_norm(_gelu(z[:, COL_VG:COL_GA]), lng_ref[...], lnb_ref[...])
    ga_ref[0] = jax.nn.sigmoid(z[:, COL_GA:COL_GB])
    gb_ref[0] = jax.nn.sigmoid(z[:, COL_GB:N_IN_PAD])


def _in_stage(seq_mode, x, sc, sh, zprev, p):
    b, t, d = x.shape
    tm = 256 if seq_mode else t
    tmod = 1 if seq_mode else tm
    grid = (b, t // tm)
    tok = lambda n: pl.BlockSpec((1, tm, n), lambda i, j: (i, j, 0))
    mod = pl.BlockSpec((1, tmod, d), lambda i, j: (i, j if not seq_mode else 0, 0))
    zp_spec = (pl.BlockSpec((1, 1, N_SHIFT_PAD), lambda i, j: (i, 0, 0)) if seq_mode
               else tok(N_SHIFT_PAD))
    hl_spec = (pl.BlockSpec((1, 1, d), lambda i, j: (i, 0, 0)) if seq_mode else tok(d))
    hl_shape = (b, 1, d) if seq_mode else (b, t, d)
    out_cols = [D_A] * 6 + [D_B] * 2 + [d] * 2
    est = (2 * tm * (d + N_SHIFT_PAD + sum(out_cols) + d) * 4 + 2 * d * N_IN_PAD * 2
           + 3 * tm * N_IN_PAD * 4 + 4 * 2**20)
    outs = pl.pallas_call(
        functools.partial(_in_body, seq_mode, tm),
        grid=grid,
        in_specs=[tok(d), mod, mod,
                  _const_spec((d, N_IN_PAD), True), _const_spec((1, N_SHIFT_PAD)), zp_spec,
                  _const_spec((LANES, 2 * D_A)), _const_spec((1, 2 * D_A)),
                  _const_spec((1, D_A)), _const_spec((1, D_A)),
                  _const_spec((1, D_B)), _const_spec((1, D_B)),
                  _const_spec((D_A, D_A))],
        out_specs=[tok(n) for n in out_cols] + [hl_spec],
        out_shape=[jax.ShapeDtypeStruct((b, t, n), F32) for n in out_cols]
        + [jax.ShapeDtypeStruct(hl_shape, F32)],
        scratch_shapes=[pltpu.VMEM((1, N_SHIFT_PAD), F32)],
        compiler_params=_params(("arbitrary", "arbitrary"), est),
        name="in_stage_seq" if seq_mode else "in_stage_row",
    )(x, sc, sh, p["w_in"], p["mu"], zprev, p["lora"], p["w0a0"], p["k_k"], p["k_a"],
      p["lnv_g"], p["lnv_b"], p["bd"])
    return outs


_WKV_3PASS = frozenset({"ark", "arkv", "h"})


def _wmm(name, a, b, dims):
    return (_mm3 if name in _WKV_3PASS else _mm1)(a, b, dims)


def _chunk_heads(x, nsub):
    c = WKV_CHUNK
    return jnp.stack([x[c * i:c * (i + 1), HEAD * h:HEAD * (h + 1)]
                      for i in range(nsub) for h in range(N_HEADS)], axis=0)


def _masked_rowsum(mask_bf, x):
    h1, h2 = _split2(x)
    return _dot(mask_bf, h1) + _dot(mask_bf, h2)


def _wkv_seq_body(nsub, r_ref, lw_ref, kp_ref, v_ref, kap_ref, bb_ref, s0_ref,
                  o_ref, sout_ref, s_scr):
    c = WKV_CHUNK
    tc = nsub * c

    @pl.when(pl.program_id(1) == 0)
    def _():
        s_scr[...] = s0_ref[0]

    row = lax.broadcasted_iota(jnp.int32, (tc, tc), 0)
    col = lax.broadcasted_iota(jnp.int32, (tc, tc), 1)
    shift = c.bit_length() - 1
    same_chunk = (row >> shift) == (col >> shift)
    lw = lw_ref[0]
    g = _masked_rowsum((same_chunk & (row >= col)).astype(BF16), lw)
    g_end = _masked_rowsum(same_chunk.astype(BF16), lw)
    e_neg = jnp.exp(-g)
    e_end = jnp.exp(g_end - g)
    ch = functools.partial(_chunk_heads, nsub=nsub)
    k = kp_ref[0]
    b = bb_ref[0]
    kap_t = ch(kap_ref[0] * jnp.exp(g - lw))
    b_t = ch(b * e_neg)
    k_t = ch(k * e_neg)
    r_t = ch(r_ref[0] * jnp.exp(g))
    b_e = ch(b * e_end)
    k_e = ch(k * e_end)
    vv = ch(v_ref[0])
    decay_end = ch(jnp.exp(g_end))

    r64 = lax.broadcasted_iota(jnp.int32, (c, c), 0)
    c64 = lax.broadcasted_iota(jnp.int32, (c, c), 1)
    tri_incl = (r64 >= c64)[None]
    tri_strict = (r64 > c64)[None]
    eye = (r64 == c64)[None]

    l_b = jnp.where(tri_strict, _wmm("lb", kap_t, b_t, _B_NT), 0.0)
    l_k = jnp.where(tri_strict, _wmm("lk", kap_t, k_t, _B_NT), 0.0)
    a_rb = jnp.where(tri_incl, _wmm("arb", r_t, b_t, _B_NT), 0.0)
    a_rk = jnp.where(tri_incl, _wmm("ark", r_t, k_t, _B_NT), 0.0)
    m = -l_b
    t_inv = jnp.where(eye, 1.0, 0.0) + m
    m = _wmm("inv", m, m, _B_NN)
    span = 2
    while 2 * span < c:
        both = _wmm("inv", jnp.concatenate([m, t_inv], axis=1), m, _B_NN)
        m = both[:, :c]
        t_inv = t_inv + both[:, c:]
        span *= 2
    t_inv = t_inv + _wmm("inv", t_inv, m, _B_NN)
    a1 = -_wmm("ta", t_inv, kap_t, _B_NN)
    u0 = -_wmm("tu", t_inv, _wmm("lkv", l_k, vv, _B_NN), _B_NN)
    a2 = r_t + _wmm("arba", a_rb, a1, _B_NN)
    o0 = _wmm("arbu", a_rb, u0, _B_NN) + _wmm("arkv", a_rk, vv, _B_NN)
    g_mat = jnp.where(eye, decay_end, 0.0) + _wmm("g", a1, b_e, _B_TN)
    h_mat = _wmm("h", jnp.concatenate([u0, vv], axis=1),
                 jnp.concatenate([b_e, k_e], axis=1), _B_TN)

    s = s_scr[...]
    for i in range(nsub):
        hs = slice(N_HEADS * i, N_HEADS * (i + 1))
        o = _wmm("o", a2[hs], s, _B_NT) + o0[hs]
        s = _wmm("s", s, g_mat[hs], _B_NN) + h_mat[hs]
        for h in range(N_HEADS):
            o_ref[0, c * i:c * (i + 1), HEAD * h:HEAD * (h + 1)] = o[h]
    s_scr[...] = s

    @pl.when(pl.program_id(1) == pl.num_programs(1) - 1)
    def _():
        sout_ref[0] = s


def _wkv_seq(r, lw, kp, v, kap, bb, s0):
    b, t, _ = r.shape
    tc = 256
    nsub = tc // WKV_CHUNK
    tok = pl.BlockSpec((1, tc, D_A), lambda i, j: (i, j, 0))
    st = pl.BlockSpec((1, N_HEADS, HEAD, HEAD), lambda i, j: (i, 0, 0, 0))
    return pl.pallas_call(
        functools.partial(_wkv_seq_body, nsub),
        grid=(b, t // tc),
        in_specs=[tok] * 6 + [st],
        out_specs=[tok, st],
        out_shape=[jax.ShapeDtypeStruct((b, t, D_A), F32),
                   jax.ShapeDtypeStruct((b, N_HEADS, HEAD, HEAD), F32)],
        scratch_shapes=[pltpu.VMEM((N_HEADS, HEAD, HEAD), F32)],
        compiler_params=_params(("arbitrary", "arbitrary"), 32 * 2**20),
        name="wkv_seq",
    )(r, lw, kp, v, kap, bb, s0)


def _wkv_step_body(r_ref, lw_ref, kp_ref, v_ref, kap_ref, bb_ref, s0_ref, o_ref, sout_ref):
    eye = (lax.broadcasted_iota(jnp.int32, (HEAD, HEAD), 0)
           == lax.broadcasted_iota(jnp.int32, (HEAD, HEAD), 1))[None]
    for h in range(N_HEADS):
        hs = slice(HEAD * h, HEAD * (h + 1))
        row = lambda ref: ref[:, hs][:, None, :]
        s = s0_ref[:, h]
        sa = -jnp.sum(s * row(kap_ref), axis=-1, keepdims=True)
        v_col = jnp.sum(jnp.where(eye, row(v_ref), 0.0), axis=-1, keepdims=True)
        s_new = s * jnp.exp(row(lw_ref)) + sa * row(bb_ref) + v_col * row(kp_ref)
        sout_ref[:, h] = s_new
        o_col = jnp.sum(s_new * row(r_ref), axis=-1, keepdims=True)
        o_ref[:, hs] = jnp.sum(jnp.where(eye, o_col, 0.0), axis=1)


def _wkv_step(r, lw, kp, v, kap, bb, s0):
    m = r.shape[0]
    nb = 8
    tok = pl.BlockSpec((nb, D_A), lambda i: (i, 0))
    st = pl.BlockSpec((nb, N_HEADS, HEAD, HEAD), lambda i: (i, 0, 0, 0))
    return pl.pallas_call(
        _wkv_step_body,
        grid=(m // nb,),
        in_specs=[tok] * 6 + [st],
        out_specs=[tok, st],
        out_shape=[jax.ShapeDtypeStruct((m, D_A), F32),
                   jax.ShapeDtypeStruct((m, N_HEADS, HEAD, HEAD), F32)],
        compiler_params=_params(("arbitrary",), 24 * 2**20),
        name="wkv_step",
    )(r, lw, kp, v, kap, bb, s0)


def _merge_body(seq_mode, tm, x_ref, o_ref, r_ref, kp_ref, v_ref, u_ref, vg_ref,
                ga_ref, gb_ref, g1_ref, sc2_ref, sh2_ref, lxg_ref, lxb_ref, rk_ref,
                bd_ref, ws_ref, bs_ref, pa_ref, pb_ref, wo_ref, l1g_ref, l1b_ref,
                x1_ref, h2_ref):
    bd = bd_ref[...]
    o = o_ref[0]
    inv_n = 1.0 / HEAD
    mu = _segsum(o, bd) * inv_n
    d = o - mu
    var = _segsum(d * d, bd) * inv_n
    on = d * lax.rsqrt(var + GN_EPS) * lxg_ref[...] + lxb_ref[...]
    v = v_ref[0]
    o_a = on + _segsum(r_ref[0] * kp_ref[0] * rk_ref[...], bd) * v
    vg = vg_ref[0]
    if seq_mode:
        row = lax.broadcasted_iota(jnp.int32, (CHUNK, CHUNK), 0)
        col = lax.broadcasted_iota(jnp.int32, (CHUNK, CHUNK), 1)
        lane = lax.broadcasted_iota(jnp.int32, (CHUNK, D_B), 1)
        gsz = D_B // N_GROUPS_B
        pieces = []
        for j in range(tm // CHUNK):
            vc = vg[j * CHUNK:(j + 1) * CHUNK, :]
            acc = bs_ref[...]
            for g in range(N_GROUPS_B):
                wg = jnp.where(row >= col, ws_ref[g], 0.0)
                vm = jnp.where((lane >= g * gsz) & (lane < (g + 1) * gsz), vc, 0.0)
                acc = acc + _mm3(wg, vm)
            pieces.append(acc)
        s = jnp.concatenate(pieces, axis=0) if len(pieces) > 1 else pieces[0]
    else:
        s = vg * ws_ref[...] + bs_ref[...]
    o_b = u_ref[0] * s
    y = _mm1(ga_ref[0] * _mm1(o_a, pa_ref[...]) + gb_ref[0] * _mm1(o_b, pb_ref[...]),
             wo_ref[...])
    x1 = _layer_norm(ALPHA * x_ref[0] + g1_ref[0] * y, l1g_ref[...], l1b_ref[...])
    x1_ref[0] = x1
    h2_ref[0] = x1 * (1.0 + sc2_ref[0]) + sh2_ref[0]


def _merge_stage(seq_mode, x, o, r, kp, v, u, vg, ga, gb, g1, sc2, sh2, p):
    b, t, d = x.shape
    tm = 256 if seq_mode else t
    tmod = 1 if seq_mode else tm
    tok = lambda n: pl.BlockSpec((1, tm, n), lambda i, j: (i, j, 0))
    mod = pl.BlockSpec((1, tmod, d), lambda i, j: (i, j if not seq_mode else 0, 0))
    ws, bs = (p["w_spatial"], p["b_spatial_full"]) if seq_mode else (p["ws_row"], p["bs_row"])
    est = 2 * tm * (2 * d + 7 * D_A + 2 * d + 2 * d) * 4 + 8 * tm * d * 4 + 16 * 2**20
    return pl.pallas_call(
        functools.partial(_merge_body, seq_mode, tm),
        grid=(b, t // tm),
        in_specs=[tok(d)] + [tok(D_A)] * 6 + [tok(d), tok(d), mod, mod, mod,
                  _const_spec((1, D_A)), _const_spec((1, D_A)), _const_spec((1, D_A)),
                  _const_spec((D_A, D_A)), _const_spec(ws.shape), _const_spec(bs.shape),
                  _const_spec((D_A, d)), _const_spec((D_B, d)), _const_spec((d, d)),
                  _const_spec((1, d)), _const_spec((1, d))],
        out_specs=[tok(d), tok(d)],
        out_shape=[jax.ShapeDtypeStruct((b, t, d), F32)] * 2,
        compiler_params=_params(("arbitrary", "arbitrary"), est),
        name="merge_seq" if seq_mode else "merge_row",
    )(x, o, r, kp, v, u, vg, ga, gb, g1, sc2, sh2, p["lnx_g"], p["lnx_b"], p["r_k"],
      p["bd"], ws, bs, p["w_branch_a"], p["w_branch_b"], p["w_out"], p["ln1_g"], p["ln1_b"])


def _route_body(h_ref, w_ref, b_ref, gate_ref, idx_ref, wts_ref, cnt_ref, base_ref):
    @pl.when(pl.program_id(0) == 0)
    def _():
        base_ref[...] = jnp.zeros_like(base_ref)

    lg = _mm3(h_ref[...], w_ref[...]) + b_ref[...]
    lane = lax.broadcasted_iota(jnp.int32, lg.shape, 1)
    lanef = lane.astype(F32)
    neg = -jnp.inf
    is_g = (lane >= N_EXPERTS) & (lane < N_EXPERTS + N_ROUTE_GROUPS)
    mg = jnp.max(jnp.where(is_g, lg, neg), axis=-1, keepdims=True)
    gidx = jnp.min(jnp.where(is_g & (lg == mg), lanef - N_EXPERTS, 1e9), axis=-1, keepdims=True)
    pg_sel = 1.0 / jnp.sum(jnp.where(is_g, jnp.exp(lg - mg), 0.0), axis=-1, keepdims=True)
    lo = gidx * EXP_PER_GROUP
    in_grp = (lanef >= lo) & (lanef < lo + EXP_PER_GROUP)
    t1 = jnp.max(jnp.where(in_grp, lg, neg), axis=-1, keepdims=True)
    i1 = jnp.min(jnp.where(in_grp & (lg == t1), lanef, 1e9), axis=-1, keepdims=True)
    rest = in_grp & (lanef != i1)
    t2 = jnp.max(jnp.where(rest, lg, neg), axis=-1, keepdims=True)
    i2 = jnp.min(jnp.where(rest & (lg == t2), lanef, 1e9), axis=-1, keepdims=True)
    e2 = jnp.exp(t2 - t1)
    w1 = pg_sel / (1.0 + e2)
    w2 = pg_sel * e2 / (1.0 + e2)
    gate_ref[...] = jnp.where(lanef == i1, w1, 0.0) + jnp.where(lanef == i2, w2, 0.0)
    tm = lg.shape[0]
    hit = (lanef == i1) | (lanef == i2)
    earlier = (lax.broadcasted_iota(jnp.int32, (tm, tm), 0)
               > lax.broadcasted_iota(jnp.int32, (tm, tm), 1))
    before = base_ref[...] + _dot(earlier.astype(BF16), hit.astype(BF16))
    rank1 = jnp.sum(jnp.where(lanef == i1, before, 0.0), axis=-1, keepdims=True)
    rank2 = jnp.sum(jnp.where(lanef == i2, before, 0.0), axis=-1, keepdims=True)
    base_ref[...] += jnp.sum(hit.astype(F32), axis=0, keepdims=True)
    cnt_ref[...] = base_ref[...].astype(jnp.int32)
    idx = jnp.where(lane == 0, i1, jnp.where(lane == 1, i2, jnp.where(lane == 2, rank1, rank2)))
    idx_ref[...] = idx.astype(jnp.int32)
    wts_ref[...] = jnp.where(lane == 0, w1, w2)


def _route(h2, p):
    m, d = h2.shape
    tm = min(m, 512)
    tok = pl.BlockSpec((tm, LANES), lambda i: (i, 0))
    return pl.pallas_call(
        _route_body,
        grid=(m // tm,),
        in_specs=[pl.BlockSpec((tm, d), lambda i: (i, 0)),
                  _const_spec((d, LANES)), _const_spec((1, LANES))],
        out_specs=[tok, tok, tok, _const_spec((1, LANES))],
        out_shape=[jax.ShapeDtypeStruct((m, LANES), F32),
                   jax.ShapeDtypeStruct((m, LANES), jnp.int32),
                   jax.ShapeDtypeStruct((m, LANES), F32),
                   jax.ShapeDtypeStruct((1, LANES), jnp.int32)],
        scratch_shapes=[pltpu.VMEM((1, LANES), F32)],
        compiler_params=_params(("arbitrary",), 24 * 2**20),
        name="route",
    )(h2, p["w_route"], p["b_route"])


def _moe_body(h_ref, gate_ref, wg_ref, wu_ref, wd_ref, x1_ref, g2_ref, l2g_ref, l2b_ref,
              out_ref, acc_ref, xb_ref):
    e = pl.program_id(1)

    @pl.when(e == 0)
    def _():
        acc_ref[...] = jnp.zeros_like(acc_ref)
        xb_ref[...] = h_ref[...].astype(BF16)

    xb = xb_ref[...]
    pre = _dot(xb, wg_ref[0])
    hid = pre * jax.nn.sigmoid(pre) * _dot(xb, wu_ref[0])
    ye = _mm1(hid, wd_ref[0])
    gate = gate_ref[...]
    lane = lax.broadcasted_iota(jnp.int32, gate.shape, 1)
    ge = jnp.sum(jnp.where(lane == e, gate, 0.0), axis=-1, keepdims=True)
    acc_ref[...] += ge * ye

    @pl.when(e == pl.num_programs(1) - 1)
    def _():
        out_ref[...] = _layer_norm(ALPHA * x1_ref[...] + g2_ref[0] * acc_ref[...],
                                   l2g_ref[...], l2b_ref[...])


def _moe(seq_len, h2, gate, x1, g2, p):
    m, d = h2.shape
    tm = min(seq_len if g2.shape[1] == 1 else m, 1024)
    tok = lambda n: pl.BlockSpec((tm, n), lambda i, e: (i, 0))
    if g2.shape[1] == 1:
        tiles_per_seq = seq_len // tm
        g2_spec = pl.BlockSpec((1, 1, d), lambda i, e: (i // tiles_per_seq, 0, 0))
    else:
        g2_spec = pl.BlockSpec((1, tm, d), lambda i, e: (0, i, 0))
    est = 2 * tm * (3 * d + LANES) * 4 + tm * d * 6 + 4 * tm * d * 4 + 8 * 2**20
    return pl.pallas_call(
        _moe_body,
        grid=(m // tm, N_EXPERTS),
        in_specs=[tok(d), tok(LANES),
                  pl.BlockSpec((1, d, D_EXPERT), lambda i, e: (e, 0, 0)),
                  pl.BlockSpec((1, d, D_EXPERT), lambda i, e: (e, 0, 0)),
                  pl.BlockSpec((1, D_EXPERT, d), lambda i, e: (e, 0, 0)),
                  tok(d), g2_spec, _const_spec((1, d)), _const_spec((1, d))],
        out_specs=tok(d),
        out_shape=jax.ShapeDtypeStruct((m, d), F32),
        scratch_shapes=[pltpu.VMEM((tm, d), F32), pltpu.VMEM((tm, d), BF16)],
        compiler_params=_params(("arbitrary", "arbitrary"), est),
        name="moe_dense",
    )(h2, gate, p["w_exp_gate"], p["w_exp_up"], p["w_exp_down"], x1, g2, p["ln2_g"], p["ln2_b"])


MOE_BLOCK = 256
MOE_TOK_TILE = 256


def _dispatch_body(tail_ref, pos_ref, h_hbm, xs_hbm, zero_buf, sem, zsem):
    step = pl.program_id(0)
    nt = MOE_TOK_TILE
    nb_max = xs_hbm.shape[0] // MOE_BLOCK

    def zero_copy(row0):
        return pltpu.make_async_copy(
            zero_buf, xs_hbm.at[pl.ds(pl.multiple_of(row0, MOE_BLOCK), MOE_BLOCK)], zsem)

    @pl.when(step == 0)
    def _():
        zero_buf[...] = jnp.zeros_like(zero_buf)
        n_used = tail_ref[N_EXPERTS]

        def each_zero_copy(fn):
            for e in range(N_EXPERTS):
                @pl.when(tail_ref[e] >= 0)
                def _():
                    fn(zero_copy(tail_ref[e]))

            def unused(j, c):
                fn(zero_copy(j * MOE_BLOCK))
                return c

            lax.fori_loop(n_used, nb_max, unused, 0)

        each_zero_copy(lambda cp: cp.start())
        each_zero_copy(lambda cp: cp.wait())

    def row_copy(t, k):
        src = h_hbm.at[pl.ds(step * nt + t, 1)]
        return pltpu.make_async_copy(src, xs_hbm.at[pl.ds(pos_ref[0, 0, 2 * t + k], 1)], sem)

    def issue(t, c):
        row_copy(t, 0).start()
        row_copy(t, 1).start()
        return c

    def drain(t, c):
        row_copy(t, 0).wait()
        row_copy(t, 1).wait()
        return c

    lax.fori_loop(0, nt, issue, 0)
    lax.fori_loop(0, nt, drain, 0)


def _dispatch(h2, pos, tail_start, n_rows):
    m, d = h2.shape
    nt = MOE_TOK_TILE
    grid_spec = pltpu.PrefetchScalarGridSpec(
        num_scalar_prefetch=1,
        grid=(m // nt,),
        in_specs=[pl.BlockSpec((1, 1, 2 * nt), lambda i, tail: (i, 0, 0),
                               memory_space=pltpu.SMEM),
                  pl.BlockSpec(memory_space=pl.ANY)],
        out_specs=pl.BlockSpec(memory_space=pl.ANY),
        scratch_shapes=[pltpu.VMEM((MOE_BLOCK, d), F32),
                        pltpu.SemaphoreType.DMA, pltpu.SemaphoreType.DMA],
    )
    return pl.pallas_call(
        _dispatch_body,
        grid_spec=grid_spec,
        out_shape=jax.ShapeDtypeStruct((n_rows, d), F32),
        compiler_params=_params(("arbitrary",), 16 * 2**20),
        name="moe_dispatch",
    )(tail_start, pos.reshape(m // nt, 1, 2 * nt), h2)


def _gmm_body(be_ref, nb_ref, x_ref, wg_ref, wu_ref, wd_ref, y_ref):
    j = pl.program_id(0)

    @pl.when(j < nb_ref[0])
    def _():
        xb = x_ref[...].astype(BF16)
        pre = _dot(xb, wg_ref[0])
        hid = pre * jax.nn.sigmoid(pre) * _dot(xb, wu_ref[0])
        y_ref[...] = _mm1(hid, wd_ref[0])

    @pl.when(j >= nb_ref[0])
    def _():
        y_ref[...] = jnp.zeros_like(y_ref)


def _gmm(xs, blk_expert, n_blocks, p):
    n_rows, d = xs.shape
    nb_max = n_rows // MOE_BLOCK
    live = lambda j, be, nb: jnp.minimum(j, nb[0] - 1)
    wspec = lambda shape: pl.BlockSpec(shape, lambda j, be, nb: (be[live(j, be, nb)], 0, 0))
    grid_spec = pltpu.PrefetchScalarGridSpec(
        num_scalar_prefetch=2,
        grid=(nb_max,),
        in_specs=[pl.BlockSpec((MOE_BLOCK, d), lambda j, be, nb: (live(j, be, nb), 0)),
                  wspec((1, d, D_EXPERT)), wspec((1, d, D_EXPERT)), wspec((1, D_EXPERT, d))],
        out_specs=pl.BlockSpec((MOE_BLOCK, d), lambda j, be, nb: (j, 0)),
    )
    return pl.pallas_call(
        _gmm_body,
        grid_spec=grid_spec,
        out_shape=jax.ShapeDtypeStruct((n_rows, d), F32),
        compiler_params=_params(("arbitrary",), 24 * 2**20),
        name="moe_gmm",
    )(blk_expert, n_blocks, xs, p["w_exp_gate"], p["w_exp_up"], p["w_exp_down"])


def _combine_body(pos_ref, ys_hbm, wts_ref, x1_ref, g2_ref, l2g_ref, l2b_ref, out_ref,
                  buf, sem):
    nt = MOE_TOK_TILE

    def row_copy(t, k):
        return pltpu.make_async_copy(ys_hbm.at[pl.ds(pos_ref[0, 0, 2 * t + k], 1)],
                                     buf.at[k, pl.ds(t, 1)], sem)

    def issue(t, c):
        row_copy(t, 0).start()
        row_copy(t, 1).start()
        return c

    def drain(t, c):
        row_copy(t, 0).wait()
        row_copy(t, 1).wait()
        return c

    lax.fori_loop(0, nt, issue, 0)
    lax.fori_loop(0, nt, drain, 0)
    w = wts_ref[...]
    moe = w[:, 0:1] * buf[0] + w[:, 1:2] * buf[1]
    out_ref[...] = _layer_norm(ALPHA * x1_ref[...] + g2_ref[0] * moe, l2g_ref[...], l2b_ref[...])


def _combine(seq_len, ys, pos, wts, x1, g2, p):
    m, d = x1.shape
    nt = MOE_TOK_TILE
    tiles_per_seq = seq_len // nt
    tok = lambda n: pl.BlockSpec((nt, n), lambda i: (i, 0))
    return pl.pallas_call(
        _combine_body,
        grid=(m // nt,),
        in_specs=[pl.BlockSpec((1, 1, 2 * nt), lambda i: (i, 0, 0), memory_space=pltpu.SMEM),
                  pl.BlockSpec(memory_space=pl.ANY), tok(LANES), tok(d),
                  pl.BlockSpec((1, 1, d), lambda i: (i // tiles_per_seq, 0, 0)),
                  _const_spec((1, d)), _const_spec((1, d))],
        out_specs=tok(d),
        out_shape=jax.ShapeDtypeStruct((m, d), F32),
        scratch_shapes=[pltpu.VMEM((2, nt, d), F32), pltpu.SemaphoreType.DMA],
        compiler_params=_params(("arbitrary",), 24 * 2**20),
        name="moe_combine",
    )(pos.reshape(m // nt, 1, 2 * nt), ys, wts, x1, g2, p["ln2_g"], p["ln2_b"])


def _moe_routed(seq_len, h2, idx, wts, counts, x1, g2, p):
    m, d = h2.shape
    blk = MOE_BLOCK
    nb_max = -(-(2 * m + N_EXPERTS * (blk - 1)) // blk)
    cnt = counts[0, :N_EXPERTS]
    padded = ((cnt + blk - 1) // blk) * blk
    ends = jnp.cumsum(padded)
    starts = ends - padded
    n_blocks = (ends[-1] // blk).astype(jnp.int32).reshape(1)
    first_row = jnp.arange(nb_max, dtype=jnp.int32) * blk
    blk_expert = jnp.minimum(
        jnp.sum((first_row[:, None] >= ends[None, :]).astype(jnp.int32), axis=1),
        N_EXPERTS - 1).astype(jnp.int32)
    tail_start = jnp.concatenate(
        [jnp.where(padded > 0, ends - blk, -1).astype(jnp.int32), n_blocks])
    pos = (starts[idx[:, 0:2]] + idx[:, 2:4]).astype(jnp.int32)
    xs = _dispatch(h2, pos, tail_start, nb_max * blk)
    ys = _gmm(xs, blk_expert, n_blocks, p)
    return _combine(seq_len, ys, pos, wts, x1, g2, p)


def _prep_layer(l, w_in, mu_shift, w0, w_decay_up, a0, w_iclr_up, k_k, k_a, r_k,
                lnx_g, lnx_b, lnv_g, lnv_b, w_spatial, b_spatial, w_branch_a, w_branch_b,
                w_out, ln1_g, ln1_b, w_route_group, b_route_group, w_route_expert,
                b_route_expert, w_exp_gate, w_exp_up, w_exp_down, ln2_g, ln2_b):
    d = D_MODEL
    pad_a = N_SHIFT_PAD - N_SHIFT
    wi = w_in[l]
    w_in_p = jnp.concatenate(
        [wi[:, :N_SHIFT], jnp.zeros((d, pad_a), F32), wi[:, N_SHIFT:]], axis=1).astype(BF16)
    mu = jnp.concatenate([mu_shift[l], jnp.zeros((pad_a,), F32)])[None]
    lora = jnp.zeros((LANES, 2 * D_A), F32)
    lora = lora.at[:R_LORA, :D_A].set(w_decay_up[l]).at[R_LORA:2 * R_LORA, D_A:].set(w_iclr_up[l])
    seg = jnp.arange(D_A) // HEAD
    row1 = lambda x: x.reshape(1, -1)
    gsz = D_B // N_GROUPS_B
    w_route = jnp.concatenate(
        [w_route_expert[l], w_route_group[l],
         jnp.zeros((d, LANES - N_EXPERTS - N_ROUTE_GROUPS), F32)], axis=1)
    b_route = jnp.concatenate(
        [b_route_expert[l], b_route_group[l],
         jnp.zeros((LANES - N_EXPERTS - N_ROUTE_GROUPS,), F32)])[None]
    return dict(
        w_in=w_in_p, mu=mu, lora=lora,
        w0a0=jnp.concatenate([w0[l], a0[l]])[None],
        k_k=row1(k_k[l]), k_a=row1(k_a[l]), r_k=row1(r_k[l]),
        lnx_g=row1(lnx_g[l]), lnx_b=row1(lnx_b[l]),
        lnv_g=row1(lnv_g[l]), lnv_b=row1(lnv_b[l]),
        bd=(seg[:, None] == seg[None, :]).astype(BF16),
        w_spatial=w_spatial[l],
        b_spatial_full=jnp.repeat(b_spatial[l].T, gsz, axis=1),
        ws_row=jnp.repeat(w_spatial[l][:, 0, 0], gsz)[None],
        bs_row=jnp.repeat(b_spatial[l][:, 0], gsz)[None],
        w_branch_a=w_branch_a[l].astype(BF16), w_branch_b=w_branch_b[l].astype(BF16),
        w_out=w_out[l].astype(BF16), ln1_g=row1(ln1_g[l]), ln1_b=row1(ln1_b[l]),
        w_route=w_route, b_route=b_route,
        w_exp_gate=w_exp_gate[l].astype(BF16), w_exp_up=w_exp_up[l].astype(BF16),
        w_exp_down=w_exp_down[l].astype(BF16), ln2_g=row1(ln2_g[l]), ln2_b=row1(ln2_b[l]),
    )


def _trunk(seq_mode, x, mods, wkv_in, shift_in, preps):
    b, t, d = x.shape
    wkv_out, shift_out, v_out = [], [], []
    for l in range(DEPTH):
        p = preps[l]
        sh1, sc1, g1, sh2, sc2, g2 = mods[l]
        if seq_mode:
            zprev = jnp.zeros((b, 1, N_SHIFT_PAD), F32) if shift_in is None else shift_in[l]
        else:
            zprev = _matmul(shift_in[l], p["w_in"][:, :N_SHIFT_PAD]).reshape(b, t, N_SHIFT_PAD)
        r, lw, kp, v, kap, bb, u, vg, ga, gb, hl = _in_stage(seq_mode, x, sc1, sh1, zprev, p)
        if seq_mode:
            o, s_new = _wkv_seq(r, lw, kp, v, kap, bb, wkv_in[l])
        else:
            flat = lambda a: a.reshape(t, D_A)
            o, s_new = _wkv_step(flat(r), flat(lw), flat(kp), flat(v), flat(kap), flat(bb),
                                 wkv_in[l])
            o = o.reshape(b, t, D_A)
        x1, h2 = _merge_stage(seq_mode, x, o, r, kp, v, u, vg, ga, gb, g1, sc2, sh2, p)
        m = b * t
        h2f = h2.reshape(m, d)
        gate, _, _, _ = _route(h2f, p)
        x = _moe(t, h2f, gate, x1.reshape(m, d), g2, p).reshape(b, t, d)
        wkv_out.append(s_new)
        shift_out.append(hl)
        v_out.append(vg)
    return x, wkv_out, shift_out, v_out


def kernel(x_prompt, x_sample, c_prompt, c_sample, state_wkv, state_shift, w_ada, b_ada, w_in, mu_shift, w0, w_decay_up, a0, w_iclr_up, k_k, k_a, r_k, lnx_g, lnx_b, lnv_g, lnv_b, w_spatial, b_spatial, w_branch_a, w_branch_b, w_out, ln1_g, ln1_b, w_route_group, b_route_group, w_route_expert, b_route_expert, w_exp_gate, w_exp_up, w_exp_down, ln2_g, ln2_b):
    bp, tp, d = x_prompt.shape
    bs = x_sample.shape[0]
    layer_params = (w_in, mu_shift, w0, w_decay_up, a0, w_iclr_up, k_k, k_a, r_k, lnx_g,
                    lnx_b, lnv_g, lnv_b, w_spatial, b_spatial, w_branch_a, w_branch_b, w_out,
                    ln1_g, ln1_b, w_route_group, b_route_group, w_route_expert,
                    b_route_expert, w_exp_gate, w_exp_up, w_exp_down, ln2_g, ln2_b)
    preps = [_prep_layer(l, *layer_params) for l in range(DEPTH)]
    mod_all = _ada(jnp.concatenate([c_prompt, c_sample], axis=0), w_ada, b_ada)
    mods_p, mods_s = [], []
    for l in range(DEPTH):
        parts = jnp.split(mod_all[l], 6, axis=-1)
        mods_p.append([q[:bp].reshape(bp, 1, d) for q in parts])
        mods_s.append([q[bp:].reshape(1, bs, d) for q in parts])

    wkv0 = jnp.zeros((DEPTH, bp, N_HEADS, HEAD, HEAD), F32)
    y_p, wkv_p, shift_p, _ = _trunk(True, x_prompt, mods_p, wkv0, None, preps)
    y_s, wkv_s, shift_s, v_s = _trunk(False, x_sample.reshape(1, bs, d), mods_s, state_wkv,
                                      state_shift, preps)
    return (y_p,
            y_s.reshape(bs, 1, d),
            jnp.stack(wkv_p),
            jnp.stack([s.reshape(bp, d) for s in shift_p]),
            jnp.stack(wkv_s),
            jnp.stack([s.reshape(bs, d) for s in shift_s]),
            jnp.stack([q.reshape(bs, 1, D_B) for q in v_s]))
```

```python
import functools

import jax
import jax.numpy as jnp
from jax import lax
from jax.experimental import pallas as pl
from jax.experimental.pallas import tpu as pltpu

F32 = jnp.float32
BF16 = jnp.bfloat16

D_MODEL = 1024
DEPTH = 2
HEAD = 64
N_HEADS = 8
D_A = N_HEADS * HEAD
R_LORA = 32
CHUNK = 128
N_GROUPS_B = 8
D_B = 512
N_SHIFT = 3 * D_A + 2 * R_LORA
N_ROUTE_GROUPS = 4
EXP_PER_GROUP = 8
N_EXPERTS = N_ROUTE_GROUPS * EXP_PER_GROUP
D_EXPERT = 256
ALPHA = (2 * DEPTH) ** 0.25
LN_EPS = 1e-5
GN_EPS = 64e-5

LANES = 128
N_SHIFT_PAD = 13 * LANES
COL_U = N_SHIFT_PAD
COL_VG = COL_U + D_B
COL_GA = COL_VG + D_B
COL_GB = COL_GA + D_MODEL
N_IN_PAD = COL_GB + D_MODEL
WKV_CHUNK = 64
VMEM_CAP_BYTES = 60000 * 1024

_NN = (((1,), (0,)), ((), ()))
_B_NT = (((2,), (2,)), ((0,), (0,)))
_B_NN = (((2,), (1,)), ((0,), (0,)))
_B_TN = (((1,), (1,)), ((0,), (0,)))


def _dot(a, b, dims=_NN):
    return lax.dot_general(a, b, dims, preferred_element_type=F32)


def _split2(x):
    hi = x.astype(BF16)
    lo = (x - hi.astype(F32)).astype(BF16)
    return hi, lo


def _mm1(a, b, dims=_NN):
    return _dot(a.astype(BF16), b.astype(BF16), dims)


def _mm3(a, b, dims=_NN):
    ah, al = _split2(a)
    bh, bl = _split2(b)
    return _dot(ah, bh, dims) + (_dot(ah, bl, dims) + _dot(al, bh, dims))


def _segsum(x, bd):
    hi, lo = _split2(x)
    return _dot(hi, bd) + _dot(lo, bd)


def _layer_norm(x, g, b):
    mu = jnp.mean(x, axis=-1, keepdims=True)
    d = x - mu
    var = jnp.mean(d * d, axis=-1, keepdims=True)
    return d * lax.rsqrt(var + LN_EPS) * g + b


def _gelu(x):
    return 0.5 * x * (1.0 + lax.erf(x * 0.7071067811865476))


def _params(sem, est_bytes):
    limit = int(min(VMEM_CAP_BYTES, max(est_bytes, 16 * 1024 * 1024)))
    return pltpu.CompilerParams(dimension_semantics=sem, vmem_limit_bytes=limit)


def _const_spec(shape, single_buffer=False):
    nd = len(shape)
    if single_buffer:
        return pl.BlockSpec(shape, lambda *_: (0,) * nd, pipeline_mode=pl.Buffered(1))
    return pl.BlockSpec(shape, lambda *_: (0,) * nd)


def _ada_body(c_ref, w_ref, b_ref, o_ref):
    c = c_ref[...]
    s = c * jax.nn.sigmoid(c)
    o_ref[0] = _mm3(s, w_ref[0]) + b_ref[0]


def _ada(c_all, w_ada, b_ada):
    depth, d, n6 = w_ada.shape
    m = c_all.shape[0]
    tn = 512
    return pl.pallas_call(
        _ada_body,
        grid=(depth, n6 // tn),
        in_specs=[
            pl.BlockSpec((m, d), lambda l, j: (0, 0)),
            pl.BlockSpec((1, d, tn), lambda l, j: (l, 0, j)),
            pl.BlockSpec((1, 1, tn), lambda l, j: (l, 0, j)),
        ],
        out_specs=pl.BlockSpec((1, m, tn), lambda l, j: (l, 0, j)),
        out_shape=jax.ShapeDtypeStruct((depth, m, n6), F32),
        compiler_params=_params(("arbitrary", "arbitrary"), 24 * 2**20),
        name="ada_mod",
    )(c_all, w_ada, b_ada.reshape(depth, 1, n6))


def _mm_body(x_ref, w_ref, o_ref):
    o_ref[...] = _mm1(x_ref[...], w_ref[...])


def _matmul(x, w):
    m, k = x.shape
    n = w.shape[1]
    return pl.pallas_call(
        _mm_body,
        grid=(1,),
        in_specs=[_const_spec((m, k)), _const_spec((k, n))],
        out_specs=_const_spec((m, n)),
        out_shape=jax.ShapeDtypeStruct((m, n), F32),
        compiler_params=_params(("arbitrary",), 24 * 2**20),
        name="shift_proj",
    )(x, w)


def _in_body(seq_mode, tm, x_ref, sc_ref, sh_ref, w_ref, mu_ref, zp_ref, lora_ref,
             w0a0_ref, kk_ref, ka_ref, lng_ref, lnb_ref, bd_ref,
             r_ref, lw_ref, kp_ref, v_ref, kap_ref, bb_ref, u_ref, vg_ref,
             ga_ref, gb_ref, hl_ref, carry_ref):
    h = x_ref[0] * (1.0 + sc_ref[0]) + sh_ref[0]
    if seq_mode:
        hl_ref[0] = h[tm - 1:tm, :]
    else:
        hl_ref[0] = h
    z = _dot(h.astype(BF16), w_ref[...])
    za = z[:, :N_SHIFT_PAD]
    if seq_mode:
        @pl.when(pl.program_id(1) == 0)
        def _():
            carry_ref[...] = zp_ref[0]

        row = lax.broadcasted_iota(jnp.int32, za.shape, 0)
        prev = jnp.where(row == 0, carry_ref[...], pltpu.roll(za, 1, 0))
        carry_ref[...] = za[tm - 1:tm, :]
    else:
        prev = zp_ref[0]
    mix = za + mu_ref[...] * (prev - za)
    r = mix[:, 0:D_A]
    k = mix[:, D_A:2 * D_A]
    v = mix[:, 2 * D_A:3 * D_A]
    xwa = mix[:, 3 * D_A:N_SHIFT_PAD]
    lane = lax.broadcasted_iota(jnp.int32, xwa.shape, 1)
    lora_in = jnp.where(lane < R_LORA, jnp.tanh(xwa), xwa)
    pre = w0a0_ref[...] + _mm3(lora_in, lora_ref[...])
    yw = -pre[:, :D_A]
    softplus = jnp.maximum(yw, 0.0) + jnp.log1p(jnp.exp(-jnp.abs(yw)))
    lw = -jnp.exp(-softplus - 0.5)
    a = jax.nn.sigmoid(pre[:, D_A:])
    bd = bd_ref[...]
    kk = k * kk_ref[...]
    kap = kk / jnp.maximum(jnp.sqrt(_segsum(kk * kk, bd)), 1e-12)
    r_ref[0] = r
    lw_ref[0] = lw
    kp_ref[0] = k * (1.0 + (a - 1.0) * ka_ref[...])
    v_ref[0] = v
    kap_ref[0] = kap
    bb_ref[0] = kap * a
    u_ref[0] = _gelu(z[:, COL_U:COL_VG])
    vg_ref[0] = _layer_norm(_gelu(z[:, COL_VG:COL_GA]), lng_ref[...], lnb_ref[...])
    ga_ref[0] = jax.nn.sigmoid(z[:, COL_GA:COL_GB])
    gb_ref[0] = jax.nn.sigmoid(z[:, COL_GB:N_IN_PAD])


def _in_stage(seq_mode, x, sc, sh, zprev, p):
    b, t, d = x.shape
    tm = 256 if seq_mode else t
    tmod = 1 if seq_mode else tm
    grid = (b, t // tm)
    tok = lambda n: pl.BlockSpec((1, tm, n), lambda i, j: (i, j, 0))
    mod = pl.BlockSpec((1, tmod, d), lambda i, j: (i, j if not seq_mode else 0, 0))
    zp_spec = (pl.BlockSpec((1, 1, N_SHIFT_PAD), lambda i, j: (i, 0, 0)) if seq_mode
               else tok(N_SHIFT_PAD))
    hl_spec = (pl.BlockSpec((1, 1, d), lambda i, j: (i, 0, 0)) if seq_mode else tok(d))
    hl_shape = (b, 1, d) if seq_mode else (b, t, d)
    out_cols = [D_A] * 6 + [D_B] * 2 + [d] * 2
    est = (2 * tm * (d + N_SHIFT_PAD + sum(out_cols) + d) * 4 + 2 * d * N_IN_PAD * 2
           + 3 * tm * N_IN_PAD * 4 + 4 * 2**20)
    outs = pl.pallas_call(
        functools.partial(_in_body, seq_mode, tm),
        grid=grid,
        in_specs=[tok(d), mod, mod,
                  _const_spec((d, N_IN_PAD), True), _const_spec((1, N_SHIFT_PAD)), zp_spec,
                  _const_spec((LANES, 2 * D_A)), _const_spec((1, 2 * D_A)),
                  _const_spec((1, D_A)), _const_spec((1, D_A)),
                  _const_spec((1, D_B)), _const_spec((1, D_B)),
                  _const_spec((D_A, D_A))],
        out_specs=[tok(n) for n in out_cols] + [hl_spec],
        out_shape=[jax.ShapeDtypeStruct((b, t, n), F32) for n in out_cols]
        + [jax.ShapeDtypeStruct(hl_shape, F32)],
        scratch_shapes=[pltpu.VMEM((1, N_SHIFT_PAD), F32)],
        compiler_params=_params(("arbitrary", "arbitrary"), est),
        name="in_stage_seq" if seq_mode else "in_stage_row",
    )(x, sc, sh, p["w_in"], p["mu"], zprev, p["lora"], p["w0a0"], p["k_k"], p["k_a"],
      p["lnv_g"], p["lnv_b"], p["bd"])
    return outs


_WKV_3PASS = frozenset({"ark", "arkv", "h"})


def _wmm(name, a, b, dims):
    return (_mm3 if name in _WKV_3PASS else _mm1)(a, b, dims)


def _chunk_heads(x, nsub):
    c = WKV_CHUNK
    return jnp.stack([x[c * i:c * (i + 1), HEAD * h:HEAD * (h + 1)]
                      for i in range(nsub) for h in range(N_HEADS)], axis=0)


def _masked_rowsum(mask_bf, x):
    h1, h2 = _split2(x)
    return _dot(mask_bf, h1) + _dot(mask_bf, h2)


def _masked_rowsum_t(x, mask):
    h1, h2 = _split2(x)
    m = mask.astype(BF16)
    return _dot(h1, m) + _dot(h2, m)


def _wkv_seq_body(nsub, r_ref, lw_ref, kp_ref, v_ref, kap_ref, bb_ref, s0_ref,
                  o_ref, sout_ref, s_scr):
    c = WKV_CHUNK
    tc = nsub * c

    @pl.when(pl.program_id(1) == 0)
    def _():
        s_scr[...] = s0_ref[0]

    row = lax.broadcasted_iota(jnp.int32, (tc, tc), 0)
    col = lax.broadcasted_iota(jnp.int32, (tc, tc), 1)
    shift = c.bit_length() - 1
    same_chunk = (row >> shift) == (col >> shift)
    lw = lw_ref[0]
    g = _masked_rowsum((same_chunk & (row >= col)).astype(BF16), lw)
    g_end = _masked_rowsum(same_chunk.astype(BF16), lw)
    e_neg = jnp.exp(-g)
    e_end = jnp.exp(g_end - g)
    ch = functools.partial(_chunk_heads, nsub=nsub)
    k = kp_ref[0]
    b = bb_ref[0]
    kap_t = ch(kap_ref[0] * jnp.exp(g - lw))
    b_t = ch(b * e_neg)
    k_t = ch(k * e_neg)
    r_t = ch(r_ref[0] * jnp.exp(g))
    b_e = ch(b * e_end)
    k_e = ch(k * e_end)
    vv = ch(v_ref[0])
    decay_end = ch(jnp.exp(g_end))

    r64 = lax.broadcasted_iota(jnp.int32, (c, c), 0)
    c64 = lax.broadcasted_iota(jnp.int32, (c, c), 1)
    tri_incl = (r64 >= c64)[None]
    tri_strict = (r64 > c64)[None]
    eye = (r64 == c64)[None]

    l_b = jnp.where(tri_strict, _wmm("lb", kap_t, b_t, _B_NT), 0.0)
    l_k = jnp.where(tri_strict, _wmm("lk", kap_t, k_t, _B_NT), 0.0)
    a_rb = jnp.where(tri_incl, _wmm("arb", r_t, b_t, _B_NT), 0.0)
    a_rk = jnp.where(tri_incl, _wmm("ark", r_t, k_t, _B_NT), 0.0)
    m = -l_b
    t_inv = jnp.where(eye, 1.0, 0.0) + m
    m = _wmm("inv", m, m, _B_NN)
    span = 2
    while 2 * span < c:
        both = _wmm("inv", jnp.concatenate([m, t_inv], axis=1), m, _B_NN)
        m = both[:, :c]
        t_inv = t_inv + both[:, c:]
        span *= 2
    t_inv = t_inv + _wmm("inv", t_inv, m, _B_NN)
    a1 = -_wmm("ta", t_inv, kap_t, _B_NN)
    u0 = -_wmm("tu", t_inv, _wmm("lkv", l_k, vv, _B_NN), _B_NN)
    a2 = r_t + _wmm("arba", a_rb, a1, _B_NN)
    o0 = _wmm("arbu", a_rb, u0, _B_NN) + _wmm("arkv", a_rk, vv, _B_NN)
    g_mat = jnp.where(eye, decay_end, 0.0) + _wmm("g", a1, b_e, _B_TN)
    h_mat = _wmm("h", jnp.concatenate([u0, vv], axis=1),
                 jnp.concatenate([b_e, k_e], axis=1), _B_TN)

    s = s_scr[...]
    for i in range(nsub):
        hs = slice(N_HEADS * i, N_HEADS * (i + 1))
        o = _wmm("o", a2[hs], s, _B_NT) + o0[hs]
        s = _wmm("s", s, g_mat[hs], _B_NN) + h_mat[hs]
        for h in range(N_HEADS):
            o_ref[0, c * i:c * (i + 1), HEAD * h:HEAD * (h + 1)] = o[h]
    s_scr[...] = s

    @pl.when(pl.program_id(1) == pl.num_programs(1) - 1)
    def _():
        sout_ref[0] = s


def _wkv_seq(r, lw, kp, v, kap, bb, s0):
    b, t, _ = r.shape
    tc = 256
    nsub = tc // WKV_CHUNK
    tok = pl.BlockSpec((1, tc, D_A), lambda i, j: (i, j, 0))
    st = pl.BlockSpec((1, N_HEADS, HEAD, HEAD), lambda i, j: (i, 0, 0, 0))
    return pl.pallas_call(
        functools.partial(_wkv_seq_body, nsub),
        grid=(b, t // tc),
        in_specs=[tok] * 6 + [st],
        out_specs=[tok, st],
        out_shape=[jax.ShapeDtypeStruct((b, t, D_A), F32),
                   jax.ShapeDtypeStruct((b, N_HEADS, HEAD, HEAD), F32)],
        scratch_shapes=[pltpu.VMEM((N_HEADS, HEAD, HEAD), F32)],
        compiler_params=_params(("arbitrary", "arbitrary"), 32 * 2**20),
        name="wkv_seq",
    )(r, lw, kp, v, kap, bb, s0)


def _wkv_step_body(r_ref, lw_ref, kp_ref, v_ref, kap_ref, bb_ref, s0_ref, o_ref, sout_ref):
    eye = (lax.broadcasted_iota(jnp.int32, (HEAD, HEAD), 0)
           == lax.broadcasted_iota(jnp.int32, (HEAD, HEAD), 1))[None]
    for h in range(N_HEADS):
        hs = slice(HEAD * h, HEAD * (h + 1))
        row = lambda ref: ref[:, hs][:, None, :]
        s = s0_ref[:, h]
        sa = -jnp.sum(s * row(kap_ref), axis=-1, keepdims=True)
        v_col = jnp.sum(jnp.where(eye, row(v_ref), 0.0), axis=-1, keepdims=True)
        s_new = s * jnp.exp(row(lw_ref)) + sa * row(bb_ref) + v_col * row(kp_ref)
        sout_ref[:, h] = s_new
        o_col = jnp.sum(s_new * row(r_ref), axis=-1, keepdims=True)
        o_ref[:, hs] = jnp.sum(jnp.where(eye, o_col, 0.0), axis=1)


def _wkv_step(r, lw, kp, v, kap, bb, s0):
    m = r.shape[0]
    nb = 8
    tok = pl.BlockSpec((nb, D_A), lambda i: (i, 0))
    st = pl.BlockSpec((nb, N_HEADS, HEAD, HEAD), lambda i: (i, 0, 0, 0))
    return pl.pallas_call(
        _wkv_step_body,
        grid=(m // nb,),
        in_specs=[tok] * 6 + [st],
        out_specs=[tok, st],
        out_shape=[jax.ShapeDtypeStruct((m, D_A), F32),
                   jax.ShapeDtypeStruct((m, N_HEADS, HEAD, HEAD), F32)],
        compiler_params=_params(("arbitrary",), 24 * 2**20),
        name="wkv_step",
    )(r, lw, kp, v, kap, bb, s0)


def _merge_body(seq_mode, tm, x_ref, o_ref, r_ref, kp_ref, v_ref, u_ref, vg_ref,
                ga_ref, gb_ref, g1_ref, sc2_ref, sh2_ref, lxg_ref, lxb_ref, rk_ref,
                bd_ref, ws_ref, bs_ref, pa_ref, pb_ref, wo_ref, l1g_ref, l1b_ref,
                x1_ref, h2_ref):
    bd = bd_ref[...]
    o = o_ref[0]
    inv_n = 1.0 / HEAD
    mu = _segsum(o, bd) * inv_n
    d = o - mu
    var = _segsum(d * d, bd) * inv_n
    on = d * lax.rsqrt(var + GN_EPS) * lxg_ref[...] + lxb_ref[...]
    v = v_ref[0]
    o_a = on + _segsum(r_ref[0] * kp_ref[0] * rk_ref[...], bd) * v
    vg = vg_ref[0]
    if seq_mode:
        row = lax.broadcasted_iota(jnp.int32, (CHUNK, CHUNK), 0)
        col = lax.broadcasted_iota(jnp.int32, (CHUNK, CHUNK), 1)
        lane = lax.broadcasted_iota(jnp.int32, (CHUNK, D_B), 1)
        gsz = D_B // N_GROUPS_B
        pieces = []
        for j in range(tm // CHUNK):
            vc = vg[j * CHUNK:(j + 1) * CHUNK, :]
            acc = bs_ref[...]
            for g in range(N_GROUPS_B):
                wg = jnp.where(row >= col, ws_ref[g], 0.0)
                vm = jnp.where((lane >= g * gsz) & (lane < (g + 1) * gsz), vc, 0.0)
                acc = acc + _mm3(wg, vm)
            pieces.append(acc)
        s = jnp.concatenate(pieces, axis=0) if len(pieces) > 1 else pieces[0]
    else:
        s = vg * ws_ref[...] + bs_ref[...]
    o_b = u_ref[0] * s
    y = _mm1(ga_ref[0] * _mm1(o_a, pa_ref[...]) + gb_ref[0] * _mm1(o_b, pb_ref[...]),
             wo_ref[...])
    x1 = _layer_norm(ALPHA * x_ref[0] + g1_ref[0] * y, l1g_ref[...], l1b_ref[...])
    x1_ref[0] = x1
    h2_ref[0] = x1 * (1.0 + sc2_ref[0]) + sh2_ref[0]


def _merge_stage(seq_mode, x, o, r, kp, v, u, vg, ga, gb, g1, sc2, sh2, p):
    b, t, d = x.shape
    tm = 256 if seq_mode else t
    tmod = 1 if seq_mode else tm
    tok = lambda n: pl.BlockSpec((1, tm, n), lambda i, j: (i, j, 0))
    mod = pl.BlockSpec((1, tmod, d), lambda i, j: (i, j if not seq_mode else 0, 0))
    ws, bs = (p["w_spatial"], p["b_spatial_full"]) if seq_mode else (p["ws_row"], p["bs_row"])
    est = 2 * tm * (2 * d + 7 * D_A + 2 * d + 2 * d) * 4 + 8 * tm * d * 4 + 16 * 2**20
    return pl.pallas_call(
        functools.partial(_merge_body, seq_mode, tm),
        grid=(b, t // tm),
        in_specs=[tok(d)] + [tok(D_A)] * 6 + [tok(d), tok(d), mod, mod, mod,
                  _const_spec((1, D_A)), _const_spec((1, D_A)), _const_spec((1, D_A)),
                  _const_spec((D_A, D_A)), _const_spec(ws.shape), _const_spec(bs.shape),
                  _const_spec((D_A, d)), _const_spec((D_B, d)), _const_spec((d, d)),
                  _const_spec((1, d)), _const_spec((1, d))],
        out_specs=[tok(d), tok(d)],
        out_shape=[jax.ShapeDtypeStruct((b, t, d), F32)] * 2,
        compiler_params=_params(("arbitrary", "arbitrary"), est),
        name="merge_seq" if seq_mode else "merge_row",
    )(x, o, r, kp, v, u, vg, ga, gb, g1, sc2, sh2, p["lnx_g"], p["lnx_b"], p["r_k"],
      p["bd"], ws, bs, p["w_branch_a"], p["w_branch_b"], p["w_out"], p["ln1_g"], p["ln1_b"])


def _route_body(h_ref, w_ref, b_ref, gate_ref, idx_ref, wts_ref, cnt_ref):
    lg = _mm3(h_ref[...], w_ref[...]) + b_ref[...]
    lane = lax.broadcasted_iota(jnp.int32, lg.shape, 1)
    lanef = lane.astype(F32)
    neg = -jnp.inf
    is_g = (lane >= N_EXPERTS) & (lane < N_EXPERTS + N_ROUTE_GROUPS)
    mg = jnp.max(jnp.where(is_g, lg, neg), axis=-1, keepdims=True)
    gidx = jnp.min(jnp.where(is_g & (lg == mg), lanef - N_EXPERTS, 1e9), axis=-1, keepdims=True)
    pg_sel = 1.0 / jnp.sum(jnp.where(is_g, jnp.exp(lg - mg), 0.0), axis=-1, keepdims=True)
    lo = gidx * EXP_PER_GROUP
    in_grp = (lanef >= lo) & (lanef < lo + EXP_PER_GROUP)
    t1 = jnp.max(jnp.where(in_grp, lg, neg), axis=-1, keepdims=True)
    i1 = jnp.min(jnp.where(in_grp & (lg == t1), lanef, 1e9), axis=-1, keepdims=True)
    rest = in_grp & (lanef != i1)
    t2 = jnp.max(jnp.where(rest, lg, neg), axis=-1, keepdims=True)
    i2 = jnp.min(jnp.where(rest & (lg == t2), lanef, 1e9), axis=-1, keepdims=True)
    e2 = jnp.exp(t2 - t1)
    w1 = pg_sel / (1.0 + e2)
    w2 = pg_sel * e2 / (1.0 + e2)
    gate_ref[...] = jnp.where(lanef == i1, w1, 0.0) + jnp.where(lanef == i2, w2, 0.0)
    tm = lg.shape[0]
    hit = (lanef == i1) | (lanef == i2)
    earlier = (lax.broadcasted_iota(jnp.int32, (tm, tm), 0)
               > lax.broadcasted_iota(jnp.int32, (tm, tm), 1))
    rank = _dot(earlier.astype(BF16), hit.astype(BF16))
    cnt = jnp.sum(hit.astype(F32), axis=0, keepdims=True)
    cnt8 = jnp.floor((cnt + 7.0) * 0.125) * 8.0
    lower_expert = (lax.broadcasted_iota(jnp.int32, (LANES, LANES), 0)
                    < lax.broadcasted_iota(jnp.int32, (LANES, LANES), 1))
    run_start = _masked_rowsum_t(jnp.broadcast_to(cnt8, (8, LANES)), lower_expert)[0:1]
    lpos = run_start + rank
    lp1 = jnp.sum(jnp.where(lanef == i1, lpos, 0.0), axis=-1, keepdims=True)
    lp2 = jnp.sum(jnp.where(lanef == i2, lpos, 0.0), axis=-1, keepdims=True)
    cnt_ref[0] = cnt.astype(jnp.int32)
    idx = jnp.where(lane == 0, i1, jnp.where(lane == 1, i2, jnp.where(lane == 2, lp1, lp2)))
    idx_ref[...] = idx.astype(jnp.int32)
    wts_ref[...] = jnp.where(lane == 0, w1, w2)


def _route(h2, p):
    m, d = h2.shape
    tm = min(m, MOE_TILE)
    tok = pl.BlockSpec((tm, LANES), lambda i: (i, 0))
    return pl.pallas_call(
        _route_body,
        grid=(m // tm,),
        in_specs=[pl.BlockSpec((tm, d), lambda i: (i, 0)),
                  _const_spec((d, LANES)), _const_spec((1, LANES))],
        out_specs=[tok, tok, tok, pl.BlockSpec((1, 1, LANES), lambda i: (i, 0, 0))],
        out_shape=[jax.ShapeDtypeStruct((m, LANES), F32),
                   jax.ShapeDtypeStruct((m, LANES), jnp.int32),
                   jax.ShapeDtypeStruct((m, LANES), F32),
                   jax.ShapeDtypeStruct((m // tm, 1, LANES), jnp.int32)],
        compiler_params=_params(("arbitrary",), 24 * 2**20),
        name="route",
    )(h2, p["w_route"], p["b_route"])


def _moe_body(h_ref, gate_ref, wg_ref, wu_ref, wd_ref, x1_ref, g2_ref, l2g_ref, l2b_ref,
              out_ref, acc_ref, xb_ref):
    e = pl.program_id(1)

    @pl.when(e == 0)
    def _():
        acc_ref[...] = jnp.zeros_like(acc_ref)
        xb_ref[...] = h_ref[...].astype(BF16)

    xb = xb_ref[...]
    pre = _dot(xb, wg_ref[0])
    hid = pre * jax.nn.sigmoid(pre) * _dot(xb, wu_ref[0])
    ye = _mm1(hid, wd_ref[0])
    gate = gate_ref[...]
    lane = lax.broadcasted_iota(jnp.int32, gate.shape, 1)
    ge = jnp.sum(jnp.where(lane == e, gate, 0.0), axis=-1, keepdims=True)
    acc_ref[...] += ge * ye

    @pl.when(e == pl.num_programs(1) - 1)
    def _():
        out_ref[...] = _layer_norm(ALPHA * x1_ref[...] + g2_ref[0] * acc_ref[...],
                                   l2g_ref[...], l2b_ref[...])


def _moe(seq_len, h2, gate, x1, g2, p):
    m, d = h2.shape
    tm = min(seq_len if g2.shape[1] == 1 else m, 1024)
    tok = lambda n: pl.BlockSpec((tm, n), lambda i, e: (i, 0))
    if g2.shape[1] == 1:
        tiles_per_seq = seq_len // tm
        g2_spec = pl.BlockSpec((1, 1, d), lambda i, e: (i // tiles_per_seq, 0, 0))
    else:
        g2_spec = pl.BlockSpec((1, tm, d), lambda i, e: (0, i, 0))
    est = 2 * tm * (3 * d + LANES) * 4 + tm * d * 6 + 4 * tm * d * 4 + 8 * 2**20
    return pl.pallas_call(
        _moe_body,
        grid=(m // tm, N_EXPERTS),
        in_specs=[tok(d), tok(LANES),
                  pl.BlockSpec((1, d, D_EXPERT), lambda i, e: (e, 0, 0)),
                  pl.BlockSpec((1, d, D_EXPERT), lambda i, e: (e, 0, 0)),
                  pl.BlockSpec((1, D_EXPERT, d), lambda i, e: (e, 0, 0)),
                  tok(d), g2_spec, _const_spec((1, d)), _const_spec((1, d))],
        out_specs=tok(d),
        out_shape=jax.ShapeDtypeStruct((m, d), F32),
        scratch_shapes=[pltpu.VMEM((tm, d), F32), pltpu.VMEM((tm, d), BF16)],
        compiler_params=_params(("arbitrary", "arbitrary"), est),
        name="moe_dense",
    )(h2, gate, p["w_exp_gate"], p["w_exp_up"], p["w_exp_down"], x1, g2, p["ln2_g"], p["ln2_b"])


MOE_TILE = 512
MOE_BLOCK = 256
MOE_RUN_ALIGN = 8
MOE_LOCAL_ROWS = 2 * MOE_TILE + 256
MOE_SLABS = tuple(2 ** k for k in range(9, 2, -1))


def _for_each_slab(run_ref, make_copy, fn):
    def one_run(e, c):
        dst = run_ref[0, 0, e]
        src = run_ref[0, 0, N_EXPERTS + e]
        n = run_ref[0, 0, 2 * N_EXPERTS + e]
        for slab in MOE_SLABS:
            off = n & (-2 * slab)

            @pl.when((n & slab) != 0)
            def _():
                fn(make_copy(pl.multiple_of(dst + off, MOE_RUN_ALIGN),
                             pl.multiple_of(src + off, MOE_RUN_ALIGN), slab))
        return c

    lax.fori_loop(0, N_EXPERTS, one_run, 0)


def _dispatch_body(tail_ref, run_ref, h_ref, idx_ref, xs_hbm, zero_buf, loc_buf, sem, zsem):
    nb_max = xs_hbm.shape[0] // MOE_BLOCK
    nt = MOE_TILE

    def zero_copy(row0):
        return pltpu.make_async_copy(
            zero_buf, xs_hbm.at[pl.ds(pl.multiple_of(row0, MOE_BLOCK), MOE_BLOCK)], zsem)

    @pl.when(pl.program_id(0) == 0)
    def _():
        zero_buf[...] = jnp.zeros_like(zero_buf)
        n_used = tail_ref[N_EXPERTS]

        def each_zero_copy(fn):
            for e in range(N_EXPERTS):
                @pl.when(tail_ref[e] >= 0)
                def _():
                    fn(zero_copy(tail_ref[e]))

            def unused(j, c):
                fn(zero_copy(j * MOE_BLOCK))
                return c

            lax.fori_loop(n_used, nb_max, unused, 0)

        each_zero_copy(lambda cp: cp.start())
        each_zero_copy(lambda cp: cp.wait())

    lp = idx_ref[...].astype(F32)
    eye = (lax.broadcasted_iota(jnp.int32, (nt, nt), 0)
           == lax.broadcasted_iota(jnp.int32, (nt, nt), 1))
    as_row = lambda col: jnp.sum(jnp.where(eye, col, 0.0), axis=0, keepdims=True)
    r = lax.broadcasted_iota(jnp.int32, (MOE_LOCAL_ROWS, nt), 0).astype(F32)
    pick = (r == as_row(lp[:, 2:3])) | (r == as_row(lp[:, 3:4]))
    loc_buf[...] = _dot(pick.astype(BF16), h_ref[...].astype(BF16))

    def make_copy(dst, src, rows):
        return pltpu.make_async_copy(loc_buf.at[pl.ds(src, rows)], xs_hbm.at[pl.ds(dst, rows)], sem)

    _for_each_slab(run_ref, make_copy, lambda cp: cp.start())
    _for_each_slab(run_ref, make_copy, lambda cp: cp.wait())


def _dispatch(h2, idx, runs, tail_start, n_rows):
    m, d = h2.shape
    nt = MOE_TILE
    grid_spec = pltpu.PrefetchScalarGridSpec(
        num_scalar_prefetch=1,
        grid=(m // nt,),
        in_specs=[pl.BlockSpec((1, 1, LANES), lambda i, tail: (i, 0, 0), memory_space=pltpu.SMEM),
                  pl.BlockSpec((nt, d), lambda i, tail: (i, 0)),
                  pl.BlockSpec((nt, LANES), lambda i, tail: (i, 0))],
        out_specs=pl.BlockSpec(memory_space=pl.ANY),
        scratch_shapes=[pltpu.VMEM((MOE_BLOCK, d), F32), pltpu.VMEM((MOE_LOCAL_ROWS, d), F32),
                        pltpu.SemaphoreType.DMA, pltpu.SemaphoreType.DMA],
    )
    return pl.pallas_call(
        _dispatch_body,
        grid_spec=grid_spec,
        out_shape=jax.ShapeDtypeStruct((n_rows, d), F32),
        compiler_params=_params(("arbitrary",), 32 * 2**20),
        name="moe_dispatch",
    )(tail_start, runs, h2, idx)


def _gmm_body(be_ref, nb_ref, x_ref, wg_ref, wu_ref, wd_ref, y_ref):
    j = pl.program_id(0)

    @pl.when(j < nb_ref[0])
    def _():
        xb = x_ref[...].astype(BF16)
        pre = _dot(xb, wg_ref[0])
        hid = pre * jax.nn.sigmoid(pre) * _dot(xb, wu_ref[0])
        y_ref[...] = _mm1(hid, wd_ref[0])

    @pl.when(j >= nb_ref[0])
    def _():
        y_ref[...] = jnp.zeros_like(y_ref)


def _gmm(xs, blk_expert, n_blocks, p):
    n_rows, d = xs.shape
    nb_max = n_rows // MOE_BLOCK
    live = lambda j, be, nb: jnp.minimum(j, nb[0] - 1)
    wspec = lambda shape: pl.BlockSpec(shape, lambda j, be, nb: (be[live(j, be, nb)], 0, 0))
    grid_spec = pltpu.PrefetchScalarGridSpec(
        num_scalar_prefetch=2,
        grid=(nb_max,),
        in_specs=[pl.BlockSpec((MOE_BLOCK, d), lambda j, be, nb: (live(j, be, nb), 0)),
                  wspec((1, d, D_EXPERT)), wspec((1, d, D_EXPERT)), wspec((1, D_EXPERT, d))],
        out_specs=pl.BlockSpec((MOE_BLOCK, d), lambda j, be, nb: (j, 0)),
    )
    return pl.pallas_call(
        _gmm_body,
        grid_spec=grid_spec,
        out_shape=jax.ShapeDtypeStruct((n_rows, d), F32),
        compiler_params=_params(("arbitrary",), 24 * 2**20),
        name="moe_gmm",
    )(blk_expert, n_blocks, xs, p["w_exp_gate"], p["w_exp_up"], p["w_exp_down"])


def _combine_body(run_ref, ys_hbm, idx_ref, wts_ref, x1_ref, g2_ref, l2g_ref, l2b_ref, out_ref,
                  loc_buf, sem):
    nt = MOE_TILE

    @pl.when(pl.program_id(0) == 0)
    def _():
        loc_buf[...] = jnp.zeros_like(loc_buf)

    def make_copy(dst, src, rows):
        return pltpu.make_async_copy(ys_hbm.at[pl.ds(dst, rows)], loc_buf.at[pl.ds(src, rows)], sem)

    _for_each_slab(run_ref, make_copy, lambda cp: cp.start())
    _for_each_slab(run_ref, make_copy, lambda cp: cp.wait())
    lp = idx_ref[...].astype(F32)
    w = wts_ref[...]
    c = lax.broadcasted_iota(jnp.int32, (nt, MOE_LOCAL_ROWS), 1).astype(F32)
    sel = (jnp.where(c == lp[:, 2:3], w[:, 0:1], 0.0)
           + jnp.where(c == lp[:, 3:4], w[:, 1:2], 0.0))
    moe = _mm3(sel, loc_buf[...])
    out_ref[...] = _layer_norm(ALPHA * x1_ref[...] + g2_ref[0] * moe, l2g_ref[...], l2b_ref[...])


def _combine(seq_len, ys, idx, wts, runs, x1, g2, p):
    m, d = x1.shape
    nt = MOE_TILE
    tiles_per_seq = seq_len // nt
    tok = lambda n: pl.BlockSpec((nt, n), lambda i: (i, 0))
    return pl.pallas_call(
        _combine_body,
        grid=(m // nt,),
        in_specs=[pl.BlockSpec((1, 1, LANES), lambda i: (i, 0, 0), memory_space=pltpu.SMEM),
                  pl.BlockSpec(memory_space=pl.ANY), tok(LANES), tok(LANES), tok(d),
                  pl.BlockSpec((1, 1, d), lambda i: (i // tiles_per_seq, 0, 0)),
                  _const_spec((1, d)), _const_spec((1, d))],
        out_specs=tok(d),
        out_shape=jax.ShapeDtypeStruct((m, d), F32),
        scratch_shapes=[pltpu.VMEM((MOE_LOCAL_ROWS, d), F32), pltpu.SemaphoreType.DMA],
        compiler_params=_params(("arbitrary",), 40 * 2**20),
        name="moe_combine",
    )(runs, ys, idx, wts, x1, g2, p["ln2_g"], p["ln2_b"])


def _moe_routed(seq_len, h2, idx, wts, tile_cnt, x1, g2, p):
    m, d = h2.shape
    blk = MOE_BLOCK
    n_tiles = m // MOE_TILE
    worst_rows = 2 * m + n_tiles * N_EXPERTS * (MOE_RUN_ALIGN - 1) + N_EXPERTS * (blk - 1)
    nb_max = -(-worst_rows // blk)
    cnt = tile_cnt[:, 0, :N_EXPERTS]
    run_len = ((cnt + MOE_RUN_ALIGN - 1) // MOE_RUN_ALIGN) * MOE_RUN_ALIGN
    local_row = jnp.cumsum(run_len, axis=1) - run_len
    rows_before = jnp.cumsum(run_len, axis=0) - run_len
    total = jnp.sum(run_len, axis=0)
    padded = ((total + blk - 1) // blk) * blk
    ends = jnp.cumsum(padded)
    starts = ends - padded
    n_blocks = (ends[-1] // blk).astype(jnp.int32).reshape(1)
    first_row = jnp.arange(nb_max, dtype=jnp.int32) * blk
    blk_expert = jnp.minimum(
        jnp.sum((first_row[:, None] >= ends[None, :]).astype(jnp.int32), axis=1),
        N_EXPERTS - 1).astype(jnp.int32)
    tail_start = jnp.concatenate(
        [jnp.where(padded > 0, ends - blk, -1).astype(jnp.int32), n_blocks])
    runs = jnp.concatenate(
        [starts[None, :] + rows_before, local_row, run_len,
         jnp.zeros((n_tiles, LANES - 3 * N_EXPERTS), jnp.int32)], axis=1).astype(jnp.int32)
    runs = runs.reshape(n_tiles, 1, LANES)
    xs = _dispatch(h2, idx, runs, tail_start, nb_max * blk)
    ys = _gmm(xs, blk_expert, n_blocks, p)
    return _combine(seq_len, ys, idx, wts, runs, x1, g2, p)


def _prep_layer(l, w_in, mu_shift, w0, w_decay_up, a0, w_iclr_up, k_k, k_a, r_k,
                lnx_g, lnx_b, lnv_g, lnv_b, w_spatial, b_spatial, w_branch_a, w_branch_b,
                w_out, ln1_g, ln1_b, w_route_group, b_route_group, w_route_expert,
                b_route_expert, w_exp_gate, w_exp_up, w_exp_down, ln2_g, ln2_b):
    d = D_MODEL
    pad_a = N_SHIFT_PAD - N_SHIFT
    wi = w_in[l]
    w_in_p = jnp.concatenate(
        [wi[:, :N_SHIFT], jnp.zeros((d, pad_a), F32), wi[:, N_SHIFT:]], axis=1).astype(BF16)
    mu = jnp.concatenate([mu_shift[l], jnp.zeros((pad_a,), F32)])[None]
    lora = jnp.zeros((LANES, 2 * D_A), F32)
    lora = lora.at[:R_LORA, :D_A].set(w_decay_up[l]).at[R_LORA:2 * R_LORA, D_A:].set(w_iclr_up[l])
    seg = jnp.arange(D_A) // HEAD
    row1 = lambda x: x.reshape(1, -1)
    gsz = D_B // N_GROUPS_B
    w_route = jnp.concatenate(
        [w_route_expert[l], w_route_group[l],
         jnp.zeros((d, LANES - N_EXPERTS - N_ROUTE_GROUPS), F32)], axis=1)
    b_route = jnp.concatenate(
        [b_route_expert[l], b_route_group[l],
         jnp.zeros((LANES - N_EXPERTS - N_ROUTE_GROUPS,), F32)])[None]
    return dict(
        w_in=w_in_p, mu=mu, lora=lora,
        w0a0=jnp.concatenate([w0[l], a0[l]])[None],
        k_k=row1(k_k[l]), k_a=row1(k_a[l]), r_k=row1(r_k[l]),
        lnx_g=row1(lnx_g[l]), lnx_b=row1(lnx_b[l]),
        lnv_g=row1(lnv_g[l]), lnv_b=row1(lnv_b[l]),
        bd=(seg[:, None] == seg[None, :]).astype(BF16),
        w_spatial=w_spatial[l],
        b_spatial_full=jnp.repeat(b_spatial[l].T, gsz, axis=1),
        ws_row=jnp.repeat(w_spatial[l][:, 0, 0], gsz)[None],
        bs_row=jnp.repeat(b_spatial[l][:, 0], gsz)[None],
        w_branch_a=w_branch_a[l].astype(BF16), w_branch_b=w_branch_b[l].astype(BF16),
        w_out=w_out[l].astype(BF16), ln1_g=row1(ln1_g[l]), ln1_b=row1(ln1_b[l]),
        w_route=w_route, b_route=b_route,
        w_exp_gate=w_exp_gate[l].astype(BF16), w_exp_up=w_exp_up[l].astype(BF16),
        w_exp_down=w_exp_down[l].astype(BF16), ln2_g=row1(ln2_g[l]), ln2_b=row1(ln2_b[l]),
    )


def _trunk(seq_mode, x, mods, wkv_in, shift_in, preps):
    b, t, d = x.shape
    wkv_out, shift_out, v_out = [], [], []
    for l in range(DEPTH):
        p = preps[l]
        sh1, sc1, g1, sh2, sc2, g2 = mods[l]
        if seq_mode:
            zprev = jnp.zeros((b, 1, N_SHIFT_PAD), F32) if shift_in is None else shift_in[l]
        else:
            zprev = _matmul(shift_in[l], p["w_in"][:, :N_SHIFT_PAD]).reshape(b, t, N_SHIFT_PAD)
        r, lw, kp, v, kap, bb, u, vg, ga, gb, hl = _in_stage(seq_mode, x, sc1, sh1, zprev, p)
        if seq_mode:
            o, s_new = _wkv_seq(r, lw, kp, v, kap, bb, wkv_in[l])
        else:
            flat = lambda a: a.reshape(t, D_A)
            o, s_new = _wkv_step(flat(r), flat(lw), flat(kp), flat(v), flat(kap), flat(bb),
                                 wkv_in[l])
            o = o.reshape(b, t, D_A)
        x1, h2 = _merge_stage(seq_mode, x, o, r, kp, v, u, vg, ga, gb, g1, sc2, sh2, p)
        m = b * t
        h2f = h2.reshape(m, d)
        gate, idx, wts, tile_cnt = _route(h2f, p)
        if seq_mode:
            x = _moe_routed(t, h2f, idx, wts, tile_cnt, x1.reshape(m, d), g2, p)
        else:
            x = _moe(t, h2f, gate, x1.reshape(m, d), g2, p)
        x = x.reshape(b, t, d)
        wkv_out.append(s_new)
        shift_out.append(hl)
        v_out.append(vg)
    return x, wkv_out, shift_out, v_out


def kernel(x_prompt, x_sample, c_prompt, c_sample, state_wkv, state_shift, w_ada, b_ada, w_in, mu_shift, w0, w_decay_up, a0, w_iclr_up, k_k, k_a, r_k, lnx_g, lnx_b, lnv_g, lnv_b, w_spatial, b_spatial, w_branch_a, w_branch_b, w_out, ln1_g, ln1_b, w_route_group, b_route_group, w_route_expert, b_route_expert, w_exp_gate, w_exp_up, w_exp_down, ln2_g, ln2_b):
    bp, tp, d = x_prompt.shape
    bs = x_sample.shape[0]
    layer_params = (w_in, mu_shift, w0, w_decay_up, a0, w_iclr_up, k_k, k_a, r_k, lnx_g,
                    lnx_b, lnv_g, lnv_b, w_spatial, b_spatial, w_branch_a, w_branch_b, w_out,
                    ln1_g, ln1_b, w_route_group, b_route_group, w_route_expert,
                    b_route_expert, w_exp_gate, w_exp_up, w_exp_down, ln2_g, ln2_b)
    preps = [_prep_layer(l, *layer_params) for l in range(DEPTH)]
    mod_all = _ada(jnp.concatenate([c_prompt, c_sample], axis=0), w_ada, b_ada)
    mods_p, mods_s = [], []
    for l in range(DEPTH):
        parts = jnp.split(mod_all[l], 6, axis=-1)
        mods_p.append([q[:bp].reshape(bp, 1, d) for q in parts])
        mods_s.append([q[bp:].reshape(1, bs, d) for q in parts])

    wkv0 = jnp.zeros((DEPTH, bp, N_HEADS, HEAD, HEAD), F32)
    y_p, wkv_p, shift_p, _ = _trunk(True, x_prompt, mods_p, wkv0, None, preps)
    y_s, wkv_s, shift_s, v_s = _trunk(False, x_sample.reshape(1, bs, d), mods_s, state_wkv,
                                      state_shift, preps)
    return (y_p,
            y_s.reshape(bs, 1, d),
            jnp.stack(wkv_p),
            jnp.stack([s.reshape(bp, d) for s in shift_p]),
            jnp.stack(wkv_s),
            jnp.stack([s.reshape(bs, d) for s in shift_s]),
            jnp.stack([q.reshape(bs, 1, D_B) for q in v_s]))
```

```python
import functools

import jax
import jax.numpy as jnp
from jax import lax
from jax.experimental import pallas as pl
from jax.experimental.pallas import tpu as pltpu

F32 = jnp.float32
BF16 = jnp.bfloat16

D_MODEL = 1024
DEPTH = 2
HEAD = 64
N_HEADS = 8
D_A = N_HEADS * HEAD
R_LORA = 32
CHUNK = 128
N_GROUPS_B = 8
D_B = 512
N_SHIFT = 3 * D_A + 2 * R_LORA
N_ROUTE_GROUPS = 4
EXP_PER_GROUP = 8
N_EXPERTS = N_ROUTE_GROUPS * EXP_PER_GROUP
D_EXPERT = 256
ALPHA = (2 * DEPTH) ** 0.25
LN_EPS = 1e-5
GN_EPS = 64e-5

LANES = 128
N_SHIFT_PAD = 13 * LANES
COL_U = N_SHIFT_PAD
COL_VG = COL_U + D_B
COL_GA = COL_VG + D_B
COL_GB = COL_GA + D_MODEL
N_IN_PAD = COL_GB + D_MODEL
WKV_CHUNK = 64
VMEM_CAP_BYTES = 60000 * 1024

_NN = (((1,), (0,)), ((), ()))
_B_NT = (((2,), (2,)), ((0,), (0,)))
_B_NN = (((2,), (1,)), ((0,), (0,)))
_B_TN = (((1,), (1,)), ((0,), (0,)))


def _dot(a, b, dims=_NN):
    return lax.dot_general(a, b, dims, preferred_element_type=F32)


def _split2(x):
    hi = x.astype(BF16)
    lo = (x - hi.astype(F32)).astype(BF16)
    return hi, lo


def _mm1(a, b, dims=_NN):
    return _dot(a.astype(BF16), b.astype(BF16), dims)


def _mm3(a, b, dims=_NN):
    ah, al = _split2(a)
    bh, bl = _split2(b)
    return _dot(ah, bh, dims) + (_dot(ah, bl, dims) + _dot(al, bh, dims))


def _segsum(x, bd):
    hi, lo = _split2(x)
    return _dot(hi, bd) + _dot(lo, bd)


def _layer_norm(x, g, b):
    mu = jnp.mean(x, axis=-1, keepdims=True)
    d = x - mu
    var = jnp.mean(d * d, axis=-1, keepdims=True)
    return d * lax.rsqrt(var + LN_EPS) * g + b


def _gelu(x):
    return 0.5 * x * (1.0 + lax.erf(x * 0.7071067811865476))


def _params(sem, est_bytes):
    limit = int(min(VMEM_CAP_BYTES, max(est_bytes, 16 * 1024 * 1024)))
    return pltpu.CompilerParams(dimension_semantics=sem, vmem_limit_bytes=limit)


def _const_spec(shape, single_buffer=False):
    nd = len(shape)
    if single_buffer:
        return pl.BlockSpec(shape, lambda *_: (0,) * nd, pipeline_mode=pl.Buffered(1))
    return pl.BlockSpec(shape, lambda *_: (0,) * nd)


def _ada_body(c_ref, w_ref, b_ref, o_ref):
    c = c_ref[...]
    s = c * jax.nn.sigmoid(c)
    o_ref[0] = _mm3(s, w_ref[0]) + b_ref[0]


def _ada(c_all, w_ada, b_ada):
    depth, d, n6 = w_ada.shape
    m = c_all.shape[0]
    tn = 512
    return pl.pallas_call(
        _ada_body,
        grid=(depth, n6 // tn),
        in_specs=[
            pl.BlockSpec((m, d), lambda l, j: (0, 0)),
            pl.BlockSpec((1, d, tn), lambda l, j: (l, 0, j)),
            pl.BlockSpec((1, 1, tn), lambda l, j: (l, 0, j)),
        ],
        out_specs=pl.BlockSpec((1, m, tn), lambda l, j: (l, 0, j)),
        out_shape=jax.ShapeDtypeStruct((depth, m, n6), F32),
        compiler_params=_params(("arbitrary", "arbitrary"), 24 * 2**20),
        name="ada_mod",
    )(c_all, w_ada, b_ada.reshape(depth, 1, n6))


def _mm_body(x_ref, w_ref, o_ref):
    o_ref[...] = _mm1(x_ref[...], w_ref[...])


def _matmul(x, w):
    m, k = x.shape
    n = w.shape[1]
    return pl.pallas_call(
        _mm_body,
        grid=(1,),
        in_specs=[_const_spec((m, k)), _const_spec((k, n))],
        out_specs=_const_spec((m, n)),
        out_shape=jax.ShapeDtypeStruct((m, n), F32),
        compiler_params=_params(("arbitrary",), 24 * 2**20),
        name="shift_proj",
    )(x, w)


def _in_body(seq_mode, tm, x_ref, sc_ref, sh_ref, w_ref, mu_ref, zp_ref, lora_ref,
             w0a0_ref, kk_ref, ka_ref, lng_ref, lnb_ref, bd_ref,
             r_ref, lw_ref, kp_ref, v_ref, kap_ref, bb_ref, u_ref, vg_ref,
             ga_ref, gb_ref, hl_ref, carry_ref):
    h = x_ref[0] * (1.0 + sc_ref[0]) + sh_ref[0]
    if seq_mode:
        hl_ref[0] = h[tm - 1:tm, :]
    else:
        hl_ref[0] = h
    hb = h.astype(BF16)
    proj = lambda lo, hi: _dot(hb, w_ref[:, lo:hi])
    za = proj(0, N_SHIFT_PAD)
    if seq_mode:
        @pl.when(pl.program_id(1) == 0)
        def _():
            carry_ref[...] = zp_ref[0]

        row = lax.broadcasted_iota(jnp.int32, za.shape, 0)
        prev = jnp.where(row == 0, carry_ref[...], pltpu.roll(za, 1, 0))
        carry_ref[...] = za[tm - 1:tm, :]
    else:
        prev = zp_ref[0]
    mix = za + mu_ref[...] * (prev - za)
    r = mix[:, 0:D_A]
    k = mix[:, D_A:2 * D_A]
    v = mix[:, 2 * D_A:3 * D_A]
    xwa = mix[:, 3 * D_A:N_SHIFT_PAD]
    lane = lax.broadcasted_iota(jnp.int32, xwa.shape, 1)
    lora_in = jnp.where(lane < R_LORA, jnp.tanh(xwa), xwa)
    pre = w0a0_ref[...] + _mm3(lora_in, lora_ref[...])
    yw = -pre[:, :D_A]
    softplus = jnp.maximum(yw, 0.0) + jnp.log1p(jnp.exp(-jnp.abs(yw)))
    lw = -jnp.exp(-softplus - 0.5)
    a = jax.nn.sigmoid(pre[:, D_A:])
    bd = bd_ref[...]
    kk = k * kk_ref[...]
    kap = kk / jnp.maximum(jnp.sqrt(_segsum(kk * kk, bd)), 1e-12)
    r_ref[0] = r
    lw_ref[0] = lw
    kp_ref[0] = k * (1.0 + (a - 1.0) * ka_ref[...])
    v_ref[0] = v
    kap_ref[0] = kap
    bb_ref[0] = kap * a
    u_ref[0] = _gelu(proj(COL_U, COL_VG))
    vg_ref[0] = _layer_norm(_gelu(proj(COL_VG, COL_GA)), lng_ref[...], lnb_ref[...])
    ga_ref[0] = jax.nn.sigmoid(proj(COL_GA, COL_GB))
    gb_ref[0] = jax.nn.sigmoid(proj(COL_GB, N_IN_PAD))


def _in_stage(seq_mode, x, sc, sh, zprev, p):
    b, t, d = x.shape
    tm = 256 if seq_mode else t
    tmod = 1 if seq_mode else tm
    grid = (b, t // tm)
    tok = lambda n: pl.BlockSpec((1, tm, n), lambda i, j: (i, j, 0))
    mod = pl.BlockSpec((1, tmod, d), lambda i, j: (i, j if not seq_mode else 0, 0))
    zp_spec = (pl.BlockSpec((1, 1, N_SHIFT_PAD), lambda i, j: (i, 0, 0)) if seq_mode
               else tok(N_SHIFT_PAD))
    hl_spec = (pl.BlockSpec((1, 1, d), lambda i, j: (i, 0, 0)) if seq_mode else tok(d))
    hl_shape = (b, 1, d) if seq_mode else (b, t, d)
    out_cols = [D_A] * 6 + [D_B] * 2 + [d] * 2
    est = (2 * tm * (d + N_SHIFT_PAD + sum(out_cols) + d) * 4 + 2 * d * N_IN_PAD * 2
           + 3 * tm * N_IN_PAD * 4 + 4 * 2**20)
    outs = pl.pallas_call(
        functools.partial(_in_body, seq_mode, tm),
        grid=grid,
        in_specs=[tok(d), mod, mod,
                  _const_spec((d, N_IN_PAD), True), _const_spec((1, N_SHIFT_PAD)), zp_spec,
                  _const_spec((LANES, 2 * D_A)), _const_spec((1, 2 * D_A)),
                  _const_spec((1, D_A)), _const_spec((1, D_A)),
                  _const_spec((1, D_B)), _const_spec((1, D_B)),
                  _const_spec((D_A, D_A))],
        out_specs=[tok(n) for n in out_cols] + [hl_spec],
        out_shape=[jax.ShapeDtypeStruct((b, t, n), F32) for n in out_cols]
        + [jax.ShapeDtypeStruct(hl_shape, F32)],
        scratch_shapes=[pltpu.VMEM((1, N_SHIFT_PAD), F32)],
        compiler_params=_params(("arbitrary", "arbitrary"), est),
        name="in_stage_seq" if seq_mode else "in_stage_row",
    )(x, sc, sh, p["w_in"], p["mu"], zprev, p["lora"], p["w0a0"], p["k_k"], p["k_a"],
      p["lnv_g"], p["lnv_b"], p["bd"])
    return outs


_WKV_3PASS = frozenset({"ark", "arkv", "h"})


def _wmm(name, a, b, dims):
    return (_mm3 if name in _WKV_3PASS else _mm1)(a, b, dims)


def _chunk_heads(x, nsub):
    c = WKV_CHUNK
    return jnp.stack([x[c * i:c * (i + 1), HEAD * h:HEAD * (h + 1)]
                      for i in range(nsub) for h in range(N_HEADS)], axis=0)


def _masked_rowsum(mask_bf, x):
    h1, h2 = _split2(x)
    return _dot(mask_bf, h1) + _dot(mask_bf, h2)


def _masked_rowsum_t(x, mask):
    h1, h2 = _split2(x)
    m = mask.astype(BF16)
    return _dot(h1, m) + _dot(h2, m)


def _wkv_seq_body(nsub, r_ref, lw_ref, kp_ref, v_ref, kap_ref, bb_ref, s0_ref,
                  o_ref, sout_ref, s_scr):
    c = WKV_CHUNK
    tc = nsub * c

    @pl.when(pl.program_id(1) == 0)
    def _():
        s_scr[...] = s0_ref[0]

    row = lax.broadcasted_iota(jnp.int32, (tc, tc), 0)
    col = lax.broadcasted_iota(jnp.int32, (tc, tc), 1)
    shift = c.bit_length() - 1
    same_chunk = (row >> shift) == (col >> shift)
    lw = lw_ref[0]
    g = _masked_rowsum((same_chunk & (row >= col)).astype(BF16), lw)
    g_end = _masked_rowsum(same_chunk.astype(BF16), lw)
    e_neg = jnp.exp(-g)
    e_end = jnp.exp(g_end - g)
    ch = functools.partial(_chunk_heads, nsub=nsub)
    k = kp_ref[0]
    b = bb_ref[0]
    kap_t = ch(kap_ref[0] * jnp.exp(g - lw))
    b_t = ch(b * e_neg)
    k_t = ch(k * e_neg)
    r_t = ch(r_ref[0] * jnp.exp(g))
    b_e = ch(b * e_end)
    k_e = ch(k * e_end)
    vv = ch(v_ref[0])
    decay_end = ch(jnp.exp(g_end))

    r64 = lax.broadcasted_iota(jnp.int32, (c, c), 0)
    c64 = lax.broadcasted_iota(jnp.int32, (c, c), 1)
    tri_incl = (r64 >= c64)[None]
    tri_strict = (r64 > c64)[None]
    eye = (r64 == c64)[None]

    l_b = jnp.where(tri_strict, _wmm("lb", kap_t, b_t, _B_NT), 0.0)
    l_k = jnp.where(tri_strict, _wmm("lk", kap_t, k_t, _B_NT), 0.0)
    a_rb = jnp.where(tri_incl, _wmm("arb", r_t, b_t, _B_NT), 0.0)
    a_rk = jnp.where(tri_incl, _wmm("ark", r_t, k_t, _B_NT), 0.0)
    m = -l_b
    t_inv = jnp.where(eye, 1.0, 0.0) + m
    m = _wmm("inv", m, m, _B_NN)
    span = 2
    while 2 * span < c:
        both = _wmm("inv", jnp.concatenate([m, t_inv], axis=1), m, _B_NN)
        m = both[:, :c]
        t_inv = t_inv + both[:, c:]
        span *= 2
    t_inv = t_inv + _wmm("inv", t_inv, m, _B_NN)
    a1 = -_wmm("ta", t_inv, kap_t, _B_NN)
    u0 = -_wmm("tu", t_inv, _wmm("lkv", l_k, vv, _B_NN), _B_NN)
    a2 = r_t + _wmm("arba", a_rb, a1, _B_NN)
    o0 = _wmm("arbu", a_rb, u0, _B_NN) + _wmm("arkv", a_rk, vv, _B_NN)
    g_mat = jnp.where(eye, decay_end, 0.0) + _wmm("g", a1, b_e, _B_TN)
    h_mat = _wmm("h", jnp.concatenate([u0, vv], axis=1),
                 jnp.concatenate([b_e, k_e], axis=1), _B_TN)

    s = s_scr[...]
    for i in range(nsub):
        hs = slice(N_HEADS * i, N_HEADS * (i + 1))
        o = _wmm("o", a2[hs], s, _B_NT) + o0[hs]
        s = _wmm("s", s, g_mat[hs], _B_NN) + h_mat[hs]
        for h in range(N_HEADS):
            o_ref[0, c * i:c * (i + 1), HEAD * h:HEAD * (h + 1)] = o[h]
    s_scr[...] = s

    @pl.when(pl.program_id(1) == pl.num_programs(1) - 1)
    def _():
        sout_ref[0] = s


def _wkv_seq(r, lw, kp, v, kap, bb, s0):
    b, t, _ = r.shape
    tc = 256
    nsub = tc // WKV_CHUNK
    tok = pl.BlockSpec((1, tc, D_A), lambda i, j: (i, j, 0))
    st = pl.BlockSpec((1, N_HEADS, HEAD, HEAD), lambda i, j: (i, 0, 0, 0))
    return pl.pallas_call(
        functools.partial(_wkv_seq_body, nsub),
        grid=(b, t // tc),
        in_specs=[tok] * 6 + [st],
        out_specs=[tok, st],
        out_shape=[jax.ShapeDtypeStruct((b, t, D_A), F32),
                   jax.ShapeDtypeStruct((b, N_HEADS, HEAD, HEAD), F32)],
        scratch_shapes=[pltpu.VMEM((N_HEADS, HEAD, HEAD), F32)],
        compiler_params=_params(("arbitrary", "arbitrary"), 32 * 2**20),
        name="wkv_seq",
    )(r, lw, kp, v, kap, bb, s0)


def _wkv_step_body(r_ref, lw_ref, kp_ref, v_ref, kap_ref, bb_ref, s0_ref, o_ref, sout_ref):
    eye = (lax.broadcasted_iota(jnp.int32, (HEAD, HEAD), 0)
           == lax.broadcasted_iota(jnp.int32, (HEAD, HEAD), 1))[None]
    for h in range(N_HEADS):
        hs = slice(HEAD * h, HEAD * (h + 1))
        row = lambda ref: ref[:, hs][:, None, :]
        s = s0_ref[:, h]
        sa = -jnp.sum(s * row(kap_ref), axis=-1, keepdims=True)
        v_col = jnp.sum(jnp.where(eye, row(v_ref), 0.0), axis=-1, keepdims=True)
        s_new = s * jnp.exp(row(lw_ref)) + sa * row(bb_ref) + v_col * row(kp_ref)
        sout_ref[:, h] = s_new
        o_col = jnp.sum(s_new * row(r_ref), axis=-1, keepdims=True)
        o_ref[:, hs] = jnp.sum(jnp.where(eye, o_col, 0.0), axis=1)


def _wkv_step(r, lw, kp, v, kap, bb, s_all, layer):
    m = r.shape[0]
    nb = 8
    tok = pl.BlockSpec((nb, D_A), lambda i: (i, 0))
    st = pl.BlockSpec((nb, N_HEADS, HEAD, HEAD), lambda i: (i, 0, 0, 0))
    st_in = pl.BlockSpec((None, nb, N_HEADS, HEAD, HEAD), lambda i: (layer, i, 0, 0, 0))
    return pl.pallas_call(
        _wkv_step_body,
        grid=(m // nb,),
        in_specs=[tok] * 6 + [st_in],
        out_specs=[tok, st],
        out_shape=[jax.ShapeDtypeStruct((m, D_A), F32),
                   jax.ShapeDtypeStruct((m, N_HEADS, HEAD, HEAD), F32)],
        compiler_params=_params(("arbitrary",), 24 * 2**20),
        name="wkv_step",
    )(r, lw, kp, v, kap, bb, s_all)


def _merge_body(seq_mode, tm, x_ref, o_ref, r_ref, kp_ref, v_ref, u_ref, vg_ref,
                ga_ref, gb_ref, g1_ref, sc2_ref, sh2_ref, lxg_ref, lxb_ref, rk_ref,
                bd_ref, ws_ref, bs_ref, pa_ref, pb_ref, wo_ref, l1g_ref, l1b_ref,
                x1_ref, h2_ref):
    bd = bd_ref[...]
    o = o_ref[0]
    inv_n = 1.0 / HEAD
    mu = _segsum(o, bd) * inv_n
    d = o - mu
    var = _segsum(d * d, bd) * inv_n
    on = d * lax.rsqrt(var + GN_EPS) * lxg_ref[...] + lxb_ref[...]
    v = v_ref[0]
    o_a = on + _segsum(r_ref[0] * kp_ref[0] * rk_ref[...], bd) * v
    vg = vg_ref[0]
    if seq_mode:
        row = lax.broadcasted_iota(jnp.int32, (CHUNK, CHUNK), 0)
        col = lax.broadcasted_iota(jnp.int32, (CHUNK, CHUNK), 1)
        lane = lax.broadcasted_iota(jnp.int32, (CHUNK, D_B), 1)
        gsz = D_B // N_GROUPS_B
        w_cat = jnp.concatenate(
            [jnp.where(row >= col, ws_ref[g], 0.0).astype(BF16) for g in range(N_GROUPS_B)], axis=1)
        pieces = []
        for j in range(tm // CHUNK):
            vc = vg[j * CHUNK:(j + 1) * CHUNK, :].astype(BF16)
            v_bd = jnp.concatenate(
                [jnp.where((lane >= g * gsz) & (lane < (g + 1) * gsz), vc, 0.0)
                 for g in range(N_GROUPS_B)], axis=0)
            pieces.append(_dot(w_cat, v_bd) + bs_ref[...])
        s = jnp.concatenate(pieces, axis=0) if len(pieces) > 1 else pieces[0]
    else:
        s = vg * ws_ref[...] + bs_ref[...]
    o_b = u_ref[0] * s
    y = _mm1(ga_ref[0] * _mm1(o_a, pa_ref[...]) + gb_ref[0] * _mm1(o_b, pb_ref[...]),
             wo_ref[...])
    x1 = _layer_norm(ALPHA * x_ref[0] + g1_ref[0] * y, l1g_ref[...], l1b_ref[...])
    x1_ref[0] = x1
    h2_ref[0] = x1 * (1.0 + sc2_ref[0]) + sh2_ref[0]


def _merge_stage(seq_mode, x, o, r, kp, v, u, vg, ga, gb, g1, sc2, sh2, p):
    b, t, d = x.shape
    tm = 256 if seq_mode else t
    tmod = 1 if seq_mode else tm
    tok = lambda n: pl.BlockSpec((1, tm, n), lambda i, j: (i, j, 0))
    mod = pl.BlockSpec((1, tmod, d), lambda i, j: (i, j if not seq_mode else 0, 0))
    ws, bs = (p["w_spatial"], p["b_spatial_full"]) if seq_mode else (p["ws_row"], p["bs_row"])
    est = 2 * tm * (2 * d + 7 * D_A + 2 * d + 2 * d) * 4 + 8 * tm * d * 4 + 16 * 2**20
    return pl.pallas_call(
        functools.partial(_merge_body, seq_mode, tm),
        grid=(b, t // tm),
        in_specs=[tok(d)] + [tok(D_A)] * 6 + [tok(d), tok(d), mod, mod, mod,
                  _const_spec((1, D_A)), _const_spec((1, D_A)), _const_spec((1, D_A)),
                  _const_spec((D_A, D_A)), _const_spec(ws.shape), _const_spec(bs.shape),
                  _const_spec((D_A, d)), _const_spec((D_B, d)), _const_spec((d, d)),
                  _const_spec((1, d)), _const_spec((1, d))],
        out_specs=[tok(d), tok(d)],
        out_shape=[jax.ShapeDtypeStruct((b, t, d), F32)] * 2,
        compiler_params=_params(("arbitrary", "arbitrary"), est),
        name="merge_seq" if seq_mode else "merge_row",
    )(x, o, r, kp, v, u, vg, ga, gb, g1, sc2, sh2, p["lnx_g"], p["lnx_b"], p["r_k"],
      p["bd"], ws, bs, p["w_branch_a"], p["w_branch_b"], p["w_out"], p["ln1_g"], p["ln1_b"])


def _route_body(h_ref, w_ref, b_ref, gate_ref, idx_ref, wts_ref, cnt_ref):
    lg = _mm3(h_ref[...], w_ref[...]) + b_ref[...]
    lane = lax.broadcasted_iota(jnp.int32, lg.shape, 1)
    lanef = lane.astype(F32)
    neg = -jnp.inf
    is_g = (lane >= N_EXPERTS) & (lane < N_EXPERTS + N_ROUTE_GROUPS)
    mg = jnp.max(jnp.where(is_g, lg, neg), axis=-1, keepdims=True)
    gidx = jnp.min(jnp.where(is_g & (lg == mg), lanef - N_EXPERTS, 1e9), axis=-1, keepdims=True)
    pg_sel = 1.0 / jnp.sum(jnp.where(is_g, jnp.exp(lg - mg), 0.0), axis=-1, keepdims=True)
    lo = gidx * EXP_PER_GROUP
    in_grp = (lanef >= lo) & (lanef < lo + EXP_PER_GROUP)
    t1 = jnp.max(jnp.where(in_grp, lg, neg), axis=-1, keepdims=True)
    i1 = jnp.min(jnp.where(in_grp & (lg == t1), lanef, 1e9), axis=-1, keepdims=True)
    rest = in_grp & (lanef != i1)
    t2 = jnp.max(jnp.where(rest, lg, neg), axis=-1, keepdims=True)
    i2 = jnp.min(jnp.where(rest & (lg == t2), lanef, 1e9), axis=-1, keepdims=True)
    e2 = jnp.exp(t2 - t1)
    w1 = pg_sel / (1.0 + e2)
    w2 = pg_sel * e2 / (1.0 + e2)
    gate_ref[...] = jnp.where(lanef == i1, w1, 0.0) + jnp.where(lanef == i2, w2, 0.0)
    tm = lg.shape[0]
    hit = (lanef == i1) | (lanef == i2)
    earlier = (lax.broadcasted_iota(jnp.int32, (tm, tm), 0)
               > lax.broadcasted_iota(jnp.int32, (tm, tm), 1))
    rank = _dot(earlier.astype(BF16), hit.astype(BF16))
    cnt = jnp.sum(hit.astype(F32), axis=0, keepdims=True)
    cnt8 = jnp.floor((cnt + 7.0) * 0.125) * 8.0
    lower_expert = (lax.broadcasted_iota(jnp.int32, (LANES, LANES), 0)
                    < lax.broadcasted_iota(jnp.int32, (LANES, LANES), 1))
    run_start = _masked_rowsum_t(jnp.broadcast_to(cnt8, (8, LANES)), lower_expert)[0:1]
    lpos = run_start + rank
    lp1 = jnp.sum(jnp.where(lanef == i1, lpos, 0.0), axis=-1, keepdims=True)
    lp2 = jnp.sum(jnp.where(lanef == i2, lpos, 0.0), axis=-1, keepdims=True)
    cnt_ref[0] = cnt.astype(jnp.int32)
    idx = jnp.where(lane == 0, i1, jnp.where(lane == 1, i2, jnp.where(lane == 2, lp1, lp2)))
    idx_ref[...] = idx.astype(jnp.int32)
    wts_ref[...] = jnp.where(lane == 0, w1, w2)


def _route(h2, p):
    m, d = h2.shape
    tm = min(m, MOE_TILE)
    tok = pl.BlockSpec((tm, LANES), lambda i: (i, 0))
    return pl.pallas_call(
        _route_body,
        grid=(m // tm,),
        in_specs=[pl.BlockSpec((tm, d), lambda i: (i, 0)),
                  _const_spec((d, LANES)), _const_spec((1, LANES))],
        out_specs=[tok, tok, tok, pl.BlockSpec((1, 1, LANES), lambda i: (i, 0, 0))],
        out_shape=[jax.ShapeDtypeStruct((m, LANES), F32),
                   jax.ShapeDtypeStruct((m, LANES), jnp.int32),
                   jax.ShapeDtypeStruct((m, LANES), F32),
                   jax.ShapeDtypeStruct((m // tm, 1, LANES), jnp.int32)],
        compiler_params=_params(("arbitrary",), 24 * 2**20),
        name="route",
    )(h2, p["w_route"], p["b_route"])


def _moe_body(h_ref, gate_ref, wg_ref, wu_ref, wd_ref, x1_ref, g2_ref, l2g_ref, l2b_ref,
              out_ref, acc_ref, xb_ref):
    e = pl.program_id(1)

    @pl.when(e == 0)
    def _():
        acc_ref[...] = jnp.zeros_like(acc_ref)
        xb_ref[...] = h_ref[...].astype(BF16)

    xb = xb_ref[...]
    pre = _dot(xb, wg_ref[0])
    hid = pre * jax.nn.sigmoid(pre) * _dot(xb, wu_ref[0])
    ye = _mm1(hid, wd_ref[0])
    gate = gate_ref[...]
    lane = lax.broadcasted_iota(jnp.int32, gate.shape, 1)
    ge = jnp.sum(jnp.where(lane == e, gate, 0.0), axis=-1, keepdims=True)
    acc_ref[...] += ge * ye

    @pl.when(e == pl.num_programs(1) - 1)
    def _():
        out_ref[...] = _layer_norm(ALPHA * x1_ref[...] + g2_ref[0] * acc_ref[...],
                                   l2g_ref[...], l2b_ref[...])


def _moe(seq_len, h2, gate, x1, g2, p):
    m, d = h2.shape
    tm = min(seq_len if g2.shape[1] == 1 else m, 1024)
    tok = lambda n: pl.BlockSpec((tm, n), lambda i, e: (i, 0))
    if g2.shape[1] == 1:
        tiles_per_seq = seq_len // tm
        g2_spec = pl.BlockSpec((1, 1, d), lambda i, e: (i // tiles_per_seq, 0, 0))
    else:
        g2_spec = pl.BlockSpec((1, tm, d), lambda i, e: (0, i, 0))
    est = 2 * tm * (3 * d + LANES) * 4 + tm * d * 6 + 4 * tm * d * 4 + 8 * 2**20
    return pl.pallas_call(
        _moe_body,
        grid=(m // tm, N_EXPERTS),
        in_specs=[tok(d), tok(LANES),
                  pl.BlockSpec((1, d, D_EXPERT), lambda i, e: (e, 0, 0)),
                  pl.BlockSpec((1, d, D_EXPERT), lambda i, e: (e, 0, 0)),
                  pl.BlockSpec((1, D_EXPERT, d), lambda i, e: (e, 0, 0)),
                  tok(d), g2_spec, _const_spec((1, d)), _const_spec((1, d))],
        out_specs=tok(d),
        out_shape=jax.ShapeDtypeStruct((m, d), F32),
        scratch_shapes=[pltpu.VMEM((tm, d), F32), pltpu.VMEM((tm, d), BF16)],
        compiler_params=_params(("arbitrary", "arbitrary"), est),
        name="moe_dense",
    )(h2, gate, p["w_exp_gate"], p["w_exp_up"], p["w_exp_down"], x1, g2, p["ln2_g"], p["ln2_b"])


MOE_TILE = 512
MOE_BLOCK = 256
MOE_RUN_ALIGN = 8
MOE_LOCAL_ROWS = 2 * MOE_TILE + 256
MOE_SLABS = tuple(2 ** k for k in range(9, 2, -1))


def _for_each_slab(run_ref, make_copy, fn):
    def one_run(e, c):
        dst = run_ref[0, 0, e]
        src = run_ref[0, 0, N_EXPERTS + e]
        n = run_ref[0, 0, 2 * N_EXPERTS + e]
        for slab in MOE_SLABS:
            off = n & (-2 * slab)

            @pl.when((n & slab) != 0)
            def _():
                fn(make_copy(pl.multiple_of(dst + off, MOE_RUN_ALIGN),
                             pl.multiple_of(src + off, MOE_RUN_ALIGN), slab))
        return c

    lax.fori_loop(0, N_EXPERTS, one_run, 0)


def _dispatch_body(tail_ref, run_ref, prev_run_ref, h_ref, idx_ref, xs_hbm, zero_buf, loc_buf,
                   sem, zsem):
    nb_max = xs_hbm.shape[0] // MOE_BLOCK
    nt = MOE_TILE

    def zero_copy(row0):
        return pltpu.make_async_copy(
            zero_buf, xs_hbm.at[pl.ds(pl.multiple_of(row0, MOE_BLOCK), MOE_BLOCK)], zsem)

    @pl.when(pl.program_id(0) == 0)
    def _():
        zero_buf[...] = jnp.zeros_like(zero_buf)
        n_used = tail_ref[N_EXPERTS]

        def each_zero_copy(fn):
            for e in range(N_EXPERTS):
                @pl.when(tail_ref[e] >= 0)
                def _():
                    fn(zero_copy(tail_ref[e]))

            def unused(j, c):
                fn(zero_copy(j * MOE_BLOCK))
                return c

            lax.fori_loop(n_used, nb_max, unused, 0)

        each_zero_copy(lambda cp: cp.start())
        each_zero_copy(lambda cp: cp.wait())

    lp = idx_ref[...].astype(F32)
    eye = (lax.broadcasted_iota(jnp.int32, (nt, nt), 0)
           == lax.broadcasted_iota(jnp.int32, (nt, nt), 1))
    as_row = lambda col: jnp.sum(jnp.where(eye, col, 0.0), axis=0, keepdims=True)
    r = lax.broadcasted_iota(jnp.int32, (MOE_LOCAL_ROWS, nt), 0).astype(F32)
    pick = (r == as_row(lp[:, 2:3])) | (r == as_row(lp[:, 3:4]))
    step = pl.program_id(0)
    slot = step & 1
    loc_buf[slot] = _dot(pick.astype(BF16), h_ref[...].astype(BF16))

    def copies_from(which):
        def make_copy(dst, src, rows):
            return pltpu.make_async_copy(loc_buf.at[which, pl.ds(src, rows)],
                                         xs_hbm.at[pl.ds(dst, rows)], sem.at[which])
        return make_copy

    _for_each_slab(run_ref, copies_from(slot), lambda cp: cp.start())

    @pl.when(step > 0)
    def _():
        _for_each_slab(prev_run_ref, copies_from(1 - slot), lambda cp: cp.wait())

    @pl.when(step == pl.num_programs(0) - 1)
    def _():
        _for_each_slab(run_ref, copies_from(slot), lambda cp: cp.wait())


def _dispatch(h2, idx, runs, tail_start, n_rows):
    m, d = h2.shape
    nt = MOE_TILE
    run_spec = lambda at: pl.BlockSpec((1, 1, LANES), lambda i, tail: (at(i), 0, 0),
                                       memory_space=pltpu.SMEM)
    grid_spec = pltpu.PrefetchScalarGridSpec(
        num_scalar_prefetch=1,
        grid=(m // nt,),
        in_specs=[run_spec(lambda i: i), run_spec(lambda i: jnp.maximum(i - 1, 0)),
                  pl.BlockSpec((nt, d), lambda i, tail: (i, 0)),
                  pl.BlockSpec((nt, LANES), lambda i, tail: (i, 0))],
        out_specs=pl.BlockSpec(memory_space=pl.ANY),
        scratch_shapes=[pltpu.VMEM((MOE_BLOCK, d), F32),
                        pltpu.VMEM((2, MOE_LOCAL_ROWS, d), F32),
                        pltpu.SemaphoreType.DMA((2,)), pltpu.SemaphoreType.DMA],
    )
    return pl.pallas_call(
        _dispatch_body,
        grid_spec=grid_spec,
        out_shape=jax.ShapeDtypeStruct((n_rows, d), F32),
        compiler_params=_params(("arbitrary",), 40 * 2**20),
        name="moe_dispatch",
    )(tail_start, runs, runs, h2, idx)


def _gmm_body(be_ref, nb_ref, x_ref, wg_ref, wu_ref, wd_ref, y_ref):
    j = pl.program_id(0)

    @pl.when(j < nb_ref[0])
    def _():
        xb = x_ref[...].astype(BF16)
        pre = _dot(xb, wg_ref[0])
        hid = pre * jax.nn.sigmoid(pre) * _dot(xb, wu_ref[0])
        y_ref[...] = _mm1(hid, wd_ref[0])

    @pl.when(j >= nb_ref[0])
    def _():
        y_ref[...] = jnp.zeros_like(y_ref)


def _gmm(xs, blk_expert, n_blocks, p):
    n_rows, d = xs.shape
    nb_max = n_rows // MOE_BLOCK
    live = lambda j, be, nb: jnp.minimum(j, nb[0] - 1)
    wspec = lambda shape: pl.BlockSpec(shape, lambda j, be, nb: (be[live(j, be, nb)], 0, 0))
    grid_spec = pltpu.PrefetchScalarGridSpec(
        num_scalar_prefetch=2,
        grid=(nb_max,),
        in_specs=[pl.BlockSpec((MOE_BLOCK, d), lambda j, be, nb: (live(j, be, nb), 0)),
                  wspec((1, d, D_EXPERT)), wspec((1, d, D_EXPERT)), wspec((1, D_EXPERT, d))],
        out_specs=pl.BlockSpec((MOE_BLOCK, d), lambda j, be, nb: (j, 0)),
    )
    return pl.pallas_call(
        _gmm_body,
        grid_spec=grid_spec,
        out_shape=jax.ShapeDtypeStruct((n_rows, d), F32),
        compiler_params=_params(("arbitrary",), 24 * 2**20),
        name="moe_gmm",
    )(blk_expert, n_blocks, xs, p["w_exp_gate"], p["w_exp_up"], p["w_exp_down"])


def _combine_body(run_ref, next_run_ref, ys_hbm, idx_ref, wts_ref, x1_ref, g2_ref, l2g_ref,
                  l2b_ref, out_ref, loc_buf, sem):
    nt = MOE_TILE
    step = pl.program_id(0)
    slot = step & 1

    def copies_into(which):
        def make_copy(dst, src, rows):
            return pltpu.make_async_copy(ys_hbm.at[pl.ds(dst, rows)],
                                         loc_buf.at[which, pl.ds(src, rows)], sem.at[which])
        return make_copy

    @pl.when(step == 0)
    def _():
        loc_buf[...] = jnp.zeros_like(loc_buf)
        _for_each_slab(run_ref, copies_into(slot), lambda cp: cp.start())

    @pl.when(step < pl.num_programs(0) - 1)
    def _():
        _for_each_slab(next_run_ref, copies_into(1 - slot), lambda cp: cp.start())

    _for_each_slab(run_ref, copies_into(slot), lambda cp: cp.wait())
    lp = idx_ref[...].astype(F32)
    w = wts_ref[...]
    c = lax.broadcasted_iota(jnp.int32, (nt, MOE_LOCAL_ROWS), 1).astype(F32)
    sel = (jnp.where(c == lp[:, 2:3], w[:, 0:1], 0.0)
           + jnp.where(c == lp[:, 3:4], w[:, 1:2], 0.0))
    moe = _mm3(sel, loc_buf[slot])
    out_ref[...] = _layer_norm(ALPHA * x1_ref[...] + g2_ref[0] * moe, l2g_ref[...], l2b_ref[...])


def _combine(seq_len, ys, idx, wts, runs, x1, g2, p):
    m, d = x1.shape
    nt = MOE_TILE
    tiles_per_seq = seq_len // nt
    tok = lambda n: pl.BlockSpec((nt, n), lambda i: (i, 0))
    n_tiles = m // nt
    run_spec = lambda at: pl.BlockSpec((1, 1, LANES), lambda i: (at(i), 0, 0),
                                       memory_space=pltpu.SMEM)
    return pl.pallas_call(
        _combine_body,
        grid=(n_tiles,),
        in_specs=[run_spec(lambda i: i), run_spec(lambda i: jnp.minimum(i + 1, n_tiles - 1)),
                  pl.BlockSpec(memory_space=pl.ANY), tok(LANES), tok(LANES), tok(d),
                  pl.BlockSpec((1, 1, d), lambda i: (i // tiles_per_seq, 0, 0)),
                  _const_spec((1, d)), _const_spec((1, d))],
        out_specs=tok(d),
        out_shape=jax.ShapeDtypeStruct((m, d), F32),
        scratch_shapes=[pltpu.VMEM((2, MOE_LOCAL_ROWS, d), F32), pltpu.SemaphoreType.DMA((2,))],
        compiler_params=_params(("arbitrary",), 48 * 2**20),
        name="moe_combine",
    )(runs, runs, ys, idx, wts, x1, g2, p["ln2_g"], p["ln2_b"])


def _moe_routed(seq_len, h2, idx, wts, tile_cnt, x1, g2, p):
    m, d = h2.shape
    blk = MOE_BLOCK
    n_tiles = m // MOE_TILE
    worst_rows = 2 * m + n_tiles * N_EXPERTS * (MOE_RUN_ALIGN - 1) + N_EXPERTS * (blk - 1)
    nb_max = -(-worst_rows // blk)
    cnt = tile_cnt[:, 0, :N_EXPERTS]
    run_len = ((cnt + MOE_RUN_ALIGN - 1) // MOE_RUN_ALIGN) * MOE_RUN_ALIGN
    local_row = jnp.cumsum(run_len, axis=1) - run_len
    rows_before = jnp.cumsum(run_len, axis=0) - run_len
    total = jnp.sum(run_len, axis=0)
    padded = ((total + blk - 1) // blk) * blk
    ends = jnp.cumsum(padded)
    starts = ends - padded
    n_blocks = (ends[-1] // blk).astype(jnp.int32).reshape(1)
    first_row = jnp.arange(nb_max, dtype=jnp.int32) * blk
    blk_expert = jnp.minimum(
        jnp.sum((first_row[:, None] >= ends[None, :]).astype(jnp.int32), axis=1),
        N_EXPERTS - 1).astype(jnp.int32)
    tail_start = jnp.concatenate(
        [jnp.where(padded > 0, ends - blk, -1).astype(jnp.int32), n_blocks])
    runs = jnp.concatenate(
        [starts[None, :] + rows_before, local_row, run_len,
         jnp.zeros((n_tiles, LANES - 3 * N_EXPERTS), jnp.int32)], axis=1).astype(jnp.int32)
    runs = runs.reshape(n_tiles, 1, LANES)
    xs = _dispatch(h2, idx, runs, tail_start, nb_max * blk)
    ys = _gmm(xs, blk_expert, n_blocks, p)
    return _combine(seq_len, ys, idx, wts, runs, x1, g2, p)


def _prep_layer(l, w_in, mu_shift, w0, w_decay_up, a0, w_iclr_up, k_k, k_a, r_k,
                lnx_g, lnx_b, lnv_g, lnv_b, w_spatial, b_spatial, w_branch_a, w_branch_b,
                w_out, ln1_g, ln1_b, w_route_group, b_route_group, w_route_expert,
                b_route_expert, w_exp_gate, w_exp_up, w_exp_down, ln2_g, ln2_b):
    d = D_MODEL
    pad_a = N_SHIFT_PAD - N_SHIFT
    wi = w_in[l]
    w_in_p = jnp.concatenate(
        [wi[:, :N_SHIFT], jnp.zeros((d, pad_a), F32), wi[:, N_SHIFT:]], axis=1).astype(BF16)
    mu = jnp.concatenate([mu_shift[l], jnp.zeros((pad_a,), F32)])[None]
    lora = jnp.zeros((LANES, 2 * D_A), F32)
    lora = lora.at[:R_LORA, :D_A].set(w_decay_up[l]).at[R_LORA:2 * R_LORA, D_A:].set(w_iclr_up[l])
    seg = jnp.arange(D_A) // HEAD
    row1 = lambda x: x.reshape(1, -1)
    gsz = D_B // N_GROUPS_B
    w_route = jnp.concatenate(
        [w_route_expert[l], w_route_group[l],
         jnp.zeros((d, LANES - N_EXPERTS - N_ROUTE_GROUPS), F32)], axis=1)
    b_route = jnp.concatenate(
        [b_route_expert[l], b_route_group[l],
         jnp.zeros((LANES - N_EXPERTS - N_ROUTE_GROUPS,), F32)])[None]
    return dict(
        w_in=w_in_p, mu=mu, lora=lora,
        w0a0=jnp.concatenate([w0[l], a0[l]])[None],
        k_k=row1(k_k[l]), k_a=row1(k_a[l]), r_k=row1(r_k[l]),
        lnx_g=row1(lnx_g[l]), lnx_b=row1(lnx_b[l]),
        lnv_g=row1(lnv_g[l]), lnv_b=row1(lnv_b[l]),
        bd=(seg[:, None] == seg[None, :]).astype(BF16),
        w_spatial=w_spatial[l],
        b_spatial_full=jnp.repeat(b_spatial[l].T, gsz, axis=1),
        ws_row=jnp.repeat(w_spatial[l][:, 0, 0], gsz)[None],
        bs_row=jnp.repeat(b_spatial[l][:, 0], gsz)[None],
        w_branch_a=w_branch_a[l].astype(BF16), w_branch_b=w_branch_b[l].astype(BF16),
        w_out=w_out[l].astype(BF16), ln1_g=row1(ln1_g[l]), ln1_b=row1(ln1_b[l]),
        w_route=w_route, b_route=b_route,
        w_exp_gate=w_exp_gate[l].astype(BF16), w_exp_up=w_exp_up[l].astype(BF16),
        w_exp_down=w_exp_down[l].astype(BF16), ln2_g=row1(ln2_g[l]), ln2_b=row1(ln2_b[l]),
    )


def _trunk(seq_mode, x, mods, wkv_in, shift_in, preps):
    b, t, d = x.shape
    wkv_out, shift_out, v_out = [], [], []
    for l in range(DEPTH):
        p = preps[l]
        sh1, sc1, g1, sh2, sc2, g2 = mods[l]
        if seq_mode:
            zprev = jnp.zeros((b, 1, N_SHIFT_PAD), F32) if shift_in is None else shift_in[l]
        else:
            zprev = _matmul(shift_in[l], p["w_in"][:, :N_SHIFT_PAD]).reshape(b, t, N_SHIFT_PAD)
        r, lw, kp, v, kap, bb, u, vg, ga, gb, hl = _in_stage(seq_mode, x, sc1, sh1, zprev, p)
        if seq_mode:
            o, s_new = _wkv_seq(r, lw, kp, v, kap, bb, wkv_in[l])
        else:
            flat = lambda a: a.reshape(t, D_A)
            o, s_new = _wkv_step(flat(r), flat(lw), flat(kp), flat(v), flat(kap), flat(bb),
                                 wkv_in, l)
            o = o.reshape(b, t, D_A)
        x1, h2 = _merge_stage(seq_mode, x, o, r, kp, v, u, vg, ga, gb, g1, sc2, sh2, p)
        m = b * t
        h2f = h2.reshape(m, d)
        gate, idx, wts, tile_cnt = _route(h2f, p)
        if seq_mode:
            x = _moe_routed(t, h2f, idx, wts, tile_cnt, x1.reshape(m, d), g2, p)
        else:
            x = _moe(t, h2f, gate, x1.reshape(m, d), g2, p)
        x = x.reshape(b, t, d)
        wkv_out.append(s_new)
        shift_out.append(hl)
        v_out.append(vg)
    return x, wkv_out, shift_out, v_out


def kernel(x_prompt, x_sample, c_prompt, c_sample, state_wkv, state_shift, w_ada, b_ada, w_in, mu_shift, w0, w_decay_up, a0, w_iclr_up, k_k, k_a, r_k, lnx_g, lnx_b, lnv_g, lnv_b, w_spatial, b_spatial, w_branch_a, w_branch_b, w_out, ln1_g, ln1_b, w_route_group, b_route_group, w_route_expert, b_route_expert, w_exp_gate, w_exp_up, w_exp_down, ln2_g, ln2_b):
    bp, tp, d = x_prompt.shape
    bs = x_sample.shape[0]
    layer_params = (w_in, mu_shift, w0, w_decay_up, a0, w_iclr_up, k_k, k_a, r_k, lnx_g,
                    lnx_b, lnv_g, lnv_b, w_spatial, b_spatial, w_branch_a, w_branch_b, w_out,
                    ln1_g, ln1_b, w_route_group, b_route_group, w_route_expert,
                    b_route_expert, w_exp_gate, w_exp_up, w_exp_down, ln2_g, ln2_b)
    preps = [_prep_layer(l, *layer_params) for l in range(DEPTH)]
    mod_all = _ada(jnp.concatenate([c_prompt, c_sample], axis=0), w_ada, b_ada)
    mods_p, mods_s = [], []
    for l in range(DEPTH):
        parts = jnp.split(mod_all[l], 6, axis=-1)
        mods_p.append([q[:bp].reshape(bp, 1, d) for q in parts])
        mods_s.append([q[bp:].reshape(1, bs, d) for q in parts])

    wkv0 = jnp.zeros((DEPTH, bp, N_HEADS, HEAD, HEAD), F32)
    y_p, wkv_p, shift_p, _ = _trunk(True, x_prompt, mods_p, wkv0, None, preps)
    y_s, wkv_s, shift_s, v_s = _trunk(False, x_sample.reshape(1, bs, d), mods_s, state_wkv,
                                      state_shift, preps)
    return (y_p,
            y_s.reshape(bs, 1, d),
            jnp.stack(wkv_p),
            jnp.stack([s.reshape(bp, d) for s in shift_p]),
            jnp.stack(wkv_s),
            jnp.stack([s.reshape(bs, d) for s in shift_s]),
            jnp.stack([q.reshape(bs, 1, D_B) for q in v_s]))
```

```python
import functools

import jax
import jax.numpy as jnp
from jax import lax
from jax.experimental import pallas as pl
from jax.experimental.pallas import tpu as pltpu

F32 = jnp.float32
BF16 = jnp.bfloat16

D_MODEL = 1024
DEPTH = 2
HEAD = 64
N_HEADS = 8
D_A = N_HEADS * HEAD
R_LORA = 32
CHUNK = 128
N_GROUPS_B = 8
D_B = 512
N_SHIFT = 3 * D_A + 2 * R_LORA
N_ROUTE_GROUPS = 4
EXP_PER_GROUP = 8
N_EXPERTS = N_ROUTE_GROUPS * EXP_PER_GROUP
D_EXPERT = 256
ALPHA = (2 * DEPTH) ** 0.25
LN_EPS = 1e-5
GN_EPS = 64e-5

LANES = 128
N_SHIFT_PAD = 13 * LANES
COL_U = N_SHIFT_PAD
COL_VG = COL_U + D_B
COL_GA = COL_VG + D_B
COL_GB = COL_GA + D_MODEL
N_IN_PAD = COL_GB + D_MODEL
WKV_CHUNK = 64
VMEM_CAP_BYTES = 60000 * 1024

_NN = (((1,), (0,)), ((), ()))
_B_NT = (((2,), (2,)), ((0,), (0,)))
_B_NN = (((2,), (1,)), ((0,), (0,)))
_B_TN = (((1,), (1,)), ((0,), (0,)))


def _dot(a, b, dims=_NN):
    return lax.dot_general(a, b, dims, preferred_element_type=F32)


def _split2(x):
    hi = x.astype(BF16)
    lo = (x - hi.astype(F32)).astype(BF16)
    return hi, lo


def _mm1(a, b, dims=_NN):
    return _dot(a.astype(BF16), b.astype(BF16), dims)


def _mm3(a, b, dims=_NN):
    ah, al = _split2(a)
    bh, bl = _split2(b)
    return _dot(ah, bh, dims) + (_dot(ah, bl, dims) + _dot(al, bh, dims))


def _segsum(x, bd):
    hi, lo = _split2(x)
    return _dot(hi, bd) + _dot(lo, bd)


def _layer_norm(x, g, b):
    mu = jnp.mean(x, axis=-1, keepdims=True)
    d = x - mu
    var = jnp.mean(d * d, axis=-1, keepdims=True)
    return d * lax.rsqrt(var + LN_EPS) * g + b


def _gelu(x):
    return 0.5 * x * (1.0 + lax.erf(x * 0.7071067811865476))


def _params(sem, est_bytes):
    limit = int(min(VMEM_CAP_BYTES, max(est_bytes, 16 * 1024 * 1024)))
    return pltpu.CompilerParams(dimension_semantics=sem, vmem_limit_bytes=limit)


def _const_spec(shape, single_buffer=False):
    nd = len(shape)
    if single_buffer:
        return pl.BlockSpec(shape, lambda *_: (0,) * nd, pipeline_mode=pl.Buffered(1))
    return pl.BlockSpec(shape, lambda *_: (0,) * nd)


def _ada_body(c_ref, w_ref, b_ref, o_ref):
    c = c_ref[...]
    s = c * jax.nn.sigmoid(c)
    o_ref[0] = _mm3(s, w_ref[0]) + b_ref[0]


def _ada(c_all, w_ada, b_ada):
    depth, d, n6 = w_ada.shape
    m = c_all.shape[0]
    tn = 512
    return pl.pallas_call(
        _ada_body,
        grid=(depth, n6 // tn),
        in_specs=[
            pl.BlockSpec((m, d), lambda l, j: (0, 0)),
            pl.BlockSpec((1, d, tn), lambda l, j: (l, 0, j)),
            pl.BlockSpec((1, 1, tn), lambda l, j: (l, 0, j)),
        ],
        out_specs=pl.BlockSpec((1, m, tn), lambda l, j: (l, 0, j)),
        out_shape=jax.ShapeDtypeStruct((depth, m, n6), F32),
        compiler_params=_params(("arbitrary", "arbitrary"), 24 * 2**20),
        name="ada_mod",
    )(c_all, w_ada, b_ada.reshape(depth, 1, n6))


def _mm_body(x_ref, w_ref, o_ref):
    o_ref[...] = _mm1(x_ref[...], w_ref[...])


def _matmul(x, w):
    m, k = x.shape
    n = w.shape[1]
    return pl.pallas_call(
        _mm_body,
        grid=(1,),
        in_specs=[_const_spec((m, k)), _const_spec((k, n))],
        out_specs=_const_spec((m, n)),
        out_shape=jax.ShapeDtypeStruct((m, n), F32),
        compiler_params=_params(("arbitrary",), 24 * 2**20),
        name="shift_proj",
    )(x, w)


def _in_body(seq_mode, tm, x_ref, sc_ref, sh_ref, w_ref, mu_ref, zp_ref, lora_ref,
             w0a0_ref, kk_ref, ka_ref, lng_ref, lnb_ref, bd_ref,
             r_ref, lw_ref, kp_ref, v_ref, kap_ref, bb_ref, u_ref, vg_ref,
             ga_ref, gb_ref, hl_ref, carry_ref):
    h = x_ref[0] * (1.0 + sc_ref[0]) + sh_ref[0]
    if seq_mode:
        hl_ref[0] = h[tm - 1:tm, :]
    else:
        hl_ref[0] = h
    hb = h.astype(BF16)
    proj = lambda lo, hi: _dot(hb, w_ref[:, lo:hi])
    za = proj(0, N_SHIFT_PAD)
    if seq_mode:
        @pl.when(pl.program_id(1) == 0)
        def _():
            carry_ref[...] = zp_ref[0]

        row = lax.broadcasted_iota(jnp.int32, za.shape, 0)
        prev = jnp.where(row == 0, carry_ref[...], pltpu.roll(za, 1, 0))
        carry_ref[...] = za[tm - 1:tm, :]
    else:
        prev = zp_ref[0]
    mix = za + mu_ref[...] * (prev - za)
    r = mix[:, 0:D_A]
    k = mix[:, D_A:2 * D_A]
    v = mix[:, 2 * D_A:3 * D_A]
    xwa = mix[:, 3 * D_A:N_SHIFT_PAD]
    lane = lax.broadcasted_iota(jnp.int32, xwa.shape, 1)
    lora_in = jnp.where(lane < R_LORA, jnp.tanh(xwa), xwa)
    pre = w0a0_ref[...] + _mm3(lora_in, lora_ref[...])
    yw = -pre[:, :D_A]
    softplus = jnp.maximum(yw, 0.0) + jnp.log1p(jnp.exp(-jnp.abs(yw)))
    lw = -jnp.exp(-softplus - 0.5)
    a = jax.nn.sigmoid(pre[:, D_A:])
    bd = bd_ref[...]
    kk = k * kk_ref[...]
    kap = kk / jnp.maximum(jnp.sqrt(_segsum(kk * kk, bd)), 1e-12)
    r_ref[0] = r
    lw_ref[0] = lw
    kp_ref[0] = k * (1.0 + (a - 1.0) * ka_ref[...])
    v_ref[0] = v
    kap_ref[0] = kap
    bb_ref[0] = kap * a
    u_ref[0] = _gelu(proj(COL_U, COL_VG))
    vg_ref[0] = _layer_norm(_gelu(proj(COL_VG, COL_GA)), lng_ref[...], lnb_ref[...])
    ga_ref[0] = jax.nn.sigmoid(proj(COL_GA, COL_GB))
    gb_ref[0] = jax.nn.sigmoid(proj(COL_GB, N_IN_PAD))


def _in_stage(seq_mode, x, sc, sh, zprev, p):
    b, t, d = x.shape
    tm = 256 if seq_mode else t
    tmod = 1 if seq_mode else tm
    grid = (b, t // tm)
    tok = lambda n: pl.BlockSpec((1, tm, n), lambda i, j: (i, j, 0))
    mod = pl.BlockSpec((1, tmod, d), lambda i, j: (i, j if not seq_mode else 0, 0))
    zp_spec = (pl.BlockSpec((1, 1, N_SHIFT_PAD), lambda i, j: (i, 0, 0)) if seq_mode
               else tok(N_SHIFT_PAD))
    hl_spec = (pl.BlockSpec((1, 1, d), lambda i, j: (i, 0, 0)) if seq_mode else tok(d))
    hl_shape = (b, 1, d) if seq_mode else (b, t, d)
    out_cols = [D_A] * 6 + [D_B] * 2 + [d] * 2
    est = (2 * tm * (d + N_SHIFT_PAD + sum(out_cols) + d) * 4 + 2 * d * N_IN_PAD * 2
           + 3 * tm * N_IN_PAD * 4 + 4 * 2**20)
    outs = pl.pallas_call(
        functools.partial(_in_body, seq_mode, tm),
        grid=grid,
        in_specs=[tok(d), mod, mod,
                  _const_spec((d, N_IN_PAD), True), _const_spec((1, N_SHIFT_PAD)), zp_spec,
                  _const_spec((LANES, 2 * D_A)), _const_spec((1, 2 * D_A)),
                  _const_spec((1, D_A)), _const_spec((1, D_A)),
                  _const_spec((1, D_B)), _const_spec((1, D_B)),
                  _const_spec((D_A, D_A))],
        out_specs=[tok(n) for n in out_cols] + [hl_spec],
        out_shape=[jax.ShapeDtypeStruct((b, t, n), F32) for n in out_cols]
        + [jax.ShapeDtypeStruct(hl_shape, F32)],
        scratch_shapes=[pltpu.VMEM((1, N_SHIFT_PAD), F32)],
        compiler_params=_params(("arbitrary", "arbitrary"), est),
        name="in_stage_seq" if seq_mode else "in_stage_row",
    )(x, sc, sh, p["w_in"], p["mu"], zprev, p["lora"], p["w0a0"], p["k_k"], p["k_a"],
      p["lnv_g"], p["lnv_b"], p["bd"])
    return outs


_WKV_3PASS = frozenset({"ark", "arkv", "h"})


def _wmm(name, a, b, dims):
    return (_mm3 if name in _WKV_3PASS else _mm1)(a, b, dims)


def _chunk_heads(x, nsub):
    c = WKV_CHUNK
    return jnp.stack([x[c * i:c * (i + 1), HEAD * h:HEAD * (h + 1)]
                      for i in range(nsub) for h in range(N_HEADS)], axis=0)


def _masked_rowsum(mask_bf, x):
    h1, h2 = _split2(x)
    return _dot(mask_bf, h1) + _dot(mask_bf, h2)


def _masked_rowsum_t(x, mask):
    h1, h2 = _split2(x)
    m = mask.astype(BF16)
    return _dot(h1, m) + _dot(h2, m)


def _wkv_seq_body(nsub, r_ref, lw_ref, kp_ref, v_ref, kap_ref, bb_ref, s0_ref,
                  o_ref, sout_ref, s_scr):
    c = WKV_CHUNK
    tc = nsub * c

    @pl.when(pl.program_id(1) == 0)
    def _():
        s_scr[...] = s0_ref[0]

    row = lax.broadcasted_iota(jnp.int32, (tc, tc), 0)
    col = lax.broadcasted_iota(jnp.int32, (tc, tc), 1)
    shift = c.bit_length() - 1
    same_chunk = (row >> shift) == (col >> shift)
    lw = lw_ref[0]
    g = _masked_rowsum((same_chunk & (row >= col)).astype(BF16), lw)
    g_end = _masked_rowsum(same_chunk.astype(BF16), lw)
    e_neg = jnp.exp(-g)
    e_end = jnp.exp(g_end - g)
    ch = functools.partial(_chunk_heads, nsub=nsub)
    k = kp_ref[0]
    b = bb_ref[0]
    kap_t = ch(kap_ref[0] * jnp.exp(g - lw))
    b_t = ch(b * e_neg)
    k_t = ch(k * e_neg)
    r_t = ch(r_ref[0] * jnp.exp(g))
    b_e = ch(b * e_end)
    k_e = ch(k * e_end)
    vv = ch(v_ref[0])
    decay_end = ch(jnp.exp(g_end))

    row = lax.broadcasted_iota(jnp.int32, (c, 2 * c), 0)[None]
    lane = lax.broadcasted_iota(jnp.int32, (c, 2 * c), 1)[None]
    colp = lane & (c - 1)
    right = lane >= c
    left_half = lambda x: x[:, :, :c]
    right_half = lambda x: x[:, :, c:]
    eye = (row == lane)[:, :, :c]

    bk_t = jnp.concatenate([b_t, k_t], axis=1)
    p_kap = _wmm("lb", kap_t, bk_t, _B_NT)
    p_r = _wmm("ark", r_t, bk_t, _B_NT)
    l_k = right_half(jnp.where(row > colp, p_kap, 0.0))
    a_r = jnp.where(row >= colp, p_r, 0.0)
    a_rb = left_half(a_r)
    a_rk = right_half(a_r)
    x = jnp.where(right, jnp.where(row == colp, 1.0, 0.0), jnp.where(row > colp, -p_kap, 0.0))
    span = 1
    while span < c:
        x = _wmm("inv", left_half(x), x, _B_NN) + jnp.where(right, x, 0.0)
        span *= 2
    t_inv = right_half(x)
    lkv = _wmm("lkv", l_k, vv, _B_NN)
    au = -_wmm("ta", t_inv, jnp.concatenate([kap_t, lkv], axis=2), _B_NN)
    a1 = left_half(au)
    u0 = right_half(au)
    arb_au = _wmm("arba", a_rb, au, _B_NN)
    a2 = r_t + left_half(arb_au)
    o0 = right_half(arb_au) + _wmm("arkv", a_rk, vv, _B_NN)
    g_mat = jnp.where(eye, decay_end, 0.0) + _wmm("g", a1, b_e, _B_TN)
    h_mat = _wmm("h", jnp.concatenate([u0, vv], axis=1),
                 jnp.concatenate([b_e, k_e], axis=1), _B_TN)

    s = s_scr[...]
    for i in range(nsub):
        hs = slice(N_HEADS * i, N_HEADS * (i + 1))
        o = _wmm("o", a2[hs], s, _B_NT) + o0[hs]
        s = _wmm("s", s, g_mat[hs], _B_NN) + h_mat[hs]
        for h in range(N_HEADS):
            o_ref[0, c * i:c * (i + 1), HEAD * h:HEAD * (h + 1)] = o[h]
    s_scr[...] = s

    @pl.when(pl.program_id(1) == pl.num_programs(1) - 1)
    def _():
        sout_ref[0] = s


def _wkv_seq(r, lw, kp, v, kap, bb, s0):
    b, t, _ = r.shape
    tc = 256
    nsub = tc // WKV_CHUNK
    tok = pl.BlockSpec((1, tc, D_A), lambda i, j: (i, j, 0))
    st = pl.BlockSpec((1, N_HEADS, HEAD, HEAD), lambda i, j: (i, 0, 0, 0))
    return pl.pallas_call(
        functools.partial(_wkv_seq_body, nsub),
        grid=(b, t // tc),
        in_specs=[tok] * 6 + [st],
        out_specs=[tok, st],
        out_shape=[jax.ShapeDtypeStruct((b, t, D_A), F32),
                   jax.ShapeDtypeStruct((b, N_HEADS, HEAD, HEAD), F32)],
        scratch_shapes=[pltpu.VMEM((N_HEADS, HEAD, HEAD), F32)],
        compiler_params=_params(("arbitrary", "arbitrary"), 32 * 2**20),
        name="wkv_seq",
    )(r, lw, kp, v, kap, bb, s0)


def _wkv_step_body(r_ref, lw_ref, kp_ref, v_ref, kap_ref, bb_ref, s0_ref, _, o_ref, sout_ref):
    eye = (lax.broadcasted_iota(jnp.int32, (HEAD, HEAD), 0)
           == lax.broadcasted_iota(jnp.int32, (HEAD, HEAD), 1))[None]
    for h in range(N_HEADS):
        hs = slice(HEAD * h, HEAD * (h + 1))
        row = lambda ref: ref[:, hs][:, None, :]
        s = s0_ref[:, h]
        sa = -jnp.sum(s * row(kap_ref), axis=-1, keepdims=True)
        v_col = jnp.sum(jnp.where(eye, row(v_ref), 0.0), axis=-1, keepdims=True)
        s_new = s * jnp.exp(row(lw_ref)) + sa * row(bb_ref) + v_col * row(kp_ref)
        sout_ref[:, h] = s_new
        o_col = jnp.sum(s_new * row(r_ref), axis=-1, keepdims=True)
        o_ref[:, hs] = jnp.sum(jnp.where(eye, o_col, 0.0), axis=1)


def _wkv_step(r, lw, kp, v, kap, bb, s0, s_new_all, layer):
    m = r.shape[0]
    nb = 8
    tok = pl.BlockSpec((nb, D_A), lambda i: (i, 0))
    st = pl.BlockSpec((nb, N_HEADS, HEAD, HEAD), lambda i: (i, 0, 0, 0))
    st_out = pl.BlockSpec((None, nb, N_HEADS, HEAD, HEAD), lambda i: (layer, i, 0, 0, 0))
    return pl.pallas_call(
        _wkv_step_body,
        grid=(m // nb,),
        in_specs=[tok] * 6 + [st, pl.BlockSpec(memory_space=pl.ANY)],
        out_specs=[tok, st_out],
        out_shape=[jax.ShapeDtypeStruct((m, D_A), F32),
                   jax.ShapeDtypeStruct(s_new_all.shape, F32)],
        input_output_aliases={7: 1},
        compiler_params=_params(("arbitrary",), 24 * 2**20),
        name="wkv_step",
    )(r, lw, kp, v, kap, bb, s0, s_new_all)


def _merge_body(seq_mode, tm, x_ref, o_ref, r_ref, kp_ref, v_ref, u_ref, vg_ref,
                ga_ref, gb_ref, g1_ref, sc2_ref, sh2_ref, lxg_ref, lxb_ref, rk_ref,
                bd_ref, ws_ref, bs_ref, pa_ref, pb_ref, wo_ref, l1g_ref, l1b_ref,
                x1_ref, h2_ref):
    bd = bd_ref[...]
    o = o_ref[0]
    inv_n = 1.0 / HEAD
    mu = _segsum(o, bd) * inv_n
    d = o - mu
    var = _segsum(d * d, bd) * inv_n
    on = d * lax.rsqrt(var + GN_EPS) * lxg_ref[...] + lxb_ref[...]
    v = v_ref[0]
    o_a = on + _segsum(r_ref[0] * kp_ref[0] * rk_ref[...], bd) * v
    vg = vg_ref[0]
    if seq_mode:
        row = lax.broadcasted_iota(jnp.int32, (CHUNK, CHUNK), 0)
        col = lax.broadcasted_iota(jnp.int32, (CHUNK, CHUNK), 1)
        lane = lax.broadcasted_iota(jnp.int32, (CHUNK, D_B), 1)
        gsz = D_B // N_GROUPS_B
        w_cat = jnp.concatenate(
            [jnp.where(row >= col, ws_ref[g], 0.0).astype(BF16) for g in range(N_GROUPS_B)], axis=1)
        pieces = []
        for j in range(tm // CHUNK):
            vc = vg[j * CHUNK:(j + 1) * CHUNK, :].astype(BF16)
            v_bd = jnp.concatenate(
                [jnp.where((lane >= g * gsz) & (lane < (g + 1) * gsz), vc, 0.0)
                 for g in range(N_GROUPS_B)], axis=0)
            pieces.append(_dot(w_cat, v_bd) + bs_ref[...])
        s = jnp.concatenate(pieces, axis=0) if len(pieces) > 1 else pieces[0]
    else:
        s = vg * ws_ref[...] + bs_ref[...]
    o_b = u_ref[0] * s
    y = _mm1(ga_ref[0] * _mm1(o_a, pa_ref[...]) + gb_ref[0] * _mm1(o_b, pb_ref[...]),
             wo_ref[...])
    x1 = _layer_norm(ALPHA * x_ref[0] + g1_ref[0] * y, l1g_ref[...], l1b_ref[...])
    x1_ref[0] = x1
    h2_ref[0] = x1 * (1.0 + sc2_ref[0]) + sh2_ref[0]


def _merge_stage(seq_mode, x, o, r, kp, v, u, vg, ga, gb, g1, sc2, sh2, p):
    b, t, d = x.shape
    tm = 256 if seq_mode else t
    tmod = 1 if seq_mode else tm
    tok = lambda n: pl.BlockSpec((1, tm, n), lambda i, j: (i, j, 0))
    mod = pl.BlockSpec((1, tmod, d), lambda i, j: (i, j if not seq_mode else 0, 0))
    ws, bs = (p["w_spatial"], p["b_spatial_full"]) if seq_mode else (p["ws_row"], p["bs_row"])
    est = 2 * tm * (2 * d + 7 * D_A + 2 * d + 2 * d) * 4 + 8 * tm * d * 4 + 16 * 2**20
    return pl.pallas_call(
        functools.partial(_merge_body, seq_mode, tm),
        grid=(b, t // tm),
        in_specs=[tok(d)] + [tok(D_A)] * 6 + [tok(d), tok(d), mod, mod, mod,
                  _const_spec((1, D_A)), _const_spec((1, D_A)), _const_spec((1, D_A)),
                  _const_spec((D_A, D_A)), _const_spec(ws.shape), _const_spec(bs.shape),
                  _const_spec((D_A, d)), _const_spec((D_B, d)), _const_spec((d, d)),
                  _const_spec((1, d)), _const_spec((1, d))],
        out_specs=[tok(d), tok(d)],
        out_shape=[jax.ShapeDtypeStruct((b, t, d), F32)] * 2,
        compiler_params=_params(("arbitrary", "arbitrary"), est),
        name="merge_seq" if seq_mode else "merge_row",
    )(x, o, r, kp, v, u, vg, ga, gb, g1, sc2, sh2, p["lnx_g"], p["lnx_b"], p["r_k"],
      p["bd"], ws, bs, p["w_branch_a"], p["w_branch_b"], p["w_out"], p["ln1_g"], p["ln1_b"])


def _route_body(h_ref, w_ref, b_ref, gate_ref, idx_ref, wts_ref, cnt_ref):
    lg = _mm3(h_ref[...], w_ref[...]) + b_ref[...]
    lane = lax.broadcasted_iota(jnp.int32, lg.shape, 1)
    lanef = lane.astype(F32)
    neg = -jnp.inf
    is_g = (lane >= N_EXPERTS) & (lane < N_EXPERTS + N_ROUTE_GROUPS)
    mg = jnp.max(jnp.where(is_g, lg, neg), axis=-1, keepdims=True)
    gidx = jnp.min(jnp.where(is_g & (lg == mg), lanef - N_EXPERTS, 1e9), axis=-1, keepdims=True)
    pg_sel = 1.0 / jnp.sum(jnp.where(is_g, jnp.exp(lg - mg), 0.0), axis=-1, keepdims=True)
    lo = gidx * EXP_PER_GROUP
    in_grp = (lanef >= lo) & (lanef < lo + EXP_PER_GROUP)
    t1 = jnp.max(jnp.where(in_grp, lg, neg), axis=-1, keepdims=True)
    i1 = jnp.min(jnp.where(in_grp & (lg == t1), lanef, 1e9), axis=-1, keepdims=True)
    rest = in_grp & (lanef != i1)
    t2 = jnp.max(jnp.where(rest, lg, neg), axis=-1, keepdims=True)
    i2 = jnp.min(jnp.where(rest & (lg == t2), lanef, 1e9), axis=-1, keepdims=True)
    e2 = jnp.exp(t2 - t1)
    w1 = pg_sel / (1.0 + e2)
    w2 = pg_sel * e2 / (1.0 + e2)
    gate_ref[...] = jnp.where(lanef == i1, w1, 0.0) + jnp.where(lanef == i2, w2, 0.0)
    tm = lg.shape[0]
    hit = (lanef == i1) | (lanef == i2)
    earlier = (lax.broadcasted_iota(jnp.int32, (tm, tm), 0)
               > lax.broadcasted_iota(jnp.int32, (tm, tm), 1))
    rank = _dot(earlier.astype(BF16), hit.astype(BF16))
    cnt = jnp.sum(hit.astype(F32), axis=0, keepdims=True)
    cnt8 = jnp.floor((cnt + 7.0) * 0.125) * 8.0
    lower_expert = (lax.broadcasted_iota(jnp.int32, (LANES, LANES), 0)
                    < lax.broadcasted_iota(jnp.int32, (LANES, LANES), 1))
    run_start = _masked_rowsum_t(jnp.broadcast_to(cnt8, (8, LANES)), lower_expert)[0:1]
    lpos = run_start + rank
    lp1 = jnp.sum(jnp.where(lanef == i1, lpos, 0.0), axis=-1, keepdims=True)
    lp2 = jnp.sum(jnp.where(lanef == i2, lpos, 0.0), axis=-1, keepdims=True)
    cnt_ref[0] = cnt.astype(jnp.int32)
    idx = jnp.where(lane == 0, i1, jnp.where(lane == 1, i2, jnp.where(lane == 2, lp1, lp2)))
    idx_ref[...] = idx.astype(jnp.int32)
    wts_ref[...] = jnp.where(lane == 0, w1, w2)


def _route(h2, p):
    m, d = h2.shape
    tm = min(m, MOE_TILE)
    tok = pl.BlockSpec((tm, LANES), lambda i: (i, 0))
    return pl.pallas_call(
        _route_body,
        grid=(m // tm,),
        in_specs=[pl.BlockSpec((tm, d), lambda i: (i, 0)),
                  _const_spec((d, LANES)), _const_spec((1, LANES))],
        out_specs=[tok, tok, tok, pl.BlockSpec((1, 1, LANES), lambda i: (i, 0, 0))],
        out_shape=[jax.ShapeDtypeStruct((m, LANES), F32),
                   jax.ShapeDtypeStruct((m, LANES), jnp.int32),
                   jax.ShapeDtypeStruct((m, LANES), F32),
                   jax.ShapeDtypeStruct((m // tm, 1, LANES), jnp.int32)],
        compiler_params=_params(("arbitrary",), 24 * 2**20),
        name="route",
    )(h2, p["w_route"], p["b_route"])


def _moe_body(h_ref, gate_ref, wg_ref, wu_ref, wd_ref, x1_ref, g2_ref, l2g_ref, l2b_ref,
              out_ref, acc_ref, xb_ref):
    e = pl.program_id(1)

    @pl.when(e == 0)
    def _():
        acc_ref[...] = jnp.zeros_like(acc_ref)
        xb_ref[...] = h_ref[...].astype(BF16)

    xb = xb_ref[...]
    pre = _dot(xb, wg_ref[0].astype(BF16))
    hid = pre * jax.nn.sigmoid(pre) * _dot(xb, wu_ref[0].astype(BF16))
    ye = _mm1(hid, wd_ref[0])
    gate = gate_ref[...]
    lane = lax.broadcasted_iota(jnp.int32, gate.shape, 1)
    ge = jnp.sum(jnp.where(lane == e, gate, 0.0), axis=-1, keepdims=True)
    acc_ref[...] += ge * ye

    @pl.when(e == pl.num_programs(1) - 1)
    def _():
        out_ref[...] = _layer_norm(ALPHA * x1_ref[...] + g2_ref[0] * acc_ref[...],
                                   l2g_ref[...], l2b_ref[...])


def _moe(seq_len, h2, gate, x1, g2, p):
    m, d = h2.shape
    layer = p["layer"]
    tm = min(seq_len if g2.shape[1] == 1 else m, 1024)
    tok = lambda n: pl.BlockSpec((tm, n), lambda i, e: (i, 0))
    if g2.shape[1] == 1:
        tiles_per_seq = seq_len // tm
        g2_spec = pl.BlockSpec((1, 1, d), lambda i, e: (i // tiles_per_seq, 0, 0))
    else:
        g2_spec = pl.BlockSpec((1, tm, d), lambda i, e: (0, i, 0))
    est = 2 * tm * (3 * d + LANES) * 4 + tm * d * 6 + 4 * tm * d * 4 + 8 * 2**20
    return pl.pallas_call(
        _moe_body,
        grid=(m // tm, N_EXPERTS),
        in_specs=[tok(d), tok(LANES),
                  pl.BlockSpec((None, 1, d, D_EXPERT), lambda i, e: (layer, e, 0, 0)),
                  pl.BlockSpec((None, 1, d, D_EXPERT), lambda i, e: (layer, e, 0, 0)),
                  pl.BlockSpec((None, 1, D_EXPERT, d), lambda i, e: (layer, e, 0, 0)),
                  tok(d), g2_spec, _const_spec((1, d)), _const_spec((1, d))],
        out_specs=tok(d),
        out_shape=jax.ShapeDtypeStruct((m, d), F32),
        scratch_shapes=[pltpu.VMEM((tm, d), F32), pltpu.VMEM((tm, d), BF16)],
        compiler_params=_params(("arbitrary", "arbitrary"), est),
        name="moe_dense",
    )(h2, gate, p["w_exp_gate"], p["w_exp_up"], p["w_exp_down"], x1, g2, p["ln2_g"], p["ln2_b"])


MOE_TILE = 512
MOE_BLOCK = 512
MOE_RUN_ALIGN = 8
MOE_LOCAL_ROWS = 2 * MOE_TILE + 256
MOE_SLABS = tuple(2 ** k for k in range(9, 2, -1))


def _for_each_slab(run_ref, make_copy, fn):
    def one_run(e, c):
        dst = run_ref[0, 0, e]
        src = run_ref[0, 0, N_EXPERTS + e]
        n = run_ref[0, 0, 2 * N_EXPERTS + e]
        for slab in MOE_SLABS:
            off = n & (-2 * slab)

            @pl.when((n & slab) != 0)
            def _():
                fn(make_copy(pl.multiple_of(dst + off, MOE_RUN_ALIGN),
                             pl.multiple_of(src + off, MOE_RUN_ALIGN), slab))
        return c

    lax.fori_loop(0, N_EXPERTS, one_run, 0)


def _dispatch_body(tail_ref, run_ref, prev_run_ref, h_ref, idx_ref, xs_hbm, zero_buf, loc_buf,
                   sem, zsem):
    nb_max = xs_hbm.shape[0] // MOE_BLOCK
    nt = MOE_TILE

    def zero_copy(row0):
        return pltpu.make_async_copy(
            zero_buf, xs_hbm.at[pl.ds(pl.multiple_of(row0, MOE_BLOCK), MOE_BLOCK)], zsem)

    @pl.when(pl.program_id(0) == 0)
    def _():
        zero_buf[...] = jnp.zeros_like(zero_buf)
        n_used = tail_ref[N_EXPERTS]

        def each_zero_copy(fn):
            for e in range(N_EXPERTS):
                @pl.when(tail_ref[e] >= 0)
                def _():
                    fn(zero_copy(tail_ref[e]))

            def unused(j, c):
                fn(zero_copy(j * MOE_BLOCK))
                return c

            lax.fori_loop(n_used, nb_max, unused, 0)

        each_zero_copy(lambda cp: cp.start())
        each_zero_copy(lambda cp: cp.wait())

    lp = idx_ref[...].astype(F32)
    eye = (lax.broadcasted_iota(jnp.int32, (nt, nt), 0)
           == lax.broadcasted_iota(jnp.int32, (nt, nt), 1))
    as_row = lambda col: jnp.sum(jnp.where(eye, col, 0.0), axis=0, keepdims=True)
    r = lax.broadcasted_iota(jnp.int32, (MOE_LOCAL_ROWS, nt), 0).astype(F32)
    pick = (r == as_row(lp[:, 2:3])) | (r == as_row(lp[:, 3:4]))
    step = pl.program_id(0)
    slot = step & 1
    loc_buf[slot] = _dot(pick.astype(BF16), h_ref[...].astype(BF16))

    def copies_from(which):
        def make_copy(dst, src, rows):
            return pltpu.make_async_copy(loc_buf.at[which, pl.ds(src, rows)],
                                         xs_hbm.at[pl.ds(dst, rows)], sem.at[which])
        return make_copy

    _for_each_slab(run_ref, copies_from(slot), lambda cp: cp.start())

    @pl.when(step > 0)
    def _():
        _for_each_slab(prev_run_ref, copies_from(1 - slot), lambda cp: cp.wait())

    @pl.when(step == pl.num_programs(0) - 1)
    def _():
        _for_each_slab(run_ref, copies_from(slot), lambda cp: cp.wait())


def _dispatch(h2, idx, runs, tail_start, n_rows):
    m, d = h2.shape
    nt = MOE_TILE
    run_spec = lambda at: pl.BlockSpec((1, 1, LANES), lambda i, tail: (at(i), 0, 0),
                                       memory_space=pltpu.SMEM)
    grid_spec = pltpu.PrefetchScalarGridSpec(
        num_scalar_prefetch=1,
        grid=(m // nt,),
        in_specs=[run_spec(lambda i: i), run_spec(lambda i: jnp.maximum(i - 1, 0)),
                  pl.BlockSpec((nt, d), lambda i, tail: (i, 0)),
                  pl.BlockSpec((nt, LANES), lambda i, tail: (i, 0))],
        out_specs=pl.BlockSpec(memory_space=pl.ANY),
        scratch_shapes=[pltpu.VMEM((MOE_BLOCK, d), F32),
                        pltpu.VMEM((2, MOE_LOCAL_ROWS, d), F32),
                        pltpu.SemaphoreType.DMA((2,)), pltpu.SemaphoreType.DMA],
    )
    return pl.pallas_call(
        _dispatch_body,
        grid_spec=grid_spec,
        out_shape=jax.ShapeDtypeStruct((n_rows, d), F32),
        compiler_params=_params(("arbitrary",), 40 * 2**20),
        name="moe_dispatch",
    )(tail_start, runs, runs, h2, idx)


def _gmm_body(be_ref, nb_ref, x_ref, wg_ref, wu_ref, wd_ref, y_ref):
    j = pl.program_id(0)

    @pl.when(j < nb_ref[0])
    def _():
        xb = x_ref[...].astype(BF16)
        pre = _dot(xb, wg_ref[0].astype(BF16))
        hid = pre * jax.nn.sigmoid(pre) * _dot(xb, wu_ref[0].astype(BF16))
        y_ref[...] = _mm1(hid, wd_ref[0])

    @pl.when(j >= nb_ref[0])
    def _():
        y_ref[...] = jnp.zeros_like(y_ref)


def _gmm(xs, blk_expert, n_blocks, p):
    n_rows, d = xs.shape
    nb_max = n_rows // MOE_BLOCK
    layer = p["layer"]
    live = lambda j, be, nb: jnp.minimum(j, nb[0] - 1)
    wspec = lambda shape: pl.BlockSpec((None,) + shape,
                                       lambda j, be, nb: (layer, be[live(j, be, nb)], 0, 0))
    grid_spec = pltpu.PrefetchScalarGridSpec(
        num_scalar_prefetch=2,
        grid=(nb_max,),
        in_specs=[pl.BlockSpec((MOE_BLOCK, d), lambda j, be, nb: (live(j, be, nb), 0)),
                  wspec((1, d, D_EXPERT)), wspec((1, d, D_EXPERT)), wspec((1, D_EXPERT, d))],
        out_specs=pl.BlockSpec((MOE_BLOCK, d), lambda j, be, nb: (j, 0)),
    )
    return pl.pallas_call(
        _gmm_body,
        grid_spec=grid_spec,
        out_shape=jax.ShapeDtypeStruct((n_rows, d), F32),
        compiler_params=_params(("arbitrary",), 24 * 2**20),
        name="moe_gmm",
    )(blk_expert, n_blocks, xs, p["w_exp_gate"], p["w_exp_up"], p["w_exp_down"])


def _combine_body(run_ref, next_run_ref, ys_hbm, idx_ref, wts_ref, x1_ref, g2_ref, l2g_ref,
                  l2b_ref, out_ref, loc_buf, sem):
    nt = MOE_TILE
    step = pl.program_id(0)
    slot = step & 1

    def copies_into(which):
        def make_copy(dst, src, rows):
            return pltpu.make_async_copy(ys_hbm.at[pl.ds(dst, rows)],
                                         loc_buf.at[which, pl.ds(src, rows)], sem.at[which])
        return make_copy

    @pl.when(step == 0)
    def _():
        loc_buf[...] = jnp.zeros_like(loc_buf)
        _for_each_slab(run_ref, copies_into(slot), lambda cp: cp.start())

    @pl.when(step < pl.num_programs(0) - 1)
    def _():
        _for_each_slab(next_run_ref, copies_into(1 - slot), lambda cp: cp.start())

    _for_each_slab(run_ref, copies_into(slot), lambda cp: cp.wait())
    lp = idx_ref[...].astype(F32)
    w = wts_ref[...]
    c = lax.broadcasted_iota(jnp.int32, (nt, MOE_LOCAL_ROWS), 1).astype(F32)
    sel = (jnp.where(c == lp[:, 2:3], w[:, 0:1], 0.0)
           + jnp.where(c == lp[:, 3:4], w[:, 1:2], 0.0))
    moe = _mm3(sel, loc_buf[slot])
    out_ref[...] = _layer_norm(ALPHA * x1_ref[...] + g2_ref[0] * moe, l2g_ref[...], l2b_ref[...])


def _combine(seq_len, ys, idx, wts, runs, x1, g2, p):
    m, d = x1.shape
    nt = MOE_TILE
    tiles_per_seq = seq_len // nt
    tok = lambda n: pl.BlockSpec((nt, n), lambda i: (i, 0))
    n_tiles = m // nt
    run_spec = lambda at: pl.BlockSpec((1, 1, LANES), lambda i: (at(i), 0, 0),
                                       memory_space=pltpu.SMEM)
    return pl.pallas_call(
        _combine_body,
        grid=(n_tiles,),
        in_specs=[run_spec(lambda i: i), run_spec(lambda i: jnp.minimum(i + 1, n_tiles - 1)),
                  pl.BlockSpec(memory_space=pl.ANY), tok(LANES), tok(LANES), tok(d),
                  pl.BlockSpec((1, 1, d), lambda i: (i // tiles_per_seq, 0, 0)),
                  _const_spec((1, d)), _const_spec((1, d))],
        out_specs=tok(d),
        out_shape=jax.ShapeDtypeStruct((m, d), F32),
        scratch_shapes=[pltpu.VMEM((2, MOE_LOCAL_ROWS, d), F32), pltpu.SemaphoreType.DMA((2,))],
        compiler_params=_params(("arbitrary",), 48 * 2**20),
        name="moe_combine",
    )(runs, runs, ys, idx, wts, x1, g2, p["ln2_g"], p["ln2_b"])


def _moe_routed(seq_len, h2, idx, wts, tile_cnt, x1, g2, p):
    m, d = h2.shape
    blk = MOE_BLOCK
    n_tiles = m // MOE_TILE
    worst_rows = 2 * m + n_tiles * N_EXPERTS * (MOE_RUN_ALIGN - 1) + N_EXPERTS * (blk - 1)
    nb_max = -(-worst_rows // blk)
    cnt = tile_cnt[:, 0, :N_EXPERTS]
    run_len = ((cnt + MOE_RUN_ALIGN - 1) // MOE_RUN_ALIGN) * MOE_RUN_ALIGN
    local_row = jnp.cumsum(run_len, axis=1) - run_len
    rows_before = jnp.cumsum(run_len, axis=0) - run_len
    total = jnp.sum(run_len, axis=0)
    padded = ((total + blk - 1) // blk) * blk
    ends = jnp.cumsum(padded)
    starts = ends - padded
    n_blocks = (ends[-1] // blk).astype(jnp.int32).reshape(1)
    first_row = jnp.arange(nb_max, dtype=jnp.int32) * blk
    blk_expert = jnp.minimum(
        jnp.sum((first_row[:, None] >= ends[None, :]).astype(jnp.int32), axis=1),
        N_EXPERTS - 1).astype(jnp.int32)
    tail_start = jnp.concatenate(
        [jnp.where(padded > 0, ends - blk, -1).astype(jnp.int32), n_blocks])
    runs = jnp.concatenate(
        [starts[None, :] + rows_before, local_row, run_len,
         jnp.zeros((n_tiles, LANES - 3 * N_EXPERTS), jnp.int32)], axis=1).astype(jnp.int32)
    runs = runs.reshape(n_tiles, 1, LANES)
    xs = _dispatch(h2, idx, runs, tail_start, nb_max * blk)
    ys = _gmm(xs, blk_expert, n_blocks, p)
    return _combine(seq_len, ys, idx, wts, runs, x1, g2, p)


def _prep_layer(l, w_in, mu_shift, w0, w_decay_up, a0, w_iclr_up, k_k, k_a, r_k,
                lnx_g, lnx_b, lnv_g, lnv_b, w_spatial, b_spatial, w_branch_a, w_branch_b,
                w_out, ln1_g, ln1_b, w_route_group, b_route_group, w_route_expert,
                b_route_expert, w_exp_gate, w_exp_up, w_exp_down, ln2_g, ln2_b):
    d = D_MODEL
    pad_a = N_SHIFT_PAD - N_SHIFT
    wi = w_in[l]
    w_in_p = jnp.concatenate(
        [wi[:, :N_SHIFT], jnp.zeros((d, pad_a), F32), wi[:, N_SHIFT:]], axis=1).astype(BF16)
    mu = jnp.concatenate([mu_shift[l], jnp.zeros((pad_a,), F32)])[None]
    lora = jnp.zeros((LANES, 2 * D_A), F32)
    lora = lora.at[:R_LORA, :D_A].set(w_decay_up[l]).at[R_LORA:2 * R_LORA, D_A:].set(w_iclr_up[l])
    seg = jnp.arange(D_A) // HEAD
    row1 = lambda x: x.reshape(1, -1)
    gsz = D_B // N_GROUPS_B
    w_route = jnp.concatenate(
        [w_route_expert[l], w_route_group[l],
         jnp.zeros((d, LANES - N_EXPERTS - N_ROUTE_GROUPS), F32)], axis=1)
    b_route = jnp.concatenate(
        [b_route_expert[l], b_route_group[l],
         jnp.zeros((LANES - N_EXPERTS - N_ROUTE_GROUPS,), F32)])[None]
    return dict(
        w_in=w_in_p, mu=mu, lora=lora,
        w0a0=jnp.concatenate([w0[l], a0[l]])[None],
        k_k=row1(k_k[l]), k_a=row1(k_a[l]), r_k=row1(r_k[l]),
        lnx_g=row1(lnx_g[l]), lnx_b=row1(lnx_b[l]),
        lnv_g=row1(lnv_g[l]), lnv_b=row1(lnv_b[l]),
        bd=(seg[:, None] == seg[None, :]).astype(BF16),
        w_spatial=w_spatial[l],
        b_spatial_full=jnp.repeat(b_spatial[l].T, gsz, axis=1),
        ws_row=jnp.repeat(w_spatial[l][:, 0, 0], gsz)[None],
        bs_row=jnp.repeat(b_spatial[l][:, 0], gsz)[None],
        w_branch_a=w_branch_a[l].astype(BF16), w_branch_b=w_branch_b[l].astype(BF16),
        w_out=w_out[l].astype(BF16), ln1_g=row1(ln1_g[l]), ln1_b=row1(ln1_b[l]),
        w_route=w_route, b_route=b_route,
        w_exp_gate=w_exp_gate, w_exp_up=w_exp_up, w_exp_down=w_exp_down, layer=l,
        ln2_g=row1(ln2_g[l]), ln2_b=row1(ln2_b[l]),
    )


def _trunk(seq_mode, x, mods, wkv_in, shift_in, preps):
    b, t, d = x.shape
    wkv_out, shift_out, v_out = [], [], []
    wkv_acc = None if seq_mode else jnp.zeros(wkv_in.shape, F32)
    for l in range(DEPTH):
        p = preps[l]
        sh1, sc1, g1, sh2, sc2, g2 = mods[l]
        if seq_mode:
            zprev = jnp.zeros((b, 1, N_SHIFT_PAD), F32) if shift_in is None else shift_in[l]
        else:
            zprev = _matmul(shift_in[l], p["w_in"][:, :N_SHIFT_PAD]).reshape(b, t, N_SHIFT_PAD)
        r, lw, kp, v, kap, bb, u, vg, ga, gb, hl = _in_stage(seq_mode, x, sc1, sh1, zprev, p)
        if seq_mode:
            o, s_new = _wkv_seq(r, lw, kp, v, kap, bb, wkv_in[l])
        else:
            flat = lambda a: a.reshape(t, D_A)
            o, wkv_acc = _wkv_step(flat(r), flat(lw), flat(kp), flat(v), flat(kap), flat(bb),
                                   wkv_in[l], wkv_acc, l)
            s_new = None
            o = o.reshape(b, t, D_A)
        x1, h2 = _merge_stage(seq_mode, x, o, r, kp, v, u, vg, ga, gb, g1, sc2, sh2, p)
        m = b * t
        h2f = h2.reshape(m, d)
        gate, idx, wts, tile_cnt = _route(h2f, p)
        if seq_mode:
            x = _moe_routed(t, h2f, idx, wts, tile_cnt, x1.reshape(m, d), g2, p)
        else:
            x = _moe(t, h2f, gate, x1.reshape(m, d), g2, p)
        x = x.reshape(b, t, d)
        wkv_out.append(s_new)
        shift_out.append(hl)
        v_out.append(vg)
    if not seq_mode:
        wkv_out = wkv_acc
    return x, wkv_out, shift_out, v_out


def kernel(x_prompt, x_sample, c_prompt, c_sample, state_wkv, state_shift, w_ada, b_ada, w_in, mu_shift, w0, w_decay_up, a0, w_iclr_up, k_k, k_a, r_k, lnx_g, lnx_b, lnv_g, lnv_b, w_spatial, b_spatial, w_branch_a, w_branch_b, w_out, ln1_g, ln1_b, w_route_group, b_route_group, w_route_expert, b_route_expert, w_exp_gate, w_exp_up, w_exp_down, ln2_g, ln2_b):
    bp, tp, d = x_prompt.shape
    bs = x_sample.shape[0]
    layer_params = (w_in, mu_shift, w0, w_decay_up, a0, w_iclr_up, k_k, k_a, r_k, lnx_g,
                    lnx_b, lnv_g, lnv_b, w_spatial, b_spatial, w_branch_a, w_branch_b, w_out,
                    ln1_g, ln1_b, w_route_group, b_route_group, w_route_expert,
                    b_route_expert, w_exp_gate, w_exp_up, w_exp_down, ln2_g, ln2_b)
    preps = [_prep_layer(l, *layer_params) for l in range(DEPTH)]
    mod_all = _ada(jnp.concatenate([c_prompt, c_sample], axis=0), w_ada, b_ada)
    mods_p, mods_s = [], []
    for l in range(DEPTH):
        parts = jnp.split(mod_all[l], 6, axis=-1)
        mods_p.append([q[:bp].reshape(bp, 1, d) for q in parts])
        mods_s.append([q[bp:].reshape(1, bs, d) for q in parts])

    wkv0 = jnp.zeros((DEPTH, bp, N_HEADS, HEAD, HEAD), F32)
    y_p, wkv_p, shift_p, _ = _trunk(True, x_prompt, mods_p, wkv0, None, preps)
    y_s, wkv_s, shift_s, v_s = _trunk(False, x_sample.reshape(1, bs, d), mods_s, state_wkv,
                                      state_shift, preps)
    return (y_p,
            y_s.reshape(bs, 1, d),
            jnp.stack(wkv_p),
            jnp.stack([s.reshape(bp, d) for s in shift_p]),
            wkv_s,
            jnp.stack([s.reshape(bs, d) for s in shift_s]),
            jnp.stack([q.reshape(bs, 1, D_B) for q in v_s]))
```

```python
import functools

import jax
import jax.numpy as jnp
from jax import lax
from jax.experimental import pallas as pl
from jax.experimental.pallas import tpu as pltpu

F32 = jnp.float32
BF16 = jnp.bfloat16

D_MODEL = 1024
DEPTH = 2
HEAD = 64
N_HEADS = 8
D_A = N_HEADS * HEAD
R_LORA = 32
CHUNK = 128
N_GROUPS_B = 8
D_B = 512
N_SHIFT = 3 * D_A + 2 * R_LORA
N_ROUTE_GROUPS = 4
EXP_PER_GROUP = 8
N_EXPERTS = N_ROUTE_GROUPS * EXP_PER_GROUP
D_EXPERT = 256
ALPHA = (2 * DEPTH) ** 0.25
LN_EPS = 1e-5
GN_EPS = 64e-5

LANES = 128
N_SHIFT_PAD = 13 * LANES
COL_U = N_SHIFT_PAD
COL_VG = COL_U + D_B
COL_GA = COL_VG + D_B
COL_GB = COL_GA + D_MODEL
N_IN_PAD = COL_GB + D_MODEL
WKV_CHUNK = 64
VMEM_CAP_BYTES = 60000 * 1024

_NN = (((1,), (0,)), ((), ()))
_B_NT = (((2,), (2,)), ((0,), (0,)))
_B_NN = (((2,), (1,)), ((0,), (0,)))
_B_TN = (((1,), (1,)), ((0,), (0,)))


def _dot(a, b, dims=_NN):
    return lax.dot_general(a, b, dims, preferred_element_type=F32)


def _split2(x):
    hi = x.astype(BF16)
    lo = (x - hi.astype(F32)).astype(BF16)
    return hi, lo


def _mm1(a, b, dims=_NN):
    return _dot(a.astype(BF16), b.astype(BF16), dims)


def _mm3(a, b, dims=_NN):
    ah, al = _split2(a)
    bh, bl = _split2(b)
    return _dot(ah, bh, dims) + (_dot(ah, bl, dims) + _dot(al, bh, dims))


def _segsum(x, bd):
    hi, lo = _split2(x)
    return _dot(hi, bd) + _dot(lo, bd)


def _layer_norm(x, g, b):
    mu = jnp.mean(x, axis=-1, keepdims=True)
    d = x - mu
    var = jnp.mean(d * d, axis=-1, keepdims=True)
    return d * lax.rsqrt(var + LN_EPS) * g + b


def _gelu(x):
    return 0.5 * x * (1.0 + lax.erf(x * 0.7071067811865476))


def _params(sem, est_bytes):
    limit = int(min(VMEM_CAP_BYTES, max(est_bytes, 16 * 1024 * 1024)))
    return pltpu.CompilerParams(dimension_semantics=sem, vmem_limit_bytes=limit)


def _const_spec(shape, single_buffer=False):
    nd = len(shape)
    if single_buffer:
        return pl.BlockSpec(shape, lambda *_: (0,) * nd, pipeline_mode=pl.Buffered(1))
    return pl.BlockSpec(shape, lambda *_: (0,) * nd)


def _ada_body(c_ref, w_ref, b_ref, o_ref):
    c = c_ref[...]
    s = c * jax.nn.sigmoid(c)
    o_ref[0] = _mm3(s, w_ref[0]) + b_ref[0]


def _ada(c_all, w_ada, b_ada):
    depth, d, n6 = w_ada.shape
    m = c_all.shape[0]
    tn = 512
    return pl.pallas_call(
        _ada_body,
        grid=(depth, n6 // tn),
        in_specs=[
            pl.BlockSpec((m, d), lambda l, j: (0, 0)),
            pl.BlockSpec((1, d, tn), lambda l, j: (l, 0, j)),
            pl.BlockSpec((1, 1, tn), lambda l, j: (l, 0, j)),
        ],
        out_specs=pl.BlockSpec((1, m, tn), lambda l, j: (l, 0, j)),
        out_shape=jax.ShapeDtypeStruct((depth, m, n6), F32),
        compiler_params=_params(("arbitrary", "arbitrary"), 24 * 2**20),
        name="ada_mod",
    )(c_all, w_ada, b_ada.reshape(depth, 1, n6))


def _mm_body(x_ref, w_ref, o_ref):
    o_ref[...] = _mm1(x_ref[...], w_ref[...])


def _matmul(x, w):
    m, k = x.shape
    n = w.shape[1]
    return pl.pallas_call(
        _mm_body,
        grid=(1,),
        in_specs=[_const_spec((m, k)), _const_spec((k, n))],
        out_specs=_const_spec((m, n)),
        out_shape=jax.ShapeDtypeStruct((m, n), F32),
        compiler_params=_params(("arbitrary",), 24 * 2**20),
        name="shift_proj",
    )(x, w)


def _in_body(seq_mode, tm, x_ref, sc_ref, sh_ref, w_ref, mu_ref, zp_ref, lora_ref,
             w0a0_ref, kk_ref, ka_ref, lng_ref, lnb_ref, bd_ref,
             r_ref, lw_ref, kp_ref, v_ref, kap_ref, bb_ref, u_ref, vg_ref,
             ga_ref, gb_ref, hl_ref, carry_ref):
    h = x_ref[0] * (1.0 + sc_ref[0]) + sh_ref[0]
    if seq_mode:
        hl_ref[0] = h[tm - 1:tm, :]
    else:
        hl_ref[0] = h
    hb = h.astype(BF16)
    proj = lambda lo, hi: _dot(hb, w_ref[:, lo:hi])
    za = proj(0, N_SHIFT_PAD)
    if seq_mode:
        @pl.when(pl.program_id(1) == 0)
        def _():
            carry_ref[...] = zp_ref[0]

        row = lax.broadcasted_iota(jnp.int32, za.shape, 0)
        prev = jnp.where(row == 0, carry_ref[...], pltpu.roll(za, 1, 0))
        carry_ref[...] = za[tm - 1:tm, :]
    else:
        prev = zp_ref[0]
    mix = za + mu_ref[...] * (prev - za)
    r = mix[:, 0:D_A]
    k = mix[:, D_A:2 * D_A]
    v = mix[:, 2 * D_A:3 * D_A]
    xwa = mix[:, 3 * D_A:N_SHIFT_PAD]
    lane = lax.broadcasted_iota(jnp.int32, xwa.shape, 1)
    lora_in = jnp.where(lane < R_LORA, jnp.tanh(xwa), xwa)
    pre = w0a0_ref[...] + _mm3(lora_in, lora_ref[...])
    yw = -pre[:, :D_A]
    softplus = jnp.maximum(yw, 0.0) + jnp.log1p(jnp.exp(-jnp.abs(yw)))
    lw = -jnp.exp(-softplus - 0.5)
    a = jax.nn.sigmoid(pre[:, D_A:])
    bd = bd_ref[...]
    kk = k * kk_ref[...]
    kap = kk / jnp.maximum(jnp.sqrt(_segsum(kk * kk, bd)), 1e-12)
    r_ref[0] = r
    lw_ref[0] = lw
    kp_ref[0] = k * (1.0 + (a - 1.0) * ka_ref[...])
    v_ref[0] = v
    kap_ref[0] = kap
    bb_ref[0] = kap * a
    u_ref[0] = _gelu(proj(COL_U, COL_VG))
    vg_ref[0] = _layer_norm(_gelu(proj(COL_VG, COL_GA)), lng_ref[...], lnb_ref[...])
    ga_ref[0] = jax.nn.sigmoid(proj(COL_GA, COL_GB))
    gb_ref[0] = jax.nn.sigmoid(proj(COL_GB, N_IN_PAD))


def _in_stage(seq_mode, x, sc, sh, zprev, p):
    b, t, d = x.shape
    tm = 256 if seq_mode else t
    tmod = 1 if seq_mode else tm
    grid = (b, t // tm)
    tok = lambda n: pl.BlockSpec((1, tm, n), lambda i, j: (i, j, 0))
    mod = pl.BlockSpec((1, tmod, d), lambda i, j: (i, j if not seq_mode else 0, 0))
    zp_spec = (pl.BlockSpec((1, 1, N_SHIFT_PAD), lambda i, j: (i, 0, 0)) if seq_mode
               else tok(N_SHIFT_PAD))
    hl_spec = (pl.BlockSpec((1, 1, d), lambda i, j: (i, 0, 0)) if seq_mode else tok(d))
    hl_shape = (b, 1, d) if seq_mode else (b, t, d)
    out_cols = [D_A] * 6 + [D_B] * 2 + [d] * 2
    est = (2 * tm * (d + N_SHIFT_PAD + sum(out_cols) + d) * 4 + 2 * d * N_IN_PAD * 2
           + 3 * tm * N_IN_PAD * 4 + 4 * 2**20)
    outs = pl.pallas_call(
        functools.partial(_in_body, seq_mode, tm),
        grid=grid,
        in_specs=[tok(d), mod, mod,
                  _const_spec((d, N_IN_PAD), True), _const_spec((1, N_SHIFT_PAD)), zp_spec,
                  _const_spec((LANES, 2 * D_A)), _const_spec((1, 2 * D_A)),
                  _const_spec((1, D_A)), _const_spec((1, D_A)),
                  _const_spec((1, D_B)), _const_spec((1, D_B)),
                  _const_spec((D_A, D_A))],
        out_specs=[tok(n) for n in out_cols] + [hl_spec],
        out_shape=[jax.ShapeDtypeStruct((b, t, n), F32) for n in out_cols]
        + [jax.ShapeDtypeStruct(hl_shape, F32)],
        scratch_shapes=[pltpu.VMEM((1, N_SHIFT_PAD), F32)],
        compiler_params=_params(("arbitrary", "arbitrary"), est),
        name="in_stage_seq" if seq_mode else "in_stage_row",
    )(x, sc, sh, p["w_in"], p["mu"], zprev, p["lora"], p["w0a0"], p["k_k"], p["k_a"],
      p["lnv_g"], p["lnv_b"], p["bd"])
    return outs


_WKV_3PASS = frozenset({"ark", "arkv", "h"})


def _wmm(name, a, b, dims):
    return (_mm3 if name in _WKV_3PASS else _mm1)(a, b, dims)


def _chunk_pairs(x, nsub):
    c = WKV_CHUNK
    w = 2 * HEAD
    return jnp.stack([x[c * i:c * (i + 1), w * q:w * (q + 1)]
                      for i in range(nsub) for q in range(N_HEADS // 2)], axis=0)


def _masked_rowsum(mask_bf, x):
    h1, h2 = _split2(x)
    return _dot(mask_bf, h1) + _dot(mask_bf, h2)


def _masked_rowsum_t(x, mask):
    h1, h2 = _split2(x)
    m = mask.astype(BF16)
    return _dot(h1, m) + _dot(h2, m)


def _wkv_seq_body(nsub, r_ref, lw_ref, kp_ref, v_ref, kap_ref, bb_ref, s0_ref,
                  o_ref, sout_ref, s_scr):
    c = WKV_CHUNK
    tc = nsub * c
    npair = N_HEADS // 2

    @pl.when(pl.program_id(1) == 0)
    def _():
        for q in range(npair):
            s_scr[q] = jnp.concatenate([s0_ref[0, 2 * q], s0_ref[0, 2 * q + 1]], axis=-1)

    row_t = lax.broadcasted_iota(jnp.int32, (tc, tc), 0)
    col_t = lax.broadcasted_iota(jnp.int32, (tc, tc), 1)
    shift = c.bit_length() - 1
    same_chunk = (row_t >> shift) == (col_t >> shift)
    lw = lw_ref[0]
    g = _masked_rowsum((same_chunk & (row_t >= col_t)).astype(BF16), lw)
    g_end = _masked_rowsum(same_chunk.astype(BF16), lw)
    e_neg = jnp.exp(-g)
    e_end = jnp.exp(g_end - g)
    cp = functools.partial(_chunk_pairs, nsub=nsub)
    k = kp_ref[0]
    b = bb_ref[0]
    kap_t = cp(kap_ref[0] * jnp.exp(g - lw))
    b_t = cp(b * e_neg)
    k_t = cp(k * e_neg)
    r_t = cp(r_ref[0] * jnp.exp(g))
    b_e = cp(b * e_end)
    k_e = cp(k * e_end)
    vv = cp(v_ref[0])
    decay_end = cp(jnp.exp(g_end))

    row = lax.broadcasted_iota(jnp.int32, (c, 2 * c), 0)[None]
    lane = lax.broadcasted_iota(jnp.int32, (c, 2 * c), 1)[None]
    colp = lane & (c - 1)
    right = lane >= c
    row2 = lax.broadcasted_iota(jnp.int32, (2 * c, 2 * c), 0)[None]
    lane2 = lax.broadcasted_iota(jnp.int32, (2 * c, 2 * c), 1)[None]
    same_head = (row2 >= c) == (lane2 >= c)

    def bd(x):
        return jnp.concatenate([jnp.where(right, 0.0, x), jnp.where(right, x, 0.0)], axis=1)

    p_b = _wmm("lb", jnp.concatenate([kap_t, r_t], axis=1), bd(b_t), _B_NT)
    l_b = jnp.where(row > colp, p_b[:, :c], 0.0)
    a_rb = jnp.where(row >= colp, p_b[:, c:], 0.0)
    bd_k = bd(k_t)
    l_k = jnp.where(row > colp, _wmm("lk", kap_t, bd_k, _B_NT), 0.0)
    a_rk = jnp.where(row >= colp, _wmm("ark", r_t, bd_k, _B_NT), 0.0)
    m = -l_b
    t_inv = jnp.where(row == colp, 1.0, 0.0) + m
    m = _wmm("inv", m, bd(m), _B_NN)
    span = 2
    while 2 * span < c:
        both = _wmm("inv", jnp.concatenate([m, t_inv], axis=1), bd(m), _B_NN)
        m = both[:, :c]
        t_inv = t_inv + both[:, c:]
        span *= 2
    t_inv = t_inv + _wmm("inv", t_inv, bd(m), _B_NN)
    bd_v = bd(vv)
    a1 = -_wmm("ta", t_inv, bd(kap_t), _B_NN)
    u0 = -_wmm("tu", t_inv, bd(_wmm("lkv", l_k, bd_v, _B_NN)), _B_NN)
    a2 = r_t + _wmm("arba", a_rb, bd(a1), _B_NN)
    o0 = _wmm("arbu", a_rb, bd(u0), _B_NN) + _wmm("arkv", a_rk, bd_v, _B_NN)
    g_bd = (jnp.where(same_head, _wmm("g", a1, b_e, _B_TN), 0.0)
            + jnp.where(row2 == lane2, decay_end[:, 0:1, :], 0.0))
    hh = _wmm("h", jnp.concatenate([u0, vv], axis=1),
              jnp.concatenate([b_e, k_e], axis=1), _B_TN)
    h_pair = jnp.where(right, hh[:, c:], hh[:, :c])

    s = s_scr[...]
    for i in range(nsub):
        ps = slice(npair * i, npair * (i + 1))
        o = _wmm("o", a2[ps], bd(s), _B_NT) + o0[ps]
        s = _wmm("s", s, g_bd[ps], _B_NN) + h_pair[ps]
        for q in range(npair):
            o_ref[0, c * i:c * (i + 1), 2 * HEAD * q:2 * HEAD * (q + 1)] = o[q]
    s_scr[...] = s

    @pl.when(pl.program_id(1) == pl.num_programs(1) - 1)
    def _():
        for q in range(npair):
            sout_ref[0, 2 * q] = s[q][:, :HEAD]
            sout_ref[0, 2 * q + 1] = s[q][:, HEAD:]


def _wkv_seq(r, lw, kp, v, kap, bb, s0):
    b, t, _ = r.shape
    tc = 256
    nsub = tc // WKV_CHUNK
    tok = pl.BlockSpec((1, tc, D_A), lambda i, j: (i, j, 0))
    st = pl.BlockSpec((1, N_HEADS, HEAD, HEAD), lambda i, j: (i, 0, 0, 0))
    return pl.pallas_call(
        functools.partial(_wkv_seq_body, nsub),
        grid=(b, t // tc),
        in_specs=[tok] * 6 + [st],
        out_specs=[tok, st],
        out_shape=[jax.ShapeDtypeStruct((b, t, D_A), F32),
                   jax.ShapeDtypeStruct((b, N_HEADS, HEAD, HEAD), F32)],
        scratch_shapes=[pltpu.VMEM((N_HEADS // 2, HEAD, 2 * HEAD), F32)],
        compiler_params=_params(("arbitrary", "arbitrary"), 32 * 2**20),
        name="wkv_seq",
    )(r, lw, kp, v, kap, bb, s0)


def _wkv_step_body(r_ref, lw_ref, kp_ref, v_ref, kap_ref, bb_ref, s0_ref, _, o_ref, sout_ref):
    eye = (lax.broadcasted_iota(jnp.int32, (HEAD, HEAD), 0)
           == lax.broadcasted_iota(jnp.int32, (HEAD, HEAD), 1))[None]
    for h in range(N_HEADS):
        hs = slice(HEAD * h, HEAD * (h + 1))
        row = lambda ref: ref[:, hs][:, None, :]
        s = s0_ref[:, h]
        sa = -jnp.sum(s * row(kap_ref), axis=-1, keepdims=True)
        v_col = jnp.sum(jnp.where(eye, row(v_ref), 0.0), axis=-1, keepdims=True)
        s_new = s * jnp.exp(row(lw_ref)) + sa * row(bb_ref) + v_col * row(kp_ref)
        sout_ref[:, h] = s_new
        o_col = jnp.sum(s_new * row(r_ref), axis=-1, keepdims=True)
        o_ref[:, hs] = jnp.sum(jnp.where(eye, o_col, 0.0), axis=1)


def _wkv_step(r, lw, kp, v, kap, bb, s0, s_new_all, layer):
    m = r.shape[0]
    nb = 8
    tok = pl.BlockSpec((nb, D_A), lambda i: (i, 0))
    st = pl.BlockSpec((nb, N_HEADS, HEAD, HEAD), lambda i: (i, 0, 0, 0))
    st_out = pl.BlockSpec((None, nb, N_HEADS, HEAD, HEAD), lambda i: (layer, i, 0, 0, 0))
    return pl.pallas_call(
        _wkv_step_body,
        grid=(m // nb,),
        in_specs=[tok] * 6 + [st, pl.BlockSpec(memory_space=pl.ANY)],
        out_specs=[tok, st_out],
        out_shape=[jax.ShapeDtypeStruct((m, D_A), F32),
                   jax.ShapeDtypeStruct(s_new_all.shape, F32)],
        input_output_aliases={7: 1},
        compiler_params=_params(("arbitrary",), 24 * 2**20),
        name="wkv_step",
    )(r, lw, kp, v, kap, bb, s0, s_new_all)


def _merge_body(seq_mode, tm, x_ref, o_ref, r_ref, kp_ref, v_ref, u_ref, vg_ref,
                ga_ref, gb_ref, g1_ref, sc2_ref, sh2_ref, lxg_ref, lxb_ref, rk_ref,
                bd_ref, ws_ref, bs_ref, pa_ref, pb_ref, wo_ref, l1g_ref, l1b_ref,
                x1_ref, h2_ref):
    bd = bd_ref[...]
    o = o_ref[0]
    inv_n = 1.0 / HEAD
    mu = _segsum(o, bd) * inv_n
    d = o - mu
    var = _segsum(d * d, bd) * inv_n
    on = d * lax.rsqrt(var + GN_EPS) * lxg_ref[...] + lxb_ref[...]
    v = v_ref[0]
    o_a = on + _segsum(r_ref[0] * kp_ref[0] * rk_ref[...], bd) * v
    vg = vg_ref[0]
    if seq_mode:
        row = lax.broadcasted_iota(jnp.int32, (CHUNK, CHUNK), 0)
        col = lax.broadcasted_iota(jnp.int32, (CHUNK, CHUNK), 1)
        lane = lax.broadcasted_iota(jnp.int32, (CHUNK, D_B), 1)
        gsz = D_B // N_GROUPS_B
        w_cat = jnp.concatenate(
            [jnp.where(row >= col, ws_ref[g], 0.0).astype(BF16) for g in range(N_GROUPS_B)], axis=1)
        pieces = []
        for j in range(tm // CHUNK):
            vc = vg[j * CHUNK:(j + 1) * CHUNK, :].astype(BF16)
            v_bd = jnp.concatenate(
                [jnp.where((lane >= g * gsz) & (lane < (g + 1) * gsz), vc, 0.0)
                 for g in range(N_GROUPS_B)], axis=0)
            pieces.append(_dot(w_cat, v_bd) + bs_ref[...])
        s = jnp.concatenate(pieces, axis=0) if len(pieces) > 1 else pieces[0]
    else:
        s = vg * ws_ref[...] + bs_ref[...]
    o_b = u_ref[0] * s
    y = _mm1(ga_ref[0] * _mm1(o_a, pa_ref[...]) + gb_ref[0] * _mm1(o_b, pb_ref[...]),
             wo_ref[...])
    x1 = _layer_norm(ALPHA * x_ref[0] + g1_ref[0] * y, l1g_ref[...], l1b_ref[...])
    x1_ref[0] = x1
    h2_ref[0] = x1 * (1.0 + sc2_ref[0]) + sh2_ref[0]


def _merge_stage(seq_mode, x, o, r, kp, v, u, vg, ga, gb, g1, sc2, sh2, p):
    b, t, d = x.shape
    tm = 256 if seq_mode else t
    tmod = 1 if seq_mode else tm
    tok = lambda n: pl.BlockSpec((1, tm, n), lambda i, j: (i, j, 0))
    mod = pl.BlockSpec((1, tmod, d), lambda i, j: (i, j if not seq_mode else 0, 0))
    ws, bs = (p["w_spatial"], p["b_spatial_full"]) if seq_mode else (p["ws_row"], p["bs_row"])
    est = 2 * tm * (2 * d + 7 * D_A + 2 * d + 2 * d) * 4 + 8 * tm * d * 4 + 16 * 2**20
    return pl.pallas_call(
        functools.partial(_merge_body, seq_mode, tm),
        grid=(b, t // tm),
        in_specs=[tok(d)] + [tok(D_A)] * 6 + [tok(d), tok(d), mod, mod, mod,
                  _const_spec((1, D_A)), _const_spec((1, D_A)), _const_spec((1, D_A)),
                  _const_spec((D_A, D_A)), _const_spec(ws.shape), _const_spec(bs.shape),
                  _const_spec((D_A, d)), _const_spec((D_B, d)), _const_spec((d, d)),
                  _const_spec((1, d)), _const_spec((1, d))],
        out_specs=[tok(d), tok(d)],
        out_shape=[jax.ShapeDtypeStruct((b, t, d), F32)] * 2,
        compiler_params=_params(("arbitrary", "arbitrary"), est),
        name="merge_seq" if seq_mode else "merge_row",
    )(x, o, r, kp, v, u, vg, ga, gb, g1, sc2, sh2, p["lnx_g"], p["lnx_b"], p["r_k"],
      p["bd"], ws, bs, p["w_branch_a"], p["w_branch_b"], p["w_out"], p["ln1_g"], p["ln1_b"])


def _route_body(h_ref, w_ref, b_ref, gate_ref, idx_ref, wts_ref, cnt_ref):
    lg = _mm3(h_ref[...], w_ref[...]) + b_ref[...]
    lane = lax.broadcasted_iota(jnp.int32, lg.shape, 1)
    lanef = lane.astype(F32)
    neg = -jnp.inf
    is_g = (lane >= N_EXPERTS) & (lane < N_EXPERTS + N_ROUTE_GROUPS)
    mg = jnp.max(jnp.where(is_g, lg, neg), axis=-1, keepdims=True)
    gidx = jnp.min(jnp.where(is_g & (lg == mg), lanef - N_EXPERTS, 1e9), axis=-1, keepdims=True)
    pg_sel = 1.0 / jnp.sum(jnp.where(is_g, jnp.exp(lg - mg), 0.0), axis=-1, keepdims=True)
    lo = gidx * EXP_PER_GROUP
    in_grp = (lanef >= lo) & (lanef < lo + EXP_PER_GROUP)
    t1 = jnp.max(jnp.where(in_grp, lg, neg), axis=-1, keepdims=True)
    i1 = jnp.min(jnp.where(in_grp & (lg == t1), lanef, 1e9), axis=-1, keepdims=True)
    rest = in_grp & (lanef != i1)
    t2 = jnp.max(jnp.where(rest, lg, neg), axis=-1, keepdims=True)
    i2 = jnp.min(jnp.where(rest & (lg == t2), lanef, 1e9), axis=-1, keepdims=True)
    e2 = jnp.exp(t2 - t1)
    w1 = pg_sel / (1.0 + e2)
    w2 = pg_sel * e2 / (1.0 + e2)
    gate_ref[...] = jnp.where(lanef == i1, w1, 0.0) + jnp.where(lanef == i2, w2, 0.0)
    tm = lg.shape[0]
    hit = (lanef == i1) | (lanef == i2)
    earlier = (lax.broadcasted_iota(jnp.int32, (tm, tm), 0)
               > lax.broadcasted_iota(jnp.int32, (tm, tm), 1))
    rank = _dot(earlier.astype(BF16), hit.astype(BF16))
    cnt = jnp.sum(hit.astype(F32), axis=0, keepdims=True)
    cnt8 = jnp.floor((cnt + 7.0) * 0.125) * 8.0
    lower_expert = (lax.broadcasted_iota(jnp.int32, (LANES, LANES), 0)
                    < lax.broadcasted_iota(jnp.int32, (LANES, LANES), 1))
    run_start = _masked_rowsum_t(jnp.broadcast_to(cnt8, (8, LANES)), lower_expert)[0:1]
    lpos = run_start + rank
    lp1 = jnp.sum(jnp.where(lanef == i1, lpos, 0.0), axis=-1, keepdims=True)
    lp2 = jnp.sum(jnp.where(lanef == i2, lpos, 0.0), axis=-1, keepdims=True)
    cnt_ref[0] = cnt.astype(jnp.int32)
    idx = jnp.where(lane == 0, i1, jnp.where(lane == 1, i2, jnp.where(lane == 2, lp1, lp2)))
    idx_ref[...] = idx.astype(jnp.int32)
    wts_ref[...] = jnp.where(lane == 0, w1, w2)


def _route(h2, p):
    m, d = h2.shape
    tm = min(m, MOE_TILE)
    tok = pl.BlockSpec((tm, LANES), lambda i: (i, 0))
    return pl.pallas_call(
        _route_body,
        grid=(m // tm,),
        in_specs=[pl.BlockSpec((tm, d), lambda i: (i, 0)),
                  _const_spec((d, LANES)), _const_spec((1, LANES))],
        out_specs=[tok, tok, tok, pl.BlockSpec((1, 1, LANES), lambda i: (i, 0, 0))],
        out_shape=[jax.ShapeDtypeStruct((m, LANES), F32),
                   jax.ShapeDtypeStruct((m, LANES), jnp.int32),
                   jax.ShapeDtypeStruct((m, LANES), F32),
                   jax.ShapeDtypeStruct((m // tm, 1, LANES), jnp.int32)],
        compiler_params=_params(("arbitrary",), 24 * 2**20),
        name="route",
    )(h2, p["w_route"], p["b_route"])


def _moe_body(h_ref, gate_ref, wg_ref, wu_ref, wd_ref, x1_ref, g2_ref, l2g_ref, l2b_ref,
              out_ref, acc_ref, xb_ref):
    e = pl.program_id(1)

    @pl.when(e == 0)
    def _():
        acc_ref[...] = jnp.zeros_like(acc_ref)
        xb_ref[...] = h_ref[...].astype(BF16)

    xb = xb_ref[...]
    pre = _dot(xb, wg_ref[0].astype(BF16))
    hid = pre * jax.nn.sigmoid(pre) * _dot(xb, wu_ref[0].astype(BF16))
    ye = _mm1(hid, wd_ref[0])
    gate = gate_ref[...]
    lane = lax.broadcasted_iota(jnp.int32, gate.shape, 1)
    ge = jnp.sum(jnp.where(lane == e, gate, 0.0), axis=-1, keepdims=True)
    acc_ref[...] += ge * ye

    @pl.when(e == pl.num_programs(1) - 1)
    def _():
        out_ref[...] = _layer_norm(ALPHA * x1_ref[...] + g2_ref[0] * acc_ref[...],
                                   l2g_ref[...], l2b_ref[...])


def _moe(seq_len, h2, gate, x1, g2, p):
    m, d = h2.shape
    layer = p["layer"]
    tm = min(seq_len if g2.shape[1] == 1 else m, 1024)
    tok = lambda n: pl.BlockSpec((tm, n), lambda i, e: (i, 0))
    if g2.shape[1] == 1:
        tiles_per_seq = seq_len // tm
        g2_spec = pl.BlockSpec((1, 1, d), lambda i, e: (i // tiles_per_seq, 0, 0))
    else:
        g2_spec = pl.BlockSpec((1, tm, d), lambda i, e: (0, i, 0))
    est = 2 * tm * (3 * d + LANES) * 4 + tm * d * 6 + 4 * tm * d * 4 + 8 * 2**20
    return pl.pallas_call(
        _moe_body,
        grid=(m // tm, N_EXPERTS),
        in_specs=[tok(d), tok(LANES),
                  pl.BlockSpec((None, 1, d, D_EXPERT), lambda i, e: (layer, e, 0, 0)),
                  pl.BlockSpec((None, 1, d, D_EXPERT), lambda i, e: (layer, e, 0, 0)),
                  pl.BlockSpec((None, 1, D_EXPERT, d), lambda i, e: (layer, e, 0, 0)),
                  tok(d), g2_spec, _const_spec((1, d)), _const_spec((1, d))],
        out_specs=tok(d),
        out_shape=jax.ShapeDtypeStruct((m, d), F32),
        scratch_shapes=[pltpu.VMEM((tm, d), F32), pltpu.VMEM((tm, d), BF16)],
        compiler_params=_params(("arbitrary", "arbitrary"), est),
        name="moe_dense",
    )(h2, gate, p["w_exp_gate"], p["w_exp_up"], p["w_exp_down"], x1, g2, p["ln2_g"], p["ln2_b"])


MOE_TILE = 512
MOE_BLOCK = 512
MOE_RUN_ALIGN = 8
MOE_LOCAL_ROWS = 2 * MOE_TILE + 256
MOE_SLABS = tuple(2 ** k for k in range(9, 2, -1))


def _for_each_slab(run_ref, make_copy, fn):
    def one_run(e, c):
        dst = run_ref[0, 0, e]
        src = run_ref[0, 0, N_EXPERTS + e]
        n = run_ref[0, 0, 2 * N_EXPERTS + e]
        for slab in MOE_SLABS:
            off = n & (-2 * slab)

            @pl.when((n & slab) != 0)
            def _():
                fn(make_copy(pl.multiple_of(dst + off, MOE_RUN_ALIGN),
                             pl.multiple_of(src + off, MOE_RUN_ALIGN), slab))
        return c

    lax.fori_loop(0, N_EXPERTS, one_run, 0)


def _dispatch_body(tail_ref, run_ref, prev_run_ref, h_ref, idx_ref, xs_hbm, zero_buf, loc_buf,
                   sem, zsem):
    nb_max = xs_hbm.shape[0] // MOE_BLOCK
    nt = MOE_TILE

    def zero_copy(row0):
        return pltpu.make_async_copy(
            zero_buf, xs_hbm.at[pl.ds(pl.multiple_of(row0, MOE_BLOCK), MOE_BLOCK)], zsem)

    @pl.when(pl.program_id(0) == 0)
    def _():
        zero_buf[...] = jnp.zeros_like(zero_buf)
        n_used = tail_ref[N_EXPERTS]

        def each_zero_copy(fn):
            for e in range(N_EXPERTS):
                @pl.when(tail_ref[e] >= 0)
                def _():
                    fn(zero_copy(tail_ref[e]))

            def unused(j, c):
                fn(zero_copy(j * MOE_BLOCK))
                return c

            lax.fori_loop(n_used, nb_max, unused, 0)

        each_zero_copy(lambda cp: cp.start())
        each_zero_copy(lambda cp: cp.wait())

    lp = idx_ref[...].astype(F32)
    eye = (lax.broadcasted_iota(jnp.int32, (nt, nt), 0)
           == lax.broadcasted_iota(jnp.int32, (nt, nt), 1))
    as_row = lambda col: jnp.sum(jnp.where(eye, col, 0.0), axis=0, keepdims=True)
    r = lax.broadcasted_iota(jnp.int32, (MOE_LOCAL_ROWS, nt), 0).astype(F32)
    pick = (r == as_row(lp[:, 2:3])) | (r == as_row(lp[:, 3:4]))
    step = pl.program_id(0)
    slot = step & 1
    loc_buf[slot] = _dot(pick.astype(BF16), h_ref[...].astype(BF16))

    def copies_from(which):
        def make_copy(dst, src, rows):
            return pltpu.make_async_copy(loc_buf.at[which, pl.ds(src, rows)],
                                         xs_hbm.at[pl.ds(dst, rows)], sem.at[which])
        return make_copy

    _for_each_slab(run_ref, copies_from(slot), lambda cp: cp.start())

    @pl.when(step > 0)
    def _():
        _for_each_slab(prev_run_ref, copies_from(1 - slot), lambda cp: cp.wait())

    @pl.when(step == pl.num_programs(0) - 1)
    def _():
        _for_each_slab(run_ref, copies_from(slot), lambda cp: cp.wait())


def _dispatch(h2, idx, runs, tail_start, n_rows):
    m, d = h2.shape
    nt = MOE_TILE
    run_spec = lambda at: pl.BlockSpec((1, 1, LANES), lambda i, tail: (at(i), 0, 0),
                                       memory_space=pltpu.SMEM)
    grid_spec = pltpu.PrefetchScalarGridSpec(
        num_scalar_prefetch=1,
        grid=(m // nt,),
        in_specs=[run_spec(lambda i: i), run_spec(lambda i: jnp.maximum(i - 1, 0)),
                  pl.BlockSpec((nt, d), lambda i, tail: (i, 0)),
                  pl.BlockSpec((nt, LANES), lambda i, tail: (i, 0))],
        out_specs=pl.BlockSpec(memory_space=pl.ANY),
        scratch_shapes=[pltpu.VMEM((MOE_BLOCK, d), F32),
                        pltpu.VMEM((2, MOE_LOCAL_ROWS, d), F32),
                        pltpu.SemaphoreType.DMA((2,)), pltpu.SemaphoreType.DMA],
    )
    return pl.pallas_call(
        _dispatch_body,
        grid_spec=grid_spec,
        out_shape=jax.ShapeDtypeStruct((n_rows, d), F32),
        compiler_params=_params(("arbitrary",), 40 * 2**20),
        name="moe_dispatch",
    )(tail_start, runs, runs, h2, idx)


def _gmm_body(be_ref, nb_ref, x_ref, wg_ref, wu_ref, wd_ref, y_ref):
    j = pl.program_id(0)

    @pl.when(j < nb_ref[0])
    def _():
        xb = x_ref[...].astype(BF16)
        pre = _dot(xb, wg_ref[0].astype(BF16))
        hid = pre * jax.nn.sigmoid(pre) * _dot(xb, wu_ref[0].astype(BF16))
        y_ref[...] = _mm1(hid, wd_ref[0])

    @pl.when(j >= nb_ref[0])
    def _():
        y_ref[...] = jnp.zeros_like(y_ref)


def _gmm(xs, blk_expert, n_blocks, p):
    n_rows, d = xs.shape
    nb_max = n_rows // MOE_BLOCK
    layer = p["layer"]
    live = lambda j, be, nb: jnp.minimum(j, nb[0] - 1)
    wspec = lambda shape: pl.BlockSpec((None,) + shape,
                                       lambda j, be, nb: (layer, be[live(j, be, nb)], 0, 0))
    grid_spec = pltpu.PrefetchScalarGridSpec(
        num_scalar_prefetch=2,
        grid=(nb_max,),
        in_specs=[pl.BlockSpec((MOE_BLOCK, d), lambda j, be, nb: (live(j, be, nb), 0)),
                  wspec((1, d, D_EXPERT)), wspec((1, d, D_EXPERT)), wspec((1, D_EXPERT, d))],
        out_specs=pl.BlockSpec((MOE_BLOCK, d), lambda j, be, nb: (j, 0)),
    )
    return pl.pallas_call(
        _gmm_body,
        grid_spec=grid_spec,
        out_shape=jax.ShapeDtypeStruct((n_rows, d), F32),
        compiler_params=_params(("arbitrary",), 24 * 2**20),
        name="moe_gmm",
    )(blk_expert, n_blocks, xs, p["w_exp_gate"], p["w_exp_up"], p["w_exp_down"])


def _combine_body(run_ref, next_run_ref, ys_hbm, idx_ref, wts_ref, x1_ref, g2_ref, l2g_ref,
                  l2b_ref, out_ref, loc_buf, sem):
    nt = MOE_TILE
    step = pl.program_id(0)
    slot = step & 1

    def copies_into(which):
        def make_copy(dst, src, rows):
            return pltpu.make_async_copy(ys_hbm.at[pl.ds(dst, rows)],
                                         loc_buf.at[which, pl.ds(src, rows)], sem.at[which])
        return make_copy

    @pl.when(step == 0)
    def _():
        loc_buf[...] = jnp.zeros_like(loc_buf)
        _for_each_slab(run_ref, copies_into(slot), lambda cp: cp.start())

    @pl.when(step < pl.num_programs(0) - 1)
    def _():
        _for_each_slab(next_run_ref, copies_into(1 - slot), lambda cp: cp.start())

    _for_each_slab(run_ref, copies_into(slot), lambda cp: cp.wait())
    lp = idx_ref[...].astype(F32)
    w = wts_ref[...]
    c = lax.broadcasted_iota(jnp.int32, (nt, MOE_LOCAL_ROWS), 1).astype(F32)
    sel = (jnp.where(c == lp[:, 2:3], w[:, 0:1], 0.0)
           + jnp.where(c == lp[:, 3:4], w[:, 1:2], 0.0))
    moe = _mm3(sel, loc_buf[slot])
    out_ref[...] = _layer_norm(ALPHA * x1_ref[...] + g2_ref[0] * moe, l2g_ref[...], l2b_ref[...])


def _combine(seq_len, ys, idx, wts, runs, x1, g2, p):
    m, d = x1.shape
    nt = MOE_TILE
    tiles_per_seq = seq_len // nt
    tok = lambda n: pl.BlockSpec((nt, n), lambda i: (i, 0))
    n_tiles = m // nt
    run_spec = lambda at: pl.BlockSpec((1, 1, LANES), lambda i: (at(i), 0, 0),
                                       memory_space=pltpu.SMEM)
    return pl.pallas_call(
        _combine_body,
        grid=(n_tiles,),
        in_specs=[run_spec(lambda i: i), run_spec(lambda i: jnp.minimum(i + 1, n_tiles - 1)),
                  pl.BlockSpec(memory_space=pl.ANY), tok(LANES), tok(LANES), tok(d),
                  pl.BlockSpec((1, 1, d), lambda i: (i // tiles_per_seq, 0, 0)),
                  _const_spec((1, d)), _const_spec((1, d))],
        out_specs=tok(d),
        out_shape=jax.ShapeDtypeStruct((m, d), F32),
        scratch_shapes=[pltpu.VMEM((2, MOE_LOCAL_ROWS, d), F32), pltpu.SemaphoreType.DMA((2,))],
        compiler_params=_params(("arbitrary",), 48 * 2**20),
        name="moe_combine",
    )(runs, runs, ys, idx, wts, x1, g2, p["ln2_g"], p["ln2_b"])


def _moe_routed(seq_len, h2, idx, wts, tile_cnt, x1, g2, p):
    m, d = h2.shape
    blk = MOE_BLOCK
    n_tiles = m // MOE_TILE
    worst_rows = 2 * m + n_tiles * N_EXPERTS * (MOE_RUN_ALIGN - 1) + N_EXPERTS * (blk - 1)
    nb_max = -(-worst_rows // blk)
    cnt = tile_cnt[:, 0, :N_EXPERTS]
    run_len = ((cnt + MOE_RUN_ALIGN - 1) // MOE_RUN_ALIGN) * MOE_RUN_ALIGN
    local_row = jnp.cumsum(run_len, axis=1) - run_len
    rows_before = jnp.cumsum(run_len, axis=0) - run_len
    total = jnp.sum(run_len, axis=0)
    padded = ((total + blk - 1) // blk) * blk
    ends = jnp.cumsum(padded)
    starts = ends - padded
    n_blocks = (ends[-1] // blk).astype(jnp.int32).reshape(1)
    first_row = jnp.arange(nb_max, dtype=jnp.int32) * blk
    blk_expert = jnp.minimum(
        jnp.sum((first_row[:, None] >= ends[None, :]).astype(jnp.int32), axis=1),
        N_EXPERTS - 1).astype(jnp.int32)
    tail_start = jnp.concatenate(
        [jnp.where(padded > 0, ends - blk, -1).astype(jnp.int32), n_blocks])
    runs = jnp.concatenate(
        [starts[None, :] + rows_before, local_row, run_len,
         jnp.zeros((n_tiles, LANES - 3 * N_EXPERTS), jnp.int32)], axis=1).astype(jnp.int32)
    runs = runs.reshape(n_tiles, 1, LANES)
    xs = _dispatch(h2, idx, runs, tail_start, nb_max * blk)
    ys = _gmm(xs, blk_expert, n_blocks, p)
    return _combine(seq_len, ys, idx, wts, runs, x1, g2, p)


def _prep_layer(l, w_in, mu_shift, w0, w_decay_up, a0, w_iclr_up, k_k, k_a, r_k,
                lnx_g, lnx_b, lnv_g, lnv_b, w_spatial, b_spatial, w_branch_a, w_branch_b,
                w_out, ln1_g, ln1_b, w_route_group, b_route_group, w_route_expert,
                b_route_expert, w_exp_gate, w_exp_up, w_exp_down, ln2_g, ln2_b):
    d = D_MODEL
    pad_a = N_SHIFT_PAD - N_SHIFT
    wi = w_in[l]
    w_in_p = jnp.concatenate(
        [wi[:, :N_SHIFT], jnp.zeros((d, pad_a), F32), wi[:, N_SHIFT:]], axis=1).astype(BF16)
    mu = jnp.concatenate([mu_shift[l], jnp.zeros((pad_a,), F32)])[None]
    lora = jnp.zeros((LANES, 2 * D_A), F32)
    lora = lora.at[:R_LORA, :D_A].set(w_decay_up[l]).at[R_LORA:2 * R_LORA, D_A:].set(w_iclr_up[l])
    seg = jnp.arange(D_A) // HEAD
    row1 = lambda x: x.reshape(1, -1)
    gsz = D_B // N_GROUPS_B
    w_route = jnp.concatenate(
        [w_route_expert[l], w_route_group[l],
         jnp.zeros((d, LANES - N_EXPERTS - N_ROUTE_GROUPS), F32)], axis=1)
    b_route = jnp.concatenate(
        [b_route_expert[l], b_route_group[l],
         jnp.zeros((LANES - N_EXPERTS - N_ROUTE_GROUPS,), F32)])[None]
    return dict(
        w_in=w_in_p, mu=mu, lora=lora,
        w0a0=jnp.concatenate([w0[l], a0[l]])[None],
        k_k=row1(k_k[l]), k_a=row1(k_a[l]), r_k=row1(r_k[l]),
        lnx_g=row1(lnx_g[l]), lnx_b=row1(lnx_b[l]),
        lnv_g=row1(lnv_g[l]), lnv_b=row1(lnv_b[l]),
        bd=(seg[:, None] == seg[None, :]).astype(BF16),
        w_spatial=w_spatial[l],
        b_spatial_full=jnp.repeat(b_spatial[l].T, gsz, axis=1),
        ws_row=jnp.repeat(w_spatial[l][:, 0, 0], gsz)[None],
        bs_row=jnp.repeat(b_spatial[l][:, 0], gsz)[None],
        w_branch_a=w_branch_a[l].astype(BF16), w_branch_b=w_branch_b[l].astype(BF16),
        w_out=w_out[l].astype(BF16), ln1_g=row1(ln1_g[l]), ln1_b=row1(ln1_b[l]),
        w_route=w_route, b_route=b_route,
        w_exp_gate=w_exp_gate, w_exp_up=w_exp_up, w_exp_down=w_exp_down, layer=l,
        ln2_g=row1(ln2_g[l]), ln2_b=row1(ln2_b[l]),
    )


def _trunk(seq_mode, x, mods, wkv_in, shift_in, preps):
    b, t, d = x.shape
    wkv_out, shift_out, v_out = [], [], []
    wkv_acc = None if seq_mode else jnp.zeros(wkv_in.shape, F32)
    for l in range(DEPTH):
        p = preps[l]
        sh1, sc1, g1, sh2, sc2, g2 = mods[l]
        if seq_mode:
            zprev = jnp.zeros((b, 1, N_SHIFT_PAD), F32) if shift_in is None else shift_in[l]
        else:
            zprev = _matmul(shift_in[l], p["w_in"][:, :N_SHIFT_PAD]).reshape(b, t, N_SHIFT_PAD)
        r, lw, kp, v, kap, bb, u, vg, ga, gb, hl = _in_stage(seq_mode, x, sc1, sh1, zprev, p)
        if seq_mode:
            o, s_new = _wkv_seq(r, lw, kp, v, kap, bb, wkv_in[l])
        else:
            flat = lambda a: a.reshape(t, D_A)
            o, wkv_acc = _wkv_step(flat(r), flat(lw), flat(kp), flat(v), flat(kap), flat(bb),
                                   wkv_in[l], wkv_acc, l)
            s_new = None
            o = o.reshape(b, t, D_A)
        x1, h2 = _merge_stage(seq_mode, x, o, r, kp, v, u, vg, ga, gb, g1, sc2, sh2, p)
        m = b * t
        h2f = h2.reshape(m, d)
        gate, idx, wts, tile_cnt = _route(h2f, p)
        if seq_mode:
            x = _moe_routed(t, h2f, idx, wts, tile_cnt, x1.reshape(m, d), g2, p)
        else:
            x = _moe(t, h2f, gate, x1.reshape(m, d), g2, p)
        x = x.reshape(b, t, d)
        wkv_out.append(s_new)
        shift_out.append(hl)
        v_out.append(vg)
    if not seq_mode:
        wkv_out = wkv_acc
    return x, wkv_out, shift_out, v_out


def kernel(x_prompt, x_sample, c_prompt, c_sample, state_wkv, state_shift, w_ada, b_ada, w_in, mu_shift, w0, w_decay_up, a0, w_iclr_up, k_k, k_a, r_k, lnx_g, lnx_b, lnv_g, lnv_b, w_spatial, b_spatial, w_branch_a, w_branch_b, w_out, ln1_g, ln1_b, w_route_group, b_route_group, w_route_expert, b_route_expert, w_exp_gate, w_exp_up, w_exp_down, ln2_g, ln2_b):
    bp, tp, d = x_prompt.shape
    bs = x_sample.shape[0]
    layer_params = (w_in, mu_shift, w0, w_decay_up, a0, w_iclr_up, k_k, k_a, r_k, lnx_g,
                    lnx_b, lnv_g, lnv_b, w_spatial, b_spatial, w_branch_a, w_branch_b, w_out,
                    ln1_g, ln1_b, w_route_group, b_route_group, w_route_expert,
                    b_route_expert, w_exp_gate, w_exp_up, w_exp_down, ln2_g, ln2_b)
    preps = [_prep_layer(l, *layer_params) for l in range(DEPTH)]
    mod_all = _ada(jnp.concatenate([c_prompt, c_sample], axis=0), w_ada, b_ada)
    mods_p, mods_s = [], []
    for l in range(DEPTH):
        parts = jnp.split(mod_all[l], 6, axis=-1)
        mods_p.append([q[:bp].reshape(bp, 1, d) for q in parts])
        mods_s.append([q[bp:].reshape(1, bs, d) for q in parts])

    wkv0 = jnp.zeros((DEPTH, bp, N_HEADS, HEAD, HEAD), F32)
    y_p, wkv_p, shift_p, _ = _trunk(True, x_prompt, mods_p, wkv0, None, preps)
    y_s, wkv_s, shift_s, v_s = _trunk(False, x_sample.reshape(1, bs, d), mods_s, state_wkv,
                                      state_shift, preps)
    return (y_p,
            y_s.reshape(bs, 1, d),
            jnp.stack(wkv_p),
            jnp.stack([s.reshape(bp, d) for s in shift_p]),
            wkv_s,
            jnp.stack([s.reshape(bs, d) for s in shift_s]),
            jnp.stack([q.reshape(bs, 1, D_B) for q in v_s]))
```

```python
import functools

import jax
import jax.numpy as jnp
from jax import lax
from jax.experimental import pallas as pl
from jax.experimental.pallas import tpu as pltpu

F32 = jnp.float32
BF16 = jnp.bfloat16

D_MODEL = 1024
DEPTH = 2
HEAD = 64
N_HEADS = 8
D_A = N_HEADS * HEAD
R_LORA = 32
CHUNK = 128
N_GROUPS_B = 8
D_B = 512
N_SHIFT = 3 * D_A + 2 * R_LORA
N_ROUTE_GROUPS = 4
EXP_PER_GROUP = 8
N_EXPERTS = N_ROUTE_GROUPS * EXP_PER_GROUP
D_EXPERT = 256
ALPHA = (2 * DEPTH) ** 0.25
LN_EPS = 1e-5
GN_EPS = 64e-5

LANES = 128
N_SHIFT_PAD = 13 * LANES
COL_U = N_SHIFT_PAD
COL_VG = COL_U + D_B
COL_GA = COL_VG + D_B
COL_GB = COL_GA + D_MODEL
N_IN_PAD = COL_GB + D_MODEL
WKV_CHUNK = 64
IN_ROW_GROUPS = 2
VMEM_CAP_BYTES = 60000 * 1024

_NN = (((1,), (0,)), ((), ()))
_B_NT = (((2,), (2,)), ((0,), (0,)))
_B_NN = (((2,), (1,)), ((0,), (0,)))
_B_TN = (((1,), (1,)), ((0,), (0,)))


def _dot(a, b, dims=_NN):
    return lax.dot_general(a, b, dims, preferred_element_type=F32)


def _split2(x):
    hi = x.astype(BF16)
    lo = (x - hi.astype(F32)).astype(BF16)
    return hi, lo


def _mm1(a, b, dims=_NN):
    return _dot(a.astype(BF16), b.astype(BF16), dims)


def _mm3(a, b, dims=_NN):
    ah, al = _split2(a)
    bh, bl = _split2(b)
    return _dot(ah, bh, dims) + (_dot(ah, bl, dims) + _dot(al, bh, dims))


def _segsum(x, bd):
    hi, lo = _split2(x)
    return _dot(hi, bd) + _dot(lo, bd)


def _layer_norm(x, g, b):
    mu = jnp.mean(x, axis=-1, keepdims=True)
    d = x - mu
    var = jnp.mean(d * d, axis=-1, keepdims=True)
    return d * lax.rsqrt(var + LN_EPS) * g + b


def _gelu(x):
    return 0.5 * x * (1.0 + lax.erf(x * 0.7071067811865476))


def _params(sem, est_bytes):
    limit = int(min(VMEM_CAP_BYTES, max(est_bytes, 16 * 1024 * 1024)))
    return pltpu.CompilerParams(dimension_semantics=sem, vmem_limit_bytes=limit)


def _const_spec(shape, single_buffer=False):
    nd = len(shape)
    if single_buffer:
        return pl.BlockSpec(shape, lambda *_: (0,) * nd, pipeline_mode=pl.Buffered(1))
    return pl.BlockSpec(shape, lambda *_: (0,) * nd)


def _ada_body(c_ref, w_ref, b_ref, o_ref):
    c = c_ref[...]
    s = c * jax.nn.sigmoid(c)
    o_ref[0] = _mm3(s, w_ref[0]) + b_ref[0]


def _ada(c_all, w_ada, b_ada):
    depth, d, n6 = w_ada.shape
    m = c_all.shape[0]
    tn = 512
    return pl.pallas_call(
        _ada_body,
        grid=(depth, n6 // tn),
        in_specs=[
            pl.BlockSpec((m, d), lambda l, j: (0, 0)),
            pl.BlockSpec((1, d, tn), lambda l, j: (l, 0, j)),
            pl.BlockSpec((1, 1, tn), lambda l, j: (l, 0, j)),
        ],
        out_specs=pl.BlockSpec((1, m, tn), lambda l, j: (l, 0, j)),
        out_shape=jax.ShapeDtypeStruct((depth, m, n6), F32),
        compiler_params=_params(("arbitrary", "arbitrary"), 24 * 2**20),
        name="ada_mod",
    )(c_all, w_ada, b_ada.reshape(depth, 1, n6))


def _mm_body(x_ref, w_ref, o_ref):
    o_ref[...] = _mm1(x_ref[...], w_ref[...])


def _matmul(x, w):
    m, k = x.shape
    n = w.shape[1]
    return pl.pallas_call(
        _mm_body,
        grid=(1,),
        in_specs=[_const_spec((m, k)), _const_spec((k, n))],
        out_specs=_const_spec((m, n)),
        out_shape=jax.ShapeDtypeStruct((m, n), F32),
        compiler_params=_params(("arbitrary",), 24 * 2**20),
        name="shift_proj",
    )(x, w)


def _in_body(seq_mode, tm, x_ref, sc_ref, sh_ref, w_ref, mu_ref, zp_ref, lora_ref,
             w0a0_ref, kk_ref, ka_ref, lng_ref, lnb_ref, bd_ref,
             r_ref, lw_ref, kp_ref, v_ref, kap_ref, bb_ref, u_ref, vg_ref,
             ga_ref, gb_ref, hl_ref, carry_ref):
    nsplit = IN_ROW_GROUPS if seq_mode else 1
    rows = tm // nsplit
    if seq_mode:
        @pl.when(pl.program_id(1) == 0)
        def _():
            carry_ref[...] = zp_ref[0]

        carry = carry_ref[...]
    bd = bd_ref[...]
    for part in range(nsplit):
        sl = slice(part * rows, (part + 1) * rows)
        h = x_ref[0, sl, :] * (1.0 + sc_ref[0]) + sh_ref[0]
        hb = h.astype(BF16)
        proj = lambda lo, hi: _dot(hb, w_ref[:, lo:hi])
        za = proj(0, N_SHIFT_PAD)
        if seq_mode:
            row = lax.broadcasted_iota(jnp.int32, za.shape, 0)
            prev = jnp.where(row == 0, carry, pltpu.roll(za, 1, 0))
            carry = za[rows - 1:rows, :]
        else:
            prev = zp_ref[0]
        mix = za + mu_ref[...] * (prev - za)
        r = mix[:, 0:D_A]
        k = mix[:, D_A:2 * D_A]
        v = mix[:, 2 * D_A:3 * D_A]
        xwa = mix[:, 3 * D_A:N_SHIFT_PAD]
        lane = lax.broadcasted_iota(jnp.int32, xwa.shape, 1)
        lora_in = jnp.where(lane < R_LORA, jnp.tanh(xwa), xwa)
        pre = w0a0_ref[...] + _mm3(lora_in, lora_ref[...])
        yw = -pre[:, :D_A]
        softplus = jnp.maximum(yw, 0.0) + jnp.log1p(jnp.exp(-jnp.abs(yw)))
        lw = -jnp.exp(-softplus - 0.5)
        a = jax.nn.sigmoid(pre[:, D_A:])
        kk = k * kk_ref[...]
        kap = kk / jnp.maximum(jnp.sqrt(_segsum(kk * kk, bd)), 1e-12)
        r_ref[0, sl, :] = r
        lw_ref[0, sl, :] = lw
        kp_ref[0, sl, :] = k * (1.0 + (a - 1.0) * ka_ref[...])
        v_ref[0, sl, :] = v
        kap_ref[0, sl, :] = kap
        bb_ref[0, sl, :] = kap * a
        u_ref[0, sl, :] = _gelu(proj(COL_U, COL_VG)).astype(BF16)
        vg_ref[0, sl, :] = _layer_norm(_gelu(proj(COL_VG, COL_GA)), lng_ref[...], lnb_ref[...])
        ga_ref[0, sl, :] = jax.nn.sigmoid(proj(COL_GA, COL_GB)).astype(BF16)
        gb_ref[0, sl, :] = jax.nn.sigmoid(proj(COL_GB, N_IN_PAD)).astype(BF16)
    if seq_mode:
        carry_ref[...] = carry
        hl_ref[0] = h[rows - 1:rows, :]
    else:
        hl_ref[0] = h


def _in_stage(seq_mode, x, sc, sh, zprev, p):
    b, t, d = x.shape
    tm = 256 if seq_mode else t
    tmod = 1 if seq_mode else tm
    grid = (b, t // tm)
    tok = lambda n: pl.BlockSpec((1, tm, n), lambda i, j: (i, j, 0))
    mod = pl.BlockSpec((1, tmod, d), lambda i, j: (i, j if not seq_mode else 0, 0))
    zp_spec = (pl.BlockSpec((1, 1, N_SHIFT_PAD), lambda i, j: (i, 0, 0)) if seq_mode
               else tok(N_SHIFT_PAD))
    hl_spec = (pl.BlockSpec((1, 1, d), lambda i, j: (i, 0, 0)) if seq_mode else tok(d))
    hl_shape = (b, 1, d) if seq_mode else (b, t, d)
    out_cols = [D_A] * 6 + [D_B] * 2 + [d] * 2
    out_dtypes = [F32] * 6 + [BF16, F32] + [BF16] * 2
    est = (2 * tm * (d + N_SHIFT_PAD + sum(out_cols) + d) * 4 + 2 * d * N_IN_PAD * 2
           + 3 * tm * N_IN_PAD * 4 + 4 * 2**20)
    outs = pl.pallas_call(
        functools.partial(_in_body, seq_mode, tm),
        grid=grid,
        in_specs=[tok(d), mod, mod,
                  _const_spec((d, N_IN_PAD), True), _const_spec((1, N_SHIFT_PAD)), zp_spec,
                  _const_spec((LANES, 2 * D_A)), _const_spec((1, 2 * D_A)),
                  _const_spec((1, D_A)), _const_spec((1, D_A)),
                  _const_spec((1, D_B)), _const_spec((1, D_B)),
                  _const_spec((D_A, D_A))],
        out_specs=[tok(n) for n in out_cols] + [hl_spec],
        out_shape=[jax.ShapeDtypeStruct((b, t, n), dt) for n, dt in zip(out_cols, out_dtypes)]
        + [jax.ShapeDtypeStruct(hl_shape, F32)],
        scratch_shapes=[pltpu.VMEM((1, N_SHIFT_PAD), F32)],
        compiler_params=_params(("arbitrary", "arbitrary"), est),
        name="in_stage_seq" if seq_mode else "in_stage_row",
    )(x, sc, sh, p["w_in"], p["mu"], zprev, p["lora"], p["w0a0"], p["k_k"], p["k_a"],
      p["lnv_g"], p["lnv_b"], p["bd"])
    return outs


_WKV_3PASS = frozenset({"ark", "arkv", "h"})


def _wmm(name, a, b, dims):
    return (_mm3 if name in _WKV_3PASS else _mm1)(a, b, dims)


def _chunk_pairs(x, nsub):
    c = WKV_CHUNK
    w = 2 * HEAD
    return jnp.stack([x[c * i:c * (i + 1), w * q:w * (q + 1)]
                      for i in range(nsub) for q in range(N_HEADS // 2)], axis=0)


def _masked_rowsum(mask_bf, x):
    h1, h2 = _split2(x)
    return _dot(mask_bf, h1) + _dot(mask_bf, h2)


def _masked_rowsum_t(x, mask):
    h1, h2 = _split2(x)
    m = mask.astype(BF16)
    return _dot(h1, m) + _dot(h2, m)


def _wkv_seq_body(nsub, r_ref, lw_ref, kp_ref, v_ref, kap_ref, bb_ref, s0_ref,
                  o_ref, sout_ref, s_scr):
    c = WKV_CHUNK
    tc = nsub * c
    npair = N_HEADS // 2

    @pl.when(pl.program_id(1) == 0)
    def _():
        for q in range(npair):
            s_scr[q] = jnp.concatenate([s0_ref[0, 2 * q], s0_ref[0, 2 * q + 1]], axis=-1)

    row_t = lax.broadcasted_iota(jnp.int32, (tc, tc), 0)
    col_t = lax.broadcasted_iota(jnp.int32, (tc, tc), 1)
    shift = c.bit_length() - 1
    same_chunk = (row_t >> shift) == (col_t >> shift)
    lw = lw_ref[0]
    g = _masked_rowsum((same_chunk & (row_t >= col_t)).astype(BF16), lw)
    g_end = _masked_rowsum(same_chunk.astype(BF16), lw)
    e_neg = jnp.exp(-g)
    e_end = jnp.exp(g_end - g)
    cp = functools.partial(_chunk_pairs, nsub=nsub)
    k = kp_ref[0]
    b = bb_ref[0]
    kap_t = cp(kap_ref[0] * jnp.exp(g - lw))
    b_t = cp(b * e_neg)
    k_t = cp(k * e_neg)
    r_t = cp(r_ref[0] * jnp.exp(g))
    b_e = cp(b * e_end)
    k_e = cp(k * e_end)
    vv = cp(v_ref[0])
    decay_end = cp(jnp.exp(g_end))

    row = lax.broadcasted_iota(jnp.int32, (c, 2 * c), 0)[None]
    lane = lax.broadcasted_iota(jnp.int32, (c, 2 * c), 1)[None]
    colp = lane & (c - 1)
    right = lane >= c
    row2 = lax.broadcasted_iota(jnp.int32, (2 * c, 2 * c), 0)[None]
    lane2 = lax.broadcasted_iota(jnp.int32, (2 * c, 2 * c), 1)[None]
    same_head = (row2 >= c) == (lane2 >= c)

    def bd(x):
        return jnp.concatenate([jnp.where(right, 0.0, x), jnp.where(right, x, 0.0)], axis=1)

    p_b = _wmm("lb", jnp.concatenate([kap_t, r_t], axis=1), bd(b_t), _B_NT)
    l_b = jnp.where(row > colp, p_b[:, :c], 0.0)
    a_rb = jnp.where(row >= colp, p_b[:, c:], 0.0)
    bd_k = bd(k_t)
    l_k = jnp.where(row > colp, _wmm("lk", kap_t, bd_k, _B_NT), 0.0)
    a_rk = jnp.where(row >= colp, _wmm("ark", r_t, bd_k, _B_NT), 0.0)
    m = -l_b
    t_inv = jnp.where(row == colp, 1.0, 0.0) + m
    m = _wmm("inv", m, bd(m), _B_NN)
    span = 2
    while 2 * span < c:
        both = _wmm("inv", jnp.concatenate([m, t_inv], axis=1), bd(m), _B_NN)
        m = both[:, :c]
        t_inv = t_inv + both[:, c:]
        span *= 2
    t_inv = t_inv + _wmm("inv", t_inv, bd(m), _B_NN)
    bd_v = bd(vv)
    a1 = -_wmm("ta", t_inv, bd(kap_t), _B_NN)
    u0 = -_wmm("tu", t_inv, bd(_wmm("lkv", l_k, bd_v, _B_NN)), _B_NN)
    a2 = r_t + _wmm("arba", a_rb, bd(a1), _B_NN)
    o0 = _wmm("arbu", a_rb, bd(u0), _B_NN) + _wmm("arkv", a_rk, bd_v, _B_NN)
    g_bd = (jnp.where(same_head, _wmm("g", a1, b_e, _B_TN), 0.0)
            + jnp.where(row2 == lane2, decay_end[:, 0:1, :], 0.0))
    hh = _wmm("h", jnp.concatenate([u0, vv], axis=1),
              jnp.concatenate([b_e, k_e], axis=1), _B_TN)
    h_pair = jnp.where(right, hh[:, c:], hh[:, :c])

    s = s_scr[...]
    for i in range(nsub):
        ps = slice(npair * i, npair * (i + 1))
        o = _wmm("o", a2[ps], bd(s), _B_NT) + o0[ps]
        s = _wmm("s", s, g_bd[ps], _B_NN) + h_pair[ps]
        for q in range(npair):
            o_ref[0, c * i:c * (i + 1), 2 * HEAD * q:2 * HEAD * (q + 1)] = o[q]
    s_scr[...] = s

    @pl.when(pl.program_id(1) == pl.num_programs(1) - 1)
    def _():
        for q in range(npair):
            sout_ref[0, 2 * q] = s[q][:, :HEAD]
            sout_ref[0, 2 * q + 1] = s[q][:, HEAD:]


def _wkv_seq(r, lw, kp, v, kap, bb, s0):
    b, t, _ = r.shape
    tc = 256
    nsub = tc // WKV_CHUNK
    tok = pl.BlockSpec((1, tc, D_A), lambda i, j: (i, j, 0))
    st = pl.BlockSpec((1, N_HEADS, HEAD, HEAD), lambda i, j: (i, 0, 0, 0))
    return pl.pallas_call(
        functools.partial(_wkv_seq_body, nsub),
        grid=(b, t // tc),
        in_specs=[tok] * 6 + [st],
        out_specs=[tok, st],
        out_shape=[jax.ShapeDtypeStruct((b, t, D_A), F32),
                   jax.ShapeDtypeStruct((b, N_HEADS, HEAD, HEAD), F32)],
        scratch_shapes=[pltpu.VMEM((N_HEADS // 2, HEAD, 2 * HEAD), F32)],
        compiler_params=_params(("arbitrary", "arbitrary"), 32 * 2**20),
        name="wkv_seq",
    )(r, lw, kp, v, kap, bb, s0)


def _wkv_step_body(r_ref, lw_ref, kp_ref, v_ref, kap_ref, bb_ref, s0_ref, _, o_ref, sout_ref):
    eye = (lax.broadcasted_iota(jnp.int32, (HEAD, HEAD), 0)
           == lax.broadcasted_iota(jnp.int32, (HEAD, HEAD), 1))[None]
    for h in range(N_HEADS):
        hs = slice(HEAD * h, HEAD * (h + 1))
        row = lambda ref: ref[:, hs][:, None, :]
        s = s0_ref[:, h]
        sa = -jnp.sum(s * row(kap_ref), axis=-1, keepdims=True)
        v_col = jnp.sum(jnp.where(eye, row(v_ref), 0.0), axis=-1, keepdims=True)
        s_new = s * jnp.exp(row(lw_ref)) + sa * row(bb_ref) + v_col * row(kp_ref)
        sout_ref[:, h] = s_new
        o_col = jnp.sum(s_new * row(r_ref), axis=-1, keepdims=True)
        o_ref[:, hs] = jnp.sum(jnp.where(eye, o_col, 0.0), axis=1)


def _wkv_step(r, lw, kp, v, kap, bb, s0, s_new_all, layer):
    m = r.shape[0]
    nb = 8
    tok = pl.BlockSpec((nb, D_A), lambda i: (i, 0))
    st = pl.BlockSpec((nb, N_HEADS, HEAD, HEAD), lambda i: (i, 0, 0, 0))
    st_out = pl.BlockSpec((None, nb, N_HEADS, HEAD, HEAD), lambda i: (layer, i, 0, 0, 0))
    return pl.pallas_call(
        _wkv_step_body,
        grid=(m // nb,),
        in_specs=[tok] * 6 + [st, pl.BlockSpec(memory_space=pl.ANY)],
        out_specs=[tok, st_out],
        out_shape=[jax.ShapeDtypeStruct((m, D_A), F32),
                   jax.ShapeDtypeStruct(s_new_all.shape, F32)],
        input_output_aliases={7: 1},
        compiler_params=_params(("arbitrary",), 24 * 2**20),
        name="wkv_step",
    )(r, lw, kp, v, kap, bb, s0, s_new_all)


def _merge_body(seq_mode, tm, x_ref, o_ref, r_ref, kp_ref, v_ref, u_ref, vg_ref,
                ga_ref, gb_ref, g1_ref, sc2_ref, sh2_ref, lxg_ref, lxb_ref, rk_ref,
                bd_ref, ws_ref, bs_ref, pa_ref, pb_ref, wo_ref, l1g_ref, l1b_ref,
                x1_ref, h2_ref):
    bd = bd_ref[...]
    o = o_ref[0]
    inv_n = 1.0 / HEAD
    mu = _segsum(o, bd) * inv_n
    d = o - mu
    var = _segsum(d * d, bd) * inv_n
    on = d * lax.rsqrt(var + GN_EPS) * lxg_ref[...] + lxb_ref[...]
    v = v_ref[0]
    o_a = on + _segsum(r_ref[0] * kp_ref[0] * rk_ref[...], bd) * v
    vg = vg_ref[0]
    if seq_mode:
        row = lax.broadcasted_iota(jnp.int32, (CHUNK, CHUNK), 0)
        col = lax.broadcasted_iota(jnp.int32, (CHUNK, CHUNK), 1)
        lane = lax.broadcasted_iota(jnp.int32, (CHUNK, D_B), 1)
        gsz = D_B // N_GROUPS_B
        w_cat = jnp.concatenate(
            [jnp.where(row >= col, ws_ref[g], 0.0).astype(BF16) for g in range(N_GROUPS_B)], axis=1)
        pieces = []
        for j in range(tm // CHUNK):
            vc = vg[j * CHUNK:(j + 1) * CHUNK, :].astype(BF16)
            v_bd = jnp.concatenate(
                [jnp.where((lane >= g * gsz) & (lane < (g + 1) * gsz), vc, 0.0)
                 for g in range(N_GROUPS_B)], axis=0)
            pieces.append(_dot(w_cat, v_bd) + bs_ref[...])
        s = jnp.concatenate(pieces, axis=0) if len(pieces) > 1 else pieces[0]
    else:
        s = vg * ws_ref[...] + bs_ref[...]
    o_b = u_ref[0] * s
    y = _mm1(ga_ref[0] * _mm1(o_a, pa_ref[...]) + gb_ref[0] * _mm1(o_b, pb_ref[...]),
             wo_ref[...])
    x1 = _layer_norm(ALPHA * x_ref[0] + g1_ref[0] * y, l1g_ref[...], l1b_ref[...])
    x1_ref[0] = x1
    h2_ref[0] = x1 * (1.0 + sc2_ref[0]) + sh2_ref[0]


def _merge_stage(seq_mode, x, o, r, kp, v, u, vg, ga, gb, g1, sc2, sh2, p):
    b, t, d = x.shape
    tm = 256 if seq_mode else t
    tmod = 1 if seq_mode else tm
    tok = lambda n: pl.BlockSpec((1, tm, n), lambda i, j: (i, j, 0))
    mod = pl.BlockSpec((1, tmod, d), lambda i, j: (i, j if not seq_mode else 0, 0))
    ws, bs = (p["w_spatial"], p["b_spatial_full"]) if seq_mode else (p["ws_row"], p["bs_row"])
    est = 2 * tm * (2 * d + 7 * D_A + 2 * d + 2 * d) * 4 + 8 * tm * d * 4 + 16 * 2**20
    return pl.pallas_call(
        functools.partial(_merge_body, seq_mode, tm),
        grid=(b, t // tm),
        in_specs=[tok(d)] + [tok(D_A)] * 6 + [tok(d), tok(d), mod, mod, mod,
                  _const_spec((1, D_A)), _const_spec((1, D_A)), _const_spec((1, D_A)),
                  _const_spec((D_A, D_A)), _const_spec(ws.shape), _const_spec(bs.shape),
                  _const_spec((D_A, d)), _const_spec((D_B, d)), _const_spec((d, d)),
                  _const_spec((1, d)), _const_spec((1, d))],
        out_specs=[tok(d), tok(d)],
        out_shape=[jax.ShapeDtypeStruct((b, t, d), F32)] * 2,
        compiler_params=_params(("arbitrary", "arbitrary"), est),
        name="merge_seq" if seq_mode else "merge_row",
    )(x, o, r, kp, v, u, vg, ga, gb, g1, sc2, sh2, p["lnx_g"], p["lnx_b"], p["r_k"],
      p["bd"], ws, bs, p["w_branch_a"], p["w_branch_b"], p["w_out"], p["ln1_g"], p["ln1_b"])


def _route_body(h_ref, w_ref, b_ref, gate_ref, idx_ref, wts_ref, cnt_ref):
    lg = _mm3(h_ref[...], w_ref[...]) + b_ref[...]
    lane = lax.broadcasted_iota(jnp.int32, lg.shape, 1)
    lanef = lane.astype(F32)
    neg = -jnp.inf
    is_g = (lane >= N_EXPERTS) & (lane < N_EXPERTS + N_ROUTE_GROUPS)
    mg = jnp.max(jnp.where(is_g, lg, neg), axis=-1, keepdims=True)
    gidx = jnp.min(jnp.where(is_g & (lg == mg), lanef - N_EXPERTS, 1e9), axis=-1, keepdims=True)
    pg_sel = 1.0 / jnp.sum(jnp.where(is_g, jnp.exp(lg - mg), 0.0), axis=-1, keepdims=True)
    lo = gidx * EXP_PER_GROUP
    in_grp = (lanef >= lo) & (lanef < lo + EXP_PER_GROUP)
    t1 = jnp.max(jnp.where(in_grp, lg, neg), axis=-1, keepdims=True)
    i1 = jnp.min(jnp.where(in_grp & (lg == t1), lanef, 1e9), axis=-1, keepdims=True)
    rest = in_grp & (lanef != i1)
    t2 = jnp.max(jnp.where(rest, lg, neg), axis=-1, keepdims=True)
    i2 = jnp.min(jnp.where(rest & (lg == t2), lanef, 1e9), axis=-1, keepdims=True)
    e2 = jnp.exp(t2 - t1)
    w1 = pg_sel / (1.0 + e2)
    w2 = pg_sel * e2 / (1.0 + e2)
    gate_ref[...] = jnp.where(lanef == i1, w1, 0.0) + jnp.where(lanef == i2, w2, 0.0)
    tm = lg.shape[0]
    hit = (lanef == i1) | (lanef == i2)
    earlier = (lax.broadcasted_iota(jnp.int32, (tm, tm), 0)
               > lax.broadcasted_iota(jnp.int32, (tm, tm), 1))
    rank = _dot(earlier.astype(BF16), hit.astype(BF16))
    cnt = jnp.sum(hit.astype(F32), axis=0, keepdims=True)
    cnt8 = jnp.floor((cnt + 7.0) * 0.125) * 8.0
    lower_expert = (lax.broadcasted_iota(jnp.int32, (LANES, LANES), 0)
                    < lax.broadcasted_iota(jnp.int32, (LANES, LANES), 1))
    run_start = _masked_rowsum_t(jnp.broadcast_to(cnt8, (8, LANES)), lower_expert)[0:1]
    lpos = run_start + rank
    lp1 = jnp.sum(jnp.where(lanef == i1, lpos, 0.0), axis=-1, keepdims=True)
    lp2 = jnp.sum(jnp.where(lanef == i2, lpos, 0.0), axis=-1, keepdims=True)
    cnt_ref[0] = cnt.astype(jnp.int32)
    idx = jnp.where(lane == 0, i1, jnp.where(lane == 1, i2, jnp.where(lane == 2, lp1, lp2)))
    idx_ref[...] = idx.astype(jnp.int32)
    wts_ref[...] = jnp.where(lane == 0, w1, w2)


def _route(h2, p):
    m, d = h2.shape
    tm = min(m, MOE_TILE)
    tok = pl.BlockSpec((tm, LANES), lambda i: (i, 0))
    return pl.pallas_call(
        _route_body,
        grid=(m // tm,),
        in_specs=[pl.BlockSpec((tm, d), lambda i: (i, 0)),
                  _const_spec((d, LANES)), _const_spec((1, LANES))],
        out_specs=[tok, tok, tok, pl.BlockSpec((1, 1, LANES), lambda i: (i, 0, 0))],
        out_shape=[jax.ShapeDtypeStruct((m, LANES), F32),
                   jax.ShapeDtypeStruct((m, LANES), jnp.int32),
                   jax.ShapeDtypeStruct((m, LANES), F32),
                   jax.ShapeDtypeStruct((m // tm, 1, LANES), jnp.int32)],
        compiler_params=_params(("arbitrary",), 24 * 2**20),
        name="route",
    )(h2, p["w_route"], p["b_route"])


def _moe_body(h_ref, gate_ref, wg_ref, wu_ref, wd_ref, x1_ref, g2_ref, l2g_ref, l2b_ref,
              out_ref, acc_ref, xb_ref):
    e = pl.program_id(1)

    @pl.when(e == 0)
    def _():
        acc_ref[...] = jnp.zeros_like(acc_ref)
        xb_ref[...] = h_ref[...].astype(BF16)

    xb = xb_ref[...]
    pre = _dot(xb, wg_ref[0].astype(BF16))
    hid = pre * jax.nn.sigmoid(pre) * _dot(xb, wu_ref[0].astype(BF16))
    ye = _mm1(hid, wd_ref[0])
    gate = gate_ref[...]
    lane = lax.broadcasted_iota(jnp.int32, gate.shape, 1)
    ge = jnp.sum(jnp.where(lane == e, gate, 0.0), axis=-1, keepdims=True)
    acc_ref[...] += ge * ye

    @pl.when(e == pl.num_programs(1) - 1)
    def _():
        out_ref[...] = _layer_norm(ALPHA * x1_ref[...] + g2_ref[0] * acc_ref[...],
                                   l2g_ref[...], l2b_ref[...])


def _moe(seq_len, h2, gate, x1, g2, p):
    m, d = h2.shape
    layer = p["layer"]
    tm = min(seq_len if g2.shape[1] == 1 else m, 1024)
    tok = lambda n: pl.BlockSpec((tm, n), lambda i, e: (i, 0))
    if g2.shape[1] == 1:
        tiles_per_seq = seq_len // tm
        g2_spec = pl.BlockSpec((1, 1, d), lambda i, e: (i // tiles_per_seq, 0, 0))
    else:
        g2_spec = pl.BlockSpec((1, tm, d), lambda i, e: (0, i, 0))
    est = 2 * tm * (3 * d + LANES) * 4 + tm * d * 6 + 4 * tm * d * 4 + 8 * 2**20
    return pl.pallas_call(
        _moe_body,
        grid=(m // tm, N_EXPERTS),
        in_specs=[tok(d), tok(LANES),
                  pl.BlockSpec((None, 1, d, D_EXPERT), lambda i, e: (layer, e, 0, 0)),
                  pl.BlockSpec((None, 1, d, D_EXPERT), lambda i, e: (layer, e, 0, 0)),
                  pl.BlockSpec((None, 1, D_EXPERT, d), lambda i, e: (layer, e, 0, 0)),
                  tok(d), g2_spec, _const_spec((1, d)), _const_spec((1, d))],
        out_specs=tok(d),
        out_shape=jax.ShapeDtypeStruct((m, d), F32),
        scratch_shapes=[pltpu.VMEM((tm, d), F32), pltpu.VMEM((tm, d), BF16)],
        compiler_params=_params(("arbitrary", "arbitrary"), est),
        name="moe_dense",
    )(h2, gate, p["w_exp_gate"], p["w_exp_up"], p["w_exp_down"], x1, g2, p["ln2_g"], p["ln2_b"])


MOE_TILE = 512
MOE_BLOCK = 512
MOE_RUN_ALIGN = 8
MOE_LOCAL_ROWS = 2 * MOE_TILE + 256
MOE_SLABS = tuple(2 ** k for k in range(9, 2, -1))


def _for_each_slab(run_ref, make_copy, fn):
    def one_run(e, c):
        dst = run_ref[0, 0, e]
        src = run_ref[0, 0, N_EXPERTS + e]
        n = run_ref[0, 0, 2 * N_EXPERTS + e]
        for slab in MOE_SLABS:
            off = n & (-2 * slab)

            @pl.when((n & slab) != 0)
            def _():
                fn(make_copy(pl.multiple_of(dst + off, MOE_RUN_ALIGN),
                             pl.multiple_of(src + off, MOE_RUN_ALIGN), slab))
        return c

    lax.fori_loop(0, N_EXPERTS, one_run, 0)


def _dispatch_body(tail_ref, run_ref, prev_run_ref, h_ref, idx_ref, xs_hbm, zero_buf, loc_buf,
                   sem, zsem):
    nb_max = xs_hbm.shape[0] // MOE_BLOCK
    nt = MOE_TILE

    def zero_copy(row0):
        return pltpu.make_async_copy(
            zero_buf, xs_hbm.at[pl.ds(pl.multiple_of(row0, MOE_BLOCK), MOE_BLOCK)], zsem)

    @pl.when(pl.program_id(0) == 0)
    def _():
        zero_buf[...] = jnp.zeros_like(zero_buf)
        n_used = tail_ref[N_EXPERTS]

        def each_zero_copy(fn):
            for e in range(N_EXPERTS):
                @pl.when(tail_ref[e] >= 0)
                def _():
                    fn(zero_copy(tail_ref[e]))

            def unused(j, c):
                fn(zero_copy(j * MOE_BLOCK))
                return c

            lax.fori_loop(n_used, nb_max, unused, 0)

        each_zero_copy(lambda cp: cp.start())
        each_zero_copy(lambda cp: cp.wait())

    lp = idx_ref[...].astype(F32)
    eye = (lax.broadcasted_iota(jnp.int32, (nt, nt), 0)
           == lax.broadcasted_iota(jnp.int32, (nt, nt), 1))
    as_row = lambda col: jnp.sum(jnp.where(eye, col, 0.0), axis=0, keepdims=True)
    r = lax.broadcasted_iota(jnp.int32, (MOE_LOCAL_ROWS, nt), 0).astype(F32)
    pick = (r == as_row(lp[:, 2:3])) | (r == as_row(lp[:, 3:4]))
    step = pl.program_id(0)
    slot = step & 1
    loc_buf[slot] = _dot(pick.astype(BF16), h_ref[...].astype(BF16))

    def copies_from(which):
        def make_copy(dst, src, rows):
            return pltpu.make_async_copy(loc_buf.at[which, pl.ds(src, rows)],
                                         xs_hbm.at[pl.ds(dst, rows)], sem.at[which])
        return make_copy

    _for_each_slab(run_ref, copies_from(slot), lambda cp: cp.start())

    @pl.when(step > 0)
    def _():
        _for_each_slab(prev_run_ref, copies_from(1 - slot), lambda cp: cp.wait())

    @pl.when(step == pl.num_programs(0) - 1)
    def _():
        _for_each_slab(run_ref, copies_from(slot), lambda cp: cp.wait())


def _dispatch(h2, idx, runs, tail_start, n_rows):
    m, d = h2.shape
    nt = MOE_TILE
    run_spec = lambda at: pl.BlockSpec((1, 1, LANES), lambda i, tail: (at(i), 0, 0),
                                       memory_space=pltpu.SMEM)
    grid_spec = pltpu.PrefetchScalarGridSpec(
        num_scalar_prefetch=1,
        grid=(m // nt,),
        in_specs=[run_spec(lambda i: i), run_spec(lambda i: jnp.maximum(i - 1, 0)),
                  pl.BlockSpec((nt, d), lambda i, tail: (i, 0)),
                  pl.BlockSpec((nt, LANES), lambda i, tail: (i, 0))],
        out_specs=pl.BlockSpec(memory_space=pl.ANY),
        scratch_shapes=[pltpu.VMEM((MOE_BLOCK, d), F32),
                        pltpu.VMEM((2, MOE_LOCAL_ROWS, d), F32),
                        pltpu.SemaphoreType.DMA((2,)), pltpu.SemaphoreType.DMA],
    )
    return pl.pallas_call(
        _dispatch_body,
        grid_spec=grid_spec,
        out_shape=jax.ShapeDtypeStruct((n_rows, d), F32),
        compiler_params=_params(("arbitrary",), 40 * 2**20),
        name="moe_dispatch",
    )(tail_start, runs, runs, h2, idx)


def _gmm_body(be_ref, nb_ref, x_ref, wg_ref, wu_ref, wd_ref, y_ref):
    j = pl.program_id(0)

    @pl.when(j < nb_ref[0])
    def _():
        xb = x_ref[...].astype(BF16)
        pre = _dot(xb, wg_ref[0].astype(BF16))
        hid = pre * jax.nn.sigmoid(pre) * _dot(xb, wu_ref[0].astype(BF16))
        y_ref[...] = _mm1(hid, wd_ref[0])

    @pl.when(j >= nb_ref[0])
    def _():
        y_ref[...] = jnp.zeros_like(y_ref)


def _gmm(xs, blk_expert, n_blocks, p):
    n_rows, d = xs.shape
    nb_max = n_rows // MOE_BLOCK
    layer = p["layer"]
    live = lambda j, be, nb: jnp.minimum(j, nb[0] - 1)
    wspec = lambda shape: pl.BlockSpec((None,) + shape,
                                       lambda j, be, nb: (layer, be[live(j, be, nb)], 0, 0))
    grid_spec = pltpu.PrefetchScalarGridSpec(
        num_scalar_prefetch=2,
        grid=(nb_max,),
        in_specs=[pl.BlockSpec((MOE_BLOCK, d), lambda j, be, nb: (live(j, be, nb), 0)),
                  wspec((1, d, D_EXPERT)), wspec((1, d, D_EXPERT)), wspec((1, D_EXPERT, d))],
        out_specs=pl.BlockSpec((MOE_BLOCK, d), lambda j, be, nb: (j, 0)),
    )
    return pl.pallas_call(
        _gmm_body,
        grid_spec=grid_spec,
        out_shape=jax.ShapeDtypeStruct((n_rows, d), F32),
        compiler_params=_params(("arbitrary",), 24 * 2**20),
        name="moe_gmm",
    )(blk_expert, n_blocks, xs, p["w_exp_gate"], p["w_exp_up"], p["w_exp_down"])


def _combine_body(run_ref, next_run_ref, ys_hbm, idx_ref, wts_ref, x1_ref, g2_ref, l2g_ref,
                  l2b_ref, out_ref, loc_buf, sem):
    nt = MOE_TILE
    step = pl.program_id(0)
    slot = step & 1

    def copies_into(which):
        def make_copy(dst, src, rows):
            return pltpu.make_async_copy(ys_hbm.at[pl.ds(dst, rows)],
                                         loc_buf.at[which, pl.ds(src, rows)], sem.at[which])
        return make_copy

    @pl.when(step == 0)
    def _():
        loc_buf[...] = jnp.zeros_like(loc_buf)
        _for_each_slab(run_ref, copies_into(slot), lambda cp: cp.start())

    @pl.when(step < pl.num_programs(0) - 1)
    def _():
        _for_each_slab(next_run_ref, copies_into(1 - slot), lambda cp: cp.start())

    _for_each_slab(run_ref, copies_into(slot), lambda cp: cp.wait())
    lp = idx_ref[...].astype(F32)
    w = wts_ref[...]
    c = lax.broadcasted_iota(jnp.int32, (nt, MOE_LOCAL_ROWS), 1).astype(F32)
    sel = (jnp.where(c == lp[:, 2:3], w[:, 0:1], 0.0)
           + jnp.where(c == lp[:, 3:4], w[:, 1:2], 0.0))
    sel_hi, sel_lo = _split2(sel)
    rows_bf = loc_buf[slot].astype(BF16)
    moe = _dot(sel_hi, rows_bf) + _dot(sel_lo, rows_bf)
    out_ref[...] = _layer_norm(ALPHA * x1_ref[...] + g2_ref[0] * moe, l2g_ref[...], l2b_ref[...])


def _combine(seq_len, ys, idx, wts, runs, x1, g2, p):
    m, d = x1.shape
    nt = MOE_TILE
    tiles_per_seq = seq_len // nt
    tok = lambda n: pl.BlockSpec((nt, n), lambda i: (i, 0))
    n_tiles = m // nt
    run_spec = lambda at: pl.BlockSpec((1, 1, LANES), lambda i: (at(i), 0, 0),
                                       memory_space=pltpu.SMEM)
    return pl.pallas_call(
        _combine_body,
        grid=(n_tiles,),
        in_specs=[run_spec(lambda i: i), run_spec(lambda i: jnp.minimum(i + 1, n_tiles - 1)),
                  pl.BlockSpec(memory_space=pl.ANY), tok(LANES), tok(LANES), tok(d),
                  pl.BlockSpec((1, 1, d), lambda i: (i // tiles_per_seq, 0, 0)),
                  _const_spec((1, d)), _const_spec((1, d))],
        out_specs=tok(d),
        out_shape=jax.ShapeDtypeStruct((m, d), F32),
        scratch_shapes=[pltpu.VMEM((2, MOE_LOCAL_ROWS, d), F32), pltpu.SemaphoreType.DMA((2,))],
        compiler_params=_params(("arbitrary",), 48 * 2**20),
        name="moe_combine",
    )(runs, runs, ys, idx, wts, x1, g2, p["ln2_g"], p["ln2_b"])


def _moe_routed(seq_len, h2, idx, wts, tile_cnt, x1, g2, p):
    m, d = h2.shape
    blk = MOE_BLOCK
    n_tiles = m // MOE_TILE
    worst_rows = 2 * m + n_tiles * N_EXPERTS * (MOE_RUN_ALIGN - 1) + N_EXPERTS * (blk - 1)
    nb_max = -(-worst_rows // blk)
    cnt = tile_cnt[:, 0, :N_EXPERTS]
    run_len = ((cnt + MOE_RUN_ALIGN - 1) // MOE_RUN_ALIGN) * MOE_RUN_ALIGN
    local_row = jnp.cumsum(run_len, axis=1) - run_len
    rows_before = jnp.cumsum(run_len, axis=0) - run_len
    total = jnp.sum(run_len, axis=0)
    padded = ((total + blk - 1) // blk) * blk
    ends = jnp.cumsum(padded)
    starts = ends - padded
    n_blocks = (ends[-1] // blk).astype(jnp.int32).reshape(1)
    first_row = jnp.arange(nb_max, dtype=jnp.int32) * blk
    blk_expert = jnp.minimum(
        jnp.sum((first_row[:, None] >= ends[None, :]).astype(jnp.int32), axis=1),
        N_EXPERTS - 1).astype(jnp.int32)
    tail_start = jnp.concatenate(
        [jnp.where(padded > 0, ends - blk, -1).astype(jnp.int32), n_blocks])
    runs = jnp.concatenate(
        [starts[None, :] + rows_before, local_row, run_len,
         jnp.zeros((n_tiles, LANES - 3 * N_EXPERTS), jnp.int32)], axis=1).astype(jnp.int32)
    runs = runs.reshape(n_tiles, 1, LANES)
    xs = _dispatch(h2, idx, runs, tail_start, nb_max * blk)
    ys = _gmm(xs, blk_expert, n_blocks, p)
    return _combine(seq_len, ys, idx, wts, runs, x1, g2, p)


def _prep_layer(l, w_in, mu_shift, w0, w_decay_up, a0, w_iclr_up, k_k, k_a, r_k,
                lnx_g, lnx_b, lnv_g, lnv_b, w_spatial, b_spatial, w_branch_a, w_branch_b,
                w_out, ln1_g, ln1_b, w_route_group, b_route_group, w_route_expert,
                b_route_expert, w_exp_gate, w_exp_up, w_exp_down, ln2_g, ln2_b):
    d = D_MODEL
    pad_a = N_SHIFT_PAD - N_SHIFT
    wi = w_in[l]
    w_in_p = jnp.concatenate(
        [wi[:, :N_SHIFT], jnp.zeros((d, pad_a), F32), wi[:, N_SHIFT:]], axis=1).astype(BF16)
    mu = jnp.concatenate([mu_shift[l], jnp.zeros((pad_a,), F32)])[None]
    lora = jnp.zeros((LANES, 2 * D_A), F32)
    lora = lora.at[:R_LORA, :D_A].set(w_decay_up[l]).at[R_LORA:2 * R_LORA, D_A:].set(w_iclr_up[l])
    seg = jnp.arange(D_A) // HEAD
    row1 = lambda x: x.reshape(1, -1)
    gsz = D_B // N_GROUPS_B
    w_route = jnp.concatenate(
        [w_route_expert[l], w_route_group[l],
         jnp.zeros((d, LANES - N_EXPERTS - N_ROUTE_GROUPS), F32)], axis=1)
    b_route = jnp.concatenate(
        [b_route_expert[l], b_route_group[l],
         jnp.zeros((LANES - N_EXPERTS - N_ROUTE_GROUPS,), F32)])[None]
    return dict(
        w_in=w_in_p, mu=mu, lora=lora,
        w0a0=jnp.concatenate([w0[l], a0[l]])[None],
        k_k=row1(k_k[l]), k_a=row1(k_a[l]), r_k=row1(r_k[l]),
        lnx_g=row1(lnx_g[l]), lnx_b=row1(lnx_b[l]),
        lnv_g=row1(lnv_g[l]), lnv_b=row1(lnv_b[l]),
        bd=(seg[:, None] == seg[None, :]).astype(BF16),
        w_spatial=w_spatial[l],
        b_spatial_full=jnp.repeat(b_spatial[l].T, gsz, axis=1),
        ws_row=jnp.repeat(w_spatial[l][:, 0, 0], gsz)[None],
        bs_row=jnp.repeat(b_spatial[l][:, 0], gsz)[None],
        w_branch_a=w_branch_a[l].astype(BF16), w_branch_b=w_branch_b[l].astype(BF16),
        w_out=w_out[l].astype(BF16), ln1_g=row1(ln1_g[l]), ln1_b=row1(ln1_b[l]),
        w_route=w_route, b_route=b_route,
        w_exp_gate=w_exp_gate, w_exp_up=w_exp_up, w_exp_down=w_exp_down, layer=l,
        ln2_g=row1(ln2_g[l]), ln2_b=row1(ln2_b[l]),
    )


def _trunk(seq_mode, x, mods, wkv_in, shift_in, preps):
    b, t, d = x.shape
    wkv_out, shift_out, v_out = [], [], []
    wkv_acc = None if seq_mode else jnp.zeros(wkv_in.shape, F32)
    for l in range(DEPTH):
        p = preps[l]
        sh1, sc1, g1, sh2, sc2, g2 = mods[l]
        if seq_mode:
            zprev = jnp.zeros((b, 1, N_SHIFT_PAD), F32) if shift_in is None else shift_in[l]
        else:
            zprev = _matmul(shift_in[l], p["w_in"][:, :N_SHIFT_PAD]).reshape(b, t, N_SHIFT_PAD)
        r, lw, kp, v, kap, bb, u, vg, ga, gb, hl = _in_stage(seq_mode, x, sc1, sh1, zprev, p)
        if seq_mode:
            o, s_new = _wkv_seq(r, lw, kp, v, kap, bb, wkv_in[l])
        else:
            flat = lambda a: a.reshape(t, D_A)
            o, wkv_acc = _wkv_step(flat(r), flat(lw), flat(kp), flat(v), flat(kap), flat(bb),
                                   wkv_in[l], wkv_acc, l)
            s_new = None
            o = o.reshape(b, t, D_A)
        x1, h2 = _merge_stage(seq_mode, x, o, r, kp, v, u, vg, ga, gb, g1, sc2, sh2, p)
        m = b * t
        h2f = h2.reshape(m, d)
        gate, idx, wts, tile_cnt = _route(h2f, p)
        if seq_mode:
            x = _moe_routed(t, h2f, idx, wts, tile_cnt, x1.reshape(m, d), g2, p)
        else:
            x = _moe(t, h2f, gate, x1.reshape(m, d), g2, p)
        x = x.reshape(b, t, d)
        wkv_out.append(s_new)
        shift_out.append(hl)
        v_out.append(vg)
    if not seq_mode:
        wkv_out = wkv_acc
    return x, wkv_out, shift_out, v_out


def kernel(x_prompt, x_sample, c_prompt, c_sample, state_wkv, state_shift, w_ada, b_ada, w_in, mu_shift, w0, w_decay_up, a0, w_iclr_up, k_k, k_a, r_k, lnx_g, lnx_b, lnv_g, lnv_b, w_spatial, b_spatial, w_branch_a, w_branch_b, w_out, ln1_g, ln1_b, w_route_group, b_route_group, w_route_expert, b_route_expert, w_exp_gate, w_exp_up, w_exp_down, ln2_g, ln2_b):
    bp, tp, d = x_prompt.shape
    bs = x_sample.shape[0]
    layer_params = (w_in, mu_shift, w0, w_decay_up, a0, w_iclr_up, k_k, k_a, r_k, lnx_g,
                    lnx_b, lnv_g, lnv_b, w_spatial, b_spatial, w_branch_a, w_branch_b, w_out,
                    ln1_g, ln1_b, w_route_group, b_route_group, w_route_expert,
                    b_route_expert, w_exp_gate, w_exp_up, w_exp_down, ln2_g, ln2_b)
    preps = [_prep_layer(l, *layer_params) for l in range(DEPTH)]
    mod_all = _ada(jnp.concatenate([c_prompt, c_sample], axis=0), w_ada, b_ada)
    mods_p, mods_s = [], []
    for l in range(DEPTH):
        parts = jnp.split(mod_all[l], 6, axis=-1)
        mods_p.append([q[:bp].reshape(bp, 1, d) for q in parts])
        mods_s.append([q[bp:].reshape(1, bs, d) for q in parts])

    wkv0 = jnp.zeros((DEPTH, bp, N_HEADS, HEAD, HEAD), F32)
    y_p, wkv_p, shift_p, _ = _trunk(True, x_prompt, mods_p, wkv0, None, preps)
    y_s, wkv_s, shift_s, v_s = _trunk(False, x_sample.reshape(1, bs, d), mods_s, state_wkv,
                                      state_shift, preps)
    return (y_p,
            y_s.reshape(bs, 1, d),
            jnp.stack(wkv_p),
            jnp.stack([s.reshape(bp, d) for s in shift_p]),
            wkv_s,
            jnp.stack([s.reshape(bs, d) for s in shift_s]),
            jnp.stack([q.reshape(bs, 1, D_B) for q in v_s]))
```

```python
import functools

import jax
import jax.numpy as jnp
from jax import lax
from jax.experimental import pallas as pl
from jax.experimental.pallas import tpu as pltpu

F32 = jnp.float32
BF16 = jnp.bfloat16

D_MODEL = 1024
DEPTH = 2
HEAD = 64
N_HEADS = 8
D_A = N_HEADS * HEAD
R_LORA = 32
CHUNK = 128
N_GROUPS_B = 8
D_B = 512
N_SHIFT = 3 * D_A + 2 * R_LORA
N_ROUTE_GROUPS = 4
EXP_PER_GROUP = 8
N_EXPERTS = N_ROUTE_GROUPS * EXP_PER_GROUP
D_EXPERT = 256
ALPHA = (2 * DEPTH) ** 0.25
LN_EPS = 1e-5
GN_EPS = 64e-5

LANES = 128
N_SHIFT_PAD = 13 * LANES
COL_U = N_SHIFT_PAD
COL_VG = COL_U + D_B
COL_GA = COL_VG + D_B
COL_GB = COL_GA + D_MODEL
N_IN_PAD = COL_GB + D_MODEL
WKV_CHUNK = 64
IN_ROW_GROUPS = 2
VMEM_CAP_BYTES = 60000 * 1024

_NN = (((1,), (0,)), ((), ()))
_B_NT = (((2,), (2,)), ((0,), (0,)))
_B_NN = (((2,), (1,)), ((0,), (0,)))
_B_TN = (((1,), (1,)), ((0,), (0,)))


def _dot(a, b, dims=_NN):
    return lax.dot_general(a, b, dims, preferred_element_type=F32)


def _split2(x):
    hi = x.astype(BF16)
    lo = (x - hi.astype(F32)).astype(BF16)
    return hi, lo


def _mm1(a, b, dims=_NN):
    return _dot(a.astype(BF16), b.astype(BF16), dims)


def _mm3(a, b, dims=_NN):
    ah, al = _split2(a)
    bh, bl = _split2(b)
    return _dot(ah, bh, dims) + (_dot(ah, bl, dims) + _dot(al, bh, dims))


def _segsum(x, bd):
    hi, lo = _split2(x)
    return _dot(hi, bd) + _dot(lo, bd)


def _layer_norm(x, g, b):
    mu = jnp.mean(x, axis=-1, keepdims=True)
    d = x - mu
    var = jnp.mean(d * d, axis=-1, keepdims=True)
    return d * lax.rsqrt(var + LN_EPS) * g + b


def _gelu(x):
    return 0.5 * x * (1.0 + lax.erf(x * 0.7071067811865476))


def _params(sem, est_bytes):
    limit = int(min(VMEM_CAP_BYTES, max(est_bytes, 16 * 1024 * 1024)))
    return pltpu.CompilerParams(dimension_semantics=sem, vmem_limit_bytes=limit)


def _const_spec(shape, single_buffer=False):
    nd = len(shape)
    if single_buffer:
        return pl.BlockSpec(shape, lambda *_: (0,) * nd, pipeline_mode=pl.Buffered(1))
    return pl.BlockSpec(shape, lambda *_: (0,) * nd)


def _ada_body(c_ref, w_ref, b_ref, o_ref):
    c = c_ref[...]
    s = c * jax.nn.sigmoid(c)
    o_ref[0] = _mm3(s, w_ref[0]) + b_ref[0]


def _ada(c_all, w_ada, b_ada):
    depth, d, n6 = w_ada.shape
    m = c_all.shape[0]
    tn = 512
    return pl.pallas_call(
        _ada_body,
        grid=(depth, n6 // tn),
        in_specs=[
            pl.BlockSpec((m, d), lambda l, j: (0, 0)),
            pl.BlockSpec((1, d, tn), lambda l, j: (l, 0, j)),
            pl.BlockSpec((1, 1, tn), lambda l, j: (l, 0, j)),
        ],
        out_specs=pl.BlockSpec((1, m, tn), lambda l, j: (l, 0, j)),
        out_shape=jax.ShapeDtypeStruct((depth, m, n6), F32),
        compiler_params=_params(("arbitrary", "arbitrary"), 24 * 2**20),
        name="ada_mod",
    )(c_all, w_ada, b_ada.reshape(depth, 1, n6))


def _mm_body(x_ref, w_ref, o_ref):
    o_ref[...] = _mm1(x_ref[...], w_ref[...])


def _matmul(x, w):
    m, k = x.shape
    n = w.shape[1]
    return pl.pallas_call(
        _mm_body,
        grid=(1,),
        in_specs=[_const_spec((m, k)), _const_spec((k, n))],
        out_specs=_const_spec((m, n)),
        out_shape=jax.ShapeDtypeStruct((m, n), F32),
        compiler_params=_params(("arbitrary",), 24 * 2**20),
        name="shift_proj",
    )(x, w)


def _in_body(seq_mode, tm, x_ref, sc_ref, sh_ref, w_ref, mu_ref, zp_ref, lora_ref,
             w0a0_ref, kk_ref, ka_ref, lng_ref, lnb_ref, bd_ref,
             r_ref, lw_ref, kp_ref, v_ref, kap_ref, bb_ref, u_ref, vg_ref,
             ga_ref, gb_ref, hl_ref, carry_ref):
    nsplit = IN_ROW_GROUPS if seq_mode else 1
    rows = tm // nsplit
    if seq_mode:
        @pl.when(pl.program_id(1) == 0)
        def _():
            carry_ref[...] = zp_ref[0]

        carry = carry_ref[...]
    bd = bd_ref[...]
    for part in range(nsplit):
        sl = slice(part * rows, (part + 1) * rows)
        h = x_ref[0, sl, :] * (1.0 + sc_ref[0]) + sh_ref[0]
        hb = h.astype(BF16)
        proj = lambda lo, hi: _dot(hb, w_ref[:, lo:hi])
        za = proj(0, N_SHIFT_PAD)
        if seq_mode:
            row = lax.broadcasted_iota(jnp.int32, za.shape, 0)
            prev = jnp.where(row == 0, carry, pltpu.roll(za, 1, 0))
            carry = za[rows - 1:rows, :]
        else:
            prev = zp_ref[0]
        mix = za + mu_ref[...] * (prev - za)
        r = mix[:, 0:D_A]
        k = mix[:, D_A:2 * D_A]
        v = mix[:, 2 * D_A:3 * D_A]
        xwa = mix[:, 3 * D_A:N_SHIFT_PAD]
        lane = lax.broadcasted_iota(jnp.int32, xwa.shape, 1)
        lora_in = jnp.where(lane < R_LORA, jnp.tanh(xwa), xwa)
        pre = w0a0_ref[...] + _mm3(lora_in, lora_ref[...])
        yw = -pre[:, :D_A]
        softplus = jnp.maximum(yw, 0.0) + jnp.log1p(jnp.exp(-jnp.abs(yw)))
        lw = -jnp.exp(-softplus - 0.5)
        a = jax.nn.sigmoid(pre[:, D_A:])
        kk = k * kk_ref[...]
        kap = kk / jnp.maximum(jnp.sqrt(_segsum(kk * kk, bd)), 1e-12)
        r_ref[0, sl, :] = r
        lw_ref[0, sl, :] = lw
        kp_ref[0, sl, :] = k * (1.0 + (a - 1.0) * ka_ref[...])
        v_ref[0, sl, :] = v
        kap_ref[0, sl, :] = kap
        bb_ref[0, sl, :] = kap * a
        u_ref[0, sl, :] = _gelu(proj(COL_U, COL_VG)).astype(BF16)
        vg_ref[0, sl, :] = _layer_norm(_gelu(proj(COL_VG, COL_GA)), lng_ref[...], lnb_ref[...])
        ga_ref[0, sl, :] = jax.nn.sigmoid(proj(COL_GA, COL_GB)).astype(BF16)
        gb_ref[0, sl, :] = jax.nn.sigmoid(proj(COL_GB, N_IN_PAD)).astype(BF16)
    if seq_mode:
        carry_ref[...] = carry
        hl_ref[0] = h[rows - 1:rows, :]
    else:
        hl_ref[0] = h


def _in_stage(seq_mode, x, sc, sh, zprev, p):
    b, t, d = x.shape
    tm = 256 if seq_mode else t
    tmod = 1 if seq_mode else tm
    grid = (b, t // tm)
    tok = lambda n: pl.BlockSpec((1, tm, n), lambda i, j: (i, j, 0))
    mod = pl.BlockSpec((1, tmod, d), lambda i, j: (i, j if not seq_mode else 0, 0))
    zp_spec = (pl.BlockSpec((1, 1, N_SHIFT_PAD), lambda i, j: (i, 0, 0)) if seq_mode
               else tok(N_SHIFT_PAD))
    hl_spec = (pl.BlockSpec((1, 1, d), lambda i, j: (i, 0, 0)) if seq_mode else tok(d))
    hl_shape = (b, 1, d) if seq_mode else (b, t, d)
    out_cols = [D_A] * 6 + [D_B] * 2 + [d] * 2
    out_dtypes = [F32] * 6 + [BF16, F32] + [BF16] * 2
    est = (2 * tm * (d + N_SHIFT_PAD + sum(out_cols) + d) * 4 + 2 * d * N_IN_PAD * 2
           + 3 * tm * N_IN_PAD * 4 + 4 * 2**20)
    outs = pl.pallas_call(
        functools.partial(_in_body, seq_mode, tm),
        grid=grid,
        in_specs=[tok(d), mod, mod,
                  _const_spec((d, N_IN_PAD), True), _const_spec((1, N_SHIFT_PAD)), zp_spec,
                  _const_spec((LANES, 2 * D_A)), _const_spec((1, 2 * D_A)),
                  _const_spec((1, D_A)), _const_spec((1, D_A)),
                  _const_spec((1, D_B)), _const_spec((1, D_B)),
                  _const_spec((D_A, D_A))],
        out_specs=[tok(n) for n in out_cols] + [hl_spec],
        out_shape=[jax.ShapeDtypeStruct((b, t, n), dt) for n, dt in zip(out_cols, out_dtypes)]
        + [jax.ShapeDtypeStruct(hl_shape, F32)],
        scratch_shapes=[pltpu.VMEM((1, N_SHIFT_PAD), F32)],
        compiler_params=_params(("arbitrary", "arbitrary"), est),
        name="in_stage_seq" if seq_mode else "in_stage_row",
    )(x, sc, sh, p["w_in"], p["mu"], zprev, p["lora"], p["w0a0"], p["k_k"], p["k_a"],
      p["lnv_g"], p["lnv_b"], p["bd"])
    return outs


_WKV_3PASS = frozenset({"h"})


def _wmm(name, a, b, dims):
    return (_mm3 if name in _WKV_3PASS else _mm1)(a, b, dims)


def _chunk_pairs(x, nsub):
    c = WKV_CHUNK
    w = 2 * HEAD
    return jnp.stack([x[c * i:c * (i + 1), w * q:w * (q + 1)]
                      for i in range(nsub) for q in range(N_HEADS // 2)], axis=0)


def _masked_rowsum(mask_bf, x):
    h1, h2 = _split2(x)
    return _dot(mask_bf, h1) + _dot(mask_bf, h2)


def _masked_rowsum_t(x, mask):
    h1, h2 = _split2(x)
    m = mask.astype(BF16)
    return _dot(h1, m) + _dot(h2, m)


def _wkv_seq_body(nsub, r_ref, lw_ref, kp_ref, v_ref, kap_ref, bb_ref, s0_ref,
                  o_ref, sout_ref, s_scr):
    c = WKV_CHUNK
    tc = nsub * c
    npair = N_HEADS // 2

    @pl.when(pl.program_id(1) == 0)
    def _():
        for q in range(npair):
            s_scr[q] = jnp.concatenate([s0_ref[0, 2 * q], s0_ref[0, 2 * q + 1]], axis=-1)

    row_t = lax.broadcasted_iota(jnp.int32, (tc, tc), 0)
    col_t = lax.broadcasted_iota(jnp.int32, (tc, tc), 1)
    shift = c.bit_length() - 1
    same_chunk = (row_t >> shift) == (col_t >> shift)
    lw = lw_ref[0]
    g = _masked_rowsum((same_chunk & (row_t >= col_t)).astype(BF16), lw)
    g_end = _masked_rowsum(same_chunk.astype(BF16), lw)
    e_neg = jnp.exp(-g)
    e_end = jnp.exp(g_end - g)
    cp = functools.partial(_chunk_pairs, nsub=nsub)
    k = kp_ref[0]
    b = bb_ref[0]
    kap_t = cp(kap_ref[0] * jnp.exp(g - lw))
    b_t = cp(b * e_neg)
    k_t = cp(k * e_neg)
    r_t = cp(r_ref[0] * jnp.exp(g))
    b_e = cp(b * e_end)
    k_e = cp(k * e_end)
    vv = cp(v_ref[0])
    decay_end = cp(jnp.exp(g_end))

    row = lax.broadcasted_iota(jnp.int32, (c, 2 * c), 0)[None]
    lane = lax.broadcasted_iota(jnp.int32, (c, 2 * c), 1)[None]
    colp = lane & (c - 1)
    right = lane >= c
    row2 = lax.broadcasted_iota(jnp.int32, (2 * c, 2 * c), 0)[None]
    lane2 = lax.broadcasted_iota(jnp.int32, (2 * c, 2 * c), 1)[None]
    same_head = (row2 >= c) == (lane2 >= c)

    def bd(x):
        return jnp.concatenate([jnp.where(right, 0.0, x), jnp.where(right, x, 0.0)], axis=1)

    p_b = _wmm("lb", jnp.concatenate([kap_t, r_t], axis=1), bd(b_t), _B_NT)
    l_b = jnp.where(row > colp, p_b[:, :c], 0.0)
    a_rb = jnp.where(row >= colp, p_b[:, c:], 0.0)
    bd_k = bd(k_t)
    l_k = jnp.where(row > colp, _wmm("lk", kap_t, bd_k, _B_NT), 0.0)
    a_rk = jnp.where(row >= colp, _wmm("ark", r_t, bd_k, _B_NT), 0.0)
    m = -l_b
    t_inv = jnp.where(row == colp, 1.0, 0.0) + m
    m = _wmm("inv", m, bd(m), _B_NN)
    span = 2
    while 2 * span < c:
        both = _wmm("inv", jnp.concatenate([m, t_inv], axis=1), bd(m), _B_NN)
        m = both[:, :c]
        t_inv = t_inv + both[:, c:]
        span *= 2
    t_inv = t_inv + _wmm("inv", t_inv, bd(m), _B_NN)
    bd_v = bd(vv)
    a1 = -_wmm("ta", t_inv, bd(kap_t), _B_NN)
    u0 = -_wmm("tu", t_inv, bd(_wmm("lkv", l_k, bd_v, _B_NN)), _B_NN)
    a2 = r_t + _wmm("arba", a_rb, bd(a1), _B_NN)
    o0 = _wmm("arbu", a_rb, bd(u0), _B_NN) + _wmm("arkv", a_rk, bd_v, _B_NN)
    g_bd = (jnp.where(same_head, _wmm("g", a1, b_e, _B_TN), 0.0)
            + jnp.where(row2 == lane2, decay_end[:, 0:1, :], 0.0))
    hh = _wmm("h", jnp.concatenate([u0, vv], axis=1),
              jnp.concatenate([b_e, k_e], axis=1), _B_TN)
    h_pair = jnp.where(right, hh[:, c:], hh[:, :c])

    s = s_scr[...]
    for i in range(nsub):
        ps = slice(npair * i, npair * (i + 1))
        o = _wmm("o", a2[ps], bd(s), _B_NT) + o0[ps]
        s = _wmm("s", s, g_bd[ps], _B_NN) + h_pair[ps]
        for q in range(npair):
            o_ref[0, c * i:c * (i + 1), 2 * HEAD * q:2 * HEAD * (q + 1)] = o[q]
    s_scr[...] = s

    @pl.when(pl.program_id(1) == pl.num_programs(1) - 1)
    def _():
        for q in range(npair):
            sout_ref[0, 2 * q] = s[q][:, :HEAD]
            sout_ref[0, 2 * q + 1] = s[q][:, HEAD:]


def _wkv_seq(r, lw, kp, v, kap, bb, s0):
    b, t, _ = r.shape
    tc = 256
    nsub = tc // WKV_CHUNK
    tok = pl.BlockSpec((1, tc, D_A), lambda i, j: (i, j, 0))
    st = pl.BlockSpec((1, N_HEADS, HEAD, HEAD), lambda i, j: (i, 0, 0, 0))
    return pl.pallas_call(
        functools.partial(_wkv_seq_body, nsub),
        grid=(b, t // tc),
        in_specs=[tok] * 6 + [st],
        out_specs=[tok, st],
        out_shape=[jax.ShapeDtypeStruct((b, t, D_A), F32),
                   jax.ShapeDtypeStruct((b, N_HEADS, HEAD, HEAD), F32)],
        scratch_shapes=[pltpu.VMEM((N_HEADS // 2, HEAD, 2 * HEAD), F32)],
        compiler_params=_params(("arbitrary", "arbitrary"), 32 * 2**20),
        name="wkv_seq",
    )(r, lw, kp, v, kap, bb, s0)


def _wkv_step_body(r_ref, lw_ref, kp_ref, v_ref, kap_ref, bb_ref, s0_ref, _, o_ref, sout_ref):
    eye = (lax.broadcasted_iota(jnp.int32, (HEAD, HEAD), 0)
           == lax.broadcasted_iota(jnp.int32, (HEAD, HEAD), 1))[None]
    for h in range(N_HEADS):
        hs = slice(HEAD * h, HEAD * (h + 1))
        row = lambda ref: ref[:, hs][:, None, :]
        s = s0_ref[:, h]
        sa = -jnp.sum(s * row(kap_ref), axis=-1, keepdims=True)
        v_col = jnp.sum(jnp.where(eye, row(v_ref), 0.0), axis=-1, keepdims=True)
        s_new = s * jnp.exp(row(lw_ref)) + sa * row(bb_ref) + v_col * row(kp_ref)
        sout_ref[:, h] = s_new
        o_col = jnp.sum(s_new * row(r_ref), axis=-1, keepdims=True)
        o_ref[:, hs] = jnp.sum(jnp.where(eye, o_col, 0.0), axis=1)


def _wkv_step(r, lw, kp, v, kap, bb, s0, s_new_all, layer):
    m = r.shape[0]
    nb = 8
    tok = pl.BlockSpec((nb, D_A), lambda i: (i, 0))
    st = pl.BlockSpec((nb, N_HEADS, HEAD, HEAD), lambda i: (i, 0, 0, 0))
    st_out = pl.BlockSpec((None, nb, N_HEADS, HEAD, HEAD), lambda i: (layer, i, 0, 0, 0))
    return pl.pallas_call(
        _wkv_step_body,
        grid=(m // nb,),
        in_specs=[tok] * 6 + [st, pl.BlockSpec(memory_space=pl.ANY)],
        out_specs=[tok, st_out],
        out_shape=[jax.ShapeDtypeStruct((m, D_A), F32),
                   jax.ShapeDtypeStruct(s_new_all.shape, F32)],
        input_output_aliases={7: 1},
        compiler_params=_params(("arbitrary",), 24 * 2**20),
        name="wkv_step",
    )(r, lw, kp, v, kap, bb, s0, s_new_all)


def _merge_body(seq_mode, tm, x_ref, o_ref, r_ref, kp_ref, v_ref, u_ref, vg_ref,
                ga_ref, gb_ref, g1_ref, sc2_ref, sh2_ref, lxg_ref, lxb_ref, rk_ref,
                bd_ref, ws_ref, bs_ref, pa_ref, pb_ref, wo_ref, l1g_ref, l1b_ref,
                x1_ref, h2_ref):
    bd = bd_ref[...]
    o = o_ref[0]
    inv_n = 1.0 / HEAD
    mu = _segsum(o, bd) * inv_n
    d = o - mu
    var = _segsum(d * d, bd) * inv_n
    on = d * lax.rsqrt(var + GN_EPS) * lxg_ref[...] + lxb_ref[...]
    v = v_ref[0]
    o_a = on + _segsum(r_ref[0] * kp_ref[0] * rk_ref[...], bd) * v
    vg = vg_ref[0]
    if seq_mode:
        row = lax.broadcasted_iota(jnp.int32, (CHUNK, CHUNK), 0)
        col = lax.broadcasted_iota(jnp.int32, (CHUNK, CHUNK), 1)
        lane = lax.broadcasted_iota(jnp.int32, (CHUNK, D_B), 1)
        gsz = D_B // N_GROUPS_B
        w_cat = jnp.concatenate(
            [jnp.where(row >= col, ws_ref[g], 0.0).astype(BF16) for g in range(N_GROUPS_B)], axis=1)
        pieces = []
        for j in range(tm // CHUNK):
            vc = vg[j * CHUNK:(j + 1) * CHUNK, :].astype(BF16)
            v_bd = jnp.concatenate(
                [jnp.where((lane >= g * gsz) & (lane < (g + 1) * gsz), vc, 0.0)
                 for g in range(N_GROUPS_B)], axis=0)
            pieces.append(_dot(w_cat, v_bd) + bs_ref[...])
        s = jnp.concatenate(pieces, axis=0) if len(pieces) > 1 else pieces[0]
    else:
        s = vg * ws_ref[...] + bs_ref[...]
    o_b = u_ref[0] * s
    y = _mm1(ga_ref[0] * _mm1(o_a, pa_ref[...]) + gb_ref[0] * _mm1(o_b, pb_ref[...]),
             wo_ref[...])
    x1 = _layer_norm(ALPHA * x_ref[0] + g1_ref[0] * y, l1g_ref[...], l1b_ref[...])
    x1_ref[0] = x1
    h2_ref[0] = x1 * (1.0 + sc2_ref[0]) + sh2_ref[0]


def _merge_stage(seq_mode, x, o, r, kp, v, u, vg, ga, gb, g1, sc2, sh2, p):
    b, t, d = x.shape
    tm = 256 if seq_mode else t
    tmod = 1 if seq_mode else tm
    tok = lambda n: pl.BlockSpec((1, tm, n), lambda i, j: (i, j, 0))
    mod = pl.BlockSpec((1, tmod, d), lambda i, j: (i, j if not seq_mode else 0, 0))
    ws, bs = (p["w_spatial"], p["b_spatial_full"]) if seq_mode else (p["ws_row"], p["bs_row"])
    est = 2 * tm * (2 * d + 7 * D_A + 2 * d + 2 * d) * 4 + 8 * tm * d * 4 + 16 * 2**20
    return pl.pallas_call(
        functools.partial(_merge_body, seq_mode, tm),
        grid=(b, t // tm),
        in_specs=[tok(d)] + [tok(D_A)] * 6 + [tok(d), tok(d), mod, mod, mod,
                  _const_spec((1, D_A)), _const_spec((1, D_A)), _const_spec((1, D_A)),
                  _const_spec((D_A, D_A)), _const_spec(ws.shape), _const_spec(bs.shape),
                  _const_spec((D_A, d)), _const_spec((D_B, d)), _const_spec((d, d)),
                  _const_spec((1, d)), _const_spec((1, d))],
        out_specs=[tok(d), tok(d)],
        out_shape=[jax.ShapeDtypeStruct((b, t, d), F32)] * 2,
        compiler_params=_params(("arbitrary", "arbitrary"), est),
        name="merge_seq" if seq_mode else "merge_row",
    )(x, o, r, kp, v, u, vg, ga, gb, g1, sc2, sh2, p["lnx_g"], p["lnx_b"], p["r_k"],
      p["bd"], ws, bs, p["w_branch_a"], p["w_branch_b"], p["w_out"], p["ln1_g"], p["ln1_b"])


def _route_body(h_ref, w_ref, b_ref, gate_ref, idx_ref, wts_ref, cnt_ref):
    lg = _mm3(h_ref[...], w_ref[...]) + b_ref[...]
    lane = lax.broadcasted_iota(jnp.int32, lg.shape, 1)
    lanef = lane.astype(F32)
    neg = -jnp.inf
    is_g = (lane >= N_EXPERTS) & (lane < N_EXPERTS + N_ROUTE_GROUPS)
    mg = jnp.max(jnp.where(is_g, lg, neg), axis=-1, keepdims=True)
    gidx = jnp.min(jnp.where(is_g & (lg == mg), lanef - N_EXPERTS, 1e9), axis=-1, keepdims=True)
    pg_sel = 1.0 / jnp.sum(jnp.where(is_g, jnp.exp(lg - mg), 0.0), axis=-1, keepdims=True)
    lo = gidx * EXP_PER_GROUP
    in_grp = (lanef >= lo) & (lanef < lo + EXP_PER_GROUP)
    t1 = jnp.max(jnp.where(in_grp, lg, neg), axis=-1, keepdims=True)
    i1 = jnp.min(jnp.where(in_grp & (lg == t1), lanef, 1e9), axis=-1, keepdims=True)
    rest = in_grp & (lanef != i1)
    t2 = jnp.max(jnp.where(rest, lg, neg), axis=-1, keepdims=True)
    i2 = jnp.min(jnp.where(rest & (lg == t2), lanef, 1e9), axis=-1, keepdims=True)
    e2 = jnp.exp(t2 - t1)
    w1 = pg_sel / (1.0 + e2)
    w2 = pg_sel * e2 / (1.0 + e2)
    gate_ref[...] = jnp.where(lanef == i1, w1, 0.0) + jnp.where(lanef == i2, w2, 0.0)
    tm = lg.shape[0]
    hit = (lanef == i1) | (lanef == i2)
    earlier = (lax.broadcasted_iota(jnp.int32, (tm, tm), 0)
               > lax.broadcasted_iota(jnp.int32, (tm, tm), 1))
    rank = _dot(earlier.astype(BF16), hit.astype(BF16))
    cnt = jnp.sum(hit.astype(F32), axis=0, keepdims=True)
    cnt8 = jnp.floor((cnt + (MOE_RUN_ALIGN - 1.0)) * (1.0 / MOE_RUN_ALIGN)) * MOE_RUN_ALIGN
    lower_expert = (lax.broadcasted_iota(jnp.int32, (LANES, LANES), 0)
                    < lax.broadcasted_iota(jnp.int32, (LANES, LANES), 1))
    run_start = _masked_rowsum_t(jnp.broadcast_to(cnt8, (8, LANES)), lower_expert)[0:1]
    lpos = run_start + rank
    lp1 = jnp.sum(jnp.where(lanef == i1, lpos, 0.0), axis=-1, keepdims=True)
    lp2 = jnp.sum(jnp.where(lanef == i2, lpos, 0.0), axis=-1, keepdims=True)
    cnt_ref[0] = cnt.astype(jnp.int32)
    idx = jnp.where(lane == 0, i1, jnp.where(lane == 1, i2, jnp.where(lane == 2, lp1, lp2)))
    idx_ref[...] = idx.astype(jnp.int32)
    wts_ref[...] = jnp.where(lane == 0, w1, w2)


def _route(h2, p):
    m, d = h2.shape
    tm = min(m, MOE_TILE)
    tok = pl.BlockSpec((tm, LANES), lambda i: (i, 0))
    return pl.pallas_call(
        _route_body,
        grid=(m // tm,),
        in_specs=[pl.BlockSpec((tm, d), lambda i: (i, 0)),
                  _const_spec((d, LANES)), _const_spec((1, LANES))],
        out_specs=[tok, tok, tok, pl.BlockSpec((1, 1, LANES), lambda i: (i, 0, 0))],
        out_shape=[jax.ShapeDtypeStruct((m, LANES), F32),
                   jax.ShapeDtypeStruct((m, LANES), jnp.int32),
                   jax.ShapeDtypeStruct((m, LANES), F32),
                   jax.ShapeDtypeStruct((m // tm, 1, LANES), jnp.int32)],
        compiler_params=_params(("arbitrary",), 24 * 2**20),
        name="route",
    )(h2, p["w_route"], p["b_route"])


def _moe_body(h_ref, gate_ref, wg_ref, wu_ref, wd_ref, x1_ref, g2_ref, l2g_ref, l2b_ref,
              out_ref, acc_ref, xb_ref):
    e = pl.program_id(1)

    @pl.when(e == 0)
    def _():
        acc_ref[...] = jnp.zeros_like(acc_ref)
        xb_ref[...] = h_ref[...].astype(BF16)

    xb = xb_ref[...]
    pre = _dot(xb, wg_ref[0].astype(BF16))
    hid = pre * jax.nn.sigmoid(pre) * _dot(xb, wu_ref[0].astype(BF16))
    ye = _mm1(hid, wd_ref[0])
    gate = gate_ref[...]
    lane = lax.broadcasted_iota(jnp.int32, gate.shape, 1)
    ge = jnp.sum(jnp.where(lane == e, gate, 0.0), axis=-1, keepdims=True)
    acc_ref[...] += ge * ye

    @pl.when(e == pl.num_programs(1) - 1)
    def _():
        out_ref[...] = _layer_norm(ALPHA * x1_ref[...] + g2_ref[0] * acc_ref[...],
                                   l2g_ref[...], l2b_ref[...])


def _moe(seq_len, h2, gate, x1, g2, p):
    m, d = h2.shape
    layer = p["layer"]
    tm = min(seq_len if g2.shape[1] == 1 else m, 1024)
    tok = lambda n: pl.BlockSpec((tm, n), lambda i, e: (i, 0))
    if g2.shape[1] == 1:
        tiles_per_seq = seq_len // tm
        g2_spec = pl.BlockSpec((1, 1, d), lambda i, e: (i // tiles_per_seq, 0, 0))
    else:
        g2_spec = pl.BlockSpec((1, tm, d), lambda i, e: (0, i, 0))
    est = 2 * tm * (3 * d + LANES) * 4 + tm * d * 6 + 4 * tm * d * 4 + 8 * 2**20
    return pl.pallas_call(
        _moe_body,
        grid=(m // tm, N_EXPERTS),
        in_specs=[tok(d), tok(LANES),
                  pl.BlockSpec((None, 1, d, D_EXPERT), lambda i, e: (layer, e, 0, 0)),
                  pl.BlockSpec((None, 1, d, D_EXPERT), lambda i, e: (layer, e, 0, 0)),
                  pl.BlockSpec((None, 1, D_EXPERT, d), lambda i, e: (layer, e, 0, 0)),
                  tok(d), g2_spec, _const_spec((1, d)), _const_spec((1, d))],
        out_specs=tok(d),
        out_shape=jax.ShapeDtypeStruct((m, d), F32),
        scratch_shapes=[pltpu.VMEM((tm, d), F32), pltpu.VMEM((tm, d), BF16)],
        compiler_params=_params(("arbitrary", "arbitrary"), est),
        name="moe_dense",
    )(h2, gate, p["w_exp_gate"], p["w_exp_up"], p["w_exp_down"], x1, g2, p["ln2_g"], p["ln2_b"])


MOE_TILE = 512
MOE_BLOCK = 512
MOE_RUN_ALIGN = 16
MOE_LOCAL_ROWS = 2 * MOE_TILE + 512
MOE_SLABS = tuple(2 ** k for k in range(9, 3, -1))


def _for_each_slab(run_ref, make_copy, fn):
    def one_run(e, c):
        dst = run_ref[0, 0, e]
        src = run_ref[0, 0, N_EXPERTS + e]
        n = run_ref[0, 0, 2 * N_EXPERTS + e]
        for slab in MOE_SLABS:
            off = n & (-2 * slab)

            @pl.when((n & slab) != 0)
            def _():
                fn(make_copy(pl.multiple_of(dst + off, MOE_RUN_ALIGN),
                             pl.multiple_of(src + off, MOE_RUN_ALIGN), slab))
        return c

    lax.fori_loop(0, N_EXPERTS, one_run, 0)


def _dispatch_body(tail_ref, run_ref, prev_run_ref, h_ref, idx_ref, xs_hbm, zero_buf, loc_buf,
                   sem, zsem, usem):
    nb_max = xs_hbm.shape[0] // MOE_BLOCK
    nt = MOE_TILE

    def zero_copy(row0, zero_sem):
        return pltpu.make_async_copy(
            zero_buf, xs_hbm.at[pl.ds(pl.multiple_of(row0, MOE_BLOCK), MOE_BLOCK)], zero_sem)

    def each_unused_copy(fn):
        def unused(j, c):
            fn(zero_copy(j * MOE_BLOCK, usem))
            return c

        lax.fori_loop(tail_ref[N_EXPERTS], nb_max, unused, 0)

    @pl.when(pl.program_id(0) == 0)
    def _():
        zero_buf[...] = jnp.zeros_like(zero_buf)

        def each_tail_copy(fn):
            for e in range(N_EXPERTS):
                @pl.when(tail_ref[e] >= 0)
                def _():
                    fn(zero_copy(tail_ref[e], zsem))

        each_tail_copy(lambda cp: cp.start())
        each_unused_copy(lambda cp: cp.start())
        each_tail_copy(lambda cp: cp.wait())

    lp = idx_ref[...].astype(F32)
    eye = (lax.broadcasted_iota(jnp.int32, (nt, nt), 0)
           == lax.broadcasted_iota(jnp.int32, (nt, nt), 1))
    as_row = lambda col: jnp.sum(jnp.where(eye, col, 0.0), axis=0, keepdims=True)
    r = lax.broadcasted_iota(jnp.int32, (MOE_LOCAL_ROWS, nt), 0).astype(F32)
    pick = (r == as_row(lp[:, 2:3])) | (r == as_row(lp[:, 3:4]))
    step = pl.program_id(0)
    slot = step & 1
    loc_buf[slot] = _dot(pick.astype(BF16), h_ref[...].astype(BF16)).astype(BF16)

    def copies_from(which):
        def make_copy(dst, src, rows):
            return pltpu.make_async_copy(loc_buf.at[which, pl.ds(src, rows)],
                                         xs_hbm.at[pl.ds(dst, rows)], sem.at[which])
        return make_copy

    _for_each_slab(run_ref, copies_from(slot), lambda cp: cp.start())

    @pl.when(step > 0)
    def _():
        _for_each_slab(prev_run_ref, copies_from(1 - slot), lambda cp: cp.wait())

    @pl.when(step == pl.num_programs(0) - 1)
    def _():
        _for_each_slab(run_ref, copies_from(slot), lambda cp: cp.wait())
        each_unused_copy(lambda cp: cp.wait())


def _dispatch(h2, idx, runs, tail_start, n_rows):
    m, d = h2.shape
    nt = MOE_TILE
    run_spec = lambda at: pl.BlockSpec((1, 1, LANES), lambda i, tail: (at(i), 0, 0),
                                       memory_space=pltpu.SMEM)
    grid_spec = pltpu.PrefetchScalarGridSpec(
        num_scalar_prefetch=1,
        grid=(m // nt,),
        in_specs=[run_spec(lambda i: i), run_spec(lambda i: jnp.maximum(i - 1, 0)),
                  pl.BlockSpec((nt, d), lambda i, tail: (i, 0)),
                  pl.BlockSpec((nt, LANES), lambda i, tail: (i, 0))],
        out_specs=pl.BlockSpec(memory_space=pl.ANY),
        scratch_shapes=[pltpu.VMEM((MOE_BLOCK, d), BF16),
                        pltpu.VMEM((2, MOE_LOCAL_ROWS, d), BF16),
                        pltpu.SemaphoreType.DMA((2,)), pltpu.SemaphoreType.DMA,
                        pltpu.SemaphoreType.DMA],
    )
    return pl.pallas_call(
        _dispatch_body,
        grid_spec=grid_spec,
        out_shape=jax.ShapeDtypeStruct((n_rows, d), BF16),
        compiler_params=_params(("arbitrary",), 40 * 2**20),
        name="moe_dispatch",
    )(tail_start, runs, runs, h2, idx)


def _gmm_body(be_ref, nb_ref, x_ref, wg_ref, wu_ref, wd_ref, y_ref):
    j = pl.program_id(0)

    @pl.when(j < nb_ref[0])
    def _():
        xb = x_ref[...]
        pre = _dot(xb, wg_ref[0].astype(BF16))
        hid = pre * jax.nn.sigmoid(pre) * _dot(xb, wu_ref[0].astype(BF16))
        y_ref[...] = _mm1(hid, wd_ref[0]).astype(BF16)

    @pl.when(j >= nb_ref[0])
    def _():
        y_ref[...] = jnp.zeros_like(y_ref)


def _gmm(xs, blk_expert, n_blocks, p):
    n_rows, d = xs.shape
    nb_max = n_rows // MOE_BLOCK
    layer = p["layer"]
    live = lambda j, be, nb: jnp.minimum(j, nb[0] - 1)
    wspec = lambda shape: pl.BlockSpec((None,) + shape,
                                       lambda j, be, nb: (layer, be[live(j, be, nb)], 0, 0))
    grid_spec = pltpu.PrefetchScalarGridSpec(
        num_scalar_prefetch=2,
        grid=(nb_max,),
        in_specs=[pl.BlockSpec((MOE_BLOCK, d), lambda j, be, nb: (live(j, be, nb), 0)),
                  wspec((1, d, D_EXPERT)), wspec((1, d, D_EXPERT)), wspec((1, D_EXPERT, d))],
        out_specs=pl.BlockSpec((MOE_BLOCK, d), lambda j, be, nb: (j, 0)),
    )
    return pl.pallas_call(
        _gmm_body,
        grid_spec=grid_spec,
        out_shape=jax.ShapeDtypeStruct((n_rows, d), BF16),
        compiler_params=_params(("arbitrary",), 24 * 2**20),
        name="moe_gmm",
    )(blk_expert, n_blocks, xs, p["w_exp_gate"], p["w_exp_up"], p["w_exp_down"])


def _combine_body(run_ref, next_run_ref, ys_hbm, idx_ref, wts_ref, x1_ref, g2_ref, l2g_ref,
                  l2b_ref, out_ref, loc_buf, sem):
    nt = MOE_TILE
    step = pl.program_id(0)
    slot = step & 1

    def copies_into(which):
        def make_copy(dst, src, rows):
            return pltpu.make_async_copy(ys_hbm.at[pl.ds(dst, rows)],
                                         loc_buf.at[which, pl.ds(src, rows)], sem.at[which])
        return make_copy

    @pl.when(step == 0)
    def _():
        loc_buf[...] = jnp.zeros_like(loc_buf)
        _for_each_slab(run_ref, copies_into(slot), lambda cp: cp.start())

    @pl.when(step < pl.num_programs(0) - 1)
    def _():
        _for_each_slab(next_run_ref, copies_into(1 - slot), lambda cp: cp.start())

    _for_each_slab(run_ref, copies_into(slot), lambda cp: cp.wait())
    lp = idx_ref[...].astype(F32)
    w = wts_ref[...]
    c = lax.broadcasted_iota(jnp.int32, (nt, MOE_LOCAL_ROWS), 1).astype(F32)
    sel = (jnp.where(c == lp[:, 2:3], w[:, 0:1], 0.0)
           + jnp.where(c == lp[:, 3:4], w[:, 1:2], 0.0))
    sel_hi, sel_lo = _split2(sel)
    rows_bf = loc_buf[slot]
    moe = _dot(sel_hi, rows_bf) + _dot(sel_lo, rows_bf)
    out_ref[...] = _layer_norm(ALPHA * x1_ref[...] + g2_ref[0] * moe, l2g_ref[...], l2b_ref[...])


def _combine(seq_len, ys, idx, wts, runs, x1, g2, p):
    m, d = x1.shape
    nt = MOE_TILE
    tiles_per_seq = seq_len // nt
    tok = lambda n: pl.BlockSpec((nt, n), lambda i: (i, 0))
    n_tiles = m // nt
    run_spec = lambda at: pl.BlockSpec((1, 1, LANES), lambda i: (at(i), 0, 0),
                                       memory_space=pltpu.SMEM)
    return pl.pallas_call(
        _combine_body,
        grid=(n_tiles,),
        in_specs=[run_spec(lambda i: i), run_spec(lambda i: jnp.minimum(i + 1, n_tiles - 1)),
                  pl.BlockSpec(memory_space=pl.ANY), tok(LANES), tok(LANES), tok(d),
                  pl.BlockSpec((1, 1, d), lambda i: (i // tiles_per_seq, 0, 0)),
                  _const_spec((1, d)), _const_spec((1, d))],
        out_specs=tok(d),
        out_shape=jax.ShapeDtypeStruct((m, d), F32),
        scratch_shapes=[pltpu.VMEM((2, MOE_LOCAL_ROWS, d), BF16), pltpu.SemaphoreType.DMA((2,))],
        compiler_params=_params(("arbitrary",), 48 * 2**20),
        name="moe_combine",
    )(runs, runs, ys, idx, wts, x1, g2, p["ln2_g"], p["ln2_b"])


def _moe_routed(seq_len, h2, idx, wts, tile_cnt, x1, g2, p):
    m, d = h2.shape
    blk = MOE_BLOCK
    n_tiles = m // MOE_TILE
    worst_rows = 2 * m + n_tiles * N_EXPERTS * (MOE_RUN_ALIGN - 1) + N_EXPERTS * (blk - 1)
    nb_max = -(-worst_rows // blk)
    cnt = tile_cnt[:, 0, :N_EXPERTS]
    run_len = ((cnt + MOE_RUN_ALIGN - 1) // MOE_RUN_ALIGN) * MOE_RUN_ALIGN
    local_row = jnp.cumsum(run_len, axis=1) - run_len
    rows_before = jnp.cumsum(run_len, axis=0) - run_len
    total = jnp.sum(run_len, axis=0)
    padded = ((total + blk - 1) // blk) * blk
    ends = jnp.cumsum(padded)
    starts = ends - padded
    n_blocks = (ends[-1] // blk).astype(jnp.int32).reshape(1)
    first_row = jnp.arange(nb_max, dtype=jnp.int32) * blk
    blk_expert = jnp.minimum(
        jnp.sum((first_row[:, None] >= ends[None, :]).astype(jnp.int32), axis=1),
        N_EXPERTS - 1).astype(jnp.int32)
    tail_start = jnp.concatenate(
        [jnp.where(padded > 0, ends - blk, -1).astype(jnp.int32), n_blocks])
    runs = jnp.concatenate(
        [starts[None, :] + rows_before, local_row, run_len,
         jnp.zeros((n_tiles, LANES - 3 * N_EXPERTS), jnp.int32)], axis=1).astype(jnp.int32)
    runs = runs.reshape(n_tiles, 1, LANES)
    xs = _dispatch(h2, idx, runs, tail_start, nb_max * blk)
    ys = _gmm(xs, blk_expert, n_blocks, p)
    return _combine(seq_len, ys, idx, wts, runs, x1, g2, p)


def _prep_layer(l, w_in, mu_shift, w0, w_decay_up, a0, w_iclr_up, k_k, k_a, r_k,
                lnx_g, lnx_b, lnv_g, lnv_b, w_spatial, b_spatial, w_branch_a, w_branch_b,
                w_out, ln1_g, ln1_b, w_route_group, b_route_group, w_route_expert,
                b_route_expert, w_exp_gate, w_exp_up, w_exp_down, ln2_g, ln2_b):
    d = D_MODEL
    pad_a = N_SHIFT_PAD - N_SHIFT
    wi = w_in[l]
    w_in_p = jnp.concatenate(
        [wi[:, :N_SHIFT], jnp.zeros((d, pad_a), F32), wi[:, N_SHIFT:]], axis=1).astype(BF16)
    mu = jnp.concatenate([mu_shift[l], jnp.zeros((pad_a,), F32)])[None]
    lora = jnp.zeros((LANES, 2 * D_A), F32)
    lora = lora.at[:R_LORA, :D_A].set(w_decay_up[l]).at[R_LORA:2 * R_LORA, D_A:].set(w_iclr_up[l])
    seg = jnp.arange(D_A) // HEAD
    row1 = lambda x: x.reshape(1, -1)
    gsz = D_B // N_GROUPS_B
    w_route = jnp.concatenate(
        [w_route_expert[l], w_route_group[l],
         jnp.zeros((d, LANES - N_EXPERTS - N_ROUTE_GROUPS), F32)], axis=1)
    b_route = jnp.concatenate(
        [b_route_expert[l], b_route_group[l],
         jnp.zeros((LANES - N_EXPERTS - N_ROUTE_GROUPS,), F32)])[None]
    return dict(
        w_in=w_in_p, mu=mu, lora=lora,
        w0a0=jnp.concatenate([w0[l], a0[l]])[None],
        k_k=row1(k_k[l]), k_a=row1(k_a[l]), r_k=row1(r_k[l]),
        lnx_g=row1(lnx_g[l]), lnx_b=row1(lnx_b[l]),
        lnv_g=row1(lnv_g[l]), lnv_b=row1(lnv_b[l]),
        bd=(seg[:, None] == seg[None, :]).astype(BF16),
        w_spatial=w_spatial[l],
        b_spatial_full=jnp.repeat(b_spatial[l].T, gsz, axis=1),
        ws_row=jnp.repeat(w_spatial[l][:, 0, 0], gsz)[None],
        bs_row=jnp.repeat(b_spatial[l][:, 0], gsz)[None],
        w_branch_a=w_branch_a[l].astype(BF16), w_branch_b=w_branch_b[l].astype(BF16),
        w_out=w_out[l].astype(BF16), ln1_g=row1(ln1_g[l]), ln1_b=row1(ln1_b[l]),
        w_route=w_route, b_route=b_route,
        w_exp_gate=w_exp_gate, w_exp_up=w_exp_up, w_exp_down=w_exp_down, layer=l,
        ln2_g=row1(ln2_g[l]), ln2_b=row1(ln2_b[l]),
    )


def _trunk(seq_mode, x, mods, wkv_in, shift_in, preps):
    b, t, d = x.shape
    wkv_out, shift_out, v_out = [], [], []
    wkv_acc = None if seq_mode else jnp.zeros(wkv_in.shape, F32)
    for l in range(DEPTH):
        p = preps[l]
        sh1, sc1, g1, sh2, sc2, g2 = mods[l]
        if seq_mode:
            zprev = jnp.zeros((b, 1, N_SHIFT_PAD), F32) if shift_in is None else shift_in[l]
        else:
            zprev = _matmul(shift_in[l], p["w_in"][:, :N_SHIFT_PAD]).reshape(b, t, N_SHIFT_PAD)
        r, lw, kp, v, kap, bb, u, vg, ga, gb, hl = _in_stage(seq_mode, x, sc1, sh1, zprev, p)
        if seq_mode:
            o, s_new = _wkv_seq(r, lw, kp, v, kap, bb, wkv_in[l])
        else:
            flat = lambda a: a.reshape(t, D_A)
            o, wkv_acc = _wkv_step(flat(r), flat(lw), flat(kp), flat(v), flat(kap), flat(bb),
                                   wkv_in[l], wkv_acc, l)
            s_new = None
            o = o.reshape(b, t, D_A)
        x1, h2 = _merge_stage(seq_mode, x, o, r, kp, v, u, vg, ga, gb, g1, sc2, sh2, p)
        m = b * t
        h2f = h2.reshape(m, d)
        gate, idx, wts, tile_cnt = _route(h2f, p)
        if seq_mode:
            x = _moe_routed(t, h2f, idx, wts, tile_cnt, x1.reshape(m, d), g2, p)
        else:
            x = _moe(t, h2f, gate, x1.reshape(m, d), g2, p)
        x = x.reshape(b, t, d)
        wkv_out.append(s_new)
        shift_out.append(hl)
        v_out.append(vg)
    if not seq_mode:
        wkv_out = wkv_acc
    return x, wkv_out, shift_out, v_out


def kernel(x_prompt, x_sample, c_prompt, c_sample, state_wkv, state_shift, w_ada, b_ada, w_in, mu_shift, w0, w_decay_up, a0, w_iclr_up, k_k, k_a, r_k, lnx_g, lnx_b, lnv_g, lnv_b, w_spatial, b_spatial, w_branch_a, w_branch_b, w_out, ln1_g, ln1_b, w_route_group, b_route_group, w_route_expert, b_route_expert, w_exp_gate, w_exp_up, w_exp_down, ln2_g, ln2_b):
    bp, tp, d = x_prompt.shape
    bs = x_sample.shape[0]
    layer_params = (w_in, mu_shift, w0, w_decay_up, a0, w_iclr_up, k_k, k_a, r_k, lnx_g,
                    lnx_b, lnv_g, lnv_b, w_spatial, b_spatial, w_branch_a, w_branch_b, w_out,
                    ln1_g, ln1_b, w_route_group, b_route_group, w_route_expert,
                    b_route_expert, w_exp_gate, w_exp_up, w_exp_down, ln2_g, ln2_b)
    preps = [_prep_layer(l, *layer_params) for l in range(DEPTH)]
    mod_all = _ada(jnp.concatenate([c_prompt, c_sample], axis=0), w_ada, b_ada)
    mods_p, mods_s = [], []
    for l in range(DEPTH):
        parts = jnp.split(mod_all[l], 6, axis=-1)
        mods_p.append([q[:bp].reshape(bp, 1, d) for q in parts])
        mods_s.append([q[bp:].reshape(1, bs, d) for q in parts])

    wkv0 = jnp.zeros((DEPTH, bp, N_HEADS, HEAD, HEAD), F32)
    y_p, wkv_p, shift_p, _ = _trunk(True, x_prompt, mods_p, wkv0, None, preps)
    y_s, wkv_s, shift_s, v_s = _trunk(False, x_sample.reshape(1, bs, d), mods_s, state_wkv,
                                      state_shift, preps)
    return (y_p,
            y_s.reshape(bs, 1, d),
            jnp.stack(wkv_p),
            jnp.stack([s.reshape(bp, d) for s in shift_p]),
            wkv_s,
            jnp.stack([s.reshape(bs, d) for s in shift_s]),
            jnp.stack([q.reshape(bs, 1, D_B) for q in v_s]))
```

```python
import functools

import jax
import jax.numpy as jnp
from jax import lax
from jax.experimental import pallas as pl
from jax.experimental.pallas import tpu as pltpu

F32 = jnp.float32
BF16 = jnp.bfloat16

D_MODEL = 1024
DEPTH = 2
HEAD = 64
N_HEADS = 8
D_A = N_HEADS * HEAD
R_LORA = 32
CHUNK = 128
N_GROUPS_B = 8
D_B = 512
N_SHIFT = 3 * D_A + 2 * R_LORA
N_ROUTE_GROUPS = 4
EXP_PER_GROUP = 8
N_EXPERTS = N_ROUTE_GROUPS * EXP_PER_GROUP
D_EXPERT = 256
ALPHA = (2 * DEPTH) ** 0.25
LN_EPS = 1e-5
GN_EPS = 64e-5

LANES = 128
N_SHIFT_PAD = 13 * LANES
COL_U = N_SHIFT_PAD
COL_VG = COL_U + D_B
COL_GA = COL_VG + D_B
COL_GB = COL_GA + D_MODEL
N_IN_PAD = COL_GB + D_MODEL
WKV_CHUNK = 64
IN_ROW_GROUPS = 4
IN_GROUP_ROWS = 128
VMEM_CAP_BYTES = 60000 * 1024

_NN = (((1,), (0,)), ((), ()))
_B_NT = (((2,), (2,)), ((0,), (0,)))
_B_NN = (((2,), (1,)), ((0,), (0,)))
_B_TN = (((1,), (1,)), ((0,), (0,)))


def _dot(a, b, dims=_NN):
    return lax.dot_general(a, b, dims, preferred_element_type=F32)


def _split2(x):
    hi = x.astype(BF16)
    lo = (x - hi.astype(F32)).astype(BF16)
    return hi, lo


def _mm1(a, b, dims=_NN):
    return _dot(a.astype(BF16), b.astype(BF16), dims)


def _mm3(a, b, dims=_NN):
    ah, al = _split2(a)
    bh, bl = _split2(b)
    return _dot(ah, bh, dims) + (_dot(ah, bl, dims) + _dot(al, bh, dims))


def _segsum(x, bd):
    hi, lo = _split2(x)
    return _dot(hi, bd) + _dot(lo, bd)


def _layer_norm(x, g, b):
    mu = jnp.mean(x, axis=-1, keepdims=True)
    d = x - mu
    var = jnp.mean(d * d, axis=-1, keepdims=True)
    return d * lax.rsqrt(var + LN_EPS) * g + b


def _gelu(x):
    return 0.5 * x * (1.0 + lax.erf(x * 0.7071067811865476))


def _params(sem, est_bytes):
    limit = int(min(VMEM_CAP_BYTES, max(est_bytes, 16 * 1024 * 1024)))
    return pltpu.CompilerParams(dimension_semantics=sem, vmem_limit_bytes=limit)


def _const_spec(shape, single_buffer=False):
    nd = len(shape)
    if single_buffer:
        return pl.BlockSpec(shape, lambda *_: (0,) * nd, pipeline_mode=pl.Buffered(1))
    return pl.BlockSpec(shape, lambda *_: (0,) * nd)


def _ada_body(c_ref, w_ref, b_ref, o_ref):
    c = c_ref[...]
    s = c * jax.nn.sigmoid(c)
    o_ref[0] = _mm3(s, w_ref[0]) + b_ref[0]


def _ada(c_all, w_ada, b_ada):
    depth, d, n6 = w_ada.shape
    m = c_all.shape[0]
    tn = 1024
    return pl.pallas_call(
        _ada_body,
        grid=(depth, n6 // tn),
        in_specs=[
            pl.BlockSpec((m, d), lambda l, j: (0, 0)),
            pl.BlockSpec((1, d, tn), lambda l, j: (l, 0, j)),
            pl.BlockSpec((1, 1, tn), lambda l, j: (l, 0, j)),
        ],
        out_specs=pl.BlockSpec((1, m, tn), lambda l, j: (l, 0, j)),
        out_shape=jax.ShapeDtypeStruct((depth, m, n6), F32),
        compiler_params=_params(("arbitrary", "arbitrary"), 24 * 2**20),
        name="ada_mod",
    )(c_all, w_ada, b_ada.reshape(depth, 1, n6))


def _mm_body(x_ref, w_ref, o_ref):
    o_ref[...] = _mm1(x_ref[...], w_ref[...])


def _matmul(x, w):
    m, k = x.shape
    n = w.shape[1]
    return pl.pallas_call(
        _mm_body,
        grid=(1,),
        in_specs=[_const_spec((m, k)), _const_spec((k, n))],
        out_specs=_const_spec((m, n)),
        out_shape=jax.ShapeDtypeStruct((m, n), F32),
        compiler_params=_params(("arbitrary",), 24 * 2**20),
        name="shift_proj",
    )(x, w)


def _in_body(seq_mode, tm, x_ref, sc_ref, sh_ref, w_ref, mu_ref, zp_ref, lora_ref,
             w0a0_ref, kk_ref, ka_ref, lng_ref, lnb_ref, bd_ref,
             r_ref, lw_ref, kp_ref, v_ref, kap_ref, bb_ref, u_ref, vg_ref,
             ga_ref, gb_ref, hl_ref, carry_ref):
    nsplit = IN_ROW_GROUPS if seq_mode else 1
    rows = tm // nsplit
    if seq_mode:
        @pl.when(pl.program_id(1) == 0)
        def _():
            carry_ref[...] = zp_ref[0]

        carry = carry_ref[...]
    bd = bd_ref[...]
    for part in range(nsplit):
        sl = slice(part * rows, (part + 1) * rows)
        h = x_ref[0, sl, :] * (1.0 + sc_ref[0]) + sh_ref[0]
        hb = h.astype(BF16)
        proj = lambda lo, hi: _dot(hb, w_ref[:, lo:hi])
        za = proj(0, N_SHIFT_PAD)
        if seq_mode:
            row = lax.broadcasted_iota(jnp.int32, za.shape, 0)
            prev = jnp.where(row == 0, carry, pltpu.roll(za, 1, 0))
            carry = za[rows - 1:rows, :]
        else:
            prev = zp_ref[0]
        mix = za + mu_ref[...] * (prev - za)
        r = mix[:, 0:D_A]
        k = mix[:, D_A:2 * D_A]
        v = mix[:, 2 * D_A:3 * D_A]
        xwa = mix[:, 3 * D_A:N_SHIFT_PAD]
        lane = lax.broadcasted_iota(jnp.int32, xwa.shape, 1)
        lora_in = jnp.where(lane < R_LORA, jnp.tanh(xwa), xwa)
        pre = w0a0_ref[...] + _mm3(lora_in, lora_ref[...])
        yw = -pre[:, :D_A]
        softplus = jnp.maximum(yw, 0.0) + jnp.log1p(jnp.exp(-jnp.abs(yw)))
        lw = -jnp.exp(-softplus - 0.5)
        a = jax.nn.sigmoid(pre[:, D_A:])
        kk = k * kk_ref[...]
        kap = kk / jnp.maximum(jnp.sqrt(_segsum(kk * kk, bd)), 1e-12)
        r_ref[0, sl, :] = r
        lw_ref[0, sl, :] = lw
        kp_ref[0, sl, :] = k * (1.0 + (a - 1.0) * ka_ref[...])
        v_ref[0, sl, :] = v
        kap_ref[0, sl, :] = kap
        bb_ref[0, sl, :] = kap * a
        u_ref[0, sl, :] = _gelu(proj(COL_U, COL_VG)).astype(BF16)
        vg_ref[0, sl, :] = _layer_norm(_gelu(proj(COL_VG, COL_GA)), lng_ref[...], lnb_ref[...])
        ga_ref[0, sl, :] = jax.nn.sigmoid(proj(COL_GA, COL_GB)).astype(BF16)
        gb_ref[0, sl, :] = jax.nn.sigmoid(proj(COL_GB, N_IN_PAD)).astype(BF16)
    if seq_mode:
        carry_ref[...] = carry
        hl_ref[0] = h[rows - 1:rows, :]
    else:
        hl_ref[0] = h


def _in_stage(seq_mode, x, sc, sh, zprev, p):
    b, t, d = x.shape
    tm = IN_ROW_GROUPS * IN_GROUP_ROWS if seq_mode else t
    tmod = 1 if seq_mode else tm
    grid = (b, t // tm)
    tok = lambda n: pl.BlockSpec((1, tm, n), lambda i, j: (i, j, 0))
    mod = pl.BlockSpec((1, tmod, d), lambda i, j: (i, j if not seq_mode else 0, 0))
    zp_spec = (pl.BlockSpec((1, 1, N_SHIFT_PAD), lambda i, j: (i, 0, 0)) if seq_mode
               else tok(N_SHIFT_PAD))
    hl_spec = (pl.BlockSpec((1, 1, d), lambda i, j: (i, 0, 0)) if seq_mode else tok(d))
    hl_shape = (b, 1, d) if seq_mode else (b, t, d)
    out_cols = [D_A] * 6 + [D_B] * 2 + [d] * 2
    out_dtypes = [F32] * 6 + [BF16, F32] + [BF16] * 2
    est = (2 * tm * (d + N_SHIFT_PAD + sum(out_cols) + d) * 4 + 2 * d * N_IN_PAD * 2
           + 3 * tm * N_IN_PAD * 4 + 4 * 2**20)
    outs = pl.pallas_call(
        functools.partial(_in_body, seq_mode, tm),
        grid=grid,
        in_specs=[tok(d), mod, mod,
                  _const_spec((d, N_IN_PAD), True), _const_spec((1, N_SHIFT_PAD)), zp_spec,
                  _const_spec((LANES, 2 * D_A)), _const_spec((1, 2 * D_A)),
                  _const_spec((1, D_A)), _const_spec((1, D_A)),
                  _const_spec((1, D_B)), _const_spec((1, D_B)),
                  _const_spec((D_A, D_A))],
        out_specs=[tok(n) for n in out_cols] + [hl_spec],
        out_shape=[jax.ShapeDtypeStruct((b, t, n), dt) for n, dt in zip(out_cols, out_dtypes)]
        + [jax.ShapeDtypeStruct(hl_shape, F32)],
        scratch_shapes=[pltpu.VMEM((1, N_SHIFT_PAD), F32)],
        compiler_params=_params(("arbitrary", "arbitrary"), est),
        name="in_stage_seq" if seq_mode else "in_stage_row",
    )(x, sc, sh, p["w_in"], p["mu"], zprev, p["lora"], p["w0a0"], p["k_k"], p["k_a"],
      p["lnv_g"], p["lnv_b"], p["bd"])
    return outs


_WKV_3PASS = frozenset({"h"})


def _wmm(name, a, b, dims):
    return (_mm3 if name in _WKV_3PASS else _mm1)(a, b, dims)


def _chunk_pairs(x, nsub):
    c = WKV_CHUNK
    w = 2 * HEAD
    return jnp.stack([x[c * i:c * (i + 1), w * q:w * (q + 1)]
                      for i in range(nsub) for q in range(N_HEADS // 2)], axis=0)


def _masked_rowsum(mask_bf, x):
    h1, h2 = _split2(x)
    return _dot(mask_bf, h1) + _dot(mask_bf, h2)


def _masked_rowsum_t(x, mask):
    h1, h2 = _split2(x)
    m = mask.astype(BF16)
    return _dot(h1, m) + _dot(h2, m)


def _wkv_seq_body(nsub, r_ref, lw_ref, kp_ref, v_ref, kap_ref, bb_ref, s0_ref,
                  o_ref, sout_ref, s_scr):
    c = WKV_CHUNK
    tc = nsub * c
    npair = N_HEADS // 2

    @pl.when(pl.program_id(1) == 0)
    def _():
        for q in range(npair):
            s_scr[q] = jnp.concatenate([s0_ref[0, 2 * q], s0_ref[0, 2 * q + 1]], axis=-1)

    row_t = lax.broadcasted_iota(jnp.int32, (tc, tc), 0)
    col_t = lax.broadcasted_iota(jnp.int32, (tc, tc), 1)
    shift = c.bit_length() - 1
    same_chunk = (row_t >> shift) == (col_t >> shift)
    lw = lw_ref[0]
    g = _masked_rowsum((same_chunk & (row_t >= col_t)).astype(BF16), lw)
    g_end = _masked_rowsum(same_chunk.astype(BF16), lw)
    e_neg = jnp.exp(-g)
    e_end = jnp.exp(g_end - g)
    cp = functools.partial(_chunk_pairs, nsub=nsub)
    k = kp_ref[0]
    b = bb_ref[0]
    kap_t = cp(kap_ref[0] * jnp.exp(g - lw))
    b_t = cp(b * e_neg)
    k_t = cp(k * e_neg)
    r_t = cp(r_ref[0] * jnp.exp(g))
    b_e = cp(b * e_end)
    k_e = cp(k * e_end)
    vv = cp(v_ref[0])
    decay_end = cp(jnp.exp(g_end))

    row = lax.broadcasted_iota(jnp.int32, (c, 2 * c), 0)[None]
    lane = lax.broadcasted_iota(jnp.int32, (c, 2 * c), 1)[None]
    colp = lane & (c - 1)
    right = lane >= c
    row2 = lax.broadcasted_iota(jnp.int32, (2 * c, 2 * c), 0)[None]
    lane2 = lax.broadcasted_iota(jnp.int32, (2 * c, 2 * c), 1)[None]
    same_head = (row2 >= c) == (lane2 >= c)

    def bd(x):
        return jnp.concatenate([jnp.where(right, 0.0, x), jnp.where(right, x, 0.0)], axis=1)

    p_b = _wmm("lb", jnp.concatenate([kap_t, r_t], axis=1), bd(b_t), _B_NT)
    l_b = jnp.where(row > colp, p_b[:, :c], 0.0)
    a_rb = jnp.where(row >= colp, p_b[:, c:], 0.0)
    bd_k = bd(k_t)
    l_k = jnp.where(row > colp, _wmm("lk", kap_t, bd_k, _B_NT), 0.0)
    a_rk = jnp.where(row >= colp, _wmm("ark", r_t, bd_k, _B_NT), 0.0)
    m = -l_b
    t_inv = jnp.where(row == colp, 1.0, 0.0) + m
    m = _wmm("inv", m, bd(m), _B_NN)
    span = 2
    while 2 * span < c:
        both = _wmm("inv", jnp.concatenate([m, t_inv], axis=1), bd(m), _B_NN)
        m = both[:, :c]
        t_inv = t_inv + both[:, c:]
        span *= 2
    t_inv = t_inv + _wmm("inv", t_inv, bd(m), _B_NN)
    bd_v = bd(vv)
    a1 = -_wmm("ta", t_inv, bd(kap_t), _B_NN)
    u0 = -_wmm("tu", t_inv, bd(_wmm("lkv", l_k, bd_v, _B_NN)), _B_NN)
    a2 = r_t + _wmm("arba", a_rb, bd(a1), _B_NN)
    o0 = _wmm("arbu", a_rb, bd(u0), _B_NN) + _wmm("arkv", a_rk, bd_v, _B_NN)
    g_bd = (jnp.where(same_head, _wmm("g", a1, b_e, _B_TN), 0.0)
            + jnp.where(row2 == lane2, decay_end[:, 0:1, :], 0.0))
    hh = _wmm("h", jnp.concatenate([u0, vv], axis=1),
              jnp.concatenate([b_e, k_e], axis=1), _B_TN)
    h_pair = jnp.where(right, hh[:, c:], hh[:, :c])

    s = s_scr[...]
    for i in range(nsub):
        ps = slice(npair * i, npair * (i + 1))
        o = _wmm("o", a2[ps], bd(s), _B_NT) + o0[ps]
        s = _wmm("s", s, g_bd[ps], _B_NN) + h_pair[ps]
        for q in range(npair):
            o_ref[0, c * i:c * (i + 1), 2 * HEAD * q:2 * HEAD * (q + 1)] = o[q]
    s_scr[...] = s

    @pl.when(pl.program_id(1) == pl.num_programs(1) - 1)
    def _():
        for q in range(npair):
            sout_ref[0, 2 * q] = s[q][:, :HEAD]
            sout_ref[0, 2 * q + 1] = s[q][:, HEAD:]


def _wkv_seq(r, lw, kp, v, kap, bb, s0):
    b, t, _ = r.shape
    tc = 256
    nsub = tc // WKV_CHUNK
    tok = pl.BlockSpec((1, tc, D_A), lambda i, j: (i, j, 0))
    st = pl.BlockSpec((1, N_HEADS, HEAD, HEAD), lambda i, j: (i, 0, 0, 0))
    return pl.pallas_call(
        functools.partial(_wkv_seq_body, nsub),
        grid=(b, t // tc),
        in_specs=[tok] * 6 + [st],
        out_specs=[tok, st],
        out_shape=[jax.ShapeDtypeStruct((b, t, D_A), F32),
                   jax.ShapeDtypeStruct((b, N_HEADS, HEAD, HEAD), F32)],
        scratch_shapes=[pltpu.VMEM((N_HEADS // 2, HEAD, 2 * HEAD), F32)],
        compiler_params=_params(("arbitrary", "arbitrary"), 32 * 2**20),
        name="wkv_seq",
    )(r, lw, kp, v, kap, bb, s0)


def _wkv_step_body(r_ref, lw_ref, kp_ref, v_ref, kap_ref, bb_ref, s0_ref, _, o_ref, sout_ref):
    eye = (lax.broadcasted_iota(jnp.int32, (HEAD, HEAD), 0)
           == lax.broadcasted_iota(jnp.int32, (HEAD, HEAD), 1))[None]
    for h in range(N_HEADS):
        hs = slice(HEAD * h, HEAD * (h + 1))
        row = lambda ref: ref[:, hs][:, None, :]
        s = s0_ref[:, h]
        sa = -jnp.sum(s * row(kap_ref), axis=-1, keepdims=True)
        v_col = jnp.sum(jnp.where(eye, row(v_ref), 0.0), axis=-1, keepdims=True)
        s_new = s * jnp.exp(row(lw_ref)) + sa * row(bb_ref) + v_col * row(kp_ref)
        sout_ref[:, h] = s_new
        o_col = jnp.sum(s_new * row(r_ref), axis=-1, keepdims=True)
        o_ref[:, hs] = jnp.sum(jnp.where(eye, o_col, 0.0), axis=1)


def _wkv_step(r, lw, kp, v, kap, bb, s0, s_new_all, layer):
    m = r.shape[0]
    nb = 8
    tok = pl.BlockSpec((nb, D_A), lambda i: (i, 0))
    st = pl.BlockSpec((nb, N_HEADS, HEAD, HEAD), lambda i: (i, 0, 0, 0))
    st_out = pl.BlockSpec((None, nb, N_HEADS, HEAD, HEAD), lambda i: (layer, i, 0, 0, 0))
    return pl.pallas_call(
        _wkv_step_body,
        grid=(m // nb,),
        in_specs=[tok] * 6 + [st, pl.BlockSpec(memory_space=pl.ANY)],
        out_specs=[tok, st_out],
        out_shape=[jax.ShapeDtypeStruct((m, D_A), F32),
                   jax.ShapeDtypeStruct(s_new_all.shape, F32)],
        input_output_aliases={7: 1},
        compiler_params=_params(("arbitrary",), 24 * 2**20),
        name="wkv_step",
    )(r, lw, kp, v, kap, bb, s0, s_new_all)


def _merge_body(seq_mode, tm, x_ref, o_ref, r_ref, kp_ref, v_ref, u_ref, vg_ref,
                ga_ref, gb_ref, g1_ref, sc2_ref, sh2_ref, lxg_ref, lxb_ref, rk_ref,
                bd_ref, ws_ref, bs_ref, pa_ref, pb_ref, wo_ref, l1g_ref, l1b_ref,
                x1_ref, h2_ref):
    bd = bd_ref[...]
    o = o_ref[0]
    inv_n = 1.0 / HEAD
    mu = _segsum(o, bd) * inv_n
    d = o - mu
    var = _segsum(d * d, bd) * inv_n
    on = d * lax.rsqrt(var + GN_EPS) * lxg_ref[...] + lxb_ref[...]
    v = v_ref[0]
    o_a = on + _segsum(r_ref[0] * kp_ref[0] * rk_ref[...], bd) * v
    vg = vg_ref[0]
    if seq_mode:
        row = lax.broadcasted_iota(jnp.int32, (CHUNK, CHUNK), 0)
        col = lax.broadcasted_iota(jnp.int32, (CHUNK, CHUNK), 1)
        lane = lax.broadcasted_iota(jnp.int32, (CHUNK, D_B), 1)
        gsz = D_B // N_GROUPS_B
        w_cat = jnp.concatenate(
            [jnp.where(row >= col, ws_ref[g], 0.0).astype(BF16) for g in range(N_GROUPS_B)], axis=1)
        pieces = []
        for j in range(tm // CHUNK):
            vc = vg[j * CHUNK:(j + 1) * CHUNK, :].astype(BF16)
            v_bd = jnp.concatenate(
                [jnp.where((lane >= g * gsz) & (lane < (g + 1) * gsz), vc, 0.0)
                 for g in range(N_GROUPS_B)], axis=0)
            pieces.append(_dot(w_cat, v_bd) + bs_ref[...])
        s = jnp.concatenate(pieces, axis=0) if len(pieces) > 1 else pieces[0]
    else:
        s = vg * ws_ref[...] + bs_ref[...]
    o_b = u_ref[0] * s
    y = _mm1(ga_ref[0] * _mm1(o_a, pa_ref[...]) + gb_ref[0] * _mm1(o_b, pb_ref[...]),
             wo_ref[...])
    x1 = _layer_norm(ALPHA * x_ref[0] + g1_ref[0] * y, l1g_ref[...], l1b_ref[...])
    x1_ref[0] = x1
    h2_ref[0] = x1 * (1.0 + sc2_ref[0]) + sh2_ref[0]


def _merge_stage(seq_mode, x, o, r, kp, v, u, vg, ga, gb, g1, sc2, sh2, p):
    b, t, d = x.shape
    tm = 256 if seq_mode else t
    tmod = 1 if seq_mode else tm
    tok = lambda n: pl.BlockSpec((1, tm, n), lambda i, j: (i, j, 0))
    mod = pl.BlockSpec((1, tmod, d), lambda i, j: (i, j if not seq_mode else 0, 0))
    ws, bs = (p["w_spatial"], p["b_spatial_full"]) if seq_mode else (p["ws_row"], p["bs_row"])
    est = 2 * tm * (2 * d + 7 * D_A + 2 * d + 2 * d) * 4 + 8 * tm * d * 4 + 16 * 2**20
    return pl.pallas_call(
        functools.partial(_merge_body, seq_mode, tm),
        grid=(b, t // tm),
        in_specs=[tok(d)] + [tok(D_A)] * 6 + [tok(d), tok(d), mod, mod, mod,
                  _const_spec((1, D_A)), _const_spec((1, D_A)), _const_spec((1, D_A)),
                  _const_spec((D_A, D_A)), _const_spec(ws.shape), _const_spec(bs.shape),
                  _const_spec((D_A, d)), _const_spec((D_B, d)), _const_spec((d, d)),
                  _const_spec((1, d)), _const_spec((1, d))],
        out_specs=[tok(d), tok(d)],
        out_shape=[jax.ShapeDtypeStruct((b, t, d), F32)] * 2,
        compiler_params=_params(("arbitrary", "arbitrary"), est),
        name="merge_seq" if seq_mode else "merge_row",
    )(x, o, r, kp, v, u, vg, ga, gb, g1, sc2, sh2, p["lnx_g"], p["lnx_b"], p["r_k"],
      p["bd"], ws, bs, p["w_branch_a"], p["w_branch_b"], p["w_out"], p["ln1_g"], p["ln1_b"])


def _route_body(h_ref, w_ref, b_ref, gate_ref, idx_ref, wts_ref, cnt_ref):
    lg = _mm3(h_ref[...], w_ref[...]) + b_ref[...]
    lane = lax.broadcasted_iota(jnp.int32, lg.shape, 1)
    lanef = lane.astype(F32)
    neg = -jnp.inf
    is_g = (lane >= N_EXPERTS) & (lane < N_EXPERTS + N_ROUTE_GROUPS)
    mg = jnp.max(jnp.where(is_g, lg, neg), axis=-1, keepdims=True)
    gidx = jnp.min(jnp.where(is_g & (lg == mg), lanef - N_EXPERTS, 1e9), axis=-1, keepdims=True)
    pg_sel = 1.0 / jnp.sum(jnp.where(is_g, jnp.exp(lg - mg), 0.0), axis=-1, keepdims=True)
    lo = gidx * EXP_PER_GROUP
    in_grp = (lanef >= lo) & (lanef < lo + EXP_PER_GROUP)
    t1 = jnp.max(jnp.where(in_grp, lg, neg), axis=-1, keepdims=True)
    i1 = jnp.min(jnp.where(in_grp & (lg == t1), lanef, 1e9), axis=-1, keepdims=True)
    rest = in_grp & (lanef != i1)
    t2 = jnp.max(jnp.where(rest, lg, neg), axis=-1, keepdims=True)
    i2 = jnp.min(jnp.where(rest & (lg == t2), lanef, 1e9), axis=-1, keepdims=True)
    e2 = jnp.exp(t2 - t1)
    w1 = pg_sel / (1.0 + e2)
    w2 = pg_sel * e2 / (1.0 + e2)
    gate_ref[...] = jnp.where(lanef == i1, w1, 0.0) + jnp.where(lanef == i2, w2, 0.0)
    tm = lg.shape[0]
    hit = (lanef == i1) | (lanef == i2)
    earlier = (lax.broadcasted_iota(jnp.int32, (tm, tm), 0)
               > lax.broadcasted_iota(jnp.int32, (tm, tm), 1))
    rank = _dot(earlier.astype(BF16), hit.astype(BF16))
    cnt = jnp.sum(hit.astype(F32), axis=0, keepdims=True)
    cnt8 = jnp.floor((cnt + (MOE_RUN_ALIGN - 1.0)) * (1.0 / MOE_RUN_ALIGN)) * MOE_RUN_ALIGN
    lower_expert = (lax.broadcasted_iota(jnp.int32, (LANES, LANES), 0)
                    < lax.broadcasted_iota(jnp.int32, (LANES, LANES), 1))
    run_start = _masked_rowsum_t(jnp.broadcast_to(cnt8, (8, LANES)), lower_expert)[0:1]
    lpos = run_start + rank
    lp1 = jnp.sum(jnp.where(lanef == i1, lpos, 0.0), axis=-1, keepdims=True)
    lp2 = jnp.sum(jnp.where(lanef == i2, lpos, 0.0), axis=-1, keepdims=True)
    cnt_ref[0] = cnt.astype(jnp.int32)
    idx = jnp.where(lane == 0, i1, jnp.where(lane == 1, i2, jnp.where(lane == 2, lp1, lp2)))
    idx_ref[...] = idx.astype(jnp.int32)
    wts_ref[...] = jnp.where(lane == 0, w1, w2)


def _route(h2, p):
    m, d = h2.shape
    tm = min(m, MOE_TILE)
    tok = pl.BlockSpec((tm, LANES), lambda i: (i, 0))
    return pl.pallas_call(
        _route_body,
        grid=(m // tm,),
        in_specs=[pl.BlockSpec((tm, d), lambda i: (i, 0)),
                  _const_spec((d, LANES)), _const_spec((1, LANES))],
        out_specs=[tok, tok, tok, pl.BlockSpec((1, 1, LANES), lambda i: (i, 0, 0))],
        out_shape=[jax.ShapeDtypeStruct((m, LANES), F32),
                   jax.ShapeDtypeStruct((m, LANES), jnp.int32),
                   jax.ShapeDtypeStruct((m, LANES), F32),
                   jax.ShapeDtypeStruct((m // tm, 1, LANES), jnp.int32)],
        compiler_params=_params(("arbitrary",), 24 * 2**20),
        name="route",
    )(h2, p["w_route"], p["b_route"])


def _moe_body(h_ref, gate_ref, wg_ref, wu_ref, wd_ref, x1_ref, g2_ref, l2g_ref, l2b_ref,
              out_ref, acc_ref, xb_ref):
    e = pl.program_id(1)

    @pl.when(e == 0)
    def _():
        acc_ref[...] = jnp.zeros_like(acc_ref)
        xb_ref[...] = h_ref[...].astype(BF16)

    xb = xb_ref[...]
    pre = _dot(xb, wg_ref[0].astype(BF16))
    hid = pre * jax.nn.sigmoid(pre) * _dot(xb, wu_ref[0].astype(BF16))
    ye = _mm1(hid, wd_ref[0])
    gate = gate_ref[...]
    lane = lax.broadcasted_iota(jnp.int32, gate.shape, 1)
    ge = jnp.sum(jnp.where(lane == e, gate, 0.0), axis=-1, keepdims=True)
    acc_ref[...] += ge * ye

    @pl.when(e == pl.num_programs(1) - 1)
    def _():
        out_ref[...] = _layer_norm(ALPHA * x1_ref[...] + g2_ref[0] * acc_ref[...],
                                   l2g_ref[...], l2b_ref[...])


def _moe(seq_len, h2, gate, x1, g2, p):
    m, d = h2.shape
    layer = p["layer"]
    tm = min(seq_len if g2.shape[1] == 1 else m, 1024)
    tok = lambda n: pl.BlockSpec((tm, n), lambda i, e: (i, 0))
    if g2.shape[1] == 1:
        tiles_per_seq = seq_len // tm
        g2_spec = pl.BlockSpec((1, 1, d), lambda i, e: (i // tiles_per_seq, 0, 0))
    else:
        g2_spec = pl.BlockSpec((1, tm, d), lambda i, e: (0, i, 0))
    est = 2 * tm * (3 * d + LANES) * 4 + tm * d * 6 + 4 * tm * d * 4 + 8 * 2**20
    return pl.pallas_call(
        _moe_body,
        grid=(m // tm, N_EXPERTS),
        in_specs=[tok(d), tok(LANES),
                  pl.BlockSpec((None, 1, d, D_EXPERT), lambda i, e: (layer, e, 0, 0)),
                  pl.BlockSpec((None, 1, d, D_EXPERT), lambda i, e: (layer, e, 0, 0)),
                  pl.BlockSpec((None, 1, D_EXPERT, d), lambda i, e: (layer, e, 0, 0)),
                  tok(d), g2_spec, _const_spec((1, d)), _const_spec((1, d))],
        out_specs=tok(d),
        out_shape=jax.ShapeDtypeStruct((m, d), F32),
        scratch_shapes=[pltpu.VMEM((tm, d), F32), pltpu.VMEM((tm, d), BF16)],
        compiler_params=_params(("arbitrary", "arbitrary"), est),
        name="moe_dense",
    )(h2, gate, p["w_exp_gate"], p["w_exp_up"], p["w_exp_down"], x1, g2, p["ln2_g"], p["ln2_b"])


MOE_TILE = 512
MOE_BLOCK = 512
MOE_RUN_ALIGN = 16
MOE_LOCAL_ROWS = 2 * MOE_TILE + 512
MOE_SLABS = tuple(2 ** k for k in range(9, 3, -1))


def _for_each_slab(run_ref, make_copy, fn):
    def one_run(e, c):
        dst = run_ref[0, 0, e]
        src = run_ref[0, 0, N_EXPERTS + e]
        n = run_ref[0, 0, 2 * N_EXPERTS + e]
        for slab in MOE_SLABS:
            off = n & (-2 * slab)

            @pl.when((n & slab) != 0)
            def _():
                fn(make_copy(pl.multiple_of(dst + off, MOE_RUN_ALIGN),
                             pl.multiple_of(src + off, MOE_RUN_ALIGN), slab))
        return c

    lax.fori_loop(0, N_EXPERTS, one_run, 0)


def _dispatch_body(tail_ref, run_ref, prev_run_ref, h_ref, idx_ref, xs_hbm, zero_buf, loc_buf,
                   sem, zsem, usem):
    nb_max = xs_hbm.shape[0] // MOE_BLOCK
    nt = MOE_TILE

    def zero_copy(row0, zero_sem):
        return pltpu.make_async_copy(
            zero_buf, xs_hbm.at[pl.ds(pl.multiple_of(row0, MOE_BLOCK), MOE_BLOCK)], zero_sem)

    def each_unused_copy(fn):
        def unused(j, c):
            fn(zero_copy(j * MOE_BLOCK, usem))
            return c

        lax.fori_loop(tail_ref[N_EXPERTS], nb_max, unused, 0)

    @pl.when(pl.program_id(0) == 0)
    def _():
        zero_buf[...] = jnp.zeros_like(zero_buf)

        def each_tail_copy(fn):
            for e in range(N_EXPERTS):
                @pl.when(tail_ref[e] >= 0)
                def _():
                    fn(zero_copy(tail_ref[e], zsem))

        each_tail_copy(lambda cp: cp.start())
        each_unused_copy(lambda cp: cp.start())
        each_tail_copy(lambda cp: cp.wait())

    lp = idx_ref[...].astype(F32)
    eye = (lax.broadcasted_iota(jnp.int32, (nt, nt), 0)
           == lax.broadcasted_iota(jnp.int32, (nt, nt), 1))
    as_row = lambda col: jnp.sum(jnp.where(eye, col, 0.0), axis=0, keepdims=True)
    r = lax.broadcasted_iota(jnp.int32, (MOE_LOCAL_ROWS, nt), 0).astype(F32)
    pick = (r == as_row(lp[:, 2:3])) | (r == as_row(lp[:, 3:4]))
    step = pl.program_id(0)
    slot = step & 1
    loc_buf[slot] = _dot(pick.astype(BF16), h_ref[...].astype(BF16)).astype(BF16)

    def copies_from(which):
        def make_copy(dst, src, rows):
            return pltpu.make_async_copy(loc_buf.at[which, pl.ds(src, rows)],
                                         xs_hbm.at[pl.ds(dst, rows)], sem.at[which])
        return make_copy

    _for_each_slab(run_ref, copies_from(slot), lambda cp: cp.start())

    @pl.when(step > 0)
    def _():
        _for_each_slab(prev_run_ref, copies_from(1 - slot), lambda cp: cp.wait())

    @pl.when(step == pl.num_programs(0) - 1)
    def _():
        _for_each_slab(run_ref, copies_from(slot), lambda cp: cp.wait())
        each_unused_copy(lambda cp: cp.wait())


def _dispatch(h2, idx, runs, tail_start, n_rows):
    m, d = h2.shape
    nt = MOE_TILE
    run_spec = lambda at: pl.BlockSpec((1, 1, LANES), lambda i, tail: (at(i), 0, 0),
                                       memory_space=pltpu.SMEM)
    grid_spec = pltpu.PrefetchScalarGridSpec(
        num_scalar_prefetch=1,
        grid=(m // nt,),
        in_specs=[run_spec(lambda i: i), run_spec(lambda i: jnp.maximum(i - 1, 0)),
                  pl.BlockSpec((nt, d), lambda i, tail: (i, 0)),
                  pl.BlockSpec((nt, LANES), lambda i, tail: (i, 0))],
        out_specs=pl.BlockSpec(memory_space=pl.ANY),
        scratch_shapes=[pltpu.VMEM((MOE_BLOCK, d), BF16),
                        pltpu.VMEM((2, MOE_LOCAL_ROWS, d), BF16),
                        pltpu.SemaphoreType.DMA((2,)), pltpu.SemaphoreType.DMA,
                        pltpu.SemaphoreType.DMA],
    )
    return pl.pallas_call(
        _dispatch_body,
        grid_spec=grid_spec,
        out_shape=jax.ShapeDtypeStruct((n_rows, d), BF16),
        compiler_params=_params(("arbitrary",), 40 * 2**20),
        name="moe_dispatch",
    )(tail_start, runs, runs, h2, idx)


GMM_RING = 3


def _gmm_body(be_ref, nb_ref, x_hbm, wg_ref, wu_ref, wd_ref, y_ref, x_ring, sem):
    j = pl.program_id(0)
    n_live = nb_ref[0]

    def fetch(step):
        slot = step % GMM_RING
        return pltpu.make_async_copy(
            x_hbm.at[pl.ds(pl.multiple_of(step * MOE_BLOCK, MOE_BLOCK), MOE_BLOCK)],
            x_ring.at[slot], sem.at[slot])

    @pl.when(j == 0)
    def _():
        for ahead in range(GMM_RING - 1):
            @pl.when(ahead < n_live)
            def _():
                fetch(ahead).start()

    @pl.when(j + (GMM_RING - 1) < n_live)
    def _():
        fetch(j + (GMM_RING - 1)).start()

    @pl.when(j < n_live)
    def _():
        fetch(j).wait()
        xb = x_ring[j % GMM_RING]
        pre = _dot(xb, wg_ref[0].astype(BF16))
        hid = pre * jax.nn.sigmoid(pre) * _dot(xb, wu_ref[0].astype(BF16))
        y_ref[...] = _mm1(hid, wd_ref[0]).astype(BF16)

    @pl.when(j >= nb_ref[0])
    def _():
        y_ref[...] = jnp.zeros_like(y_ref)


def _gmm(xs, blk_expert, n_blocks, p):
    n_rows, d = xs.shape
    nb_max = n_rows // MOE_BLOCK
    layer = p["layer"]
    live = lambda j, be, nb: jnp.minimum(j, nb[0] - 1)
    wspec = lambda shape: pl.BlockSpec((None,) + shape,
                                       lambda j, be, nb: (layer, be[live(j, be, nb)], 0, 0))
    grid_spec = pltpu.PrefetchScalarGridSpec(
        num_scalar_prefetch=2,
        grid=(nb_max,),
        in_specs=[pl.BlockSpec(memory_space=pl.ANY),
                  wspec((1, d, D_EXPERT)), wspec((1, d, D_EXPERT)), wspec((1, D_EXPERT, d))],
        out_specs=pl.BlockSpec((MOE_BLOCK, d), lambda j, be, nb: (j, 0)),
        scratch_shapes=[pltpu.VMEM((GMM_RING, MOE_BLOCK, d), BF16),
                        pltpu.SemaphoreType.DMA((GMM_RING,))],
    )
    return pl.pallas_call(
        _gmm_body,
        grid_spec=grid_spec,
        out_shape=jax.ShapeDtypeStruct((n_rows, d), BF16),
        compiler_params=_params(("arbitrary",), 24 * 2**20),
        name="moe_gmm",
    )(blk_expert, n_blocks, xs, p["w_exp_gate"], p["w_exp_up"], p["w_exp_down"])


def _combine_body(run_ref, next_run_ref, ys_hbm, idx_ref, wts_ref, x1_ref, g2_ref, l2g_ref,
                  l2b_ref, out_ref, loc_buf, sem):
    nt = MOE_TILE
    step = pl.program_id(0)
    slot = step & 1

    def copies_into(which):
        def make_copy(dst, src, rows):
            return pltpu.make_async_copy(ys_hbm.at[pl.ds(dst, rows)],
                                         loc_buf.at[which, pl.ds(src, rows)], sem.at[which])
        return make_copy

    @pl.when(step == 0)
    def _():
        loc_buf[...] = jnp.zeros_like(loc_buf)
        _for_each_slab(run_ref, copies_into(slot), lambda cp: cp.start())

    @pl.when(step < pl.num_programs(0) - 1)
    def _():
        _for_each_slab(next_run_ref, copies_into(1 - slot), lambda cp: cp.start())

    _for_each_slab(run_ref, copies_into(slot), lambda cp: cp.wait())
    lp = idx_ref[...].astype(F32)
    w = wts_ref[...]
    c = lax.broadcasted_iota(jnp.int32, (nt, MOE_LOCAL_ROWS), 1).astype(F32)
    sel = (jnp.where(c == lp[:, 2:3], w[:, 0:1], 0.0)
           + jnp.where(c == lp[:, 3:4], w[:, 1:2], 0.0))
    sel_hi, sel_lo = _split2(sel)
    rows_bf = loc_buf[slot]
    moe = _dot(sel_hi, rows_bf) + _dot(sel_lo, rows_bf)
    out_ref[...] = _layer_norm(ALPHA * x1_ref[...] + g2_ref[0] * moe, l2g_ref[...], l2b_ref[...])


def _combine(seq_len, ys, idx, wts, runs, x1, g2, p):
    m, d = x1.shape
    nt = MOE_TILE
    tiles_per_seq = seq_len // nt
    tok = lambda n: pl.BlockSpec((nt, n), lambda i: (i, 0))
    n_tiles = m // nt
    run_spec = lambda at: pl.BlockSpec((1, 1, LANES), lambda i: (at(i), 0, 0),
                                       memory_space=pltpu.SMEM)
    return pl.pallas_call(
        _combine_body,
        grid=(n_tiles,),
        in_specs=[run_spec(lambda i: i), run_spec(lambda i: jnp.minimum(i + 1, n_tiles - 1)),
                  pl.BlockSpec(memory_space=pl.ANY), tok(LANES), tok(LANES), tok(d),
                  pl.BlockSpec((1, 1, d), lambda i: (i // tiles_per_seq, 0, 0)),
                  _const_spec((1, d)), _const_spec((1, d))],
        out_specs=tok(d),
        out_shape=jax.ShapeDtypeStruct((m, d), F32),
        scratch_shapes=[pltpu.VMEM((2, MOE_LOCAL_ROWS, d), BF16), pltpu.SemaphoreType.DMA((2,))],
        compiler_params=_params(("arbitrary",), 48 * 2**20),
        name="moe_combine",
    )(runs, runs, ys, idx, wts, x1, g2, p["ln2_g"], p["ln2_b"])


def _moe_routed(seq_len, h2, idx, wts, tile_cnt, x1, g2, p):
    m, d = h2.shape
    blk = MOE_BLOCK
    n_tiles = m // MOE_TILE
    worst_rows = 2 * m + n_tiles * N_EXPERTS * (MOE_RUN_ALIGN - 1) + N_EXPERTS * (blk - 1)
    nb_max = -(-worst_rows // blk)
    cnt = tile_cnt[:, 0, :N_EXPERTS]
    run_len = ((cnt + MOE_RUN_ALIGN - 1) // MOE_RUN_ALIGN) * MOE_RUN_ALIGN
    local_row = jnp.cumsum(run_len, axis=1) - run_len
    rows_before = jnp.cumsum(run_len, axis=0) - run_len
    total = jnp.sum(run_len, axis=0)
    padded = ((total + blk - 1) // blk) * blk
    ends = jnp.cumsum(padded)
    starts = ends - padded
    n_blocks = (ends[-1] // blk).astype(jnp.int32).reshape(1)
    first_row = jnp.arange(nb_max, dtype=jnp.int32) * blk
    blk_expert = jnp.minimum(
        jnp.sum((first_row[:, None] >= ends[None, :]).astype(jnp.int32), axis=1),
        N_EXPERTS - 1).astype(jnp.int32)
    tail_start = jnp.concatenate(
        [jnp.where(padded > 0, ends - blk, -1).astype(jnp.int32), n_blocks])
    runs = jnp.concatenate(
        [starts[None, :] + rows_before, local_row, run_len,
         jnp.zeros((n_tiles, LANES - 3 * N_EXPERTS), jnp.int32)], axis=1).astype(jnp.int32)
    runs = runs.reshape(n_tiles, 1, LANES)
    xs = _dispatch(h2, idx, runs, tail_start, nb_max * blk)
    ys = _gmm(xs, blk_expert, n_blocks, p)
    return _combine(seq_len, ys, idx, wts, runs, x1, g2, p)


def _prep_layer(l, w_in, mu_shift, w0, w_decay_up, a0, w_iclr_up, k_k, k_a, r_k,
                lnx_g, lnx_b, lnv_g, lnv_b, w_spatial, b_spatial, w_branch_a, w_branch_b,
                w_out, ln1_g, ln1_b, w_route_group, b_route_group, w_route_expert,
                b_route_expert, w_exp_gate, w_exp_up, w_exp_down, ln2_g, ln2_b):
    d = D_MODEL
    pad_a = N_SHIFT_PAD - N_SHIFT
    wi = w_in[l]
    w_in_p = jnp.concatenate(
        [wi[:, :N_SHIFT], jnp.zeros((d, pad_a), F32), wi[:, N_SHIFT:]], axis=1).astype(BF16)
    mu = jnp.concatenate([mu_shift[l], jnp.zeros((pad_a,), F32)])[None]
    lora = jnp.zeros((LANES, 2 * D_A), F32)
    lora = lora.at[:R_LORA, :D_A].set(w_decay_up[l]).at[R_LORA:2 * R_LORA, D_A:].set(w_iclr_up[l])
    seg = jnp.arange(D_A) // HEAD
    row1 = lambda x: x.reshape(1, -1)
    gsz = D_B // N_GROUPS_B
    w_route = jnp.concatenate(
        [w_route_expert[l], w_route_group[l],
         jnp.zeros((d, LANES - N_EXPERTS - N_ROUTE_GROUPS), F32)], axis=1)
    b_route = jnp.concatenate(
        [b_route_expert[l], b_route_group[l],
         jnp.zeros((LANES - N_EXPERTS - N_ROUTE_GROUPS,), F32)])[None]
    return dict(
        w_in=w_in_p, mu=mu, lora=lora,
        w0a0=jnp.concatenate([w0[l], a0[l]])[None],
        k_k=row1(k_k[l]), k_a=row1(k_a[l]), r_k=row1(r_k[l]),
        lnx_g=row1(lnx_g[l]), lnx_b=row1(lnx_b[l]),
        lnv_g=row1(lnv_g[l]), lnv_b=row1(lnv_b[l]),
        bd=(seg[:, None] == seg[None, :]).astype(BF16),
        w_spatial=w_spatial[l],
        b_spatial_full=jnp.repeat(b_spatial[l].T, gsz, axis=1),
        ws_row=jnp.repeat(w_spatial[l][:, 0, 0], gsz)[None],
        bs_row=jnp.repeat(b_spatial[l][:, 0], gsz)[None],
        w_branch_a=w_branch_a[l].astype(BF16), w_branch_b=w_branch_b[l].astype(BF16),
        w_out=w_out[l].astype(BF16), ln1_g=row1(ln1_g[l]), ln1_b=row1(ln1_b[l]),
        w_route=w_route, b_route=b_route,
        w_exp_gate=w_exp_gate, w_exp_up=w_exp_up, w_exp_down=w_exp_down, layer=l,
        ln2_g=row1(ln2_g[l]), ln2_b=row1(ln2_b[l]),
    )


def _trunk(seq_mode, x, mods, wkv_in, shift_in, preps):
    b, t, d = x.shape
    wkv_out, shift_out, v_out = [], [], []
    wkv_acc = None if seq_mode else jnp.zeros(wkv_in.shape, F32)
    for l in range(DEPTH):
        p = preps[l]
        sh1, sc1, g1, sh2, sc2, g2 = mods[l]
        if seq_mode:
            zprev = jnp.zeros((b, 1, N_SHIFT_PAD), F32) if shift_in is None else shift_in[l]
        else:
            zprev = _matmul(shift_in[l], p["w_in"][:, :N_SHIFT_PAD]).reshape(b, t, N_SHIFT_PAD)
        r, lw, kp, v, kap, bb, u, vg, ga, gb, hl = _in_stage(seq_mode, x, sc1, sh1, zprev, p)
        if seq_mode:
            o, s_new = _wkv_seq(r, lw, kp, v, kap, bb, wkv_in[l])
        else:
            flat = lambda a: a.reshape(t, D_A)
            o, wkv_acc = _wkv_step(flat(r), flat(lw), flat(kp), flat(v), flat(kap), flat(bb),
                                   wkv_in[l], wkv_acc, l)
            s_new = None
            o = o.reshape(b, t, D_A)
        x1, h2 = _merge_stage(seq_mode, x, o, r, kp, v, u, vg, ga, gb, g1, sc2, sh2, p)
        m = b * t
        h2f = h2.reshape(m, d)
        gate, idx, wts, tile_cnt = _route(h2f, p)
        if seq_mode:
            x = _moe_routed(t, h2f, idx, wts, tile_cnt, x1.reshape(m, d), g2, p)
        else:
            x = _moe(t, h2f, gate, x1.reshape(m, d), g2, p)
        x = x.reshape(b, t, d)
        wkv_out.append(s_new)
        shift_out.append(hl)
        v_out.append(vg)
    if not seq_mode:
        wkv_out = wkv_acc
    return x, wkv_out, shift_out, v_out


def kernel(x_prompt, x_sample, c_prompt, c_sample, state_wkv, state_shift, w_ada, b_ada, w_in, mu_shift, w0, w_decay_up, a0, w_iclr_up, k_k, k_a, r_k, lnx_g, lnx_b, lnv_g, lnv_b, w_spatial, b_spatial, w_branch_a, w_branch_b, w_out, ln1_g, ln1_b, w_route_group, b_route_group, w_route_expert, b_route_expert, w_exp_gate, w_exp_up, w_exp_down, ln2_g, ln2_b):
    bp, tp, d = x_prompt.shape
    bs = x_sample.shape[0]
    layer_params = (w_in, mu_shift, w0, w_decay_up, a0, w_iclr_up, k_k, k_a, r_k, lnx_g,
                    lnx_b, lnv_g, lnv_b, w_spatial, b_spatial, w_branch_a, w_branch_b, w_out,
                    ln1_g, ln1_b, w_route_group, b_route_group, w_route_expert,
                    b_route_expert, w_exp_gate, w_exp_up, w_exp_down, ln2_g, ln2_b)
    preps = [_prep_layer(l, *layer_params) for l in range(DEPTH)]
    mod_all = _ada(jnp.concatenate([c_prompt, c_sample], axis=0), w_ada, b_ada)
    mods_p, mods_s = [], []
    for l in range(DEPTH):
        parts = jnp.split(mod_all[l], 6, axis=-1)
        mods_p.append([q[:bp].reshape(bp, 1, d) for q in parts])
        mods_s.append([q[bp:].reshape(1, bs, d) for q in parts])

    wkv0 = jnp.zeros((DEPTH, bp, N_HEADS, HEAD, HEAD), F32)
    y_p, wkv_p, shift_p, _ = _trunk(True, x_prompt, mods_p, wkv0, None, preps)
    y_s, wkv_s, shift_s, v_s = _trunk(False, x_sample.reshape(1, bs, d), mods_s, state_wkv,
                                      state_shift, preps)
    return (y_p,
            y_s.reshape(bs, 1, d),
            jnp.stack(wkv_p),
            jnp.stack([s.reshape(bp, d) for s in shift_p]),
            wkv_s,
            jnp.stack([s.reshape(bs, d) for s in shift_s]),
            jnp.stack([q.reshape(bs, 1, D_B) for q in v_s]))
```

```python
import functools

import jax
import jax.numpy as jnp
from jax import lax
from jax.experimental import pallas as pl
from jax.experimental.pallas import tpu as pltpu

F32 = jnp.float32
BF16 = jnp.bfloat16

D_MODEL = 1024
DEPTH = 2
HEAD = 64
N_HEADS = 8
D_A = N_HEADS * HEAD
R_LORA = 32
CHUNK = 128
N_GROUPS_B = 8
D_B = 512
N_SHIFT = 3 * D_A + 2 * R_LORA
N_ROUTE_GROUPS = 4
EXP_PER_GROUP = 8
N_EXPERTS = N_ROUTE_GROUPS * EXP_PER_GROUP
D_EXPERT = 256
ALPHA = (2 * DEPTH) ** 0.25
LN_EPS = 1e-5
GN_EPS = 64e-5

LANES = 128
N_SHIFT_PAD = 13 * LANES
COL_U = N_SHIFT_PAD
COL_VG = COL_U + D_B
COL_GA = COL_VG + D_B
COL_GB = COL_GA + D_MODEL
N_IN_PAD = COL_GB + D_MODEL
WKV_CHUNK = 64
IN_ROW_GROUPS = 4
IN_GROUP_ROWS = 128
VMEM_CAP_BYTES = 60000 * 1024

_NN = (((1,), (0,)), ((), ()))
_B_NT = (((2,), (2,)), ((0,), (0,)))
_B_NN = (((2,), (1,)), ((0,), (0,)))
_B_TN = (((1,), (1,)), ((0,), (0,)))


def _dot(a, b, dims=_NN):
    return lax.dot_general(a, b, dims, preferred_element_type=F32)


def _split2(x):
    hi = x.astype(BF16)
    lo = (x - hi.astype(F32)).astype(BF16)
    return hi, lo


def _mm1(a, b, dims=_NN):
    return _dot(a.astype(BF16), b.astype(BF16), dims)


def _mm3(a, b, dims=_NN):
    ah, al = _split2(a)
    bh, bl = _split2(b)
    return _dot(ah, bh, dims) + (_dot(ah, bl, dims) + _dot(al, bh, dims))


def _segsum(x, bd):
    hi, lo = _split2(x)
    return _dot(hi, bd) + _dot(lo, bd)


def _layer_norm(x, g, b):
    mu = jnp.mean(x, axis=-1, keepdims=True)
    d = x - mu
    var = jnp.mean(d * d, axis=-1, keepdims=True)
    return d * lax.rsqrt(var + LN_EPS) * g + b


def _gelu(x):
    return 0.5 * x * (1.0 + lax.erf(x * 0.7071067811865476))


def _params(sem, est_bytes):
    limit = int(min(VMEM_CAP_BYTES, max(est_bytes, 16 * 1024 * 1024)))
    return pltpu.CompilerParams(dimension_semantics=sem, vmem_limit_bytes=limit)


def _const_spec(shape, single_buffer=False):
    nd = len(shape)
    if single_buffer:
        return pl.BlockSpec(shape, lambda *_: (0,) * nd, pipeline_mode=pl.Buffered(1))
    return pl.BlockSpec(shape, lambda *_: (0,) * nd)


def _ada_body(c_ref, w_ref, b_ref, o_ref):
    c = c_ref[...]
    s = c * jax.nn.sigmoid(c)
    o_ref[0] = _mm3(s, w_ref[0]) + b_ref[0]


def _ada(c_all, w_ada, b_ada):
    depth, d, n6 = w_ada.shape
    m = c_all.shape[0]
    tn = 1024
    return pl.pallas_call(
        _ada_body,
        grid=(depth, n6 // tn),
        in_specs=[
            pl.BlockSpec((m, d), lambda l, j: (0, 0)),
            pl.BlockSpec((1, d, tn), lambda l, j: (l, 0, j)),
            pl.BlockSpec((1, 1, tn), lambda l, j: (l, 0, j)),
        ],
        out_specs=pl.BlockSpec((1, m, tn), lambda l, j: (l, 0, j)),
        out_shape=jax.ShapeDtypeStruct((depth, m, n6), F32),
        compiler_params=_params(("arbitrary", "arbitrary"), 24 * 2**20),
        name="ada_mod",
    )(c_all, w_ada, b_ada.reshape(depth, 1, n6))


def _mm_body(x_ref, w_ref, o_ref):
    o_ref[...] = _mm1(x_ref[...], w_ref[...])


def _matmul(x, w):
    m, k = x.shape
    n = w.shape[1]
    return pl.pallas_call(
        _mm_body,
        grid=(1,),
        in_specs=[_const_spec((m, k)), _const_spec((k, n))],
        out_specs=_const_spec((m, n)),
        out_shape=jax.ShapeDtypeStruct((m, n), F32),
        compiler_params=_params(("arbitrary",), 24 * 2**20),
        name="shift_proj",
    )(x, w)


def _in_body(seq_mode, tm, x_ref, sc_ref, sh_ref, w_ref, mu_ref, zp_ref, lora_ref,
             w0a0_ref, kk_ref, ka_ref, lng_ref, lnb_ref, bd_ref,
             r_ref, lw_ref, kp_ref, v_ref, kap_ref, bb_ref, u_ref, vg_ref,
             ga_ref, gb_ref, hl_ref, carry_ref):
    nsplit = IN_ROW_GROUPS if seq_mode else 1
    rows = tm // nsplit
    if seq_mode:
        @pl.when(pl.program_id(1) == 0)
        def _():
            carry_ref[...] = zp_ref[0]

        carry = carry_ref[...]
    bd = bd_ref[...]
    for part in range(nsplit):
        sl = slice(part * rows, (part + 1) * rows)
        h = x_ref[0, sl, :] * (1.0 + sc_ref[0]) + sh_ref[0]
        hb = h.astype(BF16)
        proj = lambda lo, hi: _dot(hb, w_ref[:, lo:hi])
        za = proj(0, N_SHIFT_PAD)
        if seq_mode:
            row = lax.broadcasted_iota(jnp.int32, za.shape, 0)
            prev = jnp.where(row == 0, carry, pltpu.roll(za, 1, 0))
            carry = za[rows - 1:rows, :]
        else:
            prev = zp_ref[0]
        mix = za + mu_ref[...] * (prev - za)
        r = mix[:, 0:D_A]
        k = mix[:, D_A:2 * D_A]
        v = mix[:, 2 * D_A:3 * D_A]
        xwa = mix[:, 3 * D_A:N_SHIFT_PAD]
        lane = lax.broadcasted_iota(jnp.int32, xwa.shape, 1)
        lora_in = jnp.where(lane < R_LORA, jnp.tanh(xwa), xwa)
        pre = w0a0_ref[...] + _mm1(lora_in, lora_ref[...])
        yw = -pre[:, :D_A]
        softplus = jnp.maximum(yw, 0.0) + jnp.log1p(jnp.exp(-jnp.abs(yw)))
        lw = -jnp.exp(-softplus - 0.5)
        a = jax.nn.sigmoid(pre[:, D_A:])
        kk = k * kk_ref[...]
        kap = kk / jnp.maximum(jnp.sqrt(_segsum(kk * kk, bd)), 1e-12)
        r_ref[0, sl, :] = r
        lw_ref[0, sl, :] = lw
        kp_ref[0, sl, :] = k * (1.0 + (a - 1.0) * ka_ref[...])
        v_ref[0, sl, :] = v
        kap_ref[0, sl, :] = kap
        bb_ref[0, sl, :] = kap * a
        u_ref[0, sl, :] = _gelu(proj(COL_U, COL_VG)).astype(BF16)
        vg_ref[0, sl, :] = _layer_norm(_gelu(proj(COL_VG, COL_GA)), lng_ref[...], lnb_ref[...])
        ga_ref[0, sl, :] = jax.nn.sigmoid(proj(COL_GA, COL_GB)).astype(BF16)
        gb_ref[0, sl, :] = jax.nn.sigmoid(proj(COL_GB, N_IN_PAD)).astype(BF16)
    if seq_mode:
        carry_ref[...] = carry
        hl_ref[0] = h[rows - 1:rows, :]
    else:
        hl_ref[0] = h


def _in_stage(seq_mode, x, sc, sh, zprev, p):
    b, t, d = x.shape
    tm = IN_ROW_GROUPS * IN_GROUP_ROWS if seq_mode else t
    tmod = 1 if seq_mode else tm
    grid = (b, t // tm)
    tok = lambda n: pl.BlockSpec((1, tm, n), lambda i, j: (i, j, 0))
    mod = pl.BlockSpec((1, tmod, d), lambda i, j: (i, j if not seq_mode else 0, 0))
    zp_spec = (pl.BlockSpec((1, 1, N_SHIFT_PAD), lambda i, j: (i, 0, 0)) if seq_mode
               else tok(N_SHIFT_PAD))
    hl_spec = (pl.BlockSpec((1, 1, d), lambda i, j: (i, 0, 0)) if seq_mode else tok(d))
    hl_shape = (b, 1, d) if seq_mode else (b, t, d)
    out_cols = [D_A] * 6 + [D_B] * 2 + [d] * 2
    out_dtypes = [F32] * 6 + [BF16, F32] + [BF16] * 2
    est = (2 * tm * (d + N_SHIFT_PAD + sum(out_cols) + d) * 4 + 2 * d * N_IN_PAD * 2
           + 3 * tm * N_IN_PAD * 4 + 4 * 2**20)
    outs = pl.pallas_call(
        functools.partial(_in_body, seq_mode, tm),
        grid=grid,
        in_specs=[tok(d), mod, mod,
                  _const_spec((d, N_IN_PAD), True), _const_spec((1, N_SHIFT_PAD)), zp_spec,
                  _const_spec((LANES, 2 * D_A)), _const_spec((1, 2 * D_A)),
                  _const_spec((1, D_A)), _const_spec((1, D_A)),
                  _const_spec((1, D_B)), _const_spec((1, D_B)),
                  _const_spec((D_A, D_A))],
        out_specs=[tok(n) for n in out_cols] + [hl_spec],
        out_shape=[jax.ShapeDtypeStruct((b, t, n), dt) for n, dt in zip(out_cols, out_dtypes)]
        + [jax.ShapeDtypeStruct(hl_shape, F32)],
        scratch_shapes=[pltpu.VMEM((1, N_SHIFT_PAD), F32)],
        compiler_params=_params(("arbitrary", "arbitrary"), est),
        name="in_stage_seq" if seq_mode else "in_stage_row",
    )(x, sc, sh, p["w_in"], p["mu"], zprev, p["lora"], p["w0a0"], p["k_k"], p["k_a"],
      p["lnv_g"], p["lnv_b"], p["bd"])
    return outs


_WKV_3PASS = frozenset({"h"})


def _wmm(name, a, b, dims):
    return (_mm3 if name in _WKV_3PASS else _mm1)(a, b, dims)


def _chunk_pairs(x, nsub):
    c = WKV_CHUNK
    w = 2 * HEAD
    return jnp.stack([x[c * i:c * (i + 1), w * q:w * (q + 1)]
                      for i in range(nsub) for q in range(N_HEADS // 2)], axis=0)


def _masked_rowsum(mask_bf, x):
    h1, h2 = _split2(x)
    return _dot(mask_bf, h1) + _dot(mask_bf, h2)


def _masked_rowsum_t(x, mask):
    h1, h2 = _split2(x)
    m = mask.astype(BF16)
    return _dot(h1, m) + _dot(h2, m)


def _wkv_seq_body(nsub, r_ref, lw_ref, kp_ref, v_ref, kap_ref, bb_ref, s0_ref,
                  o_ref, sout_ref, s_scr):
    c = WKV_CHUNK
    tc = nsub * c
    npair = N_HEADS // 2

    @pl.when(pl.program_id(1) == 0)
    def _():
        for q in range(npair):
            s_scr[q] = jnp.concatenate([s0_ref[0, 2 * q], s0_ref[0, 2 * q + 1]], axis=-1)

    row_t = lax.broadcasted_iota(jnp.int32, (tc, tc), 0)
    col_t = lax.broadcasted_iota(jnp.int32, (tc, tc), 1)
    shift = c.bit_length() - 1
    same_chunk = (row_t >> shift) == (col_t >> shift)
    lw = lw_ref[0]
    g = _masked_rowsum((same_chunk & (row_t >= col_t)).astype(BF16), lw)
    g_end = _masked_rowsum(same_chunk.astype(BF16), lw)
    e_neg = jnp.exp(-g)
    e_end = jnp.exp(g_end - g)
    cp = functools.partial(_chunk_pairs, nsub=nsub)
    k = kp_ref[0]
    b = bb_ref[0]
    kap_t = cp(kap_ref[0] * jnp.exp(g - lw))
    b_t = cp(b * e_neg)
    k_t = cp(k * e_neg)
    r_t = cp(r_ref[0] * jnp.exp(g))
    b_e = cp(b * e_end)
    k_e = cp(k * e_end)
    vv = cp(v_ref[0])
    decay_end = cp(jnp.exp(g_end))

    row = lax.broadcasted_iota(jnp.int32, (c, 2 * c), 0)[None]
    lane = lax.broadcasted_iota(jnp.int32, (c, 2 * c), 1)[None]
    colp = lane & (c - 1)
    right = lane >= c
    row2 = lax.broadcasted_iota(jnp.int32, (2 * c, 2 * c), 0)[None]
    lane2 = lax.broadcasted_iota(jnp.int32, (2 * c, 2 * c), 1)[None]
    same_head = (row2 >= c) == (lane2 >= c)

    def bd(x):
        return jnp.concatenate([jnp.where(right, 0.0, x), jnp.where(right, x, 0.0)], axis=1)

    p_b = _wmm("lb", jnp.concatenate([kap_t, r_t], axis=1), bd(b_t), _B_NT)
    l_b = jnp.where(row > colp, p_b[:, :c], 0.0)
    a_rb = jnp.where(row >= colp, p_b[:, c:], 0.0)
    bd_k = bd(k_t)
    l_k = jnp.where(row > colp, _wmm("lk", kap_t, bd_k, _B_NT), 0.0)
    a_rk = jnp.where(row >= colp, _wmm("ark", r_t, bd_k, _B_NT), 0.0)
    m = -l_b
    t_inv = jnp.where(row == colp, 1.0, 0.0) + m
    m = _wmm("inv", m, bd(m), _B_NN)
    span = 2
    while 2 * span < c:
        both = _wmm("inv", jnp.concatenate([m, t_inv], axis=1), bd(m), _B_NN)
        m = both[:, :c]
        t_inv = t_inv + both[:, c:]
        span *= 2
    t_inv = t_inv + _wmm("inv", t_inv, bd(m), _B_NN)
    bd_v = bd(vv)
    a1 = -_wmm("ta", t_inv, bd(kap_t), _B_NN)
    u0 = -_wmm("tu", t_inv, bd(_wmm("lkv", l_k, bd_v, _B_NN)), _B_NN)
    a2 = r_t + _wmm("arba", a_rb, bd(a1), _B_NN)
    o0 = _wmm("arbu", a_rb, bd(u0), _B_NN) + _wmm("arkv", a_rk, bd_v, _B_NN)
    g_bd = (jnp.where(same_head, _wmm("g", a1, b_e, _B_TN), 0.0)
            + jnp.where(row2 == lane2, decay_end[:, 0:1, :], 0.0))
    hh = _wmm("h", jnp.concatenate([u0, vv], axis=1),
              jnp.concatenate([b_e, k_e], axis=1), _B_TN)
    h_pair = jnp.where(right, hh[:, c:], hh[:, :c])

    s = s_scr[...]
    for i in range(nsub):
        ps = slice(npair * i, npair * (i + 1))
        o = _wmm("o", a2[ps], bd(s), _B_NT) + o0[ps]
        s = _wmm("s", s, g_bd[ps], _B_NN) + h_pair[ps]
        for q in range(npair):
            o_ref[0, c * i:c * (i + 1), 2 * HEAD * q:2 * HEAD * (q + 1)] = o[q]
    s_scr[...] = s

    @pl.when(pl.program_id(1) == pl.num_programs(1) - 1)
    def _():
        for q in range(npair):
            sout_ref[0, 2 * q] = s[q][:, :HEAD]
            sout_ref[0, 2 * q + 1] = s[q][:, HEAD:]


def _wkv_seq(r, lw, kp, v, kap, bb, s0):
    b, t, _ = r.shape
    tc = 256
    nsub = tc // WKV_CHUNK
    tok = pl.BlockSpec((1, tc, D_A), lambda i, j: (i, j, 0))
    st = pl.BlockSpec((1, N_HEADS, HEAD, HEAD), lambda i, j: (i, 0, 0, 0))
    return pl.pallas_call(
        functools.partial(_wkv_seq_body, nsub),
        grid=(b, t // tc),
        in_specs=[tok] * 6 + [st],
        out_specs=[tok, st],
        out_shape=[jax.ShapeDtypeStruct((b, t, D_A), F32),
                   jax.ShapeDtypeStruct((b, N_HEADS, HEAD, HEAD), F32)],
        scratch_shapes=[pltpu.VMEM((N_HEADS // 2, HEAD, 2 * HEAD), F32)],
        compiler_params=_params(("arbitrary", "arbitrary"), 32 * 2**20),
        name="wkv_seq",
    )(r, lw, kp, v, kap, bb, s0)


def _wkv_step_body(r_ref, lw_ref, kp_ref, v_ref, kap_ref, bb_ref, s0_ref, _, o_ref, sout_ref):
    eye = (lax.broadcasted_iota(jnp.int32, (HEAD, HEAD), 0)
           == lax.broadcasted_iota(jnp.int32, (HEAD, HEAD), 1))[None]
    for h in range(N_HEADS):
        hs = slice(HEAD * h, HEAD * (h + 1))
        row = lambda ref: ref[:, hs][:, None, :]
        s = s0_ref[:, h]
        sa = -jnp.sum(s * row(kap_ref), axis=-1, keepdims=True)
        v_col = jnp.sum(jnp.where(eye, row(v_ref), 0.0), axis=-1, keepdims=True)
        s_new = s * jnp.exp(row(lw_ref)) + sa * row(bb_ref) + v_col * row(kp_ref)
        sout_ref[:, h] = s_new
        o_col = jnp.sum(s_new * row(r_ref), axis=-1, keepdims=True)
        o_ref[:, hs] = jnp.sum(jnp.where(eye, o_col, 0.0), axis=1)


def _wkv_step(r, lw, kp, v, kap, bb, s0, s_new_all, layer):
    m = r.shape[0]
    nb = 8
    tok = pl.BlockSpec((nb, D_A), lambda i: (i, 0))
    st = pl.BlockSpec((nb, N_HEADS, HEAD, HEAD), lambda i: (i, 0, 0, 0))
    st_out = pl.BlockSpec((None, nb, N_HEADS, HEAD, HEAD), lambda i: (layer, i, 0, 0, 0))
    return pl.pallas_call(
        _wkv_step_body,
        grid=(m // nb,),
        in_specs=[tok] * 6 + [st, pl.BlockSpec(memory_space=pl.ANY)],
        out_specs=[tok, st_out],
        out_shape=[jax.ShapeDtypeStruct((m, D_A), F32),
                   jax.ShapeDtypeStruct(s_new_all.shape, F32)],
        input_output_aliases={7: 1},
        compiler_params=_params(("arbitrary",), 24 * 2**20),
        name="wkv_step",
    )(r, lw, kp, v, kap, bb, s0, s_new_all)


def _merge_body(seq_mode, tm, x_ref, o_ref, r_ref, kp_ref, v_ref, u_ref, vg_ref,
                ga_ref, gb_ref, g1_ref, sc2_ref, sh2_ref, lxg_ref, lxb_ref, rk_ref,
                bd_ref, ws_ref, bs_ref, pa_ref, pb_ref, wo_ref, l1g_ref, l1b_ref,
                x1_ref, h2_ref):
    bd = bd_ref[...]
    o = o_ref[0]
    inv_n = 1.0 / HEAD
    mu = _segsum(o, bd) * inv_n
    d = o - mu
    var = _segsum(d * d, bd) * inv_n
    on = d * lax.rsqrt(var + GN_EPS) * lxg_ref[...] + lxb_ref[...]
    v = v_ref[0]
    o_a = on + _segsum(r_ref[0] * kp_ref[0] * rk_ref[...], bd) * v
    vg = vg_ref[0]
    if seq_mode:
        row = lax.broadcasted_iota(jnp.int32, (CHUNK, CHUNK), 0)
        col = lax.broadcasted_iota(jnp.int32, (CHUNK, CHUNK), 1)
        lane = lax.broadcasted_iota(jnp.int32, (CHUNK, D_B), 1)
        gsz = D_B // N_GROUPS_B
        w_cat = jnp.concatenate(
            [jnp.where(row >= col, ws_ref[g], 0.0).astype(BF16) for g in range(N_GROUPS_B)], axis=1)
        pieces = []
        for j in range(tm // CHUNK):
            vc = vg[j * CHUNK:(j + 1) * CHUNK, :].astype(BF16)
            v_bd = jnp.concatenate(
                [jnp.where((lane >= g * gsz) & (lane < (g + 1) * gsz), vc, 0.0)
                 for g in range(N_GROUPS_B)], axis=0)
            pieces.append(_dot(w_cat, v_bd) + bs_ref[...])
        s = jnp.concatenate(pieces, axis=0) if len(pieces) > 1 else pieces[0]
    else:
        s = vg * ws_ref[...] + bs_ref[...]
    o_b = u_ref[0] * s
    y = _mm1(ga_ref[0] * _mm1(o_a, pa_ref[...]) + gb_ref[0] * _mm1(o_b, pb_ref[...]),
             wo_ref[...])
    x1 = _layer_norm(ALPHA * x_ref[0] + g1_ref[0] * y, l1g_ref[...], l1b_ref[...])
    x1_ref[0] = x1
    h2_ref[0] = x1 * (1.0 + sc2_ref[0]) + sh2_ref[0]


def _merge_stage(seq_mode, x, o, r, kp, v, u, vg, ga, gb, g1, sc2, sh2, p):
    b, t, d = x.shape
    tm = 256 if seq_mode else t
    tmod = 1 if seq_mode else tm
    tok = lambda n: pl.BlockSpec((1, tm, n), lambda i, j: (i, j, 0))
    mod = pl.BlockSpec((1, tmod, d), lambda i, j: (i, j if not seq_mode else 0, 0))
    ws, bs = (p["w_spatial"], p["b_spatial_full"]) if seq_mode else (p["ws_row"], p["bs_row"])
    est = 2 * tm * (2 * d + 7 * D_A + 2 * d + 2 * d) * 4 + 8 * tm * d * 4 + 16 * 2**20
    return pl.pallas_call(
        functools.partial(_merge_body, seq_mode, tm),
        grid=(b, t // tm),
        in_specs=[tok(d)] + [tok(D_A)] * 6 + [tok(d), tok(d), mod, mod, mod,
                  _const_spec((1, D_A)), _const_spec((1, D_A)), _const_spec((1, D_A)),
                  _const_spec((D_A, D_A)), _const_spec(ws.shape), _const_spec(bs.shape),
                  _const_spec((D_A, d)), _const_spec((D_B, d)), _const_spec((d, d)),
                  _const_spec((1, d)), _const_spec((1, d))],
        out_specs=[tok(d), tok(d)],
        out_shape=[jax.ShapeDtypeStruct((b, t, d), F32)] * 2,
        compiler_params=_params(("arbitrary", "arbitrary"), est),
        name="merge_seq" if seq_mode else "merge_row",
    )(x, o, r, kp, v, u, vg, ga, gb, g1, sc2, sh2, p["lnx_g"], p["lnx_b"], p["r_k"],
      p["bd"], ws, bs, p["w_branch_a"], p["w_branch_b"], p["w_out"], p["ln1_g"], p["ln1_b"])


def _route_body(h_ref, w_ref, b_ref, gate_ref, idx_ref, wts_ref, cnt_ref):
    lg = _mm3(h_ref[...], w_ref[...]) + b_ref[...]
    lane = lax.broadcasted_iota(jnp.int32, lg.shape, 1)
    lanef = lane.astype(F32)
    neg = -jnp.inf
    is_g = (lane >= N_EXPERTS) & (lane < N_EXPERTS + N_ROUTE_GROUPS)
    mg = jnp.max(jnp.where(is_g, lg, neg), axis=-1, keepdims=True)
    gidx = jnp.min(jnp.where(is_g & (lg == mg), lanef - N_EXPERTS, 1e9), axis=-1, keepdims=True)
    pg_sel = 1.0 / jnp.sum(jnp.where(is_g, jnp.exp(lg - mg), 0.0), axis=-1, keepdims=True)
    lo = gidx * EXP_PER_GROUP
    in_grp = (lanef >= lo) & (lanef < lo + EXP_PER_GROUP)
    t1 = jnp.max(jnp.where(in_grp, lg, neg), axis=-1, keepdims=True)
    i1 = jnp.min(jnp.where(in_grp & (lg == t1), lanef, 1e9), axis=-1, keepdims=True)
    rest = in_grp & (lanef != i1)
    t2 = jnp.max(jnp.where(rest, lg, neg), axis=-1, keepdims=True)
    i2 = jnp.min(jnp.where(rest & (lg == t2), lanef, 1e9), axis=-1, keepdims=True)
    e2 = jnp.exp(t2 - t1)
    w1 = pg_sel / (1.0 + e2)
    w2 = pg_sel * e2 / (1.0 + e2)
    gate_ref[...] = jnp.where(lanef == i1, w1, 0.0) + jnp.where(lanef == i2, w2, 0.0)
    tm = lg.shape[0]
    hit = (lanef == i1) | (lanef == i2)
    earlier = (lax.broadcasted_iota(jnp.int32, (tm, tm), 0)
               > lax.broadcasted_iota(jnp.int32, (tm, tm), 1))
    rank = _dot(earlier.astype(BF16), hit.astype(BF16))
    cnt = jnp.sum(hit.astype(F32), axis=0, keepdims=True)
    cnt8 = jnp.floor((cnt + (MOE_RUN_ALIGN - 1.0)) * (1.0 / MOE_RUN_ALIGN)) * MOE_RUN_ALIGN
    lower_expert = (lax.broadcasted_iota(jnp.int32, (LANES, LANES), 0)
                    < lax.broadcasted_iota(jnp.int32, (LANES, LANES), 1))
    run_start = _masked_rowsum_t(jnp.broadcast_to(cnt8, (8, LANES)), lower_expert)[0:1]
    lpos = run_start + rank
    lp1 = jnp.sum(jnp.where(lanef == i1, lpos, 0.0), axis=-1, keepdims=True)
    lp2 = jnp.sum(jnp.where(lanef == i2, lpos, 0.0), axis=-1, keepdims=True)
    cnt_ref[0] = cnt.astype(jnp.int32)
    idx = jnp.where(lane == 0, i1, jnp.where(lane == 1, i2, jnp.where(lane == 2, lp1, lp2)))
    idx_ref[...] = idx.astype(jnp.int32)
    wts_ref[...] = jnp.where(lane == 0, w1, w2)


def _route(h2, p):
    m, d = h2.shape
    tm = min(m, MOE_TILE)
    tok = pl.BlockSpec((tm, LANES), lambda i: (i, 0))
    return pl.pallas_call(
        _route_body,
        grid=(m // tm,),
        in_specs=[pl.BlockSpec((tm, d), lambda i: (i, 0)),
                  _const_spec((d, LANES)), _const_spec((1, LANES))],
        out_specs=[tok, tok, tok, pl.BlockSpec((1, 1, LANES), lambda i: (i, 0, 0))],
        out_shape=[jax.ShapeDtypeStruct((m, LANES), F32),
                   jax.ShapeDtypeStruct((m, LANES), jnp.int32),
                   jax.ShapeDtypeStruct((m, LANES), F32),
                   jax.ShapeDtypeStruct((m // tm, 1, LANES), jnp.int32)],
        compiler_params=_params(("arbitrary",), 24 * 2**20),
        name="route",
    )(h2, p["w_route"], p["b_route"])


def _moe_body(h_ref, gate_ref, wg_ref, wu_ref, wd_ref, x1_ref, g2_ref, l2g_ref, l2b_ref,
              out_ref, acc_ref, xb_ref):
    e = pl.program_id(1)

    @pl.when(e == 0)
    def _():
        acc_ref[...] = jnp.zeros_like(acc_ref)
        xb_ref[...] = h_ref[...].astype(BF16)

    xb = xb_ref[...]
    pre = _dot(xb, wg_ref[0].astype(BF16))
    hid = pre * jax.nn.sigmoid(pre) * _dot(xb, wu_ref[0].astype(BF16))
    ye = _mm1(hid, wd_ref[0])
    gate = gate_ref[...]
    lane = lax.broadcasted_iota(jnp.int32, gate.shape, 1)
    ge = jnp.sum(jnp.where(lane == e, gate, 0.0), axis=-1, keepdims=True)
    acc_ref[...] += ge * ye

    @pl.when(e == pl.num_programs(1) - 1)
    def _():
        out_ref[...] = _layer_norm(ALPHA * x1_ref[...] + g2_ref[0] * acc_ref[...],
                                   l2g_ref[...], l2b_ref[...])


def _moe(seq_len, h2, gate, x1, g2, p):
    m, d = h2.shape
    layer = p["layer"]
    tm = min(seq_len if g2.shape[1] == 1 else m, 1024)
    tok = lambda n: pl.BlockSpec((tm, n), lambda i, e: (i, 0))
    if g2.shape[1] == 1:
        tiles_per_seq = seq_len // tm
        g2_spec = pl.BlockSpec((1, 1, d), lambda i, e: (i // tiles_per_seq, 0, 0))
    else:
        g2_spec = pl.BlockSpec((1, tm, d), lambda i, e: (0, i, 0))
    est = 2 * tm * (3 * d + LANES) * 4 + tm * d * 6 + 4 * tm * d * 4 + 8 * 2**20
    return pl.pallas_call(
        _moe_body,
        grid=(m // tm, N_EXPERTS),
        in_specs=[tok(d), tok(LANES),
                  pl.BlockSpec((None, 1, d, D_EXPERT), lambda i, e: (layer, e, 0, 0)),
                  pl.BlockSpec((None, 1, d, D_EXPERT), lambda i, e: (layer, e, 0, 0)),
                  pl.BlockSpec((None, 1, D_EXPERT, d), lambda i, e: (layer, e, 0, 0)),
                  tok(d), g2_spec, _const_spec((1, d)), _const_spec((1, d))],
        out_specs=tok(d),
        out_shape=jax.ShapeDtypeStruct((m, d), F32),
        scratch_shapes=[pltpu.VMEM((tm, d), F32), pltpu.VMEM((tm, d), BF16)],
        compiler_params=_params(("arbitrary", "arbitrary"), est),
        name="moe_dense",
    )(h2, gate, p["w_exp_gate"], p["w_exp_up"], p["w_exp_down"], x1, g2, p["ln2_g"], p["ln2_b"])


MOE_TILE = 512
MOE_BLOCK = 512
MOE_RUN_ALIGN = 16
MOE_LOCAL_ROWS = 2 * MOE_TILE + 512
MOE_SLABS = tuple(2 ** k for k in range(9, 3, -1))


def _for_each_slab(run_ref, make_copy, fn):
    def one_run(e, c):
        dst = run_ref[0, 0, e]
        src = run_ref[0, 0, N_EXPERTS + e]
        n = run_ref[0, 0, 2 * N_EXPERTS + e]
        for slab in MOE_SLABS:
            off = n & (-2 * slab)

            @pl.when((n & slab) != 0)
            def _():
                fn(make_copy(pl.multiple_of(dst + off, MOE_RUN_ALIGN),
                             pl.multiple_of(src + off, MOE_RUN_ALIGN), slab))
        return c

    lax.fori_loop(0, N_EXPERTS, one_run, 0)


def _dispatch_body(tail_ref, run_ref, prev_run_ref, h_ref, idx_ref, xs_hbm, zero_buf, loc_buf,
                   sem, zsem, usem):
    nb_max = xs_hbm.shape[0] // MOE_BLOCK
    nt = MOE_TILE

    def zero_copy(row0, zero_sem):
        return pltpu.make_async_copy(
            zero_buf, xs_hbm.at[pl.ds(pl.multiple_of(row0, MOE_BLOCK), MOE_BLOCK)], zero_sem)

    def each_unused_copy(fn):
        def unused(j, c):
            fn(zero_copy(j * MOE_BLOCK, usem))
            return c

        lax.fori_loop(tail_ref[N_EXPERTS], nb_max, unused, 0)

    @pl.when(pl.program_id(0) == 0)
    def _():
        zero_buf[...] = jnp.zeros_like(zero_buf)

        def each_tail_copy(fn):
            for e in range(N_EXPERTS):
                @pl.when(tail_ref[e] >= 0)
                def _():
                    fn(zero_copy(tail_ref[e], zsem))

        each_tail_copy(lambda cp: cp.start())
        each_unused_copy(lambda cp: cp.start())
        each_tail_copy(lambda cp: cp.wait())

    lp = idx_ref[...].astype(F32)
    eye = (lax.broadcasted_iota(jnp.int32, (nt, nt), 0)
           == lax.broadcasted_iota(jnp.int32, (nt, nt), 1))
    as_row = lambda col: jnp.sum(jnp.where(eye, col, 0.0), axis=0, keepdims=True)
    r = lax.broadcasted_iota(jnp.int32, (MOE_LOCAL_ROWS, nt), 0).astype(F32)
    pick = (r == as_row(lp[:, 2:3])) | (r == as_row(lp[:, 3:4]))
    step = pl.program_id(0)
    slot = step & 1
    loc_buf[slot] = _dot(pick.astype(BF16), h_ref[...].astype(BF16)).astype(BF16)

    def copies_from(which):
        def make_copy(dst, src, rows):
            return pltpu.make_async_copy(loc_buf.at[which, pl.ds(src, rows)],
                                         xs_hbm.at[pl.ds(dst, rows)], sem.at[which])
        return make_copy

    _for_each_slab(run_ref, copies_from(slot), lambda cp: cp.start())

    @pl.when(step > 0)
    def _():
        _for_each_slab(prev_run_ref, copies_from(1 - slot), lambda cp: cp.wait())

    @pl.when(step == pl.num_programs(0) - 1)
    def _():
        _for_each_slab(run_ref, copies_from(slot), lambda cp: cp.wait())
        each_unused_copy(lambda cp: cp.wait())


def _dispatch(h2, idx, runs, tail_start, n_rows):
    m, d = h2.shape
    nt = MOE_TILE
    run_spec = lambda at: pl.BlockSpec((1, 1, LANES), lambda i, tail: (at(i), 0, 0),
                                       memory_space=pltpu.SMEM)
    grid_spec = pltpu.PrefetchScalarGridSpec(
        num_scalar_prefetch=1,
        grid=(m // nt,),
        in_specs=[run_spec(lambda i: i), run_spec(lambda i: jnp.maximum(i - 1, 0)),
                  pl.BlockSpec((nt, d), lambda i, tail: (i, 0)),
                  pl.BlockSpec((nt, LANES), lambda i, tail: (i, 0))],
        out_specs=pl.BlockSpec(memory_space=pl.ANY),
        scratch_shapes=[pltpu.VMEM((MOE_BLOCK, d), BF16),
                        pltpu.VMEM((2, MOE_LOCAL_ROWS, d), BF16),
                        pltpu.SemaphoreType.DMA((2,)), pltpu.SemaphoreType.DMA,
                        pltpu.SemaphoreType.DMA],
    )
    return pl.pallas_call(
        _dispatch_body,
        grid_spec=grid_spec,
        out_shape=jax.ShapeDtypeStruct((n_rows, d), BF16),
        compiler_params=_params(("arbitrary",), 40 * 2**20),
        name="moe_dispatch",
    )(tail_start, runs, runs, h2, idx)


GMM_RING = 3


def _gmm_body(layer, be_ref, nb_ref, first_ref, next_ref, wslot_ref,
              x_hbm, wg_hbm, wu_hbm, wd_hbm, y_ref,
              x_ring, wg_stage, wu_stage, wd_stage, wg_bf, wu_bf, wd_bf, sem, wsem):
    j = pl.program_id(0)
    n_live = nb_ref[0]

    def fetch(step):
        slot = step % GMM_RING
        return pltpu.make_async_copy(
            x_hbm.at[pl.ds(pl.multiple_of(step * MOE_BLOCK, MOE_BLOCK), MOE_BLOCK)],
            x_ring.at[slot], sem.at[slot])

    def weight_copies(expert, slot):
        return [pltpu.make_async_copy(hbm.at[layer, expert], stage.at[slot], wsem.at[slot, i])
                for i, (hbm, stage) in enumerate(((wg_hbm, wg_stage), (wu_hbm, wu_stage),
                                                  (wd_hbm, wd_stage)))]

    @pl.when(j == 0)
    def _():
        for cp in weight_copies(be_ref[0], 0):
            cp.start()
        for ahead in range(GMM_RING - 1):
            @pl.when(ahead < n_live)
            def _():
                fetch(ahead).start()

    @pl.when(j + (GMM_RING - 1) < n_live)
    def _():
        fetch(j + (GMM_RING - 1)).start()

    @pl.when(j < n_live)
    def _():
        @pl.when(first_ref[j] == 1)
        def _():
            slot = wslot_ref[j]
            for cp in weight_copies(be_ref[j], slot):
                cp.wait()
            wg_bf[...] = wg_stage[slot].astype(BF16)
            wu_bf[...] = wu_stage[slot].astype(BF16)
            wd_bf[...] = wd_stage[slot].astype(BF16)

            @pl.when(next_ref[j] >= 0)
            def _():
                for cp in weight_copies(next_ref[j], 1 - slot):
                    cp.start()

        fetch(j).wait()
        xb = x_ring[j % GMM_RING]
        pre = _dot(xb, wg_bf[...])
        hid = pre * jax.nn.sigmoid(pre) * _dot(xb, wu_bf[...])
        y_ref[...] = _dot(hid.astype(BF16), wd_bf[...]).astype(BF16)

    @pl.when(j >= nb_ref[0])
    def _():
        y_ref[...] = jnp.zeros_like(y_ref)


def _gmm(xs, blk_expert, n_blocks, run_first, run_next, run_slot, p):
    n_rows, d = xs.shape
    nb_max = n_rows // MOE_BLOCK
    any_spec = pl.BlockSpec(memory_space=pl.ANY)
    grid_spec = pltpu.PrefetchScalarGridSpec(
        num_scalar_prefetch=5,
        grid=(nb_max,),
        in_specs=[any_spec] * 4,
        out_specs=pl.BlockSpec((MOE_BLOCK, d), lambda j, *_: (j, 0)),
        scratch_shapes=[pltpu.VMEM((GMM_RING, MOE_BLOCK, d), BF16),
                        pltpu.VMEM((2, d, D_EXPERT), F32), pltpu.VMEM((2, d, D_EXPERT), F32),
                        pltpu.VMEM((2, D_EXPERT, d), F32),
                        pltpu.VMEM((d, D_EXPERT), BF16), pltpu.VMEM((d, D_EXPERT), BF16),
                        pltpu.VMEM((D_EXPERT, d), BF16),
                        pltpu.SemaphoreType.DMA((GMM_RING,)), pltpu.SemaphoreType.DMA((2, 3))],
    )
    return pl.pallas_call(
        functools.partial(_gmm_body, p["layer"]),
        grid_spec=grid_spec,
        out_shape=jax.ShapeDtypeStruct((n_rows, d), BF16),
        compiler_params=_params(("arbitrary",), 24 * 2**20),
        name="moe_gmm",
    )(blk_expert, n_blocks, run_first, run_next, run_slot,
      xs, p["w_exp_gate"], p["w_exp_up"], p["w_exp_down"])


def _combine_body(run_ref, next_run_ref, ys_hbm, idx_ref, wts_ref, x1_ref, g2_ref, l2g_ref,
                  l2b_ref, out_ref, loc_buf, sem):
    nt = MOE_TILE
    step = pl.program_id(0)
    slot = step & 1

    def copies_into(which):
        def make_copy(dst, src, rows):
            return pltpu.make_async_copy(ys_hbm.at[pl.ds(dst, rows)],
                                         loc_buf.at[which, pl.ds(src, rows)], sem.at[which])
        return make_copy

    @pl.when(step == 0)
    def _():
        loc_buf[...] = jnp.zeros_like(loc_buf)
        _for_each_slab(run_ref, copies_into(slot), lambda cp: cp.start())

    @pl.when(step < pl.num_programs(0) - 1)
    def _():
        _for_each_slab(next_run_ref, copies_into(1 - slot), lambda cp: cp.start())

    _for_each_slab(run_ref, copies_into(slot), lambda cp: cp.wait())
    lp = idx_ref[...].astype(F32)
    w = wts_ref[...]
    c = lax.broadcasted_iota(jnp.int32, (nt, MOE_LOCAL_ROWS), 1).astype(F32)
    sel = (jnp.where(c == lp[:, 2:3], w[:, 0:1], 0.0)
           + jnp.where(c == lp[:, 3:4], w[:, 1:2], 0.0))
    sel_hi, sel_lo = _split2(sel)
    rows_bf = loc_buf[slot]
    moe = _dot(sel_hi, rows_bf) + _dot(sel_lo, rows_bf)
    out_ref[...] = _layer_norm(ALPHA * x1_ref[...] + g2_ref[0] * moe, l2g_ref[...], l2b_ref[...])


def _combine(seq_len, ys, idx, wts, runs, x1, g2, p):
    m, d = x1.shape
    nt = MOE_TILE
    tiles_per_seq = seq_len // nt
    tok = lambda n: pl.BlockSpec((nt, n), lambda i: (i, 0))
    n_tiles = m // nt
    run_spec = lambda at: pl.BlockSpec((1, 1, LANES), lambda i: (at(i), 0, 0),
                                       memory_space=pltpu.SMEM)
    return pl.pallas_call(
        _combine_body,
        grid=(n_tiles,),
        in_specs=[run_spec(lambda i: i), run_spec(lambda i: jnp.minimum(i + 1, n_tiles - 1)),
                  pl.BlockSpec(memory_space=pl.ANY), tok(LANES), tok(LANES), tok(d),
                  pl.BlockSpec((1, 1, d), lambda i: (i // tiles_per_seq, 0, 0)),
                  _const_spec((1, d)), _const_spec((1, d))],
        out_specs=tok(d),
        out_shape=jax.ShapeDtypeStruct((m, d), F32),
        scratch_shapes=[pltpu.VMEM((2, MOE_LOCAL_ROWS, d), BF16), pltpu.SemaphoreType.DMA((2,))],
        compiler_params=_params(("arbitrary",), 48 * 2**20),
        name="moe_combine",
    )(runs, runs, ys, idx, wts, x1, g2, p["ln2_g"], p["ln2_b"])


def _moe_routed(seq_len, h2, idx, wts, tile_cnt, x1, g2, p):
    m, d = h2.shape
    blk = MOE_BLOCK
    n_tiles = m // MOE_TILE
    worst_rows = 2 * m + n_tiles * N_EXPERTS * (MOE_RUN_ALIGN - 1) + N_EXPERTS * (blk - 1)
    nb_max = -(-worst_rows // blk)
    cnt = tile_cnt[:, 0, :N_EXPERTS]
    run_len = ((cnt + MOE_RUN_ALIGN - 1) // MOE_RUN_ALIGN) * MOE_RUN_ALIGN
    local_row = jnp.cumsum(run_len, axis=1) - run_len
    rows_before = jnp.cumsum(run_len, axis=0) - run_len
    total = jnp.sum(run_len, axis=0)
    padded = ((total + blk - 1) // blk) * blk
    ends = jnp.cumsum(padded)
    starts = ends - padded
    n_blocks = (ends[-1] // blk).astype(jnp.int32).reshape(1)
    first_row = jnp.arange(nb_max, dtype=jnp.int32) * blk
    blk_expert = jnp.minimum(
        jnp.sum((first_row[:, None] >= ends[None, :]).astype(jnp.int32), axis=1),
        N_EXPERTS - 1).astype(jnp.int32)
    tail_start = jnp.concatenate(
        [jnp.where(padded > 0, ends - blk, -1).astype(jnp.int32), n_blocks])
    runs = jnp.concatenate(
        [starts[None, :] + rows_before, local_row, run_len,
         jnp.zeros((n_tiles, LANES - 3 * N_EXPERTS), jnp.int32)], axis=1).astype(jnp.int32)
    runs = runs.reshape(n_tiles, 1, LANES)
    prev_expert = jnp.concatenate([jnp.full((1,), -1, jnp.int32), blk_expert[:-1]])
    run_first = (blk_expert != prev_expert).astype(jnp.int32)
    run_slot = ((jnp.cumsum(run_first) - 1) % 2).astype(jnp.int32)
    used = padded > 0
    later = jnp.arange(N_EXPERTS)[None, :] > jnp.arange(N_EXPERTS)[:, None]
    next_used = jnp.min(jnp.where(later & used[None, :], jnp.arange(N_EXPERTS)[None, :], N_EXPERTS),
                        axis=1)
    run_next = jnp.where(next_used < N_EXPERTS, next_used, -1).astype(jnp.int32)[blk_expert]
    xs = _dispatch(h2, idx, runs, tail_start, nb_max * blk)
    ys = _gmm(xs, blk_expert, n_blocks, run_first, run_next, run_slot, p)
    return _combine(seq_len, ys, idx, wts, runs, x1, g2, p)


def _prep_layer(l, w_in, mu_shift, w0, w_decay_up, a0, w_iclr_up, k_k, k_a, r_k,
                lnx_g, lnx_b, lnv_g, lnv_b, w_spatial, b_spatial, w_branch_a, w_branch_b,
                w_out, ln1_g, ln1_b, w_route_group, b_route_group, w_route_expert,
                b_route_expert, w_exp_gate, w_exp_up, w_exp_down, ln2_g, ln2_b):
    d = D_MODEL
    pad_a = N_SHIFT_PAD - N_SHIFT
    wi = w_in[l]
    w_in_p = jnp.concatenate(
        [wi[:, :N_SHIFT], jnp.zeros((d, pad_a), F32), wi[:, N_SHIFT:]], axis=1).astype(BF16)
    mu = jnp.concatenate([mu_shift[l], jnp.zeros((pad_a,), F32)])[None]
    lora = jnp.zeros((LANES, 2 * D_A), F32)
    lora = lora.at[:R_LORA, :D_A].set(w_decay_up[l]).at[R_LORA:2 * R_LORA, D_A:].set(w_iclr_up[l])
    seg = jnp.arange(D_A) // HEAD
    row1 = lambda x: x.reshape(1, -1)
    gsz = D_B // N_GROUPS_B
    w_route = jnp.concatenate(
        [w_route_expert[l], w_route_group[l],
         jnp.zeros((d, LANES - N_EXPERTS - N_ROUTE_GROUPS), F32)], axis=1)
    b_route = jnp.concatenate(
        [b_route_expert[l], b_route_group[l],
         jnp.zeros((LANES - N_EXPERTS - N_ROUTE_GROUPS,), F32)])[None]
    return dict(
        w_in=w_in_p, mu=mu, lora=lora,
        w0a0=jnp.concatenate([w0[l], a0[l]])[None],
        k_k=row1(k_k[l]), k_a=row1(k_a[l]), r_k=row1(r_k[l]),
        lnx_g=row1(lnx_g[l]), lnx_b=row1(lnx_b[l]),
        lnv_g=row1(lnv_g[l]), lnv_b=row1(lnv_b[l]),
        bd=(seg[:, None] == seg[None, :]).astype(BF16),
        w_spatial=w_spatial[l],
        b_spatial_full=jnp.repeat(b_spatial[l].T, gsz, axis=1),
        ws_row=jnp.repeat(w_spatial[l][:, 0, 0], gsz)[None],
        bs_row=jnp.repeat(b_spatial[l][:, 0], gsz)[None],
        w_branch_a=w_branch_a[l].astype(BF16), w_branch_b=w_branch_b[l].astype(BF16),
        w_out=w_out[l].astype(BF16), ln1_g=row1(ln1_g[l]), ln1_b=row1(ln1_b[l]),
        w_route=w_route, b_route=b_route,
        w_exp_gate=w_exp_gate, w_exp_up=w_exp_up, w_exp_down=w_exp_down, layer=l,
        ln2_g=row1(ln2_g[l]), ln2_b=row1(ln2_b[l]),
    )


def _trunk(seq_mode, x, mods, wkv_in, shift_in, preps):
    b, t, d = x.shape
    wkv_out, shift_out, v_out = [], [], []
    wkv_acc = None if seq_mode else jnp.zeros(wkv_in.shape, F32)
    for l in range(DEPTH):
        p = preps[l]
        sh1, sc1, g1, sh2, sc2, g2 = mods[l]
        if seq_mode:
            zprev = jnp.zeros((b, 1, N_SHIFT_PAD), F32) if shift_in is None else shift_in[l]
        else:
            zprev = _matmul(shift_in[l], p["w_in"][:, :N_SHIFT_PAD]).reshape(b, t, N_SHIFT_PAD)
        r, lw, kp, v, kap, bb, u, vg, ga, gb, hl = _in_stage(seq_mode, x, sc1, sh1, zprev, p)
        if seq_mode:
            o, s_new = _wkv_seq(r, lw, kp, v, kap, bb, wkv_in[l])
        else:
            flat = lambda a: a.reshape(t, D_A)
            o, wkv_acc = _wkv_step(flat(r), flat(lw), flat(kp), flat(v), flat(kap), flat(bb),
                                   wkv_in[l], wkv_acc, l)
            s_new = None
            o = o.reshape(b, t, D_A)
        x1, h2 = _merge_stage(seq_mode, x, o, r, kp, v, u, vg, ga, gb, g1, sc2, sh2, p)
        m = b * t
        h2f = h2.reshape(m, d)
        gate, idx, wts, tile_cnt = _route(h2f, p)
        if seq_mode:
            x = _moe_routed(t, h2f, idx, wts, tile_cnt, x1.reshape(m, d), g2, p)
        else:
            x = _moe(t, h2f, gate, x1.reshape(m, d), g2, p)
        x = x.reshape(b, t, d)
        wkv_out.append(s_new)
        shift_out.append(hl)
        v_out.append(vg)
    if not seq_mode:
        wkv_out = wkv_acc
    return x, wkv_out, shift_out, v_out


def kernel(x_prompt, x_sample, c_prompt, c_sample, state_wkv, state_shift, w_ada, b_ada, w_in, mu_shift, w0, w_decay_up, a0, w_iclr_up, k_k, k_a, r_k, lnx_g, lnx_b, lnv_g, lnv_b, w_spatial, b_spatial, w_branch_a, w_branch_b, w_out, ln1_g, ln1_b, w_route_group, b_route_group, w_route_expert, b_route_expert, w_exp_gate, w_exp_up, w_exp_down, ln2_g, ln2_b):
    bp, tp, d = x_prompt.shape
    bs = x_sample.shape[0]
    layer_params = (w_in, mu_shift, w0, w_decay_up, a0, w_iclr_up, k_k, k_a, r_k, lnx_g,
                    lnx_b, lnv_g, lnv_b, w_spatial, b_spatial, w_branch_a, w_branch_b, w_out,
                    ln1_g, ln1_b, w_route_group, b_route_group, w_route_expert,
                    b_route_expert, w_exp_gate, w_exp_up, w_exp_down, ln2_g, ln2_b)
    preps = [_prep_layer(l, *layer_params) for l in range(DEPTH)]
    mod_all = _ada(jnp.concatenate([c_prompt, c_sample], axis=0), w_ada, b_ada)
    mods_p, mods_s = [], []
    for l in range(DEPTH):
        parts = jnp.split(mod_all[l], 6, axis=-1)
        mods_p.append([q[:bp].reshape(bp, 1, d) for q in parts])
        mods_s.append([q[bp:].reshape(1, bs, d) for q in parts])

    wkv0 = jnp.zeros((DEPTH, bp, N_HEADS, HEAD, HEAD), F32)
    y_p, wkv_p, shift_p, _ = _trunk(True, x_prompt, mods_p, wkv0, None, preps)
    y_s, wkv_s, shift_s, v_s = _trunk(False, x_sample.reshape(1, bs, d), mods_s, state_wkv,
                                      state_shift, preps)
    return (y_p,
            y_s.reshape(bs, 1, d),
            jnp.stack(wkv_p),
            jnp.stack([s.reshape(bp, d) for s in shift_p]),
            wkv_s,
            jnp.stack([s.reshape(bs, d) for s in shift_s]),
            jnp.stack([q.reshape(bs, 1, D_B) for q in v_s]))
```

```python
import functools

import jax
import jax.numpy as jnp
from jax import lax
from jax.experimental import pallas as pl
from jax.experimental.pallas import tpu as pltpu

F32 = jnp.float32
BF16 = jnp.bfloat16

D_MODEL = 1024
DEPTH = 2
HEAD = 64
N_HEADS = 8
D_A = N_HEADS * HEAD
R_LORA = 32
CHUNK = 128
N_GROUPS_B = 8
D_B = 512
N_SHIFT = 3 * D_A + 2 * R_LORA
N_ROUTE_GROUPS = 4
EXP_PER_GROUP = 8
N_EXPERTS = N_ROUTE_GROUPS * EXP_PER_GROUP
D_EXPERT = 256
ALPHA = (2 * DEPTH) ** 0.25
LN_EPS = 1e-5
GN_EPS = 64e-5

LANES = 128
N_SHIFT_PAD = 13 * LANES
COL_U = N_SHIFT_PAD
COL_VG = COL_U + D_B
COL_GA = COL_VG + D_B
COL_GB = COL_GA + D_MODEL
N_IN_PAD = COL_GB + D_MODEL
WKV_CHUNK = 64
IN_ROW_GROUPS = 4
IN_GROUP_ROWS = 128
VMEM_CAP_BYTES = 60000 * 1024

_NN = (((1,), (0,)), ((), ()))
_B_NT = (((2,), (2,)), ((0,), (0,)))
_B_NN = (((2,), (1,)), ((0,), (0,)))
_B_TN = (((1,), (1,)), ((0,), (0,)))


def _dot(a, b, dims=_NN):
    return lax.dot_general(a, b, dims, preferred_element_type=F32)


def _split2(x):
    hi = x.astype(BF16)
    lo = (x - hi.astype(F32)).astype(BF16)
    return hi, lo


def _mm1(a, b, dims=_NN):
    return _dot(a.astype(BF16), b.astype(BF16), dims)


def _mm3(a, b, dims=_NN):
    ah, al = _split2(a)
    bh, bl = _split2(b)
    return _dot(ah, bh, dims) + (_dot(ah, bl, dims) + _dot(al, bh, dims))


def _segsum(x, bd, two_terms=False):
    if not two_terms:
        return _dot(x.astype(BF16), bd)
    hi, lo = _split2(x)
    return _dot(hi, bd) + _dot(lo, bd)


def _layer_norm(x, g, b):
    mu = jnp.mean(x, axis=-1, keepdims=True)
    d = x - mu
    var = jnp.mean(d * d, axis=-1, keepdims=True)
    return d * lax.rsqrt(var + LN_EPS) * g + b


def _gelu(x):
    return 0.5 * x * (1.0 + lax.erf(x * 0.7071067811865476))


def _params(sem, est_bytes):
    limit = int(min(VMEM_CAP_BYTES, max(est_bytes, 16 * 1024 * 1024)))
    return pltpu.CompilerParams(dimension_semantics=sem, vmem_limit_bytes=limit)


def _const_spec(shape, single_buffer=False):
    nd = len(shape)
    if single_buffer:
        return pl.BlockSpec(shape, lambda *_: (0,) * nd, pipeline_mode=pl.Buffered(1))
    return pl.BlockSpec(shape, lambda *_: (0,) * nd)


def _ada_body(c_ref, w_ref, b_ref, o_ref):
    c = c_ref[...]
    s = c * jax.nn.sigmoid(c)
    o_ref[0] = _mm3(s, w_ref[0]) + b_ref[0]


def _ada(c_all, w_ada, b_ada):
    depth, d, n6 = w_ada.shape
    m = c_all.shape[0]
    tn = 1024
    return pl.pallas_call(
        _ada_body,
        grid=(depth, n6 // tn),
        in_specs=[
            pl.BlockSpec((m, d), lambda l, j: (0, 0)),
            pl.BlockSpec((1, d, tn), lambda l, j: (l, 0, j)),
            pl.BlockSpec((1, 1, tn), lambda l, j: (l, 0, j)),
        ],
        out_specs=pl.BlockSpec((1, m, tn), lambda l, j: (l, 0, j)),
        out_shape=jax.ShapeDtypeStruct((depth, m, n6), F32),
        compiler_params=_params(("arbitrary", "arbitrary"), 24 * 2**20),
        name="ada_mod",
    )(c_all, w_ada, b_ada.reshape(depth, 1, n6))


def _mm_body(x_ref, w_ref, o_ref):
    o_ref[...] = _mm1(x_ref[...], w_ref[...])


def _matmul(x, w):
    m, k = x.shape
    n = w.shape[1]
    return pl.pallas_call(
        _mm_body,
        grid=(1,),
        in_specs=[_const_spec((m, k)), _const_spec((k, n))],
        out_specs=_const_spec((m, n)),
        out_shape=jax.ShapeDtypeStruct((m, n), F32),
        compiler_params=_params(("arbitrary",), 24 * 2**20),
        name="shift_proj",
    )(x, w)


def _in_body(seq_mode, tm, x_ref, sc_ref, sh_ref, w_ref, mu_ref, zp_ref, lora_ref,
             w0a0_ref, kk_ref, ka_ref, lng_ref, lnb_ref, bd_ref,
             r_ref, lw_ref, kp_ref, v_ref, kap_ref, bb_ref, u_ref, vg_ref,
             ga_ref, gb_ref, hl_ref, carry_ref):
    nsplit = IN_ROW_GROUPS if seq_mode else 1
    rows = tm // nsplit
    if seq_mode:
        @pl.when(pl.program_id(1) == 0)
        def _():
            carry_ref[...] = zp_ref[0]

        carry = carry_ref[...]
    bd = bd_ref[...]
    for part in range(nsplit):
        sl = slice(part * rows, (part + 1) * rows)
        h = x_ref[0, sl, :] * (1.0 + sc_ref[0]) + sh_ref[0]
        hb = h.astype(BF16)
        proj = lambda lo, hi: _dot(hb, w_ref[:, lo:hi])
        za = proj(0, N_SHIFT_PAD)
        if seq_mode:
            row = lax.broadcasted_iota(jnp.int32, za.shape, 0)
            prev = jnp.where(row == 0, carry, pltpu.roll(za, 1, 0))
            carry = za[rows - 1:rows, :]
        else:
            prev = zp_ref[0]
        mix = za + mu_ref[...] * (prev - za)
        r = mix[:, 0:D_A]
        k = mix[:, D_A:2 * D_A]
        v = mix[:, 2 * D_A:3 * D_A]
        xwa = mix[:, 3 * D_A:N_SHIFT_PAD]
        lane = lax.broadcasted_iota(jnp.int32, xwa.shape, 1)
        lora_in = jnp.where(lane < R_LORA, jnp.tanh(xwa), xwa)
        pre = w0a0_ref[...] + _mm1(lora_in, lora_ref[...])
        yw = -pre[:, :D_A]
        softplus = jnp.maximum(yw, 0.0) + jnp.log1p(jnp.exp(-jnp.abs(yw)))
        lw = -jnp.exp(-softplus - 0.5)
        a = jax.nn.sigmoid(pre[:, D_A:])
        kk = k * kk_ref[...]
        kap = kk / jnp.maximum(jnp.sqrt(_segsum(kk * kk, bd)), 1e-12)
        r_ref[0, sl, :] = r
        lw_ref[0, sl, :] = lw
        kp_ref[0, sl, :] = k * (1.0 + (a - 1.0) * ka_ref[...])
        v_ref[0, sl, :] = v
        kap_ref[0, sl, :] = kap
        bb_ref[0, sl, :] = kap * a
        u_ref[0, sl, :] = _gelu(proj(COL_U, COL_VG)).astype(BF16)
        vg_ref[0, sl, :] = _layer_norm(_gelu(proj(COL_VG, COL_GA)), lng_ref[...], lnb_ref[...])
        ga_ref[0, sl, :] = jax.nn.sigmoid(proj(COL_GA, COL_GB)).astype(BF16)
        gb_ref[0, sl, :] = jax.nn.sigmoid(proj(COL_GB, N_IN_PAD)).astype(BF16)
    if seq_mode:
        carry_ref[...] = carry
        hl_ref[0] = h[rows - 1:rows, :]
    else:
        hl_ref[0] = h


def _in_stage(seq_mode, x, sc, sh, zprev, p):
    b, t, d = x.shape
    tm = IN_ROW_GROUPS * IN_GROUP_ROWS if seq_mode else t
    tmod = 1 if seq_mode else tm
    grid = (b, t // tm)
    tok = lambda n: pl.BlockSpec((1, tm, n), lambda i, j: (i, j, 0))
    mod = pl.BlockSpec((1, tmod, d), lambda i, j: (i, j if not seq_mode else 0, 0))
    zp_spec = (pl.BlockSpec((1, 1, N_SHIFT_PAD), lambda i, j: (i, 0, 0)) if seq_mode
               else tok(N_SHIFT_PAD))
    hl_spec = (pl.BlockSpec((1, 1, d), lambda i, j: (i, 0, 0)) if seq_mode else tok(d))
    hl_shape = (b, 1, d) if seq_mode else (b, t, d)
    out_cols = [D_A] * 6 + [D_B] * 2 + [d] * 2
    out_dtypes = [F32] * 6 + [BF16, F32] + [BF16] * 2
    est = (2 * tm * (d + N_SHIFT_PAD + sum(out_cols) + d) * 4 + 2 * d * N_IN_PAD * 2
           + 3 * tm * N_IN_PAD * 4 + 4 * 2**20)
    outs = pl.pallas_call(
        functools.partial(_in_body, seq_mode, tm),
        grid=grid,
        in_specs=[tok(d), mod, mod,
                  _const_spec((d, N_IN_PAD), True), _const_spec((1, N_SHIFT_PAD)), zp_spec,
                  _const_spec((LANES, 2 * D_A)), _const_spec((1, 2 * D_A)),
                  _const_spec((1, D_A)), _const_spec((1, D_A)),
                  _const_spec((1, D_B)), _const_spec((1, D_B)),
                  _const_spec((D_A, D_A))],
        out_specs=[tok(n) for n in out_cols] + [hl_spec],
        out_shape=[jax.ShapeDtypeStruct((b, t, n), dt) for n, dt in zip(out_cols, out_dtypes)]
        + [jax.ShapeDtypeStruct(hl_shape, F32)],
        scratch_shapes=[pltpu.VMEM((1, N_SHIFT_PAD), F32)],
        compiler_params=_params(("arbitrary", "arbitrary"), est),
        name="in_stage_seq" if seq_mode else "in_stage_row",
    )(x, sc, sh, p["w_in"], p["mu"], zprev, p["lora"], p["w0a0"], p["k_k"], p["k_a"],
      p["lnv_g"], p["lnv_b"], p["bd"])
    return outs


def _chunk_pairs(x, nsub):
    c = WKV_CHUNK
    w = 2 * HEAD
    return jnp.stack([x[c * i:c * (i + 1), w * q:w * (q + 1)]
                      for i in range(nsub) for q in range(N_HEADS // 2)], axis=0)


def _masked_rowsum(mask_bf, x):
    h1, h2 = _split2(x)
    return _dot(mask_bf, h1) + _dot(mask_bf, h2)


def _masked_rowsum_t(x, mask):
    h1, h2 = _split2(x)
    m = mask.astype(BF16)
    return _dot(h1, m) + _dot(h2, m)


def _wkv_seq_body(nsub, r_ref, lw_ref, kp_ref, v_ref, kap_ref, bb_ref, s0_ref,
                  o_ref, sout_ref, s_scr):
    c = WKV_CHUNK
    tc = nsub * c
    npair = N_HEADS // 2

    @pl.when(pl.program_id(1) == 0)
    def _():
        for q in range(npair):
            s_scr[q] = jnp.concatenate([s0_ref[0, 2 * q], s0_ref[0, 2 * q + 1]], axis=-1)

    row_t = lax.broadcasted_iota(jnp.int32, (tc, tc), 0)
    col_t = lax.broadcasted_iota(jnp.int32, (tc, tc), 1)
    shift = c.bit_length() - 1
    same_chunk = (row_t >> shift) == (col_t >> shift)
    lw = lw_ref[0]
    g = _masked_rowsum((same_chunk & (row_t >= col_t)).astype(BF16), lw)
    g_end = _masked_rowsum(same_chunk.astype(BF16), lw)
    e_neg = jnp.exp(-g)
    e_end = jnp.exp(g_end - g)
    cp = functools.partial(_chunk_pairs, nsub=nsub)
    k = kp_ref[0]
    b = bb_ref[0]
    kap_t = cp(kap_ref[0] * jnp.exp(g - lw))
    b_t = cp(b * e_neg)
    k_t = cp(k * e_neg)
    r_t = cp(r_ref[0] * jnp.exp(g))
    b_e = cp(b * e_end)
    k_e = cp(k * e_end)
    vv = cp(v_ref[0])
    decay_end = cp(jnp.exp(g_end))

    row = lax.broadcasted_iota(jnp.int32, (c, 2 * c), 0)[None]
    lane = lax.broadcasted_iota(jnp.int32, (c, 2 * c), 1)[None]
    colp = lane & (c - 1)
    right = lane >= c
    row2 = lax.broadcasted_iota(jnp.int32, (2 * c, 2 * c), 0)[None]
    lane2 = lax.broadcasted_iota(jnp.int32, (2 * c, 2 * c), 1)[None]
    same_head = (row2 >= c) == (lane2 >= c)

    def bd(x):
        return jnp.concatenate([jnp.where(right, 0.0, x), jnp.where(right, x, 0.0)], axis=1)

    p_b = _mm1(jnp.concatenate([kap_t, r_t], axis=1), bd(b_t), _B_NT)
    l_b = jnp.where(row > colp, p_b[:, :c], 0.0)
    a_rb = jnp.where(row >= colp, p_b[:, c:], 0.0)
    bd_k = bd(k_t)
    l_k = jnp.where(row > colp, _mm1(kap_t, bd_k, _B_NT), 0.0)
    a_rk = jnp.where(row >= colp, _mm1(r_t, bd_k, _B_NT), 0.0)
    m = -l_b
    t_inv = jnp.where(row == colp, 1.0, 0.0) + m
    m = _mm1(m, bd(m), _B_NN)
    span = 2
    while 2 * span < c:
        both = _mm1(jnp.concatenate([m, t_inv], axis=1), bd(m), _B_NN)
        m = both[:, :c]
        t_inv = t_inv + both[:, c:]
        span *= 2
    t_inv = t_inv + _mm1(t_inv, bd(m), _B_NN)
    bd_v = bd(vv)
    a1 = -_mm1(t_inv, bd(kap_t), _B_NN)
    u0 = -_mm1(t_inv, bd(_mm1(l_k, bd_v, _B_NN)), _B_NN)
    a2 = r_t + _mm1(a_rb, bd(a1), _B_NN)
    o0 = _mm1(a_rb, bd(u0), _B_NN) + _mm1(a_rk, bd_v, _B_NN)
    g_bd = (jnp.where(same_head, _mm1(a1, b_e, _B_TN), 0.0)
            + jnp.where(row2 == lane2, decay_end[:, 0:1, :], 0.0))
    hh = _mm1(jnp.concatenate([u0, vv], axis=1),
              jnp.concatenate([b_e, k_e], axis=1), _B_TN)
    h_pair = jnp.where(right, hh[:, c:], hh[:, :c])

    s = s_scr[...]
    for i in range(nsub):
        ps = slice(npair * i, npair * (i + 1))
        o = _mm1(a2[ps], bd(s), _B_NT) + o0[ps]
        s = _mm1(s, g_bd[ps], _B_NN) + h_pair[ps]
        for q in range(npair):
            o_ref[0, c * i:c * (i + 1), 2 * HEAD * q:2 * HEAD * (q + 1)] = o[q]
    s_scr[...] = s

    @pl.when(pl.program_id(1) == pl.num_programs(1) - 1)
    def _():
        for q in range(npair):
            sout_ref[0, 2 * q] = s[q][:, :HEAD]
            sout_ref[0, 2 * q + 1] = s[q][:, HEAD:]


def _wkv_seq(r, lw, kp, v, kap, bb, s0):
    b, t, _ = r.shape
    tc = 256
    nsub = tc // WKV_CHUNK
    tok = pl.BlockSpec((1, tc, D_A), lambda i, j: (i, j, 0))
    st = pl.BlockSpec((1, N_HEADS, HEAD, HEAD), lambda i, j: (i, 0, 0, 0))
    return pl.pallas_call(
        functools.partial(_wkv_seq_body, nsub),
        grid=(b, t // tc),
        in_specs=[tok] * 6 + [st],
        out_specs=[tok, st],
        out_shape=[jax.ShapeDtypeStruct((b, t, D_A), F32),
                   jax.ShapeDtypeStruct((b, N_HEADS, HEAD, HEAD), F32)],
        scratch_shapes=[pltpu.VMEM((N_HEADS // 2, HEAD, 2 * HEAD), F32)],
        compiler_params=_params(("arbitrary", "arbitrary"), 32 * 2**20),
        name="wkv_seq",
    )(r, lw, kp, v, kap, bb, s0)


def _wkv_step_body(r_ref, lw_ref, kp_ref, v_ref, kap_ref, bb_ref, s0_ref, _, o_ref, sout_ref):
    eye = (lax.broadcasted_iota(jnp.int32, (HEAD, HEAD), 0)
           == lax.broadcasted_iota(jnp.int32, (HEAD, HEAD), 1))[None]
    for h in range(N_HEADS):
        hs = slice(HEAD * h, HEAD * (h + 1))
        row = lambda ref: ref[:, hs][:, None, :]
        s = s0_ref[:, h]
        sa = -jnp.sum(s * row(kap_ref), axis=-1, keepdims=True)
        v_col = jnp.sum(jnp.where(eye, row(v_ref), 0.0), axis=-1, keepdims=True)
        s_new = s * jnp.exp(row(lw_ref)) + sa * row(bb_ref) + v_col * row(kp_ref)
        sout_ref[:, h] = s_new
        o_col = jnp.sum(s_new * row(r_ref), axis=-1, keepdims=True)
        o_ref[:, hs] = jnp.sum(jnp.where(eye, o_col, 0.0), axis=1)


def _wkv_step(r, lw, kp, v, kap, bb, s0, s_new_all, layer):
    m = r.shape[0]
    nb = 8
    tok = pl.BlockSpec((nb, D_A), lambda i: (i, 0))
    st = pl.BlockSpec((nb, N_HEADS, HEAD, HEAD), lambda i: (i, 0, 0, 0))
    st_out = pl.BlockSpec((None, nb, N_HEADS, HEAD, HEAD), lambda i: (layer, i, 0, 0, 0))
    return pl.pallas_call(
        _wkv_step_body,
        grid=(m // nb,),
        in_specs=[tok] * 6 + [st, pl.BlockSpec(memory_space=pl.ANY)],
        out_specs=[tok, st_out],
        out_shape=[jax.ShapeDtypeStruct((m, D_A), F32),
                   jax.ShapeDtypeStruct(s_new_all.shape, F32)],
        input_output_aliases={7: 1},
        compiler_params=_params(("arbitrary",), 24 * 2**20),
        name="wkv_step",
    )(r, lw, kp, v, kap, bb, s0, s_new_all)


def _merge_body(seq_mode, tm, x_ref, o_ref, r_ref, kp_ref, v_ref, u_ref, vg_ref,
                ga_ref, gb_ref, g1_ref, sc2_ref, sh2_ref, lxg_ref, lxb_ref, rk_ref,
                bd_ref, ws_ref, bs_ref, pa_ref, pb_ref, wo_ref, l1g_ref, l1b_ref,
                x1_ref, h2_ref):
    bd = bd_ref[...]
    o = o_ref[0]
    inv_n = 1.0 / HEAD
    mu = _segsum(o, bd, two_terms=True) * inv_n
    d = o - mu
    var = _segsum(d * d, bd) * inv_n
    on = d * lax.rsqrt(var + GN_EPS) * lxg_ref[...] + lxb_ref[...]
    v = v_ref[0]
    o_a = on + _segsum(r_ref[0] * kp_ref[0] * rk_ref[...], bd) * v
    vg = vg_ref[0]
    if seq_mode:
        row = lax.broadcasted_iota(jnp.int32, (CHUNK, CHUNK), 0)
        col = lax.broadcasted_iota(jnp.int32, (CHUNK, CHUNK), 1)
        lane = lax.broadcasted_iota(jnp.int32, (CHUNK, D_B), 1)
        gsz = D_B // N_GROUPS_B
        w_cat = jnp.concatenate(
            [jnp.where(row >= col, ws_ref[g], 0.0).astype(BF16) for g in range(N_GROUPS_B)], axis=1)
        pieces = []
        for j in range(tm // CHUNK):
            vc = vg[j * CHUNK:(j + 1) * CHUNK, :].astype(BF16)
            v_bd = jnp.concatenate(
                [jnp.where((lane >= g * gsz) & (lane < (g + 1) * gsz), vc, 0.0)
                 for g in range(N_GROUPS_B)], axis=0)
            pieces.append(_dot(w_cat, v_bd) + bs_ref[...])
        s = jnp.concatenate(pieces, axis=0) if len(pieces) > 1 else pieces[0]
    else:
        s = vg * ws_ref[...] + bs_ref[...]
    o_b = u_ref[0] * s
    y = _mm1(ga_ref[0] * _mm1(o_a, pa_ref[...]) + gb_ref[0] * _mm1(o_b, pb_ref[...]),
             wo_ref[...])
    x1 = _layer_norm(ALPHA * x_ref[0] + g1_ref[0] * y, l1g_ref[...], l1b_ref[...])
    x1_ref[0] = x1
    h2_ref[0] = x1 * (1.0 + sc2_ref[0]) + sh2_ref[0]


def _merge_stage(seq_mode, x, o, r, kp, v, u, vg, ga, gb, g1, sc2, sh2, p):
    b, t, d = x.shape
    tm = 256 if seq_mode else t
    tmod = 1 if seq_mode else tm
    tok = lambda n: pl.BlockSpec((1, tm, n), lambda i, j: (i, j, 0))
    mod = pl.BlockSpec((1, tmod, d), lambda i, j: (i, j if not seq_mode else 0, 0))
    ws, bs = (p["w_spatial"], p["b_spatial_full"]) if seq_mode else (p["ws_row"], p["bs_row"])
    est = 2 * tm * (2 * d + 7 * D_A + 2 * d + 2 * d) * 4 + 8 * tm * d * 4 + 16 * 2**20
    return pl.pallas_call(
        functools.partial(_merge_body, seq_mode, tm),
        grid=(b, t // tm),
        in_specs=[tok(d)] + [tok(D_A)] * 6 + [tok(d), tok(d), mod, mod, mod,
                  _const_spec((1, D_A)), _const_spec((1, D_A)), _const_spec((1, D_A)),
                  _const_spec((D_A, D_A)), _const_spec(ws.shape), _const_spec(bs.shape),
                  _const_spec((D_A, d)), _const_spec((D_B, d)), _const_spec((d, d)),
                  _const_spec((1, d)), _const_spec((1, d))],
        out_specs=[tok(d), tok(d)],
        out_shape=[jax.ShapeDtypeStruct((b, t, d), F32)] * 2,
        compiler_params=_params(("arbitrary", "arbitrary"), est),
        name="merge_seq" if seq_mode else "merge_row",
    )(x, o, r, kp, v, u, vg, ga, gb, g1, sc2, sh2, p["lnx_g"], p["lnx_b"], p["r_k"],
      p["bd"], ws, bs, p["w_branch_a"], p["w_branch_b"], p["w_out"], p["ln1_g"], p["ln1_b"])


def _route_body(h_ref, w_ref, b_ref, gate_ref, idx_ref, wts_ref, cnt_ref):
    lg = _mm3(h_ref[...], w_ref[...]) + b_ref[...]
    lane = lax.broadcasted_iota(jnp.int32, lg.shape, 1)
    lanef = lane.astype(F32)
    neg = -jnp.inf
    is_g = (lane >= N_EXPERTS) & (lane < N_EXPERTS + N_ROUTE_GROUPS)
    mg = jnp.max(jnp.where(is_g, lg, neg), axis=-1, keepdims=True)
    gidx = jnp.min(jnp.where(is_g & (lg == mg), lanef - N_EXPERTS, 1e9), axis=-1, keepdims=True)
    pg_sel = 1.0 / jnp.sum(jnp.where(is_g, jnp.exp(lg - mg), 0.0), axis=-1, keepdims=True)
    lo = gidx * EXP_PER_GROUP
    in_grp = (lanef >= lo) & (lanef < lo + EXP_PER_GROUP)
    t1 = jnp.max(jnp.where(in_grp, lg, neg), axis=-1, keepdims=True)
    i1 = jnp.min(jnp.where(in_grp & (lg == t1), lanef, 1e9), axis=-1, keepdims=True)
    rest = in_grp & (lanef != i1)
    t2 = jnp.max(jnp.where(rest, lg, neg), axis=-1, keepdims=True)
    i2 = jnp.min(jnp.where(rest & (lg == t2), lanef, 1e9), axis=-1, keepdims=True)
    e2 = jnp.exp(t2 - t1)
    w1 = pg_sel / (1.0 + e2)
    w2 = pg_sel * e2 / (1.0 + e2)
    gate_ref[...] = jnp.where(lanef == i1, w1, 0.0) + jnp.where(lanef == i2, w2, 0.0)
    tm = lg.shape[0]
    hit = (lanef == i1) | (lanef == i2)
    earlier = (lax.broadcasted_iota(jnp.int32, (tm, tm), 0)
               > lax.broadcasted_iota(jnp.int32, (tm, tm), 1))
    rank = _dot(earlier.astype(BF16), hit.astype(BF16))
    cnt = jnp.sum(hit.astype(F32), axis=0, keepdims=True)
    cnt8 = jnp.floor((cnt + (MOE_RUN_ALIGN - 1.0)) * (1.0 / MOE_RUN_ALIGN)) * MOE_RUN_ALIGN
    lower_expert = (lax.broadcasted_iota(jnp.int32, (LANES, LANES), 0)
                    < lax.broadcasted_iota(jnp.int32, (LANES, LANES), 1))
    run_start = _masked_rowsum_t(jnp.broadcast_to(cnt8, (8, LANES)), lower_expert)[0:1]
    lpos = run_start + rank
    lp1 = jnp.sum(jnp.where(lanef == i1, lpos, 0.0), axis=-1, keepdims=True)
    lp2 = jnp.sum(jnp.where(lanef == i2, lpos, 0.0), axis=-1, keepdims=True)
    cnt_ref[0] = cnt.astype(jnp.int32)
    idx = jnp.where(lane == 0, i1, jnp.where(lane == 1, i2, jnp.where(lane == 2, lp1, lp2)))
    idx_ref[...] = idx.astype(jnp.int32)
    wts_ref[...] = jnp.where(lane == 0, w1, w2)


def _route(h2, p):
    m, d = h2.shape
    tm = min(m, MOE_TILE)
    tok = pl.BlockSpec((tm, LANES), lambda i: (i, 0))
    return pl.pallas_call(
        _route_body,
        grid=(m // tm,),
        in_specs=[pl.BlockSpec((tm, d), lambda i: (i, 0)),
                  _const_spec((d, LANES)), _const_spec((1, LANES))],
        out_specs=[tok, tok, tok, pl.BlockSpec((1, 1, LANES), lambda i: (i, 0, 0))],
        out_shape=[jax.ShapeDtypeStruct((m, LANES), F32),
                   jax.ShapeDtypeStruct((m, LANES), jnp.int32),
                   jax.ShapeDtypeStruct((m, LANES), F32),
                   jax.ShapeDtypeStruct((m // tm, 1, LANES), jnp.int32)],
        compiler_params=_params(("arbitrary",), 24 * 2**20),
        name="route",
    )(h2, p["w_route"], p["b_route"])


def _moe_body(h_ref, gate_ref, wg_ref, wu_ref, wd_ref, x1_ref, g2_ref, l2g_ref, l2b_ref,
              out_ref, acc_ref, xb_ref):
    e = pl.program_id(1)

    @pl.when(e == 0)
    def _():
        acc_ref[...] = jnp.zeros_like(acc_ref)
        xb_ref[...] = h_ref[...].astype(BF16)

    xb = xb_ref[...]
    pre = _dot(xb, wg_ref[0].astype(BF16))
    hid = pre * jax.nn.sigmoid(pre) * _dot(xb, wu_ref[0].astype(BF16))
    ye = _mm1(hid, wd_ref[0])
    gate = gate_ref[...]
    lane = lax.broadcasted_iota(jnp.int32, gate.shape, 1)
    ge = jnp.sum(jnp.where(lane == e, gate, 0.0), axis=-1, keepdims=True)
    acc_ref[...] += ge * ye

    @pl.when(e == pl.num_programs(1) - 1)
    def _():
        out_ref[...] = _layer_norm(ALPHA * x1_ref[...] + g2_ref[0] * acc_ref[...],
                                   l2g_ref[...], l2b_ref[...])


def _moe(seq_len, h2, gate, x1, g2, p):
    m, d = h2.shape
    layer = p["layer"]
    tm = min(seq_len if g2.shape[1] == 1 else m, 1024)
    tok = lambda n: pl.BlockSpec((tm, n), lambda i, e: (i, 0))
    if g2.shape[1] == 1:
        tiles_per_seq = seq_len // tm
        g2_spec = pl.BlockSpec((1, 1, d), lambda i, e: (i // tiles_per_seq, 0, 0))
    else:
        g2_spec = pl.BlockSpec((1, tm, d), lambda i, e: (0, i, 0))
    est = 2 * tm * (3 * d + LANES) * 4 + tm * d * 6 + 4 * tm * d * 4 + 8 * 2**20
    return pl.pallas_call(
        _moe_body,
        grid=(m // tm, N_EXPERTS),
        in_specs=[tok(d), tok(LANES),
                  pl.BlockSpec((None, 1, d, D_EXPERT), lambda i, e: (layer, e, 0, 0)),
                  pl.BlockSpec((None, 1, d, D_EXPERT), lambda i, e: (layer, e, 0, 0)),
                  pl.BlockSpec((None, 1, D_EXPERT, d), lambda i, e: (layer, e, 0, 0)),
                  tok(d), g2_spec, _const_spec((1, d)), _const_spec((1, d))],
        out_specs=tok(d),
        out_shape=jax.ShapeDtypeStruct((m, d), F32),
        scratch_shapes=[pltpu.VMEM((tm, d), F32), pltpu.VMEM((tm, d), BF16)],
        compiler_params=_params(("arbitrary", "arbitrary"), est),
        name="moe_dense",
    )(h2, gate, p["w_exp_gate"], p["w_exp_up"], p["w_exp_down"], x1, g2, p["ln2_g"], p["ln2_b"])


MOE_TILE = 512
MOE_BLOCK = 512
MOE_RUN_ALIGN = 16
MOE_LOCAL_ROWS = 2 * MOE_TILE + 512
MOE_SLABS = tuple(2 ** k for k in range(9, 3, -1))


def _for_each_slab(run_ref, make_copy, fn):
    def one_run(e, c):
        dst = run_ref[0, 0, e]
        src = run_ref[0, 0, N_EXPERTS + e]
        n = run_ref[0, 0, 2 * N_EXPERTS + e]
        for slab in MOE_SLABS:
            off = n & (-2 * slab)

            @pl.when((n & slab) != 0)
            def _():
                fn(make_copy(pl.multiple_of(dst + off, MOE_RUN_ALIGN),
                             pl.multiple_of(src + off, MOE_RUN_ALIGN), slab))
        return c

    lax.fori_loop(0, N_EXPERTS, one_run, 0)


def _dispatch_body(tail_ref, run_ref, prev_run_ref, h_ref, idx_ref, xs_hbm, zero_buf, loc_buf,
                   sem, zsem, usem):
    nb_max = xs_hbm.shape[0] // MOE_BLOCK
    nt = MOE_TILE

    def zero_copy(row0, zero_sem):
        return pltpu.make_async_copy(
            zero_buf, xs_hbm.at[pl.ds(pl.multiple_of(row0, MOE_BLOCK), MOE_BLOCK)], zero_sem)

    def each_unused_copy(fn):
        def unused(j, c):
            fn(zero_copy(j * MOE_BLOCK, usem))
            return c

        lax.fori_loop(tail_ref[N_EXPERTS], nb_max, unused, 0)

    @pl.when(pl.program_id(0) == 0)
    def _():
        zero_buf[...] = jnp.zeros_like(zero_buf)

        def each_tail_copy(fn):
            for e in range(N_EXPERTS):
                @pl.when(tail_ref[e] >= 0)
                def _():
                    fn(zero_copy(tail_ref[e], zsem))

        each_tail_copy(lambda cp: cp.start())
        each_unused_copy(lambda cp: cp.start())
        each_tail_copy(lambda cp: cp.wait())

    lp = idx_ref[...].astype(F32)
    eye = (lax.broadcasted_iota(jnp.int32, (nt, nt), 0)
           == lax.broadcasted_iota(jnp.int32, (nt, nt), 1))
    as_row = lambda col: jnp.sum(jnp.where(eye, col, 0.0), axis=0, keepdims=True)
    r = lax.broadcasted_iota(jnp.int32, (MOE_LOCAL_ROWS, nt), 0).astype(F32)
    pick = (r == as_row(lp[:, 2:3])) | (r == as_row(lp[:, 3:4]))
    step = pl.program_id(0)
    slot = step & 1
    loc_buf[slot] = _dot(pick.astype(BF16), h_ref[...].astype(BF16)).astype(BF16)

    def copies_from(which):
        def make_copy(dst, src, rows):
            return pltpu.make_async_copy(loc_buf.at[which, pl.ds(src, rows)],
                                         xs_hbm.at[pl.ds(dst, rows)], sem.at[which])
        return make_copy

    _for_each_slab(run_ref, copies_from(slot), lambda cp: cp.start())

    @pl.when(step > 0)
    def _():
        _for_each_slab(prev_run_ref, copies_from(1 - slot), lambda cp: cp.wait())

    @pl.when(step == pl.num_programs(0) - 1)
    def _():
        _for_each_slab(run_ref, copies_from(slot), lambda cp: cp.wait())
        each_unused_copy(lambda cp: cp.wait())


def _dispatch(h2, idx, runs, tail_start, n_rows):
    m, d = h2.shape
    nt = MOE_TILE
    run_spec = lambda at: pl.BlockSpec((1, 1, LANES), lambda i, tail: (at(i), 0, 0),
                                       memory_space=pltpu.SMEM)
    grid_spec = pltpu.PrefetchScalarGridSpec(
        num_scalar_prefetch=1,
        grid=(m // nt,),
        in_specs=[run_spec(lambda i: i), run_spec(lambda i: jnp.maximum(i - 1, 0)),
                  pl.BlockSpec((nt, d), lambda i, tail: (i, 0)),
                  pl.BlockSpec((nt, LANES), lambda i, tail: (i, 0))],
        out_specs=pl.BlockSpec(memory_space=pl.ANY),
        scratch_shapes=[pltpu.VMEM((MOE_BLOCK, d), BF16),
                        pltpu.VMEM((2, MOE_LOCAL_ROWS, d), BF16),
                        pltpu.SemaphoreType.DMA((2,)), pltpu.SemaphoreType.DMA,
                        pltpu.SemaphoreType.DMA],
    )
    return pl.pallas_call(
        _dispatch_body,
        grid_spec=grid_spec,
        out_shape=jax.ShapeDtypeStruct((n_rows, d), BF16),
        compiler_params=_params(("arbitrary",), 40 * 2**20),
        name="moe_dispatch",
    )(tail_start, runs, runs, h2, idx)


GMM_RING = 3


def _gmm_body(layer, be_ref, nb_ref, first_ref, next_ref, wslot_ref,
              x_hbm, wg_hbm, wu_hbm, wd_hbm, y_ref,
              x_ring, wg_stage, wu_stage, wd_stage, wg_bf, wu_bf, wd_bf, sem, wsem):
    j = pl.program_id(0)
    n_live = nb_ref[0]

    def fetch(step):
        slot = step % GMM_RING
        return pltpu.make_async_copy(
            x_hbm.at[pl.ds(pl.multiple_of(step * MOE_BLOCK, MOE_BLOCK), MOE_BLOCK)],
            x_ring.at[slot], sem.at[slot])

    def weight_copies(expert, slot):
        return [pltpu.make_async_copy(hbm.at[layer, expert], stage.at[slot], wsem.at[slot, i])
                for i, (hbm, stage) in enumerate(((wg_hbm, wg_stage), (wu_hbm, wu_stage),
                                                  (wd_hbm, wd_stage)))]

    @pl.when(j == 0)
    def _():
        for cp in weight_copies(be_ref[0], 0):
            cp.start()
        for ahead in range(GMM_RING - 1):
            @pl.when(ahead < n_live)
            def _():
                fetch(ahead).start()

    @pl.when(j + (GMM_RING - 1) < n_live)
    def _():
        fetch(j + (GMM_RING - 1)).start()

    @pl.when(j < n_live)
    def _():
        @pl.when(first_ref[j] == 1)
        def _():
            slot = wslot_ref[j]
            for cp in weight_copies(be_ref[j], slot):
                cp.wait()
            wg_bf[...] = wg_stage[slot].astype(BF16)
            wu_bf[...] = wu_stage[slot].astype(BF16)
            wd_bf[...] = wd_stage[slot].astype(BF16)

            @pl.when(next_ref[j] >= 0)
            def _():
                for cp in weight_copies(next_ref[j], 1 - slot):
                    cp.start()

        fetch(j).wait()
        xb = x_ring[j % GMM_RING]
        pre = _dot(xb, wg_bf[...])
        hid = pre * jax.nn.sigmoid(pre) * _dot(xb, wu_bf[...])
        y_ref[...] = _dot(hid.astype(BF16), wd_bf[...]).astype(BF16)

    @pl.when(j >= nb_ref[0])
    def _():
        y_ref[...] = jnp.zeros_like(y_ref)


def _gmm(xs, blk_expert, n_blocks, run_first, run_next, run_slot, p):
    n_rows, d = xs.shape
    nb_max = n_rows // MOE_BLOCK
    any_spec = pl.BlockSpec(memory_space=pl.ANY)
    grid_spec = pltpu.PrefetchScalarGridSpec(
        num_scalar_prefetch=5,
        grid=(nb_max,),
        in_specs=[any_spec] * 4,
        out_specs=pl.BlockSpec((MOE_BLOCK, d), lambda j, *_: (j, 0)),
        scratch_shapes=[pltpu.VMEM((GMM_RING, MOE_BLOCK, d), BF16),
                        pltpu.VMEM((2, d, D_EXPERT), F32), pltpu.VMEM((2, d, D_EXPERT), F32),
                        pltpu.VMEM((2, D_EXPERT, d), F32),
                        pltpu.VMEM((d, D_EXPERT), BF16), pltpu.VMEM((d, D_EXPERT), BF16),
                        pltpu.VMEM((D_EXPERT, d), BF16),
                        pltpu.SemaphoreType.DMA((GMM_RING,)), pltpu.SemaphoreType.DMA((2, 3))],
    )
    return pl.pallas_call(
        functools.partial(_gmm_body, p["layer"]),
        grid_spec=grid_spec,
        out_shape=jax.ShapeDtypeStruct((n_rows, d), BF16),
        compiler_params=_params(("arbitrary",), 24 * 2**20),
        name="moe_gmm",
    )(blk_expert, n_blocks, run_first, run_next, run_slot,
      xs, p["w_exp_gate"], p["w_exp_up"], p["w_exp_down"])


def _combine_body(run_ref, next_run_ref, ys_hbm, idx_ref, wts_ref, x1_ref, g2_ref, l2g_ref,
                  l2b_ref, out_ref, loc_buf, sem):
    nt = MOE_TILE
    step = pl.program_id(0)
    slot = step & 1

    def copies_into(which):
        def make_copy(dst, src, rows):
            return pltpu.make_async_copy(ys_hbm.at[pl.ds(dst, rows)],
                                         loc_buf.at[which, pl.ds(src, rows)], sem.at[which])
        return make_copy

    @pl.when(step == 0)
    def _():
        loc_buf[...] = jnp.zeros_like(loc_buf)
        _for_each_slab(run_ref, copies_into(slot), lambda cp: cp.start())

    @pl.when(step < pl.num_programs(0) - 1)
    def _():
        _for_each_slab(next_run_ref, copies_into(1 - slot), lambda cp: cp.start())

    _for_each_slab(run_ref, copies_into(slot), lambda cp: cp.wait())
    lp = idx_ref[...].astype(F32)
    w = wts_ref[...]
    c = lax.broadcasted_iota(jnp.int32, (nt, MOE_LOCAL_ROWS), 1).astype(F32)
    sel = (jnp.where(c == lp[:, 2:3], w[:, 0:1], 0.0)
           + jnp.where(c == lp[:, 3:4], w[:, 1:2], 0.0))
    sel_hi, sel_lo = _split2(sel)
    rows_bf = loc_buf[slot]
    moe = _dot(sel_hi, rows_bf) + _dot(sel_lo, rows_bf)
    out_ref[...] = _layer_norm(ALPHA * x1_ref[...] + g2_ref[0] * moe, l2g_ref[...], l2b_ref[...])


def _combine(seq_len, ys, idx, wts, runs, x1, g2, p):
    m, d = x1.shape
    nt = MOE_TILE
    tiles_per_seq = seq_len // nt
    tok = lambda n: pl.BlockSpec((nt, n), lambda i: (i, 0))
    n_tiles = m // nt
    run_spec = lambda at: pl.BlockSpec((1, 1, LANES), lambda i: (at(i), 0, 0),
                                       memory_space=pltpu.SMEM)
    return pl.pallas_call(
        _combine_body,
        grid=(n_tiles,),
        in_specs=[run_spec(lambda i: i), run_spec(lambda i: jnp.minimum(i + 1, n_tiles - 1)),
                  pl.BlockSpec(memory_space=pl.ANY), tok(LANES), tok(LANES), tok(d),
                  pl.BlockSpec((1, 1, d), lambda i: (i // tiles_per_seq, 0, 0)),
                  _const_spec((1, d)), _const_spec((1, d))],
        out_specs=tok(d),
        out_shape=jax.ShapeDtypeStruct((m, d), F32),
        scratch_shapes=[pltpu.VMEM((2, MOE_LOCAL_ROWS, d), BF16), pltpu.SemaphoreType.DMA((2,))],
        compiler_params=_params(("arbitrary",), 48 * 2**20),
        name="moe_combine",
    )(runs, runs, ys, idx, wts, x1, g2, p["ln2_g"], p["ln2_b"])


def _moe_routed(seq_len, h2, idx, wts, tile_cnt, x1, g2, p):
    m, d = h2.shape
    blk = MOE_BLOCK
    n_tiles = m // MOE_TILE
    worst_rows = 2 * m + n_tiles * N_EXPERTS * (MOE_RUN_ALIGN - 1) + N_EXPERTS * (blk - 1)
    nb_max = -(-worst_rows // blk)
    cnt = tile_cnt[:, 0, :N_EXPERTS]
    run_len = ((cnt + MOE_RUN_ALIGN - 1) // MOE_RUN_ALIGN) * MOE_RUN_ALIGN
    local_row = jnp.cumsum(run_len, axis=1) - run_len
    rows_before = jnp.cumsum(run_len, axis=0) - run_len
    total = jnp.sum(run_len, axis=0)
    padded = ((total + blk - 1) // blk) * blk
    ends = jnp.cumsum(padded)
    starts = ends - padded
    n_blocks = (ends[-1] // blk).astype(jnp.int32).reshape(1)
    first_row = jnp.arange(nb_max, dtype=jnp.int32) * blk
    blk_expert = jnp.minimum(
        jnp.sum((first_row[:, None] >= ends[None, :]).astype(jnp.int32), axis=1),
        N_EXPERTS - 1).astype(jnp.int32)
    tail_start = jnp.concatenate(
        [jnp.where(padded > 0, ends - blk, -1).astype(jnp.int32), n_blocks])
    runs = jnp.concatenate(
        [starts[None, :] + rows_before, local_row, run_len,
         jnp.zeros((n_tiles, LANES - 3 * N_EXPERTS), jnp.int32)], axis=1).astype(jnp.int32)
    runs = runs.reshape(n_tiles, 1, LANES)
    prev_expert = jnp.concatenate([jnp.full((1,), -1, jnp.int32), blk_expert[:-1]])
    run_first = (blk_expert != prev_expert).astype(jnp.int32)
    run_slot = ((jnp.cumsum(run_first) - 1) % 2).astype(jnp.int32)
    used = padded > 0
    later = jnp.arange(N_EXPERTS)[None, :] > jnp.arange(N_EXPERTS)[:, None]
    next_used = jnp.min(jnp.where(later & used[None, :], jnp.arange(N_EXPERTS)[None, :], N_EXPERTS),
                        axis=1)
    run_next = jnp.where(next_used < N_EXPERTS, next_used, -1).astype(jnp.int32)[blk_expert]
    xs = _dispatch(h2, idx, runs, tail_start, nb_max * blk)
    ys = _gmm(xs, blk_expert, n_blocks, run_first, run_next, run_slot, p)
    return _combine(seq_len, ys, idx, wts, runs, x1, g2, p)


def _prep_layer(l, w_in, mu_shift, w0, w_decay_up, a0, w_iclr_up, k_k, k_a, r_k,
                lnx_g, lnx_b, lnv_g, lnv_b, w_spatial, b_spatial, w_branch_a, w_branch_b,
                w_out, ln1_g, ln1_b, w_route_group, b_route_group, w_route_expert,
                b_route_expert, w_exp_gate, w_exp_up, w_exp_down, ln2_g, ln2_b):
    d = D_MODEL
    pad_a = N_SHIFT_PAD - N_SHIFT
    wi = w_in[l]
    w_in_p = jnp.concatenate(
        [wi[:, :N_SHIFT], jnp.zeros((d, pad_a), F32), wi[:, N_SHIFT:]], axis=1).astype(BF16)
    mu = jnp.concatenate([mu_shift[l], jnp.zeros((pad_a,), F32)])[None]
    lora = jnp.zeros((LANES, 2 * D_A), F32)
    lora = lora.at[:R_LORA, :D_A].set(w_decay_up[l]).at[R_LORA:2 * R_LORA, D_A:].set(w_iclr_up[l])
    seg = jnp.arange(D_A) // HEAD
    row1 = lambda x: x.reshape(1, -1)
    gsz = D_B // N_GROUPS_B
    w_route = jnp.concatenate(
        [w_route_expert[l], w_route_group[l],
         jnp.zeros((d, LANES - N_EXPERTS - N_ROUTE_GROUPS), F32)], axis=1)
    b_route = jnp.concatenate(
        [b_route_expert[l], b_route_group[l],
         jnp.zeros((LANES - N_EXPERTS - N_ROUTE_GROUPS,), F32)])[None]
    return dict(
        w_in=w_in_p, mu=mu, lora=lora,
        w0a0=jnp.concatenate([w0[l], a0[l]])[None],
        k_k=row1(k_k[l]), k_a=row1(k_a[l]), r_k=row1(r_k[l]),
        lnx_g=row1(lnx_g[l]), lnx_b=row1(lnx_b[l]),
        lnv_g=row1(lnv_g[l]), lnv_b=row1(lnv_b[l]),
        bd=(seg[:, None] == seg[None, :]).astype(BF16),
        w_spatial=w_spatial[l],
        b_spatial_full=jnp.repeat(b_spatial[l].T, gsz, axis=1),
        ws_row=jnp.repeat(w_spatial[l][:, 0, 0], gsz)[None],
        bs_row=jnp.repeat(b_spatial[l][:, 0], gsz)[None],
        w_branch_a=w_branch_a[l].astype(BF16), w_branch_b=w_branch_b[l].astype(BF16),
        w_out=w_out[l].astype(BF16), ln1_g=row1(ln1_g[l]), ln1_b=row1(ln1_b[l]),
        w_route=w_route, b_route=b_route,
        w_exp_gate=w_exp_gate, w_exp_up=w_exp_up, w_exp_down=w_exp_down, layer=l,
        ln2_g=row1(ln2_g[l]), ln2_b=row1(ln2_b[l]),
    )


def _trunk(seq_mode, x, mods, wkv_in, shift_in, preps):
    b, t, d = x.shape
    wkv_out, shift_out, v_out = [], [], []
    wkv_acc = None if seq_mode else jnp.zeros(wkv_in.shape, F32)
    for l in range(DEPTH):
        p = preps[l]
        sh1, sc1, g1, sh2, sc2, g2 = mods[l]
        if seq_mode:
            zprev = jnp.zeros((b, 1, N_SHIFT_PAD), F32) if shift_in is None else shift_in[l]
        else:
            zprev = _matmul(shift_in[l], p["w_in"][:, :N_SHIFT_PAD]).reshape(b, t, N_SHIFT_PAD)
        r, lw, kp, v, kap, bb, u, vg, ga, gb, hl = _in_stage(seq_mode, x, sc1, sh1, zprev, p)
        if seq_mode:
            o, s_new = _wkv_seq(r, lw, kp, v, kap, bb, wkv_in[l])
        else:
            flat = lambda a: a.reshape(t, D_A)
            o, wkv_acc = _wkv_step(flat(r), flat(lw), flat(kp), flat(v), flat(kap), flat(bb),
                                   wkv_in[l], wkv_acc, l)
            s_new = None
            o = o.reshape(b, t, D_A)
        x1, h2 = _merge_stage(seq_mode, x, o, r, kp, v, u, vg, ga, gb, g1, sc2, sh2, p)
        m = b * t
        h2f = h2.reshape(m, d)
        gate, idx, wts, tile_cnt = _route(h2f, p)
        if seq_mode:
            x = _moe_routed(t, h2f, idx, wts, tile_cnt, x1.reshape(m, d), g2, p)
        else:
            x = _moe(t, h2f, gate, x1.reshape(m, d), g2, p)
        x = x.reshape(b, t, d)
        wkv_out.append(s_new)
        shift_out.append(hl)
        v_out.append(vg)
    if not seq_mode:
        wkv_out = wkv_acc
    return x, wkv_out, shift_out, v_out


def kernel(x_prompt, x_sample, c_prompt, c_sample, state_wkv, state_shift, w_ada, b_ada, w_in, mu_shift, w0, w_decay_up, a0, w_iclr_up, k_k, k_a, r_k, lnx_g, lnx_b, lnv_g, lnv_b, w_spatial, b_spatial, w_branch_a, w_branch_b, w_out, ln1_g, ln1_b, w_route_group, b_route_group, w_route_expert, b_route_expert, w_exp_gate, w_exp_up, w_exp_down, ln2_g, ln2_b):
    bp, tp, d = x_prompt.shape
    bs = x_sample.shape[0]
    layer_params = (w_in, mu_shift, w0, w_decay_up, a0, w_iclr_up, k_k, k_a, r_k, lnx_g,
                    lnx_b, lnv_g, lnv_b, w_spatial, b_spatial, w_branch_a, w_branch_b, w_out,
                    ln1_g, ln1_b, w_route_group, b_route_group, w_route_expert,
                    b_route_expert, w_exp_gate, w_exp_up, w_exp_down, ln2_g, ln2_b)
    preps = [_prep_layer(l, *layer_params) for l in range(DEPTH)]
    mod_all = _ada(jnp.concatenate([c_prompt, c_sample], axis=0), w_ada, b_ada)
    mods_p, mods_s = [], []
    for l in range(DEPTH):
        parts = jnp.split(mod_all[l], 6, axis=-1)
        mods_p.append([q[:bp].reshape(bp, 1, d) for q in parts])
        mods_s.append([q[bp:].reshape(1, bs, d) for q in parts])

    wkv0 = jnp.zeros((DEPTH, bp, N_HEADS, HEAD, HEAD), F32)
    y_p, wkv_p, shift_p, _ = _trunk(True, x_prompt, mods_p, wkv0, None, preps)
    y_s, wkv_s, shift_s, v_s = _trunk(False, x_sample.reshape(1, bs, d), mods_s, state_wkv,
                                      state_shift, preps)
    return (y_p,
            y_s.reshape(bs, 1, d),
            jnp.stack(wkv_p),
            jnp.stack([s.reshape(bp, d) for s in shift_p]),
            wkv_s,
            jnp.stack([s.reshape(bs, d) for s in shift_s]),
            jnp.stack([q.reshape(bs, 1, D_B) for q in v_s]))
```

```python
import functools

import jax
import jax.numpy as jnp
from jax import lax
from jax.experimental import pallas as pl
from jax.experimental.pallas import tpu as pltpu

F32 = jnp.float32
BF16 = jnp.bfloat16

D_MODEL = 1024
DEPTH = 2
HEAD = 64
N_HEADS = 8
D_A = N_HEADS * HEAD
R_LORA = 32
CHUNK = 128
N_GROUPS_B = 8
D_B = 512
N_SHIFT = 3 * D_A + 2 * R_LORA
N_ROUTE_GROUPS = 4
EXP_PER_GROUP = 8
N_EXPERTS = N_ROUTE_GROUPS * EXP_PER_GROUP
D_EXPERT = 256
ALPHA = (2 * DEPTH) ** 0.25
LN_EPS = 1e-5
GN_EPS = 64e-5

LANES = 128
SUBLANES = 8
MIB = 2 ** 20
N_SHIFT_PAD = 13 * LANES
COL_U = N_SHIFT_PAD
COL_VG = COL_U + D_B
COL_GA = COL_VG + D_B
COL_GB = COL_GA + D_MODEL
N_IN_PAD = COL_GB + D_MODEL
WKV_CHUNK = 64
WKV_TILE = 256
WKV_STEP_ROWS = 8
MERGE_TILE = 512
DENSE_MOE_TILE = 1024
ADA_COLS = 1024
IN_ROW_GROUPS = 4
IN_GROUP_ROWS = 128
VMEM_CAP_BYTES = 60000 * 1024

_NN = (((1,), (0,)), ((), ()))
_B_NT = (((2,), (2,)), ((0,), (0,)))
_B_NN = (((2,), (1,)), ((0,), (0,)))
_B_TN = (((1,), (1,)), ((0,), (0,)))


def _dot(a, b, dims=_NN):
    return lax.dot_general(a, b, dims, preferred_element_type=F32)


def _split2(x):
    hi = x.astype(BF16)
    lo = (x - hi.astype(F32)).astype(BF16)
    return hi, lo


def _mm1(a, b, dims=_NN):
    return _dot(a.astype(BF16), b.astype(BF16), dims)


def _mm3(a, b, dims=_NN):
    ah, al = _split2(a)
    bh, bl = _split2(b)
    return _dot(ah, bh, dims) + (_dot(ah, bl, dims) + _dot(al, bh, dims))


def _segsum(x, bd, two_terms=False):
    if not two_terms:
        return _dot(x.astype(BF16), bd)
    hi, lo = _split2(x)
    return _dot(hi, bd) + _dot(lo, bd)


def _layer_norm(x, g, b):
    mu = jnp.mean(x, axis=-1, keepdims=True)
    d = x - mu
    var = jnp.mean(d * d, axis=-1, keepdims=True)
    return d * lax.rsqrt(var + LN_EPS) * g + b


def _gelu(x):
    return 0.5 * x * (1.0 + lax.erf(x * 0.7071067811865476))


def _params(sem, est_bytes):
    limit = int(min(VMEM_CAP_BYTES, max(est_bytes, 16 * 1024 * 1024)))
    return pltpu.CompilerParams(dimension_semantics=sem, vmem_limit_bytes=limit)


def _const_spec(shape, single_buffer=False):
    nd = len(shape)
    if single_buffer:
        return pl.BlockSpec(shape, lambda *_: (0,) * nd, pipeline_mode=pl.Buffered(1))
    return pl.BlockSpec(shape, lambda *_: (0,) * nd)


def _ada_body(c_ref, w_ref, b_ref, o_ref):
    c = c_ref[...]
    s = c * jax.nn.sigmoid(c)
    o_ref[0] = _mm3(s, w_ref[0]) + b_ref[0]


def _ada(c_all, w_ada, b_ada):
    depth, d, n6 = w_ada.shape
    m = c_all.shape[0]
    tn = ADA_COLS
    return pl.pallas_call(
        _ada_body,
        grid=(depth, n6 // tn),
        in_specs=[
            pl.BlockSpec((m, d), lambda l, j: (0, 0)),
            pl.BlockSpec((1, d, tn), lambda l, j: (l, 0, j)),
            pl.BlockSpec((1, 1, tn), lambda l, j: (l, 0, j)),
        ],
        out_specs=pl.BlockSpec((1, m, tn), lambda l, j: (l, 0, j)),
        out_shape=jax.ShapeDtypeStruct((depth, m, n6), F32),
        compiler_params=_params(("arbitrary", "arbitrary"), 24 * MIB),
        name="ada_mod",
    )(c_all, w_ada, b_ada.reshape(depth, 1, n6))


def _mm_body(x_ref, w_ref, o_ref):
    o_ref[...] = _mm1(x_ref[...], w_ref[...])


def _matmul(x, w):
    m, k = x.shape
    n = w.shape[1]
    return pl.pallas_call(
        _mm_body,
        grid=(1,),
        in_specs=[_const_spec((m, k)), _const_spec((k, n))],
        out_specs=_const_spec((m, n)),
        out_shape=jax.ShapeDtypeStruct((m, n), F32),
        compiler_params=_params(("arbitrary",), 24 * MIB),
        name="shift_proj",
    )(x, w)


def _in_body(seq_mode, tm, x_ref, sc_ref, sh_ref, w_ref, mu_ref, zp_ref, lora_ref,
             w0a0_ref, kk_ref, ka_ref, lng_ref, lnb_ref, bd_ref,
             r_ref, lw_ref, kp_ref, v_ref, kap_ref, bb_ref, u_ref, vg_ref,
             ga_ref, gb_ref, hl_ref, carry_ref):
    nsplit = IN_ROW_GROUPS if seq_mode else 1
    rows = tm // nsplit
    if seq_mode:
        @pl.when(pl.program_id(1) == 0)
        def _():
            carry_ref[...] = zp_ref[0]

        carry = carry_ref[...]
    bd = bd_ref[...]
    for part in range(nsplit):
        sl = slice(part * rows, (part + 1) * rows)
        h = x_ref[0, sl, :] * (1.0 + sc_ref[0]) + sh_ref[0]
        hb = h.astype(BF16)
        proj = lambda lo, hi: _dot(hb, w_ref[:, lo:hi])
        za = proj(0, N_SHIFT_PAD)
        if seq_mode:
            row = lax.broadcasted_iota(jnp.int32, za.shape, 0)
            prev = jnp.where(row == 0, carry, pltpu.roll(za, 1, 0))
            carry = za[rows - 1:rows, :]
        else:
            prev = zp_ref[0]
        mix = za + mu_ref[...] * (prev - za)
        r = mix[:, 0:D_A]
        k = mix[:, D_A:2 * D_A]
        v = mix[:, 2 * D_A:3 * D_A]
        xwa = mix[:, 3 * D_A:N_SHIFT_PAD]
        lane = lax.broadcasted_iota(jnp.int32, xwa.shape, 1)
        lora_in = jnp.where(lane < R_LORA, jnp.tanh(xwa), xwa)
        pre = w0a0_ref[...] + _mm1(lora_in, lora_ref[...])
        yw = -pre[:, :D_A]
        softplus = jnp.maximum(yw, 0.0) + jnp.log1p(jnp.exp(-jnp.abs(yw)))
        lw = -jnp.exp(-softplus - 0.5)
        a = jax.nn.sigmoid(pre[:, D_A:])
        kk = k * kk_ref[...]
        kap = kk / jnp.maximum(jnp.sqrt(_segsum(kk * kk, bd)), 1e-12)
        r_ref[0, sl, :] = r
        lw_ref[0, sl, :] = lw
        kp_ref[0, sl, :] = k * (1.0 + (a - 1.0) * ka_ref[...])
        v_ref[0, sl, :] = v
        kap_ref[0, sl, :] = kap
        bb_ref[0, sl, :] = kap * a
        u_ref[0, sl, :] = _gelu(proj(COL_U, COL_VG)).astype(BF16)
        vg_ref[0, sl, :] = _layer_norm(_gelu(proj(COL_VG, COL_GA)), lng_ref[...], lnb_ref[...])
        ga_ref[0, sl, :] = jax.nn.sigmoid(proj(COL_GA, COL_GB)).astype(BF16)
        gb_ref[0, sl, :] = jax.nn.sigmoid(proj(COL_GB, N_IN_PAD)).astype(BF16)
    if seq_mode:
        carry_ref[...] = carry
        hl_ref[0] = h[rows - 1:rows, :]
    else:
        hl_ref[0] = h


def _in_stage(seq_mode, x, sc, sh, zprev, p):
    b, t, d = x.shape
    tm = IN_ROW_GROUPS * IN_GROUP_ROWS if seq_mode else t
    tmod = 1 if seq_mode else tm
    grid = (b, t // tm)
    tok = lambda n: pl.BlockSpec((1, tm, n), lambda i, j: (i, j, 0))
    mod = pl.BlockSpec((1, tmod, d), lambda i, j: (i, j if not seq_mode else 0, 0))
    zp_spec = (pl.BlockSpec((1, 1, N_SHIFT_PAD), lambda i, j: (i, 0, 0)) if seq_mode
               else tok(N_SHIFT_PAD))
    hl_spec = (pl.BlockSpec((1, 1, d), lambda i, j: (i, 0, 0)) if seq_mode else tok(d))
    hl_shape = (b, 1, d) if seq_mode else (b, t, d)
    out_cols = [D_A] * 6 + [D_B] * 2 + [d] * 2
    out_dtypes = [F32] * 6 + [BF16, F32] + [BF16] * 2
    est = (2 * tm * (d + N_SHIFT_PAD + sum(out_cols) + d) * 4 + 2 * d * N_IN_PAD * 2
           + 3 * tm * N_IN_PAD * 4 + 4 * MIB)
    outs = pl.pallas_call(
        functools.partial(_in_body, seq_mode, tm),
        grid=grid,
        in_specs=[tok(d), mod, mod,
                  _const_spec((d, N_IN_PAD), True), _const_spec((1, N_SHIFT_PAD)), zp_spec,
                  _const_spec((LANES, 2 * D_A)), _const_spec((1, 2 * D_A)),
                  _const_spec((1, D_A)), _const_spec((1, D_A)),
                  _const_spec((1, D_B)), _const_spec((1, D_B)),
                  _const_spec((D_A, D_A))],
        out_specs=[tok(n) for n in out_cols] + [hl_spec],
        out_shape=[jax.ShapeDtypeStruct((b, t, n), dt) for n, dt in zip(out_cols, out_dtypes)]
        + [jax.ShapeDtypeStruct(hl_shape, F32)],
        scratch_shapes=[pltpu.VMEM((1, N_SHIFT_PAD), F32)],
        compiler_params=_params(("arbitrary", "arbitrary"), est),
        name="in_stage_seq" if seq_mode else "in_stage_row",
    )(x, sc, sh, p["w_in"], p["mu"], zprev, p["lora"], p["w0a0"], p["k_k"], p["k_a"],
      p["lnv_g"], p["lnv_b"], p["bd"])
    return outs


def _chunk_pairs(x, nsub):
    c = WKV_CHUNK
    w = 2 * HEAD
    return jnp.stack([x[c * i:c * (i + 1), w * q:w * (q + 1)]
                      for i in range(nsub) for q in range(N_HEADS // 2)], axis=0)


def _masked_rowsum(mask_bf, x):
    h1, h2 = _split2(x)
    return _dot(mask_bf, h1) + _dot(mask_bf, h2)


def _masked_rowsum_t(x, mask):
    h1, h2 = _split2(x)
    m = mask.astype(BF16)
    return _dot(h1, m) + _dot(h2, m)


def _wkv_seq_body(nsub, r_ref, lw_ref, kp_ref, v_ref, kap_ref, bb_ref, s0_ref,
                  o_ref, sout_ref, s_scr):
    c = WKV_CHUNK
    tc = nsub * c
    npair = N_HEADS // 2

    @pl.when(pl.program_id(1) == 0)
    def _():
        for q in range(npair):
            s_scr[q] = jnp.concatenate([s0_ref[0, 2 * q], s0_ref[0, 2 * q + 1]], axis=-1)

    row_t = lax.broadcasted_iota(jnp.int32, (tc, tc), 0)
    col_t = lax.broadcasted_iota(jnp.int32, (tc, tc), 1)
    shift = c.bit_length() - 1
    same_chunk = (row_t >> shift) == (col_t >> shift)
    lw = lw_ref[0]
    g = _masked_rowsum((same_chunk & (row_t >= col_t)).astype(BF16), lw)
    g_end = _masked_rowsum(same_chunk.astype(BF16), lw)
    e_neg = jnp.exp(-g)
    e_end = jnp.exp(g_end - g)
    cp = functools.partial(_chunk_pairs, nsub=nsub)
    k = kp_ref[0]
    b = bb_ref[0]
    kap_t = cp(kap_ref[0] * jnp.exp(g - lw))
    b_t = cp(b * e_neg)
    k_t = cp(k * e_neg)
    r_t = cp(r_ref[0] * jnp.exp(g))
    b_e = cp(b * e_end)
    k_e = cp(k * e_end)
    vv = cp(v_ref[0])
    decay_end = cp(jnp.exp(g_end))

    row = lax.broadcasted_iota(jnp.int32, (c, 2 * c), 0)[None]
    lane = lax.broadcasted_iota(jnp.int32, (c, 2 * c), 1)[None]
    colp = lane & (c - 1)
    right = lane >= c
    row2 = lax.broadcasted_iota(jnp.int32, (2 * c, 2 * c), 0)[None]
    lane2 = lax.broadcasted_iota(jnp.int32, (2 * c, 2 * c), 1)[None]
    same_head = (row2 >= c) == (lane2 >= c)

    def bd(x):
        return jnp.concatenate([jnp.where(right, 0.0, x), jnp.where(right, x, 0.0)], axis=1)

    p_b = _mm1(jnp.concatenate([kap_t, r_t], axis=1), bd(b_t), _B_NT)
    l_b = jnp.where(row > colp, p_b[:, :c], 0.0)
    a_rb = jnp.where(row >= colp, p_b[:, c:], 0.0)
    bd_k = bd(k_t)
    l_k = jnp.where(row > colp, _mm1(kap_t, bd_k, _B_NT), 0.0)
    a_rk = jnp.where(row >= colp, _mm1(r_t, bd_k, _B_NT), 0.0)
    m = -l_b
    t_inv = jnp.where(row == colp, 1.0, 0.0) + m
    m = _mm1(m, bd(m), _B_NN)
    span = 2
    while 2 * span < c:
        both = _mm1(jnp.concatenate([m, t_inv], axis=1), bd(m), _B_NN)
        m = both[:, :c]
        t_inv = t_inv + both[:, c:]
        span *= 2
    t_inv = t_inv + _mm1(t_inv, bd(m), _B_NN)
    bd_v = bd(vv)
    a1 = -_mm1(t_inv, bd(kap_t), _B_NN)
    u0 = -_mm1(t_inv, bd(_mm1(l_k, bd_v, _B_NN)), _B_NN)
    a2 = r_t + _mm1(a_rb, bd(a1), _B_NN)
    o0 = _mm1(a_rb, bd(u0), _B_NN) + _mm1(a_rk, bd_v, _B_NN)
    g_bd = (jnp.where(same_head, _mm1(a1, b_e, _B_TN), 0.0)
            + jnp.where(row2 == lane2, decay_end[:, 0:1, :], 0.0))
    hh = _mm1(jnp.concatenate([u0, vv], axis=1),
              jnp.concatenate([b_e, k_e], axis=1), _B_TN)
    h_pair = jnp.where(right, hh[:, c:], hh[:, :c])

    s = s_scr[...]
    for i in range(nsub):
        ps = slice(npair * i, npair * (i + 1))
        o = _mm1(a2[ps], bd(s), _B_NT) + o0[ps]
        s = _mm1(s, g_bd[ps], _B_NN) + h_pair[ps]
        for q in range(npair):
            o_ref[0, c * i:c * (i + 1), 2 * HEAD * q:2 * HEAD * (q + 1)] = o[q]
    s_scr[...] = s

    @pl.when(pl.program_id(1) == pl.num_programs(1) - 1)
    def _():
        for q in range(npair):
            sout_ref[0, 2 * q] = s[q][:, :HEAD]
            sout_ref[0, 2 * q + 1] = s[q][:, HEAD:]


def _wkv_seq(r, lw, kp, v, kap, bb, s0):
    b, t, _ = r.shape
    tc = WKV_TILE
    nsub = tc // WKV_CHUNK
    tok = pl.BlockSpec((1, tc, D_A), lambda i, j: (i, j, 0))
    st = pl.BlockSpec((1, N_HEADS, HEAD, HEAD), lambda i, j: (i, 0, 0, 0))
    return pl.pallas_call(
        functools.partial(_wkv_seq_body, nsub),
        grid=(b, t // tc),
        in_specs=[tok] * 6 + [st],
        out_specs=[tok, st],
        out_shape=[jax.ShapeDtypeStruct((b, t, D_A), F32),
                   jax.ShapeDtypeStruct((b, N_HEADS, HEAD, HEAD), F32)],
        scratch_shapes=[pltpu.VMEM((N_HEADS // 2, HEAD, 2 * HEAD), F32)],
        compiler_params=_params(("arbitrary", "arbitrary"), 32 * MIB),
        name="wkv_seq",
    )(r, lw, kp, v, kap, bb, s0)


def _wkv_step_body(r_ref, lw_ref, kp_ref, v_ref, kap_ref, bb_ref, s0_ref, _, o_ref, sout_ref):
    eye = (lax.broadcasted_iota(jnp.int32, (HEAD, HEAD), 0)
           == lax.broadcasted_iota(jnp.int32, (HEAD, HEAD), 1))[None]
    for h in range(N_HEADS):
        hs = slice(HEAD * h, HEAD * (h + 1))
        row = lambda ref: ref[:, hs][:, None, :]
        s = s0_ref[:, h]
        sa = -jnp.sum(s * row(kap_ref), axis=-1, keepdims=True)
        v_col = jnp.sum(jnp.where(eye, row(v_ref), 0.0), axis=-1, keepdims=True)
        s_new = s * jnp.exp(row(lw_ref)) + sa * row(bb_ref) + v_col * row(kp_ref)
        sout_ref[:, h] = s_new
        o_col = jnp.sum(s_new * row(r_ref), axis=-1, keepdims=True)
        o_ref[:, hs] = jnp.sum(jnp.where(eye, o_col, 0.0), axis=1)


def _wkv_step(r, lw, kp, v, kap, bb, s0, s_new_all, layer):
    m = r.shape[0]
    nb = WKV_STEP_ROWS
    tok = pl.BlockSpec((nb, D_A), lambda i: (i, 0))
    st = pl.BlockSpec((nb, N_HEADS, HEAD, HEAD), lambda i: (i, 0, 0, 0))
    st_out = pl.BlockSpec((None, nb, N_HEADS, HEAD, HEAD), lambda i: (layer, i, 0, 0, 0))
    return pl.pallas_call(
        _wkv_step_body,
        grid=(m // nb,),
        in_specs=[tok] * 6 + [st, pl.BlockSpec(memory_space=pl.ANY)],
        out_specs=[tok, st_out],
        out_shape=[jax.ShapeDtypeStruct((m, D_A), F32),
                   jax.ShapeDtypeStruct(s_new_all.shape, F32)],
        input_output_aliases={7: 1},
        compiler_params=_params(("arbitrary",), 24 * MIB),
        name="wkv_step",
    )(r, lw, kp, v, kap, bb, s0, s_new_all)


def _merge_body(seq_mode, tm, x_ref, o_ref, r_ref, kp_ref, v_ref, u_ref, vg_ref,
                ga_ref, gb_ref, g1_ref, sc2_ref, sh2_ref, lxg_ref, lxb_ref, rk_ref,
                bd_ref, ws_ref, bs_ref, pa_ref, pb_ref, wo_ref, l1g_ref, l1b_ref,
                x1_ref, h2_ref):
    bd = bd_ref[...]
    o = o_ref[0]
    inv_n = 1.0 / HEAD
    mu = _segsum(o, bd, two_terms=True) * inv_n
    d = o - mu
    var = _segsum(d * d, bd) * inv_n
    on = d * lax.rsqrt(var + GN_EPS) * lxg_ref[...] + lxb_ref[...]
    v = v_ref[0]
    o_a = on + _segsum(r_ref[0] * kp_ref[0] * rk_ref[...], bd) * v
    vg = vg_ref[0]
    if seq_mode:
        row = lax.broadcasted_iota(jnp.int32, (CHUNK, CHUNK), 0)
        col = lax.broadcasted_iota(jnp.int32, (CHUNK, CHUNK), 1)
        lane = lax.broadcasted_iota(jnp.int32, (CHUNK, D_B), 1)
        gsz = D_B // N_GROUPS_B
        w_cat = jnp.concatenate(
            [jnp.where(row >= col, ws_ref[g], 0.0).astype(BF16) for g in range(N_GROUPS_B)], axis=1)
        pieces = []
        for j in range(tm // CHUNK):
            vc = vg[j * CHUNK:(j + 1) * CHUNK, :].astype(BF16)
            v_bd = jnp.concatenate(
                [jnp.where((lane >= g * gsz) & (lane < (g + 1) * gsz), vc, 0.0)
                 for g in range(N_GROUPS_B)], axis=0)
            pieces.append(_dot(w_cat, v_bd) + bs_ref[...])
        s = jnp.concatenate(pieces, axis=0) if len(pieces) > 1 else pieces[0]
    else:
        s = vg * ws_ref[...] + bs_ref[...]
    o_b = u_ref[0] * s
    y = _mm1(ga_ref[0] * _mm1(o_a, pa_ref[...]) + gb_ref[0] * _mm1(o_b, pb_ref[...]),
             wo_ref[...])
    x1 = _layer_norm(ALPHA * x_ref[0] + g1_ref[0] * y, l1g_ref[...], l1b_ref[...])
    x1_ref[0] = x1
    h2_ref[0] = x1 * (1.0 + sc2_ref[0]) + sh2_ref[0]


def _merge_stage(seq_mode, x, o, r, kp, v, u, vg, ga, gb, g1, sc2, sh2, p):
    b, t, d = x.shape
    tm = MERGE_TILE if seq_mode else t
    tmod = 1 if seq_mode else tm
    tok = lambda n: pl.BlockSpec((1, tm, n), lambda i, j: (i, j, 0))
    mod = pl.BlockSpec((1, tmod, d), lambda i, j: (i, j if not seq_mode else 0, 0))
    ws, bs = (p["w_spatial"], p["b_spatial_full"]) if seq_mode else (p["ws_row"], p["bs_row"])
    est = 2 * tm * (2 * d + 7 * D_A + 2 * d + 2 * d) * 4 + 8 * tm * d * 4 + 16 * MIB
    return pl.pallas_call(
        functools.partial(_merge_body, seq_mode, tm),
        grid=(b, t // tm),
        in_specs=[tok(d)] + [tok(D_A)] * 6 + [tok(d), tok(d), mod, mod, mod,
                  _const_spec((1, D_A)), _const_spec((1, D_A)), _const_spec((1, D_A)),
                  _const_spec((D_A, D_A)), _const_spec(ws.shape), _const_spec(bs.shape),
                  _const_spec((D_A, d)), _const_spec((D_B, d)), _const_spec((d, d)),
                  _const_spec((1, d)), _const_spec((1, d))],
        out_specs=[tok(d), tok(d)],
        out_shape=[jax.ShapeDtypeStruct((b, t, d), F32)] * 2,
        compiler_params=_params(("arbitrary", "arbitrary"), est),
        name="merge_seq" if seq_mode else "merge_row",
    )(x, o, r, kp, v, u, vg, ga, gb, g1, sc2, sh2, p["lnx_g"], p["lnx_b"], p["r_k"],
      p["bd"], ws, bs, p["w_branch_a"], p["w_branch_b"], p["w_out"], p["ln1_g"], p["ln1_b"])


def _route_body(h_ref, w_ref, b_ref, gate_ref, idx_ref, wts_ref, cnt_ref):
    lg = _mm3(h_ref[...], w_ref[...]) + b_ref[...]
    lane = lax.broadcasted_iota(jnp.int32, lg.shape, 1)
    lanef = lane.astype(F32)
    neg = -jnp.inf
    is_g = (lane >= N_EXPERTS) & (lane < N_EXPERTS + N_ROUTE_GROUPS)
    mg = jnp.max(jnp.where(is_g, lg, neg), axis=-1, keepdims=True)
    gidx = jnp.min(jnp.where(is_g & (lg == mg), lanef - N_EXPERTS, 1e9), axis=-1, keepdims=True)
    pg_sel = 1.0 / jnp.sum(jnp.where(is_g, jnp.exp(lg - mg), 0.0), axis=-1, keepdims=True)
    lo = gidx * EXP_PER_GROUP
    in_grp = (lanef >= lo) & (lanef < lo + EXP_PER_GROUP)
    t1 = jnp.max(jnp.where(in_grp, lg, neg), axis=-1, keepdims=True)
    i1 = jnp.min(jnp.where(in_grp & (lg == t1), lanef, 1e9), axis=-1, keepdims=True)
    rest = in_grp & (lanef != i1)
    t2 = jnp.max(jnp.where(rest, lg, neg), axis=-1, keepdims=True)
    i2 = jnp.min(jnp.where(rest & (lg == t2), lanef, 1e9), axis=-1, keepdims=True)
    e2 = jnp.exp(t2 - t1)
    w1 = pg_sel / (1.0 + e2)
    w2 = pg_sel * e2 / (1.0 + e2)
    gate_ref[...] = jnp.where(lanef == i1, w1, 0.0) + jnp.where(lanef == i2, w2, 0.0)
    tm = lg.shape[0]
    hit = (lanef == i1) | (lanef == i2)
    earlier = (lax.broadcasted_iota(jnp.int32, (tm, tm), 0)
               > lax.broadcasted_iota(jnp.int32, (tm, tm), 1))
    rank = _dot(earlier.astype(BF16), hit.astype(BF16))
    cnt = jnp.sum(hit.astype(F32), axis=0, keepdims=True)
    run_len = jnp.floor((cnt + (MOE_RUN_ALIGN - 1.0)) * (1.0 / MOE_RUN_ALIGN)) * MOE_RUN_ALIGN
    lower_expert = (lax.broadcasted_iota(jnp.int32, (LANES, LANES), 0)
                    < lax.broadcasted_iota(jnp.int32, (LANES, LANES), 1))
    run_start = _masked_rowsum_t(jnp.broadcast_to(run_len, (SUBLANES, LANES)), lower_expert)[0:1]
    lpos = run_start + rank
    lp1 = jnp.sum(jnp.where(lanef == i1, lpos, 0.0), axis=-1, keepdims=True)
    lp2 = jnp.sum(jnp.where(lanef == i2, lpos, 0.0), axis=-1, keepdims=True)
    cnt_ref[0] = cnt.astype(jnp.int32)
    idx = jnp.where(lane == 0, i1, jnp.where(lane == 1, i2, jnp.where(lane == 2, lp1, lp2)))
    idx_ref[...] = idx.astype(jnp.int32)
    wts_ref[...] = jnp.where(lane == 0, w1, w2)


def _route(h2, p):
    m, d = h2.shape
    tm = min(m, MOE_TILE)
    tok = pl.BlockSpec((tm, LANES), lambda i: (i, 0))
    return pl.pallas_call(
        _route_body,
        grid=(m // tm,),
        in_specs=[pl.BlockSpec((tm, d), lambda i: (i, 0)),
                  _const_spec((d, LANES)), _const_spec((1, LANES))],
        out_specs=[tok, tok, tok, pl.BlockSpec((1, 1, LANES), lambda i: (i, 0, 0))],
        out_shape=[jax.ShapeDtypeStruct((m, LANES), F32),
                   jax.ShapeDtypeStruct((m, LANES), jnp.int32),
                   jax.ShapeDtypeStruct((m, LANES), F32),
                   jax.ShapeDtypeStruct((m // tm, 1, LANES), jnp.int32)],
        compiler_params=_params(("arbitrary",), 24 * MIB),
        name="route",
    )(h2, p["w_route"], p["b_route"])


def _moe_body(h_ref, gate_ref, wg_ref, wu_ref, wd_ref, x1_ref, g2_ref, l2g_ref, l2b_ref,
              out_ref, acc_ref, xb_ref):
    e = pl.program_id(1)

    @pl.when(e == 0)
    def _():
        acc_ref[...] = jnp.zeros_like(acc_ref)
        xb_ref[...] = h_ref[...].astype(BF16)

    xb = xb_ref[...]
    pre = _dot(xb, wg_ref[0].astype(BF16))
    hid = pre * jax.nn.sigmoid(pre) * _dot(xb, wu_ref[0].astype(BF16))
    ye = _mm1(hid, wd_ref[0])
    gate = gate_ref[...]
    lane = lax.broadcasted_iota(jnp.int32, gate.shape, 1)
    ge = jnp.sum(jnp.where(lane == e, gate, 0.0), axis=-1, keepdims=True)
    acc_ref[...] += ge * ye

    @pl.when(e == pl.num_programs(1) - 1)
    def _():
        out_ref[...] = _layer_norm(ALPHA * x1_ref[...] + g2_ref[0] * acc_ref[...],
                                   l2g_ref[...], l2b_ref[...])


def _moe(seq_len, h2, gate, x1, g2, p):
    m, d = h2.shape
    layer = p["layer"]
    tm = min(seq_len if g2.shape[1] == 1 else m, DENSE_MOE_TILE)
    tok = lambda n: pl.BlockSpec((tm, n), lambda i, e: (i, 0))
    if g2.shape[1] == 1:
        tiles_per_seq = seq_len // tm
        g2_spec = pl.BlockSpec((1, 1, d), lambda i, e: (i // tiles_per_seq, 0, 0))
    else:
        g2_spec = pl.BlockSpec((1, tm, d), lambda i, e: (0, i, 0))
    est = 2 * tm * (3 * d + LANES) * 4 + tm * d * 6 + 4 * tm * d * 4 + 8 * MIB
    return pl.pallas_call(
        _moe_body,
        grid=(m // tm, N_EXPERTS),
        in_specs=[tok(d), tok(LANES),
                  pl.BlockSpec((None, 1, d, D_EXPERT), lambda i, e: (layer, e, 0, 0)),
                  pl.BlockSpec((None, 1, d, D_EXPERT), lambda i, e: (layer, e, 0, 0)),
                  pl.BlockSpec((None, 1, D_EXPERT, d), lambda i, e: (layer, e, 0, 0)),
                  tok(d), g2_spec, _const_spec((1, d)), _const_spec((1, d))],
        out_specs=tok(d),
        out_shape=jax.ShapeDtypeStruct((m, d), F32),
        scratch_shapes=[pltpu.VMEM((tm, d), F32), pltpu.VMEM((tm, d), BF16)],
        compiler_params=_params(("arbitrary", "arbitrary"), est),
        name="moe_dense",
    )(h2, gate, p["w_exp_gate"], p["w_exp_up"], p["w_exp_down"], x1, g2, p["ln2_g"], p["ln2_b"])


MOE_TILE = 512
MOE_BLOCK = 512
MOE_RUN_ALIGN = 16
MOE_LOCAL_ROWS = 2 * MOE_TILE + 512
MOE_SLABS = tuple(2 ** k for k in range(9, 3, -1))


def _for_each_slab(run_ref, make_copy, fn):
    def one_run(e, c):
        dst = run_ref[0, 0, e]
        src = run_ref[0, 0, N_EXPERTS + e]
        n = run_ref[0, 0, 2 * N_EXPERTS + e]
        for slab in MOE_SLABS:
            off = n & (-2 * slab)

            @pl.when((n & slab) != 0)
            def _():
                fn(make_copy(pl.multiple_of(dst + off, MOE_RUN_ALIGN),
                             pl.multiple_of(src + off, MOE_RUN_ALIGN), slab))
        return c

    lax.fori_loop(0, N_EXPERTS, one_run, 0)


def _dispatch_body(tail_ref, run_ref, prev_run_ref, h_ref, idx_ref, xs_hbm, zero_buf, loc_buf,
                   sem, zsem, usem):
    nb_max = xs_hbm.shape[0] // MOE_BLOCK
    nt = MOE_TILE

    def zero_copy(row0, zero_sem):
        return pltpu.make_async_copy(
            zero_buf, xs_hbm.at[pl.ds(pl.multiple_of(row0, MOE_BLOCK), MOE_BLOCK)], zero_sem)

    def each_unused_copy(fn):
        def unused(j, c):
            fn(zero_copy(j * MOE_BLOCK, usem))
            return c

        lax.fori_loop(tail_ref[N_EXPERTS], nb_max, unused, 0)

    @pl.when(pl.program_id(0) == 0)
    def _():
        zero_buf[...] = jnp.zeros_like(zero_buf)

        def each_tail_copy(fn):
            for e in range(N_EXPERTS):
                @pl.when(tail_ref[e] >= 0)
                def _():
                    fn(zero_copy(tail_ref[e], zsem))

        each_tail_copy(lambda cp: cp.start())
        each_unused_copy(lambda cp: cp.start())
        each_tail_copy(lambda cp: cp.wait())

    lp = idx_ref[...].astype(F32)
    eye = (lax.broadcasted_iota(jnp.int32, (nt, nt), 0)
           == lax.broadcasted_iota(jnp.int32, (nt, nt), 1))
    as_row = lambda col: jnp.sum(jnp.where(eye, col, 0.0), axis=0, keepdims=True)
    r = lax.broadcasted_iota(jnp.int32, (MOE_LOCAL_ROWS, nt), 0).astype(F32)
    pick = (r == as_row(lp[:, 2:3])) | (r == as_row(lp[:, 3:4]))
    step = pl.program_id(0)
    slot = step & 1
    loc_buf[slot] = _dot(pick.astype(BF16), h_ref[...].astype(BF16)).astype(BF16)

    def copies_from(which):
        def make_copy(dst, src, rows):
            return pltpu.make_async_copy(loc_buf.at[which, pl.ds(src, rows)],
                                         xs_hbm.at[pl.ds(dst, rows)], sem.at[which])
        return make_copy

    _for_each_slab(run_ref, copies_from(slot), lambda cp: cp.start())

    @pl.when(step > 0)
    def _():
        _for_each_slab(prev_run_ref, copies_from(1 - slot), lambda cp: cp.wait())

    @pl.when(step == pl.num_programs(0) - 1)
    def _():
        _for_each_slab(run_ref, copies_from(slot), lambda cp: cp.wait())
        each_unused_copy(lambda cp: cp.wait())


def _dispatch(h2, idx, runs, tail_start, n_rows):
    m, d = h2.shape
    nt = MOE_TILE
    run_spec = lambda at: pl.BlockSpec((1, 1, LANES), lambda i, tail: (at(i), 0, 0),
                                       memory_space=pltpu.SMEM)
    grid_spec = pltpu.PrefetchScalarGridSpec(
        num_scalar_prefetch=1,
        grid=(m // nt,),
        in_specs=[run_spec(lambda i: i), run_spec(lambda i: jnp.maximum(i - 1, 0)),
                  pl.BlockSpec((nt, d), lambda i, tail: (i, 0)),
                  pl.BlockSpec((nt, LANES), lambda i, tail: (i, 0))],
        out_specs=pl.BlockSpec(memory_space=pl.ANY),
        scratch_shapes=[pltpu.VMEM((MOE_BLOCK, d), BF16),
                        pltpu.VMEM((2, MOE_LOCAL_ROWS, d), BF16),
                        pltpu.SemaphoreType.DMA((2,)), pltpu.SemaphoreType.DMA,
                        pltpu.SemaphoreType.DMA],
    )
    return pl.pallas_call(
        _dispatch_body,
        grid_spec=grid_spec,
        out_shape=jax.ShapeDtypeStruct((n_rows, d), BF16),
        compiler_params=_params(("arbitrary",), 40 * MIB),
        name="moe_dispatch",
    )(tail_start, runs, runs, h2, idx)


GMM_RING = 3


def _gmm_body(layer, be_ref, nb_ref, first_ref, next_ref, wslot_ref,
              x_hbm, wg_hbm, wu_hbm, wd_hbm, y_ref,
              x_ring, wg_stage, wu_stage, wd_stage, wg_bf, wu_bf, wd_bf, sem, wsem):
    j = pl.program_id(0)
    n_live = nb_ref[0]

    def fetch(step):
        slot = step % GMM_RING
        return pltpu.make_async_copy(
            x_hbm.at[pl.ds(pl.multiple_of(step * MOE_BLOCK, MOE_BLOCK), MOE_BLOCK)],
            x_ring.at[slot], sem.at[slot])

    def weight_copies(expert, slot):
        return [pltpu.make_async_copy(hbm.at[layer, expert], stage.at[slot], wsem.at[slot, i])
                for i, (hbm, stage) in enumerate(((wg_hbm, wg_stage), (wu_hbm, wu_stage),
                                                  (wd_hbm, wd_stage)))]

    @pl.when(j == 0)
    def _():
        for cp in weight_copies(be_ref[0], 0):
            cp.start()
        for ahead in range(GMM_RING - 1):
            @pl.when(ahead < n_live)
            def _():
                fetch(ahead).start()

    @pl.when(j + (GMM_RING - 1) < n_live)
    def _():
        fetch(j + (GMM_RING - 1)).start()

    @pl.when(j < n_live)
    def _():
        @pl.when(first_ref[j] == 1)
        def _():
            slot = wslot_ref[j]
            for cp in weight_copies(be_ref[j], slot):
                cp.wait()
            wg_bf[...] = wg_stage[slot].astype(BF16)
            wu_bf[...] = wu_stage[slot].astype(BF16)
            wd_bf[...] = wd_stage[slot].astype(BF16)

            @pl.when(next_ref[j] >= 0)
            def _():
                for cp in weight_copies(next_ref[j], 1 - slot):
                    cp.start()

        fetch(j).wait()
        xb = x_ring[j % GMM_RING]
        pre = _dot(xb, wg_bf[...])
        hid = pre * jax.nn.sigmoid(pre) * _dot(xb, wu_bf[...])
        y_ref[...] = _dot(hid.astype(BF16), wd_bf[...]).astype(BF16)

    @pl.when(j >= nb_ref[0])
    def _():
        y_ref[...] = jnp.zeros_like(y_ref)


def _gmm(xs, blk_expert, n_blocks, run_first, run_next, run_slot, p):
    n_rows, d = xs.shape
    nb_max = n_rows // MOE_BLOCK
    any_spec = pl.BlockSpec(memory_space=pl.ANY)
    grid_spec = pltpu.PrefetchScalarGridSpec(
        num_scalar_prefetch=5,
        grid=(nb_max,),
        in_specs=[any_spec] * 4,
        out_specs=pl.BlockSpec((MOE_BLOCK, d), lambda j, *_: (j, 0)),
        scratch_shapes=[pltpu.VMEM((GMM_RING, MOE_BLOCK, d), BF16),
                        pltpu.VMEM((2, d, D_EXPERT), F32), pltpu.VMEM((2, d, D_EXPERT), F32),
                        pltpu.VMEM((2, D_EXPERT, d), F32),
                        pltpu.VMEM((d, D_EXPERT), BF16), pltpu.VMEM((d, D_EXPERT), BF16),
                        pltpu.VMEM((D_EXPERT, d), BF16),
                        pltpu.SemaphoreType.DMA((GMM_RING,)), pltpu.SemaphoreType.DMA((2, 3))],
    )
    return pl.pallas_call(
        functools.partial(_gmm_body, p["layer"]),
        grid_spec=grid_spec,
        out_shape=jax.ShapeDtypeStruct((n_rows, d), BF16),
        compiler_params=_params(("arbitrary",), 24 * MIB),
        name="moe_gmm",
    )(blk_expert, n_blocks, run_first, run_next, run_slot,
      xs, p["w_exp_gate"], p["w_exp_up"], p["w_exp_down"])


def _combine_body(run_ref, next_run_ref, ys_hbm, idx_ref, wts_ref, x1_ref, g2_ref, l2g_ref,
                  l2b_ref, out_ref, loc_buf, sem):
    nt = MOE_TILE
    step = pl.program_id(0)
    slot = step & 1

    def copies_into(which):
        def make_copy(dst, src, rows):
            return pltpu.make_async_copy(ys_hbm.at[pl.ds(dst, rows)],
                                         loc_buf.at[which, pl.ds(src, rows)], sem.at[which])
        return make_copy

    @pl.when(step == 0)
    def _():
        loc_buf[...] = jnp.zeros_like(loc_buf)
        _for_each_slab(run_ref, copies_into(slot), lambda cp: cp.start())

    @pl.when(step < pl.num_programs(0) - 1)
    def _():
        _for_each_slab(next_run_ref, copies_into(1 - slot), lambda cp: cp.start())

    _for_each_slab(run_ref, copies_into(slot), lambda cp: cp.wait())
    lp = idx_ref[...].astype(F32)
    w = wts_ref[...]
    c = lax.broadcasted_iota(jnp.int32, (nt, MOE_LOCAL_ROWS), 1).astype(F32)
    sel = (jnp.where(c == lp[:, 2:3], w[:, 0:1], 0.0)
           + jnp.where(c == lp[:, 3:4], w[:, 1:2], 0.0))
    sel_hi, sel_lo = _split2(sel)
    rows_bf = loc_buf[slot]
    moe = _dot(sel_hi, rows_bf) + _dot(sel_lo, rows_bf)
    out_ref[...] = _layer_norm(ALPHA * x1_ref[...] + g2_ref[0] * moe, l2g_ref[...], l2b_ref[...])


def _combine(seq_len, ys, idx, wts, runs, x1, g2, p):
    m, d = x1.shape
    nt = MOE_TILE
    tiles_per_seq = seq_len // nt
    tok = lambda n: pl.BlockSpec((nt, n), lambda i: (i, 0))
    n_tiles = m // nt
    run_spec = lambda at: pl.BlockSpec((1, 1, LANES), lambda i: (at(i), 0, 0),
                                       memory_space=pltpu.SMEM)
    return pl.pallas_call(
        _combine_body,
        grid=(n_tiles,),
        in_specs=[run_spec(lambda i: i), run_spec(lambda i: jnp.minimum(i + 1, n_tiles - 1)),
                  pl.BlockSpec(memory_space=pl.ANY), tok(LANES), tok(LANES), tok(d),
                  pl.BlockSpec((1, 1, d), lambda i: (i // tiles_per_seq, 0, 0)),
                  _const_spec((1, d)), _const_spec((1, d))],
        out_specs=tok(d),
        out_shape=jax.ShapeDtypeStruct((m, d), F32),
        scratch_shapes=[pltpu.VMEM((2, MOE_LOCAL_ROWS, d), BF16), pltpu.SemaphoreType.DMA((2,))],
        compiler_params=_params(("arbitrary",), 48 * MIB),
        name="moe_combine",
    )(runs, runs, ys, idx, wts, x1, g2, p["ln2_g"], p["ln2_b"])


def _moe_routed(seq_len, h2, idx, wts, tile_cnt, x1, g2, p):
    m, d = h2.shape
    blk = MOE_BLOCK
    n_tiles = m // MOE_TILE
    worst_rows = 2 * m + n_tiles * N_EXPERTS * (MOE_RUN_ALIGN - 1) + N_EXPERTS * (blk - 1)
    nb_max = -(-worst_rows // blk)
    cnt = tile_cnt[:, 0, :N_EXPERTS]
    run_len = ((cnt + MOE_RUN_ALIGN - 1) // MOE_RUN_ALIGN) * MOE_RUN_ALIGN
    local_row = jnp.cumsum(run_len, axis=1) - run_len
    rows_before = jnp.cumsum(run_len, axis=0) - run_len
    total = jnp.sum(run_len, axis=0)
    padded = ((total + blk - 1) // blk) * blk
    ends = jnp.cumsum(padded)
    starts = ends - padded
    n_blocks = (ends[-1] // blk).astype(jnp.int32).reshape(1)
    first_row = jnp.arange(nb_max, dtype=jnp.int32) * blk
    blk_expert = jnp.minimum(
        jnp.sum((first_row[:, None] >= ends[None, :]).astype(jnp.int32), axis=1),
        N_EXPERTS - 1).astype(jnp.int32)
    tail_start = jnp.concatenate(
        [jnp.where(padded > 0, ends - blk, -1).astype(jnp.int32), n_blocks])
    runs = jnp.concatenate(
        [starts[None, :] + rows_before, local_row, run_len,
         jnp.zeros((n_tiles, LANES - 3 * N_EXPERTS), jnp.int32)], axis=1).astype(jnp.int32)
    runs = runs.reshape(n_tiles, 1, LANES)
    prev_expert = jnp.concatenate([jnp.full((1,), -1, jnp.int32), blk_expert[:-1]])
    run_first = (blk_expert != prev_expert).astype(jnp.int32)
    run_slot = ((jnp.cumsum(run_first) - 1) % 2).astype(jnp.int32)
    used = padded > 0
    later = jnp.arange(N_EXPERTS)[None, :] > jnp.arange(N_EXPERTS)[:, None]
    next_used = jnp.min(jnp.where(later & used[None, :], jnp.arange(N_EXPERTS)[None, :], N_EXPERTS),
                        axis=1)
    run_next = jnp.where(next_used < N_EXPERTS, next_used, -1).astype(jnp.int32)[blk_expert]
    xs = _dispatch(h2, idx, runs, tail_start, nb_max * blk)
    ys = _gmm(xs, blk_expert, n_blocks, run_first, run_next, run_slot, p)
    return _combine(seq_len, ys, idx, wts, runs, x1, g2, p)


def _prep_layer(l, w_in, mu_shift, w0, w_decay_up, a0, w_iclr_up, k_k, k_a, r_k,
                lnx_g, lnx_b, lnv_g, lnv_b, w_spatial, b_spatial, w_branch_a, w_branch_b,
                w_out, ln1_g, ln1_b, w_route_group, b_route_group, w_route_expert,
                b_route_expert, w_exp_gate, w_exp_up, w_exp_down, ln2_g, ln2_b):
    d = D_MODEL
    pad_a = N_SHIFT_PAD - N_SHIFT
    wi = w_in[l]
    w_in_p = jnp.concatenate(
        [wi[:, :N_SHIFT], jnp.zeros((d, pad_a), F32), wi[:, N_SHIFT:]], axis=1).astype(BF16)
    mu = jnp.concatenate([mu_shift[l], jnp.zeros((pad_a,), F32)])[None]
    lora = jnp.zeros((LANES, 2 * D_A), F32)
    lora = lora.at[:R_LORA, :D_A].set(w_decay_up[l]).at[R_LORA:2 * R_LORA, D_A:].set(w_iclr_up[l])
    seg = jnp.arange(D_A) // HEAD
    row1 = lambda x: x.reshape(1, -1)
    gsz = D_B // N_GROUPS_B
    w_route = jnp.concatenate(
        [w_route_expert[l], w_route_group[l],
         jnp.zeros((d, LANES - N_EXPERTS - N_ROUTE_GROUPS), F32)], axis=1)
    b_route = jnp.concatenate(
        [b_route_expert[l], b_route_group[l],
         jnp.zeros((LANES - N_EXPERTS - N_ROUTE_GROUPS,), F32)])[None]
    return dict(
        w_in=w_in_p, mu=mu, lora=lora,
        w0a0=jnp.concatenate([w0[l], a0[l]])[None],
        k_k=row1(k_k[l]), k_a=row1(k_a[l]), r_k=row1(r_k[l]),
        lnx_g=row1(lnx_g[l]), lnx_b=row1(lnx_b[l]),
        lnv_g=row1(lnv_g[l]), lnv_b=row1(lnv_b[l]),
        bd=(seg[:, None] == seg[None, :]).astype(BF16),
        w_spatial=w_spatial[l],
        b_spatial_full=jnp.repeat(b_spatial[l].T, gsz, axis=1),
        ws_row=jnp.repeat(w_spatial[l][:, 0, 0], gsz)[None],
        bs_row=jnp.repeat(b_spatial[l][:, 0], gsz)[None],
        w_branch_a=w_branch_a[l].astype(BF16), w_branch_b=w_branch_b[l].astype(BF16),
        w_out=w_out[l].astype(BF16), ln1_g=row1(ln1_g[l]), ln1_b=row1(ln1_b[l]),
        w_route=w_route, b_route=b_route,
        w_exp_gate=w_exp_gate, w_exp_up=w_exp_up, w_exp_down=w_exp_down, layer=l,
        ln2_g=row1(ln2_g[l]), ln2_b=row1(ln2_b[l]),
    )


def _trunk(seq_mode, x, mods, wkv_in, shift_in, preps):
    b, t, d = x.shape
    wkv_out, shift_out, v_out = [], [], []
    wkv_acc = None if seq_mode else jnp.zeros(wkv_in.shape, F32)
    for l in range(DEPTH):
        p = preps[l]
        sh1, sc1, g1, sh2, sc2, g2 = mods[l]
        if seq_mode:
            zprev = jnp.zeros((b, 1, N_SHIFT_PAD), F32) if shift_in is None else shift_in[l]
        else:
            zprev = _matmul(shift_in[l], p["w_in"][:, :N_SHIFT_PAD]).reshape(b, t, N_SHIFT_PAD)
        r, lw, kp, v, kap, bb, u, vg, ga, gb, hl = _in_stage(seq_mode, x, sc1, sh1, zprev, p)
        if seq_mode:
            o, s_new = _wkv_seq(r, lw, kp, v, kap, bb, wkv_in[l])
        else:
            flat = lambda a: a.reshape(t, D_A)
            o, wkv_acc = _wkv_step(flat(r), flat(lw), flat(kp), flat(v), flat(kap), flat(bb),
                                   wkv_in[l], wkv_acc, l)
            o = o.reshape(b, t, D_A)
        x1, h2 = _merge_stage(seq_mode, x, o, r, kp, v, u, vg, ga, gb, g1, sc2, sh2, p)
        m = b * t
        h2f = h2.reshape(m, d)
        gate, idx, wts, tile_cnt = _route(h2f, p)
        if seq_mode:
            x = _moe_routed(t, h2f, idx, wts, tile_cnt, x1.reshape(m, d), g2, p)
        else:
            x = _moe(t, h2f, gate, x1.reshape(m, d), g2, p)
        x = x.reshape(b, t, d)
        if seq_mode:
            wkv_out.append(s_new)
        shift_out.append(hl)
        v_out.append(vg)
    if not seq_mode:
        wkv_out = wkv_acc
    return x, wkv_out, shift_out, v_out


def kernel(x_prompt, x_sample, c_prompt, c_sample, state_wkv, state_shift, w_ada, b_ada, w_in, mu_shift, w0, w_decay_up, a0, w_iclr_up, k_k, k_a, r_k, lnx_g, lnx_b, lnv_g, lnv_b, w_spatial, b_spatial, w_branch_a, w_branch_b, w_out, ln1_g, ln1_b, w_route_group, b_route_group, w_route_expert, b_route_expert, w_exp_gate, w_exp_up, w_exp_down, ln2_g, ln2_b):
    bp, tp, d = x_prompt.shape
    bs = x_sample.shape[0]
    layer_params = (w_in, mu_shift, w0, w_decay_up, a0, w_iclr_up, k_k, k_a, r_k, lnx_g,
                    lnx_b, lnv_g, lnv_b, w_spatial, b_spatial, w_branch_a, w_branch_b, w_out,
                    ln1_g, ln1_b, w_route_group, b_route_group, w_route_expert,
                    b_route_expert, w_exp_gate, w_exp_up, w_exp_down, ln2_g, ln2_b)
    preps = [_prep_layer(l, *layer_params) for l in range(DEPTH)]
    mod_all = _ada(jnp.concatenate([c_prompt, c_sample], axis=0), w_ada, b_ada)
    mods_p, mods_s = [], []
    for l in range(DEPTH):
        parts = jnp.split(mod_all[l], 6, axis=-1)
        mods_p.append([q[:bp].reshape(bp, 1, d) for q in parts])
        mods_s.append([q[bp:].reshape(1, bs, d) for q in parts])

    wkv0 = jnp.zeros((DEPTH, bp, N_HEADS, HEAD, HEAD), F32)
    y_p, wkv_p, shift_p, _ = _trunk(True, x_prompt, mods_p, wkv0, None, preps)
    y_s, wkv_s, shift_s, v_s = _trunk(False, x_sample.reshape(1, bs, d), mods_s, state_wkv,
                                      state_shift, preps)
    return (y_p,
            y_s.reshape(bs, 1, d),
            jnp.stack(wkv_p),
            jnp.stack([s.reshape(bp, d) for s in shift_p]),
            wkv_s,
            jnp.stack([s.reshape(bs, d) for s in shift_s]),
            jnp.stack([q.reshape(bs, 1, D_B) for q in v_s]))
```

```python
import functools

import jax
import jax.numpy as jnp
from jax import lax
from jax.experimental import pallas as pl
from jax.experimental.pallas import tpu as pltpu

F32 = jnp.float32
BF16 = jnp.bfloat16

D_MODEL = 1024
DEPTH = 2
HEAD = 64
N_HEADS = 8
D_A = N_HEADS * HEAD
R_LORA = 32
CHUNK = 128
N_GROUPS_B = 8
D_B = 512
N_SHIFT = 3 * D_A + 2 * R_LORA
N_ROUTE_GROUPS = 4
EXP_PER_GROUP = 8
N_EXPERTS = N_ROUTE_GROUPS * EXP_PER_GROUP
D_EXPERT = 256
ALPHA = (2 * DEPTH) ** 0.25
LN_EPS = 1e-5
GN_EPS = 64e-5

LANES = 128
SUBLANES = 8
MIB = 2 ** 20
N_SHIFT_PAD = 13 * LANES
COL_U = N_SHIFT_PAD
COL_VG = COL_U + D_B
COL_GA = COL_VG + D_B
COL_GB = COL_GA + D_MODEL
N_IN_PAD = COL_GB + D_MODEL
WKV_CHUNK = 64
WKV_TILE = 256
WKV_STEP_ROWS = 8
MERGE_TILE = 512
DENSE_MOE_TILE = 1024
ADA_COLS = 1024
IN_ROW_GROUPS = 4
IN_GROUP_ROWS = 128
VMEM_CAP_BYTES = 60000 * 1024

_NN = (((1,), (0,)), ((), ()))
_B_NT = (((2,), (2,)), ((0,), (0,)))
_B_NN = (((2,), (1,)), ((0,), (0,)))
_B_TN = (((1,), (1,)), ((0,), (0,)))


def _dot(a, b, dims=_NN):
    return lax.dot_general(a, b, dims, preferred_element_type=F32)


def _split2(x):
    hi = x.astype(BF16)
    lo = (x - hi.astype(F32)).astype(BF16)
    return hi, lo


def _mm1(a, b, dims=_NN):
    return _dot(a.astype(BF16), b.astype(BF16), dims)


def _mm3(a, b, dims=_NN):
    ah, al = _split2(a)
    bh, bl = _split2(b)
    return _dot(ah, bh, dims) + (_dot(ah, bl, dims) + _dot(al, bh, dims))


def _segsum(x, bd, two_terms=False):
    if not two_terms:
        return _dot(x.astype(BF16), bd)
    hi, lo = _split2(x)
    return _dot(hi, bd) + _dot(lo, bd)


def _layer_norm(x, g, b):
    mu = jnp.mean(x, axis=-1, keepdims=True)
    d = x - mu
    var = jnp.mean(d * d, axis=-1, keepdims=True)
    return d * lax.rsqrt(var + LN_EPS) * g + b


def _gelu(x):
    return 0.5 * x * (1.0 + lax.erf(x * 0.7071067811865476))


def _params(sem, est_bytes):
    limit = int(min(VMEM_CAP_BYTES, max(est_bytes, 16 * 1024 * 1024)))
    return pltpu.CompilerParams(dimension_semantics=sem, vmem_limit_bytes=limit)


def _const_spec(shape, single_buffer=False):
    nd = len(shape)
    if single_buffer:
        return pl.BlockSpec(shape, lambda *_: (0,) * nd, pipeline_mode=pl.Buffered(1))
    return pl.BlockSpec(shape, lambda *_: (0,) * nd)


def _ada_body(c_ref, w_ref, b_ref, o_ref):
    c = c_ref[...]
    s = c * jax.nn.sigmoid(c)
    o_ref[0] = _mm3(s, w_ref[0]) + b_ref[0]


def _ada(c_all, w_ada, b_ada):
    depth, d, n6 = w_ada.shape
    m = c_all.shape[0]
    tn = ADA_COLS
    return pl.pallas_call(
        _ada_body,
        grid=(depth, n6 // tn),
        in_specs=[
            pl.BlockSpec((m, d), lambda l, j: (0, 0)),
            pl.BlockSpec((1, d, tn), lambda l, j: (l, 0, j)),
            pl.BlockSpec((1, 1, tn), lambda l, j: (l, 0, j)),
        ],
        out_specs=pl.BlockSpec((1, m, tn), lambda l, j: (l, 0, j)),
        out_shape=jax.ShapeDtypeStruct((depth, m, n6), F32),
        compiler_params=_params(("arbitrary", "arbitrary"), 24 * MIB),
        name="ada_mod",
    )(c_all, w_ada, b_ada.reshape(depth, 1, n6))


def _mm_body(x_ref, w_ref, o_ref):
    o_ref[...] = _mm1(x_ref[...], w_ref[...])


def _matmul(x, w):
    m, k = x.shape
    n = w.shape[1]
    return pl.pallas_call(
        _mm_body,
        grid=(1,),
        in_specs=[_const_spec((m, k)), _const_spec((k, n))],
        out_specs=_const_spec((m, n)),
        out_shape=jax.ShapeDtypeStruct((m, n), F32),
        compiler_params=_params(("arbitrary",), 24 * MIB),
        name="shift_proj",
    )(x, w)


def _in_body(seq_mode, tm, x_ref, sc_ref, sh_ref, w_ref, mu_ref, zp_ref, lora_ref,
             w0a0_ref, kk_ref, ka_ref, lng_ref, lnb_ref, bd_ref,
             r_ref, lw_ref, kp_ref, v_ref, kap_ref, bb_ref, u_ref, vg_ref,
             ga_ref, gb_ref, hl_ref, carry_ref):
    nsplit = IN_ROW_GROUPS if seq_mode else 1
    rows = tm // nsplit
    if seq_mode:
        @pl.when(pl.program_id(1) == 0)
        def _():
            carry_ref[...] = zp_ref[0]

        carry = carry_ref[...]
    bd = bd_ref[...]
    for part in range(nsplit):
        sl = slice(part * rows, (part + 1) * rows)
        h = x_ref[0, sl, :] * (1.0 + sc_ref[0]) + sh_ref[0]
        hb = h.astype(BF16)
        proj = lambda lo, hi: _dot(hb, w_ref[:, lo:hi])
        za = proj(0, N_SHIFT_PAD)
        if seq_mode:
            row = lax.broadcasted_iota(jnp.int32, za.shape, 0)
            prev = jnp.where(row == 0, carry, pltpu.roll(za, 1, 0))
            carry = za[rows - 1:rows, :]
        else:
            prev = zp_ref[0]
        mix = za + mu_ref[...] * (prev - za)
        r = mix[:, 0:D_A]
        k = mix[:, D_A:2 * D_A]
        v = mix[:, 2 * D_A:3 * D_A]
        xwa = mix[:, 3 * D_A:N_SHIFT_PAD]
        lane = lax.broadcasted_iota(jnp.int32, xwa.shape, 1)
        lora_in = jnp.where(lane < R_LORA, jnp.tanh(xwa), xwa)
        pre = w0a0_ref[...] + _mm1(lora_in, lora_ref[...])
        yw = -pre[:, :D_A]
        softplus = jnp.maximum(yw, 0.0) + jnp.log1p(jnp.exp(-jnp.abs(yw)))
        lw = -jnp.exp(-softplus - 0.5)
        a = jax.nn.sigmoid(pre[:, D_A:])
        kk = k * kk_ref[...]
        kap = kk / jnp.maximum(jnp.sqrt(_segsum(kk * kk, bd)), 1e-12)
        r_ref[0, sl, :] = r
        lw_ref[0, sl, :] = lw
        kp_ref[0, sl, :] = k * (1.0 + (a - 1.0) * ka_ref[...])
        v_ref[0, sl, :] = v
        kap_ref[0, sl, :] = kap
        bb_ref[0, sl, :] = kap * a
        u_ref[0, sl, :] = _gelu(proj(COL_U, COL_VG)).astype(BF16)
        vg_ref[0, sl, :] = _layer_norm(_gelu(proj(COL_VG, COL_GA)), lng_ref[...], lnb_ref[...])
        ga_ref[0, sl, :] = jax.nn.sigmoid(proj(COL_GA, COL_GB)).astype(BF16)
        gb_ref[0, sl, :] = jax.nn.sigmoid(proj(COL_GB, N_IN_PAD)).astype(BF16)
    if seq_mode:
        carry_ref[...] = carry
        hl_ref[0] = h[rows - 1:rows, :]
    else:
        hl_ref[0] = h


def _in_stage(seq_mode, x, sc, sh, zprev, p):
    b, t, d = x.shape
    tm = IN_ROW_GROUPS * IN_GROUP_ROWS if seq_mode else t
    tmod = 1 if seq_mode else tm
    grid = (b, t // tm)
    tok = lambda n: pl.BlockSpec((1, tm, n), lambda i, j: (i, j, 0))
    mod = pl.BlockSpec((1, tmod, d), lambda i, j: (i, j if not seq_mode else 0, 0))
    zp_spec = (pl.BlockSpec((1, 1, N_SHIFT_PAD), lambda i, j: (i, 0, 0)) if seq_mode
               else tok(N_SHIFT_PAD))
    hl_spec = (pl.BlockSpec((1, 1, d), lambda i, j: (i, 0, 0)) if seq_mode else tok(d))
    hl_shape = (b, 1, d) if seq_mode else (b, t, d)
    out_cols = [D_A] * 6 + [D_B] * 2 + [d] * 2
    out_dtypes = [F32] * 6 + [BF16, F32] + [BF16] * 2
    est = (2 * tm * (d + N_SHIFT_PAD + sum(out_cols) + d) * 4 + 2 * d * N_IN_PAD * 2
           + 3 * tm * N_IN_PAD * 4 + 4 * MIB)
    outs = pl.pallas_call(
        functools.partial(_in_body, seq_mode, tm),
        grid=grid,
        in_specs=[tok(d), mod, mod,
                  _const_spec((d, N_IN_PAD), True), _const_spec((1, N_SHIFT_PAD)), zp_spec,
                  _const_spec((LANES, 2 * D_A)), _const_spec((1, 2 * D_A)),
                  _const_spec((1, D_A)), _const_spec((1, D_A)),
                  _const_spec((1, D_B)), _const_spec((1, D_B)),
                  _const_spec((D_A, D_A))],
        out_specs=[tok(n) for n in out_cols] + [hl_spec],
        out_shape=[jax.ShapeDtypeStruct((b, t, n), dt) for n, dt in zip(out_cols, out_dtypes)]
        + [jax.ShapeDtypeStruct(hl_shape, F32)],
        scratch_shapes=[pltpu.VMEM((1, N_SHIFT_PAD), F32)],
        compiler_params=_params(("arbitrary", "arbitrary"), est),
        name="in_stage_seq" if seq_mode else "in_stage_row",
    )(x, sc, sh, p["w_in"], p["mu"], zprev, p["lora"], p["w0a0"], p["k_k"], p["k_a"],
      p["lnv_g"], p["lnv_b"], p["bd"])
    return outs


def _chunk_pairs(x, nsub):
    c = WKV_CHUNK
    w = 2 * HEAD
    return jnp.stack([x[c * i:c * (i + 1), w * q:w * (q + 1)]
                      for i in range(nsub) for q in range(N_HEADS // 2)], axis=0)


def _masked_rowsum(mask_bf, x):
    h1, h2 = _split2(x)
    return _dot(mask_bf, h1) + _dot(mask_bf, h2)


def _masked_rowsum_t(x, mask):
    h1, h2 = _split2(x)
    m = mask.astype(BF16)
    return _dot(h1, m) + _dot(h2, m)


def _wkv_seq_body(nsub, r_ref, lw_ref, kp_ref, v_ref, kap_ref, bb_ref, s0_ref,
                  o_ref, sout_ref, s_scr):
    c = WKV_CHUNK
    tc = nsub * c
    npair = N_HEADS // 2

    @pl.when(pl.program_id(1) == 0)
    def _():
        for q in range(npair):
            s_scr[q] = jnp.concatenate([s0_ref[0, 2 * q], s0_ref[0, 2 * q + 1]], axis=-1)

    row_t = lax.broadcasted_iota(jnp.int32, (tc, tc), 0)
    col_t = lax.broadcasted_iota(jnp.int32, (tc, tc), 1)
    shift = c.bit_length() - 1
    same_chunk = (row_t >> shift) == (col_t >> shift)
    lw = lw_ref[0]
    g = _masked_rowsum((same_chunk & (row_t >= col_t)).astype(BF16), lw)
    g_end = _masked_rowsum(same_chunk.astype(BF16), lw)
    e_neg = jnp.exp(-g)
    e_end = jnp.exp(g_end - g)
    cp = functools.partial(_chunk_pairs, nsub=nsub)
    k = kp_ref[0]
    b = bb_ref[0]
    kap_t = cp(kap_ref[0] * jnp.exp(g - lw))
    b_t = cp(b * e_neg)
    k_t = cp(k * e_neg)
    r_t = cp(r_ref[0] * jnp.exp(g))
    b_e = cp(b * e_end)
    k_e = cp(k * e_end)
    vv = cp(v_ref[0])
    decay_end = cp(jnp.exp(g_end))

    row = lax.broadcasted_iota(jnp.int32, (c, 2 * c), 0)[None]
    lane = lax.broadcasted_iota(jnp.int32, (c, 2 * c), 1)[None]
    colp = lane & (c - 1)
    right = lane >= c
    row2 = lax.broadcasted_iota(jnp.int32, (2 * c, 2 * c), 0)[None]
    lane2 = lax.broadcasted_iota(jnp.int32, (2 * c, 2 * c), 1)[None]
    same_head = (row2 >= c) == (lane2 >= c)

    def bd(x):
        return jnp.concatenate([jnp.where(right, 0.0, x), jnp.where(right, x, 0.0)], axis=1)

    p_b = _mm1(jnp.concatenate([kap_t, r_t], axis=1), bd(b_t), _B_NT)
    l_b = jnp.where(row > colp, p_b[:, :c], 0.0)
    a_rb = jnp.where(row >= colp, p_b[:, c:], 0.0)
    bd_k = bd(k_t)
    l_k = jnp.where(row > colp, _mm1(kap_t, bd_k, _B_NT), 0.0)
    a_rk = jnp.where(row >= colp, _mm1(r_t, bd_k, _B_NT), 0.0)
    m = -l_b
    t_inv = jnp.where(row == colp, 1.0, 0.0) + m
    m = _mm1(m, bd(m), _B_NN)
    span = 2
    while 2 * span < c:
        both = _mm1(jnp.concatenate([m, t_inv], axis=1), bd(m), _B_NN)
        m = both[:, :c]
        t_inv = t_inv + both[:, c:]
        span *= 2
    t_inv = t_inv + _mm1(t_inv, bd(m), _B_NN)
    bd_v = bd(vv)
    a1 = -_mm1(t_inv, bd(kap_t), _B_NN)
    u0 = -_mm1(t_inv, bd(_mm1(l_k, bd_v, _B_NN)), _B_NN)
    a2 = r_t + _mm1(a_rb, bd(a1), _B_NN)
    o0 = _mm1(a_rb, bd(u0), _B_NN) + _mm1(a_rk, bd_v, _B_NN)
    g_bd = (jnp.where(same_head, _mm1(a1, b_e, _B_TN), 0.0)
            + jnp.where(row2 == lane2, decay_end[:, 0:1, :], 0.0))
    hh = _mm1(jnp.concatenate([u0, vv], axis=1),
              jnp.concatenate([b_e, k_e], axis=1), _B_TN)
    h_pair = jnp.where(right, hh[:, c:], hh[:, :c])

    s = s_scr[...]
    for i in range(nsub):
        ps = slice(npair * i, npair * (i + 1))
        o = _mm1(a2[ps], bd(s), _B_NT) + o0[ps]
        s = _mm1(s, g_bd[ps], _B_NN) + h_pair[ps]
        for q in range(npair):
            o_ref[0, c * i:c * (i + 1), 2 * HEAD * q:2 * HEAD * (q + 1)] = o[q]
    s_scr[...] = s

    @pl.when(pl.program_id(1) == pl.num_programs(1) - 1)
    def _():
        for q in range(npair):
            sout_ref[0, 2 * q] = s[q][:, :HEAD]
            sout_ref[0, 2 * q + 1] = s[q][:, HEAD:]


def _wkv_seq(r, lw, kp, v, kap, bb, s0):
    b, t, _ = r.shape
    tc = WKV_TILE
    nsub = tc // WKV_CHUNK
    tok = pl.BlockSpec((1, tc, D_A), lambda i, j: (i, j, 0))
    st = pl.BlockSpec((1, N_HEADS, HEAD, HEAD), lambda i, j: (i, 0, 0, 0))
    return pl.pallas_call(
        functools.partial(_wkv_seq_body, nsub),
        grid=(b, t // tc),
        in_specs=[tok] * 6 + [st],
        out_specs=[tok, st],
        out_shape=[jax.ShapeDtypeStruct((b, t, D_A), F32),
                   jax.ShapeDtypeStruct((b, N_HEADS, HEAD, HEAD), F32)],
        scratch_shapes=[pltpu.VMEM((N_HEADS // 2, HEAD, 2 * HEAD), F32)],
        compiler_params=_params(("arbitrary", "arbitrary"), 32 * MIB),
        name="wkv_seq",
    )(r, lw, kp, v, kap, bb, s0)


def _wkv_step_body(r_ref, lw_ref, kp_ref, v_ref, kap_ref, bb_ref, s0_ref, _, o_ref, sout_ref):
    for h in range(N_HEADS):
        hs = slice(HEAD * h, HEAD * (h + 1))
        row = lambda ref: ref[:, hs][:, None, :]
        s = s0_ref[:, h]
        sa = -jnp.sum(s * row(kap_ref), axis=-1, keepdims=True)
        v_col = v_ref[:, hs][:, :, None]
        s_new = s * jnp.exp(row(lw_ref)) + sa * row(bb_ref) + v_col * row(kp_ref)
        sout_ref[:, h] = s_new
        o_ref[:, hs] = jnp.sum(s_new * row(r_ref), axis=-1)


def _wkv_step(r, lw, kp, v, kap, bb, s0, s_new_all, layer):
    m = r.shape[0]
    nb = WKV_STEP_ROWS
    tok = pl.BlockSpec((nb, D_A), lambda i: (i, 0))
    st = pl.BlockSpec((nb, N_HEADS, HEAD, HEAD), lambda i: (i, 0, 0, 0))
    st_out = pl.BlockSpec((None, nb, N_HEADS, HEAD, HEAD), lambda i: (layer, i, 0, 0, 0))
    return pl.pallas_call(
        _wkv_step_body,
        grid=(m // nb,),
        in_specs=[tok] * 6 + [st, pl.BlockSpec(memory_space=pl.ANY)],
        out_specs=[tok, st_out],
        out_shape=[jax.ShapeDtypeStruct((m, D_A), F32),
                   jax.ShapeDtypeStruct(s_new_all.shape, F32)],
        input_output_aliases={7: 1},
        compiler_params=_params(("arbitrary",), 24 * MIB),
        name="wkv_step",
    )(r, lw, kp, v, kap, bb, s0, s_new_all)


def _merge_body(seq_mode, tm, x_ref, o_ref, r_ref, kp_ref, v_ref, u_ref, vg_ref,
                ga_ref, gb_ref, g1_ref, sc2_ref, sh2_ref, lxg_ref, lxb_ref, rk_ref,
                bd_ref, ws_ref, bs_ref, pa_ref, pb_ref, wo_ref, l1g_ref, l1b_ref,
                x1_ref, h2_ref):
    bd = bd_ref[...]
    o = o_ref[0]
    inv_n = 1.0 / HEAD
    mu = _segsum(o, bd, two_terms=True) * inv_n
    d = o - mu
    var = _segsum(d * d, bd) * inv_n
    on = d * lax.rsqrt(var + GN_EPS) * lxg_ref[...] + lxb_ref[...]
    v = v_ref[0]
    o_a = on + _segsum(r_ref[0] * kp_ref[0] * rk_ref[...], bd) * v
    vg = vg_ref[0]
    if seq_mode:
        row = lax.broadcasted_iota(jnp.int32, (CHUNK, CHUNK), 0)
        col = lax.broadcasted_iota(jnp.int32, (CHUNK, CHUNK), 1)
        lane = lax.broadcasted_iota(jnp.int32, (CHUNK, D_B), 1)
        gsz = D_B // N_GROUPS_B
        w_cat = jnp.concatenate(
            [jnp.where(row >= col, ws_ref[g], 0.0).astype(BF16) for g in range(N_GROUPS_B)], axis=1)
        pieces = []
        for j in range(tm // CHUNK):
            vc = vg[j * CHUNK:(j + 1) * CHUNK, :].astype(BF16)
            v_bd = jnp.concatenate(
                [jnp.where((lane >= g * gsz) & (lane < (g + 1) * gsz), vc, 0.0)
                 for g in range(N_GROUPS_B)], axis=0)
            pieces.append(_dot(w_cat, v_bd) + bs_ref[...])
        s = jnp.concatenate(pieces, axis=0) if len(pieces) > 1 else pieces[0]
    else:
        s = vg * ws_ref[...] + bs_ref[...]
    o_b = u_ref[0] * s
    y = _mm1(ga_ref[0] * _mm1(o_a, pa_ref[...]) + gb_ref[0] * _mm1(o_b, pb_ref[...]),
             wo_ref[...])
    x1 = _layer_norm(ALPHA * x_ref[0] + g1_ref[0] * y, l1g_ref[...], l1b_ref[...])
    x1_ref[0] = x1
    h2_ref[0] = x1 * (1.0 + sc2_ref[0]) + sh2_ref[0]


def _merge_stage(seq_mode, x, o, r, kp, v, u, vg, ga, gb, g1, sc2, sh2, p):
    b, t, d = x.shape
    tm = MERGE_TILE if seq_mode else t
    tmod = 1 if seq_mode else tm
    tok = lambda n: pl.BlockSpec((1, tm, n), lambda i, j: (i, j, 0))
    mod = pl.BlockSpec((1, tmod, d), lambda i, j: (i, j if not seq_mode else 0, 0))
    ws, bs = (p["w_spatial"], p["b_spatial_full"]) if seq_mode else (p["ws_row"], p["bs_row"])
    est = 2 * tm * (2 * d + 7 * D_A + 2 * d + 2 * d) * 4 + 8 * tm * d * 4 + 16 * MIB
    return pl.pallas_call(
        functools.partial(_merge_body, seq_mode, tm),
        grid=(b, t // tm),
        in_specs=[tok(d)] + [tok(D_A)] * 6 + [tok(d), tok(d), mod, mod, mod,
                  _const_spec((1, D_A)), _const_spec((1, D_A)), _const_spec((1, D_A)),
                  _const_spec((D_A, D_A)), _const_spec(ws.shape), _const_spec(bs.shape),
                  _const_spec((D_A, d)), _const_spec((D_B, d)), _const_spec((d, d)),
                  _const_spec((1, d)), _const_spec((1, d))],
        out_specs=[tok(d), tok(d)],
        out_shape=[jax.ShapeDtypeStruct((b, t, d), F32)] * 2,
        compiler_params=_params(("arbitrary", "arbitrary"), est),
        name="merge_seq" if seq_mode else "merge_row",
    )(x, o, r, kp, v, u, vg, ga, gb, g1, sc2, sh2, p["lnx_g"], p["lnx_b"], p["r_k"],
      p["bd"], ws, bs, p["w_branch_a"], p["w_branch_b"], p["w_out"], p["ln1_g"], p["ln1_b"])


def _route_body(h_ref, w_ref, b_ref, gate_ref, idx_ref, wts_ref, cnt_ref):
    lg = _mm3(h_ref[...], w_ref[...]) + b_ref[...]
    lane = lax.broadcasted_iota(jnp.int32, lg.shape, 1)
    lanef = lane.astype(F32)
    neg = -jnp.inf
    is_g = (lane >= N_EXPERTS) & (lane < N_EXPERTS + N_ROUTE_GROUPS)
    mg = jnp.max(jnp.where(is_g, lg, neg), axis=-1, keepdims=True)
    gidx = jnp.min(jnp.where(is_g & (lg == mg), lanef - N_EXPERTS, 1e9), axis=-1, keepdims=True)
    pg_sel = 1.0 / jnp.sum(jnp.where(is_g, jnp.exp(lg - mg), 0.0), axis=-1, keepdims=True)
    lo = gidx * EXP_PER_GROUP
    in_grp = (lanef >= lo) & (lanef < lo + EXP_PER_GROUP)
    t1 = jnp.max(jnp.where(in_grp, lg, neg), axis=-1, keepdims=True)
    i1 = jnp.min(jnp.where(in_grp & (lg == t1), lanef, 1e9), axis=-1, keepdims=True)
    rest = in_grp & (lanef != i1)
    t2 = jnp.max(jnp.where(rest, lg, neg), axis=-1, keepdims=True)
    i2 = jnp.min(jnp.where(rest & (lg == t2), lanef, 1e9), axis=-1, keepdims=True)
    e2 = jnp.exp(t2 - t1)
    w1 = pg_sel / (1.0 + e2)
    w2 = pg_sel * e2 / (1.0 + e2)
    gate_ref[...] = jnp.where(lanef == i1, w1, 0.0) + jnp.where(lanef == i2, w2, 0.0)
    tm = lg.shape[0]
    hit = (lanef == i1) | (lanef == i2)
    earlier = (lax.broadcasted_iota(jnp.int32, (tm, tm), 0)
               > lax.broadcasted_iota(jnp.int32, (tm, tm), 1))
    rank = _dot(earlier.astype(BF16), hit.astype(BF16))
    cnt = jnp.sum(hit.astype(F32), axis=0, keepdims=True)
    run_len = jnp.floor((cnt + (MOE_RUN_ALIGN - 1.0)) * (1.0 / MOE_RUN_ALIGN)) * MOE_RUN_ALIGN
    lower_expert = (lax.broadcasted_iota(jnp.int32, (LANES, LANES), 0)
                    < lax.broadcasted_iota(jnp.int32, (LANES, LANES), 1))
    run_start = _masked_rowsum_t(jnp.broadcast_to(run_len, (SUBLANES, LANES)), lower_expert)[0:1]
    lpos = run_start + rank
    lp1 = jnp.sum(jnp.where(lanef == i1, lpos, 0.0), axis=-1, keepdims=True)
    lp2 = jnp.sum(jnp.where(lanef == i2, lpos, 0.0), axis=-1, keepdims=True)
    cnt_ref[0] = cnt.astype(jnp.int32)
    idx = jnp.where(lane == 0, i1, jnp.where(lane == 1, i2, jnp.where(lane == 2, lp1, lp2)))
    idx_ref[...] = idx.astype(jnp.int32)
    wts_ref[...] = jnp.where(lane == 0, w1, w2)


def _route(h2, p):
    m, d = h2.shape
    tm = min(m, MOE_TILE)
    tok = pl.BlockSpec((tm, LANES), lambda i: (i, 0))
    return pl.pallas_call(
        _route_body,
        grid=(m // tm,),
        in_specs=[pl.BlockSpec((tm, d), lambda i: (i, 0)),
                  _const_spec((d, LANES)), _const_spec((1, LANES))],
        out_specs=[tok, tok, tok, pl.BlockSpec((1, 1, LANES), lambda i: (i, 0, 0))],
        out_shape=[jax.ShapeDtypeStruct((m, LANES), F32),
                   jax.ShapeDtypeStruct((m, LANES), jnp.int32),
                   jax.ShapeDtypeStruct((m, LANES), F32),
                   jax.ShapeDtypeStruct((m // tm, 1, LANES), jnp.int32)],
        compiler_params=_params(("arbitrary",), 24 * MIB),
        name="route",
    )(h2, p["w_route"], p["b_route"])


def _moe_body(h_ref, gate_ref, wg_ref, wu_ref, wd_ref, x1_ref, g2_ref, l2g_ref, l2b_ref,
              out_ref, acc_ref, xb_ref):
    e = pl.program_id(1)

    @pl.when(e == 0)
    def _():
        acc_ref[...] = jnp.zeros_like(acc_ref)
        xb_ref[...] = h_ref[...].astype(BF16)

    xb = xb_ref[...]
    pre = _dot(xb, wg_ref[0].astype(BF16))
    hid = pre * jax.nn.sigmoid(pre) * _dot(xb, wu_ref[0].astype(BF16))
    ye = _mm1(hid, wd_ref[0])
    gate = gate_ref[...]
    lane = lax.broadcasted_iota(jnp.int32, gate.shape, 1)
    ge = jnp.sum(jnp.where(lane == e, gate, 0.0), axis=-1, keepdims=True)
    acc_ref[...] += ge * ye

    @pl.when(e == pl.num_programs(1) - 1)
    def _():
        out_ref[...] = _layer_norm(ALPHA * x1_ref[...] + g2_ref[0] * acc_ref[...],
                                   l2g_ref[...], l2b_ref[...])


def _moe(seq_len, h2, gate, x1, g2, p):
    m, d = h2.shape
    layer = p["layer"]
    tm = min(seq_len if g2.shape[1] == 1 else m, DENSE_MOE_TILE)
    tok = lambda n: pl.BlockSpec((tm, n), lambda i, e: (i, 0))
    if g2.shape[1] == 1:
        tiles_per_seq = seq_len // tm
        g2_spec = pl.BlockSpec((1, 1, d), lambda i, e: (i // tiles_per_seq, 0, 0))
    else:
        g2_spec = pl.BlockSpec((1, tm, d), lambda i, e: (0, i, 0))
    est = 2 * tm * (3 * d + LANES) * 4 + tm * d * 6 + 4 * tm * d * 4 + 8 * MIB
    return pl.pallas_call(
        _moe_body,
        grid=(m // tm, N_EXPERTS),
        in_specs=[tok(d), tok(LANES),
                  pl.BlockSpec((None, 1, d, D_EXPERT), lambda i, e: (layer, e, 0, 0)),
                  pl.BlockSpec((None, 1, d, D_EXPERT), lambda i, e: (layer, e, 0, 0)),
                  pl.BlockSpec((None, 1, D_EXPERT, d), lambda i, e: (layer, e, 0, 0)),
                  tok(d), g2_spec, _const_spec((1, d)), _const_spec((1, d))],
        out_specs=tok(d),
        out_shape=jax.ShapeDtypeStruct((m, d), F32),
        scratch_shapes=[pltpu.VMEM((tm, d), F32), pltpu.VMEM((tm, d), BF16)],
        compiler_params=_params(("arbitrary", "arbitrary"), est),
        name="moe_dense",
    )(h2, gate, p["w_exp_gate"], p["w_exp_up"], p["w_exp_down"], x1, g2, p["ln2_g"], p["ln2_b"])


MOE_TILE = 512
MOE_BLOCK = 512
MOE_RUN_ALIGN = 16
MOE_LOCAL_ROWS = 2 * MOE_TILE + 512
MOE_SLABS = tuple(2 ** k for k in range(9, 3, -1))


def _for_each_slab(run_ref, make_copy, fn):
    def one_run(e, c):
        dst = run_ref[0, 0, e]
        src = run_ref[0, 0, N_EXPERTS + e]
        n = run_ref[0, 0, 2 * N_EXPERTS + e]
        for slab in MOE_SLABS:
            off = n & (-2 * slab)

            @pl.when((n & slab) != 0)
            def _():
                fn(make_copy(pl.multiple_of(dst + off, MOE_RUN_ALIGN),
                             pl.multiple_of(src + off, MOE_RUN_ALIGN), slab))
        return c

    lax.fori_loop(0, N_EXPERTS, one_run, 0)


def _dispatch_body(tail_ref, run_ref, prev_run_ref, h_ref, idx_ref, xs_hbm, zero_buf, loc_buf,
                   sem, zsem, usem):
    nb_max = xs_hbm.shape[0] // MOE_BLOCK
    nt = MOE_TILE

    def zero_copy(row0, zero_sem):
        return pltpu.make_async_copy(
            zero_buf, xs_hbm.at[pl.ds(pl.multiple_of(row0, MOE_BLOCK), MOE_BLOCK)], zero_sem)

    def each_unused_copy(fn):
        def unused(j, c):
            fn(zero_copy(j * MOE_BLOCK, usem))
            return c

        lax.fori_loop(tail_ref[N_EXPERTS], nb_max, unused, 0)

    @pl.when(pl.program_id(0) == 0)
    def _():
        zero_buf[...] = jnp.zeros_like(zero_buf)

        def each_tail_copy(fn):
            for e in range(N_EXPERTS):
                @pl.when(tail_ref[e] >= 0)
                def _():
                    fn(zero_copy(tail_ref[e], zsem))

        each_tail_copy(lambda cp: cp.start())
        each_unused_copy(lambda cp: cp.start())
        each_tail_copy(lambda cp: cp.wait())

    lp = idx_ref[...].astype(F32)
    eye = (lax.broadcasted_iota(jnp.int32, (nt, nt), 0)
           == lax.broadcasted_iota(jnp.int32, (nt, nt), 1))
    as_row = lambda col: jnp.sum(jnp.where(eye, col, 0.0), axis=0, keepdims=True)
    r = lax.broadcasted_iota(jnp.int32, (MOE_LOCAL_ROWS, nt), 0).astype(F32)
    pick = (r == as_row(lp[:, 2:3])) | (r == as_row(lp[:, 3:4]))
    step = pl.program_id(0)
    slot = step & 1
    loc_buf[slot] = _dot(pick.astype(BF16), h_ref[...].astype(BF16)).astype(BF16)

    def copies_from(which):
        def make_copy(dst, src, rows):
            return pltpu.make_async_copy(loc_buf.at[which, pl.ds(src, rows)],
                                         xs_hbm.at[pl.ds(dst, rows)], sem.at[which])
        return make_copy

    _for_each_slab(run_ref, copies_from(slot), lambda cp: cp.start())

    @pl.when(step > 0)
    def _():
        _for_each_slab(prev_run_ref, copies_from(1 - slot), lambda cp: cp.wait())

    @pl.when(step == pl.num_programs(0) - 1)
    def _():
        _for_each_slab(run_ref, copies_from(slot), lambda cp: cp.wait())
        each_unused_copy(lambda cp: cp.wait())


def _dispatch(h2, idx, runs, tail_start, n_rows):
    m, d = h2.shape
    nt = MOE_TILE
    run_spec = lambda at: pl.BlockSpec((1, 1, LANES), lambda i, tail: (at(i), 0, 0),
                                       memory_space=pltpu.SMEM)
    grid_spec = pltpu.PrefetchScalarGridSpec(
        num_scalar_prefetch=1,
        grid=(m // nt,),
        in_specs=[run_spec(lambda i: i), run_spec(lambda i: jnp.maximum(i - 1, 0)),
                  pl.BlockSpec((nt, d), lambda i, tail: (i, 0)),
                  pl.BlockSpec((nt, LANES), lambda i, tail: (i, 0))],
        out_specs=pl.BlockSpec(memory_space=pl.ANY),
        scratch_shapes=[pltpu.VMEM((MOE_BLOCK, d), BF16),
                        pltpu.VMEM((2, MOE_LOCAL_ROWS, d), BF16),
                        pltpu.SemaphoreType.DMA((2,)), pltpu.SemaphoreType.DMA,
                        pltpu.SemaphoreType.DMA],
    )
    return pl.pallas_call(
        _dispatch_body,
        grid_spec=grid_spec,
        out_shape=jax.ShapeDtypeStruct((n_rows, d), BF16),
        compiler_params=_params(("arbitrary",), 40 * MIB),
        name="moe_dispatch",
    )(tail_start, runs, runs, h2, idx)


GMM_RING = 3


def _gmm_body(layer, be_ref, nb_ref, first_ref, next_ref, wslot_ref,
              x_hbm, wg_hbm, wu_hbm, wd_hbm, y_ref,
              x_ring, wg_stage, wu_stage, wd_stage, wg_bf, wu_bf, wd_bf, sem, wsem):
    j = pl.program_id(0)
    n_live = nb_ref[0]

    def fetch(step):
        slot = step % GMM_RING
        return pltpu.make_async_copy(
            x_hbm.at[pl.ds(pl.multiple_of(step * MOE_BLOCK, MOE_BLOCK), MOE_BLOCK)],
            x_ring.at[slot], sem.at[slot])

    def weight_copies(expert, slot):
        return [pltpu.make_async_copy(hbm.at[layer, expert], stage.at[slot], wsem.at[slot, i])
                for i, (hbm, stage) in enumerate(((wg_hbm, wg_stage), (wu_hbm, wu_stage),
                                                  (wd_hbm, wd_stage)))]

    @pl.when(j == 0)
    def _():
        for cp in weight_copies(be_ref[0], 0):
            cp.start()
        for ahead in range(GMM_RING - 1):
            @pl.when(ahead < n_live)
            def _():
                fetch(ahead).start()

    @pl.when(j + (GMM_RING - 1) < n_live)
    def _():
        fetch(j + (GMM_RING - 1)).start()

    @pl.when(j < n_live)
    def _():
        @pl.when(first_ref[j] == 1)
        def _():
            slot = wslot_ref[j]
            for cp in weight_copies(be_ref[j], slot):
                cp.wait()
            wg_bf[...] = wg_stage[slot].astype(BF16)
            wu_bf[...] = wu_stage[slot].astype(BF16)
            wd_bf[...] = wd_stage[slot].astype(BF16)

            @pl.when(next_ref[j] >= 0)
            def _():
                for cp in weight_copies(next_ref[j], 1 - slot):
                    cp.start()

        fetch(j).wait()
        xb = x_ring[j % GMM_RING]
        pre = _dot(xb, wg_bf[...])
        hid = pre * jax.nn.sigmoid(pre) * _dot(xb, wu_bf[...])
        y_ref[...] = _dot(hid.astype(BF16), wd_bf[...]).astype(BF16)

    @pl.when(j >= nb_ref[0])
    def _():
        y_ref[...] = jnp.zeros_like(y_ref)


def _gmm(xs, blk_expert, n_blocks, run_first, run_next, run_slot, p):
    n_rows, d = xs.shape
    nb_max = n_rows // MOE_BLOCK
    any_spec = pl.BlockSpec(memory_space=pl.ANY)
    grid_spec = pltpu.PrefetchScalarGridSpec(
        num_scalar_prefetch=5,
        grid=(nb_max,),
        in_specs=[any_spec] * 4,
        out_specs=pl.BlockSpec((MOE_BLOCK, d), lambda j, *_: (j, 0)),
        scratch_shapes=[pltpu.VMEM((GMM_RING, MOE_BLOCK, d), BF16),
                        pltpu.VMEM((2, d, D_EXPERT), F32), pltpu.VMEM((2, d, D_EXPERT), F32),
                        pltpu.VMEM((2, D_EXPERT, d), F32),
                        pltpu.VMEM((d, D_EXPERT), BF16), pltpu.VMEM((d, D_EXPERT), BF16),
                        pltpu.VMEM((D_EXPERT, d), BF16),
                        pltpu.SemaphoreType.DMA((GMM_RING,)), pltpu.SemaphoreType.DMA((2, 3))],
    )
    return pl.pallas_call(
        functools.partial(_gmm_body, p["layer"]),
        grid_spec=grid_spec,
        out_shape=jax.ShapeDtypeStruct((n_rows, d), BF16),
        compiler_params=_params(("arbitrary",), 24 * MIB),
        name="moe_gmm",
    )(blk_expert, n_blocks, run_first, run_next, run_slot,
      xs, p["w_exp_gate"], p["w_exp_up"], p["w_exp_down"])


def _combine_body(run_ref, next_run_ref, ys_hbm, idx_ref, wts_ref, x1_ref, g2_ref, l2g_ref,
                  l2b_ref, out_ref, loc_buf, sem):
    nt = MOE_TILE
    step = pl.program_id(0)
    slot = step & 1

    def copies_into(which):
        def make_copy(dst, src, rows):
            return pltpu.make_async_copy(ys_hbm.at[pl.ds(dst, rows)],
                                         loc_buf.at[which, pl.ds(src, rows)], sem.at[which])
        return make_copy

    @pl.when(step == 0)
    def _():
        loc_buf[...] = jnp.zeros_like(loc_buf)
        _for_each_slab(run_ref, copies_into(slot), lambda cp: cp.start())

    @pl.when(step < pl.num_programs(0) - 1)
    def _():
        _for_each_slab(next_run_ref, copies_into(1 - slot), lambda cp: cp.start())

    _for_each_slab(run_ref, copies_into(slot), lambda cp: cp.wait())
    lp = idx_ref[...].astype(F32)
    w = wts_ref[...]
    c = lax.broadcasted_iota(jnp.int32, (nt, MOE_LOCAL_ROWS), 1).astype(F32)
    sel = (jnp.where(c == lp[:, 2:3], w[:, 0:1], 0.0)
           + jnp.where(c == lp[:, 3:4], w[:, 1:2], 0.0))
    sel_hi, sel_lo = _split2(sel)
    rows_bf = loc_buf[slot]
    moe = _dot(sel_hi, rows_bf) + _dot(sel_lo, rows_bf)
    out_ref[...] = _layer_norm(ALPHA * x1_ref[...] + g2_ref[0] * moe, l2g_ref[...], l2b_ref[...])


def _combine(seq_len, ys, idx, wts, runs, x1, g2, p):
    m, d = x1.shape
    nt = MOE_TILE
    tiles_per_seq = seq_len // nt
    tok = lambda n: pl.BlockSpec((nt, n), lambda i: (i, 0))
    n_tiles = m // nt
    run_spec = lambda at: pl.BlockSpec((1, 1, LANES), lambda i: (at(i), 0, 0),
                                       memory_space=pltpu.SMEM)
    return pl.pallas_call(
        _combine_body,
        grid=(n_tiles,),
        in_specs=[run_spec(lambda i: i), run_spec(lambda i: jnp.minimum(i + 1, n_tiles - 1)),
                  pl.BlockSpec(memory_space=pl.ANY), tok(LANES), tok(LANES), tok(d),
                  pl.BlockSpec((1, 1, d), lambda i: (i // tiles_per_seq, 0, 0)),
                  _const_spec((1, d)), _const_spec((1, d))],
        out_specs=tok(d),
        out_shape=jax.ShapeDtypeStruct((m, d), F32),
        scratch_shapes=[pltpu.VMEM((2, MOE_LOCAL_ROWS, d), BF16), pltpu.SemaphoreType.DMA((2,))],
        compiler_params=_params(("arbitrary",), 48 * MIB),
        name="moe_combine",
    )(runs, runs, ys, idx, wts, x1, g2, p["ln2_g"], p["ln2_b"])


def _moe_routed(seq_len, h2, idx, wts, tile_cnt, x1, g2, p):
    m, d = h2.shape
    blk = MOE_BLOCK
    n_tiles = m // MOE_TILE
    worst_rows = 2 * m + n_tiles * N_EXPERTS * (MOE_RUN_ALIGN - 1) + N_EXPERTS * (blk - 1)
    nb_max = -(-worst_rows // blk)
    cnt = tile_cnt[:, 0, :N_EXPERTS]
    run_len = ((cnt + MOE_RUN_ALIGN - 1) // MOE_RUN_ALIGN) * MOE_RUN_ALIGN
    local_row = jnp.cumsum(run_len, axis=1) - run_len
    rows_before = jnp.cumsum(run_len, axis=0) - run_len
    total = jnp.sum(run_len, axis=0)
    padded = ((total + blk - 1) // blk) * blk
    ends = jnp.cumsum(padded)
    starts = ends - padded
    n_blocks = (ends[-1] // blk).astype(jnp.int32).reshape(1)
    first_row = jnp.arange(nb_max, dtype=jnp.int32) * blk
    blk_expert = jnp.minimum(
        jnp.sum((first_row[:, None] >= ends[None, :]).astype(jnp.int32), axis=1),
        N_EXPERTS - 1).astype(jnp.int32)
    tail_start = jnp.concatenate(
        [jnp.where(padded > 0, ends - blk, -1).astype(jnp.int32), n_blocks])
    runs = jnp.concatenate(
        [starts[None, :] + rows_before, local_row, run_len,
         jnp.zeros((n_tiles, LANES - 3 * N_EXPERTS), jnp.int32)], axis=1).astype(jnp.int32)
    runs = runs.reshape(n_tiles, 1, LANES)
    prev_expert = jnp.concatenate([jnp.full((1,), -1, jnp.int32), blk_expert[:-1]])
    run_first = (blk_expert != prev_expert).astype(jnp.int32)
    run_slot = ((jnp.cumsum(run_first) - 1) % 2).astype(jnp.int32)
    used = padded > 0
    later = jnp.arange(N_EXPERTS)[None, :] > jnp.arange(N_EXPERTS)[:, None]
    next_used = jnp.min(jnp.where(later & used[None, :], jnp.arange(N_EXPERTS)[None, :], N_EXPERTS),
                        axis=1)
    run_next = jnp.where(next_used < N_EXPERTS, next_used, -1).astype(jnp.int32)[blk_expert]
    xs = _dispatch(h2, idx, runs, tail_start, nb_max * blk)
    ys = _gmm(xs, blk_expert, n_blocks, run_first, run_next, run_slot, p)
    return _combine(seq_len, ys, idx, wts, runs, x1, g2, p)


def _prep_layer(l, w_in, mu_shift, w0, w_decay_up, a0, w_iclr_up, k_k, k_a, r_k,
                lnx_g, lnx_b, lnv_g, lnv_b, w_spatial, b_spatial, w_branch_a, w_branch_b,
                w_out, ln1_g, ln1_b, w_route_group, b_route_group, w_route_expert,
                b_route_expert, w_exp_gate, w_exp_up, w_exp_down, ln2_g, ln2_b):
    d = D_MODEL
    pad_a = N_SHIFT_PAD - N_SHIFT
    wi = w_in[l]
    w_in_p = jnp.concatenate(
        [wi[:, :N_SHIFT], jnp.zeros((d, pad_a), F32), wi[:, N_SHIFT:]], axis=1).astype(BF16)
    mu = jnp.concatenate([mu_shift[l], jnp.zeros((pad_a,), F32)])[None]
    lora = jnp.zeros((LANES, 2 * D_A), F32)
    lora = lora.at[:R_LORA, :D_A].set(w_decay_up[l]).at[R_LORA:2 * R_LORA, D_A:].set(w_iclr_up[l])
    seg = jnp.arange(D_A) // HEAD
    row1 = lambda x: x.reshape(1, -1)
    gsz = D_B // N_GROUPS_B
    w_route = jnp.concatenate(
        [w_route_expert[l], w_route_group[l],
         jnp.zeros((d, LANES - N_EXPERTS - N_ROUTE_GROUPS), F32)], axis=1)
    b_route = jnp.concatenate(
        [b_route_expert[l], b_route_group[l],
         jnp.zeros((LANES - N_EXPERTS - N_ROUTE_GROUPS,), F32)])[None]
    return dict(
        w_in=w_in_p, mu=mu, lora=lora,
        w0a0=jnp.concatenate([w0[l], a0[l]])[None],
        k_k=row1(k_k[l]), k_a=row1(k_a[l]), r_k=row1(r_k[l]),
        lnx_g=row1(lnx_g[l]), lnx_b=row1(lnx_b[l]),
        lnv_g=row1(lnv_g[l]), lnv_b=row1(lnv_b[l]),
        bd=(seg[:, None] == seg[None, :]).astype(BF16),
        w_spatial=w_spatial[l],
        b_spatial_full=jnp.repeat(b_spatial[l].T, gsz, axis=1),
        ws_row=jnp.repeat(w_spatial[l][:, 0, 0], gsz)[None],
        bs_row=jnp.repeat(b_spatial[l][:, 0], gsz)[None],
        w_branch_a=w_branch_a[l].astype(BF16), w_branch_b=w_branch_b[l].astype(BF16),
        w_out=w_out[l].astype(BF16), ln1_g=row1(ln1_g[l]), ln1_b=row1(ln1_b[l]),
        w_route=w_route, b_route=b_route,
        w_exp_gate=w_exp_gate, w_exp_up=w_exp_up, w_exp_down=w_exp_down, layer=l,
        ln2_g=row1(ln2_g[l]), ln2_b=row1(ln2_b[l]),
    )


def _trunk(seq_mode, x, mods, wkv_in, shift_in, preps):
    b, t, d = x.shape
    wkv_out, shift_out, v_out = [], [], []
    wkv_acc = None if seq_mode else jnp.zeros(wkv_in.shape, F32)
    for l in range(DEPTH):
        p = preps[l]
        sh1, sc1, g1, sh2, sc2, g2 = mods[l]
        if seq_mode:
            zprev = jnp.zeros((b, 1, N_SHIFT_PAD), F32) if shift_in is None else shift_in[l]
        else:
            zprev = _matmul(shift_in[l], p["w_in"][:, :N_SHIFT_PAD]).reshape(b, t, N_SHIFT_PAD)
        r, lw, kp, v, kap, bb, u, vg, ga, gb, hl = _in_stage(seq_mode, x, sc1, sh1, zprev, p)
        if seq_mode:
            o, s_new = _wkv_seq(r, lw, kp, v, kap, bb, wkv_in[l])
        else:
            flat = lambda a: a.reshape(t, D_A)
            o, wkv_acc = _wkv_step(flat(r), flat(lw), flat(kp), flat(v), flat(kap), flat(bb),
                                   wkv_in[l], wkv_acc, l)
            o = o.reshape(b, t, D_A)
        x1, h2 = _merge_stage(seq_mode, x, o, r, kp, v, u, vg, ga, gb, g1, sc2, sh2, p)
        m = b * t
        h2f = h2.reshape(m, d)
        gate, idx, wts, tile_cnt = _route(h2f, p)
        if seq_mode:
            x = _moe_routed(t, h2f, idx, wts, tile_cnt, x1.reshape(m, d), g2, p)
        else:
            x = _moe(t, h2f, gate, x1.reshape(m, d), g2, p)
        x = x.reshape(b, t, d)
        if seq_mode:
            wkv_out.append(s_new)
        shift_out.append(hl)
        v_out.append(vg)
    if not seq_mode:
        wkv_out = wkv_acc
    return x, wkv_out, shift_out, v_out


def kernel(x_prompt, x_sample, c_prompt, c_sample, state_wkv, state_shift, w_ada, b_ada, w_in, mu_shift, w0, w_decay_up, a0, w_iclr_up, k_k, k_a, r_k, lnx_g, lnx_b, lnv_g, lnv_b, w_spatial, b_spatial, w_branch_a, w_branch_b, w_out, ln1_g, ln1_b, w_route_group, b_route_group, w_route_expert, b_route_expert, w_exp_gate, w_exp_up, w_exp_down, ln2_g, ln2_b):
    bp, tp, d = x_prompt.shape
    bs = x_sample.shape[0]
    layer_params = (w_in, mu_shift, w0, w_decay_up, a0, w_iclr_up, k_k, k_a, r_k, lnx_g,
                    lnx_b, lnv_g, lnv_b, w_spatial, b_spatial, w_branch_a, w_branch_b, w_out,
                    ln1_g, ln1_b, w_route_group, b_route_group, w_route_expert,
                    b_route_expert, w_exp_gate, w_exp_up, w_exp_down, ln2_g, ln2_b)
    preps = [_prep_layer(l, *layer_params) for l in range(DEPTH)]
    mod_all = _ada(jnp.concatenate([c_prompt, c_sample], axis=0), w_ada, b_ada)
    mods_p, mods_s = [], []
    for l in range(DEPTH):
        parts = jnp.split(mod_all[l], 6, axis=-1)
        mods_p.append([q[:bp].reshape(bp, 1, d) for q in parts])
        mods_s.append([q[bp:].reshape(1, bs, d) for q in parts])

    wkv0 = jnp.zeros((DEPTH, bp, N_HEADS, HEAD, HEAD), F32)
    y_p, wkv_p, shift_p, _ = _trunk(True, x_prompt, mods_p, wkv0, None, preps)
    y_s, wkv_s, shift_s, v_s = _trunk(False, x_sample.reshape(1, bs, d), mods_s, state_wkv,
                                      state_shift, preps)
    return (y_p,
            y_s.reshape(bs, 1, d),
            jnp.stack(wkv_p),
            jnp.stack([s.reshape(bp, d) for s in shift_p]),
            wkv_s,
            jnp.stack([s.reshape(bs, d) for s in shift_s]),
            jnp.stack([q.reshape(bs, 1, D_B) for q in v_s]))
```

```python
import functools

import jax
import jax.numpy as jnp
from jax import lax
from jax.experimental import pallas as pl
from jax.experimental.pallas import tpu as pltpu

F32 = jnp.float32
BF16 = jnp.bfloat16

D_MODEL = 1024
DEPTH = 2
HEAD = 64
N_HEADS = 8
D_A = N_HEADS * HEAD
R_LORA = 32
CHUNK = 128
N_GROUPS_B = 8
D_B = 512
N_SHIFT = 3 * D_A + 2 * R_LORA
N_ROUTE_GROUPS = 4
EXP_PER_GROUP = 8
N_EXPERTS = N_ROUTE_GROUPS * EXP_PER_GROUP
D_EXPERT = 256
ALPHA = (2 * DEPTH) ** 0.25
LN_EPS = 1e-5
GN_EPS = 64e-5

LANES = 128
SUBLANES = 8
MIB = 2 ** 20
N_SHIFT_PAD = 13 * LANES
COL_U = N_SHIFT_PAD
COL_VG = COL_U + D_B
COL_GA = COL_VG + D_B
COL_GB = COL_GA + D_MODEL
N_IN_PAD = COL_GB + D_MODEL
WKV_CHUNK = 64
WKV_TILE = 256
WKV_STEP_ROWS = 16
MERGE_TILE = 512
DENSE_MOE_TILE = 1024
ADA_COLS = 1024
IN_ROW_GROUPS = 4
IN_GROUP_ROWS = 128
VMEM_CAP_BYTES = 60000 * 1024

_NN = (((1,), (0,)), ((), ()))
_B_NT = (((2,), (2,)), ((0,), (0,)))
_B_NN = (((2,), (1,)), ((0,), (0,)))
_B_TN = (((1,), (1,)), ((0,), (0,)))


def _dot(a, b, dims=_NN):
    return lax.dot_general(a, b, dims, preferred_element_type=F32)


def _split2(x):
    hi = x.astype(BF16)
    lo = (x - hi.astype(F32)).astype(BF16)
    return hi, lo


def _mm1(a, b, dims=_NN):
    return _dot(a.astype(BF16), b.astype(BF16), dims)


def _mm3(a, b, dims=_NN):
    ah, al = _split2(a)
    bh, bl = _split2(b)
    return _dot(ah, bh, dims) + (_dot(ah, bl, dims) + _dot(al, bh, dims))


def _segsum(x, bd, two_terms=False):
    if not two_terms:
        return _dot(x.astype(BF16), bd)
    hi, lo = _split2(x)
    return _dot(hi, bd) + _dot(lo, bd)


def _layer_norm(x, g, b):
    mu = jnp.mean(x, axis=-1, keepdims=True)
    d = x - mu
    var = jnp.mean(d * d, axis=-1, keepdims=True)
    return d * lax.rsqrt(var + LN_EPS) * g + b


def _gelu(x):
    return 0.5 * x * (1.0 + lax.erf(x * 0.7071067811865476))


def _params(sem, est_bytes):
    limit = int(min(VMEM_CAP_BYTES, max(est_bytes, 16 * 1024 * 1024)))
    return pltpu.CompilerParams(dimension_semantics=sem, vmem_limit_bytes=limit)


def _const_spec(shape, single_buffer=False):
    nd = len(shape)
    if single_buffer:
        return pl.BlockSpec(shape, lambda *_: (0,) * nd, pipeline_mode=pl.Buffered(1))
    return pl.BlockSpec(shape, lambda *_: (0,) * nd)


def _ada_body(c_ref, w_ref, b_ref, o_ref):
    c = c_ref[...]
    s = c * jax.nn.sigmoid(c)
    o_ref[0] = _mm3(s, w_ref[0]) + b_ref[0]


def _ada(c_all, w_ada, b_ada):
    depth, d, n6 = w_ada.shape
    m = c_all.shape[0]
    tn = ADA_COLS
    return pl.pallas_call(
        _ada_body,
        grid=(depth, n6 // tn),
        in_specs=[
            pl.BlockSpec((m, d), lambda l, j: (0, 0)),
            pl.BlockSpec((1, d, tn), lambda l, j: (l, 0, j)),
            pl.BlockSpec((1, 1, tn), lambda l, j: (l, 0, j)),
        ],
        out_specs=pl.BlockSpec((1, m, tn), lambda l, j: (l, 0, j)),
        out_shape=jax.ShapeDtypeStruct((depth, m, n6), F32),
        compiler_params=_params(("arbitrary", "arbitrary"), 24 * MIB),
        name="ada_mod",
    )(c_all, w_ada, b_ada.reshape(depth, 1, n6))


def _mm_body(x_ref, w_ref, o_ref):
    o_ref[...] = _mm1(x_ref[...], w_ref[...])


def _matmul(x, w):
    m, k = x.shape
    n = w.shape[1]
    return pl.pallas_call(
        _mm_body,
        grid=(1,),
        in_specs=[_const_spec((m, k)), _const_spec((k, n))],
        out_specs=_const_spec((m, n)),
        out_shape=jax.ShapeDtypeStruct((m, n), F32),
        compiler_params=_params(("arbitrary",), 24 * MIB),
        name="shift_proj",
    )(x, w)


def _in_body(seq_mode, tm, x_ref, sc_ref, sh_ref, w_ref, mu_ref, zp_ref, lora_ref,
             w0a0_ref, kk_ref, ka_ref, lng_ref, lnb_ref, bd_ref,
             r_ref, lw_ref, kp_ref, v_ref, kap_ref, bb_ref, u_ref, vg_ref,
             ga_ref, gb_ref, hl_ref, carry_ref):
    nsplit = IN_ROW_GROUPS if seq_mode else 1
    rows = tm // nsplit
    if seq_mode:
        @pl.when(pl.program_id(1) == 0)
        def _():
            carry_ref[...] = zp_ref[0]

        carry = carry_ref[...]
    bd = bd_ref[...]
    for part in range(nsplit):
        sl = slice(part * rows, (part + 1) * rows)
        h = x_ref[0, sl, :] * (1.0 + sc_ref[0]) + sh_ref[0]
        hb = h.astype(BF16)
        proj = lambda lo, hi: _dot(hb, w_ref[:, lo:hi])
        za = proj(0, N_SHIFT_PAD)
        if seq_mode:
            row = lax.broadcasted_iota(jnp.int32, za.shape, 0)
            prev = jnp.where(row == 0, carry, pltpu.roll(za, 1, 0))
            carry = za[rows - 1:rows, :]
        else:
            prev = zp_ref[0]
        mix = za + mu_ref[...] * (prev - za)
        r = mix[:, 0:D_A]
        k = mix[:, D_A:2 * D_A]
        v = mix[:, 2 * D_A:3 * D_A]
        xwa = mix[:, 3 * D_A:N_SHIFT_PAD]
        lane = lax.broadcasted_iota(jnp.int32, xwa.shape, 1)
        lora_in = jnp.where(lane < R_LORA, jnp.tanh(xwa), xwa)
        pre = w0a0_ref[...] + _mm1(lora_in, lora_ref[...])
        yw = -pre[:, :D_A]
        softplus = jnp.maximum(yw, 0.0) + jnp.log1p(jnp.exp(-jnp.abs(yw)))
        lw = -jnp.exp(-softplus - 0.5)
        a = jax.nn.sigmoid(pre[:, D_A:])
        kk = k * kk_ref[...]
        kap = kk / jnp.maximum(jnp.sqrt(_segsum(kk * kk, bd)), 1e-12)
        r_ref[0, sl, :] = r
        lw_ref[0, sl, :] = lw
        kp_ref[0, sl, :] = k * (1.0 + (a - 1.0) * ka_ref[...])
        v_ref[0, sl, :] = v
        kap_ref[0, sl, :] = kap
        bb_ref[0, sl, :] = kap * a
        u_ref[0, sl, :] = _gelu(proj(COL_U, COL_VG)).astype(BF16)
        vg_ref[0, sl, :] = _layer_norm(_gelu(proj(COL_VG, COL_GA)), lng_ref[...], lnb_ref[...])
        ga_ref[0, sl, :] = jax.nn.sigmoid(proj(COL_GA, COL_GB)).astype(BF16)
        gb_ref[0, sl, :] = jax.nn.sigmoid(proj(COL_GB, N_IN_PAD)).astype(BF16)
    if seq_mode:
        carry_ref[...] = carry
        hl_ref[0] = h[rows - 1:rows, :]
    else:
        hl_ref[0] = h


def _in_stage(seq_mode, x, sc, sh, zprev, p):
    b, t, d = x.shape
    tm = IN_ROW_GROUPS * IN_GROUP_ROWS if seq_mode else t
    tmod = 1 if seq_mode else tm
    grid = (b, t // tm)
    tok = lambda n: pl.BlockSpec((1, tm, n), lambda i, j: (i, j, 0))
    mod = pl.BlockSpec((1, tmod, d), lambda i, j: (i, j if not seq_mode else 0, 0))
    zp_spec = (pl.BlockSpec((1, 1, N_SHIFT_PAD), lambda i, j: (i, 0, 0)) if seq_mode
               else tok(N_SHIFT_PAD))
    hl_spec = (pl.BlockSpec((1, 1, d), lambda i, j: (i, 0, 0)) if seq_mode else tok(d))
    hl_shape = (b, 1, d) if seq_mode else (b, t, d)
    out_cols = [D_A] * 6 + [D_B] * 2 + [d] * 2
    out_dtypes = [F32] * 6 + [BF16, F32] + [BF16] * 2
    est = (2 * tm * (d + N_SHIFT_PAD + sum(out_cols) + d) * 4 + 2 * d * N_IN_PAD * 2
           + 3 * tm * N_IN_PAD * 4 + 4 * MIB)
    outs = pl.pallas_call(
        functools.partial(_in_body, seq_mode, tm),
        grid=grid,
        in_specs=[tok(d), mod, mod,
                  _const_spec((d, N_IN_PAD), True), _const_spec((1, N_SHIFT_PAD)), zp_spec,
                  _const_spec((LANES, 2 * D_A)), _const_spec((1, 2 * D_A)),
                  _const_spec((1, D_A)), _const_spec((1, D_A)),
                  _const_spec((1, D_B)), _const_spec((1, D_B)),
                  _const_spec((D_A, D_A))],
        out_specs=[tok(n) for n in out_cols] + [hl_spec],
        out_shape=[jax.ShapeDtypeStruct((b, t, n), dt) for n, dt in zip(out_cols, out_dtypes)]
        + [jax.ShapeDtypeStruct(hl_shape, F32)],
        scratch_shapes=[pltpu.VMEM((1, N_SHIFT_PAD), F32)],
        compiler_params=_params(("arbitrary", "arbitrary"), est),
        name="in_stage_seq" if seq_mode else "in_stage_row",
    )(x, sc, sh, p["w_in"], p["mu"], zprev, p["lora"], p["w0a0"], p["k_k"], p["k_a"],
      p["lnv_g"], p["lnv_b"], p["bd"])
    return outs


def _chunk_pairs(x, nsub):
    c = WKV_CHUNK
    w = 2 * HEAD
    return jnp.stack([x[c * i:c * (i + 1), w * q:w * (q + 1)]
                      for i in range(nsub) for q in range(N_HEADS // 2)], axis=0)


def _masked_rowsum(mask_bf, x):
    h1, h2 = _split2(x)
    return _dot(mask_bf, h1) + _dot(mask_bf, h2)


def _masked_rowsum_t(x, mask):
    h1, h2 = _split2(x)
    m = mask.astype(BF16)
    return _dot(h1, m) + _dot(h2, m)


def _wkv_seq_body(nsub, r_ref, lw_ref, kp_ref, v_ref, kap_ref, bb_ref, s0_ref,
                  o_ref, sout_ref, s_scr):
    c = WKV_CHUNK
    tc = nsub * c
    npair = N_HEADS // 2

    @pl.when(pl.program_id(1) == 0)
    def _():
        for q in range(npair):
            s_scr[q] = jnp.concatenate([s0_ref[0, 2 * q], s0_ref[0, 2 * q + 1]], axis=-1)

    row_t = lax.broadcasted_iota(jnp.int32, (tc, tc), 0)
    col_t = lax.broadcasted_iota(jnp.int32, (tc, tc), 1)
    shift = c.bit_length() - 1
    same_chunk = (row_t >> shift) == (col_t >> shift)
    lw = lw_ref[0]
    g = _masked_rowsum((same_chunk & (row_t >= col_t)).astype(BF16), lw)
    g_end = _masked_rowsum(same_chunk.astype(BF16), lw)
    e_neg = jnp.exp(-g)
    e_end = jnp.exp(g_end - g)
    cp = functools.partial(_chunk_pairs, nsub=nsub)
    k = kp_ref[0]
    b = bb_ref[0]
    kap_t = cp(kap_ref[0] * jnp.exp(g - lw))
    b_t = cp(b * e_neg)
    k_t = cp(k * e_neg)
    r_t = cp(r_ref[0] * jnp.exp(g))
    b_e = cp(b * e_end)
    k_e = cp(k * e_end)
    vv = cp(v_ref[0])
    decay_end = cp(jnp.exp(g_end))

    row = lax.broadcasted_iota(jnp.int32, (c, 2 * c), 0)[None]
    lane = lax.broadcasted_iota(jnp.int32, (c, 2 * c), 1)[None]
    colp = lane & (c - 1)
    right = lane >= c
    row2 = lax.broadcasted_iota(jnp.int32, (2 * c, 2 * c), 0)[None]
    lane2 = lax.broadcasted_iota(jnp.int32, (2 * c, 2 * c), 1)[None]
    same_head = (row2 >= c) == (lane2 >= c)

    def bd(x):
        return jnp.concatenate([jnp.where(right, 0.0, x), jnp.where(right, x, 0.0)], axis=1)

    p_b = _mm1(jnp.concatenate([kap_t, r_t], axis=1), bd(b_t), _B_NT)
    l_b = jnp.where(row > colp, p_b[:, :c], 0.0)
    a_rb = jnp.where(row >= colp, p_b[:, c:], 0.0)
    bd_k = bd(k_t)
    l_k = jnp.where(row > colp, _mm1(kap_t, bd_k, _B_NT), 0.0)
    a_rk = jnp.where(row >= colp, _mm1(r_t, bd_k, _B_NT), 0.0)
    m = -l_b
    t_inv = jnp.where(row == colp, 1.0, 0.0) + m
    m = _mm1(m, bd(m), _B_NN)
    span = 2
    while 2 * span < c:
        both = _mm1(jnp.concatenate([m, t_inv], axis=1), bd(m), _B_NN)
        m = both[:, :c]
        t_inv = t_inv + both[:, c:]
        span *= 2
    t_inv = t_inv + _mm1(t_inv, bd(m), _B_NN)
    bd_v = bd(vv)
    a1 = -_mm1(t_inv, bd(kap_t), _B_NN)
    u0 = -_mm1(t_inv, bd(_mm1(l_k, bd_v, _B_NN)), _B_NN)
    a2 = r_t + _mm1(a_rb, bd(a1), _B_NN)
    o0 = _mm1(a_rb, bd(u0), _B_NN) + _mm1(a_rk, bd_v, _B_NN)
    g_bd = (jnp.where(same_head, _mm1(a1, b_e, _B_TN), 0.0)
            + jnp.where(row2 == lane2, decay_end[:, 0:1, :], 0.0))
    hh = _mm1(jnp.concatenate([u0, vv], axis=1),
              jnp.concatenate([b_e, k_e], axis=1), _B_TN)
    h_pair = jnp.where(right, hh[:, c:], hh[:, :c])

    s = s_scr[...]
    for i in range(nsub):
        ps = slice(npair * i, npair * (i + 1))
        o = _mm1(a2[ps], bd(s), _B_NT) + o0[ps]
        s = _mm1(s, g_bd[ps], _B_NN) + h_pair[ps]
        for q in range(npair):
            o_ref[0, c * i:c * (i + 1), 2 * HEAD * q:2 * HEAD * (q + 1)] = o[q]
    s_scr[...] = s

    @pl.when(pl.program_id(1) == pl.num_programs(1) - 1)
    def _():
        for q in range(npair):
            sout_ref[0, 2 * q] = s[q][:, :HEAD]
            sout_ref[0, 2 * q + 1] = s[q][:, HEAD:]


def _wkv_seq(r, lw, kp, v, kap, bb, s0):
    b, t, _ = r.shape
    tc = WKV_TILE
    nsub = tc // WKV_CHUNK
    tok = pl.BlockSpec((1, tc, D_A), lambda i, j: (i, j, 0))
    st = pl.BlockSpec((1, N_HEADS, HEAD, HEAD), lambda i, j: (i, 0, 0, 0))
    return pl.pallas_call(
        functools.partial(_wkv_seq_body, nsub),
        grid=(b, t // tc),
        in_specs=[tok] * 6 + [st],
        out_specs=[tok, st],
        out_shape=[jax.ShapeDtypeStruct((b, t, D_A), F32),
                   jax.ShapeDtypeStruct((b, N_HEADS, HEAD, HEAD), F32)],
        scratch_shapes=[pltpu.VMEM((N_HEADS // 2, HEAD, 2 * HEAD), F32)],
        compiler_params=_params(("arbitrary", "arbitrary"), 32 * MIB),
        name="wkv_seq",
    )(r, lw, kp, v, kap, bb, s0)


def _wkv_step_body(r_ref, lw_ref, kp_ref, v_ref, kap_ref, bb_ref, s0_ref, _, o_ref, sout_ref):
    for h in range(N_HEADS):
        hs = slice(HEAD * h, HEAD * (h + 1))
        row = lambda ref: ref[:, hs][:, None, :]
        s = s0_ref[:, h]
        sa = -jnp.sum(s * row(kap_ref), axis=-1, keepdims=True)
        v_col = v_ref[:, hs][:, :, None]
        s_new = s * jnp.exp(row(lw_ref)) + sa * row(bb_ref) + v_col * row(kp_ref)
        sout_ref[:, h] = s_new
        o_ref[:, hs] = jnp.sum(s_new * row(r_ref), axis=-1)


def _wkv_step(r, lw, kp, v, kap, bb, s0, s_new_all, layer):
    m = r.shape[0]
    nb = WKV_STEP_ROWS
    tok = pl.BlockSpec((nb, D_A), lambda i: (i, 0))
    st = pl.BlockSpec((nb, N_HEADS, HEAD, HEAD), lambda i: (i, 0, 0, 0))
    st_out = pl.BlockSpec((None, nb, N_HEADS, HEAD, HEAD), lambda i: (layer, i, 0, 0, 0))
    return pl.pallas_call(
        _wkv_step_body,
        grid=(m // nb,),
        in_specs=[tok] * 6 + [st, pl.BlockSpec(memory_space=pl.ANY)],
        out_specs=[tok, st_out],
        out_shape=[jax.ShapeDtypeStruct((m, D_A), F32),
                   jax.ShapeDtypeStruct(s_new_all.shape, F32)],
        input_output_aliases={7: 1},
        compiler_params=_params(("arbitrary",), 40 * MIB),
        name="wkv_step",
    )(r, lw, kp, v, kap, bb, s0, s_new_all)


def _merge_body(seq_mode, tm, x_ref, o_ref, r_ref, kp_ref, v_ref, u_ref, vg_ref,
                ga_ref, gb_ref, g1_ref, sc2_ref, sh2_ref, lxg_ref, lxb_ref, rk_ref,
                bd_ref, ws_ref, bs_ref, pa_ref, pb_ref, wo_ref, l1g_ref, l1b_ref,
                x1_ref, h2_ref):
    bd = bd_ref[...]
    o = o_ref[0]
    inv_n = 1.0 / HEAD
    mu = _segsum(o, bd, two_terms=True) * inv_n
    d = o - mu
    var = _segsum(d * d, bd) * inv_n
    on = d * lax.rsqrt(var + GN_EPS) * lxg_ref[...] + lxb_ref[...]
    v = v_ref[0]
    o_a = on + _segsum(r_ref[0] * kp_ref[0] * rk_ref[...], bd) * v
    vg = vg_ref[0]
    if seq_mode:
        row = lax.broadcasted_iota(jnp.int32, (CHUNK, CHUNK), 0)
        col = lax.broadcasted_iota(jnp.int32, (CHUNK, CHUNK), 1)
        lane = lax.broadcasted_iota(jnp.int32, (CHUNK, D_B), 1)
        gsz = D_B // N_GROUPS_B
        w_cat = jnp.concatenate(
            [jnp.where(row >= col, ws_ref[g], 0.0).astype(BF16) for g in range(N_GROUPS_B)], axis=1)
        pieces = []
        for j in range(tm // CHUNK):
            vc = vg[j * CHUNK:(j + 1) * CHUNK, :].astype(BF16)
            v_bd = jnp.concatenate(
                [jnp.where((lane >= g * gsz) & (lane < (g + 1) * gsz), vc, 0.0)
                 for g in range(N_GROUPS_B)], axis=0)
            pieces.append(_dot(w_cat, v_bd) + bs_ref[...])
        s = jnp.concatenate(pieces, axis=0) if len(pieces) > 1 else pieces[0]
    else:
        s = vg * ws_ref[...] + bs_ref[...]
    o_b = u_ref[0] * s
    y = _mm1(ga_ref[0] * _mm1(o_a, pa_ref[...]) + gb_ref[0] * _mm1(o_b, pb_ref[...]),
             wo_ref[...])
    x1 = _layer_norm(ALPHA * x_ref[0] + g1_ref[0] * y, l1g_ref[...], l1b_ref[...])
    x1_ref[0] = x1
    h2_ref[0] = x1 * (1.0 + sc2_ref[0]) + sh2_ref[0]


def _merge_stage(seq_mode, x, o, r, kp, v, u, vg, ga, gb, g1, sc2, sh2, p):
    b, t, d = x.shape
    tm = MERGE_TILE if seq_mode else t
    tmod = 1 if seq_mode else tm
    tok = lambda n: pl.BlockSpec((1, tm, n), lambda i, j: (i, j, 0))
    mod = pl.BlockSpec((1, tmod, d), lambda i, j: (i, j if not seq_mode else 0, 0))
    ws, bs = (p["w_spatial"], p["b_spatial_full"]) if seq_mode else (p["ws_row"], p["bs_row"])
    est = 2 * tm * (2 * d + 7 * D_A + 2 * d + 2 * d) * 4 + 8 * tm * d * 4 + 16 * MIB
    return pl.pallas_call(
        functools.partial(_merge_body, seq_mode, tm),
        grid=(b, t // tm),
        in_specs=[tok(d)] + [tok(D_A)] * 6 + [tok(d), tok(d), mod, mod, mod,
                  _const_spec((1, D_A)), _const_spec((1, D_A)), _const_spec((1, D_A)),
                  _const_spec((D_A, D_A)), _const_spec(ws.shape), _const_spec(bs.shape),
                  _const_spec((D_A, d)), _const_spec((D_B, d)), _const_spec((d, d)),
                  _const_spec((1, d)), _const_spec((1, d))],
        out_specs=[tok(d), tok(d)],
        out_shape=[jax.ShapeDtypeStruct((b, t, d), F32)] * 2,
        compiler_params=_params(("arbitrary", "arbitrary"), est),
        name="merge_seq" if seq_mode else "merge_row",
    )(x, o, r, kp, v, u, vg, ga, gb, g1, sc2, sh2, p["lnx_g"], p["lnx_b"], p["r_k"],
      p["bd"], ws, bs, p["w_branch_a"], p["w_branch_b"], p["w_out"], p["ln1_g"], p["ln1_b"])


def _route_body(h_ref, w_ref, b_ref, gate_ref, idx_ref, wts_ref, cnt_ref):
    lg = _mm3(h_ref[...], w_ref[...]) + b_ref[...]
    lane = lax.broadcasted_iota(jnp.int32, lg.shape, 1)
    lanef = lane.astype(F32)
    neg = -jnp.inf
    is_g = (lane >= N_EXPERTS) & (lane < N_EXPERTS + N_ROUTE_GROUPS)
    mg = jnp.max(jnp.where(is_g, lg, neg), axis=-1, keepdims=True)
    gidx = jnp.min(jnp.where(is_g & (lg == mg), lanef - N_EXPERTS, 1e9), axis=-1, keepdims=True)
    pg_sel = 1.0 / jnp.sum(jnp.where(is_g, jnp.exp(lg - mg), 0.0), axis=-1, keepdims=True)
    lo = gidx * EXP_PER_GROUP
    in_grp = (lanef >= lo) & (lanef < lo + EXP_PER_GROUP)
    t1 = jnp.max(jnp.where(in_grp, lg, neg), axis=-1, keepdims=True)
    i1 = jnp.min(jnp.where(in_grp & (lg == t1), lanef, 1e9), axis=-1, keepdims=True)
    rest = in_grp & (lanef != i1)
    t2 = jnp.max(jnp.where(rest, lg, neg), axis=-1, keepdims=True)
    i2 = jnp.min(jnp.where(rest & (lg == t2), lanef, 1e9), axis=-1, keepdims=True)
    e2 = jnp.exp(t2 - t1)
    w1 = pg_sel / (1.0 + e2)
    w2 = pg_sel * e2 / (1.0 + e2)
    gate_ref[...] = jnp.where(lanef == i1, w1, 0.0) + jnp.where(lanef == i2, w2, 0.0)
    tm = lg.shape[0]
    hit = (lanef == i1) | (lanef == i2)
    earlier = (lax.broadcasted_iota(jnp.int32, (tm, tm), 0)
               > lax.broadcasted_iota(jnp.int32, (tm, tm), 1))
    rank = _dot(earlier.astype(BF16), hit.astype(BF16))
    cnt = jnp.sum(hit.astype(F32), axis=0, keepdims=True)
    run_len = jnp.floor((cnt + (MOE_RUN_ALIGN - 1.0)) * (1.0 / MOE_RUN_ALIGN)) * MOE_RUN_ALIGN
    lower_expert = (lax.broadcasted_iota(jnp.int32, (LANES, LANES), 0)
                    < lax.broadcasted_iota(jnp.int32, (LANES, LANES), 1))
    run_start = _masked_rowsum_t(jnp.broadcast_to(run_len, (SUBLANES, LANES)), lower_expert)[0:1]
    lpos = run_start + rank
    lp1 = jnp.sum(jnp.where(lanef == i1, lpos, 0.0), axis=-1, keepdims=True)
    lp2 = jnp.sum(jnp.where(lanef == i2, lpos, 0.0), axis=-1, keepdims=True)
    cnt_ref[0] = cnt.astype(jnp.int32)
    idx = jnp.where(lane == 0, i1, jnp.where(lane == 1, i2, jnp.where(lane == 2, lp1, lp2)))
    idx_ref[...] = idx.astype(jnp.int32)
    wts_ref[...] = jnp.where(lane == 0, w1, w2)


def _route(h2, p):
    m, d = h2.shape
    tm = min(m, MOE_TILE)
    tok = pl.BlockSpec((tm, LANES), lambda i: (i, 0))
    return pl.pallas_call(
        _route_body,
        grid=(m // tm,),
        in_specs=[pl.BlockSpec((tm, d), lambda i: (i, 0)),
                  _const_spec((d, LANES)), _const_spec((1, LANES))],
        out_specs=[tok, tok, tok, pl.BlockSpec((1, 1, LANES), lambda i: (i, 0, 0))],
        out_shape=[jax.ShapeDtypeStruct((m, LANES), F32),
                   jax.ShapeDtypeStruct((m, LANES), jnp.int32),
                   jax.ShapeDtypeStruct((m, LANES), F32),
                   jax.ShapeDtypeStruct((m // tm, 1, LANES), jnp.int32)],
        compiler_params=_params(("arbitrary",), 24 * MIB),
        name="route",
    )(h2, p["w_route"], p["b_route"])


def _moe_body(h_ref, gate_ref, wg_ref, wu_ref, wd_ref, x1_ref, g2_ref, l2g_ref, l2b_ref,
              out_ref, acc_ref, xb_ref):
    e = pl.program_id(1)

    @pl.when(e == 0)
    def _():
        acc_ref[...] = jnp.zeros_like(acc_ref)
        xb_ref[...] = h_ref[...].astype(BF16)

    xb = xb_ref[...]
    pre = _dot(xb, wg_ref[0].astype(BF16))
    hid = pre * jax.nn.sigmoid(pre) * _dot(xb, wu_ref[0].astype(BF16))
    ye = _mm1(hid, wd_ref[0])
    gate = gate_ref[...]
    lane = lax.broadcasted_iota(jnp.int32, gate.shape, 1)
    ge = jnp.sum(jnp.where(lane == e, gate, 0.0), axis=-1, keepdims=True)
    acc_ref[...] += ge * ye

    @pl.when(e == pl.num_programs(1) - 1)
    def _():
        out_ref[...] = _layer_norm(ALPHA * x1_ref[...] + g2_ref[0] * acc_ref[...],
                                   l2g_ref[...], l2b_ref[...])


def _moe(seq_len, h2, gate, x1, g2, p):
    m, d = h2.shape
    layer = p["layer"]
    tm = min(seq_len if g2.shape[1] == 1 else m, DENSE_MOE_TILE)
    tok = lambda n: pl.BlockSpec((tm, n), lambda i, e: (i, 0))
    if g2.shape[1] == 1:
        tiles_per_seq = seq_len // tm
        g2_spec = pl.BlockSpec((1, 1, d), lambda i, e: (i // tiles_per_seq, 0, 0))
    else:
        g2_spec = pl.BlockSpec((1, tm, d), lambda i, e: (0, i, 0))
    est = 2 * tm * (3 * d + LANES) * 4 + tm * d * 6 + 4 * tm * d * 4 + 8 * MIB
    return pl.pallas_call(
        _moe_body,
        grid=(m // tm, N_EXPERTS),
        in_specs=[tok(d), tok(LANES),
                  pl.BlockSpec((None, 1, d, D_EXPERT), lambda i, e: (layer, e, 0, 0)),
                  pl.BlockSpec((None, 1, d, D_EXPERT), lambda i, e: (layer, e, 0, 0)),
                  pl.BlockSpec((None, 1, D_EXPERT, d), lambda i, e: (layer, e, 0, 0)),
                  tok(d), g2_spec, _const_spec((1, d)), _const_spec((1, d))],
        out_specs=tok(d),
        out_shape=jax.ShapeDtypeStruct((m, d), F32),
        scratch_shapes=[pltpu.VMEM((tm, d), F32), pltpu.VMEM((tm, d), BF16)],
        compiler_params=_params(("arbitrary", "arbitrary"), est),
        name="moe_dense",
    )(h2, gate, p["w_exp_gate"], p["w_exp_up"], p["w_exp_down"], x1, g2, p["ln2_g"], p["ln2_b"])


MOE_TILE = 512
MOE_BLOCK = 512
MOE_RUN_ALIGN = 16
MXU_DEPTH = 256
MOE_LOCAL_ROWS = -(-(2 * MOE_TILE + N_EXPERTS * (MOE_RUN_ALIGN - 1)) // MXU_DEPTH) * MXU_DEPTH
MOE_SLABS = tuple(MOE_TILE >> k for k in range((MOE_TILE // MOE_RUN_ALIGN).bit_length()))


def _for_each_slab(run_ref, make_copy, fn):
    def one_run(e, c):
        dst = run_ref[0, 0, e]
        src = run_ref[0, 0, N_EXPERTS + e]
        n = run_ref[0, 0, 2 * N_EXPERTS + e]
        for slab in MOE_SLABS:
            off = n & (-2 * slab)

            @pl.when((n & slab) != 0)
            def _():
                fn(make_copy(pl.multiple_of(dst + off, MOE_RUN_ALIGN),
                             pl.multiple_of(src + off, MOE_RUN_ALIGN), slab))
        return c

    lax.fori_loop(0, N_EXPERTS, one_run, 0)


def _dispatch_body(tail_ref, run_ref, prev_run_ref, h_ref, idx_ref, xs_hbm, zero_buf, loc_buf,
                   sem, zsem, usem):
    nb_max = xs_hbm.shape[0] // MOE_BLOCK
    nt = MOE_TILE

    def zero_copy(row0, zero_sem):
        return pltpu.make_async_copy(
            zero_buf, xs_hbm.at[pl.ds(pl.multiple_of(row0, MOE_BLOCK), MOE_BLOCK)], zero_sem)

    def each_unused_copy(fn):
        def unused(j, c):
            fn(zero_copy(j * MOE_BLOCK, usem))
            return c

        lax.fori_loop(tail_ref[N_EXPERTS], nb_max, unused, 0)

    @pl.when(pl.program_id(0) == 0)
    def _():
        zero_buf[...] = jnp.zeros_like(zero_buf)

        def each_tail_copy(fn):
            for e in range(N_EXPERTS):
                @pl.when(tail_ref[e] >= 0)
                def _():
                    fn(zero_copy(tail_ref[e], zsem))

        each_tail_copy(lambda cp: cp.start())
        each_unused_copy(lambda cp: cp.start())
        each_tail_copy(lambda cp: cp.wait())

    lp = idx_ref[...].astype(F32)
    eye = (lax.broadcasted_iota(jnp.int32, (nt, nt), 0)
           == lax.broadcasted_iota(jnp.int32, (nt, nt), 1))
    as_row = lambda col: jnp.sum(jnp.where(eye, col, 0.0), axis=0, keepdims=True)
    r = lax.broadcasted_iota(jnp.int32, (MOE_LOCAL_ROWS, nt), 0).astype(F32)
    pick = (r == as_row(lp[:, 2:3])) | (r == as_row(lp[:, 3:4]))
    step = pl.program_id(0)
    slot = step & 1
    loc_buf[slot] = _dot(pick.astype(BF16), h_ref[...].astype(BF16)).astype(BF16)

    def copies_from(which):
        def make_copy(dst, src, rows):
            return pltpu.make_async_copy(loc_buf.at[which, pl.ds(src, rows)],
                                         xs_hbm.at[pl.ds(dst, rows)], sem.at[which])
        return make_copy

    _for_each_slab(run_ref, copies_from(slot), lambda cp: cp.start())

    @pl.when(step > 0)
    def _():
        _for_each_slab(prev_run_ref, copies_from(1 - slot), lambda cp: cp.wait())

    @pl.when(step == pl.num_programs(0) - 1)
    def _():
        _for_each_slab(run_ref, copies_from(slot), lambda cp: cp.wait())
        each_unused_copy(lambda cp: cp.wait())


def _dispatch(h2, idx, runs, tail_start, n_rows):
    m, d = h2.shape
    nt = MOE_TILE
    run_spec = lambda at: pl.BlockSpec((1, 1, LANES), lambda i, tail: (at(i), 0, 0),
                                       memory_space=pltpu.SMEM)
    grid_spec = pltpu.PrefetchScalarGridSpec(
        num_scalar_prefetch=1,
        grid=(m // nt,),
        in_specs=[run_spec(lambda i: i), run_spec(lambda i: jnp.maximum(i - 1, 0)),
                  pl.BlockSpec((nt, d), lambda i, tail: (i, 0)),
                  pl.BlockSpec((nt, LANES), lambda i, tail: (i, 0))],
        out_specs=pl.BlockSpec(memory_space=pl.ANY),
        scratch_shapes=[pltpu.VMEM((MOE_BLOCK, d), BF16),
                        pltpu.VMEM((2, MOE_LOCAL_ROWS, d), BF16),
                        pltpu.SemaphoreType.DMA((2,)), pltpu.SemaphoreType.DMA,
                        pltpu.SemaphoreType.DMA],
    )
    return pl.pallas_call(
        _dispatch_body,
        grid_spec=grid_spec,
        out_shape=jax.ShapeDtypeStruct((n_rows, d), BF16),
        compiler_params=_params(("arbitrary",), 40 * MIB),
        name="moe_dispatch",
    )(tail_start, runs, runs, h2, idx)


GMM_RING = 3


def _gmm_body(layer, be_ref, nb_ref, first_ref, next_ref, wslot_ref,
              x_hbm, wg_hbm, wu_hbm, wd_hbm, y_ref,
              x_ring, wg_stage, wu_stage, wd_stage, wg_bf, wu_bf, wd_bf, sem, wsem):
    j = pl.program_id(0)
    n_live = nb_ref[0]

    def fetch(step):
        slot = step % GMM_RING
        return pltpu.make_async_copy(
            x_hbm.at[pl.ds(pl.multiple_of(step * MOE_BLOCK, MOE_BLOCK), MOE_BLOCK)],
            x_ring.at[slot], sem.at[slot])

    def weight_copies(expert, slot):
        return [pltpu.make_async_copy(hbm.at[layer, expert], stage.at[slot], wsem.at[slot, i])
                for i, (hbm, stage) in enumerate(((wg_hbm, wg_stage), (wu_hbm, wu_stage),
                                                  (wd_hbm, wd_stage)))]

    @pl.when(j == 0)
    def _():
        for cp in weight_copies(be_ref[0], 0):
            cp.start()
        for ahead in range(GMM_RING - 1):
            @pl.when(ahead < n_live)
            def _():
                fetch(ahead).start()

    @pl.when(j + (GMM_RING - 1) < n_live)
    def _():
        fetch(j + (GMM_RING - 1)).start()

    @pl.when(j < n_live)
    def _():
        @pl.when(first_ref[j] == 1)
        def _():
            slot = wslot_ref[j]
            for cp in weight_copies(be_ref[j], slot):
                cp.wait()
            wg_bf[...] = wg_stage[slot].astype(BF16)
            wu_bf[...] = wu_stage[slot].astype(BF16)
            wd_bf[...] = wd_stage[slot].astype(BF16)

            @pl.when(next_ref[j] >= 0)
            def _():
                for cp in weight_copies(next_ref[j], 1 - slot):
                    cp.start()

        fetch(j).wait()
        xb = x_ring[j % GMM_RING]
        pre = _dot(xb, wg_bf[...])
        hid = pre * jax.nn.sigmoid(pre) * _dot(xb, wu_bf[...])
        y_ref[...] = _dot(hid.astype(BF16), wd_bf[...]).astype(BF16)

    @pl.when(j >= nb_ref[0])
    def _():
        y_ref[...] = jnp.zeros_like(y_ref)


def _gmm(xs, blk_expert, n_blocks, run_first, run_next, run_slot, p):
    n_rows, d = xs.shape
    nb_max = n_rows // MOE_BLOCK
    any_spec = pl.BlockSpec(memory_space=pl.ANY)
    grid_spec = pltpu.PrefetchScalarGridSpec(
        num_scalar_prefetch=5,
        grid=(nb_max,),
        in_specs=[any_spec] * 4,
        out_specs=pl.BlockSpec((MOE_BLOCK, d), lambda j, *_: (j, 0)),
        scratch_shapes=[pltpu.VMEM((GMM_RING, MOE_BLOCK, d), BF16),
                        pltpu.VMEM((2, d, D_EXPERT), F32), pltpu.VMEM((2, d, D_EXPERT), F32),
                        pltpu.VMEM((2, D_EXPERT, d), F32),
                        pltpu.VMEM((d, D_EXPERT), BF16), pltpu.VMEM((d, D_EXPERT), BF16),
                        pltpu.VMEM((D_EXPERT, d), BF16),
                        pltpu.SemaphoreType.DMA((GMM_RING,)), pltpu.SemaphoreType.DMA((2, 3))],
    )
    return pl.pallas_call(
        functools.partial(_gmm_body, p["layer"]),
        grid_spec=grid_spec,
        out_shape=jax.ShapeDtypeStruct((n_rows, d), BF16),
        compiler_params=_params(("arbitrary",), 24 * MIB),
        name="moe_gmm",
    )(blk_expert, n_blocks, run_first, run_next, run_slot,
      xs, p["w_exp_gate"], p["w_exp_up"], p["w_exp_down"])


def _combine_body(run_ref, next_run_ref, ys_hbm, idx_ref, wts_ref, x1_ref, g2_ref, l2g_ref,
                  l2b_ref, out_ref, loc_buf, sem):
    nt = MOE_TILE
    step = pl.program_id(0)
    slot = step & 1

    def copies_into(which):
        def make_copy(dst, src, rows):
            return pltpu.make_async_copy(ys_hbm.at[pl.ds(dst, rows)],
                                         loc_buf.at[which, pl.ds(src, rows)], sem.at[which])
        return make_copy

    @pl.when(step == 0)
    def _():
        loc_buf[...] = jnp.zeros_like(loc_buf)
        _for_each_slab(run_ref, copies_into(slot), lambda cp: cp.start())

    @pl.when(step < pl.num_programs(0) - 1)
    def _():
        _for_each_slab(next_run_ref, copies_into(1 - slot), lambda cp: cp.start())

    _for_each_slab(run_ref, copies_into(slot), lambda cp: cp.wait())
    lp = idx_ref[...].astype(F32)
    w = wts_ref[...]
    c = lax.broadcasted_iota(jnp.int32, (nt, MOE_LOCAL_ROWS), 1).astype(F32)
    sel = (jnp.where(c == lp[:, 2:3], w[:, 0:1], 0.0)
           + jnp.where(c == lp[:, 3:4], w[:, 1:2], 0.0))
    sel_hi, sel_lo = _split2(sel)
    rows_bf = loc_buf[slot]
    moe = _dot(sel_hi, rows_bf) + _dot(sel_lo, rows_bf)
    out_ref[...] = _layer_norm(ALPHA * x1_ref[...] + g2_ref[0] * moe, l2g_ref[...], l2b_ref[...])


def _combine(seq_len, ys, idx, wts, runs, x1, g2, p):
    m, d = x1.shape
    nt = MOE_TILE
    tiles_per_seq = seq_len // nt
    tok = lambda n: pl.BlockSpec((nt, n), lambda i: (i, 0))
    n_tiles = m // nt
    run_spec = lambda at: pl.BlockSpec((1, 1, LANES), lambda i: (at(i), 0, 0),
                                       memory_space=pltpu.SMEM)
    return pl.pallas_call(
        _combine_body,
        grid=(n_tiles,),
        in_specs=[run_spec(lambda i: i), run_spec(lambda i: jnp.minimum(i + 1, n_tiles - 1)),
                  pl.BlockSpec(memory_space=pl.ANY), tok(LANES), tok(LANES), tok(d),
                  pl.BlockSpec((1, 1, d), lambda i: (i // tiles_per_seq, 0, 0)),
                  _const_spec((1, d)), _const_spec((1, d))],
        out_specs=tok(d),
        out_shape=jax.ShapeDtypeStruct((m, d), F32),
        scratch_shapes=[pltpu.VMEM((2, MOE_LOCAL_ROWS, d), BF16), pltpu.SemaphoreType.DMA((2,))],
        compiler_params=_params(("arbitrary",), 48 * MIB),
        name="moe_combine",
    )(runs, runs, ys, idx, wts, x1, g2, p["ln2_g"], p["ln2_b"])


def _moe_routed(seq_len, h2, idx, wts, tile_cnt, x1, g2, p):
    m, d = h2.shape
    blk = MOE_BLOCK
    n_tiles = m // MOE_TILE
    worst_rows = 2 * m + n_tiles * N_EXPERTS * (MOE_RUN_ALIGN - 1) + N_EXPERTS * (blk - 1)
    nb_max = -(-worst_rows // blk)
    cnt = tile_cnt[:, 0, :N_EXPERTS]
    run_len = ((cnt + MOE_RUN_ALIGN - 1) // MOE_RUN_ALIGN) * MOE_RUN_ALIGN
    local_row = jnp.cumsum(run_len, axis=1) - run_len
    rows_before = jnp.cumsum(run_len, axis=0) - run_len
    total = jnp.sum(run_len, axis=0)
    padded = ((total + blk - 1) // blk) * blk
    ends = jnp.cumsum(padded)
    starts = ends - padded
    n_blocks = (ends[-1] // blk).astype(jnp.int32).reshape(1)
    first_row = jnp.arange(nb_max, dtype=jnp.int32) * blk
    blk_expert = jnp.minimum(
        jnp.sum((first_row[:, None] >= ends[None, :]).astype(jnp.int32), axis=1),
        N_EXPERTS - 1).astype(jnp.int32)
    tail_start = jnp.concatenate(
        [jnp.where(padded > 0, ends - blk, -1).astype(jnp.int32), n_blocks])
    runs = jnp.concatenate(
        [starts[None, :] + rows_before, local_row, run_len,
         jnp.zeros((n_tiles, LANES - 3 * N_EXPERTS), jnp.int32)], axis=1).astype(jnp.int32)
    runs = runs.reshape(n_tiles, 1, LANES)
    prev_expert = jnp.concatenate([jnp.full((1,), -1, jnp.int32), blk_expert[:-1]])
    run_first = (blk_expert != prev_expert).astype(jnp.int32)
    run_slot = ((jnp.cumsum(run_first) - 1) % 2).astype(jnp.int32)
    used = padded > 0
    later = jnp.arange(N_EXPERTS)[None, :] > jnp.arange(N_EXPERTS)[:, None]
    next_used = jnp.min(jnp.where(later & used[None, :], jnp.arange(N_EXPERTS)[None, :], N_EXPERTS),
                        axis=1)
    run_next = jnp.where(next_used < N_EXPERTS, next_used, -1).astype(jnp.int32)[blk_expert]
    xs = _dispatch(h2, idx, runs, tail_start, nb_max * blk)
    ys = _gmm(xs, blk_expert, n_blocks, run_first, run_next, run_slot, p)
    return _combine(seq_len, ys, idx, wts, runs, x1, g2, p)


def _prep_layer(l, w_in, mu_shift, w0, w_decay_up, a0, w_iclr_up, k_k, k_a, r_k,
                lnx_g, lnx_b, lnv_g, lnv_b, w_spatial, b_spatial, w_branch_a, w_branch_b,
                w_out, ln1_g, ln1_b, w_route_group, b_route_group, w_route_expert,
                b_route_expert, w_exp_gate, w_exp_up, w_exp_down, ln2_g, ln2_b):
    d = D_MODEL
    pad_a = N_SHIFT_PAD - N_SHIFT
    wi = w_in[l]
    w_in_p = jnp.concatenate(
        [wi[:, :N_SHIFT], jnp.zeros((d, pad_a), F32), wi[:, N_SHIFT:]], axis=1).astype(BF16)
    mu = jnp.concatenate([mu_shift[l], jnp.zeros((pad_a,), F32)])[None]
    lora = jnp.zeros((LANES, 2 * D_A), F32)
    lora = lora.at[:R_LORA, :D_A].set(w_decay_up[l]).at[R_LORA:2 * R_LORA, D_A:].set(w_iclr_up[l])
    seg = jnp.arange(D_A) // HEAD
    row1 = lambda x: x.reshape(1, -1)
    gsz = D_B // N_GROUPS_B
    w_route = jnp.concatenate(
        [w_route_expert[l], w_route_group[l],
         jnp.zeros((d, LANES - N_EXPERTS - N_ROUTE_GROUPS), F32)], axis=1)
    b_route = jnp.concatenate(
        [b_route_expert[l], b_route_group[l],
         jnp.zeros((LANES - N_EXPERTS - N_ROUTE_GROUPS,), F32)])[None]
    return dict(
        w_in=w_in_p, mu=mu, lora=lora,
        w0a0=jnp.concatenate([w0[l], a0[l]])[None],
        k_k=row1(k_k[l]), k_a=row1(k_a[l]), r_k=row1(r_k[l]),
        lnx_g=row1(lnx_g[l]), lnx_b=row1(lnx_b[l]),
        lnv_g=row1(lnv_g[l]), lnv_b=row1(lnv_b[l]),
        bd=(seg[:, None] == seg[None, :]).astype(BF16),
        w_spatial=w_spatial[l],
        b_spatial_full=jnp.repeat(b_spatial[l].T, gsz, axis=1),
        ws_row=jnp.repeat(w_spatial[l][:, 0, 0], gsz)[None],
        bs_row=jnp.repeat(b_spatial[l][:, 0], gsz)[None],
        w_branch_a=w_branch_a[l].astype(BF16), w_branch_b=w_branch_b[l].astype(BF16),
        w_out=w_out[l].astype(BF16), ln1_g=row1(ln1_g[l]), ln1_b=row1(ln1_b[l]),
        w_route=w_route, b_route=b_route,
        w_exp_gate=w_exp_gate, w_exp_up=w_exp_up, w_exp_down=w_exp_down, layer=l,
        ln2_g=row1(ln2_g[l]), ln2_b=row1(ln2_b[l]),
    )


def _trunk(seq_mode, x, mods, wkv_in, shift_in, preps):
    b, t, d = x.shape
    wkv_out, shift_out, v_out = [], [], []
    wkv_acc = None if seq_mode else jnp.zeros(wkv_in.shape, F32)
    for l in range(DEPTH):
        p = preps[l]
        sh1, sc1, g1, sh2, sc2, g2 = mods[l]
        if seq_mode:
            zprev = jnp.zeros((b, 1, N_SHIFT_PAD), F32) if shift_in is None else shift_in[l]
        else:
            zprev = _matmul(shift_in[l], p["w_in"][:, :N_SHIFT_PAD]).reshape(b, t, N_SHIFT_PAD)
        r, lw, kp, v, kap, bb, u, vg, ga, gb, hl = _in_stage(seq_mode, x, sc1, sh1, zprev, p)
        if seq_mode:
            o, s_new = _wkv_seq(r, lw, kp, v, kap, bb, wkv_in[l])
        else:
            flat = lambda a: a.reshape(t, D_A)
            o, wkv_acc = _wkv_step(flat(r), flat(lw), flat(kp), flat(v), flat(kap), flat(bb),
                                   wkv_in[l], wkv_acc, l)
            o = o.reshape(b, t, D_A)
        x1, h2 = _merge_stage(seq_mode, x, o, r, kp, v, u, vg, ga, gb, g1, sc2, sh2, p)
        m = b * t
        h2f = h2.reshape(m, d)
        gate, idx, wts, tile_cnt = _route(h2f, p)
        if seq_mode:
            x = _moe_routed(t, h2f, idx, wts, tile_cnt, x1.reshape(m, d), g2, p)
        else:
            x = _moe(t, h2f, gate, x1.reshape(m, d), g2, p)
        x = x.reshape(b, t, d)
        if seq_mode:
            wkv_out.append(s_new)
        shift_out.append(hl)
        v_out.append(vg)
    if not seq_mode:
        wkv_out = wkv_acc
    return x, wkv_out, shift_out, v_out


def kernel(x_prompt, x_sample, c_prompt, c_sample, state_wkv, state_shift, w_ada, b_ada, w_in, mu_shift, w0, w_decay_up, a0, w_iclr_up, k_k, k_a, r_k, lnx_g, lnx_b, lnv_g, lnv_b, w_spatial, b_spatial, w_branch_a, w_branch_b, w_out, ln1_g, ln1_b, w_route_group, b_route_group, w_route_expert, b_route_expert, w_exp_gate, w_exp_up, w_exp_down, ln2_g, ln2_b):
    bp, tp, d = x_prompt.shape
    bs = x_sample.shape[0]
    layer_params = (w_in, mu_shift, w0, w_decay_up, a0, w_iclr_up, k_k, k_a, r_k, lnx_g,
                    lnx_b, lnv_g, lnv_b, w_spatial, b_spatial, w_branch_a, w_branch_b, w_out,
                    ln1_g, ln1_b, w_route_group, b_route_group, w_route_expert,
                    b_route_expert, w_exp_gate, w_exp_up, w_exp_down, ln2_g, ln2_b)
    preps = [_prep_layer(l, *layer_params) for l in range(DEPTH)]
    mod_all = _ada(jnp.concatenate([c_prompt, c_sample], axis=0), w_ada, b_ada)
    mods_p, mods_s = [], []
    for l in range(DEPTH):
        parts = jnp.split(mod_all[l], 6, axis=-1)
        mods_p.append([q[:bp].reshape(bp, 1, d) for q in parts])
        mods_s.append([q[bp:].reshape(1, bs, d) for q in parts])

    wkv0 = jnp.zeros((DEPTH, bp, N_HEADS, HEAD, HEAD), F32)
    y_p, wkv_p, shift_p, _ = _trunk(True, x_prompt, mods_p, wkv0, None, preps)
    y_s, wkv_s, shift_s, v_s = _trunk(False, x_sample.reshape(1, bs, d), mods_s, state_wkv,
                                      state_shift, preps)
    return (y_p,
            y_s.reshape(bs, 1, d),
            jnp.stack(wkv_p),
            jnp.stack([s.reshape(bp, d) for s in shift_p]),
            wkv_s,
            jnp.stack([s.reshape(bs, d) for s in shift_s]),
            jnp.stack([q.reshape(bs, 1, D_B) for q in v_s]))
```

```python
import functools

import jax
import jax.numpy as jnp
from jax import lax
from jax.experimental import pallas as pl
from jax.experimental.pallas import tpu as pltpu

F32 = jnp.float32
BF16 = jnp.bfloat16

D_MODEL = 1024
DEPTH = 2
HEAD = 64
N_HEADS = 8
D_A = N_HEADS * HEAD
R_LORA = 32
CHUNK = 128
N_GROUPS_B = 8
D_B = 512
N_SHIFT = 3 * D_A + 2 * R_LORA
N_ROUTE_GROUPS = 4
EXP_PER_GROUP = 8
N_EXPERTS = N_ROUTE_GROUPS * EXP_PER_GROUP
D_EXPERT = 256
ALPHA = (2 * DEPTH) ** 0.25
LN_EPS = 1e-5
GN_EPS = 64e-5

LANES = 128
SUBLANES = 8
MIB = 2 ** 20
N_SHIFT_PAD = 13 * LANES
COL_U = N_SHIFT_PAD
COL_VG = COL_U + D_B
COL_GA = COL_VG + D_B
COL_GB = COL_GA + D_MODEL
N_IN_PAD = COL_GB + D_MODEL
WKV_CHUNK = 64
WKV_TILE = 256
WKV_STEP_ROWS = 32
MERGE_TILE = 512
DENSE_MOE_TILE = 1024
ADA_COLS = 1024
IN_ROW_GROUPS = 2
IN_GROUP_ROWS = 256
VMEM_CAP_BYTES = 60000 * 1024

_NN = (((1,), (0,)), ((), ()))
_B_NT = (((2,), (2,)), ((0,), (0,)))
_B_NN = (((2,), (1,)), ((0,), (0,)))
_B_TN = (((1,), (1,)), ((0,), (0,)))


def _dot(a, b, dims=_NN):
    return lax.dot_general(a, b, dims, preferred_element_type=F32)


def _split2(x):
    hi = x.astype(BF16)
    lo = (x - hi.astype(F32)).astype(BF16)
    return hi, lo


def _mm1(a, b, dims=_NN):
    return _dot(a.astype(BF16), b.astype(BF16), dims)


def _mm3(a, b, dims=_NN):
    ah, al = _split2(a)
    bh, bl = _split2(b)
    return _dot(ah, bh, dims) + (_dot(ah, bl, dims) + _dot(al, bh, dims))


def _segsum(x, bd, two_terms=False):
    if not two_terms:
        return _dot(x.astype(BF16), bd)
    hi, lo = _split2(x)
    return _dot(hi, bd) + _dot(lo, bd)


def _layer_norm(x, g, b):
    mu = jnp.mean(x, axis=-1, keepdims=True)
    d = x - mu
    var = jnp.mean(d * d, axis=-1, keepdims=True)
    return d * lax.rsqrt(var + LN_EPS) * g + b


def _gelu(x):
    return 0.5 * x * (1.0 + lax.erf(x * 0.7071067811865476))


def _params(sem, est_bytes):
    limit = int(min(VMEM_CAP_BYTES, max(est_bytes, 16 * 1024 * 1024)))
    return pltpu.CompilerParams(dimension_semantics=sem, vmem_limit_bytes=limit)


def _const_spec(shape, single_buffer=False):
    nd = len(shape)
    if single_buffer:
        return pl.BlockSpec(shape, lambda *_: (0,) * nd, pipeline_mode=pl.Buffered(1))
    return pl.BlockSpec(shape, lambda *_: (0,) * nd)


def _ada_body(c_ref, w_ref, b_ref, o_ref):
    c = c_ref[...]
    s = c * jax.nn.sigmoid(c)
    o_ref[0] = _mm3(s, w_ref[0]) + b_ref[0]


def _ada(c_all, w_ada, b_ada):
    depth, d, n6 = w_ada.shape
    m = c_all.shape[0]
    tn = ADA_COLS
    return pl.pallas_call(
        _ada_body,
        grid=(depth, n6 // tn),
        in_specs=[
            pl.BlockSpec((m, d), lambda l, j: (0, 0)),
            pl.BlockSpec((1, d, tn), lambda l, j: (l, 0, j)),
            pl.BlockSpec((1, 1, tn), lambda l, j: (l, 0, j)),
        ],
        out_specs=pl.BlockSpec((1, m, tn), lambda l, j: (l, 0, j)),
        out_shape=jax.ShapeDtypeStruct((depth, m, n6), F32),
        compiler_params=_params(("arbitrary", "arbitrary"), 24 * MIB),
        name="ada_mod",
    )(c_all, w_ada, b_ada.reshape(depth, 1, n6))


def _mm_body(x_ref, w_ref, o_ref):
    o_ref[...] = _mm1(x_ref[...], w_ref[...])


def _matmul(x, w):
    m, k = x.shape
    n = w.shape[1]
    return pl.pallas_call(
        _mm_body,
        grid=(1,),
        in_specs=[_const_spec((m, k)), _const_spec((k, n))],
        out_specs=_const_spec((m, n)),
        out_shape=jax.ShapeDtypeStruct((m, n), F32),
        compiler_params=_params(("arbitrary",), 24 * MIB),
        name="shift_proj",
    )(x, w)


def _in_body(seq_mode, tm, x_ref, sc_ref, sh_ref, w_ref, mu_ref, zp_ref, lora_ref,
             w0a0_ref, kk_ref, ka_ref, lng_ref, lnb_ref, bd_ref,
             r_ref, lw_ref, kp_ref, v_ref, kap_ref, bb_ref, u_ref, vg_ref,
             ga_ref, gb_ref, hl_ref, carry_ref):
    nsplit = IN_ROW_GROUPS if seq_mode else 1
    rows = tm // nsplit
    if seq_mode:
        @pl.when(pl.program_id(1) == 0)
        def _():
            carry_ref[...] = zp_ref[0]

        carry = carry_ref[...]
    bd = bd_ref[...]
    for part in range(nsplit):
        sl = slice(part * rows, (part + 1) * rows)
        h = x_ref[0, sl, :] * (1.0 + sc_ref[0]) + sh_ref[0]
        hb = h.astype(BF16)
        proj = lambda lo, hi: _dot(hb, w_ref[:, lo:hi])
        za = proj(0, N_SHIFT_PAD)
        if seq_mode:
            row = lax.broadcasted_iota(jnp.int32, za.shape, 0)
            prev = jnp.where(row == 0, carry, pltpu.roll(za, 1, 0))
            carry = za[rows - 1:rows, :]
        else:
            prev = zp_ref[0]
        mix = za + mu_ref[...] * (prev - za)
        r = mix[:, 0:D_A]
        k = mix[:, D_A:2 * D_A]
        v = mix[:, 2 * D_A:3 * D_A]
        xwa = mix[:, 3 * D_A:N_SHIFT_PAD]
        lane = lax.broadcasted_iota(jnp.int32, xwa.shape, 1)
        lora_in = jnp.where(lane < R_LORA, jnp.tanh(xwa), xwa)
        pre = w0a0_ref[...] + _mm1(lora_in, lora_ref[...])
        yw = -pre[:, :D_A]
        softplus = jnp.maximum(yw, 0.0) + jnp.log1p(jnp.exp(-jnp.abs(yw)))
        lw = -jnp.exp(-softplus - 0.5)
        a = jax.nn.sigmoid(pre[:, D_A:])
        kk = k * kk_ref[...]
        kap = kk / jnp.maximum(jnp.sqrt(_segsum(kk * kk, bd)), 1e-12)
        r_ref[0, sl, :] = r
        lw_ref[0, sl, :] = lw
        kp_ref[0, sl, :] = k * (1.0 + (a - 1.0) * ka_ref[...])
        v_ref[0, sl, :] = v
        kap_ref[0, sl, :] = kap
        bb_ref[0, sl, :] = kap * a
        u_ref[0, sl, :] = _gelu(proj(COL_U, COL_VG)).astype(BF16)
        vg_ref[0, sl, :] = _layer_norm(_gelu(proj(COL_VG, COL_GA)), lng_ref[...], lnb_ref[...])
        ga_ref[0, sl, :] = jax.nn.sigmoid(proj(COL_GA, COL_GB)).astype(BF16)
        gb_ref[0, sl, :] = jax.nn.sigmoid(proj(COL_GB, N_IN_PAD)).astype(BF16)
    if seq_mode:
        carry_ref[...] = carry
        hl_ref[0] = h[rows - 1:rows, :]
    else:
        hl_ref[0] = h


def _in_stage(seq_mode, x, sc, sh, zprev, p):
    b, t, d = x.shape
    tm = IN_ROW_GROUPS * IN_GROUP_ROWS if seq_mode else t
    tmod = 1 if seq_mode else tm
    grid = (b, t // tm)
    tok = lambda n: pl.BlockSpec((1, tm, n), lambda i, j: (i, j, 0))
    mod = pl.BlockSpec((1, tmod, d), lambda i, j: (i, j if not seq_mode else 0, 0))
    zp_spec = (pl.BlockSpec((1, 1, N_SHIFT_PAD), lambda i, j: (i, 0, 0)) if seq_mode
               else tok(N_SHIFT_PAD))
    hl_spec = (pl.BlockSpec((1, 1, d), lambda i, j: (i, 0, 0)) if seq_mode else tok(d))
    hl_shape = (b, 1, d) if seq_mode else (b, t, d)
    out_cols = [D_A] * 6 + [D_B] * 2 + [d] * 2
    out_dtypes = [F32] * 6 + [BF16, F32] + [BF16] * 2
    est = (2 * tm * (d + N_SHIFT_PAD + sum(out_cols) + d) * 4 + 2 * d * N_IN_PAD * 2
           + 3 * tm * N_IN_PAD * 4 + 4 * MIB)
    outs = pl.pallas_call(
        functools.partial(_in_body, seq_mode, tm),
        grid=grid,
        in_specs=[tok(d), mod, mod,
                  _const_spec((d, N_IN_PAD), True), _const_spec((1, N_SHIFT_PAD)), zp_spec,
                  _const_spec((LANES, 2 * D_A)), _const_spec((1, 2 * D_A)),
                  _const_spec((1, D_A)), _const_spec((1, D_A)),
                  _const_spec((1, D_B)), _const_spec((1, D_B)),
                  _const_spec((D_A, D_A))],
        out_specs=[tok(n) for n in out_cols] + [hl_spec],
        out_shape=[jax.ShapeDtypeStruct((b, t, n), dt) for n, dt in zip(out_cols, out_dtypes)]
        + [jax.ShapeDtypeStruct(hl_shape, F32)],
        scratch_shapes=[pltpu.VMEM((1, N_SHIFT_PAD), F32)],
        compiler_params=_params(("arbitrary", "arbitrary"), est),
        name="in_stage_seq" if seq_mode else "in_stage_row",
    )(x, sc, sh, p["w_in"], p["mu"], zprev, p["lora"], p["w0a0"], p["k_k"], p["k_a"],
      p["lnv_g"], p["lnv_b"], p["bd"])
    return outs


def _chunk_pairs(x, nsub):
    c = WKV_CHUNK
    w = 2 * HEAD
    return jnp.stack([x[c * i:c * (i + 1), w * q:w * (q + 1)]
                      for i in range(nsub) for q in range(N_HEADS // 2)], axis=0)


def _masked_rowsum(mask_bf, x):
    h1, h2 = _split2(x)
    return _dot(mask_bf, h1) + _dot(mask_bf, h2)


def _masked_rowsum_t(x, mask):
    h1, h2 = _split2(x)
    m = mask.astype(BF16)
    return _dot(h1, m) + _dot(h2, m)


def _wkv_seq_body(nsub, r_ref, lw_ref, kp_ref, v_ref, kap_ref, bb_ref, s0_ref,
                  o_ref, sout_ref, s_scr):
    c = WKV_CHUNK
    tc = nsub * c
    npair = N_HEADS // 2

    @pl.when(pl.program_id(1) == 0)
    def _():
        for q in range(npair):
            s_scr[q] = jnp.concatenate([s0_ref[0, 2 * q], s0_ref[0, 2 * q + 1]], axis=-1)

    row_t = lax.broadcasted_iota(jnp.int32, (tc, tc), 0)
    col_t = lax.broadcasted_iota(jnp.int32, (tc, tc), 1)
    shift = c.bit_length() - 1
    same_chunk = (row_t >> shift) == (col_t >> shift)
    lw = lw_ref[0]
    g = _masked_rowsum((same_chunk & (row_t >= col_t)).astype(BF16), lw)
    g_end = _masked_rowsum(same_chunk.astype(BF16), lw)
    e_neg = jnp.exp(-g)
    e_end = jnp.exp(g_end - g)
    cp = functools.partial(_chunk_pairs, nsub=nsub)
    k = kp_ref[0]
    b = bb_ref[0]
    kap_t = cp(kap_ref[0] * jnp.exp(g - lw))
    b_t = cp(b * e_neg)
    k_t = cp(k * e_neg)
    r_t = cp(r_ref[0] * jnp.exp(g))
    b_e = cp(b * e_end)
    k_e = cp(k * e_end)
    vv = cp(v_ref[0])
    decay_end = cp(jnp.exp(g_end))

    row = lax.broadcasted_iota(jnp.int32, (c, 2 * c), 0)[None]
    lane = lax.broadcasted_iota(jnp.int32, (c, 2 * c), 1)[None]
    colp = lane & (c - 1)
    right = lane >= c
    row2 = lax.broadcasted_iota(jnp.int32, (2 * c, 2 * c), 0)[None]
    lane2 = lax.broadcasted_iota(jnp.int32, (2 * c, 2 * c), 1)[None]
    same_head = (row2 >= c) == (lane2 >= c)

    def bd(x):
        return jnp.concatenate([jnp.where(right, 0.0, x), jnp.where(right, x, 0.0)], axis=1)

    p_b = _mm1(jnp.concatenate([kap_t, r_t], axis=1), bd(b_t), _B_NT)
    l_b = jnp.where(row > colp, p_b[:, :c], 0.0)
    a_rb = jnp.where(row >= colp, p_b[:, c:], 0.0)
    bd_k = bd(k_t)
    l_k = jnp.where(row > colp, _mm1(kap_t, bd_k, _B_NT), 0.0)
    a_rk = jnp.where(row >= colp, _mm1(r_t, bd_k, _B_NT), 0.0)
    m = -l_b
    t_inv = jnp.where(row == colp, 1.0, 0.0) + m
    m = _mm1(m, bd(m), _B_NN)
    span = 2
    while 2 * span < c:
        both = _mm1(jnp.concatenate([m, t_inv], axis=1), bd(m), _B_NN)
        m = both[:, :c]
        t_inv = t_inv + both[:, c:]
        span *= 2
    t_inv = t_inv + _mm1(t_inv, bd(m), _B_NN)
    bd_v = bd(vv)
    a1 = -_mm1(t_inv, bd(kap_t), _B_NN)
    u0 = -_mm1(t_inv, bd(_mm1(l_k, bd_v, _B_NN)), _B_NN)
    a2 = r_t + _mm1(a_rb, bd(a1), _B_NN)
    o0 = _mm1(a_rb, bd(u0), _B_NN) + _mm1(a_rk, bd_v, _B_NN)
    g_bd = (jnp.where(same_head, _mm1(a1, b_e, _B_TN), 0.0)
            + jnp.where(row2 == lane2, decay_end[:, 0:1, :], 0.0))
    hh = _mm1(jnp.concatenate([u0, vv], axis=1),
              jnp.concatenate([b_e, k_e], axis=1), _B_TN)
    h_pair = jnp.where(right, hh[:, c:], hh[:, :c])

    s = s_scr[...]
    for i in range(nsub):
        ps = slice(npair * i, npair * (i + 1))
        o = _mm1(a2[ps], bd(s), _B_NT) + o0[ps]
        s = _mm1(s, g_bd[ps], _B_NN) + h_pair[ps]
        for q in range(npair):
            o_ref[0, c * i:c * (i + 1), 2 * HEAD * q:2 * HEAD * (q + 1)] = o[q]
    s_scr[...] = s

    @pl.when(pl.program_id(1) == pl.num_programs(1) - 1)
    def _():
        for q in range(npair):
            sout_ref[0, 2 * q] = s[q][:, :HEAD]
            sout_ref[0, 2 * q + 1] = s[q][:, HEAD:]


def _wkv_seq(r, lw, kp, v, kap, bb, s0):
    b, t, _ = r.shape
    tc = WKV_TILE
    nsub = tc // WKV_CHUNK
    tok = pl.BlockSpec((1, tc, D_A), lambda i, j: (i, j, 0))
    st = pl.BlockSpec((1, N_HEADS, HEAD, HEAD), lambda i, j: (i, 0, 0, 0))
    return pl.pallas_call(
        functools.partial(_wkv_seq_body, nsub),
        grid=(b, t // tc),
        in_specs=[tok] * 6 + [st],
        out_specs=[tok, st],
        out_shape=[jax.ShapeDtypeStruct((b, t, D_A), F32),
                   jax.ShapeDtypeStruct((b, N_HEADS, HEAD, HEAD), F32)],
        scratch_shapes=[pltpu.VMEM((N_HEADS // 2, HEAD, 2 * HEAD), F32)],
        compiler_params=_params(("arbitrary", "arbitrary"), 32 * MIB),
        name="wkv_seq",
    )(r, lw, kp, v, kap, bb, s0)


def _wkv_step_body(r_ref, lw_ref, kp_ref, v_ref, kap_ref, bb_ref, s0_ref, _, o_ref, sout_ref):
    for h in range(N_HEADS):
        hs = slice(HEAD * h, HEAD * (h + 1))
        row = lambda ref: ref[:, hs][:, None, :]
        s = s0_ref[:, h]
        sa = -jnp.sum(s * row(kap_ref), axis=-1, keepdims=True)
        v_col = v_ref[:, hs][:, :, None]
        s_new = s * jnp.exp(row(lw_ref)) + sa * row(bb_ref) + v_col * row(kp_ref)
        sout_ref[:, h] = s_new
        o_ref[:, hs] = jnp.sum(s_new * row(r_ref), axis=-1)


def _wkv_step(r, lw, kp, v, kap, bb, s0, s_new_all, layer):
    m = r.shape[0]
    nb = WKV_STEP_ROWS
    tok = pl.BlockSpec((nb, D_A), lambda i: (i, 0))
    st = pl.BlockSpec((nb, N_HEADS, HEAD, HEAD), lambda i: (i, 0, 0, 0))
    st_out = pl.BlockSpec((None, nb, N_HEADS, HEAD, HEAD), lambda i: (layer, i, 0, 0, 0))
    return pl.pallas_call(
        _wkv_step_body,
        grid=(m // nb,),
        in_specs=[tok] * 6 + [st, pl.BlockSpec(memory_space=pl.ANY)],
        out_specs=[tok, st_out],
        out_shape=[jax.ShapeDtypeStruct((m, D_A), F32),
                   jax.ShapeDtypeStruct(s_new_all.shape, F32)],
        input_output_aliases={7: 1},
        compiler_params=_params(("arbitrary",), 56 * MIB),
        name="wkv_step",
    )(r, lw, kp, v, kap, bb, s0, s_new_all)


def _merge_body(seq_mode, tm, x_ref, o_ref, r_ref, kp_ref, v_ref, u_ref, vg_ref,
                ga_ref, gb_ref, g1_ref, sc2_ref, sh2_ref, lxg_ref, lxb_ref, rk_ref,
                bd_ref, ws_ref, bs_ref, pa_ref, pb_ref, wo_ref, l1g_ref, l1b_ref,
                x1_ref, h2_ref):
    bd = bd_ref[...]
    o = o_ref[0]
    inv_n = 1.0 / HEAD
    mu = _segsum(o, bd, two_terms=True) * inv_n
    d = o - mu
    var = _segsum(d * d, bd) * inv_n
    on = d * lax.rsqrt(var + GN_EPS) * lxg_ref[...] + lxb_ref[...]
    v = v_ref[0]
    o_a = on + _segsum(r_ref[0] * kp_ref[0] * rk_ref[...], bd) * v
    vg = vg_ref[0]
    if seq_mode:
        row = lax.broadcasted_iota(jnp.int32, (CHUNK, CHUNK), 0)
        col = lax.broadcasted_iota(jnp.int32, (CHUNK, CHUNK), 1)
        lane = lax.broadcasted_iota(jnp.int32, (CHUNK, D_B), 1)
        gsz = D_B // N_GROUPS_B
        w_cat = jnp.concatenate(
            [jnp.where(row >= col, ws_ref[g], 0.0).astype(BF16) for g in range(N_GROUPS_B)], axis=1)
        pieces = []
        for j in range(tm // CHUNK):
            vc = vg[j * CHUNK:(j + 1) * CHUNK, :].astype(BF16)
            v_bd = jnp.concatenate(
                [jnp.where((lane >= g * gsz) & (lane < (g + 1) * gsz), vc, 0.0)
                 for g in range(N_GROUPS_B)], axis=0)
            pieces.append(_dot(w_cat, v_bd) + bs_ref[...])
        s = jnp.concatenate(pieces, axis=0) if len(pieces) > 1 else pieces[0]
    else:
        s = vg * ws_ref[...] + bs_ref[...]
    o_b = u_ref[0] * s
    y = _mm1(ga_ref[0] * _mm1(o_a, pa_ref[...]) + gb_ref[0] * _mm1(o_b, pb_ref[...]),
             wo_ref[...])
    x1 = _layer_norm(ALPHA * x_ref[0] + g1_ref[0] * y, l1g_ref[...], l1b_ref[...])
    x1_ref[0] = x1
    h2_ref[0] = x1 * (1.0 + sc2_ref[0]) + sh2_ref[0]


def _merge_stage(seq_mode, x, o, r, kp, v, u, vg, ga, gb, g1, sc2, sh2, p):
    b, t, d = x.shape
    tm = MERGE_TILE if seq_mode else t
    tmod = 1 if seq_mode else tm
    tok = lambda n: pl.BlockSpec((1, tm, n), lambda i, j: (i, j, 0))
    mod = pl.BlockSpec((1, tmod, d), lambda i, j: (i, j if not seq_mode else 0, 0))
    ws, bs = (p["w_spatial"], p["b_spatial_full"]) if seq_mode else (p["ws_row"], p["bs_row"])
    est = 2 * tm * (2 * d + 7 * D_A + 2 * d + 2 * d) * 4 + 8 * tm * d * 4 + 16 * MIB
    return pl.pallas_call(
        functools.partial(_merge_body, seq_mode, tm),
        grid=(b, t // tm),
        in_specs=[tok(d)] + [tok(D_A)] * 6 + [tok(d), tok(d), mod, mod, mod,
                  _const_spec((1, D_A)), _const_spec((1, D_A)), _const_spec((1, D_A)),
                  _const_spec((D_A, D_A)), _const_spec(ws.shape), _const_spec(bs.shape),
                  _const_spec((D_A, d)), _const_spec((D_B, d)), _const_spec((d, d)),
                  _const_spec((1, d)), _const_spec((1, d))],
        out_specs=[tok(d), tok(d)],
        out_shape=[jax.ShapeDtypeStruct((b, t, d), F32)] * 2,
        compiler_params=_params(("arbitrary", "arbitrary"), est),
        name="merge_seq" if seq_mode else "merge_row",
    )(x, o, r, kp, v, u, vg, ga, gb, g1, sc2, sh2, p["lnx_g"], p["lnx_b"], p["r_k"],
      p["bd"], ws, bs, p["w_branch_a"], p["w_branch_b"], p["w_out"], p["ln1_g"], p["ln1_b"])


def _route_body(h_ref, w_ref, b_ref, gate_ref, idx_ref, wts_ref, cnt_ref):
    lg = _mm3(h_ref[...], w_ref[...]) + b_ref[...]
    lane = lax.broadcasted_iota(jnp.int32, lg.shape, 1)
    lanef = lane.astype(F32)
    neg = -jnp.inf
    is_g = (lane >= N_EXPERTS) & (lane < N_EXPERTS + N_ROUTE_GROUPS)
    mg = jnp.max(jnp.where(is_g, lg, neg), axis=-1, keepdims=True)
    gidx = jnp.min(jnp.where(is_g & (lg == mg), lanef - N_EXPERTS, 1e9), axis=-1, keepdims=True)
    pg_sel = 1.0 / jnp.sum(jnp.where(is_g, jnp.exp(lg - mg), 0.0), axis=-1, keepdims=True)
    lo = gidx * EXP_PER_GROUP
    in_grp = (lanef >= lo) & (lanef < lo + EXP_PER_GROUP)
    t1 = jnp.max(jnp.where(in_grp, lg, neg), axis=-1, keepdims=True)
    i1 = jnp.min(jnp.where(in_grp & (lg == t1), lanef, 1e9), axis=-1, keepdims=True)
    rest = in_grp & (lanef != i1)
    t2 = jnp.max(jnp.where(rest, lg, neg), axis=-1, keepdims=True)
    i2 = jnp.min(jnp.where(rest & (lg == t2), lanef, 1e9), axis=-1, keepdims=True)
    e2 = jnp.exp(t2 - t1)
    w1 = pg_sel / (1.0 + e2)
    w2 = pg_sel * e2 / (1.0 + e2)
    gate_ref[...] = jnp.where(lanef == i1, w1, 0.0) + jnp.where(lanef == i2, w2, 0.0)
    tm = lg.shape[0]
    hit = (lanef == i1) | (lanef == i2)
    earlier = (lax.broadcasted_iota(jnp.int32, (tm, tm), 0)
               > lax.broadcasted_iota(jnp.int32, (tm, tm), 1))
    rank = _dot(earlier.astype(BF16), hit.astype(BF16))
    cnt = jnp.sum(hit.astype(F32), axis=0, keepdims=True)
    run_len = jnp.floor((cnt + (MOE_RUN_ALIGN - 1.0)) * (1.0 / MOE_RUN_ALIGN)) * MOE_RUN_ALIGN
    lower_expert = (lax.broadcasted_iota(jnp.int32, (LANES, LANES), 0)
                    < lax.broadcasted_iota(jnp.int32, (LANES, LANES), 1))
    run_start = _masked_rowsum_t(jnp.broadcast_to(run_len, (SUBLANES, LANES)), lower_expert)[0:1]
    lpos = run_start + rank
    lp1 = jnp.sum(jnp.where(lanef == i1, lpos, 0.0), axis=-1, keepdims=True)
    lp2 = jnp.sum(jnp.where(lanef == i2, lpos, 0.0), axis=-1, keepdims=True)
    cnt_ref[0] = cnt.astype(jnp.int32)
    idx = jnp.where(lane == 0, i1, jnp.where(lane == 1, i2, jnp.where(lane == 2, lp1, lp2)))
    idx_ref[...] = idx.astype(jnp.int32)
    wts_ref[...] = jnp.where(lane == 0, w1, w2)


def _route(h2, p):
    m, d = h2.shape
    tm = min(m, MOE_TILE)
    tok = pl.BlockSpec((tm, LANES), lambda i: (i, 0))
    return pl.pallas_call(
        _route_body,
        grid=(m // tm,),
        in_specs=[pl.BlockSpec((tm, d), lambda i: (i, 0)),
                  _const_spec((d, LANES)), _const_spec((1, LANES))],
        out_specs=[tok, tok, tok, pl.BlockSpec((1, 1, LANES), lambda i: (i, 0, 0))],
        out_shape=[jax.ShapeDtypeStruct((m, LANES), F32),
                   jax.ShapeDtypeStruct((m, LANES), jnp.int32),
                   jax.ShapeDtypeStruct((m, LANES), F32),
                   jax.ShapeDtypeStruct((m // tm, 1, LANES), jnp.int32)],
        compiler_params=_params(("arbitrary",), 24 * MIB),
        name="route",
    )(h2, p["w_route"], p["b_route"])


def _moe_body(h_ref, gate_ref, wg_ref, wu_ref, wd_ref, x1_ref, g2_ref, l2g_ref, l2b_ref,
              out_ref, acc_ref, xb_ref):
    e = pl.program_id(1)

    @pl.when(e == 0)
    def _():
        acc_ref[...] = jnp.zeros_like(acc_ref)
        xb_ref[...] = h_ref[...].astype(BF16)

    xb = xb_ref[...]
    pre = _dot(xb, wg_ref[0].astype(BF16))
    hid = pre * jax.nn.sigmoid(pre) * _dot(xb, wu_ref[0].astype(BF16))
    ye = _mm1(hid, wd_ref[0])
    gate = gate_ref[...]
    lane = lax.broadcasted_iota(jnp.int32, gate.shape, 1)
    ge = jnp.sum(jnp.where(lane == e, gate, 0.0), axis=-1, keepdims=True)
    acc_ref[...] += ge * ye

    @pl.when(e == pl.num_programs(1) - 1)
    def _():
        out_ref[...] = _layer_norm(ALPHA * x1_ref[...] + g2_ref[0] * acc_ref[...],
                                   l2g_ref[...], l2b_ref[...])


def _moe(seq_len, h2, gate, x1, g2, p):
    m, d = h2.shape
    layer = p["layer"]
    tm = min(seq_len if g2.shape[1] == 1 else m, DENSE_MOE_TILE)
    tok = lambda n: pl.BlockSpec((tm, n), lambda i, e: (i, 0))
    if g2.shape[1] == 1:
        tiles_per_seq = seq_len // tm
        g2_spec = pl.BlockSpec((1, 1, d), lambda i, e: (i // tiles_per_seq, 0, 0))
    else:
        g2_spec = pl.BlockSpec((1, tm, d), lambda i, e: (0, i, 0))
    est = 2 * tm * (3 * d + LANES) * 4 + tm * d * 6 + 4 * tm * d * 4 + 8 * MIB
    return pl.pallas_call(
        _moe_body,
        grid=(m // tm, N_EXPERTS),
        in_specs=[tok(d), tok(LANES),
                  pl.BlockSpec((None, 1, d, D_EXPERT), lambda i, e: (layer, e, 0, 0)),
                  pl.BlockSpec((None, 1, d, D_EXPERT), lambda i, e: (layer, e, 0, 0)),
                  pl.BlockSpec((None, 1, D_EXPERT, d), lambda i, e: (layer, e, 0, 0)),
                  tok(d), g2_spec, _const_spec((1, d)), _const_spec((1, d))],
        out_specs=tok(d),
        out_shape=jax.ShapeDtypeStruct((m, d), F32),
        scratch_shapes=[pltpu.VMEM((tm, d), F32), pltpu.VMEM((tm, d), BF16)],
        compiler_params=_params(("arbitrary", "arbitrary"), est),
        name="moe_dense",
    )(h2, gate, p["w_exp_gate"], p["w_exp_up"], p["w_exp_down"], x1, g2, p["ln2_g"], p["ln2_b"])


MOE_TILE = 512
MOE_BLOCK = 512
MOE_RUN_ALIGN = 16
MXU_DEPTH = 256
MOE_LOCAL_ROWS = -(-(2 * MOE_TILE + N_EXPERTS * (MOE_RUN_ALIGN - 1)) // MXU_DEPTH) * MXU_DEPTH
MOE_SLABS = tuple(MOE_TILE >> k for k in range((MOE_TILE // MOE_RUN_ALIGN).bit_length()))


def _for_each_slab(run_ref, make_copy, fn):
    def one_run(e, c):
        dst = run_ref[0, 0, e]
        src = run_ref[0, 0, N_EXPERTS + e]
        n = run_ref[0, 0, 2 * N_EXPERTS + e]
        for slab in MOE_SLABS:
            off = n & (-2 * slab)

            @pl.when((n & slab) != 0)
            def _():
                fn(make_copy(pl.multiple_of(dst + off, MOE_RUN_ALIGN),
                             pl.multiple_of(src + off, MOE_RUN_ALIGN), slab))
        return c

    lax.fori_loop(0, N_EXPERTS, one_run, 0)


def _dispatch_body(tail_ref, run_ref, prev_run_ref, h_ref, idx_ref, xs_hbm, zero_buf, loc_buf,
                   sem, zsem, usem):
    nb_max = xs_hbm.shape[0] // MOE_BLOCK
    nt = MOE_TILE

    def zero_copy(row0, zero_sem):
        return pltpu.make_async_copy(
            zero_buf, xs_hbm.at[pl.ds(pl.multiple_of(row0, MOE_BLOCK), MOE_BLOCK)], zero_sem)

    def each_unused_copy(fn):
        def unused(j, c):
            fn(zero_copy(j * MOE_BLOCK, usem))
            return c

        lax.fori_loop(tail_ref[N_EXPERTS], nb_max, unused, 0)

    @pl.when(pl.program_id(0) == 0)
    def _():
        zero_buf[...] = jnp.zeros_like(zero_buf)

        def each_tail_copy(fn):
            for e in range(N_EXPERTS):
                @pl.when(tail_ref[e] >= 0)
                def _():
                    fn(zero_copy(tail_ref[e], zsem))

        each_tail_copy(lambda cp: cp.start())
        each_unused_copy(lambda cp: cp.start())
        each_tail_copy(lambda cp: cp.wait())

    lp = idx_ref[...].astype(F32)
    eye = (lax.broadcasted_iota(jnp.int32, (nt, nt), 0)
           == lax.broadcasted_iota(jnp.int32, (nt, nt), 1))
    as_row = lambda col: jnp.sum(jnp.where(eye, col, 0.0), axis=0, keepdims=True)
    r = lax.broadcasted_iota(jnp.int32, (MOE_LOCAL_ROWS, nt), 0).astype(F32)
    pick = (r == as_row(lp[:, 2:3])) | (r == as_row(lp[:, 3:4]))
    step = pl.program_id(0)
    slot = step & 1
    loc_buf[slot] = _dot(pick.astype(BF16), h_ref[...].astype(BF16)).astype(BF16)

    def copies_from(which):
        def make_copy(dst, src, rows):
            return pltpu.make_async_copy(loc_buf.at[which, pl.ds(src, rows)],
                                         xs_hbm.at[pl.ds(dst, rows)], sem.at[which])
        return make_copy

    _for_each_slab(run_ref, copies_from(slot), lambda cp: cp.start())

    @pl.when(step > 0)
    def _():
        _for_each_slab(prev_run_ref, copies_from(1 - slot), lambda cp: cp.wait())

    @pl.when(step == pl.num_programs(0) - 1)
    def _():
        _for_each_slab(run_ref, copies_from(slot), lambda cp: cp.wait())
        each_unused_copy(lambda cp: cp.wait())


def _dispatch(h2, idx, runs, tail_start, n_rows):
    m, d = h2.shape
    nt = MOE_TILE
    run_spec = lambda at: pl.BlockSpec((1, 1, LANES), lambda i, tail: (at(i), 0, 0),
                                       memory_space=pltpu.SMEM)
    grid_spec = pltpu.PrefetchScalarGridSpec(
        num_scalar_prefetch=1,
        grid=(m // nt,),
        in_specs=[run_spec(lambda i: i), run_spec(lambda i: jnp.maximum(i - 1, 0)),
                  pl.BlockSpec((nt, d), lambda i, tail: (i, 0)),
                  pl.BlockSpec((nt, LANES), lambda i, tail: (i, 0))],
        out_specs=pl.BlockSpec(memory_space=pl.ANY),
        scratch_shapes=[pltpu.VMEM((MOE_BLOCK, d), BF16),
                        pltpu.VMEM((2, MOE_LOCAL_ROWS, d), BF16),
                        pltpu.SemaphoreType.DMA((2,)), pltpu.SemaphoreType.DMA,
                        pltpu.SemaphoreType.DMA],
    )
    return pl.pallas_call(
        _dispatch_body,
        grid_spec=grid_spec,
        out_shape=jax.ShapeDtypeStruct((n_rows, d), BF16),
        compiler_params=_params(("arbitrary",), 40 * MIB),
        name="moe_dispatch",
    )(tail_start, runs, runs, h2, idx)


GMM_RING = 3


def _gmm_body(layer, be_ref, nb_ref, first_ref, next_ref, wslot_ref,
              x_hbm, wg_hbm, wu_hbm, wd_hbm, y_ref,
              x_ring, wg_stage, wu_stage, wd_stage, wg_bf, wu_bf, wd_bf, sem, wsem):
    j = pl.program_id(0)
    n_live = nb_ref[0]

    def fetch(step):
        slot = step % GMM_RING
        return pltpu.make_async_copy(
            x_hbm.at[pl.ds(pl.multiple_of(step * MOE_BLOCK, MOE_BLOCK), MOE_BLOCK)],
            x_ring.at[slot], sem.at[slot])

    def weight_copies(expert, slot):
        return [pltpu.make_async_copy(hbm.at[layer, expert], stage.at[slot], wsem.at[slot, i])
                for i, (hbm, stage) in enumerate(((wg_hbm, wg_stage), (wu_hbm, wu_stage),
                                                  (wd_hbm, wd_stage)))]

    @pl.when(j == 0)
    def _():
        for cp in weight_copies(be_ref[0], 0):
            cp.start()
        for ahead in range(GMM_RING - 1):
            @pl.when(ahead < n_live)
            def _():
                fetch(ahead).start()

    @pl.when(j + (GMM_RING - 1) < n_live)
    def _():
        fetch(j + (GMM_RING - 1)).start()

    @pl.when(j < n_live)
    def _():
        @pl.when(first_ref[j] == 1)
        def _():
            slot = wslot_ref[j]
            for cp in weight_copies(be_ref[j], slot):
                cp.wait()
            wg_bf[...] = wg_stage[slot].astype(BF16)
            wu_bf[...] = wu_stage[slot].astype(BF16)
            wd_bf[...] = wd_stage[slot].astype(BF16)

            @pl.when(next_ref[j] >= 0)
            def _():
                for cp in weight_copies(next_ref[j], 1 - slot):
                    cp.start()

        fetch(j).wait()
        xb = x_ring[j % GMM_RING]
        pre = _dot(xb, wg_bf[...])
        hid = pre * jax.nn.sigmoid(pre) * _dot(xb, wu_bf[...])
        y_ref[...] = _dot(hid.astype(BF16), wd_bf[...]).astype(BF16)

    @pl.when(j >= nb_ref[0])
    def _():
        y_ref[...] = jnp.zeros_like(y_ref)


def _gmm(xs, blk_expert, n_blocks, run_first, run_next, run_slot, p):
    n_rows, d = xs.shape
    nb_max = n_rows // MOE_BLOCK
    any_spec = pl.BlockSpec(memory_space=pl.ANY)
    grid_spec = pltpu.PrefetchScalarGridSpec(
        num_scalar_prefetch=5,
        grid=(nb_max,),
        in_specs=[any_spec] * 4,
        out_specs=pl.BlockSpec((MOE_BLOCK, d), lambda j, *_: (j, 0)),
        scratch_shapes=[pltpu.VMEM((GMM_RING, MOE_BLOCK, d), BF16),
                        pltpu.VMEM((2, d, D_EXPERT), F32), pltpu.VMEM((2, d, D_EXPERT), F32),
                        pltpu.VMEM((2, D_EXPERT, d), F32),
                        pltpu.VMEM((d, D_EXPERT), BF16), pltpu.VMEM((d, D_EXPERT), BF16),
                        pltpu.VMEM((D_EXPERT, d), BF16),
                        pltpu.SemaphoreType.DMA((GMM_RING,)), pltpu.SemaphoreType.DMA((2, 3))],
    )
    return pl.pallas_call(
        functools.partial(_gmm_body, p["layer"]),
        grid_spec=grid_spec,
        out_shape=jax.ShapeDtypeStruct((n_rows, d), BF16),
        compiler_params=_params(("arbitrary",), 24 * MIB),
        name="moe_gmm",
    )(blk_expert, n_blocks, run_first, run_next, run_slot,
      xs, p["w_exp_gate"], p["w_exp_up"], p["w_exp_down"])


def _combine_body(run_ref, next_run_ref, ys_hbm, idx_ref, wts_ref, x1_ref, g2_ref, l2g_ref,
                  l2b_ref, out_ref, loc_buf, sem):
    nt = MOE_TILE
    step = pl.program_id(0)
    slot = step & 1

    def copies_into(which):
        def make_copy(dst, src, rows):
            return pltpu.make_async_copy(ys_hbm.at[pl.ds(dst, rows)],
                                         loc_buf.at[which, pl.ds(src, rows)], sem.at[which])
        return make_copy

    @pl.when(step == 0)
    def _():
        loc_buf[...] = jnp.zeros_like(loc_buf)
        _for_each_slab(run_ref, copies_into(slot), lambda cp: cp.start())

    @pl.when(step < pl.num_programs(0) - 1)
    def _():
        _for_each_slab(next_run_ref, copies_into(1 - slot), lambda cp: cp.start())

    _for_each_slab(run_ref, copies_into(slot), lambda cp: cp.wait())
    lp = idx_ref[...].astype(F32)
    w = wts_ref[...]
    c = lax.broadcasted_iota(jnp.int32, (nt, MOE_LOCAL_ROWS), 1).astype(F32)
    sel = (jnp.where(c == lp[:, 2:3], w[:, 0:1], 0.0)
           + jnp.where(c == lp[:, 3:4], w[:, 1:2], 0.0))
    sel_hi, sel_lo = _split2(sel)
    rows_bf = loc_buf[slot]
    moe = _dot(sel_hi, rows_bf) + _dot(sel_lo, rows_bf)
    out_ref[...] = _layer_norm(ALPHA * x1_ref[...] + g2_ref[0] * moe, l2g_ref[...], l2b_ref[...])


def _combine(seq_len, ys, idx, wts, runs, x1, g2, p):
    m, d = x1.shape
    nt = MOE_TILE
    tiles_per_seq = seq_len // nt
    tok = lambda n: pl.BlockSpec((nt, n), lambda i: (i, 0))
    n_tiles = m // nt
    run_spec = lambda at: pl.BlockSpec((1, 1, LANES), lambda i: (at(i), 0, 0),
                                       memory_space=pltpu.SMEM)
    return pl.pallas_call(
        _combine_body,
        grid=(n_tiles,),
        in_specs=[run_spec(lambda i: i), run_spec(lambda i: jnp.minimum(i + 1, n_tiles - 1)),
                  pl.BlockSpec(memory_space=pl.ANY), tok(LANES), tok(LANES), tok(d),
                  pl.BlockSpec((1, 1, d), lambda i: (i // tiles_per_seq, 0, 0)),
                  _const_spec((1, d)), _const_spec((1, d))],
        out_specs=tok(d),
        out_shape=jax.ShapeDtypeStruct((m, d), F32),
        scratch_shapes=[pltpu.VMEM((2, MOE_LOCAL_ROWS, d), BF16), pltpu.SemaphoreType.DMA((2,))],
        compiler_params=_params(("arbitrary",), 48 * MIB),
        name="moe_combine",
    )(runs, runs, ys, idx, wts, x1, g2, p["ln2_g"], p["ln2_b"])


def _moe_routed(seq_len, h2, idx, wts, tile_cnt, x1, g2, p):
    m, d = h2.shape
    blk = MOE_BLOCK
    n_tiles = m // MOE_TILE
    worst_rows = 2 * m + n_tiles * N_EXPERTS * (MOE_RUN_ALIGN - 1) + N_EXPERTS * (blk - 1)
    nb_max = -(-worst_rows // blk)
    cnt = tile_cnt[:, 0, :N_EXPERTS]
    run_len = ((cnt + MOE_RUN_ALIGN - 1) // MOE_RUN_ALIGN) * MOE_RUN_ALIGN
    local_row = jnp.cumsum(run_len, axis=1) - run_len
    rows_before = jnp.cumsum(run_len, axis=0) - run_len
    total = jnp.sum(run_len, axis=0)
    padded = ((total + blk - 1) // blk) * blk
    ends = jnp.cumsum(padded)
    starts = ends - padded
    n_blocks = (ends[-1] // blk).astype(jnp.int32).reshape(1)
    first_row = jnp.arange(nb_max, dtype=jnp.int32) * blk
    blk_expert = jnp.minimum(
        jnp.sum((first_row[:, None] >= ends[None, :]).astype(jnp.int32), axis=1),
        N_EXPERTS - 1).astype(jnp.int32)
    tail_start = jnp.concatenate(
        [jnp.where(padded > 0, ends - blk, -1).astype(jnp.int32), n_blocks])
    runs = jnp.concatenate(
        [starts[None, :] + rows_before, local_row, run_len,
         jnp.zeros((n_tiles, LANES - 3 * N_EXPERTS), jnp.int32)], axis=1).astype(jnp.int32)
    runs = runs.reshape(n_tiles, 1, LANES)
    prev_expert = jnp.concatenate([jnp.full((1,), -1, jnp.int32), blk_expert[:-1]])
    run_first = (blk_expert != prev_expert).astype(jnp.int32)
    run_slot = ((jnp.cumsum(run_first) - 1) % 2).astype(jnp.int32)
    used = padded > 0
    later = jnp.arange(N_EXPERTS)[None, :] > jnp.arange(N_EXPERTS)[:, None]
    next_used = jnp.min(jnp.where(later & used[None, :], jnp.arange(N_EXPERTS)[None, :], N_EXPERTS),
                        axis=1)
    run_next = jnp.where(next_used < N_EXPERTS, next_used, -1).astype(jnp.int32)[blk_expert]
    xs = _dispatch(h2, idx, runs, tail_start, nb_max * blk)
    ys = _gmm(xs, blk_expert, n_blocks, run_first, run_next, run_slot, p)
    return _combine(seq_len, ys, idx, wts, runs, x1, g2, p)


def _prep_layer(l, w_in, mu_shift, w0, w_decay_up, a0, w_iclr_up, k_k, k_a, r_k,
                lnx_g, lnx_b, lnv_g, lnv_b, w_spatial, b_spatial, w_branch_a, w_branch_b,
                w_out, ln1_g, ln1_b, w_route_group, b_route_group, w_route_expert,
                b_route_expert, w_exp_gate, w_exp_up, w_exp_down, ln2_g, ln2_b):
    d = D_MODEL
    pad_a = N_SHIFT_PAD - N_SHIFT
    wi = w_in[l]
    w_in_p = jnp.concatenate(
        [wi[:, :N_SHIFT], jnp.zeros((d, pad_a), F32), wi[:, N_SHIFT:]], axis=1).astype(BF16)
    mu = jnp.concatenate([mu_shift[l], jnp.zeros((pad_a,), F32)])[None]
    lora = jnp.zeros((LANES, 2 * D_A), F32)
    lora = lora.at[:R_LORA, :D_A].set(w_decay_up[l]).at[R_LORA:2 * R_LORA, D_A:].set(w_iclr_up[l])
    seg = jnp.arange(D_A) // HEAD
    row1 = lambda x: x.reshape(1, -1)
    gsz = D_B // N_GROUPS_B
    w_route = jnp.concatenate(
        [w_route_expert[l], w_route_group[l],
         jnp.zeros((d, LANES - N_EXPERTS - N_ROUTE_GROUPS), F32)], axis=1)
    b_route = jnp.concatenate(
        [b_route_expert[l], b_route_group[l],
         jnp.zeros((LANES - N_EXPERTS - N_ROUTE_GROUPS,), F32)])[None]
    return dict(
        w_in=w_in_p, mu=mu, lora=lora,
        w0a0=jnp.concatenate([w0[l], a0[l]])[None],
        k_k=row1(k_k[l]), k_a=row1(k_a[l]), r_k=row1(r_k[l]),
        lnx_g=row1(lnx_g[l]), lnx_b=row1(lnx_b[l]),
        lnv_g=row1(lnv_g[l]), lnv_b=row1(lnv_b[l]),
        bd=(seg[:, None] == seg[None, :]).astype(BF16),
        w_spatial=w_spatial[l],
        b_spatial_full=jnp.repeat(b_spatial[l].T, gsz, axis=1),
        ws_row=jnp.repeat(w_spatial[l][:, 0, 0], gsz)[None],
        bs_row=jnp.repeat(b_spatial[l][:, 0], gsz)[None],
        w_branch_a=w_branch_a[l].astype(BF16), w_branch_b=w_branch_b[l].astype(BF16),
        w_out=w_out[l].astype(BF16), ln1_g=row1(ln1_g[l]), ln1_b=row1(ln1_b[l]),
        w_route=w_route, b_route=b_route,
        w_exp_gate=w_exp_gate, w_exp_up=w_exp_up, w_exp_down=w_exp_down, layer=l,
        ln2_g=row1(ln2_g[l]), ln2_b=row1(ln2_b[l]),
    )


def _trunk(seq_mode, x, mods, wkv_in, shift_in, preps):
    b, t, d = x.shape
    wkv_out, shift_out, v_out = [], [], []
    wkv_acc = None if seq_mode else jnp.zeros(wkv_in.shape, F32)
    for l in range(DEPTH):
        p = preps[l]
        sh1, sc1, g1, sh2, sc2, g2 = mods[l]
        if seq_mode:
            zprev = jnp.zeros((b, 1, N_SHIFT_PAD), F32) if shift_in is None else shift_in[l]
        else:
            zprev = _matmul(shift_in[l], p["w_in"][:, :N_SHIFT_PAD]).reshape(b, t, N_SHIFT_PAD)
        r, lw, kp, v, kap, bb, u, vg, ga, gb, hl = _in_stage(seq_mode, x, sc1, sh1, zprev, p)
        if seq_mode:
            o, s_new = _wkv_seq(r, lw, kp, v, kap, bb, wkv_in[l])
        else:
            flat = lambda a: a.reshape(t, D_A)
            o, wkv_acc = _wkv_step(flat(r), flat(lw), flat(kp), flat(v), flat(kap), flat(bb),
                                   wkv_in[l], wkv_acc, l)
            o = o.reshape(b, t, D_A)
        x1, h2 = _merge_stage(seq_mode, x, o, r, kp, v, u, vg, ga, gb, g1, sc2, sh2, p)
        m = b * t
        h2f = h2.reshape(m, d)
        gate, idx, wts, tile_cnt = _route(h2f, p)
        if seq_mode:
            x = _moe_routed(t, h2f, idx, wts, tile_cnt, x1.reshape(m, d), g2, p)
        else:
            x = _moe(t, h2f, gate, x1.reshape(m, d), g2, p)
        x = x.reshape(b, t, d)
        if seq_mode:
            wkv_out.append(s_new)
        shift_out.append(hl)
        v_out.append(vg)
    if not seq_mode:
        wkv_out = wkv_acc
    return x, wkv_out, shift_out, v_out


def kernel(x_prompt, x_sample, c_prompt, c_sample, state_wkv, state_shift, w_ada, b_ada, w_in, mu_shift, w0, w_decay_up, a0, w_iclr_up, k_k, k_a, r_k, lnx_g, lnx_b, lnv_g, lnv_b, w_spatial, b_spatial, w_branch_a, w_branch_b, w_out, ln1_g, ln1_b, w_route_group, b_route_group, w_route_expert, b_route_expert, w_exp_gate, w_exp_up, w_exp_down, ln2_g, ln2_b):
    bp, tp, d = x_prompt.shape
    bs = x_sample.shape[0]
    layer_params = (w_in, mu_shift, w0, w_decay_up, a0, w_iclr_up, k_k, k_a, r_k, lnx_g,
                    lnx_b, lnv_g, lnv_b, w_spatial, b_spatial, w_branch_a, w_branch_b, w_out,
                    ln1_g, ln1_b, w_route_group, b_route_group, w_route_expert,
                    b_route_expert, w_exp_gate, w_exp_up, w_exp_down, ln2_g, ln2_b)
    preps = [_prep_layer(l, *layer_params) for l in range(DEPTH)]
    mod_all = _ada(jnp.concatenate([c_prompt, c_sample], axis=0), w_ada, b_ada)
    mods_p, mods_s = [], []
    for l in range(DEPTH):
        parts = jnp.split(mod_all[l], 6, axis=-1)
        mods_p.append([q[:bp].reshape(bp, 1, d) for q in parts])
        mods_s.append([q[bp:].reshape(1, bs, d) for q in parts])

    wkv0 = jnp.zeros((DEPTH, bp, N_HEADS, HEAD, HEAD), F32)
    y_p, wkv_p, shift_p, _ = _trunk(True, x_prompt, mods_p, wkv0, None, preps)
    y_s, wkv_s, shift_s, v_s = _trunk(False, x_sample.reshape(1, bs, d), mods_s, state_wkv,
                                      state_shift, preps)
    return (y_p,
            y_s.reshape(bs, 1, d),
            jnp.stack(wkv_p),
            jnp.stack([s.reshape(bp, d) for s in shift_p]),
            wkv_s,
            jnp.stack([s.reshape(bs, d) for s in shift_s]),
            jnp.stack([q.reshape(bs, 1, D_B) for q in v_s]))
```

```python
import functools

import jax
import jax.numpy as jnp
from jax import lax
from jax.experimental import pallas as pl
from jax.experimental.pallas import tpu as pltpu

F32 = jnp.float32
BF16 = jnp.bfloat16

D_MODEL = 1024
DEPTH = 2
HEAD = 64
N_HEADS = 8
D_A = N_HEADS * HEAD
R_LORA = 32
CHUNK = 128
N_GROUPS_B = 8
D_B = 512
N_SHIFT = 3 * D_A + 2 * R_LORA
N_ROUTE_GROUPS = 4
EXP_PER_GROUP = 8
N_EXPERTS = N_ROUTE_GROUPS * EXP_PER_GROUP
D_EXPERT = 256
ALPHA = (2 * DEPTH) ** 0.25
LN_EPS = 1e-5
GN_EPS = 64e-5

LANES = 128
SUBLANES = 8
MIB = 2 ** 20
N_SHIFT_PAD = 13 * LANES
COL_U = N_SHIFT_PAD
COL_VG = COL_U + D_B
COL_GA = COL_VG + D_B
COL_GB = COL_GA + D_MODEL
N_IN_PAD = COL_GB + D_MODEL
WKV_CHUNK = 64
WKV_TILE = 256
WKV_STEP_ROWS = 32
MERGE_TILE = 512
DENSE_MOE_TILE = 1024
ADA_COLS = 1024
IN_ROW_GROUPS = 2
IN_GROUP_ROWS = 256
VMEM_CAP_BYTES = 60000 * 1024

_NN = (((1,), (0,)), ((), ()))
_B_NT = (((2,), (2,)), ((0,), (0,)))
_B_NN = (((2,), (1,)), ((0,), (0,)))
_B_TN = (((1,), (1,)), ((0,), (0,)))


def _dot(a, b, dims=_NN):
    return lax.dot_general(a, b, dims, preferred_element_type=F32)


def _split2(x):
    hi = x.astype(BF16)
    lo = (x - hi.astype(F32)).astype(BF16)
    return hi, lo


def _mm1(a, b, dims=_NN):
    return _dot(a.astype(BF16), b.astype(BF16), dims)


def _mm3(a, b, dims=_NN):
    ah, al = _split2(a)
    bh, bl = _split2(b)
    return _dot(ah, bh, dims) + (_dot(ah, bl, dims) + _dot(al, bh, dims))


def _segsum(x, bd, two_terms=False):
    if not two_terms:
        return _dot(x.astype(BF16), bd)
    hi, lo = _split2(x)
    return _dot(hi, bd) + _dot(lo, bd)


def _layer_norm(x, g, b):
    mu = jnp.mean(x, axis=-1, keepdims=True)
    d = x - mu
    var = jnp.mean(d * d, axis=-1, keepdims=True)
    return d * lax.rsqrt(var + LN_EPS) * g + b


def _gelu(x):
    return 0.5 * x * (1.0 + lax.erf(x * 0.7071067811865476))


def _params(sem, est_bytes):
    limit = int(min(VMEM_CAP_BYTES, max(est_bytes, 16 * 1024 * 1024)))
    return pltpu.CompilerParams(dimension_semantics=sem, vmem_limit_bytes=limit)


def _const_spec(shape, single_buffer=False):
    nd = len(shape)
    if single_buffer:
        return pl.BlockSpec(shape, lambda *_: (0,) * nd, pipeline_mode=pl.Buffered(1))
    return pl.BlockSpec(shape, lambda *_: (0,) * nd)


def _ada_body(c_ref, w_ref, b_ref, o_ref):
    c = c_ref[...]
    s = c * jax.nn.sigmoid(c)
    o_ref[0] = _mm3(s, w_ref[0]) + b_ref[0]


def _ada(c_all, w_ada, b_ada):
    depth, d, n6 = w_ada.shape
    m = c_all.shape[0]
    tn = ADA_COLS
    return pl.pallas_call(
        _ada_body,
        grid=(depth, n6 // tn),
        in_specs=[
            pl.BlockSpec((m, d), lambda l, j: (0, 0)),
            pl.BlockSpec((1, d, tn), lambda l, j: (l, 0, j)),
            pl.BlockSpec((1, 1, tn), lambda l, j: (l, 0, j)),
        ],
        out_specs=pl.BlockSpec((1, m, tn), lambda l, j: (l, 0, j)),
        out_shape=jax.ShapeDtypeStruct((depth, m, n6), F32),
        compiler_params=_params(("arbitrary", "arbitrary"), 24 * MIB),
        name="ada_mod",
    )(c_all, w_ada, b_ada.reshape(depth, 1, n6))


def _mm_body(x_ref, w_ref, o_ref):
    o_ref[...] = _mm1(x_ref[...], w_ref[...])


def _matmul(x, w):
    m, k = x.shape
    n = w.shape[1]
    return pl.pallas_call(
        _mm_body,
        grid=(1,),
        in_specs=[_const_spec((m, k)), _const_spec((k, n))],
        out_specs=_const_spec((m, n)),
        out_shape=jax.ShapeDtypeStruct((m, n), F32),
        compiler_params=_params(("arbitrary",), 24 * MIB),
        name="shift_proj",
    )(x, w)


def _in_body(seq_mode, tm, x_ref, sc_ref, sh_ref, w_ref, mu_ref, zp_ref, lora_ref,
             w0a0_ref, kk_ref, ka_ref, lng_ref, lnb_ref, bd_ref,
             r_ref, lw_ref, kp_ref, v_ref, kap_ref, bb_ref, u_ref, vg_ref,
             ga_ref, gb_ref, hl_ref, carry_ref):
    nsplit = IN_ROW_GROUPS if seq_mode else 1
    rows = tm // nsplit
    if seq_mode:
        @pl.when(pl.program_id(1) == 0)
        def _():
            carry_ref[...] = zp_ref[0]

        carry = carry_ref[...]
    bd = bd_ref[...]
    for part in range(nsplit):
        sl = slice(part * rows, (part + 1) * rows)
        h = x_ref[0, sl, :] * (1.0 + sc_ref[0]) + sh_ref[0]
        hb = h.astype(BF16)
        proj = lambda lo, hi: _dot(hb, w_ref[:, lo:hi])
        za = proj(0, N_SHIFT_PAD)
        if seq_mode:
            row = lax.broadcasted_iota(jnp.int32, za.shape, 0)
            prev = jnp.where(row == 0, carry, pltpu.roll(za, 1, 0))
            carry = za[rows - 1:rows, :]
        else:
            prev = zp_ref[0]
        mix = za + mu_ref[...] * (prev - za)
        r = mix[:, 0:D_A]
        k = mix[:, D_A:2 * D_A]
        v = mix[:, 2 * D_A:3 * D_A]
        xwa = mix[:, 3 * D_A:N_SHIFT_PAD]
        lane = lax.broadcasted_iota(jnp.int32, xwa.shape, 1)
        lora_in = jnp.where(lane < R_LORA, jnp.tanh(xwa), xwa)
        pre = w0a0_ref[...] + _mm1(lora_in, lora_ref[...])
        yw = -pre[:, :D_A]
        softplus = jnp.maximum(yw, 0.0) + jnp.log1p(jnp.exp(-jnp.abs(yw)))
        lw = -jnp.exp(-softplus - 0.5)
        a = jax.nn.sigmoid(pre[:, D_A:])
        kk = k * kk_ref[...]
        kap = kk / jnp.maximum(jnp.sqrt(_segsum(kk * kk, bd)), 1e-12)
        r_ref[0, sl, :] = r
        lw_ref[0, sl, :] = lw
        kp_ref[0, sl, :] = k * (1.0 + (a - 1.0) * ka_ref[...])
        v_ref[0, sl, :] = v
        kap_ref[0, sl, :] = kap
        bb_ref[0, sl, :] = kap * a
        u_ref[0, sl, :] = _gelu(proj(COL_U, COL_VG)).astype(BF16)
        vg_ref[0, sl, :] = _layer_norm(_gelu(proj(COL_VG, COL_GA)), lng_ref[...], lnb_ref[...])
        ga_ref[0, sl, :] = jax.nn.sigmoid(proj(COL_GA, COL_GB)).astype(BF16)
        gb_ref[0, sl, :] = jax.nn.sigmoid(proj(COL_GB, N_IN_PAD)).astype(BF16)
    if seq_mode:
        carry_ref[...] = carry
        hl_ref[0] = h[rows - 1:rows, :]
    else:
        hl_ref[0] = h


def _in_stage(seq_mode, x, sc, sh, zprev, p):
    b, t, d = x.shape
    tm = IN_ROW_GROUPS * IN_GROUP_ROWS if seq_mode else t
    tmod = 1 if seq_mode else tm
    grid = (b, t // tm)
    tok = lambda n: pl.BlockSpec((1, tm, n), lambda i, j: (i, j, 0))
    mod = pl.BlockSpec((1, tmod, d), lambda i, j: (i, j if not seq_mode else 0, 0))
    zp_spec = (pl.BlockSpec((1, 1, N_SHIFT_PAD), lambda i, j: (i, 0, 0)) if seq_mode
               else tok(N_SHIFT_PAD))
    hl_spec = (pl.BlockSpec((1, 1, d), lambda i, j: (i, 0, 0)) if seq_mode else tok(d))
    hl_shape = (b, 1, d) if seq_mode else (b, t, d)
    out_cols = [D_A] * 6 + [D_B] * 2 + [d] * 2
    out_dtypes = [F32] * 6 + [BF16, F32] + [BF16] * 2
    est = (2 * tm * (d + N_SHIFT_PAD + sum(out_cols) + d) * 4 + 2 * d * N_IN_PAD * 2
           + 3 * tm * N_IN_PAD * 4 + 4 * MIB)
    outs = pl.pallas_call(
        functools.partial(_in_body, seq_mode, tm),
        grid=grid,
        in_specs=[tok(d), mod, mod,
                  _const_spec((d, N_IN_PAD), True), _const_spec((1, N_SHIFT_PAD)), zp_spec,
                  _const_spec((LANES, 2 * D_A)), _const_spec((1, 2 * D_A)),
                  _const_spec((1, D_A)), _const_spec((1, D_A)),
                  _const_spec((1, D_B)), _const_spec((1, D_B)),
                  _const_spec((D_A, D_A))],
        out_specs=[tok(n) for n in out_cols] + [hl_spec],
        out_shape=[jax.ShapeDtypeStruct((b, t, n), dt) for n, dt in zip(out_cols, out_dtypes)]
        + [jax.ShapeDtypeStruct(hl_shape, F32)],
        scratch_shapes=[pltpu.VMEM((1, N_SHIFT_PAD), F32)],
        compiler_params=_params(("arbitrary", "arbitrary"), est),
        name="in_stage_seq" if seq_mode else "in_stage_row",
    )(x, sc, sh, p["w_in"], p["mu"], zprev, p["lora"], p["w0a0"], p["k_k"], p["k_a"],
      p["lnv_g"], p["lnv_b"], p["bd"])
    return outs


def _chunk_pairs(x, nsub):
    c = WKV_CHUNK
    w = 2 * HEAD
    return jnp.stack([x[c * i:c * (i + 1), w * q:w * (q + 1)]
                      for i in range(nsub) for q in range(N_HEADS // 2)], axis=0)


def _masked_rowsum(mask_bf, x):
    h1, h2 = _split2(x)
    return _dot(mask_bf, h1) + _dot(mask_bf, h2)


def _masked_rowsum_t(x, mask):
    h1, h2 = _split2(x)
    m = mask.astype(BF16)
    return _dot(h1, m) + _dot(h2, m)


def _wkv_seq_body(nsub, r_ref, lw_ref, kp_ref, v_ref, kap_ref, bb_ref, s0_ref,
                  o_ref, sout_ref, s_scr):
    c = WKV_CHUNK
    tc = nsub * c
    npair = N_HEADS // 2

    @pl.when(pl.program_id(1) == 0)
    def _():
        for q in range(npair):
            s_scr[q] = jnp.concatenate([s0_ref[0, 2 * q], s0_ref[0, 2 * q + 1]], axis=-1)

    row_t = lax.broadcasted_iota(jnp.int32, (tc, tc), 0)
    col_t = lax.broadcasted_iota(jnp.int32, (tc, tc), 1)
    shift = c.bit_length() - 1
    same_chunk = (row_t >> shift) == (col_t >> shift)
    lw = lw_ref[0]
    g = _masked_rowsum((same_chunk & (row_t >= col_t)).astype(BF16), lw)
    g_end = _masked_rowsum(same_chunk.astype(BF16), lw)
    e_neg = jnp.exp(-g)
    e_end = jnp.exp(g_end - g)
    cp = functools.partial(_chunk_pairs, nsub=nsub)
    k = kp_ref[0]
    b = bb_ref[0]
    kap_t = cp(kap_ref[0] * jnp.exp(g - lw))
    b_t = cp(b * e_neg)
    k_t = cp(k * e_neg)
    r_t = cp(r_ref[0] * jnp.exp(g))
    b_e = cp(b * e_end)
    k_e = cp(k * e_end)
    vv = cp(v_ref[0])
    decay_end = cp(jnp.exp(g_end))

    row = lax.broadcasted_iota(jnp.int32, (c, 2 * c), 0)[None]
    lane = lax.broadcasted_iota(jnp.int32, (c, 2 * c), 1)[None]
    colp = lane & (c - 1)
    right = lane >= c
    row2 = lax.broadcasted_iota(jnp.int32, (2 * c, 2 * c), 0)[None]
    lane2 = lax.broadcasted_iota(jnp.int32, (2 * c, 2 * c), 1)[None]
    same_head = (row2 >= c) == (lane2 >= c)

    def bd(x):
        return jnp.concatenate([jnp.where(right, 0.0, x), jnp.where(right, x, 0.0)], axis=1)

    p_b = _mm1(jnp.concatenate([kap_t, r_t], axis=1), bd(b_t), _B_NT)
    l_b = jnp.where(row > colp, p_b[:, :c], 0.0)
    a_rb = jnp.where(row >= colp, p_b[:, c:], 0.0)
    bd_k = bd(k_t)
    l_k = jnp.where(row > colp, _mm1(kap_t, bd_k, _B_NT), 0.0)
    a_rk = jnp.where(row >= colp, _mm1(r_t, bd_k, _B_NT), 0.0)
    m = -l_b
    t_inv = jnp.where(row == colp, 1.0, 0.0) + m
    m = _mm1(m, bd(m), _B_NN)
    span = 2
    while 2 * span < c:
        both = _mm1(jnp.concatenate([m, t_inv], axis=1), bd(m), _B_NN)
        m = both[:, :c]
        t_inv = t_inv + both[:, c:]
        span *= 2
    t_inv = t_inv + _mm1(t_inv, bd(m), _B_NN)
    bd_v = bd(vv)
    a1 = -_mm1(t_inv, bd(kap_t), _B_NN)
    u0 = -_mm1(t_inv, bd(_mm1(l_k, bd_v, _B_NN)), _B_NN)
    a2 = r_t + _mm1(a_rb, bd(a1), _B_NN)
    o0 = _mm1(a_rb, bd(u0), _B_NN) + _mm1(a_rk, bd_v, _B_NN)
    g_bd = (jnp.where(same_head, _mm1(a1, b_e, _B_TN), 0.0)
            + jnp.where(row2 == lane2, decay_end[:, 0:1, :], 0.0))
    hh = _mm1(jnp.concatenate([u0, vv], axis=1),
              jnp.concatenate([b_e, k_e], axis=1), _B_TN)
    h_pair = jnp.where(right, hh[:, c:], hh[:, :c])

    s = s_scr[...]
    for i in range(nsub):
        ps = slice(npair * i, npair * (i + 1))
        o = _mm1(a2[ps], bd(s), _B_NT) + o0[ps]
        s = _mm1(s, g_bd[ps], _B_NN) + h_pair[ps]
        for q in range(npair):
            o_ref[0, c * i:c * (i + 1), 2 * HEAD * q:2 * HEAD * (q + 1)] = o[q]
    s_scr[...] = s

    @pl.when(pl.program_id(1) == pl.num_programs(1) - 1)
    def _():
        for q in range(npair):
            sout_ref[0, 2 * q] = s[q][:, :HEAD]
            sout_ref[0, 2 * q + 1] = s[q][:, HEAD:]


def _wkv_seq(r, lw, kp, v, kap, bb, s0):
    b, t, _ = r.shape
    tc = WKV_TILE
    nsub = tc // WKV_CHUNK
    tok = pl.BlockSpec((1, tc, D_A), lambda i, j: (i, j, 0))
    st = pl.BlockSpec((1, N_HEADS, HEAD, HEAD), lambda i, j: (i, 0, 0, 0))
    return pl.pallas_call(
        functools.partial(_wkv_seq_body, nsub),
        grid=(b, t // tc),
        in_specs=[tok] * 6 + [st],
        out_specs=[tok, st],
        out_shape=[jax.ShapeDtypeStruct((b, t, D_A), F32),
                   jax.ShapeDtypeStruct((b, N_HEADS, HEAD, HEAD), F32)],
        scratch_shapes=[pltpu.VMEM((N_HEADS // 2, HEAD, 2 * HEAD), F32)],
        compiler_params=_params(("arbitrary", "arbitrary"), 32 * MIB),
        name="wkv_seq",
    )(r, lw, kp, v, kap, bb, s0)


def _wkv_step_body(r_ref, lw_ref, kp_ref, v_ref, kap_ref, bb_ref, s0_ref, _, o_ref, sout_ref):
    for h in range(N_HEADS):
        hs = slice(HEAD * h, HEAD * (h + 1))
        row = lambda ref: ref[:, hs][:, None, :]
        s = s0_ref[:, h]
        sa = -jnp.sum(s * row(kap_ref), axis=-1, keepdims=True)
        v_col = v_ref[:, hs][:, :, None]
        s_new = s * jnp.exp(row(lw_ref)) + sa * row(bb_ref) + v_col * row(kp_ref)
        sout_ref[:, h] = s_new
        o_ref[:, hs] = jnp.sum(s_new * row(r_ref), axis=-1)


def _wkv_step(r, lw, kp, v, kap, bb, s0, s_new_all, layer):
    m = r.shape[0]
    nb = WKV_STEP_ROWS
    tok = pl.BlockSpec((nb, D_A), lambda i: (i, 0))
    st = pl.BlockSpec((nb, N_HEADS, HEAD, HEAD), lambda i: (i, 0, 0, 0))
    st_out = pl.BlockSpec((None, nb, N_HEADS, HEAD, HEAD), lambda i: (layer, i, 0, 0, 0))
    return pl.pallas_call(
        _wkv_step_body,
        grid=(m // nb,),
        in_specs=[tok] * 6 + [st, pl.BlockSpec(memory_space=pl.ANY)],
        out_specs=[tok, st_out],
        out_shape=[jax.ShapeDtypeStruct((m, D_A), F32),
                   jax.ShapeDtypeStruct(s_new_all.shape, F32)],
        input_output_aliases={7: 1},
        compiler_params=_params(("arbitrary",), 56 * MIB),
        name="wkv_step",
    )(r, lw, kp, v, kap, bb, s0, s_new_all)


def _merge_body(seq_mode, tm, x_ref, o_ref, r_ref, kp_ref, v_ref, u_ref, vg_ref,
                ga_ref, gb_ref, g1_ref, sc2_ref, sh2_ref, lxg_ref, lxb_ref, rk_ref,
                bd_ref, ws_ref, bs_ref, pa_ref, pb_ref, wo_ref, l1g_ref, l1b_ref,
                x1_ref, h2_ref):
    bd = bd_ref[...]
    o = o_ref[0]
    inv_n = 1.0 / HEAD
    mu = _segsum(o, bd, two_terms=True) * inv_n
    d = o - mu
    var = _segsum(d * d, bd) * inv_n
    on = d * lax.rsqrt(var + GN_EPS) * lxg_ref[...] + lxb_ref[...]
    v = v_ref[0]
    o_a = on + _segsum(r_ref[0] * kp_ref[0] * rk_ref[...], bd) * v
    vg = vg_ref[0]
    if seq_mode:
        row = lax.broadcasted_iota(jnp.int32, (CHUNK, CHUNK), 0)
        col = lax.broadcasted_iota(jnp.int32, (CHUNK, CHUNK), 1)
        lane = lax.broadcasted_iota(jnp.int32, (CHUNK, D_B), 1)
        gsz = D_B // N_GROUPS_B
        w_cat = jnp.concatenate(
            [jnp.where(row >= col, ws_ref[g], 0.0).astype(BF16) for g in range(N_GROUPS_B)], axis=1)
        pieces = []
        for j in range(tm // CHUNK):
            vc = vg[j * CHUNK:(j + 1) * CHUNK, :].astype(BF16)
            v_bd = jnp.concatenate(
                [jnp.where((lane >= g * gsz) & (lane < (g + 1) * gsz), vc, 0.0)
                 for g in range(N_GROUPS_B)], axis=0)
            pieces.append(_dot(w_cat, v_bd) + bs_ref[...])
        s = jnp.concatenate(pieces, axis=0) if len(pieces) > 1 else pieces[0]
    else:
        s = vg * ws_ref[...] + bs_ref[...]
    o_b = u_ref[0] * s
    y = _mm1(ga_ref[0] * _mm1(o_a, pa_ref[...]) + gb_ref[0] * _mm1(o_b, pb_ref[...]),
             wo_ref[...])
    x1 = _layer_norm(ALPHA * x_ref[0] + g1_ref[0] * y, l1g_ref[...], l1b_ref[...])
    x1_ref[0] = x1
    h2_ref[0] = x1 * (1.0 + sc2_ref[0]) + sh2_ref[0]


def _merge_stage(seq_mode, x, o, r, kp, v, u, vg, ga, gb, g1, sc2, sh2, p):
    b, t, d = x.shape
    tm = MERGE_TILE if seq_mode else t
    tmod = 1 if seq_mode else tm
    tok = lambda n: pl.BlockSpec((1, tm, n), lambda i, j: (i, j, 0))
    mod = pl.BlockSpec((1, tmod, d), lambda i, j: (i, j if not seq_mode else 0, 0))
    ws, bs = (p["w_spatial"], p["b_spatial_full"]) if seq_mode else (p["ws_row"], p["bs_row"])
    est = 2 * tm * (2 * d + 7 * D_A + 2 * d + 2 * d) * 4 + 8 * tm * d * 4 + 16 * MIB
    return pl.pallas_call(
        functools.partial(_merge_body, seq_mode, tm),
        grid=(b, t // tm),
        in_specs=[tok(d)] + [tok(D_A)] * 6 + [tok(d), tok(d), mod, mod, mod,
                  _const_spec((1, D_A)), _const_spec((1, D_A)), _const_spec((1, D_A)),
                  _const_spec((D_A, D_A)), _const_spec(ws.shape), _const_spec(bs.shape),
                  _const_spec((D_A, d)), _const_spec((D_B, d)), _const_spec((d, d)),
                  _const_spec((1, d)), _const_spec((1, d))],
        out_specs=[tok(d), tok(d)],
        out_shape=[jax.ShapeDtypeStruct((b, t, d), F32)] * 2,
        compiler_params=_params(("arbitrary", "arbitrary"), est),
        name="merge_seq" if seq_mode else "merge_row",
    )(x, o, r, kp, v, u, vg, ga, gb, g1, sc2, sh2, p["lnx_g"], p["lnx_b"], p["r_k"],
      p["bd"], ws, bs, p["w_branch_a"], p["w_branch_b"], p["w_out"], p["ln1_g"], p["ln1_b"])


def _route_body(h_ref, w_ref, b_ref, gate_ref, idx_ref, wts_ref, cnt_ref):
    lg = _mm3(h_ref[...], w_ref[...]) + b_ref[...]
    lane = lax.broadcasted_iota(jnp.int32, lg.shape, 1)
    lanef = lane.astype(F32)
    neg = -jnp.inf
    is_g = (lane >= N_EXPERTS) & (lane < N_EXPERTS + N_ROUTE_GROUPS)
    mg = jnp.max(jnp.where(is_g, lg, neg), axis=-1, keepdims=True)
    gidx = jnp.min(jnp.where(is_g & (lg == mg), lanef - N_EXPERTS, 1e9), axis=-1, keepdims=True)
    pg_sel = 1.0 / jnp.sum(jnp.where(is_g, jnp.exp(lg - mg), 0.0), axis=-1, keepdims=True)
    lo = gidx * EXP_PER_GROUP
    in_grp = (lanef >= lo) & (lanef < lo + EXP_PER_GROUP)
    t1 = jnp.max(jnp.where(in_grp, lg, neg), axis=-1, keepdims=True)
    i1 = jnp.min(jnp.where(in_grp & (lg == t1), lanef, 1e9), axis=-1, keepdims=True)
    rest = in_grp & (lanef != i1)
    t2 = jnp.max(jnp.where(rest, lg, neg), axis=-1, keepdims=True)
    i2 = jnp.min(jnp.where(rest & (lg == t2), lanef, 1e9), axis=-1, keepdims=True)
    e2 = jnp.exp(t2 - t1)
    w1 = pg_sel / (1.0 + e2)
    w2 = pg_sel * e2 / (1.0 + e2)
    gate_ref[...] = jnp.where(lanef == i1, w1, 0.0) + jnp.where(lanef == i2, w2, 0.0)
    tm = lg.shape[0]
    hit = (lanef == i1) | (lanef == i2)
    earlier = (lax.broadcasted_iota(jnp.int32, (tm, tm), 0)
               > lax.broadcasted_iota(jnp.int32, (tm, tm), 1))
    rank = _dot(earlier.astype(BF16), hit.astype(BF16))
    cnt = jnp.sum(hit.astype(F32), axis=0, keepdims=True)
    run_len = jnp.floor((cnt + (MOE_RUN_ALIGN - 1.0)) * (1.0 / MOE_RUN_ALIGN)) * MOE_RUN_ALIGN
    lower_expert = (lax.broadcasted_iota(jnp.int32, (LANES, LANES), 0)
                    < lax.broadcasted_iota(jnp.int32, (LANES, LANES), 1))
    run_start = _masked_rowsum_t(jnp.broadcast_to(run_len, (SUBLANES, LANES)), lower_expert)[0:1]
    lpos = run_start + rank
    lp1 = jnp.sum(jnp.where(lanef == i1, lpos, 0.0), axis=-1, keepdims=True)
    lp2 = jnp.sum(jnp.where(lanef == i2, lpos, 0.0), axis=-1, keepdims=True)
    cnt_ref[0] = cnt.astype(jnp.int32)
    idx = jnp.where(lane == 0, i1, jnp.where(lane == 1, i2, jnp.where(lane == 2, lp1, lp2)))
    idx_ref[...] = idx.astype(jnp.int32)
    wts_ref[...] = jnp.where(lane == 0, w1, w2)


def _route(h2, p):
    m, d = h2.shape
    tm = min(m, MOE_TILE)
    tok = pl.BlockSpec((tm, LANES), lambda i: (i, 0))
    return pl.pallas_call(
        _route_body,
        grid=(m // tm,),
        in_specs=[pl.BlockSpec((tm, d), lambda i: (i, 0)),
                  _const_spec((d, LANES)), _const_spec((1, LANES))],
        out_specs=[tok, tok, tok, pl.BlockSpec((1, 1, LANES), lambda i: (i, 0, 0))],
        out_shape=[jax.ShapeDtypeStruct((m, LANES), F32),
                   jax.ShapeDtypeStruct((m, LANES), jnp.int32),
                   jax.ShapeDtypeStruct((m, LANES), F32),
                   jax.ShapeDtypeStruct((m // tm, 1, LANES), jnp.int32)],
        compiler_params=_params(("arbitrary",), 24 * MIB),
        name="route",
    )(h2, p["w_route"], p["b_route"])


def _moe_body(h_ref, gate_ref, wg_ref, wu_ref, wd_ref, x1_ref, g2_ref, l2g_ref, l2b_ref,
              out_ref, acc_ref, xb_ref):
    e = pl.program_id(1)

    @pl.when(e == 0)
    def _():
        acc_ref[...] = jnp.zeros_like(acc_ref)
        xb_ref[...] = h_ref[...].astype(BF16)

    xb = xb_ref[...]
    pre = _dot(xb, wg_ref[0].astype(BF16))
    hid = pre * jax.nn.sigmoid(pre) * _dot(xb, wu_ref[0].astype(BF16))
    ye = _mm1(hid, wd_ref[0])
    gate = gate_ref[...]
    lane = lax.broadcasted_iota(jnp.int32, gate.shape, 1)
    ge = jnp.sum(jnp.where(lane == e, gate, 0.0), axis=-1, keepdims=True)
    acc_ref[...] += ge * ye

    @pl.when(e == pl.num_programs(1) - 1)
    def _():
        out_ref[...] = _layer_norm(ALPHA * x1_ref[...] + g2_ref[0] * acc_ref[...],
                                   l2g_ref[...], l2b_ref[...])


def _moe(seq_len, h2, gate, x1, g2, p):
    m, d = h2.shape
    layer = p["layer"]
    tm = min(seq_len if g2.shape[1] == 1 else m, DENSE_MOE_TILE)
    tok = lambda n: pl.BlockSpec((tm, n), lambda i, e: (i, 0))
    if g2.shape[1] == 1:
        tiles_per_seq = seq_len // tm
        g2_spec = pl.BlockSpec((1, 1, d), lambda i, e: (i // tiles_per_seq, 0, 0))
    else:
        g2_spec = pl.BlockSpec((1, tm, d), lambda i, e: (0, i, 0))
    est = 2 * tm * (3 * d + LANES) * 4 + tm * d * 6 + 4 * tm * d * 4 + 8 * MIB
    return pl.pallas_call(
        _moe_body,
        grid=(m // tm, N_EXPERTS),
        in_specs=[tok(d), tok(LANES),
                  pl.BlockSpec((None, 1, d, D_EXPERT), lambda i, e: (layer, e, 0, 0)),
                  pl.BlockSpec((None, 1, d, D_EXPERT), lambda i, e: (layer, e, 0, 0)),
                  pl.BlockSpec((None, 1, D_EXPERT, d), lambda i, e: (layer, e, 0, 0)),
                  tok(d), g2_spec, _const_spec((1, d)), _const_spec((1, d))],
        out_specs=tok(d),
        out_shape=jax.ShapeDtypeStruct((m, d), F32),
        scratch_shapes=[pltpu.VMEM((tm, d), F32), pltpu.VMEM((tm, d), BF16)],
        compiler_params=_params(("arbitrary", "arbitrary"), est),
        name="moe_dense",
    )(h2, gate, p["w_exp_gate"], p["w_exp_up"], p["w_exp_down"], x1, g2, p["ln2_g"], p["ln2_b"])


MOE_TILE = 512
MOE_BLOCK = 256
MOE_RUN_ALIGN = 16
MXU_DEPTH = 256
MOE_LOCAL_ROWS = -(-(2 * MOE_TILE + N_EXPERTS * (MOE_RUN_ALIGN - 1)) // MXU_DEPTH) * MXU_DEPTH
MOE_SLABS = tuple(MOE_TILE >> k for k in range((MOE_TILE // MOE_RUN_ALIGN).bit_length()))


def _for_each_slab(run_ref, make_copy, fn):
    def one_run(e, c):
        dst = run_ref[0, 0, e]
        src = run_ref[0, 0, N_EXPERTS + e]
        n = run_ref[0, 0, 2 * N_EXPERTS + e]
        for slab in MOE_SLABS:
            off = n & (-2 * slab)

            @pl.when((n & slab) != 0)
            def _():
                fn(make_copy(pl.multiple_of(dst + off, MOE_RUN_ALIGN),
                             pl.multiple_of(src + off, MOE_RUN_ALIGN), slab))
        return c

    lax.fori_loop(0, N_EXPERTS, one_run, 0)


def _dispatch_body(tail_ref, run_ref, prev_run_ref, h_ref, idx_ref, xs_hbm, zero_buf, loc_buf,
                   sem, zsem, usem):
    nb_max = xs_hbm.shape[0] // MOE_BLOCK
    nt = MOE_TILE

    def zero_copy(row0, zero_sem):
        return pltpu.make_async_copy(
            zero_buf, xs_hbm.at[pl.ds(pl.multiple_of(row0, MOE_BLOCK), MOE_BLOCK)], zero_sem)

    def each_unused_copy(fn):
        def unused(j, c):
            fn(zero_copy(j * MOE_BLOCK, usem))
            return c

        lax.fori_loop(tail_ref[N_EXPERTS], nb_max, unused, 0)

    @pl.when(pl.program_id(0) == 0)
    def _():
        zero_buf[...] = jnp.zeros_like(zero_buf)

        def each_tail_copy(fn):
            for e in range(N_EXPERTS):
                @pl.when(tail_ref[e] >= 0)
                def _():
                    fn(zero_copy(tail_ref[e], zsem))

        each_tail_copy(lambda cp: cp.start())
        each_unused_copy(lambda cp: cp.start())
        each_tail_copy(lambda cp: cp.wait())

    lp = idx_ref[...].astype(F32)
    eye = (lax.broadcasted_iota(jnp.int32, (nt, nt), 0)
           == lax.broadcasted_iota(jnp.int32, (nt, nt), 1))
    as_row = lambda col: jnp.sum(jnp.where(eye, col, 0.0), axis=0, keepdims=True)
    r = lax.broadcasted_iota(jnp.int32, (MOE_LOCAL_ROWS, nt), 0).astype(F32)
    pick = (r == as_row(lp[:, 2:3])) | (r == as_row(lp[:, 3:4]))
    step = pl.program_id(0)
    slot = step & 1
    loc_buf[slot] = _dot(pick.astype(BF16), h_ref[...].astype(BF16)).astype(BF16)

    def copies_from(which):
        def make_copy(dst, src, rows):
            return pltpu.make_async_copy(loc_buf.at[which, pl.ds(src, rows)],
                                         xs_hbm.at[pl.ds(dst, rows)], sem.at[which])
        return make_copy

    _for_each_slab(run_ref, copies_from(slot), lambda cp: cp.start())

    @pl.when(step > 0)
    def _():
        _for_each_slab(prev_run_ref, copies_from(1 - slot), lambda cp: cp.wait())

    @pl.when(step == pl.num_programs(0) - 1)
    def _():
        _for_each_slab(run_ref, copies_from(slot), lambda cp: cp.wait())
        each_unused_copy(lambda cp: cp.wait())


def _dispatch(h2, idx, runs, tail_start, n_rows):
    m, d = h2.shape
    nt = MOE_TILE
    run_spec = lambda at: pl.BlockSpec((1, 1, LANES), lambda i, tail: (at(i), 0, 0),
                                       memory_space=pltpu.SMEM)
    grid_spec = pltpu.PrefetchScalarGridSpec(
        num_scalar_prefetch=1,
        grid=(m // nt,),
        in_specs=[run_spec(lambda i: i), run_spec(lambda i: jnp.maximum(i - 1, 0)),
                  pl.BlockSpec((nt, d), lambda i, tail: (i, 0)),
                  pl.BlockSpec((nt, LANES), lambda i, tail: (i, 0))],
        out_specs=pl.BlockSpec(memory_space=pl.ANY),
        scratch_shapes=[pltpu.VMEM((MOE_BLOCK, d), BF16),
                        pltpu.VMEM((2, MOE_LOCAL_ROWS, d), BF16),
                        pltpu.SemaphoreType.DMA((2,)), pltpu.SemaphoreType.DMA,
                        pltpu.SemaphoreType.DMA],
    )
    return pl.pallas_call(
        _dispatch_body,
        grid_spec=grid_spec,
        out_shape=jax.ShapeDtypeStruct((n_rows, d), BF16),
        compiler_params=_params(("arbitrary",), 40 * MIB),
        name="moe_dispatch",
    )(tail_start, runs, runs, h2, idx)


GMM_RING = 3


def _gmm_body(layer, be_ref, nb_ref, first_ref, next_ref, wslot_ref,
              x_hbm, wg_hbm, wu_hbm, wd_hbm, y_ref,
              x_ring, wg_stage, wu_stage, wd_stage, wg_bf, wu_bf, wd_bf, sem, wsem):
    j = pl.program_id(0)
    n_live = nb_ref[0]

    def fetch(step):
        slot = step % GMM_RING
        return pltpu.make_async_copy(
            x_hbm.at[pl.ds(pl.multiple_of(step * MOE_BLOCK, MOE_BLOCK), MOE_BLOCK)],
            x_ring.at[slot], sem.at[slot])

    def weight_copies(expert, slot):
        return [pltpu.make_async_copy(hbm.at[layer, expert], stage.at[slot], wsem.at[slot, i])
                for i, (hbm, stage) in enumerate(((wg_hbm, wg_stage), (wu_hbm, wu_stage),
                                                  (wd_hbm, wd_stage)))]

    @pl.when(j == 0)
    def _():
        for cp in weight_copies(be_ref[0], 0):
            cp.start()
        for ahead in range(GMM_RING - 1):
            @pl.when(ahead < n_live)
            def _():
                fetch(ahead).start()

    @pl.when(j + (GMM_RING - 1) < n_live)
    def _():
        fetch(j + (GMM_RING - 1)).start()

    @pl.when(j < n_live)
    def _():
        @pl.when(first_ref[j] == 1)
        def _():
            slot = wslot_ref[j]
            for cp in weight_copies(be_ref[j], slot):
                cp.wait()
            wg_bf[...] = wg_stage[slot].astype(BF16)
            wu_bf[...] = wu_stage[slot].astype(BF16)
            wd_bf[...] = wd_stage[slot].astype(BF16)

            @pl.when(next_ref[j] >= 0)
            def _():
                for cp in weight_copies(next_ref[j], 1 - slot):
                    cp.start()

        fetch(j).wait()
        xb = x_ring[j % GMM_RING]
        pre = _dot(xb, wg_bf[...])
        hid = pre * jax.nn.sigmoid(pre) * _dot(xb, wu_bf[...])
        y_ref[...] = _dot(hid.astype(BF16), wd_bf[...]).astype(BF16)

    @pl.when(j >= nb_ref[0])
    def _():
        y_ref[...] = jnp.zeros_like(y_ref)


def _gmm(xs, blk_expert, n_blocks, run_first, run_next, run_slot, p):
    n_rows, d = xs.shape
    nb_max = n_rows // MOE_BLOCK
    any_spec = pl.BlockSpec(memory_space=pl.ANY)
    grid_spec = pltpu.PrefetchScalarGridSpec(
        num_scalar_prefetch=5,
        grid=(nb_max,),
        in_specs=[any_spec] * 4,
        out_specs=pl.BlockSpec((MOE_BLOCK, d), lambda j, *_: (j, 0)),
        scratch_shapes=[pltpu.VMEM((GMM_RING, MOE_BLOCK, d), BF16),
                        pltpu.VMEM((2, d, D_EXPERT), F32), pltpu.VMEM((2, d, D_EXPERT), F32),
                        pltpu.VMEM((2, D_EXPERT, d), F32),
                        pltpu.VMEM((d, D_EXPERT), BF16), pltpu.VMEM((d, D_EXPERT), BF16),
                        pltpu.VMEM((D_EXPERT, d), BF16),
                        pltpu.SemaphoreType.DMA((GMM_RING,)), pltpu.SemaphoreType.DMA((2, 3))],
    )
    return pl.pallas_call(
        functools.partial(_gmm_body, p["layer"]),
        grid_spec=grid_spec,
        out_shape=jax.ShapeDtypeStruct((n_rows, d), BF16),
        compiler_params=_params(("arbitrary",), 24 * MIB),
        name="moe_gmm",
    )(blk_expert, n_blocks, run_first, run_next, run_slot,
      xs, p["w_exp_gate"], p["w_exp_up"], p["w_exp_down"])


def _combine_body(run_ref, next_run_ref, ys_hbm, idx_ref, wts_ref, x1_ref, g2_ref, l2g_ref,
                  l2b_ref, out_ref, loc_buf, sem):
    nt = MOE_TILE
    step = pl.program_id(0)
    slot = step & 1

    def copies_into(which):
        def make_copy(dst, src, rows):
            return pltpu.make_async_copy(ys_hbm.at[pl.ds(dst, rows)],
                                         loc_buf.at[which, pl.ds(src, rows)], sem.at[which])
        return make_copy

    @pl.when(step == 0)
    def _():
        loc_buf[...] = jnp.zeros_like(loc_buf)
        _for_each_slab(run_ref, copies_into(slot), lambda cp: cp.start())

    @pl.when(step < pl.num_programs(0) - 1)
    def _():
        _for_each_slab(next_run_ref, copies_into(1 - slot), lambda cp: cp.start())

    _for_each_slab(run_ref, copies_into(slot), lambda cp: cp.wait())
    lp = idx_ref[...].astype(F32)
    w = wts_ref[...]
    c = lax.broadcasted_iota(jnp.int32, (nt, MOE_LOCAL_ROWS), 1).astype(F32)
    sel = (jnp.where(c == lp[:, 2:3], w[:, 0:1], 0.0)
           + jnp.where(c == lp[:, 3:4], w[:, 1:2], 0.0))
    sel_hi, sel_lo = _split2(sel)
    rows_bf = loc_buf[slot]
    moe = _dot(sel_hi, rows_bf) + _dot(sel_lo, rows_bf)
    out_ref[...] = _layer_norm(ALPHA * x1_ref[...] + g2_ref[0] * moe, l2g_ref[...], l2b_ref[...])


def _combine(seq_len, ys, idx, wts, runs, x1, g2, p):
    m, d = x1.shape
    nt = MOE_TILE
    tiles_per_seq = seq_len // nt
    tok = lambda n: pl.BlockSpec((nt, n), lambda i: (i, 0))
    n_tiles = m // nt
    run_spec = lambda at: pl.BlockSpec((1, 1, LANES), lambda i: (at(i), 0, 0),
                                       memory_space=pltpu.SMEM)
    return pl.pallas_call(
        _combine_body,
        grid=(n_tiles,),
        in_specs=[run_spec(lambda i: i), run_spec(lambda i: jnp.minimum(i + 1, n_tiles - 1)),
                  pl.BlockSpec(memory_space=pl.ANY), tok(LANES), tok(LANES), tok(d),
                  pl.BlockSpec((1, 1, d), lambda i: (i // tiles_per_seq, 0, 0)),
                  _const_spec((1, d)), _const_spec((1, d))],
        out_specs=tok(d),
        out_shape=jax.ShapeDtypeStruct((m, d), F32),
        scratch_shapes=[pltpu.VMEM((2, MOE_LOCAL_ROWS, d), BF16), pltpu.SemaphoreType.DMA((2,))],
        compiler_params=_params(("arbitrary",), 48 * MIB),
        name="moe_combine",
    )(runs, runs, ys, idx, wts, x1, g2, p["ln2_g"], p["ln2_b"])


def _moe_routed(seq_len, h2, idx, wts, tile_cnt, x1, g2, p):
    m, d = h2.shape
    blk = MOE_BLOCK
    n_tiles = m // MOE_TILE
    worst_rows = 2 * m + n_tiles * N_EXPERTS * (MOE_RUN_ALIGN - 1) + N_EXPERTS * (blk - 1)
    nb_max = -(-worst_rows // blk)
    cnt = tile_cnt[:, 0, :N_EXPERTS]
    run_len = ((cnt + MOE_RUN_ALIGN - 1) // MOE_RUN_ALIGN) * MOE_RUN_ALIGN
    local_row = jnp.cumsum(run_len, axis=1) - run_len
    rows_before = jnp.cumsum(run_len, axis=0) - run_len
    total = jnp.sum(run_len, axis=0)
    padded = ((total + blk - 1) // blk) * blk
    ends = jnp.cumsum(padded)
    starts = ends - padded
    n_blocks = (ends[-1] // blk).astype(jnp.int32).reshape(1)
    first_row = jnp.arange(nb_max, dtype=jnp.int32) * blk
    blk_expert = jnp.minimum(
        jnp.sum((first_row[:, None] >= ends[None, :]).astype(jnp.int32), axis=1),
        N_EXPERTS - 1).astype(jnp.int32)
    tail_start = jnp.concatenate(
        [jnp.where(padded > 0, ends - blk, -1).astype(jnp.int32), n_blocks])
    runs = jnp.concatenate(
        [starts[None, :] + rows_before, local_row, run_len,
         jnp.zeros((n_tiles, LANES - 3 * N_EXPERTS), jnp.int32)], axis=1).astype(jnp.int32)
    runs = runs.reshape(n_tiles, 1, LANES)
    prev_expert = jnp.concatenate([jnp.full((1,), -1, jnp.int32), blk_expert[:-1]])
    run_first = (blk_expert != prev_expert).astype(jnp.int32)
    run_slot = ((jnp.cumsum(run_first) - 1) % 2).astype(jnp.int32)
    used = padded > 0
    later = jnp.arange(N_EXPERTS)[None, :] > jnp.arange(N_EXPERTS)[:, None]
    next_used = jnp.min(jnp.where(later & used[None, :], jnp.arange(N_EXPERTS)[None, :], N_EXPERTS),
                        axis=1)
    run_next = jnp.where(next_used < N_EXPERTS, next_used, -1).astype(jnp.int32)[blk_expert]
    xs = _dispatch(h2, idx, runs, tail_start, nb_max * blk)
    ys = _gmm(xs, blk_expert, n_blocks, run_first, run_next, run_slot, p)
    return _combine(seq_len, ys, idx, wts, runs, x1, g2, p)


def _prep_layer(l, w_in, mu_shift, w0, w_decay_up, a0, w_iclr_up, k_k, k_a, r_k,
                lnx_g, lnx_b, lnv_g, lnv_b, w_spatial, b_spatial, w_branch_a, w_branch_b,
                w_out, ln1_g, ln1_b, w_route_group, b_route_group, w_route_expert,
                b_route_expert, w_exp_gate, w_exp_up, w_exp_down, ln2_g, ln2_b):
    d = D_MODEL
    pad_a = N_SHIFT_PAD - N_SHIFT
    wi = w_in[l]
    w_in_p = jnp.concatenate(
        [wi[:, :N_SHIFT], jnp.zeros((d, pad_a), F32), wi[:, N_SHIFT:]], axis=1).astype(BF16)
    mu = jnp.concatenate([mu_shift[l], jnp.zeros((pad_a,), F32)])[None]
    lora = jnp.zeros((LANES, 2 * D_A), F32)
    lora = lora.at[:R_LORA, :D_A].set(w_decay_up[l]).at[R_LORA:2 * R_LORA, D_A:].set(w_iclr_up[l])
    seg = jnp.arange(D_A) // HEAD
    row1 = lambda x: x.reshape(1, -1)
    gsz = D_B // N_GROUPS_B
    w_route = jnp.concatenate(
        [w_route_expert[l], w_route_group[l],
         jnp.zeros((d, LANES - N_EXPERTS - N_ROUTE_GROUPS), F32)], axis=1)
    b_route = jnp.concatenate(
        [b_route_expert[l], b_route_group[l],
         jnp.zeros((LANES - N_EXPERTS - N_ROUTE_GROUPS,), F32)])[None]
    return dict(
        w_in=w_in_p, mu=mu, lora=lora,
        w0a0=jnp.concatenate([w0[l], a0[l]])[None],
        k_k=row1(k_k[l]), k_a=row1(k_a[l]), r_k=row1(r_k[l]),
        lnx_g=row1(lnx_g[l]), lnx_b=row1(lnx_b[l]),
        lnv_g=row1(lnv_g[l]), lnv_b=row1(lnv_b[l]),
        bd=(seg[:, None] == seg[None, :]).astype(BF16),
        w_spatial=w_spatial[l],
        b_spatial_full=jnp.repeat(b_spatial[l].T, gsz, axis=1),
        ws_row=jnp.repeat(w_spatial[l][:, 0, 0], gsz)[None],
        bs_row=jnp.repeat(b_spatial[l][:, 0], gsz)[None],
        w_branch_a=w_branch_a[l].astype(BF16), w_branch_b=w_branch_b[l].astype(BF16),
        w_out=w_out[l].astype(BF16), ln1_g=row1(ln1_g[l]), ln1_b=row1(ln1_b[l]),
        w_route=w_route, b_route=b_route,
        w_exp_gate=w_exp_gate, w_exp_up=w_exp_up, w_exp_down=w_exp_down, layer=l,
        ln2_g=row1(ln2_g[l]), ln2_b=row1(ln2_b[l]),
    )


def _trunk(seq_mode, x, mods, wkv_in, shift_in, preps):
    b, t, d = x.shape
    wkv_out, shift_out, v_out = [], [], []
    wkv_acc = None if seq_mode else jnp.zeros(wkv_in.shape, F32)
    for l in range(DEPTH):
        p = preps[l]
        sh1, sc1, g1, sh2, sc2, g2 = mods[l]
        if seq_mode:
            zprev = jnp.zeros((b, 1, N_SHIFT_PAD), F32) if shift_in is None else shift_in[l]
        else:
            zprev = _matmul(shift_in[l], p["w_in"][:, :N_SHIFT_PAD]).reshape(b, t, N_SHIFT_PAD)
        r, lw, kp, v, kap, bb, u, vg, ga, gb, hl = _in_stage(seq_mode, x, sc1, sh1, zprev, p)
        if seq_mode:
            o, s_new = _wkv_seq(r, lw, kp, v, kap, bb, wkv_in[l])
        else:
            flat = lambda a: a.reshape(t, D_A)
            o, wkv_acc = _wkv_step(flat(r), flat(lw), flat(kp), flat(v), flat(kap), flat(bb),
                                   wkv_in[l], wkv_acc, l)
            o = o.reshape(b, t, D_A)
        x1, h2 = _merge_stage(seq_mode, x, o, r, kp, v, u, vg, ga, gb, g1, sc2, sh2, p)
        m = b * t
        h2f = h2.reshape(m, d)
        gate, idx, wts, tile_cnt = _route(h2f, p)
        if seq_mode:
            x = _moe_routed(t, h2f, idx, wts, tile_cnt, x1.reshape(m, d), g2, p)
        else:
            x = _moe(t, h2f, gate, x1.reshape(m, d), g2, p)
        x = x.reshape(b, t, d)
        if seq_mode:
            wkv_out.append(s_new)
        shift_out.append(hl)
        v_out.append(vg)
    if not seq_mode:
        wkv_out = wkv_acc
    return x, wkv_out, shift_out, v_out


def kernel(x_prompt, x_sample, c_prompt, c_sample, state_wkv, state_shift, w_ada, b_ada, w_in, mu_shift, w0, w_decay_up, a0, w_iclr_up, k_k, k_a, r_k, lnx_g, lnx_b, lnv_g, lnv_b, w_spatial, b_spatial, w_branch_a, w_branch_b, w_out, ln1_g, ln1_b, w_route_group, b_route_group, w_route_expert, b_route_expert, w_exp_gate, w_exp_up, w_exp_down, ln2_g, ln2_b):
    bp, tp, d = x_prompt.shape
    bs = x_sample.shape[0]
    layer_params = (w_in, mu_shift, w0, w_decay_up, a0, w_iclr_up, k_k, k_a, r_k, lnx_g,
                    lnx_b, lnv_g, lnv_b, w_spatial, b_spatial, w_branch_a, w_branch_b, w_out,
                    ln1_g, ln1_b, w_route_group, b_route_group, w_route_expert,
                    b_route_expert, w_exp_gate, w_exp_up, w_exp_down, ln2_g, ln2_b)
    preps = [_prep_layer(l, *layer_params) for l in range(DEPTH)]
    mod_all = _ada(jnp.concatenate([c_prompt, c_sample], axis=0), w_ada, b_ada)
    mods_p, mods_s = [], []
    for l in range(DEPTH):
        parts = jnp.split(mod_all[l], 6, axis=-1)
        mods_p.append([q[:bp].reshape(bp, 1, d) for q in parts])
        mods_s.append([q[bp:].reshape(1, bs, d) for q in parts])

    wkv0 = jnp.zeros((DEPTH, bp, N_HEADS, HEAD, HEAD), F32)
    y_p, wkv_p, shift_p, _ = _trunk(True, x_prompt, mods_p, wkv0, None, preps)
    y_s, wkv_s, shift_s, v_s = _trunk(False, x_sample.reshape(1, bs, d), mods_s, state_wkv,
                                      state_shift, preps)
    return (y_p,
            y_s.reshape(bs, 1, d),
            jnp.stack(wkv_p),
            jnp.stack([s.reshape(bp, d) for s in shift_p]),
            wkv_s,
            jnp.stack([s.reshape(bs, d) for s in shift_s]),
            jnp.stack([q.reshape(bs, 1, D_B) for q in v_s]))
```

```python
import functools

import jax
import jax.numpy as jnp
from jax import lax
from jax.experimental import pallas as pl
from jax.experimental.pallas import tpu as pltpu

F32 = jnp.float32
BF16 = jnp.bfloat16

D_MODEL = 1024
DEPTH = 2
HEAD = 64
N_HEADS = 8
D_A = N_HEADS * HEAD
R_LORA = 32
CHUNK = 128
N_GROUPS_B = 8
D_B = 512
N_SHIFT = 3 * D_A + 2 * R_LORA
N_ROUTE_GROUPS = 4
EXP_PER_GROUP = 8
N_EXPERTS = N_ROUTE_GROUPS * EXP_PER_GROUP
D_EXPERT = 256
ALPHA = (2 * DEPTH) ** 0.25
LN_EPS = 1e-5
GN_EPS = 64e-5

LANES = 128
SUBLANES = 8
MIB = 2 ** 20
N_SHIFT_PAD = 13 * LANES
COL_U = N_SHIFT_PAD
COL_VG = COL_U + D_B
COL_GA = COL_VG + D_B
COL_GB = COL_GA + D_MODEL
N_IN_PAD = COL_GB + D_MODEL
WKV_CHUNK = 64
WKV_TILE = 256
WKV_STEP_ROWS = 32
MERGE_TILE = 512
DENSE_MOE_TILE = 1024
ADA_COLS = 1024
IN_ROW_GROUPS = 2
IN_GROUP_ROWS = 256
VMEM_CAP_BYTES = 60000 * 1024

_NN = (((1,), (0,)), ((), ()))
_B_NT = (((2,), (2,)), ((0,), (0,)))
_B_NN = (((2,), (1,)), ((0,), (0,)))
_B_TN = (((1,), (1,)), ((0,), (0,)))


def _dot(a, b, dims=_NN):
    return lax.dot_general(a, b, dims, preferred_element_type=F32)


def _split2(x):
    hi = x.astype(BF16)
    lo = (x - hi.astype(F32)).astype(BF16)
    return hi, lo


def _mm1(a, b, dims=_NN):
    return _dot(a.astype(BF16), b.astype(BF16), dims)


def _mm3(a, b, dims=_NN):
    ah, al = _split2(a)
    bh, bl = _split2(b)
    return _dot(ah, bh, dims) + (_dot(ah, bl, dims) + _dot(al, bh, dims))


def _segsum(x, bd, two_terms=False):
    if not two_terms:
        return _dot(x.astype(BF16), bd)
    hi, lo = _split2(x)
    return _dot(hi, bd) + _dot(lo, bd)


def _layer_norm(x, g, b):
    mu = jnp.mean(x, axis=-1, keepdims=True)
    d = x - mu
    var = jnp.mean(d * d, axis=-1, keepdims=True)
    return d * lax.rsqrt(var + LN_EPS) * g + b


def _gelu(x):
    return 0.5 * x * (1.0 + lax.erf(x * 0.7071067811865476))


def _params(sem, est_bytes):
    limit = int(min(VMEM_CAP_BYTES, max(est_bytes, 16 * 1024 * 1024)))
    return pltpu.CompilerParams(dimension_semantics=sem, vmem_limit_bytes=limit)


def _const_spec(shape, single_buffer=False):
    nd = len(shape)
    if single_buffer:
        return pl.BlockSpec(shape, lambda *_: (0,) * nd, pipeline_mode=pl.Buffered(1))
    return pl.BlockSpec(shape, lambda *_: (0,) * nd)


def _ada_body(c_ref, w_ref, b_ref, o_ref):
    c = c_ref[...]
    s = c * jax.nn.sigmoid(c)
    o_ref[0] = _mm3(s, w_ref[0]) + b_ref[0]


def _ada(c_all, w_ada, b_ada):
    depth, d, n6 = w_ada.shape
    m = c_all.shape[0]
    tn = ADA_COLS
    return pl.pallas_call(
        _ada_body,
        grid=(depth, n6 // tn),
        in_specs=[
            pl.BlockSpec((m, d), lambda l, j: (0, 0)),
            pl.BlockSpec((1, d, tn), lambda l, j: (l, 0, j)),
            pl.BlockSpec((1, 1, tn), lambda l, j: (l, 0, j)),
        ],
        out_specs=pl.BlockSpec((1, m, tn), lambda l, j: (l, 0, j)),
        out_shape=jax.ShapeDtypeStruct((depth, m, n6), F32),
        compiler_params=_params(("arbitrary", "arbitrary"), 24 * MIB),
        name="ada_mod",
    )(c_all, w_ada, b_ada.reshape(depth, 1, n6))


def _mm_body(x_ref, w_ref, o_ref):
    o_ref[...] = _mm1(x_ref[...], w_ref[...])


def _matmul(x, w):
    m, k = x.shape
    n = w.shape[1]
    return pl.pallas_call(
        _mm_body,
        grid=(1,),
        in_specs=[_const_spec((m, k)), _const_spec((k, n))],
        out_specs=_const_spec((m, n)),
        out_shape=jax.ShapeDtypeStruct((m, n), F32),
        compiler_params=_params(("arbitrary",), 24 * MIB),
        name="shift_proj",
    )(x, w)


def _in_body(seq_mode, tm, x_ref, sc_ref, sh_ref, w_ref, mu_ref, zp_ref, lora_ref,
             w0a0_ref, kk_ref, ka_ref, lng_ref, lnb_ref, bd_ref,
             r_ref, lw_ref, kp_ref, v_ref, kap_ref, bb_ref, u_ref, vg_ref,
             ga_ref, gb_ref, hl_ref, carry_ref):
    nsplit = IN_ROW_GROUPS if seq_mode else 1
    rows = tm // nsplit
    if seq_mode:
        @pl.when(pl.program_id(1) == 0)
        def _():
            carry_ref[...] = zp_ref[0]

        carry = carry_ref[...]
    bd = bd_ref[...]
    for part in range(nsplit):
        sl = slice(part * rows, (part + 1) * rows)
        h = x_ref[0, sl, :] * (1.0 + sc_ref[0]) + sh_ref[0]
        hb = h.astype(BF16)
        proj = lambda lo, hi: _dot(hb, w_ref[:, lo:hi])
        za = proj(0, N_SHIFT_PAD)
        if seq_mode:
            row = lax.broadcasted_iota(jnp.int32, za.shape, 0)
            prev = jnp.where(row == 0, carry, pltpu.roll(za, 1, 0))
            carry = za[rows - 1:rows, :]
        else:
            prev = zp_ref[0]
        mix = za + mu_ref[...] * (prev - za)
        r = mix[:, 0:D_A]
        k = mix[:, D_A:2 * D_A]
        v = mix[:, 2 * D_A:3 * D_A]
        xwa = mix[:, 3 * D_A:N_SHIFT_PAD]
        lane = lax.broadcasted_iota(jnp.int32, xwa.shape, 1)
        lora_in = jnp.where(lane < R_LORA, jnp.tanh(xwa), xwa)
        pre = w0a0_ref[...] + _mm1(lora_in, lora_ref[...])
        yw = -pre[:, :D_A]
        softplus = jnp.maximum(yw, 0.0) + jnp.log1p(jnp.exp(-jnp.abs(yw)))
        lw = -jnp.exp(-softplus - 0.5)
        a = jax.nn.sigmoid(pre[:, D_A:])
        kk = k * kk_ref[...]
        kap = kk / jnp.maximum(jnp.sqrt(_segsum(kk * kk, bd)), 1e-12)
        r_ref[0, sl, :] = r
        lw_ref[0, sl, :] = lw
        kp_ref[0, sl, :] = k * (1.0 + (a - 1.0) * ka_ref[...])
        v_ref[0, sl, :] = v
        kap_ref[0, sl, :] = kap
        bb_ref[0, sl, :] = kap * a
        u_ref[0, sl, :] = _gelu(proj(COL_U, COL_VG)).astype(BF16)
        vg_ref[0, sl, :] = _layer_norm(_gelu(proj(COL_VG, COL_GA)), lng_ref[...], lnb_ref[...])
        ga_ref[0, sl, :] = jax.nn.sigmoid(proj(COL_GA, COL_GB)).astype(BF16)
        gb_ref[0, sl, :] = jax.nn.sigmoid(proj(COL_GB, N_IN_PAD)).astype(BF16)
    if seq_mode:
        carry_ref[...] = carry
        hl_ref[0] = h[rows - 1:rows, :]
    else:
        hl_ref[0] = h


def _in_stage(seq_mode, x, sc, sh, zprev, p):
    b, t, d = x.shape
    tm = IN_ROW_GROUPS * IN_GROUP_ROWS if seq_mode else t
    tmod = 1 if seq_mode else tm
    grid = (b, t // tm)
    tok = lambda n: pl.BlockSpec((1, tm, n), lambda i, j: (i, j, 0))
    mod = pl.BlockSpec((1, tmod, d), lambda i, j: (i, j if not seq_mode else 0, 0))
    zp_spec = (pl.BlockSpec((1, 1, N_SHIFT_PAD), lambda i, j: (i, 0, 0)) if seq_mode
               else tok(N_SHIFT_PAD))
    hl_spec = (pl.BlockSpec((1, 1, d), lambda i, j: (i, 0, 0)) if seq_mode else tok(d))
    hl_shape = (b, 1, d) if seq_mode else (b, t, d)
    out_cols = [D_A] * 6 + [D_B] * 2 + [d] * 2
    out_dtypes = [F32] * 6 + [BF16, F32] + [BF16] * 2
    est = (2 * tm * (d + N_SHIFT_PAD + sum(out_cols) + d) * 4 + 2 * d * N_IN_PAD * 2
           + 3 * tm * N_IN_PAD * 4 + 4 * MIB)
    outs = pl.pallas_call(
        functools.partial(_in_body, seq_mode, tm),
        grid=grid,
        in_specs=[tok(d), mod, mod,
                  _const_spec((d, N_IN_PAD), True), _const_spec((1, N_SHIFT_PAD)), zp_spec,
                  _const_spec((LANES, 2 * D_A)), _const_spec((1, 2 * D_A)),
                  _const_spec((1, D_A)), _const_spec((1, D_A)),
                  _const_spec((1, D_B)), _const_spec((1, D_B)),
                  _const_spec((D_A, D_A))],
        out_specs=[tok(n) for n in out_cols] + [hl_spec],
        out_shape=[jax.ShapeDtypeStruct((b, t, n), dt) for n, dt in zip(out_cols, out_dtypes)]
        + [jax.ShapeDtypeStruct(hl_shape, F32)],
        scratch_shapes=[pltpu.VMEM((1, N_SHIFT_PAD), F32)],
        compiler_params=_params(("arbitrary", "arbitrary"), est),
        name="in_stage_seq" if seq_mode else "in_stage_row",
    )(x, sc, sh, p["w_in"], p["mu"], zprev, p["lora"], p["w0a0"], p["k_k"], p["k_a"],
      p["lnv_g"], p["lnv_b"], p["bd"])
    return outs


def _chunk_pairs(x, nsub):
    c = WKV_CHUNK
    w = 2 * HEAD
    return jnp.stack([x[c * i:c * (i + 1), w * q:w * (q + 1)]
                      for i in range(nsub) for q in range(N_HEADS // 2)], axis=0)


def _masked_rowsum(mask_bf, x):
    h1, h2 = _split2(x)
    return _dot(mask_bf, h1) + _dot(mask_bf, h2)


def _masked_rowsum_t(x, mask):
    h1, h2 = _split2(x)
    m = mask.astype(BF16)
    return _dot(h1, m) + _dot(h2, m)


def _wkv_seq_body(nsub, r_ref, lw_ref, kp_ref, v_ref, kap_ref, bb_ref, s0_ref,
                  o_ref, sout_ref, s_scr):
    c = WKV_CHUNK
    tc = nsub * c
    npair = N_HEADS // 2

    @pl.when(pl.program_id(1) == 0)
    def _():
        for q in range(npair):
            s_scr[q] = jnp.concatenate([s0_ref[0, 2 * q], s0_ref[0, 2 * q + 1]], axis=-1)

    row_t = lax.broadcasted_iota(jnp.int32, (tc, tc), 0)
    col_t = lax.broadcasted_iota(jnp.int32, (tc, tc), 1)
    shift = c.bit_length() - 1
    same_chunk = (row_t >> shift) == (col_t >> shift)
    lw = lw_ref[0]
    g = _masked_rowsum((same_chunk & (row_t >= col_t)).astype(BF16), lw)
    g_end = _masked_rowsum(same_chunk.astype(BF16), lw)
    e_neg = jnp.exp(-g)
    e_end = jnp.exp(g_end - g)
    cp = functools.partial(_chunk_pairs, nsub=nsub)
    k = kp_ref[0]
    b = bb_ref[0]
    kap_t = cp(kap_ref[0] * jnp.exp(g - lw))
    b_t = cp(b * e_neg)
    k_t = cp(k * e_neg)
    r_t = cp(r_ref[0] * jnp.exp(g))
    b_e = cp(b * e_end)
    k_e = cp(k * e_end)
    vv = cp(v_ref[0])
    decay_end = cp(jnp.exp(g_end))

    row = lax.broadcasted_iota(jnp.int32, (c, 2 * c), 0)[None]
    lane = lax.broadcasted_iota(jnp.int32, (c, 2 * c), 1)[None]
    colp = lane & (c - 1)
    right = lane >= c
    row2 = lax.broadcasted_iota(jnp.int32, (2 * c, 2 * c), 0)[None]
    lane2 = lax.broadcasted_iota(jnp.int32, (2 * c, 2 * c), 1)[None]
    same_head = (row2 >= c) == (lane2 >= c)

    def bd(x):
        return jnp.concatenate([jnp.where(right, 0.0, x), jnp.where(right, x, 0.0)], axis=1)

    p_b = _mm1(jnp.concatenate([kap_t, r_t], axis=1), bd(b_t), _B_NT)
    l_b = jnp.where(row > colp, p_b[:, :c], 0.0)
    a_rb = jnp.where(row >= colp, p_b[:, c:], 0.0)
    bd_k = bd(k_t)
    l_k = jnp.where(row > colp, _mm1(kap_t, bd_k, _B_NT), 0.0)
    a_rk = jnp.where(row >= colp, _mm1(r_t, bd_k, _B_NT), 0.0)
    m = -l_b
    t_inv = jnp.where(row == colp, 1.0, 0.0) + m
    m = _mm1(m, bd(m), _B_NN)
    span = 2
    while 2 * span < c:
        both = _mm1(jnp.concatenate([m, t_inv], axis=1), bd(m), _B_NN)
        m = both[:, :c]
        t_inv = t_inv + both[:, c:]
        span *= 2
    t_inv = t_inv + _mm1(t_inv, bd(m), _B_NN)
    bd_v = bd(vv)
    a1 = -_mm1(t_inv, bd(kap_t), _B_NN)
    u0 = -_mm1(t_inv, bd(_mm1(l_k, bd_v, _B_NN)), _B_NN)
    a2 = r_t + _mm1(a_rb, bd(a1), _B_NN)
    o0 = _mm1(a_rb, bd(u0), _B_NN) + _mm1(a_rk, bd_v, _B_NN)
    g_bd = (jnp.where(same_head, _mm1(a1, b_e, _B_TN), 0.0)
            + jnp.where(row2 == lane2, decay_end[:, 0:1, :], 0.0))
    hh = _mm1(jnp.concatenate([u0, vv], axis=1),
              jnp.concatenate([b_e, k_e], axis=1), _B_TN)
    h_pair = jnp.where(right, hh[:, c:], hh[:, :c])

    s = s_scr[...]
    for i in range(nsub):
        ps = slice(npair * i, npair * (i + 1))
        o = _mm1(a2[ps], bd(s), _B_NT) + o0[ps]
        s = _mm1(s, g_bd[ps], _B_NN) + h_pair[ps]
        for q in range(npair):
            o_ref[0, c * i:c * (i + 1), 2 * HEAD * q:2 * HEAD * (q + 1)] = o[q]
    s_scr[...] = s

    @pl.when(pl.program_id(1) == pl.num_programs(1) - 1)
    def _():
        for q in range(npair):
            sout_ref[0, 2 * q] = s[q][:, :HEAD]
            sout_ref[0, 2 * q + 1] = s[q][:, HEAD:]


def _wkv_seq(r, lw, kp, v, kap, bb, s0):
    b, t, _ = r.shape
    tc = WKV_TILE
    nsub = tc // WKV_CHUNK
    tok = pl.BlockSpec((1, tc, D_A), lambda i, j: (i, j, 0))
    st = pl.BlockSpec((1, N_HEADS, HEAD, HEAD), lambda i, j: (i, 0, 0, 0))
    return pl.pallas_call(
        functools.partial(_wkv_seq_body, nsub),
        grid=(b, t // tc),
        in_specs=[tok] * 6 + [st],
        out_specs=[tok, st],
        out_shape=[jax.ShapeDtypeStruct((b, t, D_A), F32),
                   jax.ShapeDtypeStruct((b, N_HEADS, HEAD, HEAD), F32)],
        scratch_shapes=[pltpu.VMEM((N_HEADS // 2, HEAD, 2 * HEAD), F32)],
        compiler_params=_params(("arbitrary", "arbitrary"), 32 * MIB),
        name="wkv_seq",
    )(r, lw, kp, v, kap, bb, s0)


def _wkv_step_body(r_ref, lw_ref, kp_ref, v_ref, kap_ref, bb_ref, s0_ref, _, o_ref, sout_ref):
    for h in range(N_HEADS):
        hs = slice(HEAD * h, HEAD * (h + 1))
        row = lambda ref: ref[:, hs][:, None, :]
        s = s0_ref[:, h]
        sa = -jnp.sum(s * row(kap_ref), axis=-1, keepdims=True)
        v_col = v_ref[:, hs][:, :, None]
        s_new = s * jnp.exp(row(lw_ref)) + sa * row(bb_ref) + v_col * row(kp_ref)
        sout_ref[:, h] = s_new
        o_ref[:, hs] = jnp.sum(s_new * row(r_ref), axis=-1)


def _wkv_step(r, lw, kp, v, kap, bb, s0, s_new_all, layer):
    m = r.shape[0]
    nb = WKV_STEP_ROWS
    tok = pl.BlockSpec((nb, D_A), lambda i: (i, 0))
    st = pl.BlockSpec((nb, N_HEADS, HEAD, HEAD), lambda i: (i, 0, 0, 0))
    st_out = pl.BlockSpec((None, nb, N_HEADS, HEAD, HEAD), lambda i: (layer, i, 0, 0, 0))
    return pl.pallas_call(
        _wkv_step_body,
        grid=(m // nb,),
        in_specs=[tok] * 6 + [st, pl.BlockSpec(memory_space=pl.ANY)],
        out_specs=[tok, st_out],
        out_shape=[jax.ShapeDtypeStruct((m, D_A), F32),
                   jax.ShapeDtypeStruct(s_new_all.shape, F32)],
        input_output_aliases={7: 1},
        compiler_params=_params(("arbitrary",), 56 * MIB),
        name="wkv_step",
    )(r, lw, kp, v, kap, bb, s0, s_new_all)


def _merge_body(seq_mode, tm, x_ref, o_ref, r_ref, kp_ref, v_ref, u_ref, vg_ref,
                ga_ref, gb_ref, g1_ref, sc2_ref, sh2_ref, lxg_ref, lxb_ref, rk_ref,
                bd_ref, ws_ref, bs_ref, pa_ref, pb_ref, wo_ref, l1g_ref, l1b_ref,
                x1_ref, h2_ref):
    bd = bd_ref[...]
    o = o_ref[0]
    inv_n = 1.0 / HEAD
    mu = _segsum(o, bd, two_terms=True) * inv_n
    d = o - mu
    var = _segsum(d * d, bd) * inv_n
    on = d * lax.rsqrt(var + GN_EPS) * lxg_ref[...] + lxb_ref[...]
    v = v_ref[0]
    o_a = on + _segsum(r_ref[0] * kp_ref[0] * rk_ref[...], bd) * v
    vg = vg_ref[0]
    if seq_mode:
        row = lax.broadcasted_iota(jnp.int32, (CHUNK, CHUNK), 0)
        col = lax.broadcasted_iota(jnp.int32, (CHUNK, CHUNK), 1)
        lane = lax.broadcasted_iota(jnp.int32, (CHUNK, D_B), 1)
        gsz = D_B // N_GROUPS_B
        w_cat = jnp.concatenate(
            [jnp.where(row >= col, ws_ref[g], 0.0).astype(BF16) for g in range(N_GROUPS_B)], axis=1)
        pieces = []
        for j in range(tm // CHUNK):
            vc = vg[j * CHUNK:(j + 1) * CHUNK, :].astype(BF16)
            v_bd = jnp.concatenate(
                [jnp.where((lane >= g * gsz) & (lane < (g + 1) * gsz), vc, 0.0)
                 for g in range(N_GROUPS_B)], axis=0)
            pieces.append(_dot(w_cat, v_bd) + bs_ref[...])
        s = jnp.concatenate(pieces, axis=0) if len(pieces) > 1 else pieces[0]
    else:
        s = vg * ws_ref[...] + bs_ref[...]
    o_b = u_ref[0] * s
    y = _mm1(ga_ref[0] * _mm1(o_a, pa_ref[...]) + gb_ref[0] * _mm1(o_b, pb_ref[...]),
             wo_ref[...])
    x1 = _layer_norm(ALPHA * x_ref[0] + g1_ref[0] * y, l1g_ref[...], l1b_ref[...])
    x1_ref[0] = x1
    h2_ref[0] = x1 * (1.0 + sc2_ref[0]) + sh2_ref[0]


def _merge_stage(seq_mode, x, o, r, kp, v, u, vg, ga, gb, g1, sc2, sh2, p):
    b, t, d = x.shape
    tm = MERGE_TILE if seq_mode else t
    tmod = 1 if seq_mode else tm
    tok = lambda n: pl.BlockSpec((1, tm, n), lambda i, j: (i, j, 0))
    mod = pl.BlockSpec((1, tmod, d), lambda i, j: (i, j if not seq_mode else 0, 0))
    ws, bs = (p["w_spatial"], p["b_spatial_full"]) if seq_mode else (p["ws_row"], p["bs_row"])
    est = 2 * tm * (2 * d + 7 * D_A + 2 * d + 2 * d) * 4 + 8 * tm * d * 4 + 16 * MIB
    return pl.pallas_call(
        functools.partial(_merge_body, seq_mode, tm),
        grid=(b, t // tm),
        in_specs=[tok(d)] + [tok(D_A)] * 6 + [tok(d), tok(d), mod, mod, mod,
                  _const_spec((1, D_A)), _const_spec((1, D_A)), _const_spec((1, D_A)),
                  _const_spec((D_A, D_A)), _const_spec(ws.shape), _const_spec(bs.shape),
                  _const_spec((D_A, d)), _const_spec((D_B, d)), _const_spec((d, d)),
                  _const_spec((1, d)), _const_spec((1, d))],
        out_specs=[tok(d), tok(d)],
        out_shape=[jax.ShapeDtypeStruct((b, t, d), F32)] * 2,
        compiler_params=_params(("arbitrary", "arbitrary"), est),
        name="merge_seq" if seq_mode else "merge_row",
    )(x, o, r, kp, v, u, vg, ga, gb, g1, sc2, sh2, p["lnx_g"], p["lnx_b"], p["r_k"],
      p["bd"], ws, bs, p["w_branch_a"], p["w_branch_b"], p["w_out"], p["ln1_g"], p["ln1_b"])


def _route_body(h_ref, w_ref, b_ref, gate_ref, idx_ref, wts_ref, cnt_ref):
    lg = _mm3(h_ref[...], w_ref[...]) + b_ref[...]
    lane = lax.broadcasted_iota(jnp.int32, lg.shape, 1)
    lanef = lane.astype(F32)
    neg = -jnp.inf
    is_g = (lane >= N_EXPERTS) & (lane < N_EXPERTS + N_ROUTE_GROUPS)
    mg = jnp.max(jnp.where(is_g, lg, neg), axis=-1, keepdims=True)
    gidx = jnp.min(jnp.where(is_g & (lg == mg), lanef - N_EXPERTS, 1e9), axis=-1, keepdims=True)
    pg_sel = 1.0 / jnp.sum(jnp.where(is_g, jnp.exp(lg - mg), 0.0), axis=-1, keepdims=True)
    lo = gidx * EXP_PER_GROUP
    in_grp = (lanef >= lo) & (lanef < lo + EXP_PER_GROUP)
    t1 = jnp.max(jnp.where(in_grp, lg, neg), axis=-1, keepdims=True)
    i1 = jnp.min(jnp.where(in_grp & (lg == t1), lanef, 1e9), axis=-1, keepdims=True)
    rest = in_grp & (lanef != i1)
    t2 = jnp.max(jnp.where(rest, lg, neg), axis=-1, keepdims=True)
    i2 = jnp.min(jnp.where(rest & (lg == t2), lanef, 1e9), axis=-1, keepdims=True)
    e2 = jnp.exp(t2 - t1)
    w1 = pg_sel / (1.0 + e2)
    w2 = pg_sel * e2 / (1.0 + e2)
    gate_ref[...] = jnp.where(lanef == i1, w1, 0.0) + jnp.where(lanef == i2, w2, 0.0)
    tm = lg.shape[0]
    hit = (lanef == i1) | (lanef == i2)
    earlier = (lax.broadcasted_iota(jnp.int32, (tm, tm), 0)
               > lax.broadcasted_iota(jnp.int32, (tm, tm), 1))
    rank = _dot(earlier.astype(BF16), hit.astype(BF16))
    cnt = jnp.sum(hit.astype(F32), axis=0, keepdims=True)
    run_len = jnp.floor((cnt + (MOE_RUN_ALIGN - 1.0)) * (1.0 / MOE_RUN_ALIGN)) * MOE_RUN_ALIGN
    lower_expert = (lax.broadcasted_iota(jnp.int32, (LANES, LANES), 0)
                    < lax.broadcasted_iota(jnp.int32, (LANES, LANES), 1))
    run_start = _masked_rowsum_t(jnp.broadcast_to(run_len, (SUBLANES, LANES)), lower_expert)[0:1]
    lpos = run_start + rank
    lp1 = jnp.sum(jnp.where(lanef == i1, lpos, 0.0), axis=-1, keepdims=True)
    lp2 = jnp.sum(jnp.where(lanef == i2, lpos, 0.0), axis=-1, keepdims=True)
    cnt_ref[0] = cnt.astype(jnp.int32)
    idx = jnp.where(lane == 0, i1, jnp.where(lane == 1, i2, jnp.where(lane == 2, lp1, lp2)))
    idx_ref[...] = idx.astype(jnp.int32)
    wts_ref[...] = jnp.where(lane == 0, w1, w2)


def _route(h2, p):
    m, d = h2.shape
    tm = min(m, MOE_TILE)
    tok = pl.BlockSpec((tm, LANES), lambda i: (i, 0))
    return pl.pallas_call(
        _route_body,
        grid=(m // tm,),
        in_specs=[pl.BlockSpec((tm, d), lambda i: (i, 0)),
                  _const_spec((d, LANES)), _const_spec((1, LANES))],
        out_specs=[tok, tok, tok, pl.BlockSpec((1, 1, LANES), lambda i: (i, 0, 0))],
        out_shape=[jax.ShapeDtypeStruct((m, LANES), F32),
                   jax.ShapeDtypeStruct((m, LANES), jnp.int32),
                   jax.ShapeDtypeStruct((m, LANES), F32),
                   jax.ShapeDtypeStruct((m // tm, 1, LANES), jnp.int32)],
        compiler_params=_params(("arbitrary",), 24 * MIB),
        name="route",
    )(h2, p["w_route"], p["b_route"])


def _moe_body(h_ref, gate_ref, wg_ref, wu_ref, wd_ref, x1_ref, g2_ref, l2g_ref, l2b_ref,
              out_ref, acc_ref, xb_ref):
    e = pl.program_id(1)

    @pl.when(e == 0)
    def _():
        acc_ref[...] = jnp.zeros_like(acc_ref)
        xb_ref[...] = h_ref[...].astype(BF16)

    xb = xb_ref[...]
    pre = _dot(xb, wg_ref[0].astype(BF16))
    hid = pre * jax.nn.sigmoid(pre) * _dot(xb, wu_ref[0].astype(BF16))
    ye = _mm1(hid, wd_ref[0])
    gate = gate_ref[...]
    lane = lax.broadcasted_iota(jnp.int32, gate.shape, 1)
    ge = jnp.sum(jnp.where(lane == e, gate, 0.0), axis=-1, keepdims=True)
    acc_ref[...] += ge * ye

    @pl.when(e == pl.num_programs(1) - 1)
    def _():
        out_ref[...] = _layer_norm(ALPHA * x1_ref[...] + g2_ref[0] * acc_ref[...],
                                   l2g_ref[...], l2b_ref[...])


def _moe(seq_len, h2, gate, x1, g2, p):
    m, d = h2.shape
    layer = p["layer"]
    tm = min(seq_len if g2.shape[1] == 1 else m, DENSE_MOE_TILE)
    tok = lambda n: pl.BlockSpec((tm, n), lambda i, e: (i, 0))
    if g2.shape[1] == 1:
        tiles_per_seq = seq_len // tm
        g2_spec = pl.BlockSpec((1, 1, d), lambda i, e: (i // tiles_per_seq, 0, 0))
    else:
        g2_spec = pl.BlockSpec((1, tm, d), lambda i, e: (0, i, 0))
    est = 2 * tm * (3 * d + LANES) * 4 + tm * d * 6 + 4 * tm * d * 4 + 8 * MIB
    return pl.pallas_call(
        _moe_body,
        grid=(m // tm, N_EXPERTS),
        in_specs=[tok(d), tok(LANES),
                  pl.BlockSpec((None, 1, d, D_EXPERT), lambda i, e: (layer, e, 0, 0)),
                  pl.BlockSpec((None, 1, d, D_EXPERT), lambda i, e: (layer, e, 0, 0)),
                  pl.BlockSpec((None, 1, D_EXPERT, d), lambda i, e: (layer, e, 0, 0)),
                  tok(d), g2_spec, _const_spec((1, d)), _const_spec((1, d))],
        out_specs=tok(d),
        out_shape=jax.ShapeDtypeStruct((m, d), F32),
        scratch_shapes=[pltpu.VMEM((tm, d), F32), pltpu.VMEM((tm, d), BF16)],
        compiler_params=_params(("arbitrary", "arbitrary"), est),
        name="moe_dense",
    )(h2, gate, p["w_exp_gate"], p["w_exp_up"], p["w_exp_down"], x1, g2, p["ln2_g"], p["ln2_b"])


MOE_TILE = 512
MOE_BLOCK = 512
MOE_RUN_ALIGN = 16
MXU_DEPTH = 256
MOE_LOCAL_ROWS = -(-(2 * MOE_TILE + N_EXPERTS * (MOE_RUN_ALIGN - 1)) // MXU_DEPTH) * MXU_DEPTH
MOE_SLABS = tuple(MOE_TILE >> k for k in range((MOE_TILE // MOE_RUN_ALIGN).bit_length()))


def _for_each_slab(run_ref, make_copy, fn):
    def one_run(e, c):
        dst = run_ref[0, 0, e]
        src = run_ref[0, 0, N_EXPERTS + e]
        n = run_ref[0, 0, 2 * N_EXPERTS + e]
        for slab in MOE_SLABS:
            off = n & (-2 * slab)

            @pl.when((n & slab) != 0)
            def _():
                fn(make_copy(pl.multiple_of(dst + off, MOE_RUN_ALIGN),
                             pl.multiple_of(src + off, MOE_RUN_ALIGN), slab))
        return c

    lax.fori_loop(0, N_EXPERTS, one_run, 0)


def _dispatch_body(tail_ref, run_ref, prev_run_ref, h_ref, idx_ref, xs_hbm, zero_buf, loc_buf,
                   sem, zsem, usem):
    nb_max = xs_hbm.shape[0] // MOE_BLOCK
    nt = MOE_TILE

    def zero_copy(row0, zero_sem):
        return pltpu.make_async_copy(
            zero_buf, xs_hbm.at[pl.ds(pl.multiple_of(row0, MOE_BLOCK), MOE_BLOCK)], zero_sem)

    def each_unused_copy(fn):
        def unused(j, c):
            fn(zero_copy(j * MOE_BLOCK, usem))
            return c

        lax.fori_loop(tail_ref[N_EXPERTS], nb_max, unused, 0)

    @pl.when(pl.program_id(0) == 0)
    def _():
        zero_buf[...] = jnp.zeros_like(zero_buf)

        def each_tail_copy(fn):
            for e in range(N_EXPERTS):
                @pl.when(tail_ref[e] >= 0)
                def _():
                    fn(zero_copy(tail_ref[e], zsem))

        each_tail_copy(lambda cp: cp.start())
        each_unused_copy(lambda cp: cp.start())
        each_tail_copy(lambda cp: cp.wait())

    lp = idx_ref[...].astype(F32)
    eye = (lax.broadcasted_iota(jnp.int32, (nt, nt), 0)
           == lax.broadcasted_iota(jnp.int32, (nt, nt), 1))
    as_row = lambda col: jnp.sum(jnp.where(eye, col, 0.0), axis=0, keepdims=True)
    r = lax.broadcasted_iota(jnp.int32, (MOE_LOCAL_ROWS, nt), 0).astype(F32)
    pick = (r == as_row(lp[:, 2:3])) | (r == as_row(lp[:, 3:4]))
    step = pl.program_id(0)
    slot = step & 1
    loc_buf[slot] = _dot(pick.astype(BF16), h_ref[...].astype(BF16)).astype(BF16)

    def copies_from(which):
        def make_copy(dst, src, rows):
            return pltpu.make_async_copy(loc_buf.at[which, pl.ds(src, rows)],
                                         xs_hbm.at[pl.ds(dst, rows)], sem.at[which])
        return make_copy

    _for_each_slab(run_ref, copies_from(slot), lambda cp: cp.start())

    @pl.when(step > 0)
    def _():
        _for_each_slab(prev_run_ref, copies_from(1 - slot), lambda cp: cp.wait())

    @pl.when(step == pl.num_programs(0) - 1)
    def _():
        _for_each_slab(run_ref, copies_from(slot), lambda cp: cp.wait())
        each_unused_copy(lambda cp: cp.wait())


def _dispatch(h2, idx, runs, tail_start, n_rows):
    m, d = h2.shape
    nt = MOE_TILE
    run_spec = lambda at: pl.BlockSpec((1, 1, LANES), lambda i, tail: (at(i), 0, 0),
                                       memory_space=pltpu.SMEM)
    grid_spec = pltpu.PrefetchScalarGridSpec(
        num_scalar_prefetch=1,
        grid=(m // nt,),
        in_specs=[run_spec(lambda i: i), run_spec(lambda i: jnp.maximum(i - 1, 0)),
                  pl.BlockSpec((nt, d), lambda i, tail: (i, 0)),
                  pl.BlockSpec((nt, LANES), lambda i, tail: (i, 0))],
        out_specs=pl.BlockSpec(memory_space=pl.ANY),
        scratch_shapes=[pltpu.VMEM((MOE_BLOCK, d), BF16),
                        pltpu.VMEM((2, MOE_LOCAL_ROWS, d), BF16),
                        pltpu.SemaphoreType.DMA((2,)), pltpu.SemaphoreType.DMA,
                        pltpu.SemaphoreType.DMA],
    )
    return pl.pallas_call(
        _dispatch_body,
        grid_spec=grid_spec,
        out_shape=jax.ShapeDtypeStruct((n_rows, d), BF16),
        compiler_params=_params(("arbitrary",), 40 * MIB),
        name="moe_dispatch",
    )(tail_start, runs, runs, h2, idx)


GMM_RING = 3


def _gmm_body(layer, be_ref, nb_ref, first_ref, next_ref, wslot_ref,
              x_hbm, wg_hbm, wu_hbm, wd_hbm, y_ref,
              x_ring, wg_stage, wu_stage, wd_stage, wg_bf, wu_bf, wd_bf, sem, wsem):
    j = pl.program_id(0)
    n_live = nb_ref[0]

    def fetch(step):
        slot = step % GMM_RING
        return pltpu.make_async_copy(
            x_hbm.at[pl.ds(pl.multiple_of(step * MOE_BLOCK, MOE_BLOCK), MOE_BLOCK)],
            x_ring.at[slot], sem.at[slot])

    def weight_copies(expert, slot):
        return [pltpu.make_async_copy(hbm.at[layer, expert], stage.at[slot], wsem.at[slot, i])
                for i, (hbm, stage) in enumerate(((wg_hbm, wg_stage), (wu_hbm, wu_stage),
                                                  (wd_hbm, wd_stage)))]

    @pl.when(j == 0)
    def _():
        for cp in weight_copies(be_ref[0], 0):
            cp.start()
        for ahead in range(GMM_RING - 1):
            @pl.when(ahead < n_live)
            def _():
                fetch(ahead).start()

    @pl.when(j + (GMM_RING - 1) < n_live)
    def _():
        fetch(j + (GMM_RING - 1)).start()

    @pl.when(j < n_live)
    def _():
        @pl.when(first_ref[j] == 1)
        def _():
            slot = wslot_ref[j]
            for cp in weight_copies(be_ref[j], slot):
                cp.wait()
            wg_bf[...] = wg_stage[slot].astype(BF16)
            wu_bf[...] = wu_stage[slot].astype(BF16)
            wd_bf[...] = wd_stage[slot].astype(BF16)

            @pl.when(next_ref[j] >= 0)
            def _():
                for cp in weight_copies(next_ref[j], 1 - slot):
                    cp.start()

        fetch(j).wait()
        xb = x_ring[j % GMM_RING]
        pre = _dot(xb, wg_bf[...])
        hid = pre * jax.nn.sigmoid(pre) * _dot(xb, wu_bf[...])
        y_ref[...] = _dot(hid.astype(BF16), wd_bf[...]).astype(BF16)

    @pl.when(j >= nb_ref[0])
    def _():
        y_ref[...] = jnp.zeros_like(y_ref)


def _gmm(xs, blk_expert, n_blocks, run_first, run_next, run_slot, p):
    n_rows, d = xs.shape
    nb_max = n_rows // MOE_BLOCK
    any_spec = pl.BlockSpec(memory_space=pl.ANY)
    grid_spec = pltpu.PrefetchScalarGridSpec(
        num_scalar_prefetch=5,
        grid=(nb_max,),
        in_specs=[any_spec] * 4,
        out_specs=pl.BlockSpec((MOE_BLOCK, d), lambda j, *_: (j, 0)),
        scratch_shapes=[pltpu.VMEM((GMM_RING, MOE_BLOCK, d), BF16),
                        pltpu.VMEM((2, d, D_EXPERT), F32), pltpu.VMEM((2, d, D_EXPERT), F32),
                        pltpu.VMEM((2, D_EXPERT, d), F32),
                        pltpu.VMEM((d, D_EXPERT), BF16), pltpu.VMEM((d, D_EXPERT), BF16),
                        pltpu.VMEM((D_EXPERT, d), BF16),
                        pltpu.SemaphoreType.DMA((GMM_RING,)), pltpu.SemaphoreType.DMA((2, 3))],
    )
    return pl.pallas_call(
        functools.partial(_gmm_body, p["layer"]),
        grid_spec=grid_spec,
        out_shape=jax.ShapeDtypeStruct((n_rows, d), BF16),
        compiler_params=_params(("arbitrary",), 24 * MIB),
        name="moe_gmm",
    )(blk_expert, n_blocks, run_first, run_next, run_slot,
      xs, p["w_exp_gate"], p["w_exp_up"], p["w_exp_down"])


def _combine_body(run_ref, next_run_ref, ys_hbm, idx_ref, wts_ref, x1_ref, g2_ref, l2g_ref,
                  l2b_ref, out_ref, loc_buf, sem):
    nt = MOE_TILE
    step = pl.program_id(0)
    slot = step & 1

    def copies_into(which):
        def make_copy(dst, src, rows):
            return pltpu.make_async_copy(ys_hbm.at[pl.ds(dst, rows)],
                                         loc_buf.at[which, pl.ds(src, rows)], sem.at[which])
        return make_copy

    @pl.when(step == 0)
    def _():
        loc_buf[...] = jnp.zeros_like(loc_buf)
        _for_each_slab(run_ref, copies_into(slot), lambda cp: cp.start())

    @pl.when(step < pl.num_programs(0) - 1)
    def _():
        _for_each_slab(next_run_ref, copies_into(1 - slot), lambda cp: cp.start())

    _for_each_slab(run_ref, copies_into(slot), lambda cp: cp.wait())
    lp = idx_ref[...].astype(F32)
    w = wts_ref[...]
    c = lax.broadcasted_iota(jnp.int32, (nt, MOE_LOCAL_ROWS), 1).astype(F32)
    sel = (jnp.where(c == lp[:, 2:3], w[:, 0:1], 0.0)
           + jnp.where(c == lp[:, 3:4], w[:, 1:2], 0.0))
    moe = _dot(sel.astype(BF16), loc_buf[slot])
    out_ref[...] = _layer_norm(ALPHA * x1_ref[...] + g2_ref[0] * moe, l2g_ref[...], l2b_ref[...])


def _combine(seq_len, ys, idx, wts, runs, x1, g2, p):
    m, d = x1.shape
    nt = MOE_TILE
    tiles_per_seq = seq_len // nt
    tok = lambda n: pl.BlockSpec((nt, n), lambda i: (i, 0))
    n_tiles = m // nt
    run_spec = lambda at: pl.BlockSpec((1, 1, LANES), lambda i: (at(i), 0, 0),
                                       memory_space=pltpu.SMEM)
    return pl.pallas_call(
        _combine_body,
        grid=(n_tiles,),
        in_specs=[run_spec(lambda i: i), run_spec(lambda i: jnp.minimum(i + 1, n_tiles - 1)),
                  pl.BlockSpec(memory_space=pl.ANY), tok(LANES), tok(LANES), tok(d),
                  pl.BlockSpec((1, 1, d), lambda i: (i // tiles_per_seq, 0, 0)),
                  _const_spec((1, d)), _const_spec((1, d))],
        out_specs=tok(d),
        out_shape=jax.ShapeDtypeStruct((m, d), F32),
        scratch_shapes=[pltpu.VMEM((2, MOE_LOCAL_ROWS, d), BF16), pltpu.SemaphoreType.DMA((2,))],
        compiler_params=_params(("arbitrary",), 48 * MIB),
        name="moe_combine",
    )(runs, runs, ys, idx, wts, x1, g2, p["ln2_g"], p["ln2_b"])


def _moe_routed(seq_len, h2, idx, wts, tile_cnt, x1, g2, p):
    m, d = h2.shape
    blk = MOE_BLOCK
    n_tiles = m // MOE_TILE
    worst_rows = 2 * m + n_tiles * N_EXPERTS * (MOE_RUN_ALIGN - 1) + N_EXPERTS * (blk - 1)
    nb_max = -(-worst_rows // blk)
    cnt = tile_cnt[:, 0, :N_EXPERTS]
    run_len = ((cnt + MOE_RUN_ALIGN - 1) // MOE_RUN_ALIGN) * MOE_RUN_ALIGN
    local_row = jnp.cumsum(run_len, axis=1) - run_len
    rows_before = jnp.cumsum(run_len, axis=0) - run_len
    total = jnp.sum(run_len, axis=0)
    padded = ((total + blk - 1) // blk) * blk
    ends = jnp.cumsum(padded)
    starts = ends - padded
    n_blocks = (ends[-1] // blk).astype(jnp.int32).reshape(1)
    first_row = jnp.arange(nb_max, dtype=jnp.int32) * blk
    blk_expert = jnp.minimum(
        jnp.sum((first_row[:, None] >= ends[None, :]).astype(jnp.int32), axis=1),
        N_EXPERTS - 1).astype(jnp.int32)
    tail_start = jnp.concatenate(
        [jnp.where(padded > 0, ends - blk, -1).astype(jnp.int32), n_blocks])
    runs = jnp.concatenate(
        [starts[None, :] + rows_before, local_row, run_len,
         jnp.zeros((n_tiles, LANES - 3 * N_EXPERTS), jnp.int32)], axis=1).astype(jnp.int32)
    runs = runs.reshape(n_tiles, 1, LANES)
    prev_expert = jnp.concatenate([jnp.full((1,), -1, jnp.int32), blk_expert[:-1]])
    run_first = (blk_expert != prev_expert).astype(jnp.int32)
    run_slot = ((jnp.cumsum(run_first) - 1) % 2).astype(jnp.int32)
    used = padded > 0
    later = jnp.arange(N_EXPERTS)[None, :] > jnp.arange(N_EXPERTS)[:, None]
    next_used = jnp.min(jnp.where(later & used[None, :], jnp.arange(N_EXPERTS)[None, :], N_EXPERTS),
                        axis=1)
    run_next = jnp.where(next_used < N_EXPERTS, next_used, -1).astype(jnp.int32)[blk_expert]
    xs = _dispatch(h2, idx, runs, tail_start, nb_max * blk)
    ys = _gmm(xs, blk_expert, n_blocks, run_first, run_next, run_slot, p)
    return _combine(seq_len, ys, idx, wts, runs, x1, g2, p)


def _prep_layer(l, w_in, mu_shift, w0, w_decay_up, a0, w_iclr_up, k_k, k_a, r_k,
                lnx_g, lnx_b, lnv_g, lnv_b, w_spatial, b_spatial, w_branch_a, w_branch_b,
                w_out, ln1_g, ln1_b, w_route_group, b_route_group, w_route_expert,
                b_route_expert, w_exp_gate, w_exp_up, w_exp_down, ln2_g, ln2_b):
    d = D_MODEL
    pad_a = N_SHIFT_PAD - N_SHIFT
    wi = w_in[l]
    w_in_p = jnp.concatenate(
        [wi[:, :N_SHIFT], jnp.zeros((d, pad_a), F32), wi[:, N_SHIFT:]], axis=1).astype(BF16)
    mu = jnp.concatenate([mu_shift[l], jnp.zeros((pad_a,), F32)])[None]
    lora = jnp.zeros((LANES, 2 * D_A), F32)
    lora = lora.at[:R_LORA, :D_A].set(w_decay_up[l]).at[R_LORA:2 * R_LORA, D_A:].set(w_iclr_up[l])
    seg = jnp.arange(D_A) // HEAD
    row1 = lambda x: x.reshape(1, -1)
    gsz = D_B // N_GROUPS_B
    w_route = jnp.concatenate(
        [w_route_expert[l], w_route_group[l],
         jnp.zeros((d, LANES - N_EXPERTS - N_ROUTE_GROUPS), F32)], axis=1)
    b_route = jnp.concatenate(
        [b_route_expert[l], b_route_group[l],
         jnp.zeros((LANES - N_EXPERTS - N_ROUTE_GROUPS,), F32)])[None]
    return dict(
        w_in=w_in_p, mu=mu, lora=lora,
        w0a0=jnp.concatenate([w0[l], a0[l]])[None],
        k_k=row1(k_k[l]), k_a=row1(k_a[l]), r_k=row1(r_k[l]),
        lnx_g=row1(lnx_g[l]), lnx_b=row1(lnx_b[l]),
        lnv_g=row1(lnv_g[l]), lnv_b=row1(lnv_b[l]),
        bd=(seg[:, None] == seg[None, :]).astype(BF16),
        w_spatial=w_spatial[l],
        b_spatial_full=jnp.repeat(b_spatial[l].T, gsz, axis=1),
        ws_row=jnp.repeat(w_spatial[l][:, 0, 0], gsz)[None],
        bs_row=jnp.repeat(b_spatial[l][:, 0], gsz)[None],
        w_branch_a=w_branch_a[l].astype(BF16), w_branch_b=w_branch_b[l].astype(BF16),
        w_out=w_out[l].astype(BF16), ln1_g=row1(ln1_g[l]), ln1_b=row1(ln1_b[l]),
        w_route=w_route, b_route=b_route,
        w_exp_gate=w_exp_gate, w_exp_up=w_exp_up, w_exp_down=w_exp_down, layer=l,
        ln2_g=row1(ln2_g[l]), ln2_b=row1(ln2_b[l]),
    )


def _trunk(seq_mode, x, mods, wkv_in, shift_in, preps):
    b, t, d = x.shape
    wkv_out, shift_out, v_out = [], [], []
    wkv_acc = None if seq_mode else jnp.zeros(wkv_in.shape, F32)
    for l in range(DEPTH):
        p = preps[l]
        sh1, sc1, g1, sh2, sc2, g2 = mods[l]
        if seq_mode:
            zprev = jnp.zeros((b, 1, N_SHIFT_PAD), F32) if shift_in is None else shift_in[l]
        else:
            zprev = _matmul(shift_in[l], p["w_in"][:, :N_SHIFT_PAD]).reshape(b, t, N_SHIFT_PAD)
        r, lw, kp, v, kap, bb, u, vg, ga, gb, hl = _in_stage(seq_mode, x, sc1, sh1, zprev, p)
        if seq_mode:
            o, s_new = _wkv_seq(r, lw, kp, v, kap, bb, wkv_in[l])
        else:
            flat = lambda a: a.reshape(t, D_A)
            o, wkv_acc = _wkv_step(flat(r), flat(lw), flat(kp), flat(v), flat(kap), flat(bb),
                                   wkv_in[l], wkv_acc, l)
            o = o.reshape(b, t, D_A)
        x1, h2 = _merge_stage(seq_mode, x, o, r, kp, v, u, vg, ga, gb, g1, sc2, sh2, p)
        m = b * t
        h2f = h2.reshape(m, d)
        gate, idx, wts, tile_cnt = _route(h2f, p)
        if seq_mode:
            x = _moe_routed(t, h2f, idx, wts, tile_cnt, x1.reshape(m, d), g2, p)
        else:
            x = _moe(t, h2f, gate, x1.reshape(m, d), g2, p)
        x = x.reshape(b, t, d)
        if seq_mode:
            wkv_out.append(s_new)
        shift_out.append(hl)
        v_out.append(vg)
    if not seq_mode:
        wkv_out = wkv_acc
    return x, wkv_out, shift_out, v_out


def kernel(x_prompt, x_sample, c_prompt, c_sample, state_wkv, state_shift, w_ada, b_ada, w_in, mu_shift, w0, w_decay_up, a0, w_iclr_up, k_k, k_a, r_k, lnx_g, lnx_b, lnv_g, lnv_b, w_spatial, b_spatial, w_branch_a, w_branch_b, w_out, ln1_g, ln1_b, w_route_group, b_route_group, w_route_expert, b_route_expert, w_exp_gate, w_exp_up, w_exp_down, ln2_g, ln2_b):
    bp, tp, d = x_prompt.shape
    bs = x_sample.shape[0]
    layer_params = (w_in, mu_shift, w0, w_decay_up, a0, w_iclr_up, k_k, k_a, r_k, lnx_g,
                    lnx_b, lnv_g, lnv_b, w_spatial, b_spatial, w_branch_a, w_branch_b, w_out,
                    ln1_g, ln1_b, w_route_group, b_route_group, w_route_expert,
                    b_route_expert, w_exp_gate, w_exp_up, w_exp_down, ln2_g, ln2_b)
    preps = [_prep_layer(l, *layer_params) for l in range(DEPTH)]
    mod_all = _ada(jnp.concatenate([c_prompt, c_sample], axis=0), w_ada, b_ada)
    mods_p, mods_s = [], []
    for l in range(DEPTH):
        parts = jnp.split(mod_all[l], 6, axis=-1)
        mods_p.append([q[:bp].reshape(bp, 1, d) for q in parts])
        mods_s.append([q[bp:].reshape(1, bs, d) for q in parts])

    wkv0 = jnp.zeros((DEPTH, bp, N_HEADS, HEAD, HEAD), F32)
    y_p, wkv_p, shift_p, _ = _trunk(True, x_prompt, mods_p, wkv0, None, preps)
    y_s, wkv_s, shift_s, v_s = _trunk(False, x_sample.reshape(1, bs, d), mods_s, state_wkv,
                                      state_shift, preps)
    return (y_p,
            y_s.reshape(bs, 1, d),
            jnp.stack(wkv_p),
            jnp.stack([s.reshape(bp, d) for s in shift_p]),
            wkv_s,
            jnp.stack([s.reshape(bs, d) for s in shift_s]),
            jnp.stack([q.reshape(bs, 1, D_B) for q in v_s]))
```

```python
import functools

import jax
import jax.numpy as jnp
from jax import lax
from jax.experimental import pallas as pl
from jax.experimental.pallas import tpu as pltpu

F32 = jnp.float32
BF16 = jnp.bfloat16

D_MODEL = 1024
DEPTH = 2
HEAD = 64
N_HEADS = 8
D_A = N_HEADS * HEAD
R_LORA = 32
CHUNK = 128
N_GROUPS_B = 8
D_B = 512
N_SHIFT = 3 * D_A + 2 * R_LORA
N_ROUTE_GROUPS = 4
EXP_PER_GROUP = 8
N_EXPERTS = N_ROUTE_GROUPS * EXP_PER_GROUP
D_EXPERT = 256
ALPHA = (2 * DEPTH) ** 0.25
LN_EPS = 1e-5
GN_EPS = 64e-5

LANES = 128
SUBLANES = 8
MIB = 2 ** 20
N_SHIFT_PAD = 13 * LANES
COL_U = N_SHIFT_PAD
COL_VG = COL_U + D_B
COL_GA = COL_VG + D_B
COL_GB = COL_GA + D_MODEL
N_IN_PAD = COL_GB + D_MODEL
WKV_CHUNK = 64
WKV_TILE = 256
WKV_STEP_ROWS = 32
MERGE_TILE = 512
DENSE_MOE_TILE = 1024
ADA_COLS = 1024
IN_ROW_GROUPS = 2
IN_GROUP_ROWS = 256
VMEM_CAP_BYTES = 60000 * 1024

_NN = (((1,), (0,)), ((), ()))
_B_NT = (((2,), (2,)), ((0,), (0,)))
_B_NN = (((2,), (1,)), ((0,), (0,)))
_B_TN = (((1,), (1,)), ((0,), (0,)))


def _dot(a, b, dims=_NN):
    return lax.dot_general(a, b, dims, preferred_element_type=F32)


def _split2(x):
    hi = x.astype(BF16)
    lo = (x - hi.astype(F32)).astype(BF16)
    return hi, lo


def _mm1(a, b, dims=_NN):
    return _dot(a.astype(BF16), b.astype(BF16), dims)


def _mm3(a, b, dims=_NN):
    ah, al = _split2(a)
    bh, bl = _split2(b)
    return _dot(ah, bh, dims) + (_dot(ah, bl, dims) + _dot(al, bh, dims))


def _segsum(x, bd, two_terms=False):
    if not two_terms:
        return _dot(x.astype(BF16), bd)
    hi, lo = _split2(x)
    return _dot(hi, bd) + _dot(lo, bd)


def _layer_norm(x, g, b):
    mu = jnp.mean(x, axis=-1, keepdims=True)
    d = x - mu
    var = jnp.mean(d * d, axis=-1, keepdims=True)
    return d * lax.rsqrt(var + LN_EPS) * g + b


def _gelu(x):
    return 0.5 * x * (1.0 + lax.erf(x * 0.7071067811865476))


def _params(sem, est_bytes):
    limit = int(min(VMEM_CAP_BYTES, max(est_bytes, 16 * 1024 * 1024)))
    return pltpu.CompilerParams(dimension_semantics=sem, vmem_limit_bytes=limit)


def _const_spec(shape, single_buffer=False):
    nd = len(shape)
    if single_buffer:
        return pl.BlockSpec(shape, lambda *_: (0,) * nd, pipeline_mode=pl.Buffered(1))
    return pl.BlockSpec(shape, lambda *_: (0,) * nd)


def _ada_body(c_ref, w_ref, b_ref, o_ref):
    c = c_ref[...]
    s = c * jax.nn.sigmoid(c)
    o_ref[0] = _mm3(s, w_ref[0]) + b_ref[0]


def _ada(c_all, w_ada, b_ada):
    depth, d, n6 = w_ada.shape
    m = c_all.shape[0]
    tn = ADA_COLS
    return pl.pallas_call(
        _ada_body,
        grid=(depth, n6 // tn),
        in_specs=[
            pl.BlockSpec((m, d), lambda l, j: (0, 0)),
            pl.BlockSpec((1, d, tn), lambda l, j: (l, 0, j)),
            pl.BlockSpec((1, 1, tn), lambda l, j: (l, 0, j)),
        ],
        out_specs=pl.BlockSpec((1, m, tn), lambda l, j: (l, 0, j)),
        out_shape=jax.ShapeDtypeStruct((depth, m, n6), F32),
        compiler_params=_params(("arbitrary", "arbitrary"), 24 * MIB),
        name="ada_mod",
    )(c_all, w_ada, b_ada.reshape(depth, 1, n6))


def _mm_body(x_ref, w_ref, o_ref):
    o_ref[...] = _mm1(x_ref[...], w_ref[...])


def _matmul(x, w):
    m, k = x.shape
    n = w.shape[1]
    return pl.pallas_call(
        _mm_body,
        grid=(1,),
        in_specs=[_const_spec((m, k)), _const_spec((k, n))],
        out_specs=_const_spec((m, n)),
        out_shape=jax.ShapeDtypeStruct((m, n), F32),
        compiler_params=_params(("arbitrary",), 24 * MIB),
        name="shift_proj",
    )(x, w)


def _in_body(seq_mode, tm, x_ref, sc_ref, sh_ref, w_ref, mu_ref, zp_ref, lora_ref,
             w0a0_ref, kk_ref, ka_ref, lng_ref, lnb_ref, bd_ref,
             r_ref, lw_ref, kp_ref, v_ref, kap_ref, bb_ref, u_ref, vg_ref,
             ga_ref, gb_ref, hl_ref, carry_ref):
    nsplit = IN_ROW_GROUPS if seq_mode else 1
    rows = tm // nsplit
    if seq_mode:
        @pl.when(pl.program_id(1) == 0)
        def _():
            carry_ref[...] = zp_ref[0]

        carry = carry_ref[...]
    bd = bd_ref[...]
    for part in range(nsplit):
        sl = slice(part * rows, (part + 1) * rows)
        h = x_ref[0, sl, :] * (1.0 + sc_ref[0]) + sh_ref[0]
        hb = h.astype(BF16)
        proj = lambda lo, hi: _dot(hb, w_ref[:, lo:hi])
        za = proj(0, N_SHIFT_PAD)
        if seq_mode:
            row = lax.broadcasted_iota(jnp.int32, za.shape, 0)
            prev = jnp.where(row == 0, carry, pltpu.roll(za, 1, 0))
            carry = za[rows - 1:rows, :]
        else:
            prev = zp_ref[0]
        mix = za + mu_ref[...] * (prev - za)
        r = mix[:, 0:D_A]
        k = mix[:, D_A:2 * D_A]
        v = mix[:, 2 * D_A:3 * D_A]
        xwa = mix[:, 3 * D_A:N_SHIFT_PAD]
        lane = lax.broadcasted_iota(jnp.int32, xwa.shape, 1)
        lora_in = jnp.where(lane < R_LORA, jnp.tanh(xwa), xwa)
        pre = w0a0_ref[...] + _mm1(lora_in, lora_ref[...])
        yw = -pre[:, :D_A]
        softplus = jnp.maximum(yw, 0.0) + jnp.log1p(jnp.exp(-jnp.abs(yw)))
        lw = -jnp.exp(-softplus - 0.5)
        a = jax.nn.sigmoid(pre[:, D_A:])
        kk = k * kk_ref[...]
        kap = kk / jnp.maximum(jnp.sqrt(_segsum(kk * kk, bd)), 1e-12)
        r_ref[0, sl, :] = r
        lw_ref[0, sl, :] = lw
        kp_ref[0, sl, :] = k * (1.0 + (a - 1.0) * ka_ref[...])
        v_ref[0, sl, :] = v
        kap_ref[0, sl, :] = kap
        bb_ref[0, sl, :] = kap * a
        u_ref[0, sl, :] = _gelu(proj(COL_U, COL_VG)).astype(BF16)
        vg_ref[0, sl, :] = _layer_norm(_gelu(proj(COL_VG, COL_GA)), lng_ref[...], lnb_ref[...])
        ga_ref[0, sl, :] = jax.nn.sigmoid(proj(COL_GA, COL_GB)).astype(BF16)
        gb_ref[0, sl, :] = jax.nn.sigmoid(proj(COL_GB, N_IN_PAD)).astype(BF16)
    if seq_mode:
        carry_ref[...] = carry
        hl_ref[0] = h[rows - 1:rows, :]
    else:
        hl_ref[0] = h


def _in_stage(seq_mode, x, sc, sh, zprev, p):
    b, t, d = x.shape
    tm = IN_ROW_GROUPS * IN_GROUP_ROWS if seq_mode else t
    tmod = 1 if seq_mode else tm
    grid = (b, t // tm)
    tok = lambda n: pl.BlockSpec((1, tm, n), lambda i, j: (i, j, 0))
    mod = pl.BlockSpec((1, tmod, d), lambda i, j: (i, j if not seq_mode else 0, 0))
    zp_spec = (pl.BlockSpec((1, 1, N_SHIFT_PAD), lambda i, j: (i, 0, 0)) if seq_mode
               else tok(N_SHIFT_PAD))
    hl_spec = (pl.BlockSpec((1, 1, d), lambda i, j: (i, 0, 0)) if seq_mode else tok(d))
    hl_shape = (b, 1, d) if seq_mode else (b, t, d)
    out_cols = [D_A] * 6 + [D_B] * 2 + [d] * 2
    out_dtypes = [F32] * 6 + [BF16, F32] + [BF16] * 2
    est = (2 * tm * (d + N_SHIFT_PAD + sum(out_cols) + d) * 4 + 2 * d * N_IN_PAD * 2
           + 3 * tm * N_IN_PAD * 4 + 4 * MIB)
    outs = pl.pallas_call(
        functools.partial(_in_body, seq_mode, tm),
        grid=grid,
        in_specs=[tok(d), mod, mod,
                  _const_spec((d, N_IN_PAD), True), _const_spec((1, N_SHIFT_PAD)), zp_spec,
                  _const_spec((LANES, 2 * D_A)), _const_spec((1, 2 * D_A)),
                  _const_spec((1, D_A)), _const_spec((1, D_A)),
                  _const_spec((1, D_B)), _const_spec((1, D_B)),
                  _const_spec((D_A, D_A))],
        out_specs=[tok(n) for n in out_cols] + [hl_spec],
        out_shape=[jax.ShapeDtypeStruct((b, t, n), dt) for n, dt in zip(out_cols, out_dtypes)]
        + [jax.ShapeDtypeStruct(hl_shape, F32)],
        scratch_shapes=[pltpu.VMEM((1, N_SHIFT_PAD), F32)],
        compiler_params=_params(("arbitrary", "arbitrary"), est),
        name="in_stage_seq" if seq_mode else "in_stage_row",
    )(x, sc, sh, p["w_in"], p["mu"], zprev, p["lora"], p["w0a0"], p["k_k"], p["k_a"],
      p["lnv_g"], p["lnv_b"], p["bd"])
    return outs


def _chunk_pairs(x, nsub):
    c = WKV_CHUNK
    w = 2 * HEAD
    return jnp.stack([x[c * i:c * (i + 1), w * q:w * (q + 1)]
                      for i in range(nsub) for q in range(N_HEADS // 2)], axis=0)


def _masked_rowsum(mask_bf, x):
    h1, h2 = _split2(x)
    return _dot(mask_bf, h1) + _dot(mask_bf, h2)


def _masked_rowsum_t(x, mask):
    h1, h2 = _split2(x)
    m = mask.astype(BF16)
    return _dot(h1, m) + _dot(h2, m)


def _wkv_seq_body(nsub, r_ref, lw_ref, kp_ref, v_ref, kap_ref, bb_ref, s0_ref,
                  o_ref, sout_ref, s_scr):
    c = WKV_CHUNK
    tc = nsub * c
    npair = N_HEADS // 2

    @pl.when(pl.program_id(1) == 0)
    def _():
        for q in range(npair):
            s_scr[q] = jnp.concatenate([s0_ref[0, 2 * q], s0_ref[0, 2 * q + 1]], axis=-1)

    row_t = lax.broadcasted_iota(jnp.int32, (tc, tc), 0)
    col_t = lax.broadcasted_iota(jnp.int32, (tc, tc), 1)
    shift = c.bit_length() - 1
    same_chunk = (row_t >> shift) == (col_t >> shift)
    lw = lw_ref[0]
    g = _masked_rowsum((same_chunk & (row_t >= col_t)).astype(BF16), lw)
    g_end = _masked_rowsum(same_chunk.astype(BF16), lw)
    e_neg = jnp.exp(-g)
    e_end = jnp.exp(g_end - g)
    cp = functools.partial(_chunk_pairs, nsub=nsub)
    k = kp_ref[0]
    b = bb_ref[0]
    kap_t = cp(kap_ref[0] * jnp.exp(g - lw))
    b_t = cp(b * e_neg)
    k_t = cp(k * e_neg)
    r_t = cp(r_ref[0] * jnp.exp(g))
    b_e = cp(b * e_end)
    k_e = cp(k * e_end)
    vv = cp(v_ref[0])
    decay_end = cp(jnp.exp(g_end))

    row = lax.broadcasted_iota(jnp.int32, (c, 2 * c), 0)[None]
    lane = lax.broadcasted_iota(jnp.int32, (c, 2 * c), 1)[None]
    colp = lane & (c - 1)
    right = lane >= c
    row2 = lax.broadcasted_iota(jnp.int32, (2 * c, 2 * c), 0)[None]
    lane2 = lax.broadcasted_iota(jnp.int32, (2 * c, 2 * c), 1)[None]
    same_head = (row2 >= c) == (lane2 >= c)

    def bd(x):
        return jnp.concatenate([jnp.where(right, 0.0, x), jnp.where(right, x, 0.0)], axis=1)

    p_b = _mm1(jnp.concatenate([kap_t, r_t], axis=1), bd(b_t), _B_NT)
    l_b = jnp.where(row > colp, p_b[:, :c], 0.0)
    a_rb = jnp.where(row >= colp, p_b[:, c:], 0.0)
    bd_k = bd(k_t)
    l_k = jnp.where(row > colp, _mm1(kap_t, bd_k, _B_NT), 0.0)
    a_rk = jnp.where(row >= colp, _mm1(r_t, bd_k, _B_NT), 0.0)
    m = -l_b
    t_inv = jnp.where(row == colp, 1.0, 0.0) + m
    m = _mm1(m, bd(m), _B_NN)
    span = 2
    while 2 * span < c:
        both = _mm1(jnp.concatenate([m, t_inv], axis=1), bd(m), _B_NN)
        m = both[:, :c]
        t_inv = t_inv + both[:, c:]
        span *= 2
    t_inv = t_inv + _mm1(t_inv, bd(m), _B_NN)
    bd_v = bd(vv)
    a1 = -_mm1(t_inv, bd(kap_t), _B_NN)
    u0 = -_mm1(t_inv, bd(_mm1(l_k, bd_v, _B_NN)), _B_NN)
    a2 = r_t + _mm1(a_rb, bd(a1), _B_NN)
    o0 = _mm1(a_rb, bd(u0), _B_NN) + _mm1(a_rk, bd_v, _B_NN)
    g_bd = (jnp.where(same_head, _mm1(a1, b_e, _B_TN), 0.0)
            + jnp.where(row2 == lane2, decay_end[:, 0:1, :], 0.0))
    hh = _mm1(jnp.concatenate([u0, vv], axis=1),
              jnp.concatenate([b_e, k_e], axis=1), _B_TN)
    h_pair = jnp.where(right, hh[:, c:], hh[:, :c])

    s = s_scr[...]
    for i in range(nsub):
        ps = slice(npair * i, npair * (i + 1))
        o = _mm1(a2[ps], bd(s), _B_NT) + o0[ps]
        s = _mm1(s, g_bd[ps], _B_NN) + h_pair[ps]
        for q in range(npair):
            o_ref[0, c * i:c * (i + 1), 2 * HEAD * q:2 * HEAD * (q + 1)] = o[q]
    s_scr[...] = s

    @pl.when(pl.program_id(1) == pl.num_programs(1) - 1)
    def _():
        for q in range(npair):
            sout_ref[0, 2 * q] = s[q][:, :HEAD]
            sout_ref[0, 2 * q + 1] = s[q][:, HEAD:]


def _wkv_seq(r, lw, kp, v, kap, bb, s0):
    b, t, _ = r.shape
    tc = WKV_TILE
    nsub = tc // WKV_CHUNK
    tok = pl.BlockSpec((1, tc, D_A), lambda i, j: (i, j, 0))
    st = pl.BlockSpec((1, N_HEADS, HEAD, HEAD), lambda i, j: (i, 0, 0, 0))
    return pl.pallas_call(
        functools.partial(_wkv_seq_body, nsub),
        grid=(b, t // tc),
        in_specs=[tok] * 6 + [st],
        out_specs=[tok, st],
        out_shape=[jax.ShapeDtypeStruct((b, t, D_A), F32),
                   jax.ShapeDtypeStruct((b, N_HEADS, HEAD, HEAD), F32)],
        scratch_shapes=[pltpu.VMEM((N_HEADS // 2, HEAD, 2 * HEAD), F32)],
        compiler_params=_params(("arbitrary", "arbitrary"), 32 * MIB),
        name="wkv_seq",
    )(r, lw, kp, v, kap, bb, s0)


def _wkv_step_body(r_ref, lw_ref, kp_ref, v_ref, kap_ref, bb_ref, s0_ref, _, o_ref, sout_ref):
    for h in range(N_HEADS):
        hs = slice(HEAD * h, HEAD * (h + 1))
        row = lambda ref: ref[:, hs][:, None, :]
        s = s0_ref[:, h]
        sa = -jnp.sum(s * row(kap_ref), axis=-1, keepdims=True)
        v_col = v_ref[:, hs][:, :, None]
        s_new = s * jnp.exp(row(lw_ref)) + sa * row(bb_ref) + v_col * row(kp_ref)
        sout_ref[:, h] = s_new
        o_ref[:, hs] = jnp.sum(s_new * row(r_ref), axis=-1)


def _wkv_step(r, lw, kp, v, kap, bb, s0, s_new_all, layer):
    m = r.shape[0]
    nb = WKV_STEP_ROWS
    tok = pl.BlockSpec((nb, D_A), lambda i: (i, 0))
    st = pl.BlockSpec((nb, N_HEADS, HEAD, HEAD), lambda i: (i, 0, 0, 0))
    st_out = pl.BlockSpec((None, nb, N_HEADS, HEAD, HEAD), lambda i: (layer, i, 0, 0, 0))
    return pl.pallas_call(
        _wkv_step_body,
        grid=(m // nb,),
        in_specs=[tok] * 6 + [st, pl.BlockSpec(memory_space=pl.ANY)],
        out_specs=[tok, st_out],
        out_shape=[jax.ShapeDtypeStruct((m, D_A), F32),
                   jax.ShapeDtypeStruct(s_new_all.shape, F32)],
        input_output_aliases={7: 1},
        compiler_params=_params(("arbitrary",), 56 * MIB),
        name="wkv_step",
    )(r, lw, kp, v, kap, bb, s0, s_new_all)


def _merge_body(seq_mode, tm, x_ref, o_ref, r_ref, kp_ref, v_ref, u_ref, vg_ref,
                ga_ref, gb_ref, g1_ref, sc2_ref, sh2_ref, lxg_ref, lxb_ref, rk_ref,
                bd_ref, ws_ref, bs_ref, pa_ref, pb_ref, wo_ref, l1g_ref, l1b_ref,
                x1_ref, h2_ref):
    bd = bd_ref[...]
    o = o_ref[0]
    inv_n = 1.0 / HEAD
    mu = _segsum(o, bd, two_terms=True) * inv_n
    d = o - mu
    var = _segsum(d * d, bd) * inv_n
    on = d * lax.rsqrt(var + GN_EPS) * lxg_ref[...] + lxb_ref[...]
    v = v_ref[0]
    o_a = on + _segsum(r_ref[0] * kp_ref[0] * rk_ref[...], bd) * v
    vg = vg_ref[0]
    if seq_mode:
        row = lax.broadcasted_iota(jnp.int32, (CHUNK, CHUNK), 0)
        col = lax.broadcasted_iota(jnp.int32, (CHUNK, CHUNK), 1)
        lane = lax.broadcasted_iota(jnp.int32, (CHUNK, D_B), 1)
        gsz = D_B // N_GROUPS_B
        w_cat = jnp.concatenate(
            [jnp.where(row >= col, ws_ref[g], 0.0).astype(BF16) for g in range(N_GROUPS_B)], axis=1)
        pieces = []
        for j in range(tm // CHUNK):
            vc = vg[j * CHUNK:(j + 1) * CHUNK, :].astype(BF16)
            v_bd = jnp.concatenate(
                [jnp.where((lane >= g * gsz) & (lane < (g + 1) * gsz), vc, 0.0)
                 for g in range(N_GROUPS_B)], axis=0)
            pieces.append(_dot(w_cat, v_bd) + bs_ref[...])
        s = jnp.concatenate(pieces, axis=0) if len(pieces) > 1 else pieces[0]
    else:
        s = vg * ws_ref[...] + bs_ref[...]
    o_b = u_ref[0] * s
    y = _mm1(ga_ref[0] * _mm1(o_a, pa_ref[...]) + gb_ref[0] * _mm1(o_b, pb_ref[...]),
             wo_ref[...])
    x1 = _layer_norm(ALPHA * x_ref[0] + g1_ref[0] * y, l1g_ref[...], l1b_ref[...])
    x1_ref[0] = x1
    h2_ref[0] = x1 * (1.0 + sc2_ref[0]) + sh2_ref[0]


def _merge_stage(seq_mode, x, o, r, kp, v, u, vg, ga, gb, g1, sc2, sh2, p):
    b, t, d = x.shape
    tm = MERGE_TILE if seq_mode else t
    tmod = 1 if seq_mode else tm
    tok = lambda n: pl.BlockSpec((1, tm, n), lambda i, j: (i, j, 0))
    mod = pl.BlockSpec((1, tmod, d), lambda i, j: (i, j if not seq_mode else 0, 0))
    ws, bs = (p["w_spatial"], p["b_spatial_full"]) if seq_mode else (p["ws_row"], p["bs_row"])
    est = 2 * tm * (2 * d + 7 * D_A + 2 * d + 2 * d) * 4 + 8 * tm * d * 4 + 16 * MIB
    return pl.pallas_call(
        functools.partial(_merge_body, seq_mode, tm),
        grid=(b, t // tm),
        in_specs=[tok(d)] + [tok(D_A)] * 6 + [tok(d), tok(d), mod, mod, mod,
                  _const_spec((1, D_A)), _const_spec((1, D_A)), _const_spec((1, D_A)),
                  _const_spec((D_A, D_A)), _const_spec(ws.shape), _const_spec(bs.shape),
                  _const_spec((D_A, d)), _const_spec((D_B, d)), _const_spec((d, d)),
                  _const_spec((1, d)), _const_spec((1, d))],
        out_specs=[tok(d), tok(d)],
        out_shape=[jax.ShapeDtypeStruct((b, t, d), F32)] * 2,
        compiler_params=_params(("arbitrary", "arbitrary"), est),
        name="merge_seq" if seq_mode else "merge_row",
    )(x, o, r, kp, v, u, vg, ga, gb, g1, sc2, sh2, p["lnx_g"], p["lnx_b"], p["r_k"],
      p["bd"], ws, bs, p["w_branch_a"], p["w_branch_b"], p["w_out"], p["ln1_g"], p["ln1_b"])


def _route_body(h_ref, w_ref, b_ref, gate_ref, idx_ref, wts_ref, cnt_ref):
    lg = _mm3(h_ref[...], w_ref[...]) + b_ref[...]
    lane = lax.broadcasted_iota(jnp.int32, lg.shape, 1)
    lanef = lane.astype(F32)
    neg = -jnp.inf
    is_g = (lane >= N_EXPERTS) & (lane < N_EXPERTS + N_ROUTE_GROUPS)
    mg = jnp.max(jnp.where(is_g, lg, neg), axis=-1, keepdims=True)
    gidx = jnp.min(jnp.where(is_g & (lg == mg), lanef - N_EXPERTS, 1e9), axis=-1, keepdims=True)
    pg_sel = 1.0 / jnp.sum(jnp.where(is_g, jnp.exp(lg - mg), 0.0), axis=-1, keepdims=True)
    lo = gidx * EXP_PER_GROUP
    in_grp = (lanef >= lo) & (lanef < lo + EXP_PER_GROUP)
    t1 = jnp.max(jnp.where(in_grp, lg, neg), axis=-1, keepdims=True)
    i1 = jnp.min(jnp.where(in_grp & (lg == t1), lanef, 1e9), axis=-1, keepdims=True)
    rest = in_grp & (lanef != i1)
    t2 = jnp.max(jnp.where(rest, lg, neg), axis=-1, keepdims=True)
    i2 = jnp.min(jnp.where(rest & (lg == t2), lanef, 1e9), axis=-1, keepdims=True)
    e2 = jnp.exp(t2 - t1)
    w1 = pg_sel / (1.0 + e2)
    w2 = pg_sel * e2 / (1.0 + e2)
    gate_ref[...] = jnp.where(lanef == i1, w1, 0.0) + jnp.where(lanef == i2, w2, 0.0)
    tm = lg.shape[0]
    hit = (lanef == i1) | (lanef == i2)
    earlier = (lax.broadcasted_iota(jnp.int32, (tm, tm), 0)
               > lax.broadcasted_iota(jnp.int32, (tm, tm), 1))
    rank = _dot(earlier.astype(BF16), hit.astype(BF16))
    cnt = jnp.sum(hit.astype(F32), axis=0, keepdims=True)
    run_len = jnp.floor((cnt + (MOE_RUN_ALIGN - 1.0)) * (1.0 / MOE_RUN_ALIGN)) * MOE_RUN_ALIGN
    lower_expert = (lax.broadcasted_iota(jnp.int32, (LANES, LANES), 0)
                    < lax.broadcasted_iota(jnp.int32, (LANES, LANES), 1))
    run_start = _masked_rowsum_t(jnp.broadcast_to(run_len, (SUBLANES, LANES)), lower_expert)[0:1]
    lpos = run_start + rank
    lp1 = jnp.sum(jnp.where(lanef == i1, lpos, 0.0), axis=-1, keepdims=True)
    lp2 = jnp.sum(jnp.where(lanef == i2, lpos, 0.0), axis=-1, keepdims=True)
    cnt_ref[0] = cnt.astype(jnp.int32)
    idx = jnp.where(lane == 0, i1, jnp.where(lane == 1, i2, jnp.where(lane == 2, lp1, lp2)))
    idx_ref[...] = idx.astype(jnp.int32)
    wts_ref[...] = jnp.where(lane == 0, w1, w2)


def _route(h2, p):
    m, d = h2.shape
    tm = min(m, MOE_TILE)
    tok = pl.BlockSpec((tm, LANES), lambda i: (i, 0))
    return pl.pallas_call(
        _route_body,
        grid=(m // tm,),
        in_specs=[pl.BlockSpec((tm, d), lambda i: (i, 0)),
                  _const_spec((d, LANES)), _const_spec((1, LANES))],
        out_specs=[tok, tok, tok, pl.BlockSpec((1, 1, LANES), lambda i: (i, 0, 0))],
        out_shape=[jax.ShapeDtypeStruct((m, LANES), F32),
                   jax.ShapeDtypeStruct((m, LANES), jnp.int32),
                   jax.ShapeDtypeStruct((m, LANES), F32),
                   jax.ShapeDtypeStruct((m // tm, 1, LANES), jnp.int32)],
        compiler_params=_params(("arbitrary",), 24 * MIB),
        name="route",
    )(h2, p["w_route"], p["b_route"])


def _moe_body(h_ref, gate_ref, wg_ref, wu_ref, wd_ref, x1_ref, g2_ref, l2g_ref, l2b_ref,
              out_ref, acc_ref, xb_ref):
    e = pl.program_id(1)

    @pl.when(e == 0)
    def _():
        acc_ref[...] = jnp.zeros_like(acc_ref)
        xb_ref[...] = h_ref[...].astype(BF16)

    xb = xb_ref[...]
    pre = _dot(xb, wg_ref[0].astype(BF16))
    hid = pre * jax.nn.sigmoid(pre) * _dot(xb, wu_ref[0].astype(BF16))
    ye = _mm1(hid, wd_ref[0])
    gate = gate_ref[...]
    lane = lax.broadcasted_iota(jnp.int32, gate.shape, 1)
    ge = jnp.sum(jnp.where(lane == e, gate, 0.0), axis=-1, keepdims=True)
    acc_ref[...] += ge * ye

    @pl.when(e == pl.num_programs(1) - 1)
    def _():
        out_ref[...] = _layer_norm(ALPHA * x1_ref[...] + g2_ref[0] * acc_ref[...],
                                   l2g_ref[...], l2b_ref[...])


def _moe(seq_len, h2, gate, x1, g2, p):
    m, d = h2.shape
    layer = p["layer"]
    tm = min(seq_len if g2.shape[1] == 1 else m, DENSE_MOE_TILE)
    tok = lambda n: pl.BlockSpec((tm, n), lambda i, e: (i, 0))
    if g2.shape[1] == 1:
        tiles_per_seq = seq_len // tm
        g2_spec = pl.BlockSpec((1, 1, d), lambda i, e: (i // tiles_per_seq, 0, 0))
    else:
        g2_spec = pl.BlockSpec((1, tm, d), lambda i, e: (0, i, 0))
    est = 2 * tm * (3 * d + LANES) * 4 + tm * d * 6 + 4 * tm * d * 4 + 8 * MIB
    return pl.pallas_call(
        _moe_body,
        grid=(m // tm, N_EXPERTS),
        in_specs=[tok(d), tok(LANES),
                  pl.BlockSpec((None, 1, d, D_EXPERT), lambda i, e: (layer, e, 0, 0)),
                  pl.BlockSpec((None, 1, d, D_EXPERT), lambda i, e: (layer, e, 0, 0)),
                  pl.BlockSpec((None, 1, D_EXPERT, d), lambda i, e: (layer, e, 0, 0)),
                  tok(d), g2_spec, _const_spec((1, d)), _const_spec((1, d))],
        out_specs=tok(d),
        out_shape=jax.ShapeDtypeStruct((m, d), F32),
        scratch_shapes=[pltpu.VMEM((tm, d), F32), pltpu.VMEM((tm, d), BF16)],
        compiler_params=_params(("arbitrary", "arbitrary"), est),
        name="moe_dense",
    )(h2, gate, p["w_exp_gate"], p["w_exp_up"], p["w_exp_down"], x1, g2, p["ln2_g"], p["ln2_b"])


MOE_TILE = 512
MOE_BLOCK = 512
MOE_RUN_ALIGN = 16
MXU_DEPTH = 256
MOE_LOCAL_ROWS = -(-(2 * MOE_TILE + N_EXPERTS * (MOE_RUN_ALIGN - 1)) // MXU_DEPTH) * MXU_DEPTH
MOE_SLABS = tuple(MOE_TILE >> k for k in range((MOE_TILE // MOE_RUN_ALIGN).bit_length()))


N_DMA_THREADS = 2
_START_SLAB = lambda cp, thread: cp.start(priority=thread)
_WAIT_SLAB = lambda cp, thread: cp.wait()


def _for_each_slab(run_ref, make_copy, fn):
    def one_run(e, c):
        dst = run_ref[0, 0, e]
        src = run_ref[0, 0, N_EXPERTS + e]
        n = run_ref[0, 0, 2 * N_EXPERTS + e]
        for k, slab in enumerate(MOE_SLABS):
            off = n & (-2 * slab)

            @pl.when((n & slab) != 0)
            def _():
                fn(make_copy(pl.multiple_of(dst + off, MOE_RUN_ALIGN),
                             pl.multiple_of(src + off, MOE_RUN_ALIGN), slab), k % N_DMA_THREADS)
        return c

    lax.fori_loop(0, N_EXPERTS, one_run, 0)


def _dispatch_body(tail_ref, run_ref, prev_run_ref, h_ref, idx_ref, xs_hbm, zero_buf, loc_buf,
                   sem, zsem, usem):
    nb_max = xs_hbm.shape[0] // MOE_BLOCK
    nt = MOE_TILE

    def zero_copy(row0, zero_sem):
        return pltpu.make_async_copy(
            zero_buf, xs_hbm.at[pl.ds(pl.multiple_of(row0, MOE_BLOCK), MOE_BLOCK)], zero_sem)

    def each_unused_copy(fn):
        def unused(j, c):
            fn(zero_copy(j * MOE_BLOCK, usem))
            return c

        lax.fori_loop(tail_ref[N_EXPERTS], nb_max, unused, 0)

    @pl.when(pl.program_id(0) == 0)
    def _():
        zero_buf[...] = jnp.zeros_like(zero_buf)

        def each_tail_copy(fn):
            for e in range(N_EXPERTS):
                @pl.when(tail_ref[e] >= 0)
                def _():
                    fn(zero_copy(tail_ref[e], zsem))

        each_tail_copy(lambda cp: cp.start())
        each_unused_copy(lambda cp: cp.start())
        each_tail_copy(lambda cp: cp.wait())

    lp = idx_ref[...].astype(F32)
    eye = (lax.broadcasted_iota(jnp.int32, (nt, nt), 0)
           == lax.broadcasted_iota(jnp.int32, (nt, nt), 1))
    as_row = lambda col: jnp.sum(jnp.where(eye, col, 0.0), axis=0, keepdims=True)
    r = lax.broadcasted_iota(jnp.int32, (MOE_LOCAL_ROWS, nt), 0).astype(F32)
    pick = (r == as_row(lp[:, 2:3])) | (r == as_row(lp[:, 3:4]))
    step = pl.program_id(0)
    slot = step & 1
    loc_buf[slot] = _dot(pick.astype(BF16), h_ref[...].astype(BF16)).astype(BF16)

    def copies_from(which):
        def make_copy(dst, src, rows):
            return pltpu.make_async_copy(loc_buf.at[which, pl.ds(src, rows)],
                                         xs_hbm.at[pl.ds(dst, rows)], sem.at[which])
        return make_copy

    _for_each_slab(run_ref, copies_from(slot), _START_SLAB)

    @pl.when(step > 0)
    def _():
        _for_each_slab(prev_run_ref, copies_from(1 - slot), _WAIT_SLAB)

    @pl.when(step == pl.num_programs(0) - 1)
    def _():
        _for_each_slab(run_ref, copies_from(slot), _WAIT_SLAB)
        each_unused_copy(lambda cp: cp.wait())


def _dispatch(h2, idx, runs, tail_start, n_rows):
    m, d = h2.shape
    nt = MOE_TILE
    run_spec = lambda at: pl.BlockSpec((1, 1, LANES), lambda i, tail: (at(i), 0, 0),
                                       memory_space=pltpu.SMEM)
    grid_spec = pltpu.PrefetchScalarGridSpec(
        num_scalar_prefetch=1,
        grid=(m // nt,),
        in_specs=[run_spec(lambda i: i), run_spec(lambda i: jnp.maximum(i - 1, 0)),
                  pl.BlockSpec((nt, d), lambda i, tail: (i, 0)),
                  pl.BlockSpec((nt, LANES), lambda i, tail: (i, 0))],
        out_specs=pl.BlockSpec(memory_space=pl.ANY),
        scratch_shapes=[pltpu.VMEM((MOE_BLOCK, d), BF16),
                        pltpu.VMEM((2, MOE_LOCAL_ROWS, d), BF16),
                        pltpu.SemaphoreType.DMA((2,)), pltpu.SemaphoreType.DMA,
                        pltpu.SemaphoreType.DMA],
    )
    return pl.pallas_call(
        _dispatch_body,
        grid_spec=grid_spec,
        out_shape=jax.ShapeDtypeStruct((n_rows, d), BF16),
        compiler_params=_params(("arbitrary",), 40 * MIB),
        name="moe_dispatch",
    )(tail_start, runs, runs, h2, idx)


GMM_RING = 3


def _gmm_body(layer, be_ref, nb_ref, first_ref, next_ref, wslot_ref,
              x_hbm, wg_hbm, wu_hbm, wd_hbm, y_ref,
              x_ring, wg_stage, wu_stage, wd_stage, wg_bf, wu_bf, wd_bf, sem, wsem):
    j = pl.program_id(0)
    n_live = nb_ref[0]

    def fetch(step):
        slot = step % GMM_RING
        return pltpu.make_async_copy(
            x_hbm.at[pl.ds(pl.multiple_of(step * MOE_BLOCK, MOE_BLOCK), MOE_BLOCK)],
            x_ring.at[slot], sem.at[slot])

    def weight_copies(expert, slot):
        return [pltpu.make_async_copy(hbm.at[layer, expert], stage.at[slot], wsem.at[slot, i])
                for i, (hbm, stage) in enumerate(((wg_hbm, wg_stage), (wu_hbm, wu_stage),
                                                  (wd_hbm, wd_stage)))]

    @pl.when(j == 0)
    def _():
        for cp in weight_copies(be_ref[0], 0):
            cp.start()
        for ahead in range(GMM_RING - 1):
            @pl.when(ahead < n_live)
            def _():
                fetch(ahead).start()

    @pl.when(j + (GMM_RING - 1) < n_live)
    def _():
        fetch(j + (GMM_RING - 1)).start()

    @pl.when(j < n_live)
    def _():
        @pl.when(first_ref[j] == 1)
        def _():
            slot = wslot_ref[j]
            for cp in weight_copies(be_ref[j], slot):
                cp.wait()
            wg_bf[...] = wg_stage[slot].astype(BF16)
            wu_bf[...] = wu_stage[slot].astype(BF16)
            wd_bf[...] = wd_stage[slot].astype(BF16)

            @pl.when(next_ref[j] >= 0)
            def _():
                for cp in weight_copies(next_ref[j], 1 - slot):
                    cp.start()

        fetch(j).wait()
        xb = x_ring[j % GMM_RING]
        pre = _dot(xb, wg_bf[...])
        hid = pre * jax.nn.sigmoid(pre) * _dot(xb, wu_bf[...])
        y_ref[...] = _dot(hid.astype(BF16), wd_bf[...]).astype(BF16)

    @pl.when(j >= nb_ref[0])
    def _():
        y_ref[...] = jnp.zeros_like(y_ref)


def _gmm(xs, blk_expert, n_blocks, run_first, run_next, run_slot, p):
    n_rows, d = xs.shape
    nb_max = n_rows // MOE_BLOCK
    any_spec = pl.BlockSpec(memory_space=pl.ANY)
    grid_spec = pltpu.PrefetchScalarGridSpec(
        num_scalar_prefetch=5,
        grid=(nb_max,),
        in_specs=[any_spec] * 4,
        out_specs=pl.BlockSpec((MOE_BLOCK, d), lambda j, *_: (j, 0)),
        scratch_shapes=[pltpu.VMEM((GMM_RING, MOE_BLOCK, d), BF16),
                        pltpu.VMEM((2, d, D_EXPERT), F32), pltpu.VMEM((2, d, D_EXPERT), F32),
                        pltpu.VMEM((2, D_EXPERT, d), F32),
                        pltpu.VMEM((d, D_EXPERT), BF16), pltpu.VMEM((d, D_EXPERT), BF16),
                        pltpu.VMEM((D_EXPERT, d), BF16),
                        pltpu.SemaphoreType.DMA((GMM_RING,)), pltpu.SemaphoreType.DMA((2, 3))],
    )
    return pl.pallas_call(
        functools.partial(_gmm_body, p["layer"]),
        grid_spec=grid_spec,
        out_shape=jax.ShapeDtypeStruct((n_rows, d), BF16),
        compiler_params=_params(("arbitrary",), 24 * MIB),
        name="moe_gmm",
    )(blk_expert, n_blocks, run_first, run_next, run_slot,
      xs, p["w_exp_gate"], p["w_exp_up"], p["w_exp_down"])


def _combine_body(run_ref, next_run_ref, ys_hbm, idx_ref, wts_ref, x1_ref, g2_ref, l2g_ref,
                  l2b_ref, out_ref, loc_buf, sem):
    nt = MOE_TILE
    step = pl.program_id(0)
    slot = step & 1

    def copies_into(which):
        def make_copy(dst, src, rows):
            return pltpu.make_async_copy(ys_hbm.at[pl.ds(dst, rows)],
                                         loc_buf.at[which, pl.ds(src, rows)], sem.at[which])
        return make_copy

    @pl.when(step == 0)
    def _():
        loc_buf[...] = jnp.zeros_like(loc_buf)
        _for_each_slab(run_ref, copies_into(slot), _START_SLAB)

    @pl.when(step < pl.num_programs(0) - 1)
    def _():
        _for_each_slab(next_run_ref, copies_into(1 - slot), _START_SLAB)

    _for_each_slab(run_ref, copies_into(slot), _WAIT_SLAB)
    lp = idx_ref[...].astype(F32)
    w = wts_ref[...]
    c = lax.broadcasted_iota(jnp.int32, (nt, MOE_LOCAL_ROWS), 1).astype(F32)
    sel = (jnp.where(c == lp[:, 2:3], w[:, 0:1], 0.0)
           + jnp.where(c == lp[:, 3:4], w[:, 1:2], 0.0))
    moe = _dot(sel.astype(BF16), loc_buf[slot])
    out_ref[...] = _layer_norm(ALPHA * x1_ref[...] + g2_ref[0] * moe, l2g_ref[...], l2b_ref[...])


def _combine(seq_len, ys, idx, wts, runs, x1, g2, p):
    m, d = x1.shape
    nt = MOE_TILE
    tiles_per_seq = seq_len // nt
    tok = lambda n: pl.BlockSpec((nt, n), lambda i: (i, 0))
    n_tiles = m // nt
    run_spec = lambda at: pl.BlockSpec((1, 1, LANES), lambda i: (at(i), 0, 0),
                                       memory_space=pltpu.SMEM)
    return pl.pallas_call(
        _combine_body,
        grid=(n_tiles,),
        in_specs=[run_spec(lambda i: i), run_spec(lambda i: jnp.minimum(i + 1, n_tiles - 1)),
                  pl.BlockSpec(memory_space=pl.ANY), tok(LANES), tok(LANES), tok(d),
                  pl.BlockSpec((1, 1, d), lambda i: (i // tiles_per_seq, 0, 0)),
                  _const_spec((1, d)), _const_spec((1, d))],
        out_specs=tok(d),
        out_shape=jax.ShapeDtypeStruct((m, d), F32),
        scratch_shapes=[pltpu.VMEM((2, MOE_LOCAL_ROWS, d), BF16), pltpu.SemaphoreType.DMA((2,))],
        compiler_params=_params(("arbitrary",), 48 * MIB),
        name="moe_combine",
    )(runs, runs, ys, idx, wts, x1, g2, p["ln2_g"], p["ln2_b"])


def _moe_routed(seq_len, h2, idx, wts, tile_cnt, x1, g2, p):
    m, d = h2.shape
    blk = MOE_BLOCK
    n_tiles = m // MOE_TILE
    worst_rows = 2 * m + n_tiles * N_EXPERTS * (MOE_RUN_ALIGN - 1) + N_EXPERTS * (blk - 1)
    nb_max = -(-worst_rows // blk)
    cnt = tile_cnt[:, 0, :N_EXPERTS]
    run_len = ((cnt + MOE_RUN_ALIGN - 1) // MOE_RUN_ALIGN) * MOE_RUN_ALIGN
    local_row = jnp.cumsum(run_len, axis=1) - run_len
    rows_before = jnp.cumsum(run_len, axis=0) - run_len
    total = jnp.sum(run_len, axis=0)
    padded = ((total + blk - 1) // blk) * blk
    ends = jnp.cumsum(padded)
    starts = ends - padded
    n_blocks = (ends[-1] // blk).astype(jnp.int32).reshape(1)
    first_row = jnp.arange(nb_max, dtype=jnp.int32) * blk
    blk_expert = jnp.minimum(
        jnp.sum((first_row[:, None] >= ends[None, :]).astype(jnp.int32), axis=1),
        N_EXPERTS - 1).astype(jnp.int32)
    tail_start = jnp.concatenate(
        [jnp.where(padded > 0, ends - blk, -1).astype(jnp.int32), n_blocks])
    runs = jnp.concatenate(
        [starts[None, :] + rows_before, local_row, run_len,
         jnp.zeros((n_tiles, LANES - 3 * N_EXPERTS), jnp.int32)], axis=1).astype(jnp.int32)
    runs = runs.reshape(n_tiles, 1, LANES)
    prev_expert = jnp.concatenate([jnp.full((1,), -1, jnp.int32), blk_expert[:-1]])
    run_first = (blk_expert != prev_expert).astype(jnp.int32)
    run_slot = ((jnp.cumsum(run_first) - 1) % 2).astype(jnp.int32)
    used = padded > 0
    later = jnp.arange(N_EXPERTS)[None, :] > jnp.arange(N_EXPERTS)[:, None]
    next_used = jnp.min(jnp.where(later & used[None, :], jnp.arange(N_EXPERTS)[None, :], N_EXPERTS),
                        axis=1)
    run_next = jnp.where(next_used < N_EXPERTS, next_used, -1).astype(jnp.int32)[blk_expert]
    xs = _dispatch(h2, idx, runs, tail_start, nb_max * blk)
    ys = _gmm(xs, blk_expert, n_blocks, run_first, run_next, run_slot, p)
    return _combine(seq_len, ys, idx, wts, runs, x1, g2, p)


def _prep_layer(l, w_in, mu_shift, w0, w_decay_up, a0, w_iclr_up, k_k, k_a, r_k,
                lnx_g, lnx_b, lnv_g, lnv_b, w_spatial, b_spatial, w_branch_a, w_branch_b,
                w_out, ln1_g, ln1_b, w_route_group, b_route_group, w_route_expert,
                b_route_expert, w_exp_gate, w_exp_up, w_exp_down, ln2_g, ln2_b):
    d = D_MODEL
    pad_a = N_SHIFT_PAD - N_SHIFT
    wi = w_in[l]
    w_in_p = jnp.concatenate(
        [wi[:, :N_SHIFT], jnp.zeros((d, pad_a), F32), wi[:, N_SHIFT:]], axis=1).astype(BF16)
    mu = jnp.concatenate([mu_shift[l], jnp.zeros((pad_a,), F32)])[None]
    lora = jnp.zeros((LANES, 2 * D_A), F32)
    lora = lora.at[:R_LORA, :D_A].set(w_decay_up[l]).at[R_LORA:2 * R_LORA, D_A:].set(w_iclr_up[l])
    seg = jnp.arange(D_A) // HEAD
    row1 = lambda x: x.reshape(1, -1)
    gsz = D_B // N_GROUPS_B
    w_route = jnp.concatenate(
        [w_route_expert[l], w_route_group[l],
         jnp.zeros((d, LANES - N_EXPERTS - N_ROUTE_GROUPS), F32)], axis=1)
    b_route = jnp.concatenate(
        [b_route_expert[l], b_route_group[l],
         jnp.zeros((LANES - N_EXPERTS - N_ROUTE_GROUPS,), F32)])[None]
    return dict(
        w_in=w_in_p, mu=mu, lora=lora,
        w0a0=jnp.concatenate([w0[l], a0[l]])[None],
        k_k=row1(k_k[l]), k_a=row1(k_a[l]), r_k=row1(r_k[l]),
        lnx_g=row1(lnx_g[l]), lnx_b=row1(lnx_b[l]),
        lnv_g=row1(lnv_g[l]), lnv_b=row1(lnv_b[l]),
        bd=(seg[:, None] == seg[None, :]).astype(BF16),
        w_spatial=w_spatial[l],
        b_spatial_full=jnp.repeat(b_spatial[l].T, gsz, axis=1),
        ws_row=jnp.repeat(w_spatial[l][:, 0, 0], gsz)[None],
        bs_row=jnp.repeat(b_spatial[l][:, 0], gsz)[None],
        w_branch_a=w_branch_a[l].astype(BF16), w_branch_b=w_branch_b[l].astype(BF16),
        w_out=w_out[l].astype(BF16), ln1_g=row1(ln1_g[l]), ln1_b=row1(ln1_b[l]),
        w_route=w_route, b_route=b_route,
        w_exp_gate=w_exp_gate, w_exp_up=w_exp_up, w_exp_down=w_exp_down, layer=l,
        ln2_g=row1(ln2_g[l]), ln2_b=row1(ln2_b[l]),
    )


def _trunk(seq_mode, x, mods, wkv_in, shift_in, preps):
    b, t, d = x.shape
    wkv_out, shift_out, v_out = [], [], []
    wkv_acc = None if seq_mode else jnp.zeros(wkv_in.shape, F32)
    for l in range(DEPTH):
        p = preps[l]
        sh1, sc1, g1, sh2, sc2, g2 = mods[l]
        if seq_mode:
            zprev = jnp.zeros((b, 1, N_SHIFT_PAD), F32) if shift_in is None else shift_in[l]
        else:
            zprev = _matmul(shift_in[l], p["w_in"][:, :N_SHIFT_PAD]).reshape(b, t, N_SHIFT_PAD)
        r, lw, kp, v, kap, bb, u, vg, ga, gb, hl = _in_stage(seq_mode, x, sc1, sh1, zprev, p)
        if seq_mode:
            o, s_new = _wkv_seq(r, lw, kp, v, kap, bb, wkv_in[l])
        else:
            flat = lambda a: a.reshape(t, D_A)
            o, wkv_acc = _wkv_step(flat(r), flat(lw), flat(kp), flat(v), flat(kap), flat(bb),
                                   wkv_in[l], wkv_acc, l)
            o = o.reshape(b, t, D_A)
        x1, h2 = _merge_stage(seq_mode, x, o, r, kp, v, u, vg, ga, gb, g1, sc2, sh2, p)
        m = b * t
        h2f = h2.reshape(m, d)
        gate, idx, wts, tile_cnt = _route(h2f, p)
        if seq_mode:
            x = _moe_routed(t, h2f, idx, wts, tile_cnt, x1.reshape(m, d), g2, p)
        else:
            x = _moe(t, h2f, gate, x1.reshape(m, d), g2, p)
        x = x.reshape(b, t, d)
        if seq_mode:
            wkv_out.append(s_new)
        shift_out.append(hl)
        v_out.append(vg)
    if not seq_mode:
        wkv_out = wkv_acc
    return x, wkv_out, shift_out, v_out


def kernel(x_prompt, x_sample, c_prompt, c_sample, state_wkv, state_shift, w_ada, b_ada, w_in, mu_shift, w0, w_decay_up, a0, w_iclr_up, k_k, k_a, r_k, lnx_g, lnx_b, lnv_g, lnv_b, w_spatial, b_spatial, w_branch_a, w_branch_b, w_out, ln1_g, ln1_b, w_route_group, b_route_group, w_route_expert, b_route_expert, w_exp_gate, w_exp_up, w_exp_down, ln2_g, ln2_b):
    bp, tp, d = x_prompt.shape
    bs = x_sample.shape[0]
    layer_params = (w_in, mu_shift, w0, w_decay_up, a0, w_iclr_up, k_k, k_a, r_k, lnx_g,
                    lnx_b, lnv_g, lnv_b, w_spatial, b_spatial, w_branch_a, w_branch_b, w_out,
                    ln1_g, ln1_b, w_route_group, b_route_group, w_route_expert,
                    b_route_expert, w_exp_gate, w_exp_up, w_exp_down, ln2_g, ln2_b)
    preps = [_prep_layer(l, *layer_params) for l in range(DEPTH)]
    mod_all = _ada(jnp.concatenate([c_prompt, c_sample], axis=0), w_ada, b_ada)
    mods_p, mods_s = [], []
    for l in range(DEPTH):
        parts = jnp.split(mod_all[l], 6, axis=-1)
        mods_p.append([q[:bp].reshape(bp, 1, d) for q in parts])
        mods_s.append([q[bp:].reshape(1, bs, d) for q in parts])

    wkv0 = jnp.zeros((DEPTH, bp, N_HEADS, HEAD, HEAD), F32)
    y_p, wkv_p, shift_p, _ = _trunk(True, x_prompt, mods_p, wkv0, None, preps)
    y_s, wkv_s, shift_s, v_s = _trunk(False, x_sample.reshape(1, bs, d), mods_s, state_wkv,
                                      state_shift, preps)
    return (y_p,
            y_s.reshape(bs, 1, d),
            jnp.stack(wkv_p),
            jnp.stack([s.reshape(bp, d) for s in shift_p]),
            wkv_s,
            jnp.stack([s.reshape(bs, d) for s in shift_s]),
            jnp.stack([q.reshape(bs, 1, D_B) for q in v_s]))
```
